```python
import math
import jax, jax.numpy as jnp
from jax import lax
import numpy as np

D_MODEL = 1024
BATCH = 2
SEQ = 8192
DEPTH = 2

N_META = 16
BLOCK_Q = 128
PAD = BLOCK_Q - N_META
HG_WIDTH = D_MODEL // 2
HG_HEADS = 4
HG_DK = HG_WIDTH // HG_HEADS
HG_DV = HG_WIDTH // HG_HEADS
HG_CHUNK = 64
DA_WIDTH = D_MODEL // 4
DA_HEADS = 4
DA_DV = DA_WIDTH // DA_HEADS
DA_DQK = DA_DV // 2
FX_WIDTH = D_MODEL // 4
FX_HEADS = 4
FX_DH = FX_WIDTH // FX_HEADS
MIX_WIDTH = HG_WIDTH + DA_WIDTH + FX_WIDTH
IN_SIZES = (HG_WIDTH, HG_WIDTH, HG_WIDTH, HG_WIDTH,
            DA_WIDTH, DA_WIDTH, DA_WIDTH,
            FX_WIDTH, FX_WIDTH, FX_WIDTH, FX_HEADS)
IN_WIDTH = sum(IN_SIZES)
N_GROUPS = 4
EXPERTS_PER_GROUP = 8
N_EXPERTS = N_GROUPS * EXPERTS_PER_GROUP
TOP_K = 2
D_EXPERT = D_MODEL // 2
MOE_BLOCK = 128
ROPE_THETA = 10000.0
EPS = 1e-6
NEG = -1e30
TINY = 1e-30

kernel_name = "hymba_hgrn2_diff_fox_hmoe"


def _rmsnorm(x, g):
    x32 = x.astype(jnp.float32)
    y = x32 * lax.rsqrt(jnp.mean(x32 * x32, axis=-1, keepdims=True) + EPS)
    return (y * g.astype(jnp.float32)).astype(x.dtype)


def _rope(x, pos):
    d = x.shape[-1]
    inv = ROPE_THETA ** (-jnp.arange(0, d, 2, dtype=jnp.float32) / d)
    ang = pos.astype(jnp.float32)[:, None] * inv[None, :]
    cos, sin = jnp.cos(ang), jnp.sin(ang)
    x32 = x.astype(jnp.float32)
    x1, x2 = x32[..., : d // 2], x32[..., d // 2:]
    return jnp.concatenate([x1 * cos - x2 * sin, x2 * cos + x1 * sin], axis=-1).astype(x.dtype)


def _to_heads(t, n):
    b, l, _ = t.shape
    return t.reshape(b, l, n, -1).transpose(0, 2, 1, 3)


def _hgrn2(q, fz, i, g, lb, gain, valid):
    b, l, _ = q.shape
    f32 = jnp.float32
    z = fz.astype(f32)
    lb = lb.astype(f32)
    sig = jax.nn.sigmoid(z)
    f = lb + (1.0 - lb) * sig
    log_f = jnp.log(jnp.maximum(f, TINY))
    k = (1.0 - lb) * (1.0 - sig)
    m = valid[None, :, None]
    log_f = jnp.where(m, log_f, 0.0)
    k = jnp.where(m, k, 0.0)
    qs = jax.nn.silu(q.astype(f32)) * (HG_DK ** -0.5)
    n_chunks = l // HG_CHUNK

    def chunks(t):
        return _to_heads(t, HG_HEADS).reshape(b, HG_HEADS, n_chunks, HG_CHUNK, -1).transpose(2, 0, 1, 3, 4)

    qc, kc, vc, gc = chunks(qs), chunks(k), chunks(i.astype(f32)), chunks(log_f)
    causal = jnp.tril(jnp.ones((HG_CHUNK, HG_CHUNK), dtype=bool))[:, :, None]

    def step(S, inp):
        qt, kt, vt, gt = inp
        G = jnp.cumsum(gt, axis=-2)
        o_inter = jnp.einsum('bhtk,bhkv->bhtv', qt * jnp.exp(G), S)
        rel = jnp.where(causal, G[:, :, :, None, :] - G[:, :, None, :, :], 0.0)
        decay = jnp.where(causal, jnp.exp(rel), 0.0)
        A = jnp.einsum('bhtk,bhsk,bhtsk->bhts', qt, kt, decay)
        o = o_inter + jnp.einsum('bhts,bhsv->bhtv', A, vt)
        G_last = G[:, :, -1:, :]
        S = jnp.exp(G_last[:, :, 0, :, None]) * S + jnp.einsum('bhsk,bhsv->bhkv', kt * jnp.exp(G_last - G), vt)
        return S, o

    S0 = jnp.zeros((b, HG_HEADS, HG_DK, HG_DV), f32)
    _, o = lax.scan(step, S0, (qc, kc, vc, gc))
    o = o.transpose(1, 0, 3, 2, 4).reshape(b, l, HG_HEADS, HG_DV).astype(q.dtype)
    o = _rmsnorm(o, gain) * jax.nn.silu(g).reshape(b, l, HG_HEADS, HG_DV)
    return o.reshape(b, l, HG_WIDTH)


def _diff_attn(q, k, v, lam_vecs, gain, lam_init, pos, valid):
    b, l, _ = q.shape
    nb = l // BLOCK_Q
    q = _rope(_to_heads(q, 2 * DA_HEADS), pos).reshape(b, DA_HEADS, 2, l, DA_DQK)
    k = _rope(_to_heads(k, 2 * DA_HEADS), pos).reshape(b, DA_HEADS, 2, l, DA_DQK)
    v = _to_heads(v, DA_HEADS)
    lv = lam_vecs.astype(jnp.float32)
    lam = jnp.exp(jnp.sum(lv[0] * lv[1])) - jnp.exp(jnp.sum(lv[2] * lv[3])) + lam_init
    qb = q.reshape(b, DA_HEADS, 2, nb, BLOCK_Q, DA_DQK).transpose(3, 0, 1, 2, 4, 5)
    key_idx = jnp.arange(l)
    scale = DA_DQK ** -0.5

    def block(args):
        qi, bi = args
        s = jnp.einsum('bhcqd,bhckd->bhcqk', qi, k).astype(jnp.float32) * scale
        q_idx = bi * BLOCK_Q + jnp.arange(BLOCK_Q)
        mask = (key_idx[None, :] <= q_idx[:, None]) & valid[None, :]
        p = jax.nn.softmax(jnp.where(mask, s, NEG), axis=-1)
        w = (p[:, :, 0] - lam * p[:, :, 1]).astype(v.dtype)
        return jnp.einsum('bhqk,bhkv->bhqv', w, v)

    o = lax.map(block, (qb, jnp.arange(nb)))
    o = o.transpose(1, 0, 3, 2, 4).reshape(b, l, DA_HEADS, DA_DV)
    o = _rmsnorm(o, gain) * (1.0 - lam_init)
    return o.reshape(b, l, DA_WIDTH)


def _fox_attn(q, k, v, fz, bias, gain, valid):
    b, l, _ = q.shape
    nb = l // BLOCK_Q
    q, k, v = _to_heads(q, FX_HEADS), _to_heads(k, FX_HEADS), _to_heads(v, FX_HEADS)
    log_f = jax.nn.log_sigmoid(fz.astype(jnp.float32) + bias.astype(jnp.float32))
    D = jnp.cumsum(log_f, axis=1).transpose(0, 2, 1)
    qb = q.reshape(b, FX_HEADS, nb, BLOCK_Q, FX_DH).transpose(2, 0, 1, 3, 4)
    db = D.reshape(b, FX_HEADS, nb, BLOCK_Q).transpose(2, 0, 1, 3)
    key_idx = jnp.arange(l)
    scale = FX_DH ** -0.5

    def block(args):
        qi, di, bi = args
        s = jnp.einsum('bhqd,bhkd->bhqk', qi, k).astype(jnp.float32) * scale
        s = s + di[..., :, None] - D[:, :, None, :]
        q_idx = bi * BLOCK_Q + jnp.arange(BLOCK_Q)
        mask = (key_idx[None, :] <= q_idx[:, None]) & valid[None, :]
        p = jax.nn.softmax(jnp.where(mask, s, NEG), axis=-1).astype(v.dtype)
        return jnp.einsum('bhqk,bhkv->bhqv', p, v)

    o = lax.map(block, (qb, db, jnp.arange(nb)))
    o = o.transpose(1, 0, 3, 2, 4).reshape(b, l, FX_HEADS, FX_DH)
    return _rmsnorm(o, gain).reshape(b, l, FX_WIDTH)


def _hier_moe(u, w_group, b_group, w_router, b_router, w1, w3, w2):
    b, l, d = u.shape
    n = b * l
    h = u.reshape(n, d)
    g_logits = (h @ w_group).astype(jnp.float32) + b_group.astype(jnp.float32)
    p_group = jax.nn.softmax(g_logits, axis=-1)
    g_sel = jnp.argmax(g_logits, axis=-1)
    p_gsel = jnp.take_along_axis(p_group, g_sel[:, None], axis=1)
    e_logits = ((h @ w_router).astype(jnp.float32) + b_router.astype(jnp.float32)).reshape(n, N_GROUPS, EXPERTS_PER_GROUP)
    e_in = jnp.take_along_axis(e_logits, g_sel[:, None, None], axis=1)[:, 0]
    top_p, top_i = lax.top_k(jax.nn.softmax(e_in, axis=-1), TOP_K)
    gate = p_gsel * top_p / jnp.sum(top_p, axis=-1, keepdims=True)
    expert = g_sel[:, None] * EXPERTS_PER_GROUP + top_i

    a = n * TOP_K
    e_flat = expert.reshape(a).astype(jnp.int32)
    t_flat = jnp.repeat(jnp.arange(n, dtype=jnp.int32), TOP_K)
    w_flat = gate.reshape(a)
    order = jnp.argsort(e_flat)
    e_s, t_s, w_s = e_flat[order], t_flat[order], w_flat[order]
    counts = jnp.bincount(e_flat, length=N_EXPERTS)
    padded = (counts + MOE_BLOCK - 1) // MOE_BLOCK * MOE_BLOCK
    start = jnp.cumsum(counts) - counts
    p_end = jnp.cumsum(padded)
    p_start = p_end - padded
    dest = p_start[e_s] + jnp.arange(a, dtype=jnp.int32) - start[e_s]
    p_rows = ((a + MOE_BLOCK - 1) // MOE_BLOCK + N_EXPERTS) * MOE_BLOCK
    tok_buf = jnp.full((p_rows,), n, dtype=jnp.int32).at[dest].set(t_s)
    w_buf = jnp.zeros((p_rows,), jnp.float32).at[dest].set(w_s)
    n_blocks = p_rows // MOE_BLOCK
    blk_expert = jnp.minimum(jnp.searchsorted(p_end, jnp.arange(n_blocks, dtype=jnp.int32) * MOE_BLOCK, side='right'), N_EXPERTS - 1)
    x_pad = jnp.concatenate([h, jnp.zeros((1, d), h.dtype)], axis=0)

    def run(args):
        tok, e = args
        xb = x_pad[tok]
        return (jax.nn.silu(xb @ w1[e]) * (xb @ w3[e])) @ w2[e]

    y = lax.map(run, (tok_buf.reshape(n_blocks, MOE_BLOCK), blk_expert))
    out = jnp.zeros((n + 1, d), h.dtype).at[tok_buf].add(y.reshape(p_rows, d) * w_buf[:, None].astype(h.dtype))
    return out[:n].reshape(b, l, d)


def setup_inputs(seed: int = 0) -> dict:
    key = jax.random.key(seed)
    ks = jax.random.split(key, 20)
    nrm = jax.random.normal
    f32 = jnp.float32
    return {
        "x": nrm(ks[0], (BATCH, SEQ, D_MODEL), f32),
        "meta_tokens": nrm(ks[1], (N_META, D_MODEL), f32),
        "norm_mix": 1.0 + 0.02 * nrm(ks[2], (DEPTH, D_MODEL), f32),
        "w_in": nrm(ks[3], (DEPTH, D_MODEL, IN_WIDTH), f32) * D_MODEL ** -0.5,
        "hgrn_lb": 0.5 * nrm(ks[4], (DEPTH, HG_WIDTH), f32),
        "hgrn_norm": 1.0 + 0.02 * nrm(ks[5], (DEPTH, HG_DV), f32),
        "diff_lambda": 0.1 * nrm(ks[6], (DEPTH, 4, DA_DQK), f32),
        "diff_norm": 1.0 + 0.02 * nrm(ks[7], (DEPTH, DA_DV), f32),
        "fox_bias": 3.0 + 0.1 * nrm(ks[8], (DEPTH, FX_HEADS), f32),
        "fox_norm": 1.0 + 0.02 * nrm(ks[9], (DEPTH, FX_DH), f32),
        "w_out": nrm(ks[10], (DEPTH, MIX_WIDTH, D_MODEL), f32) * MIX_WIDTH ** -0.5,
        "norm_ffn": 1.0 + 0.02 * nrm(ks[11], (DEPTH, D_MODEL), f32),
        "w_group": nrm(ks[12], (DEPTH, D_MODEL, N_GROUPS), f32) * D_MODEL ** -0.5,
        "b_group": 0.01 * nrm(ks[13], (DEPTH, N_GROUPS), f32),
        "w_router": nrm(ks[14], (DEPTH, D_MODEL, N_EXPERTS), f32) * D_MODEL ** -0.5,
        "b_router": 0.01 * nrm(ks[15], (DEPTH, N_EXPERTS), f32),
        "w1": nrm(ks[16], (DEPTH, N_EXPERTS, D_MODEL, D_EXPERT), f32) * D_MODEL ** -0.5,
        "w3": nrm(ks[17], (DEPTH, N_EXPERTS, D_MODEL, D_EXPERT), f32) * D_MODEL ** -0.5,
        "w2": nrm(ks[18], (DEPTH, N_EXPERTS, D_EXPERT, D_MODEL), f32) * D_EXPERT ** -0.5,
        "norm_final": 1.0 + 0.02 * nrm(ks[19], (D_MODEL,), f32),
    }


def reference(x, meta_tokens, norm_mix, w_in, hgrn_lb, hgrn_norm, diff_lambda, diff_norm,
              fox_bias, fox_norm, w_out, norm_ffn, w_group, b_group, w_router, b_router,
              w1, w3, w2, norm_final):
    b = x.shape[0]
    pad = jnp.zeros((b, PAD, D_MODEL), x.dtype)
    meta = jnp.broadcast_to(meta_tokens.astype(x.dtype)[None], (b, N_META, D_MODEL))
    h = jnp.concatenate([pad, meta, x], axis=1)
    l = h.shape[1]
    idx = jnp.arange(l)
    pos = idx - PAD
    valid = idx >= PAD
    s_lb = jax.nn.softmax(hgrn_lb.astype(jnp.float32), axis=0)
    lb_all = jnp.cumsum(s_lb, axis=0) - s_lb[0]
    split_at = np.cumsum(IN_SIZES)[:-1].tolist()
    for layer in range(DEPTH):
        u = _rmsnorm(h, norm_mix[layer])
        proj = u @ w_in[layer]
        hq, hf, hi, hg, dq, dk, dv, fq, fk, fv, ff = jnp.split(proj, split_at, axis=-1)
        lam_init = 0.8 - 0.6 * math.exp(-0.3 * layer)
        o_a = _hgrn2(hq, hf, hi, hg, lb_all[layer], hgrn_norm[layer], valid)
        o_b = _diff_attn(dq, dk, dv, diff_lambda[layer], diff_norm[layer], lam_init, pos, valid)
        o_c = _fox_attn(fq, fk, fv, ff, fox_bias[layer], fox_norm[layer], valid)
        h = h + jnp.concatenate([o_a, o_b, o_c], axis=-1) @ w_out[layer]
        u = _rmsnorm(h, norm_ffn[layer])
        h = h + _hier_moe(u, w_group[layer], b_group[layer], w_router[layer], b_router[layer],
                          w1[layer], w3[layer], w2[layer])
    h = _rmsnorm(h, norm_final)
    return h[:, PAD + N_META:]
```

```python
import functools
import math

import jax
import jax.numpy as jnp
from jax import lax
from jax.experimental import pallas as pl
from jax.experimental.pallas import tpu as pltpu

F32 = jnp.float32
BF16 = jnp.bfloat16

D_MODEL = 1024
N_META = 16
HG_HEADS = 4
HG_DK = 128
HG_WIDTH = 512
HG_CHUNK = 64
HG_SUB = 16
DA_HEADS = 4
DA_DV = 64
DA_DQK = 32
DA_WIDTH = 256
FX_HEADS = 4
FX_DH = 64
FX_WIDTH = 256
N_GROUPS = 4
EXPERTS_PER_GROUP = 8
N_EXPERTS = 32
D_EXPERT = 512
ROPE_THETA = 10000.0
EPS = 1e-6
NEG = -1e30
TINY = 1e-30

ROW_TILE = 256
ATTN_BLOCK = 256
MOE_BLOCK = 256
LPAD = ROW_TILE - N_META
LANES = 128

_C_HQIG = (0, 1536)
_C_HF = (1536, 2048)
_C_DQK = (2048, 2560)
_C_DQKP = (2560, 3072)
_C_DV = (3072, 3328)
_C_FOX = (3328, 4096)
_C_FF = (4096, 4224)
IN_COLS = 4224

_NT = (((1,), (1,)), ((), ()))
_TN = (((0,), (0,)), ((), ()))


def _sigmoid(x):
    return 1.0 / (1.0 + jnp.exp(-x))


def _rms(x, g):
    return x * lax.rsqrt(jnp.mean(x * x, axis=-1, keepdims=True) + EPS) * g


def _in_proj_kernel(h_ref, g_ref, w_ref, cos_ref, sin_ref,
                    hqig_ref, hf_ref, dqk_ref, dv_ref, fox_ref, ff_ref):
    u = _rms(h_ref[...], g_ref[...]).astype(BF16)

    def mm(c):
        return jnp.dot(u, w_ref[:, c[0]:c[1]], preferred_element_type=F32)

    hqig_ref[...] = mm(_C_HQIG).astype(BF16)
    hf_ref[...] = mm(_C_HF)
    dqk_ref[...] = (mm(_C_DQK) * cos_ref[...] + mm(_C_DQKP) * sin_ref[...]).astype(BF16)
    dv_ref[...] = mm(_C_DV).astype(BF16)
    fox_ref[...] = mm(_C_FOX).astype(BF16)
    ff_ref[...] = mm(_C_FF)


def _in_proj(h, g, w_cat, cos_t, sin_t, seq_len):
    n = h.shape[0]
    tm = ROW_TILE
    n_seq_tiles = seq_len // tm
    row = lambda w: pl.BlockSpec((tm, w), lambda i: (i, 0))
    tab = pl.BlockSpec((tm, 512), lambda i: (i % n_seq_tiles, 0))
    return pl.pallas_call(
        _in_proj_kernel,
        grid=(n // tm,),
        in_specs=[row(D_MODEL),
                  pl.BlockSpec((1, D_MODEL), lambda i: (0, 0)),
                  pl.BlockSpec((D_MODEL, IN_COLS), lambda i: (0, 0)),
                  tab, tab],
        out_specs=[row(1536), row(512), row(512), row(256), row(768), row(128)],
        out_shape=[jax.ShapeDtypeStruct((n, 1536), BF16),
                   jax.ShapeDtypeStruct((n, 512), F32),
                   jax.ShapeDtypeStruct((n, 512), BF16),
                   jax.ShapeDtypeStruct((n, 256), BF16),
                   jax.ShapeDtypeStruct((n, 768), BF16),
                   jax.ShapeDtypeStruct((n, 128), F32)],
        compiler_params=pltpu.CompilerParams(dimension_semantics=("arbitrary",),
                                             vmem_limit_bytes=56 * 1024 * 1024),
        name="in_proj",
    )(h, g, w_cat, cos_t, sin_t)


def _fox_decay_kernel(ff_ref, b_ref, d_ref, carry_ref):
    @pl.when(pl.program_id(1) == 0)
    def _():
        carry_ref[...] = jnp.zeros_like(carry_ref)

    x = ff_ref[0] + b_ref[...]
    lf = jnp.minimum(x, 0.0) - jnp.log(1.0 + jnp.exp(-jnp.abs(x)))
    t = x.shape[0]
    tri = (lax.broadcasted_iota(jnp.int32, (t, t), 1)
           <= lax.broadcasted_iota(jnp.int32, (t, t), 0)).astype(F32)
    cs = jnp.dot(tri, lf, precision=lax.Precision.HIGHEST,
                 preferred_element_type=F32) + carry_ref[...]
    d_ref[0] = cs
    carry_ref[...] = cs[t - 1:t, :]


def _fox_decay(ff, bias_row):
    b, l, _ = ff.shape
    t = ROW_TILE
    return pl.pallas_call(
        _fox_decay_kernel,
        grid=(b, l // t),
        in_specs=[pl.BlockSpec((1, t, LANES), lambda i, j: (i, j, 0)),
                  pl.BlockSpec((1, LANES), lambda i, j: (0, 0))],
        out_specs=pl.BlockSpec((1, t, LANES), lambda i, j: (i, j, 0)),
        out_shape=jax.ShapeDtypeStruct((b, l, LANES), F32),
        scratch_shapes=[pltpu.VMEM((1, LANES), F32)],
        compiler_params=pltpu.CompilerParams(dimension_semantics=("arbitrary", "arbitrary")),
        name="fox_decay",
    )(ff, bias_row)


def _fox_attn_kernel(q_ref, k_ref, v_ref, dq_ref, dk_ref, g_ref, o_ref,
                     m_ref, l_ref, acc_ref, *, blk):
    i = pl.program_id(1)
    q_idx = i * blk + lax.broadcasted_iota(jnp.int32, (blk, blk), 0)
    k_iota = lax.broadcasted_iota(jnp.int32, (blk, blk), 1)
    for h in range(FX_HEADS):
        sl = slice(h * FX_DH, (h + 1) * FX_DH)
        q = q_ref[:, sl]
        dq = dq_ref[0, :, h:h + 1]
        m_ref[...] = jnp.full(m_ref.shape, NEG, F32)
        l_ref[...] = jnp.zeros_like(l_ref)
        acc_ref[...] = jnp.zeros_like(acc_ref)

        def body(j, carry):
            k0 = pl.multiple_of(j * blk, blk)
            k = k_ref[pl.ds(k0, blk), sl]
            v = v_ref[pl.ds(k0, blk), sl]
            dk = dk_ref[0, j, h:h + 1, :]
            s = lax.dot_general(q, k, _NT, preferred_element_type=F32)
            s = s + dq - dk
            k_idx = k0 + k_iota
            s = jnp.where((k_idx <= q_idx) & (k_idx >= LPAD), s, NEG)
            m_prev = m_ref[...]
            m_new = jnp.maximum(m_prev, jnp.max(s, axis=-1, keepdims=True))
            alpha = jnp.exp(m_prev - m_new)
            p = jnp.exp(s - m_new)
            l_ref[...] = alpha * l_ref[...] + jnp.sum(p, axis=-1, keepdims=True)
            acc_ref[...] = alpha * acc_ref[...] + jnp.dot(
                p.astype(BF16), v, preferred_element_type=F32)
            m_ref[...] = m_new
            return carry

        lax.fori_loop(0, i + 1, body, 0)
        o = acc_ref[...] / l_ref[...]
        o_ref[:, sl] = _rms(o, g_ref[...]).astype(BF16)


def _fox_attn(fox, d_col, d_row, gain, batch, seq_len):
    n = fox.shape[0]
    blk = ATTN_BLOCK
    nq = seq_len // blk
    return pl.pallas_call(
        functools.partial(_fox_attn_kernel, blk=blk),
        grid=(batch, nq),
        in_specs=[pl.BlockSpec((blk, FX_WIDTH), lambda b, i: (b * nq + i, 0)),
                  pl.BlockSpec((seq_len, FX_WIDTH), lambda b, i: (b, 1)),
                  pl.BlockSpec((seq_len, FX_WIDTH), lambda b, i: (b, 2)),
                  pl.BlockSpec((1, blk, LANES), lambda b, i: (b, i, 0)),
                  pl.BlockSpec((1, nq, 8, blk), lambda b, i: (b, 0, 0, 0)),
                  pl.BlockSpec((1, FX_DH), lambda b, i: (0, 0))],
        out_specs=pl.BlockSpec((blk, FX_WIDTH), lambda b, i: (b * nq + i, 0)),
        out_shape=jax.ShapeDtypeStruct((n, FX_WIDTH), BF16),
        scratch_shapes=[pltpu.VMEM((blk, 1), F32), pltpu.VMEM((blk, 1), F32),
                        pltpu.VMEM((blk, FX_DH), F32)],
        compiler_params=pltpu.CompilerParams(dimension_semantics=("arbitrary", "arbitrary"),
                                             vmem_limit_bytes=56 * 1024 * 1024),
        name="fox_attn",
    )(fox, fox, fox, d_col, d_row, gain)


def _diff_attn_kernel(q_ref, k_ref, v_ref, lam_ref, g_ref, o_ref,
                      m_ref, l_ref, acc_ref, *, blk, lam_init):
    i = pl.program_id(1)
    q_idx = i * blk + lax.broadcasted_iota(jnp.int32, (blk, blk), 0)
    k_iota = lax.broadcasted_iota(jnp.int32, (blk, blk), 1)
    lv = lam_ref[...]
    lam = (jnp.exp(jnp.sum(lv[0:1] * lv[1:2], axis=-1, keepdims=True))
           - jnp.exp(jnp.sum(lv[2:3] * lv[3:4], axis=-1, keepdims=True)) + lam_init)
    for h in range(DA_HEADS):
        sl = slice(h * DA_DV, (h + 1) * DA_DV)
        m_ref[...] = jnp.full(m_ref.shape, NEG, F32)
        l_ref[...] = jnp.zeros_like(l_ref)
        acc_ref[...] = jnp.zeros_like(acc_ref)

        def body(j, carry):
            k0 = pl.multiple_of(j * blk, blk)
            v = v_ref[pl.ds(k0, blk), sl]
            k_idx = k0 + k_iota
            mask = (k_idx <= q_idx) & (k_idx >= LPAD)
            for c in range(2):
                cs = slice(h * DA_DV + c * DA_DQK, h * DA_DV + (c + 1) * DA_DQK)
                s = lax.dot_general(q_ref[:, cs], k_ref[pl.ds(k0, blk), cs], _NT,
                                    preferred_element_type=F32)
                s = jnp.where(mask, s, NEG)
                m_prev = m_ref[c]
                m_new = jnp.maximum(m_prev, jnp.max(s, axis=-1, keepdims=True))
                alpha = jnp.exp(m_prev - m_new)
                p = jnp.exp(s - m_new)
                l_ref[c] = alpha * l_ref[c] + jnp.sum(p, axis=-1, keepdims=True)
                acc_ref[c] = alpha * acc_ref[c] + jnp.dot(
                    p.astype(BF16), v, preferred_element_type=F32)
                m_ref[c] = m_new
            return carry

        lax.fori_loop(0, i + 1, body, 0)
        o = acc_ref[0] / l_ref[0] - lam * (acc_ref[1] / l_ref[1])
        o_ref[:, sl] = (_rms(o, g_ref[...]) * (1.0 - lam_init)).astype(BF16)


def _diff_attn(dqk, dv, lam_vecs, gain, lam_init, batch, seq_len):
    n = dqk.shape[0]
    blk = ATTN_BLOCK
    nq = seq_len // blk
    return pl.pallas_call(
        functools.partial(_diff_attn_kernel, blk=blk, lam_init=lam_init),
        grid=(batch, nq),
        in_specs=[pl.BlockSpec((blk, DA_WIDTH), lambda b, i: (b * nq + i, 0)),
                  pl.BlockSpec((seq_len, DA_WIDTH), lambda b, i: (b, 1)),
                  pl.BlockSpec((seq_len, DA_WIDTH), lambda b, i: (b, 0)),
                  pl.BlockSpec((8, LANES), lambda b, i: (0, 0)),
                  pl.BlockSpec((1, DA_DV), lambda b, i: (0, 0))],
        out_specs=pl.BlockSpec((blk, DA_WIDTH), lambda b, i: (b * nq + i, 0)),
        out_shape=jax.ShapeDtypeStruct((n, DA_WIDTH), BF16),
        scratch_shapes=[pltpu.VMEM((2, blk, 1), F32), pltpu.VMEM((2, blk, 1), F32),
                        pltpu.VMEM((2, blk, DA_DV), F32)],
        compiler_params=pltpu.CompilerParams(dimension_semantics=("arbitrary", "arbitrary"),
                                             vmem_limit_bytes=56 * 1024 * 1024),
        name="diff_attn",
    )(dqk, dqk, dv, lam_vecs, gain)


def _hgrn_kernel(q_ref, i_ref, g_ref, f_ref, lb_ref, gain_ref, o_ref,
                 st_ref, gs_ref, ks_ref, *, tl):
    j = pl.program_id(2)

    @pl.when(j == 0)
    def _():
        st_ref[...] = jnp.zeros_like(st_ref)

    c_rows, sub = HG_CHUNK, HG_SUB
    lb = lb_ref[...]
    tri = (lax.broadcasted_iota(jnp.int32, (c_rows, c_rows), 1)
           <= lax.broadcasted_iota(jnp.int32, (c_rows, c_rows), 0)).astype(F32)
    row_c = lax.broadcasted_iota(jnp.int32, (c_rows, 1), 0)
    t_sub = lax.broadcasted_iota(jnp.int32, (sub, 1), 0)
    lane_c = lax.broadcasted_iota(jnp.int32, (sub, c_rows), 1)

    def chunk(c, carry):
        r0 = pl.multiple_of(c * c_rows, c_rows)
        z = f_ref[pl.ds(r0, c_rows), :]
        sig = _sigmoid(z)
        log_f = jnp.log(jnp.maximum(lb + (1.0 - lb) * sig, TINY))
        kk = (1.0 - lb) * (1.0 - sig)
        valid = (j * tl + r0 + row_c) >= LPAD
        log_f = jnp.where(valid, log_f, 0.0)
        kk = jnp.where(valid, kk, 0.0)
        G = jnp.dot(tri, log_f, precision=lax.Precision.HIGHEST, preferred_element_type=F32)
        qf = q_ref[pl.ds(r0, c_rows), :].astype(F32)
        qs = qf * _sigmoid(qf) * (HG_DK ** -0.5)
        v = i_ref[pl.ds(r0, c_rows), :]
        gs_ref[...] = G
        ks_ref[...] = kk
        st = st_ref[...]
        o_inter = lax.dot_general((qs * jnp.exp(G)).astype(BF16), st.astype(BF16), _NT,
                                  preferred_element_type=F32)
        outs = []
        for b in range(c_rows // sub):
            lo = b * sub
            q_b = qs[lo:lo + sub]
            g_b = G[lo:lo + sub]
            if b == 0:
                a_blk = jnp.zeros((sub, c_rows), F32)
            else:
                ref = G[lo - 1:lo]
                qa = q_b * jnp.exp(g_b - ref)
                kb = jnp.where(row_c < lo, kk * jnp.exp(jnp.minimum(ref - G, 0.0)), 0.0)
                a_blk = lax.dot_general(qa.astype(BF16), kb.astype(BF16), _NT,
                                        preferred_element_type=F32)

            def diag(s, a_acc, q_b=q_b, g_b=g_b, lo=lo):
                g_s = gs_ref[pl.ds(s, 1), :]
                k_s = ks_ref[pl.ds(s, 1), :]
                y = q_b * k_s * jnp.exp(jnp.minimum(g_b - g_s, 0.0))
                a = jnp.sum(y, axis=-1, keepdims=True)
                a = jnp.where(t_sub + lo >= s, a, 0.0)
                return jnp.where(lane_c == s, a, a_acc)

            a_blk = lax.fori_loop(lo, lo + sub, diag, a_blk)
            outs.append(o_inter[lo:lo + sub]
                        + jnp.dot(a_blk.astype(BF16), v, preferred_element_type=F32))
        o = jnp.concatenate(outs, axis=0)
        g_last = G[c_rows - 1:c_rows]
        kd = kk * jnp.exp(g_last - G)
        st_ref[...] = st * jnp.exp(g_last) + lax.dot_general(
            v, kd.astype(BF16), _TN, preferred_element_type=F32)
        gate = g_ref[pl.ds(r0, c_rows), :].astype(F32)
        o = _rms(o, gain_ref[...]) * (gate * _sigmoid(gate))
        o_ref[pl.ds(r0, c_rows), :] = o.astype(BF16)
        return carry

    lax.fori_loop(0, tl // c_rows, chunk, 0)


def _hgrn(hqig, hf, lb_row, gain, batch, seq_len):
    n = hqig.shape[0]
    tl = ROW_TILE
    nt = seq_len // tl
    col = lambda off: pl.BlockSpec((tl, HG_DK), lambda b, h, j: (b * nt + j, off + h))
    return pl.pallas_call(
        functools.partial(_hgrn_kernel, tl=tl),
        grid=(batch, HG_HEADS, nt),
        in_specs=[col(0), col(HG_HEADS), col(2 * HG_HEADS), col(0),
                  pl.BlockSpec((1, HG_DK), lambda b, h, j: (0, h)),
                  pl.BlockSpec((1, HG_DK), lambda b, h, j: (0, 0))],
        out_specs=col(0),
        out_shape=jax.ShapeDtypeStruct((n, HG_WIDTH), BF16),
        scratch_shapes=[pltpu.VMEM((HG_DK, HG_DK), F32),
                        pltpu.VMEM((HG_CHUNK, HG_DK), F32),
                        pltpu.VMEM((HG_CHUNK, HG_DK), F32)],
        compiler_params=pltpu.CompilerParams(
            dimension_semantics=("arbitrary", "arbitrary", "arbitrary")),
        name="hgrn2",
    )(hqig, hqig, hqig, hf, lb_row, gain)


def _out_router_kernel(oa_ref, ob_ref, oc_ref, h_ref, wo_ref, g_ref, wr_ref, br_ref,
                       h1_ref, u_ref, route_ref, cnt_ref, carry_ref):
    @pl.when(pl.program_id(0) == 0)
    def _():
        carry_ref[...] = jnp.zeros_like(carry_ref)

    h1 = (h_ref[...]
          + jnp.dot(oa_ref[...], wo_ref[0:512, :], preferred_element_type=F32)
          + jnp.dot(ob_ref[...], wo_ref[512:768, :], preferred_element_type=F32)
          + jnp.dot(oc_ref[...], wo_ref[768:1024, :], preferred_element_type=F32))
    h1_ref[...] = h1
    u = _rms(h1, g_ref[...])
    u_ref[...] = u
    logits = jnp.dot(u, wr_ref[...], precision=lax.Precision.HIGHEST,
                     preferred_element_type=F32) + br_ref[...]
    tm = logits.shape[0]
    lane = lax.broadcasted_iota(jnp.int32, (tm, LANES), 1).astype(F32)
    big = float(LANES)
    is_g = (lane >= N_EXPERTS) & (lane < N_EXPERTS + N_GROUPS)
    gl = jnp.where(is_g, logits, -jnp.inf)
    gmax = jnp.max(gl, axis=-1, keepdims=True)
    gsel = jnp.min(jnp.where(gl == gmax, lane, big), axis=-1, keepdims=True) - N_EXPERTS
    p_g = 1.0 / jnp.sum(jnp.exp(gl - gmax), axis=-1, keepdims=True)
    lo = gsel * EXPERTS_PER_GROUP
    el = jnp.where((lane >= lo) & (lane < lo + EXPERTS_PER_GROUP), logits, -jnp.inf)
    m1 = jnp.max(el, axis=-1, keepdims=True)
    i1 = jnp.min(jnp.where(el == m1, lane, big), axis=-1, keepdims=True)
    el2 = jnp.where(lane == i1, -jnp.inf, el)
    m2 = jnp.max(el2, axis=-1, keepdims=True)
    i2 = jnp.min(jnp.where(el2 == m2, lane, big), axis=-1, keepdims=True)
    r = jnp.exp(m2 - m1)
    gate1 = p_g / (1.0 + r)
    gate2 = gate1 * r
    oh1 = lane == i1
    oh2 = lane == i2
    onehot = jnp.where(oh1 | oh2, 1.0, 0.0)
    tri = (lax.broadcasted_iota(jnp.int32, (tm, tm), 1)
           < lax.broadcasted_iota(jnp.int32, (tm, tm), 0)).astype(BF16)
    before = jnp.dot(tri, onehot.astype(BF16), preferred_element_type=F32) + carry_ref[...]
    rank1 = jnp.sum(jnp.where(oh1, before, 0.0), axis=-1, keepdims=True)
    rank2 = jnp.sum(jnp.where(oh2, before, 0.0), axis=-1, keepdims=True)
    total = carry_ref[...] + jnp.sum(onehot, axis=0, keepdims=True)
    carry_ref[...] = total
    cnt_ref[...] = total
    route = jnp.where(lane == 0, i1, 0.0)
    for idx, val in ((1, i2), (2, gate1), (3, gate2), (4, rank1), (5, rank2)):
        route = jnp.where(lane == idx, val, route)
    route_ref[...] = route


def _out_router(oa, ob, oc, h, w_out, g, w_rt, b_rt):
    n = h.shape[0]
    tm = ROW_TILE
    row = lambda w: pl.BlockSpec((tm, w), lambda i: (i, 0))
    const = lambda r, c: pl.BlockSpec((r, c), lambda i: (0, 0))
    return pl.pallas_call(
        _out_router_kernel,
        grid=(n // tm,),
        in_specs=[row(512), row(256), row(256), row(D_MODEL), const(D_MODEL, D_MODEL),
                  const(1, D_MODEL), const(D_MODEL, LANES), const(1, LANES)],
        out_specs=[row(D_MODEL), row(D_MODEL), row(LANES), const(1, LANES)],
        out_shape=[jax.ShapeDtypeStruct((n, D_MODEL), F32),
                   jax.ShapeDtypeStruct((n, D_MODEL), F32),
                   jax.ShapeDtypeStruct((n, LANES), F32),
                   jax.ShapeDtypeStruct((1, LANES), F32)],
        scratch_shapes=[pltpu.VMEM((1, LANES), F32)],
        compiler_params=pltpu.CompilerParams(dimension_semantics=("arbitrary",)),
        name="out_router",
    )(oa, ob, oc, h, w_out, g, w_rt, b_rt)


def _row_scatter(src_ref, dst_ref, sem, r, d):
    return pltpu.make_async_copy(src_ref.at[pl.ds(r, 1)], dst_ref.at[pl.ds(d, 1)], sem)


def _dispatch_kernel(dest_ref, u_ref, xs_in_ref, xs_ref, sem, *, tm, n):
    del xs_in_ref
    base = pl.program_id(0) * tm

    def issue(r, carry):
        for k in range(2):
            _row_scatter(u_ref, xs_ref, sem, r, dest_ref[k * n + base + r]).start()
        return carry

    def drain(r, carry):
        for k in range(2):
            _row_scatter(u_ref, xs_ref, sem, r, dest_ref[k * n + base + r]).wait()
        return carry

    lax.fori_loop(0, tm, issue, 0)
    lax.fori_loop(0, tm, drain, 0)


def _dispatch(dest, u, xs_zero):
    n = u.shape[0]
    tm = ROW_TILE
    return pl.pallas_call(
        functools.partial(_dispatch_kernel, tm=tm, n=n),
        grid_spec=pltpu.PrefetchScalarGridSpec(
            num_scalar_prefetch=1,
            grid=(n // tm,),
            in_specs=[pl.BlockSpec((tm, D_MODEL), lambda i, d: (i, 0)),
                      pl.BlockSpec(memory_space=pl.ANY)],
            out_specs=pl.BlockSpec(memory_space=pl.ANY),
            scratch_shapes=[pltpu.SemaphoreType.DMA(())]),
        out_shape=jax.ShapeDtypeStruct(xs_zero.shape, xs_zero.dtype),
        input_output_aliases={2: 0},
        compiler_params=pltpu.CompilerParams(dimension_semantics=("arbitrary",)),
        name="moe_dispatch",
    )(dest, u, xs_zero)


def _expert_kernel(be_ref, na_ref, x_ref, w1_ref, w3_ref, w2_ref, y_ref):
    del be_ref
    active = pl.program_id(0) < na_ref[0]

    @pl.when(active)
    def _():
        x = x_ref[...].astype(BF16)
        a = jnp.dot(x, w1_ref[0], preferred_element_type=F32)
        b = jnp.dot(x, w3_ref[0], preferred_element_type=F32)
        act = (a * _sigmoid(a) * b).astype(BF16)
        y_ref[...] = jnp.dot(act, w2_ref[0], preferred_element_type=F32)

    @pl.when(jnp.logical_not(active))
    def _():
        y_ref[...] = jnp.zeros_like(y_ref)


def _experts(blk_expert, n_active, xs, w1, w3, w2):
    p_rows = xs.shape[0]
    tb = MOE_BLOCK
    last = lambda j, na: jnp.minimum(j, na[0] - 1)
    return pl.pallas_call(
        _expert_kernel,
        grid_spec=pltpu.PrefetchScalarGridSpec(
            num_scalar_prefetch=2,
            grid=(p_rows // tb,),
            in_specs=[pl.BlockSpec((tb, D_MODEL), lambda j, be, na: (last(j, na), 0)),
                      pl.BlockSpec((1, D_MODEL, D_EXPERT), lambda j, be, na: (be[j], 0, 0)),
                      pl.BlockSpec((1, D_MODEL, D_EXPERT), lambda j, be, na: (be[j], 0, 0)),
                      pl.BlockSpec((1, D_EXPERT, D_MODEL), lambda j, be, na: (be[j], 0, 0))],
            out_specs=pl.BlockSpec((tb, D_MODEL), lambda j, be, na: (j, 0))),
        out_shape=jax.ShapeDtypeStruct((p_rows, D_MODEL), F32),
        compiler_params=pltpu.CompilerParams(dimension_semantics=("arbitrary",)),
        name="moe_experts",
    )(blk_expert, n_active, xs, w1, w3, w2)


def _row_gather(src_ref, dst_ref, sem, d, r):
    return pltpu.make_async_copy(src_ref.at[pl.ds(d, 1)], dst_ref.at[pl.ds(r, 1)], sem)


def _combine_kernel(dest_ref, h_ref, route_ref, gfin_ref, ys_ref, o_ref,
                    y0_ref, y1_ref, sem, *, tm, n, final):
    base = pl.program_id(0) * tm
    bufs = (y0_ref, y1_ref)

    def issue(r, carry):
        for k in range(2):
            _row_gather(ys_ref, bufs[k], sem, dest_ref[k * n + base + r], r).start()
        return carry

    def drain(r, carry):
        for k in range(2):
            _row_gather(ys_ref, bufs[k], sem, dest_ref[k * n + base + r], r).wait()
        return carry

    lax.fori_loop(0, tm, issue, 0)
    lax.fori_loop(0, tm, drain, 0)
    route = route_ref[...]
    out = h_ref[...] + route[:, 2:3] * y0_ref[...] + route[:, 3:4] * y1_ref[...]
    if final:
        out = _rms(out, gfin_ref[...])
    o_ref[...] = out


def _combine(dest, h1, route, g_final, ys, final):
    n = h1.shape[0]
    tm = ROW_TILE
    return pl.pallas_call(
        functools.partial(_combine_kernel, tm=tm, n=n, final=final),
        grid_spec=pltpu.PrefetchScalarGridSpec(
            num_scalar_prefetch=1,
            grid=(n // tm,),
            in_specs=[pl.BlockSpec((tm, D_MODEL), lambda i, d: (i, 0)),
                      pl.BlockSpec((tm, LANES), lambda i, d: (i, 0)),
                      pl.BlockSpec((1, D_MODEL), lambda i, d: (0, 0)),
                      pl.BlockSpec(memory_space=pl.ANY)],
            out_specs=pl.BlockSpec((tm, D_MODEL), lambda i, d: (i, 0)),
            scratch_shapes=[pltpu.VMEM((tm, D_MODEL), F32), pltpu.VMEM((tm, D_MODEL), F32),
                            pltpu.SemaphoreType.DMA(())]),
        out_shape=jax.ShapeDtypeStruct((n, D_MODEL), F32),
        compiler_params=pltpu.CompilerParams(dimension_semantics=("arbitrary",)),
        name="moe_combine",
    )(dest, h1, route, g_final, ys)


def _rope_tables(seq_len):
    pos = (jnp.arange(seq_len) - LPAD).astype(F32)
    col = jnp.arange(DA_WIDTH)
    inv = ROPE_THETA ** (-(2.0 * (col % (DA_DQK // 2))).astype(F32) / DA_DQK)
    ang = pos[:, None] * inv[None, :]
    first = (col % DA_DQK) < DA_DQK // 2
    cos = jnp.cos(ang)
    sin = jnp.where(first[None, :], -jnp.sin(ang), jnp.sin(ang))
    scale = DA_DQK ** -0.5
    return (jnp.concatenate([cos * scale, cos], axis=1),
            jnp.concatenate([sin * scale, sin], axis=1))


def _in_weights(w):
    hq, hf, hi, hg = (w[:, i * 512:(i + 1) * 512] for i in range(4))
    dq, dk, dv = (w[:, 2048 + i * 256:2048 + (i + 1) * 256] for i in range(3))
    fq, fk, fv = (w[:, 2816 + i * 256:2816 + (i + 1) * 256] for i in range(3))
    ff = w[:, 3584:3588]
    col = jnp.arange(DA_WIDTH)
    half = DA_DQK // 2
    partner = jnp.where((col % DA_DQK) < half, col + half, col - half)
    cat = jnp.concatenate(
        [hq, hi, hg, hf, dq, dk, dq[:, partner], dk[:, partner], dv,
         fq * (FX_DH ** -0.5), fk, fv, ff, jnp.zeros((D_MODEL, LANES - FX_HEADS), w.dtype)],
        axis=1)
    return cat.astype(BF16)


def _pad_lanes(v, width=LANES):
    return jnp.zeros((1, width), F32).at[0, :v.shape[0]].set(v.astype(F32))


def kernel(x, meta_tokens, norm_mix, w_in, hgrn_lb, hgrn_norm, diff_lambda, diff_norm,
           fox_bias, fox_norm, w_out, norm_ffn, w_group, b_group, w_router, b_router,
           w1, w3, w2, norm_final):
    batch, seq, d = x.shape
    depth = w_in.shape[0]
    seq_len = LPAD + N_META + seq
    n = batch * seq_len
    blk = ATTN_BLOCK
    pad = jnp.zeros((batch, LPAD, d), x.dtype)
    meta = jnp.broadcast_to(meta_tokens.astype(x.dtype)[None], (batch, N_META, d))
    h = jnp.concatenate([pad, meta, x], axis=1).reshape(n, d)

    s_lb = jax.nn.softmax(hgrn_lb.astype(F32), axis=0)
    lb_all = jnp.cumsum(s_lb, axis=0) - s_lb[0]
    cos_t, sin_t = _rope_tables(seq_len)

    tb = MOE_BLOCK
    n_blocks = (2 * n) // tb + N_EXPERTS
    p_rows = n_blocks * tb

    for layer in range(depth):
        lam_init = 0.8 - 0.6 * math.exp(-0.3 * layer)
        hqig, hf, dqk, dv, fox, ff = _in_proj(
            h, norm_mix[layer][None, :], _in_weights(w_in[layer]), cos_t, sin_t, seq_len)

        o_a = _hgrn(hqig, hf, lb_all[layer][None, :], hgrn_norm[layer][None, :],
                    batch, seq_len)

        lam_vecs = jnp.zeros((8, LANES), F32).at[:4, :DA_DQK].set(diff_lambda[layer].astype(F32))
        o_b = _diff_attn(dqk, dv, lam_vecs, diff_norm[layer][None, :], lam_init, batch, seq_len)

        d_col = _fox_decay(ff.reshape(batch, seq_len, LANES), _pad_lanes(fox_bias[layer]))
        d_row = d_col[:, :, :8].reshape(batch, seq_len // blk, blk, 8).transpose(0, 1, 3, 2)
        o_c = _fox_attn(fox, d_col, d_row, fox_norm[layer][None, :], batch, seq_len)

        w_rt = jnp.concatenate(
            [w_router[layer], w_group[layer],
             jnp.zeros((d, LANES - N_EXPERTS - N_GROUPS), F32)], axis=1)
        b_rt = _pad_lanes(jnp.concatenate([b_router[layer], b_group[layer]]))
        h1, u2, route, counts = _out_router(
            o_a, o_b, o_c, h, w_out[layer].astype(BF16), norm_ffn[layer][None, :], w_rt, b_rt)

        cnt = counts[0, :N_EXPERTS].astype(jnp.int32)
        padded = (cnt + tb - 1) // tb * tb
        p_end = jnp.cumsum(padded)
        p_start = p_end - padded
        ids = route[:, 0:2].astype(jnp.int32)
        ranks = route[:, 4:6].astype(jnp.int32)
        dest = (p_start[ids] + ranks).T.reshape(2 * n)
        blk_expert = jnp.minimum(
            jnp.searchsorted(p_end, jnp.arange(n_blocks, dtype=jnp.int32) * tb, side='right'),
            N_EXPERTS - 1).astype(jnp.int32)
        n_active = (p_end[-1:] // tb).astype(jnp.int32)

        xs = _dispatch(dest, u2, jnp.zeros((p_rows, d), F32))
        ys = _experts(blk_expert, n_active, xs, w1[layer].astype(BF16),
                      w3[layer].astype(BF16), w2[layer].astype(BF16))
        h = _combine(dest, h1, route, norm_final[None, :], ys, final=(layer == depth - 1))

    return h.reshape(batch, seq_len, d)[:, LPAD + N_META:]
```

```python
import functools
import math

import jax
import jax.numpy as jnp
from jax import lax
from jax.experimental import pallas as pl
from jax.experimental.pallas import tpu as pltpu

F32 = jnp.float32
BF16 = jnp.bfloat16

D_MODEL = 1024
N_META = 16
HG_HEADS = 4
HG_DK = 128
HG_WIDTH = 512
HG_CHUNK = 64
HG_SUB = 16
DA_HEADS = 4
DA_DV = 64
DA_DQK = 32
DA_WIDTH = 256
FX_HEADS = 4
FX_DH = 64
FX_WIDTH = 256
N_GROUPS = 4
EXPERTS_PER_GROUP = 8
N_EXPERTS = 32
D_EXPERT = 512
ROPE_THETA = 10000.0
EPS = 1e-6
NEG = -1e30
TINY = 1e-30
LOG2E = 1.4426950408889634

LANES = 128
SLOT = LANES
HEAD_W = 4 * SLOT
ROW_TILE = 256
PROJ_TILE = 512
ATTN_BLOCK = 512
MOE_BLOCK = 256
LPAD = ATTN_BLOCK - N_META
ONE_LANE = 64
VMEM_LIMIT = 56 * 1024 * 1024

_C_HQIG = (0, 1536)
_C_HF = (1536, 2048)
_C_DQK = (2048, 3584)
_C_DQKP = (3584, 5120)
_C_DV = (5120, 5632)
_C_FQ = (5632, 6144)
_C_FK = (6144, 6656)
_C_FV = (6656, 7168)
_C_FF = (7168, 7296)
IN_COLS = 7296

_NT = (((1,), (1,)), ((), ()))
_TN = (((0,), (0,)), ((), ()))


def _sigmoid(x):
    return 1.0 / (1.0 + jnp.exp(-x))


def _rms(x, g):
    return x * lax.rsqrt(jnp.mean(x * x, axis=-1, keepdims=True) + EPS) * g


def _slot_lane(shape):
    return lax.broadcasted_iota(jnp.int32, shape, 1) % SLOT


def _in_proj_kernel(h_ref, g_ref, w_ref, cos_ref, sin_ref,
                    hqig_ref, hf_ref, dqk_ref, dv_ref, fqk_ref, fv_ref, ff_ref):
    u = _rms(h_ref[...], g_ref[...]).astype(BF16)

    def mm(c):
        return jnp.dot(u, w_ref[:, c[0]:c[1]], preferred_element_type=F32)

    def with_ones(v):
        return jnp.where(_slot_lane(v.shape) == ONE_LANE, 1.0, v)

    hqig_ref[...] = mm(_C_HQIG).astype(BF16)
    hf_ref[...] = mm(_C_HF)
    dqk_ref[...] = (mm(_C_DQK) * cos_ref[...] + mm(_C_DQKP) * sin_ref[...]).astype(BF16)
    dv_ref[...] = with_ones(mm(_C_DV)).astype(BF16)
    fqk_ref[:, 0:HEAD_W] = (mm(_C_FQ) * LOG2E).astype(BF16)
    fqk_ref[:, HEAD_W:2 * HEAD_W] = mm(_C_FK).astype(BF16)
    fv_ref[...] = with_ones(mm(_C_FV)).astype(BF16)
    ff_ref[...] = mm(_C_FF)


def _in_proj(h, g, w_cat, cos_t, sin_t, seq_len):
    n = h.shape[0]
    tm = PROJ_TILE
    n_seq_tiles = seq_len // tm
    row = lambda w: pl.BlockSpec((tm, w), lambda i: (i, 0))
    tab = pl.BlockSpec((tm, 1536), lambda i: (i % n_seq_tiles, 0))
    widths = (1536, 512, 1536, HEAD_W, 2 * HEAD_W, HEAD_W, LANES)
    dtypes = (BF16, F32, BF16, BF16, BF16, BF16, F32)
    return pl.pallas_call(
        _in_proj_kernel,
        grid=(n // tm,),
        in_specs=[row(D_MODEL),
                  pl.BlockSpec((1, D_MODEL), lambda i: (0, 0)),
                  pl.BlockSpec((D_MODEL, IN_COLS), lambda i: (0, 0),
                               pipeline_mode=pl.Buffered(1)),
                  tab, tab],
        out_specs=[row(w) for w in widths],
        out_shape=[jax.ShapeDtypeStruct((n, w), t) for w, t in zip(widths, dtypes)],
        compiler_params=pltpu.CompilerParams(dimension_semantics=("arbitrary",),
                                             vmem_limit_bytes=VMEM_LIMIT),
        name="in_proj",
    )(h, g, w_cat, cos_t, sin_t)


def _bf16_split3(x):
    hi = x.astype(BF16).astype(F32)
    r = x - hi
    mid = r.astype(BF16).astype(F32)
    return hi, mid, r - mid


def _fox_prep_kernel(ff_ref, b_ref, qk_ref, o_ref, carry_ref):
    @pl.when(pl.program_id(1) == 0)
    def _():
        carry_ref[...] = jnp.zeros_like(carry_ref)

    x = ff_ref[0] + b_ref[...]
    lf = jnp.minimum(x, 0.0) - jnp.log(1.0 + jnp.exp(-jnp.abs(x)))
    t = x.shape[0]
    tri = (lax.broadcasted_iota(jnp.int32, (t, t), 1)
           <= lax.broadcasted_iota(jnp.int32, (t, t), 0)).astype(F32)
    cs = jnp.dot(tri, lf, precision=lax.Precision.HIGHEST,
                 preferred_element_type=F32) + carry_ref[...]
    carry_ref[...] = cs[t - 1:t, :]
    parts = _bf16_split3(cs * LOG2E)
    lane = lax.broadcasted_iota(jnp.int32, (t, SLOT), 1)
    d0 = FX_DH
    for h in range(FX_HEADS):
        q = qk_ref[:, h * SLOT:(h + 1) * SLOT].astype(F32)
        k = qk_ref[:, HEAD_W + h * SLOT:HEAD_W + (h + 1) * SLOT].astype(F32)
        for i, part in enumerate(parts):
            col = part[:, h:h + 1]
            q = jnp.where(lane == d0 + i, col, q)
            k = jnp.where(lane == d0 + 3 + i, -col, k)
        q = jnp.where((lane >= d0 + 3) & (lane < d0 + 6), 1.0, q)
        k = jnp.where((lane >= d0) & (lane < d0 + 3), 1.0, k)
        o_ref[:, h * SLOT:(h + 1) * SLOT] = q.astype(BF16)
        o_ref[:, HEAD_W + h * SLOT:HEAD_W + (h + 1) * SLOT] = k.astype(BF16)


def _fox_prep(ff, bias_row, fqk, batch, seq_len):
    n = fqk.shape[0]
    t = ROW_TILE
    nt = seq_len // t
    return pl.pallas_call(
        _fox_prep_kernel,
        grid=(batch, nt),
        in_specs=[pl.BlockSpec((1, t, LANES), lambda b, j: (b, j, 0)),
                  pl.BlockSpec((1, LANES), lambda b, j: (0, 0)),
                  pl.BlockSpec((t, 2 * HEAD_W), lambda b, j: (b * nt + j, 0))],
        out_specs=pl.BlockSpec((t, 2 * HEAD_W), lambda b, j: (b * nt + j, 0)),
        out_shape=jax.ShapeDtypeStruct((n, 2 * HEAD_W), BF16),
        scratch_shapes=[pltpu.VMEM((1, LANES), F32)],
        compiler_params=pltpu.CompilerParams(dimension_semantics=("arbitrary", "arbitrary")),
        name="fox_prep",
    )(ff, bias_row, fqk)


def _softmax_block(s, m_prev):
    tiles = [s[:, t * LANES:(t + 1) * LANES] for t in range(s.shape[1] // LANES)]
    row_max = functools.reduce(jnp.maximum, tiles)
    m_new = jnp.maximum(m_prev, jnp.max(row_max, axis=-1, keepdims=True))
    alpha = jnp.exp2(m_prev - m_new)
    p = jnp.concatenate([jnp.exp2(t - m_new).astype(BF16) for t in tiles], axis=1)
    return p, alpha, m_new


def _block_start(j, blk):
    return j * blk if isinstance(j, int) else pl.multiple_of(j * blk, blk)


def _causal_valid(i, j, blk):
    q_idx = i * blk + lax.broadcasted_iota(jnp.int32, (blk, blk), 0)
    k_idx = j * blk + lax.broadcasted_iota(jnp.int32, (blk, blk), 1)
    return (k_idx <= q_idx) & (k_idx >= LPAD)


def _over_key_blocks(i, step):
    step(0, True)

    def plain(j, carry):
        step(j, False)
        return carry

    lax.fori_loop(1, i, plain, 0)

    @pl.when(i > 0)
    def _():
        step(i, True)


def _head_out(acc):
    lane = lax.broadcasted_iota(jnp.int32, acc.shape, 1)
    return jnp.where(lane < ONE_LANE, acc / acc[:, ONE_LANE:ONE_LANE + 1], 0.0)


def _head_rms(o, g):
    ms = jnp.sum(o * o, axis=-1, keepdims=True) * (1.0 / ONE_LANE)
    return o * lax.rsqrt(ms + EPS) * g


def _fox_attn_kernel(q_ref, k_ref, v_ref, g_ref, o_ref, m_ref, acc_ref, *, blk):
    i = pl.program_id(2)
    q = q_ref[...]
    m_ref[...] = jnp.full(m_ref.shape, NEG, F32)
    acc_ref[...] = jnp.zeros_like(acc_ref)

    def step(j, masked):
        k0 = _block_start(j, blk)
        s = lax.dot_general(q, k_ref[pl.ds(k0, blk), :], _NT, preferred_element_type=F32)
        if masked:
            s = jnp.where(_causal_valid(i, j, blk), s, NEG)
        p, alpha, m_new = _softmax_block(s, m_ref[...])
        acc_ref[...] = alpha * acc_ref[...] + jnp.dot(
            p, v_ref[pl.ds(k0, blk), :], preferred_element_type=F32)
        m_ref[...] = m_new

    _over_key_blocks(i, step)
    o_ref[...] = _head_rms(_head_out(acc_ref[...]), g_ref[...]).astype(BF16)


def _fox_attn(fqk, fv, gain, batch, seq_len):
    n = fqk.shape[0]
    blk = ATTN_BLOCK
    nq = seq_len // blk
    return pl.pallas_call(
        functools.partial(_fox_attn_kernel, blk=blk),
        grid=(batch, FX_HEADS, nq),
        in_specs=[pl.BlockSpec((blk, SLOT), lambda b, h, i: (b * nq + i, h)),
                  pl.BlockSpec((seq_len, SLOT), lambda b, h, i: (b, FX_HEADS + h)),
                  pl.BlockSpec((seq_len, SLOT), lambda b, h, i: (b, h)),
                  pl.BlockSpec((1, SLOT), lambda b, h, i: (0, 0))],
        out_specs=pl.BlockSpec((blk, SLOT), lambda b, h, i: (b * nq + i, h)),
        out_shape=jax.ShapeDtypeStruct((n, HEAD_W), BF16),
        scratch_shapes=[pltpu.VMEM((blk, LANES), F32), pltpu.VMEM((blk, SLOT), F32)],
        compiler_params=pltpu.CompilerParams(
            dimension_semantics=("arbitrary", "arbitrary", "arbitrary"),
            vmem_limit_bytes=VMEM_LIMIT),
        name="fox_attn",
    )(fqk, fqk, fv, gain)


def _diff_attn_kernel(q_ref, k_ref, v_ref, lam_ref, g_ref, o_ref, m_ref, acc_ref,
                      *, blk, lam_init):
    i = pl.program_id(2)
    q = q_ref[...]
    m_ref[...] = jnp.full(m_ref.shape, NEG, F32)
    acc_ref[...] = jnp.zeros_like(acc_ref)

    def step(j, masked):
        k0 = _block_start(j, blk)
        mask = _causal_valid(i, j, blk) if masked else None
        ps, alphas = [], []
        for c in range(2):
            s = lax.dot_general(q, k_ref[pl.ds(k0, blk), c * SLOT:(c + 1) * SLOT], _NT,
                                preferred_element_type=F32)
            if masked:
                s = jnp.where(mask, s, NEG)
            p, alpha, m_new = _softmax_block(s, m_ref[c])
            m_ref[c] = m_new
            ps.append(p)
            alphas.append(alpha)
        pv = jnp.dot(jnp.concatenate(ps, axis=0), v_ref[pl.ds(k0, blk), :],
                     preferred_element_type=F32)
        for c in range(2):
            acc_ref[c] = alphas[c] * acc_ref[c] + pv[c * blk:(c + 1) * blk]

    _over_key_blocks(i, step)
    lv = lam_ref[...]
    lam = (jnp.exp(jnp.sum(lv[0:1] * lv[1:2], axis=-1, keepdims=True))
           - jnp.exp(jnp.sum(lv[2:3] * lv[3:4], axis=-1, keepdims=True)) + lam_init)
    o = _head_out(acc_ref[0]) - lam * _head_out(acc_ref[1])
    o_ref[...] = (_head_rms(o, g_ref[...]) * (1.0 - lam_init)).astype(BF16)


def _diff_attn(dqk, dv, lam_vecs, gain, lam_init, batch, seq_len):
    n = dqk.shape[0]
    blk = ATTN_BLOCK
    nq = seq_len // blk
    return pl.pallas_call(
        functools.partial(_diff_attn_kernel, blk=blk, lam_init=lam_init),
        grid=(batch, DA_HEADS, nq),
        in_specs=[pl.BlockSpec((blk, SLOT), lambda b, h, i: (b * nq + i, h)),
                  pl.BlockSpec((seq_len, 2 * SLOT), lambda b, h, i: (b, 2 + h)),
                  pl.BlockSpec((seq_len, SLOT), lambda b, h, i: (b, h)),
                  pl.BlockSpec((8, LANES), lambda b, h, i: (0, 0)),
                  pl.BlockSpec((1, SLOT), lambda b, h, i: (0, 0))],
        out_specs=pl.BlockSpec((blk, SLOT), lambda b, h, i: (b * nq + i, h)),
        out_shape=jax.ShapeDtypeStruct((n, HEAD_W), BF16),
        scratch_shapes=[pltpu.VMEM((2, blk, LANES), F32), pltpu.VMEM((2, blk, SLOT), F32)],
        compiler_params=pltpu.CompilerParams(
            dimension_semantics=("arbitrary", "arbitrary", "arbitrary"),
            vmem_limit_bytes=VMEM_LIMIT),
        name="diff_attn",
    )(dqk, dqk, dv, lam_vecs, gain)


def _hgrn_kernel(q_ref, i_ref, g_ref, f_ref, lb_ref, gain_ref, o_ref,
                 st_ref, gs_ref, ks_ref, *, tl):
    j = pl.program_id(2)

    @pl.when(j == 0)
    def _():
        st_ref[...] = jnp.zeros_like(st_ref)

    c_rows, sub = HG_CHUNK, HG_SUB
    lb = lb_ref[...]
    tri = (lax.broadcasted_iota(jnp.int32, (c_rows, c_rows), 1)
           <= lax.broadcasted_iota(jnp.int32, (c_rows, c_rows), 0)).astype(F32)
    row_c = lax.broadcasted_iota(jnp.int32, (c_rows, 1), 0)
    t_sub = lax.broadcasted_iota(jnp.int32, (sub, 1), 0)
    lane_c = lax.broadcasted_iota(jnp.int32, (sub, c_rows), 1)

    def chunk(c, carry):
        r0 = pl.multiple_of(c * c_rows, c_rows)
        z = f_ref[pl.ds(r0, c_rows), :]
        sig = _sigmoid(z)
        log_f = jnp.log(jnp.maximum(lb + (1.0 - lb) * sig, TINY))
        kk = (1.0 - lb) * (1.0 - sig)
        valid = (j * tl + r0 + row_c) >= LPAD
        log_f = jnp.where(valid, log_f, 0.0)
        kk = jnp.where(valid, kk, 0.0)
        G = jnp.dot(tri, log_f, precision=lax.Precision.HIGHEST, preferred_element_type=F32)
        qf = q_ref[pl.ds(r0, c_rows), :].astype(F32)
        qs = qf * _sigmoid(qf) * (HG_DK ** -0.5)
        v = i_ref[pl.ds(r0, c_rows), :]
        gs_ref[...] = G
        ks_ref[...] = kk
        st = st_ref[...]
        o_inter = lax.dot_general((qs * jnp.exp(G)).astype(BF16), st.astype(BF16), _NT,
                                  preferred_element_type=F32)
        outs = []
        for b in range(c_rows // sub):
            lo = b * sub
            q_b = qs[lo:lo + sub]
            g_b = G[lo:lo + sub]
            if b == 0:
                a_blk = jnp.zeros((sub, c_rows), F32)
            else:
                ref = G[lo - 1:lo]
                qa = q_b * jnp.exp(g_b - ref)
                kb = jnp.where(row_c < lo, kk * jnp.exp(jnp.minimum(ref - G, 0.0)), 0.0)
                a_blk = lax.dot_general(qa.astype(BF16), kb.astype(BF16), _NT,
                                        preferred_element_type=F32)

            for s in range(lo, lo + sub):
                g_s = gs_ref[s:s + 1, :]
                k_s = ks_ref[s:s + 1, :]
                y = q_b * k_s * jnp.exp(jnp.minimum(g_b - g_s, 0.0))
                a = jnp.sum(y, axis=-1, keepdims=True)
                a = jnp.where(t_sub + lo >= s, a, 0.0)
                a_blk = jnp.where(lane_c == s, a, a_blk)
            outs.append(o_inter[lo:lo + sub]
                        + jnp.dot(a_blk.astype(BF16), v, preferred_element_type=F32))
        o = jnp.concatenate(outs, axis=0)
        g_last = G[c_rows - 1:c_rows]
        kd = kk * jnp.exp(g_last - G)
        st_ref[...] = st * jnp.exp(g_last) + lax.dot_general(
            v, kd.astype(BF16), _TN, preferred_element_type=F32)
        gate = g_ref[pl.ds(r0, c_rows), :].astype(F32)
        o = _rms(o, gain_ref[...]) * (gate * _sigmoid(gate))
        o_ref[pl.ds(r0, c_rows), :] = o.astype(BF16)
        return carry

    lax.fori_loop(0, tl // c_rows, chunk, 0)


def _hgrn(hqig, hf, lb_row, gain, batch, seq_len):
    n = hqig.shape[0]
    tl = ROW_TILE
    nt = seq_len // tl
    col = lambda off: pl.BlockSpec((tl, HG_DK), lambda b, h, j: (b * nt + j, off + h))
    return pl.pallas_call(
        functools.partial(_hgrn_kernel, tl=tl),
        grid=(batch, HG_HEADS, nt),
        in_specs=[col(0), col(HG_HEADS), col(2 * HG_HEADS), col(0),
                  pl.BlockSpec((1, HG_DK), lambda b, h, j: (0, h)),
                  pl.BlockSpec((1, HG_DK), lambda b, h, j: (0, 0))],
        out_specs=col(0),
        out_shape=jax.ShapeDtypeStruct((n, HG_WIDTH), BF16),
        scratch_shapes=[pltpu.VMEM((HG_DK, HG_DK), F32),
                        pltpu.VMEM((HG_CHUNK, HG_DK), F32),
                        pltpu.VMEM((HG_CHUNK, HG_DK), F32)],
        compiler_params=pltpu.CompilerParams(
            dimension_semantics=("arbitrary", "arbitrary", "arbitrary")),
        name="hgrn2",
    )(hqig, hqig, hqig, hf, lb_row, gain)


def _out_router_kernel(oa_ref, ob_ref, oc_ref, h_ref, wo_ref, g_ref, wr_ref, br_ref,
                       h1_ref, u_ref, route_ref, cnt_ref, carry_ref):
    @pl.when(pl.program_id(0) == 0)
    def _():
        carry_ref[...] = jnp.zeros_like(carry_ref)

    h1 = (h_ref[...]
          + jnp.dot(oa_ref[...], wo_ref[0:512, :], preferred_element_type=F32)
          + jnp.dot(ob_ref[...], wo_ref[512:1024, :], preferred_element_type=F32)
          + jnp.dot(oc_ref[...], wo_ref[1024:1536, :], preferred_element_type=F32))
    h1_ref[...] = h1
    u = _rms(h1, g_ref[...])
    u_ref[...] = u
    logits = jnp.dot(u, wr_ref[...], precision=lax.Precision.HIGHEST,
                     preferred_element_type=F32) + br_ref[...]
    tm = logits.shape[0]
    lane = lax.broadcasted_iota(jnp.int32, (tm, LANES), 1).astype(F32)
    big = float(LANES)
    is_g = (lane >= N_EXPERTS) & (lane < N_EXPERTS + N_GROUPS)
    gl = jnp.where(is_g, logits, -jnp.inf)
    gmax = jnp.max(gl, axis=-1, keepdims=True)
    gsel = jnp.min(jnp.where(gl == gmax, lane, big), axis=-1, keepdims=True) - N_EXPERTS
    p_g = 1.0 / jnp.sum(jnp.exp(gl - gmax), axis=-1, keepdims=True)
    lo = gsel * EXPERTS_PER_GROUP
    el = jnp.where((lane >= lo) & (lane < lo + EXPERTS_PER_GROUP), logits, -jnp.inf)
    m1 = jnp.max(el, axis=-1, keepdims=True)
    i1 = jnp.min(jnp.where(el == m1, lane, big), axis=-1, keepdims=True)
    el2 = jnp.where(lane == i1, -jnp.inf, el)
    m2 = jnp.max(el2, axis=-1, keepdims=True)
    i2 = jnp.min(jnp.where(el2 == m2, lane, big), axis=-1, keepdims=True)
    r = jnp.exp(m2 - m1)
    gate1 = p_g / (1.0 + r)
    gate2 = gate1 * r
    oh1 = lane == i1
    oh2 = lane == i2
    onehot = jnp.where(oh1 | oh2, 1.0, 0.0)
    tri = (lax.broadcasted_iota(jnp.int32, (tm, tm), 1)
           < lax.broadcasted_iota(jnp.int32, (tm, tm), 0)).astype(BF16)
    before = jnp.dot(tri, onehot.astype(BF16), preferred_element_type=F32) + carry_ref[...]
    rank1 = jnp.sum(jnp.where(oh1, before, 0.0), axis=-1, keepdims=True)
    rank2 = jnp.sum(jnp.where(oh2, before, 0.0), axis=-1, keepdims=True)
    total = carry_ref[...] + jnp.sum(onehot, axis=0, keepdims=True)
    carry_ref[...] = total
    cnt_ref[...] = total
    route = jnp.where(lane == 0, i1, 0.0)
    for idx, val in ((1, i2), (2, gate1), (3, gate2), (4, rank1), (5, rank2)):
        route = jnp.where(lane == idx, val, route)
    route_ref[...] = route


def _out_router(oa, ob, oc, h, w_out, g, w_rt, b_rt):
    n = h.shape[0]
    tm = ROW_TILE
    row = lambda w: pl.BlockSpec((tm, w), lambda i: (i, 0))
    const = lambda r, c: pl.BlockSpec((r, c), lambda i: (0, 0))
    return pl.pallas_call(
        _out_router_kernel,
        grid=(n // tm,),
        in_specs=[row(HG_WIDTH), row(HEAD_W), row(HEAD_W), row(D_MODEL),
                  const(HG_WIDTH + 2 * HEAD_W, D_MODEL),
                  const(1, D_MODEL), const(D_MODEL, LANES), const(1, LANES)],
        out_specs=[row(D_MODEL), row(D_MODEL), row(LANES), const(1, LANES)],
        out_shape=[jax.ShapeDtypeStruct((n, D_MODEL), F32),
                   jax.ShapeDtypeStruct((n, D_MODEL), F32),
                   jax.ShapeDtypeStruct((n, LANES), F32),
                   jax.ShapeDtypeStruct((1, LANES), F32)],
        scratch_shapes=[pltpu.VMEM((1, LANES), F32)],
        compiler_params=pltpu.CompilerParams(dimension_semantics=("arbitrary",)),
        name="out_router",
    )(oa, ob, oc, h, w_out, g, w_rt, b_rt)


def _row_scatter(src_ref, dst_ref, sem, r, d):
    return pltpu.make_async_copy(src_ref.at[pl.ds(r, 1)], dst_ref.at[pl.ds(d, 1)], sem)


def _dispatch_kernel(dest_ref, u_ref, xs_in_ref, xs_ref, sem, *, tm, n):
    del xs_in_ref
    base = pl.program_id(0) * tm

    def issue(r, carry):
        for k in range(2):
            _row_scatter(u_ref, xs_ref, sem, r, dest_ref[k * n + base + r]).start()
        return carry

    def drain(r, carry):
        for k in range(2):
            _row_scatter(u_ref, xs_ref, sem, r, dest_ref[k * n + base + r]).wait()
        return carry

    lax.fori_loop(0, tm, issue, 0)
    lax.fori_loop(0, tm, drain, 0)


def _dispatch(dest, u, xs_zero):
    n = u.shape[0]
    tm = ROW_TILE
    return pl.pallas_call(
        functools.partial(_dispatch_kernel, tm=tm, n=n),
        grid_spec=pltpu.PrefetchScalarGridSpec(
            num_scalar_prefetch=1,
            grid=(n // tm,),
            in_specs=[pl.BlockSpec((tm, D_MODEL), lambda i, d: (i, 0)),
                      pl.BlockSpec(memory_space=pl.ANY)],
            out_specs=pl.BlockSpec(memory_space=pl.ANY),
            scratch_shapes=[pltpu.SemaphoreType.DMA(())]),
        out_shape=jax.ShapeDtypeStruct(xs_zero.shape, xs_zero.dtype),
        input_output_aliases={2: 0},
        compiler_params=pltpu.CompilerParams(dimension_semantics=("arbitrary",)),
        name="moe_dispatch",
    )(dest, u, xs_zero)


def _expert_kernel(be_ref, na_ref, x_ref, w1_ref, w3_ref, w2_ref, y_ref):
    del be_ref
    active = pl.program_id(0) < na_ref[0]

    @pl.when(active)
    def _():
        x = x_ref[...].astype(BF16)
        a = jnp.dot(x, w1_ref[0], preferred_element_type=F32)
        b = jnp.dot(x, w3_ref[0], preferred_element_type=F32)
        act = (a * _sigmoid(a) * b).astype(BF16)
        y_ref[...] = jnp.dot(act, w2_ref[0], preferred_element_type=F32)

    @pl.when(jnp.logical_not(active))
    def _():
        y_ref[...] = jnp.zeros_like(y_ref)


def _experts(blk_expert, n_active, xs, w1, w3, w2):
    p_rows = xs.shape[0]
    tb = MOE_BLOCK
    last = lambda j, na: jnp.minimum(j, na[0] - 1)
    return pl.pallas_call(
        _expert_kernel,
        grid_spec=pltpu.PrefetchScalarGridSpec(
            num_scalar_prefetch=2,
            grid=(p_rows // tb,),
            in_specs=[pl.BlockSpec((tb, D_MODEL), lambda j, be, na: (last(j, na), 0)),
                      pl.BlockSpec((1, D_MODEL, D_EXPERT), lambda j, be, na: (be[j], 0, 0)),
                      pl.BlockSpec((1, D_MODEL, D_EXPERT), lambda j, be, na: (be[j], 0, 0)),
                      pl.BlockSpec((1, D_EXPERT, D_MODEL), lambda j, be, na: (be[j], 0, 0))],
            out_specs=pl.BlockSpec((tb, D_MODEL), lambda j, be, na: (j, 0))),
        out_shape=jax.ShapeDtypeStruct((p_rows, D_MODEL), F32),
        compiler_params=pltpu.CompilerParams(dimension_semantics=("arbitrary",)),
        name="moe_experts",
    )(blk_expert, n_active, xs, w1, w3, w2)


def _row_gather(src_ref, dst_ref, sem, d, r):
    return pltpu.make_async_copy(src_ref.at[pl.ds(d, 1)], dst_ref.at[pl.ds(r, 1)], sem)


def _combine_kernel(dest_ref, h_ref, route_ref, gfin_ref, ys_ref, o_ref,
                    y0_ref, y1_ref, sem, *, tm, n, final):
    base = pl.program_id(0) * tm
    bufs = (y0_ref, y1_ref)

    def issue(r, carry):
        for k in range(2):
            _row_gather(ys_ref, bufs[k], sem, dest_ref[k * n + base + r], r).start()
        return carry

    def drain(r, carry):
        for k in range(2):
            _row_gather(ys_ref, bufs[k], sem, dest_ref[k * n + base + r], r).wait()
        return carry

    lax.fori_loop(0, tm, issue, 0)
    lax.fori_loop(0, tm, drain, 0)
    route = route_ref[...]
    out = h_ref[...] + route[:, 2:3] * y0_ref[...] + route[:, 3:4] * y1_ref[...]
    if final:
        out = _rms(out, gfin_ref[...])
    o_ref[...] = out


def _combine(dest, h1, route, g_final, ys, final):
    n = h1.shape[0]
    tm = ROW_TILE
    return pl.pallas_call(
        functools.partial(_combine_kernel, tm=tm, n=n, final=final),
        grid_spec=pltpu.PrefetchScalarGridSpec(
            num_scalar_prefetch=1,
            grid=(n // tm,),
            in_specs=[pl.BlockSpec((tm, D_MODEL), lambda i, d: (i, 0)),
                      pl.BlockSpec((tm, LANES), lambda i, d: (i, 0)),
                      pl.BlockSpec((1, D_MODEL), lambda i, d: (0, 0)),
                      pl.BlockSpec(memory_space=pl.ANY)],
            out_specs=pl.BlockSpec((tm, D_MODEL), lambda i, d: (i, 0)),
            scratch_shapes=[pltpu.VMEM((tm, D_MODEL), F32), pltpu.VMEM((tm, D_MODEL), F32),
                            pltpu.SemaphoreType.DMA(())]),
        out_shape=jax.ShapeDtypeStruct((n, D_MODEL), F32),
        compiler_params=pltpu.CompilerParams(dimension_semantics=("arbitrary",)),
        name="moe_combine",
    )(dest, h1, route, g_final, ys)


def _slot_cols():
    src = jnp.arange(4 * 64)
    return (src // 64) * SLOT + src % 64


def _diff_k_cols():
    src = jnp.arange(DA_WIDTH)
    h, c, j = src // DA_DV, (src % DA_DV) // DA_DQK, src % DA_DQK
    return h * 2 * SLOT + c * SLOT + c * DA_DQK + j


def _spread(x, cols, width):
    return jnp.zeros(x.shape[:-1] + (width,), x.dtype).at[..., cols].set(x)


def _rope_tables(seq_len):
    pos = (jnp.arange(seq_len) - LPAD).astype(F32)
    col = jnp.arange(DA_WIDTH)
    inv = ROPE_THETA ** (-(2.0 * (col % (DA_DQK // 2))).astype(F32) / DA_DQK)
    ang = pos[:, None] * inv[None, :]
    first = (col % DA_DQK) < DA_DQK // 2
    cos = jnp.cos(ang)
    sin = jnp.where(first[None, :], -jnp.sin(ang), jnp.sin(ang))
    q_scale = DA_DQK ** -0.5 * LOG2E
    lay = lambda t: jnp.concatenate(
        [_spread(t * q_scale, _slot_cols(), HEAD_W), _spread(t, _diff_k_cols(), 2 * HEAD_W)],
        axis=1)
    return lay(cos), lay(sin)


def _in_weights(w):
    hq, hf, hi, hg = (w[:, i * 512:(i + 1) * 512] for i in range(4))
    dq, dk, dv = (w[:, 2048 + i * 256:2048 + (i + 1) * 256] for i in range(3))
    fq, fk, fv = (w[:, 2816 + i * 256:2816 + (i + 1) * 256] for i in range(3))
    ff = w[:, 3584:3588]
    col = jnp.arange(DA_WIDTH)
    half = DA_DQK // 2
    partner = jnp.where((col % DA_DQK) < half, col + half, col - half)
    slots = lambda m: _spread(m, _slot_cols(), HEAD_W)
    kslots = lambda m: _spread(m, _diff_k_cols(), 2 * HEAD_W)
    cat = jnp.concatenate(
        [hq, hi, hg, hf,
         slots(dq), kslots(dk), slots(dq[:, partner]), kslots(dk[:, partner]),
         slots(dv), slots(fq * (FX_DH ** -0.5)), slots(fk), slots(fv),
         ff, jnp.zeros((D_MODEL, LANES - FX_HEADS), w.dtype)],
        axis=1)
    return cat.astype(BF16)


def _pad_lanes(v, width=LANES):
    return jnp.zeros((1, width), F32).at[0, :v.shape[0]].set(v.astype(F32))


def kernel(x, meta_tokens, norm_mix, w_in, hgrn_lb, hgrn_norm, diff_lambda, diff_norm,
           fox_bias, fox_norm, w_out, norm_ffn, w_group, b_group, w_router, b_router,
           w1, w3, w2, norm_final):
    batch, seq, d = x.shape
    depth = w_in.shape[0]
    seq_len = LPAD + N_META + seq
    n = batch * seq_len
    pad = jnp.zeros((batch, LPAD, d), x.dtype)
    meta = jnp.broadcast_to(meta_tokens.astype(x.dtype)[None], (batch, N_META, d))
    h = jnp.concatenate([pad, meta, x], axis=1).reshape(n, d)

    s_lb = jax.nn.softmax(hgrn_lb.astype(F32), axis=0)
    lb_all = jnp.cumsum(s_lb, axis=0) - s_lb[0]
    cos_t, sin_t = _rope_tables(seq_len)

    tb = MOE_BLOCK
    n_blocks = (2 * n) // tb + N_EXPERTS
    p_rows = n_blocks * tb

    for layer in range(depth):
        lam_init = 0.8 - 0.6 * math.exp(-0.3 * layer)
        hqig, hf, dqk, dv, fqk, fv, ff = _in_proj(
            h, norm_mix[layer][None, :], _in_weights(w_in[layer]), cos_t, sin_t, seq_len)

        o_a = _hgrn(hqig, hf, lb_all[layer][None, :], hgrn_norm[layer][None, :],
                    batch, seq_len)

        lam_vecs = jnp.zeros((8, LANES), F32).at[:4, :DA_DQK].set(diff_lambda[layer].astype(F32))
        o_b = _diff_attn(dqk, dv, lam_vecs, _pad_lanes(diff_norm[layer]), lam_init,
                         batch, seq_len)

        fqk_aug = _fox_prep(ff.reshape(batch, seq_len, LANES), _pad_lanes(fox_bias[layer]),
                            fqk, batch, seq_len)
        o_c = _fox_attn(fqk_aug, fv, _pad_lanes(fox_norm[layer]), batch, seq_len)

        wo = w_out[layer]
        slot_rows = lambda m: _spread(m.T, _slot_cols(), HEAD_W).T
        wo_cat = jnp.concatenate(
            [wo[:HG_WIDTH], slot_rows(wo[HG_WIDTH:HG_WIDTH + DA_WIDTH]),
             slot_rows(wo[HG_WIDTH + DA_WIDTH:])], axis=0).astype(BF16)
        w_rt = jnp.concatenate(
            [w_router[layer], w_group[layer],
             jnp.zeros((d, LANES - N_EXPERTS - N_GROUPS), F32)], axis=1)
        b_rt = _pad_lanes(jnp.concatenate([b_router[layer], b_group[layer]]))
        h1, u2, route, counts = _out_router(
            o_a, o_b, o_c, h, wo_cat, norm_ffn[layer][None, :], w_rt, b_rt)

        cnt = counts[0, :N_EXPERTS].astype(jnp.int32)
        padded = (cnt + tb - 1) // tb * tb
        p_end = jnp.cumsum(padded)
        p_start = p_end - padded
        ids = route[:, 0:2].astype(jnp.int32)
        ranks = route[:, 4:6].astype(jnp.int32)
        dest = (p_start[ids] + ranks).T.reshape(2 * n)
        blk_expert = jnp.minimum(
            jnp.searchsorted(p_end, jnp.arange(n_blocks, dtype=jnp.int32) * tb, side='right'),
            N_EXPERTS - 1).astype(jnp.int32)
        n_active = (p_end[-1:] // tb).astype(jnp.int32)

        xs = _dispatch(dest, u2, jnp.zeros((p_rows, d), F32))
        ys = _experts(blk_expert, n_active, xs, w1[layer].astype(BF16),
                      w3[layer].astype(BF16), w2[layer].astype(BF16))
        h = _combine(dest, h1, route, norm_final[None, :], ys, final=(layer == depth - 1))

    return h.reshape(batch, seq_len, d)[:, LPAD + N_META:]
```

```python
import functools
import math

import jax
import jax.numpy as jnp
from jax import lax
from jax.experimental import pallas as pl
from jax.experimental.pallas import tpu as pltpu

F32 = jnp.float32
BF16 = jnp.bfloat16

D_MODEL = 1024
N_META = 16
HG_HEADS = 4
HG_DK = 128
HG_WIDTH = 512
HG_CHUNK = 64
HG_SUB = 16
HG_PER_STEP = 2
DA_HEADS = 4
DA_DV = 64
DA_DQK = 32
DA_WIDTH = 256
FX_HEADS = 4
FX_DH = 64
FX_WIDTH = 256
N_GROUPS = 4
EXPERTS_PER_GROUP = 8
N_EXPERTS = 32
D_EXPERT = 512
ROPE_THETA = 10000.0
EPS = 1e-6
NEG = -1e30
TINY = 1e-30
LOG2E = 1.4426950408889634

LANES = 128
SLOT = LANES
HEAD_W = 4 * SLOT
ROW_TILE = 256
PROJ_TILE = 512
ATTN_BLOCK = 512
MOE_BLOCK = 256
LPAD = ATTN_BLOCK - N_META
ONE_LANE = 64
VMEM_LIMIT = 56 * 1024 * 1024

_C_HQIG = (0, 1536)
_C_HF = (1536, 2048)
_C_DQK = (2048, 3584)
_C_DQKP = (3584, 5120)
_C_DV = (5120, 5632)
_C_FQ = (5632, 6144)
_C_FK = (6144, 6656)
_C_FV = (6656, 7168)
_C_FF = (7168, 7296)
IN_COLS = 7296

_NT = (((1,), (1,)), ((), ()))
_TN = (((0,), (0,)), ((), ()))


def _sigmoid(x):
    return 1.0 / (1.0 + jnp.exp(-x))


def _rms(x, g):
    return x * lax.rsqrt(jnp.mean(x * x, axis=-1, keepdims=True) + EPS) * g


def _slot_lane(shape):
    return lax.broadcasted_iota(jnp.int32, shape, 1) % SLOT


def _in_proj_kernel(h_ref, g_ref, w_ref, cos_ref, sin_ref,
                    hqig_ref, hf_ref, dqk_ref, dv_ref, fqk_ref, fv_ref, ff_ref):
    u = _rms(h_ref[...], g_ref[...]).astype(BF16)

    def mm(c):
        return jnp.dot(u, w_ref[:, c[0]:c[1]], preferred_element_type=F32)

    def with_ones(v):
        return jnp.where(_slot_lane(v.shape) == ONE_LANE, 1.0, v)

    hqig_ref[...] = mm(_C_HQIG).astype(BF16)
    hf_ref[...] = mm(_C_HF)
    dqk_ref[...] = (mm(_C_DQK) * cos_ref[...] + mm(_C_DQKP) * sin_ref[...]).astype(BF16)
    dv_ref[...] = with_ones(mm(_C_DV)).astype(BF16)
    fqk_ref[:, 0:HEAD_W] = (mm(_C_FQ) * LOG2E).astype(BF16)
    fqk_ref[:, HEAD_W:2 * HEAD_W] = mm(_C_FK).astype(BF16)
    fv_ref[...] = with_ones(mm(_C_FV)).astype(BF16)
    ff_ref[...] = mm(_C_FF)


def _in_proj(h, g, w_cat, cos_t, sin_t, seq_len):
    n = h.shape[0]
    tm = PROJ_TILE
    n_seq_tiles = seq_len // tm
    row = lambda w: pl.BlockSpec((tm, w), lambda i: (i, 0))
    tab = pl.BlockSpec((tm, 1536), lambda i: (i % n_seq_tiles, 0))
    widths = (1536, 512, 1536, HEAD_W, 2 * HEAD_W, HEAD_W, LANES)
    dtypes = (BF16, F32, BF16, BF16, BF16, BF16, F32)
    return pl.pallas_call(
        _in_proj_kernel,
        grid=(n // tm,),
        in_specs=[row(D_MODEL),
                  pl.BlockSpec((1, D_MODEL), lambda i: (0, 0)),
                  pl.BlockSpec((D_MODEL, IN_COLS), lambda i: (0, 0),
                               pipeline_mode=pl.Buffered(1)),
                  tab, tab],
        out_specs=[row(w) for w in widths],
        out_shape=[jax.ShapeDtypeStruct((n, w), t) for w, t in zip(widths, dtypes)],
        compiler_params=pltpu.CompilerParams(dimension_semantics=("arbitrary",),
                                             vmem_limit_bytes=VMEM_LIMIT),
        name="in_proj",
    )(h, g, w_cat, cos_t, sin_t)


def _bf16_split3(x):
    hi = x.astype(BF16).astype(F32)
    r = x - hi
    mid = r.astype(BF16).astype(F32)
    return hi, mid, r - mid


def _fox_prep_kernel(ff_ref, b_ref, qk_ref, o_ref, carry_ref):
    @pl.when(pl.program_id(1) == 0)
    def _():
        carry_ref[...] = jnp.zeros_like(carry_ref)

    x = ff_ref[0] + b_ref[...]
    lf = jnp.minimum(x, 0.0) - jnp.log(1.0 + jnp.exp(-jnp.abs(x)))
    t = x.shape[0]
    tri = (lax.broadcasted_iota(jnp.int32, (t, t), 1)
           <= lax.broadcasted_iota(jnp.int32, (t, t), 0)).astype(F32)
    cs = jnp.dot(tri, lf, precision=lax.Precision.HIGHEST,
                 preferred_element_type=F32) + carry_ref[...]
    carry_ref[...] = cs[t - 1:t, :]
    parts = _bf16_split3(cs * LOG2E)
    lane = lax.broadcasted_iota(jnp.int32, (t, SLOT), 1)
    d0 = FX_DH
    for h in range(FX_HEADS):
        q = qk_ref[:, h * SLOT:(h + 1) * SLOT].astype(F32)
        k = qk_ref[:, HEAD_W + h * SLOT:HEAD_W + (h + 1) * SLOT].astype(F32)
        for i, part in enumerate(parts):
            col = part[:, h:h + 1]
            q = jnp.where(lane == d0 + i, col, q)
            k = jnp.where(lane == d0 + 3 + i, -col, k)
        q = jnp.where((lane >= d0 + 3) & (lane < d0 + 6), 1.0, q)
        k = jnp.where((lane >= d0) & (lane < d0 + 3), 1.0, k)
        o_ref[:, h * SLOT:(h + 1) * SLOT] = q.astype(BF16)
        o_ref[:, HEAD_W + h * SLOT:HEAD_W + (h + 1) * SLOT] = k.astype(BF16)


def _fox_prep(ff, bias_row, fqk, batch, seq_len):
    n = fqk.shape[0]
    t = ROW_TILE
    nt = seq_len // t
    return pl.pallas_call(
        _fox_prep_kernel,
        grid=(batch, nt),
        in_specs=[pl.BlockSpec((1, t, LANES), lambda b, j: (b, j, 0)),
                  pl.BlockSpec((1, LANES), lambda b, j: (0, 0)),
                  pl.BlockSpec((t, 2 * HEAD_W), lambda b, j: (b * nt + j, 0))],
        out_specs=pl.BlockSpec((t, 2 * HEAD_W), lambda b, j: (b * nt + j, 0)),
        out_shape=jax.ShapeDtypeStruct((n, 2 * HEAD_W), BF16),
        scratch_shapes=[pltpu.VMEM((1, LANES), F32)],
        compiler_params=pltpu.CompilerParams(dimension_semantics=("arbitrary", "arbitrary")),
        name="fox_prep",
    )(ff, bias_row, fqk)


def _softmax_block(s, m_prev):
    tiles = [s[:, t * LANES:(t + 1) * LANES] for t in range(s.shape[1] // LANES)]
    row_max = functools.reduce(jnp.maximum, tiles)
    m_new = jnp.maximum(m_prev, jnp.max(row_max, axis=-1, keepdims=True))
    alpha = jnp.exp2(m_prev - m_new)
    p = jnp.concatenate([jnp.exp2(t - m_new).astype(BF16) for t in tiles], axis=1)
    return p, alpha, m_new


def _block_start(j, blk):
    return j * blk if isinstance(j, int) else pl.multiple_of(j * blk, blk)


def _causal_valid(i, j, blk):
    q_idx = i * blk + lax.broadcasted_iota(jnp.int32, (blk, blk), 0)
    k_idx = j * blk + lax.broadcasted_iota(jnp.int32, (blk, blk), 1)
    return (k_idx <= q_idx) & (k_idx >= LPAD)


def _pipelined_key_blocks(i, scores, consume, mask_first):
    scores(0, 0)

    def pair(t, carry):
        @pl.when(t == 0)
        def _():
            mask_first()

        scores(2 * t + 1, 1)
        consume(2 * t, 0, False)
        scores(2 * t + 2, 0)
        consume(2 * t + 1, 1, False)
        return carry

    lax.fori_loop(0, lax.shift_right_logical(i, 1), pair, 0)
    odd = (i & 1) == 1

    @pl.when(jnp.logical_not(odd))
    def _():
        consume(i, 0, True)

    @pl.when(odd)
    def _():
        scores(i, 1)
        consume(i - 1, 0, True)
        consume(i, 1, True)


def _head_out(acc):
    lane = lax.broadcasted_iota(jnp.int32, acc.shape, 1)
    return jnp.where(lane < ONE_LANE, acc / acc[:, ONE_LANE:ONE_LANE + 1], 0.0)


def _head_rms(o, g):
    ms = jnp.sum(o * o, axis=-1, keepdims=True) * (1.0 / ONE_LANE)
    return o * lax.rsqrt(ms + EPS) * g


def _fox_attn_kernel(q_ref, k_ref, v_ref, g_ref, o_ref, m_ref, acc_ref, sa_ref, sb_ref,
                     *, blk):
    i = pl.program_id(2)
    q = q_ref[...]
    bufs = (sa_ref, sb_ref)
    m_ref[...] = jnp.full(m_ref.shape, NEG, F32)
    acc_ref[...] = jnp.zeros_like(acc_ref)

    def scores(j, buf):
        bufs[buf][...] = lax.dot_general(q, k_ref[pl.ds(_block_start(j, blk), blk), :], _NT,
                                         preferred_element_type=F32)

    def consume(j, buf, masked):
        s = bufs[buf][...]
        if masked:
            s = jnp.where(_causal_valid(i, j, blk), s, NEG)
        p, alpha, m_new = _softmax_block(s, m_ref[...])
        acc_ref[...] = alpha * acc_ref[...] + jnp.dot(
            p, v_ref[pl.ds(_block_start(j, blk), blk), :], preferred_element_type=F32)
        m_ref[...] = m_new

    def mask_first():
        sa_ref[...] = jnp.where(_causal_valid(i, 0, blk), sa_ref[...], NEG)

    _pipelined_key_blocks(i, scores, consume, mask_first)
    o_ref[...] = _head_rms(_head_out(acc_ref[...]), g_ref[...]).astype(BF16)


def _fox_attn(fqk, fv, gain, batch, seq_len):
    n = fqk.shape[0]
    blk = ATTN_BLOCK
    nq = seq_len // blk
    return pl.pallas_call(
        functools.partial(_fox_attn_kernel, blk=blk),
        grid=(batch, FX_HEADS, nq),
        in_specs=[pl.BlockSpec((blk, SLOT), lambda b, h, i: (b * nq + i, h)),
                  pl.BlockSpec((seq_len, SLOT), lambda b, h, i: (b, FX_HEADS + h)),
                  pl.BlockSpec((seq_len, SLOT), lambda b, h, i: (b, h)),
                  pl.BlockSpec((1, SLOT), lambda b, h, i: (0, 0))],
        out_specs=pl.BlockSpec((blk, SLOT), lambda b, h, i: (b * nq + i, h)),
        out_shape=jax.ShapeDtypeStruct((n, HEAD_W), BF16),
        scratch_shapes=[pltpu.VMEM((blk, LANES), F32), pltpu.VMEM((blk, SLOT), F32),
                        pltpu.VMEM((blk, blk), F32), pltpu.VMEM((blk, blk), F32)],
        compiler_params=pltpu.CompilerParams(
            dimension_semantics=("arbitrary", "arbitrary", "arbitrary"),
            vmem_limit_bytes=VMEM_LIMIT),
        name="fox_attn",
    )(fqk, fqk, fv, gain)


def _diff_attn_kernel(q_ref, k_ref, v_ref, lam_ref, g_ref, o_ref, m_ref, acc_ref,
                      sa_ref, sb_ref, *, blk, lam_init):
    i = pl.program_id(2)
    q = q_ref[...]
    bufs = (sa_ref, sb_ref)
    m_ref[...] = jnp.full(m_ref.shape, NEG, F32)
    acc_ref[...] = jnp.zeros_like(acc_ref)

    def scores(j, buf):
        k0 = _block_start(j, blk)
        for c in range(2):
            bufs[buf][c] = lax.dot_general(
                q, k_ref[pl.ds(k0, blk), c * SLOT:(c + 1) * SLOT], _NT,
                preferred_element_type=F32)

    def consume(j, buf, masked):
        mask = _causal_valid(i, j, blk) if masked else None
        ps, alphas = [], []
        for c in range(2):
            s = bufs[buf][c]
            if masked:
                s = jnp.where(mask, s, NEG)
            p, alpha, m_new = _softmax_block(s, m_ref[c])
            m_ref[c] = m_new
            ps.append(p)
            alphas.append(alpha)
        pv = jnp.dot(jnp.concatenate(ps, axis=0),
                     v_ref[pl.ds(_block_start(j, blk), blk), :],
                     preferred_element_type=F32)
        for c in range(2):
            acc_ref[c] = alphas[c] * acc_ref[c] + pv[c * blk:(c + 1) * blk]

    def mask_first():
        mask = _causal_valid(i, 0, blk)
        for c in range(2):
            sa_ref[c] = jnp.where(mask, sa_ref[c], NEG)

    _pipelined_key_blocks(i, scores, consume, mask_first)
    lv = lam_ref[...]
    lam = (jnp.exp(jnp.sum(lv[0:1] * lv[1:2], axis=-1, keepdims=True))
           - jnp.exp(jnp.sum(lv[2:3] * lv[3:4], axis=-1, keepdims=True)) + lam_init)
    o = _head_out(acc_ref[0]) - lam * _head_out(acc_ref[1])
    o_ref[...] = (_head_rms(o, g_ref[...]) * (1.0 - lam_init)).astype(BF16)


def _diff_attn(dqk, dv, lam_vecs, gain, lam_init, batch, seq_len):
    n = dqk.shape[0]
    blk = ATTN_BLOCK
    nq = seq_len // blk
    return pl.pallas_call(
        functools.partial(_diff_attn_kernel, blk=blk, lam_init=lam_init),
        grid=(batch, DA_HEADS, nq),
        in_specs=[pl.BlockSpec((blk, SLOT), lambda b, h, i: (b * nq + i, h)),
                  pl.BlockSpec((seq_len, 2 * SLOT), lambda b, h, i: (b, 2 + h)),
                  pl.BlockSpec((seq_len, SLOT), lambda b, h, i: (b, h)),
                  pl.BlockSpec((8, LANES), lambda b, h, i: (0, 0)),
                  pl.BlockSpec((1, SLOT), lambda b, h, i: (0, 0))],
        out_specs=pl.BlockSpec((blk, SLOT), lambda b, h, i: (b * nq + i, h)),
        out_shape=jax.ShapeDtypeStruct((n, HEAD_W), BF16),
        scratch_shapes=[pltpu.VMEM((2, blk, LANES), F32), pltpu.VMEM((2, blk, SLOT), F32),
                        pltpu.VMEM((2, blk, blk), F32), pltpu.VMEM((2, blk, blk), F32)],
        compiler_params=pltpu.CompilerParams(
            dimension_semantics=("arbitrary", "arbitrary", "arbitrary"),
            vmem_limit_bytes=VMEM_LIMIT),
        name="diff_attn",
    )(dqk, dqk, dv, lam_vecs, gain)


def _hgrn_kernel(q_ref, i_ref, g_ref, f_ref, lb_ref, gain_ref, o_ref,
                 st_ref, gs_ref, ks_ref, *, tl):
    j = pl.program_id(2)

    @pl.when(j == 0)
    def _():
        st_ref[...] = jnp.zeros_like(st_ref)

    c_rows, sub = HG_CHUNK, HG_SUB
    tri = (lax.broadcasted_iota(jnp.int32, (c_rows, c_rows), 1)
           <= lax.broadcasted_iota(jnp.int32, (c_rows, c_rows), 0)).astype(F32)
    row_c = lax.broadcasted_iota(jnp.int32, (c_rows, 1), 0)
    t_sub = lax.broadcasted_iota(jnp.int32, (sub, 1), 0)
    lane_c = lax.broadcasted_iota(jnp.int32, (sub, c_rows), 1)

    def chunk(c, carry):
        for hh in range(HG_PER_STEP):
            head_chunk(c, hh)
        return carry

    def head_chunk(c, hh):
        r0 = pl.multiple_of(c * c_rows, c_rows)
        rows = pl.ds(r0, c_rows)
        cols = slice(hh * HG_DK, (hh + 1) * HG_DK)
        lb = lb_ref[:, cols]
        z = f_ref[rows, cols]
        sig = _sigmoid(z)
        log_f = jnp.log(jnp.maximum(lb + (1.0 - lb) * sig, TINY))
        kk = (1.0 - lb) * (1.0 - sig)
        valid = (j * tl + r0 + row_c) >= LPAD
        log_f = jnp.where(valid, log_f, 0.0)
        kk = jnp.where(valid, kk, 0.0)
        G = jnp.dot(tri, log_f, precision=lax.Precision.HIGHEST, preferred_element_type=F32)
        qf = q_ref[rows, cols].astype(F32)
        qs = qf * _sigmoid(qf) * (HG_DK ** -0.5)
        v = i_ref[rows, cols]
        gs_ref[hh] = G
        ks_ref[hh] = kk
        st = st_ref[hh]
        o_inter = lax.dot_general((qs * jnp.exp(G)).astype(BF16), st.astype(BF16), _NT,
                                  preferred_element_type=F32)
        outs = []
        for b in range(c_rows // sub):
            lo = b * sub
            q_b = qs[lo:lo + sub]
            g_b = G[lo:lo + sub]
            if b == 0:
                a_blk = jnp.zeros((sub, c_rows), F32)
            else:
                ref = G[lo - 1:lo]
                qa = q_b * jnp.exp(g_b - ref)
                kb = jnp.where(row_c < lo, kk * jnp.exp(jnp.minimum(ref - G, 0.0)), 0.0)
                a_blk = lax.dot_general(qa.astype(BF16), kb.astype(BF16), _NT,
                                        preferred_element_type=F32)

            for s in range(lo, lo + sub):
                g_s = gs_ref[hh, s:s + 1, :]
                k_s = ks_ref[hh, s:s + 1, :]
                y = q_b * k_s * jnp.exp(jnp.minimum(g_b - g_s, 0.0))
                a = jnp.sum(y, axis=-1, keepdims=True)
                a = jnp.where(t_sub + lo >= s, a, 0.0)
                a_blk = jnp.where(lane_c == s, a, a_blk)
            outs.append(o_inter[lo:lo + sub]
                        + jnp.dot(a_blk.astype(BF16), v, preferred_element_type=F32))
        o = jnp.concatenate(outs, axis=0)
        g_last = G[c_rows - 1:c_rows]
        kd = kk * jnp.exp(g_last - G)
        st_ref[hh] = st * jnp.exp(g_last) + lax.dot_general(
            v, kd.astype(BF16), _TN, preferred_element_type=F32)
        gate = g_ref[rows, cols].astype(F32)
        o = _rms(o, gain_ref[...]) * (gate * _sigmoid(gate))
        o_ref[rows, cols] = o.astype(BF16)

    lax.fori_loop(0, tl // c_rows, chunk, 0)


def _hgrn(hqig, hf, lb_row, gain, batch, seq_len):
    n = hqig.shape[0]
    tl = ROW_TILE
    nt = seq_len // tl
    w = HG_PER_STEP * HG_DK
    groups = HG_HEADS // HG_PER_STEP
    col = lambda off: pl.BlockSpec((tl, w), lambda b, h, j: (b * nt + j, off + h))
    return pl.pallas_call(
        functools.partial(_hgrn_kernel, tl=tl),
        grid=(batch, groups, nt),
        in_specs=[col(0), col(groups), col(2 * groups), col(0),
                  pl.BlockSpec((1, w), lambda b, h, j: (0, h)),
                  pl.BlockSpec((1, HG_DK), lambda b, h, j: (0, 0))],
        out_specs=col(0),
        out_shape=jax.ShapeDtypeStruct((n, HG_WIDTH), BF16),
        scratch_shapes=[pltpu.VMEM((HG_PER_STEP, HG_DK, HG_DK), F32),
                        pltpu.VMEM((HG_PER_STEP, HG_CHUNK, HG_DK), F32),
                        pltpu.VMEM((HG_PER_STEP, HG_CHUNK, HG_DK), F32)],
        compiler_params=pltpu.CompilerParams(
            dimension_semantics=("arbitrary", "arbitrary", "arbitrary")),
        name="hgrn2",
    )(hqig, hqig, hqig, hf, lb_row, gain)


def _out_router_kernel(oa_ref, ob_ref, oc_ref, h_ref, wo_ref, g_ref, wrh_ref, wrl_ref, br_ref,
                       h1_ref, u_ref, route_ref, cnt_ref, carry_ref):
    @pl.when(pl.program_id(0) == 0)
    def _():
        carry_ref[...] = jnp.zeros_like(carry_ref)

    h1 = (h_ref[...]
          + jnp.dot(oa_ref[...], wo_ref[0:512, :], preferred_element_type=F32)
          + jnp.dot(ob_ref[...], wo_ref[512:1024, :], preferred_element_type=F32)
          + jnp.dot(oc_ref[...], wo_ref[1024:1536, :], preferred_element_type=F32))
    h1_ref[...] = h1
    u = _rms(h1, g_ref[...])
    u_ref[...] = u
    u_hi = u.astype(BF16)
    u_lo = (u - u_hi.astype(F32)).astype(BF16)
    logits = (jnp.dot(u_hi, wrh_ref[...], preferred_element_type=F32)
              + (jnp.dot(u_hi, wrl_ref[...], preferred_element_type=F32)
                 + jnp.dot(u_lo, wrh_ref[...], preferred_element_type=F32))
              + br_ref[...])
    tm = logits.shape[0]
    lane = lax.broadcasted_iota(jnp.int32, (tm, LANES), 1).astype(F32)
    big = float(LANES)
    is_g = (lane >= N_EXPERTS) & (lane < N_EXPERTS + N_GROUPS)
    gl = jnp.where(is_g, logits, -jnp.inf)
    gmax = jnp.max(gl, axis=-1, keepdims=True)
    gsel = jnp.min(jnp.where(gl == gmax, lane, big), axis=-1, keepdims=True) - N_EXPERTS
    p_g = 1.0 / jnp.sum(jnp.exp(gl - gmax), axis=-1, keepdims=True)
    lo = gsel * EXPERTS_PER_GROUP
    el = jnp.where((lane >= lo) & (lane < lo + EXPERTS_PER_GROUP), logits, -jnp.inf)
    m1 = jnp.max(el, axis=-1, keepdims=True)
    i1 = jnp.min(jnp.where(el == m1, lane, big), axis=-1, keepdims=True)
    el2 = jnp.where(lane == i1, -jnp.inf, el)
    m2 = jnp.max(el2, axis=-1, keepdims=True)
    i2 = jnp.min(jnp.where(el2 == m2, lane, big), axis=-1, keepdims=True)
    r = jnp.exp(m2 - m1)
    gate1 = p_g / (1.0 + r)
    gate2 = gate1 * r
    oh1 = lane == i1
    oh2 = lane == i2
    onehot = jnp.where(oh1 | oh2, 1.0, 0.0)
    tri = (lax.broadcasted_iota(jnp.int32, (tm, tm), 1)
           < lax.broadcasted_iota(jnp.int32, (tm, tm), 0)).astype(BF16)
    before = jnp.dot(tri, onehot.astype(BF16), preferred_element_type=F32) + carry_ref[...]
    rank1 = jnp.sum(jnp.where(oh1, before, 0.0), axis=-1, keepdims=True)
    rank2 = jnp.sum(jnp.where(oh2, before, 0.0), axis=-1, keepdims=True)
    total = carry_ref[...] + jnp.sum(onehot, axis=0, keepdims=True)
    carry_ref[...] = total
    cnt_ref[...] = total
    route = jnp.where(lane == 0, i1, 0.0)
    for idx, val in ((1, i2), (2, gate1), (3, gate2), (4, rank1), (5, rank2)):
        route = jnp.where(lane == idx, val, route)
    route_ref[...] = route


def _out_router(oa, ob, oc, h, w_out, g, w_rt, b_rt):
    n = h.shape[0]
    w_rt_hi = w_rt.astype(BF16)
    w_rt_lo = (w_rt - w_rt_hi.astype(F32)).astype(BF16)
    tm = ROW_TILE
    row = lambda w: pl.BlockSpec((tm, w), lambda i: (i, 0))
    const = lambda r, c: pl.BlockSpec((r, c), lambda i: (0, 0))
    return pl.pallas_call(
        _out_router_kernel,
        grid=(n // tm,),
        in_specs=[row(HG_WIDTH), row(HEAD_W), row(HEAD_W), row(D_MODEL),
                  const(HG_WIDTH + 2 * HEAD_W, D_MODEL),
                  const(1, D_MODEL), const(D_MODEL, LANES), const(D_MODEL, LANES),
                  const(1, LANES)],
        out_specs=[row(D_MODEL), row(D_MODEL), row(LANES), const(1, LANES)],
        out_shape=[jax.ShapeDtypeStruct((n, D_MODEL), F32),
                   jax.ShapeDtypeStruct((n, D_MODEL), F32),
                   jax.ShapeDtypeStruct((n, LANES), F32),
                   jax.ShapeDtypeStruct((1, LANES), F32)],
        scratch_shapes=[pltpu.VMEM((1, LANES), F32)],
        compiler_params=pltpu.CompilerParams(dimension_semantics=("arbitrary",)),
        name="out_router",
    )(oa, ob, oc, h, w_out, g, w_rt_hi, w_rt_lo, b_rt)


def _row_scatter(src_ref, dst_ref, sem, r, d):
    return pltpu.make_async_copy(src_ref.at[pl.ds(r, 1)], dst_ref.at[pl.ds(d, 1)], sem)


def _dispatch_kernel(dest_ref, u_ref, xs_in_ref, xs_ref, sem, *, tm, n):
    del xs_in_ref
    base = pl.program_id(0) * tm

    def issue(r, carry):
        for k in range(2):
            _row_scatter(u_ref, xs_ref, sem, r, dest_ref[k * n + base + r]).start()
        return carry

    def drain(r, carry):
        for k in range(2):
            _row_scatter(u_ref, xs_ref, sem, r, dest_ref[k * n + base + r]).wait()
        return carry

    lax.fori_loop(0, tm, issue, 0)
    lax.fori_loop(0, tm, drain, 0)


def _dispatch(dest, u, xs_zero):
    n = u.shape[0]
    tm = ROW_TILE
    return pl.pallas_call(
        functools.partial(_dispatch_kernel, tm=tm, n=n),
        grid_spec=pltpu.PrefetchScalarGridSpec(
            num_scalar_prefetch=1,
            grid=(n // tm,),
            in_specs=[pl.BlockSpec((tm, D_MODEL), lambda i, d: (i, 0)),
                      pl.BlockSpec(memory_space=pl.ANY)],
            out_specs=pl.BlockSpec(memory_space=pl.ANY),
            scratch_shapes=[pltpu.SemaphoreType.DMA(())]),
        out_shape=jax.ShapeDtypeStruct(xs_zero.shape, xs_zero.dtype),
        input_output_aliases={2: 0},
        compiler_params=pltpu.CompilerParams(dimension_semantics=("arbitrary",)),
        name="moe_dispatch",
    )(dest, u, xs_zero)


def _expert_kernel(be_ref, na_ref, x_ref, w1_ref, w3_ref, w2_ref, y_ref,
                   w1b_ref, w3b_ref, w2b_ref):
    j = pl.program_id(0)
    active = j < na_ref[0]
    new_expert = (j == 0) | (be_ref[j] != be_ref[jnp.maximum(j - 1, 0)])

    @pl.when(active & new_expert)
    def _():
        w1b_ref[...] = w1_ref[0].astype(BF16)
        w3b_ref[...] = w3_ref[0].astype(BF16)
        w2b_ref[...] = w2_ref[0].astype(BF16)

    @pl.when(active)
    def _():
        x = x_ref[...].astype(BF16)
        a = jnp.dot(x, w1b_ref[...], preferred_element_type=F32)
        b = jnp.dot(x, w3b_ref[...], preferred_element_type=F32)
        act = (a * _sigmoid(a) * b).astype(BF16)
        y_ref[...] = jnp.dot(act, w2b_ref[...], preferred_element_type=F32)

    @pl.when(jnp.logical_not(active))
    def _():
        y_ref[...] = jnp.zeros_like(y_ref)


def _experts(blk_expert, n_active, xs, w1, w3, w2):
    p_rows = xs.shape[0]
    tb = MOE_BLOCK
    last = lambda j, na: jnp.maximum(jnp.minimum(j, na[0] - 1), 0)
    return pl.pallas_call(
        _expert_kernel,
        grid_spec=pltpu.PrefetchScalarGridSpec(
            num_scalar_prefetch=2,
            grid=(p_rows // tb,),
            in_specs=[pl.BlockSpec((tb, D_MODEL), lambda j, be, na: (last(j, na), 0)),
                      pl.BlockSpec((1, D_MODEL, D_EXPERT), lambda j, be, na: (be[j], 0, 0)),
                      pl.BlockSpec((1, D_MODEL, D_EXPERT), lambda j, be, na: (be[j], 0, 0)),
                      pl.BlockSpec((1, D_EXPERT, D_MODEL), lambda j, be, na: (be[j], 0, 0))],
            out_specs=pl.BlockSpec((tb, D_MODEL), lambda j, be, na: (j, 0)),
            scratch_shapes=[pltpu.VMEM((D_MODEL, D_EXPERT), BF16),
                            pltpu.VMEM((D_MODEL, D_EXPERT), BF16),
                            pltpu.VMEM((D_EXPERT, D_MODEL), BF16)]),
        out_shape=jax.ShapeDtypeStruct((p_rows, D_MODEL), F32),
        compiler_params=pltpu.CompilerParams(dimension_semantics=("arbitrary",),
                                             vmem_limit_bytes=VMEM_LIMIT),
        name="moe_experts",
    )(blk_expert, n_active, xs, w1, w3, w2)


def _row_gather(src_ref, dst_ref, sem, d, r):
    return pltpu.make_async_copy(src_ref.at[pl.ds(d, 1)], dst_ref.at[pl.ds(r, 1)], sem)


def _combine_kernel(dest_ref, h_ref, route_ref, gfin_ref, ys_ref, o_ref,
                    y0_ref, y1_ref, sem, *, tm, n, tiles_per_seq, skip_tiles, final):
    base = (pl.program_id(0) * tiles_per_seq + skip_tiles + pl.program_id(1)) * tm
    bufs = (y0_ref, y1_ref)

    def issue(r, carry):
        for k in range(2):
            _row_gather(ys_ref, bufs[k], sem, dest_ref[k * n + base + r], r).start()
        return carry

    def drain(r, carry):
        for k in range(2):
            _row_gather(ys_ref, bufs[k], sem, dest_ref[k * n + base + r], r).wait()
        return carry

    lax.fori_loop(0, tm, issue, 0)
    lax.fori_loop(0, tm, drain, 0)
    route = route_ref[...]
    out = h_ref[...] + route[:, 2:3] * y0_ref[...] + route[:, 3:4] * y1_ref[...]
    if final:
        out = _rms(out, gfin_ref[...])
    o_ref[...] = out


def _combine(dest, h1, route, g_final, ys, batch, seq_len, final):
    n = h1.shape[0]
    tm = ROW_TILE
    tiles_per_seq = seq_len // tm
    skip_tiles = (LPAD + N_META) // tm if final else 0
    out_tiles = tiles_per_seq - skip_tiles
    in_row = lambda w: pl.BlockSpec(
        (tm, w), lambda b, i, d: (b * tiles_per_seq + skip_tiles + i, 0))
    return pl.pallas_call(
        functools.partial(_combine_kernel, tm=tm, n=n, tiles_per_seq=tiles_per_seq,
                          skip_tiles=skip_tiles, final=final),
        grid_spec=pltpu.PrefetchScalarGridSpec(
            num_scalar_prefetch=1,
            grid=(batch, out_tiles),
            in_specs=[in_row(D_MODEL), in_row(LANES),
                      pl.BlockSpec((1, D_MODEL), lambda b, i, d: (0, 0)),
                      pl.BlockSpec(memory_space=pl.ANY)],
            out_specs=pl.BlockSpec((tm, D_MODEL), lambda b, i, d: (b * out_tiles + i, 0)),
            scratch_shapes=[pltpu.VMEM((tm, D_MODEL), F32), pltpu.VMEM((tm, D_MODEL), F32),
                            pltpu.SemaphoreType.DMA(())]),
        out_shape=jax.ShapeDtypeStruct((batch * out_tiles * tm, D_MODEL), F32),
        compiler_params=pltpu.CompilerParams(dimension_semantics=("arbitrary", "arbitrary")),
        name="moe_combine",
    )(dest, h1, route, g_final, ys)


def _slot_cols():
    src = jnp.arange(4 * 64)
    return (src // 64) * SLOT + src % 64


def _diff_k_cols():
    src = jnp.arange(DA_WIDTH)
    h, c, j = src // DA_DV, (src % DA_DV) // DA_DQK, src % DA_DQK
    return h * 2 * SLOT + c * SLOT + c * DA_DQK + j


def _spread(x, cols, width):
    return jnp.zeros(x.shape[:-1] + (width,), x.dtype).at[..., cols].set(x)


def _rope_tables(seq_len):
    pos = (jnp.arange(seq_len) - LPAD).astype(F32)
    col = jnp.arange(DA_WIDTH)
    inv = ROPE_THETA ** (-(2.0 * (col % (DA_DQK // 2))).astype(F32) / DA_DQK)
    ang = pos[:, None] * inv[None, :]
    first = (col % DA_DQK) < DA_DQK // 2
    cos = jnp.cos(ang)
    sin = jnp.where(first[None, :], -jnp.sin(ang), jnp.sin(ang))
    q_scale = DA_DQK ** -0.5 * LOG2E
    lay = lambda t: jnp.concatenate(
        [_spread(t * q_scale, _slot_cols(), HEAD_W), _spread(t, _diff_k_cols(), 2 * HEAD_W)],
        axis=1)
    return lay(cos), lay(sin)


def _in_weights(w):
    hq, hf, hi, hg = (w[:, i * 512:(i + 1) * 512] for i in range(4))
    dq, dk, dv = (w[:, 2048 + i * 256:2048 + (i + 1) * 256] for i in range(3))
    fq, fk, fv = (w[:, 2816 + i * 256:2816 + (i + 1) * 256] for i in range(3))
    ff = w[:, 3584:3588]
    col = jnp.arange(DA_WIDTH)
    half = DA_DQK // 2
    partner = jnp.where((col % DA_DQK) < half, col + half, col - half)
    slots = lambda m: _spread(m, _slot_cols(), HEAD_W)
    kslots = lambda m: _spread(m, _diff_k_cols(), 2 * HEAD_W)
    cat = jnp.concatenate(
        [hq, hi, hg, hf,
         slots(dq), kslots(dk), slots(dq[:, partner]), kslots(dk[:, partner]),
         slots(dv), slots(fq * (FX_DH ** -0.5)), slots(fk), slots(fv),
         ff, jnp.zeros((D_MODEL, LANES - FX_HEADS), w.dtype)],
        axis=1)
    return cat.astype(BF16)


def _pad_lanes(v, width=LANES):
    return jnp.zeros((1, width), F32).at[0, :v.shape[0]].set(v.astype(F32))


def kernel(x, meta_tokens, norm_mix, w_in, hgrn_lb, hgrn_norm, diff_lambda, diff_norm,
           fox_bias, fox_norm, w_out, norm_ffn, w_group, b_group, w_router, b_router,
           w1, w3, w2, norm_final):
    batch, seq, d = x.shape
    depth = w_in.shape[0]
    seq_len = LPAD + N_META + seq
    n = batch * seq_len
    pad = jnp.zeros((batch, LPAD, d), x.dtype)
    meta = jnp.broadcast_to(meta_tokens.astype(x.dtype)[None], (batch, N_META, d))
    h = jnp.concatenate([pad, meta, x], axis=1).reshape(n, d)

    s_lb = jax.nn.softmax(hgrn_lb.astype(F32), axis=0)
    lb_all = jnp.cumsum(s_lb, axis=0) - s_lb[0]
    cos_t, sin_t = _rope_tables(seq_len)

    tb = MOE_BLOCK
    n_blocks = (2 * n) // tb + N_EXPERTS
    p_rows = n_blocks * tb

    for layer in range(depth):
        lam_init = 0.8 - 0.6 * math.exp(-0.3 * layer)
        hqig, hf, dqk, dv, fqk, fv, ff = _in_proj(
            h, norm_mix[layer][None, :], _in_weights(w_in[layer]), cos_t, sin_t, seq_len)

        o_a = _hgrn(hqig, hf, lb_all[layer][None, :], hgrn_norm[layer][None, :],
                    batch, seq_len)

        lam_vecs = jnp.zeros((8, LANES), F32).at[:4, :DA_DQK].set(diff_lambda[layer].astype(F32))
        o_b = _diff_attn(dqk, dv, lam_vecs, _pad_lanes(diff_norm[layer]), lam_init,
                         batch, seq_len)

        fqk_aug = _fox_prep(ff.reshape(batch, seq_len, LANES), _pad_lanes(fox_bias[layer]),
                            fqk, batch, seq_len)
        o_c = _fox_attn(fqk_aug, fv, _pad_lanes(fox_norm[layer]), batch, seq_len)

        wo = w_out[layer]
        slot_rows = lambda m: _spread(m.T, _slot_cols(), HEAD_W).T
        wo_cat = jnp.concatenate(
            [wo[:HG_WIDTH], slot_rows(wo[HG_WIDTH:HG_WIDTH + DA_WIDTH]),
             slot_rows(wo[HG_WIDTH + DA_WIDTH:])], axis=0).astype(BF16)
        w_rt = jnp.concatenate(
            [w_router[layer], w_group[layer],
             jnp.zeros((d, LANES - N_EXPERTS - N_GROUPS), F32)], axis=1)
        b_rt = _pad_lanes(jnp.concatenate([b_router[layer], b_group[layer]]))
        h1, u2, route, counts = _out_router(
            o_a, o_b, o_c, h, wo_cat, norm_ffn[layer][None, :], w_rt, b_rt)

        cnt = counts[0, :N_EXPERTS].astype(jnp.int32)
        padded = (cnt + tb - 1) // tb * tb
        p_end = jnp.cumsum(padded)
        p_start = p_end - padded
        ids = route[:, 0:2].astype(jnp.int32)
        ranks = route[:, 4:6].astype(jnp.int32)
        dest = (p_start[ids] + ranks).T.reshape(2 * n)
        blk_start = jnp.arange(n_blocks, dtype=jnp.int32) * tb
        blk_expert = jnp.minimum(
            jnp.sum((p_end[None, :] <= blk_start[:, None]).astype(jnp.int32), axis=1),
            N_EXPERTS - 1)
        n_active = (p_end[-1:] // tb).astype(jnp.int32)

        xs = _dispatch(dest, u2, jnp.zeros((p_rows, d), F32))
        ys = _experts(blk_expert, n_active, xs, w1[layer], w3[layer], w2[layer])
        h = _combine(dest, h1, route, norm_final[None, :], ys, batch, seq_len,
                     final=(layer == depth - 1))

    return h.reshape(batch, seq, d)
```

```python
import functools
import math

import jax
import jax.numpy as jnp
from jax import lax
from jax.experimental import pallas as pl
from jax.experimental.pallas import tpu as pltpu

F32 = jnp.float32
BF16 = jnp.bfloat16

D_MODEL = 1024
N_META = 16
HG_HEADS = 4
HG_DK = 128
HG_WIDTH = 512
HG_CHUNK = 64
HG_SUB = 16
HG_PER_STEP = 2
DA_HEADS = 4
DA_DV = 64
DA_DQK = 32
DA_WIDTH = 256
FX_HEADS = 4
FX_DH = 64
FX_WIDTH = 256
N_GROUPS = 4
EXPERTS_PER_GROUP = 8
N_EXPERTS = 32
D_EXPERT = 512
ROPE_THETA = 10000.0
EPS = 1e-6
NEG = -1e30
TINY = 1e-30
LOG2E = 1.4426950408889634

LANES = 128
SLOT = LANES
HEAD_W = 4 * SLOT
ROW_TILE = 256
PROJ_TILE = 512
ATTN_BLOCK = 512
MOE_BLOCK = 256
LPAD = ATTN_BLOCK - N_META
ONE_LANE = 64
VMEM_LIMIT = 56 * 1024 * 1024

_C_HQIG = (0, 1536)
_C_HF = (1536, 2048)
_C_DQK = (2048, 3584)
_C_DQKP = (3584, 5120)
_C_DV = (5120, 5632)
_C_FQ = (5632, 6144)
_C_FK = (6144, 6656)
_C_FV = (6656, 7168)
_C_FF = (7168, 7296)
IN_COLS = 7296

_NT = (((1,), (1,)), ((), ()))
_TN = (((0,), (0,)), ((), ()))


def _sigmoid(x):
    return 1.0 / (1.0 + jnp.exp(-x))


def _rms(x, g):
    return x * lax.rsqrt(jnp.mean(x * x, axis=-1, keepdims=True) + EPS) * g


def _slot_lane(shape):
    return lax.broadcasted_iota(jnp.int32, shape, 1) % SLOT


def _in_proj_kernel(h_ref, g_ref, w_ref, cos_ref, sin_ref,
                    hqig_ref, hf_ref, dqk_ref, dv_ref, fqk_ref, fv_ref, ff_ref):
    u = _rms(h_ref[...], g_ref[...]).astype(BF16)

    def mm(c):
        return jnp.dot(u, w_ref[:, c[0]:c[1]], preferred_element_type=F32)

    def with_ones(v):
        return jnp.where(_slot_lane(v.shape) == ONE_LANE, 1.0, v)

    hqig_ref[...] = mm(_C_HQIG).astype(BF16)
    hf_ref[...] = mm(_C_HF)
    dqk_ref[...] = (mm(_C_DQK) * cos_ref[...] + mm(_C_DQKP) * sin_ref[...]).astype(BF16)
    dv_ref[...] = with_ones(mm(_C_DV)).astype(BF16)
    fqk_ref[:, 0:HEAD_W] = (mm(_C_FQ) * LOG2E).astype(BF16)
    fqk_ref[:, HEAD_W:2 * HEAD_W] = mm(_C_FK).astype(BF16)
    fv_ref[...] = with_ones(mm(_C_FV)).astype(BF16)
    ff_ref[...] = mm(_C_FF)


def _in_proj(h, g, w_cat, cos_t, sin_t, seq_len):
    n = h.shape[0]
    tm = PROJ_TILE
    n_seq_tiles = seq_len // tm
    row = lambda w: pl.BlockSpec((tm, w), lambda i: (i, 0))
    tab = pl.BlockSpec((tm, 1536), lambda i: (i % n_seq_tiles, 0))
    widths = (1536, 512, 1536, HEAD_W, 2 * HEAD_W, HEAD_W, LANES)
    dtypes = (BF16, F32, BF16, BF16, BF16, BF16, F32)
    return pl.pallas_call(
        _in_proj_kernel,
        grid=(n // tm,),
        in_specs=[row(D_MODEL),
                  pl.BlockSpec((1, D_MODEL), lambda i: (0, 0)),
                  pl.BlockSpec((D_MODEL, IN_COLS), lambda i: (0, 0),
                               pipeline_mode=pl.Buffered(1)),
                  tab, tab],
        out_specs=[row(w) for w in widths],
        out_shape=[jax.ShapeDtypeStruct((n, w), t) for w, t in zip(widths, dtypes)],
        compiler_params=pltpu.CompilerParams(dimension_semantics=("arbitrary",),
                                             vmem_limit_bytes=VMEM_LIMIT),
        name="in_proj",
    )(h, g, w_cat, cos_t, sin_t)


def _bf16_split3(x):
    hi = x.astype(BF16).astype(F32)
    r = x - hi
    mid = r.astype(BF16).astype(F32)
    return hi, mid, r - mid


def _fox_prep_kernel(ff_ref, b_ref, qk_ref, o_ref, carry_ref):
    @pl.when(pl.program_id(1) == 0)
    def _():
        carry_ref[...] = jnp.zeros_like(carry_ref)

    x = ff_ref[0] + b_ref[...]
    lf = jnp.minimum(x, 0.0) - jnp.log(1.0 + jnp.exp(-jnp.abs(x)))
    t = x.shape[0]
    tri = (lax.broadcasted_iota(jnp.int32, (t, t), 1)
           <= lax.broadcasted_iota(jnp.int32, (t, t), 0)).astype(F32)
    cs = jnp.dot(tri, lf, precision=lax.Precision.HIGHEST,
                 preferred_element_type=F32) + carry_ref[...]
    carry_ref[...] = cs[t - 1:t, :]
    parts = _bf16_split3(cs * LOG2E)
    lane = lax.broadcasted_iota(jnp.int32, (t, SLOT), 1)
    d0 = FX_DH
    for h in range(FX_HEADS):
        q = qk_ref[:, h * SLOT:(h + 1) * SLOT].astype(F32)
        k = qk_ref[:, HEAD_W + h * SLOT:HEAD_W + (h + 1) * SLOT].astype(F32)
        for i, part in enumerate(parts):
            col = part[:, h:h + 1]
            q = jnp.where(lane == d0 + i, col, q)
            k = jnp.where(lane == d0 + 3 + i, -col, k)
        q = jnp.where((lane >= d0 + 3) & (lane < d0 + 6), 1.0, q)
        k = jnp.where((lane >= d0) & (lane < d0 + 3), 1.0, k)
        o_ref[:, h * SLOT:(h + 1) * SLOT] = q.astype(BF16)
        o_ref[:, HEAD_W + h * SLOT:HEAD_W + (h + 1) * SLOT] = k.astype(BF16)


def _fox_prep(ff, bias_row, fqk, batch, seq_len):
    n = fqk.shape[0]
    t = ROW_TILE
    nt = seq_len // t
    return pl.pallas_call(
        _fox_prep_kernel,
        grid=(batch, nt),
        in_specs=[pl.BlockSpec((1, t, LANES), lambda b, j: (b, j, 0)),
                  pl.BlockSpec((1, LANES), lambda b, j: (0, 0)),
                  pl.BlockSpec((t, 2 * HEAD_W), lambda b, j: (b * nt + j, 0))],
        out_specs=pl.BlockSpec((t, 2 * HEAD_W), lambda b, j: (b * nt + j, 0)),
        out_shape=jax.ShapeDtypeStruct((n, 2 * HEAD_W), BF16),
        scratch_shapes=[pltpu.VMEM((1, LANES), F32)],
        compiler_params=pltpu.CompilerParams(dimension_semantics=("arbitrary", "arbitrary")),
        name="fox_prep",
    )(ff, bias_row, fqk)


def _softmax_block(s, m_prev):
    m_new = jnp.maximum(m_prev, jnp.max(s, axis=0, keepdims=True))
    alpha = jnp.exp2(m_prev - m_new)
    p = jnp.exp2(s - m_new).astype(BF16)
    return p, alpha, m_new


def _block_start(j, blk):
    return j * blk if isinstance(j, int) else pl.multiple_of(j * blk, blk)


def _causal_valid(i, j, blk):
    k_idx = j * blk + lax.broadcasted_iota(jnp.int32, (blk, blk), 0)
    q_idx = i * blk + lax.broadcasted_iota(jnp.int32, (blk, blk), 1)
    return (k_idx <= q_idx) & (k_idx >= LPAD)


def _pipelined_key_blocks(i, scores, consume, mask_first):
    scores(0, 0)

    def pair(t, carry):
        @pl.when(t == 0)
        def _():
            mask_first()

        scores(2 * t + 1, 1)
        consume(2 * t, 0, False)
        scores(2 * t + 2, 0)
        consume(2 * t + 1, 1, False)
        return carry

    lax.fori_loop(0, lax.shift_right_logical(i, 1), pair, 0)
    odd = (i & 1) == 1

    @pl.when(jnp.logical_not(odd))
    def _():
        consume(i, 0, True)

    @pl.when(odd)
    def _():
        scores(i, 1)
        consume(i - 1, 0, True)
        consume(i, 1, True)


def _head_out(acc):
    lane = lax.broadcasted_iota(jnp.int32, acc.shape, 1)
    return jnp.where(lane < ONE_LANE, acc / acc[:, ONE_LANE:ONE_LANE + 1], 0.0)


def _head_rms(o, g):
    ms = jnp.sum(o * o, axis=-1, keepdims=True) * (1.0 / ONE_LANE)
    return o * lax.rsqrt(ms + EPS) * g


def _fox_attn_kernel(q_ref, k_ref, vt_ref, g_ref, o_ref, m_ref, acc_ref, sa_ref, sb_ref,
                     *, blk):
    i = pl.program_id(2)
    q = q_ref[...]
    bufs = (sa_ref, sb_ref)
    m_ref[...] = jnp.full(m_ref.shape, NEG, F32)
    acc_ref[...] = jnp.zeros_like(acc_ref)

    def scores(j, buf):
        bufs[buf][...] = lax.dot_general(k_ref[pl.ds(_block_start(j, blk), blk), :], q, _NT,
                                         preferred_element_type=F32)

    def consume(j, buf, masked):
        s = bufs[buf][...]
        if masked:
            s = jnp.where(_causal_valid(i, j, blk), s, NEG)
        p, alpha, m_new = _softmax_block(s, m_ref[...])
        acc_ref[...] = alpha * acc_ref[...] + jnp.dot(
            vt_ref[0, 0, j], p, preferred_element_type=F32)
        m_ref[...] = m_new

    def mask_first():
        sa_ref[...] = jnp.where(_causal_valid(i, 0, blk), sa_ref[...], NEG)

    _pipelined_key_blocks(i, scores, consume, mask_first)
    o_ref[...] = _head_rms(_head_out(acc_ref[...].T), g_ref[...]).astype(BF16)


def _value_blocks_t(v, batch, seq_len):
    nk = seq_len // ATTN_BLOCK
    return v.reshape(batch, nk, ATTN_BLOCK, 4, SLOT).transpose(0, 3, 1, 4, 2)


def _value_spec(seq_len):
    nk = seq_len // ATTN_BLOCK
    return pl.BlockSpec((1, 1, nk, SLOT, ATTN_BLOCK), lambda b, h, i: (b, h, 0, 0, 0))


def _fox_attn(fqk, fv, gain, batch, seq_len):
    n = fqk.shape[0]
    blk = ATTN_BLOCK
    nq = seq_len // blk
    return pl.pallas_call(
        functools.partial(_fox_attn_kernel, blk=blk),
        grid=(batch, FX_HEADS, nq),
        in_specs=[pl.BlockSpec((blk, SLOT), lambda b, h, i: (b * nq + i, h)),
                  pl.BlockSpec((seq_len, SLOT), lambda b, h, i: (b, FX_HEADS + h)),
                  _value_spec(seq_len),
                  pl.BlockSpec((1, SLOT), lambda b, h, i: (0, 0))],
        out_specs=pl.BlockSpec((blk, SLOT), lambda b, h, i: (b * nq + i, h)),
        out_shape=jax.ShapeDtypeStruct((n, HEAD_W), BF16),
        scratch_shapes=[pltpu.VMEM((1, blk), F32), pltpu.VMEM((SLOT, blk), F32),
                        pltpu.VMEM((blk, blk), F32), pltpu.VMEM((blk, blk), F32)],
        compiler_params=pltpu.CompilerParams(
            dimension_semantics=("arbitrary", "arbitrary", "arbitrary"),
            vmem_limit_bytes=VMEM_LIMIT),
        name="fox_attn",
    )(fqk, fqk, _value_blocks_t(fv, batch, seq_len), gain)


def _diff_attn_kernel(q_ref, k_ref, vt_ref, lam_ref, g_ref, o_ref, m_ref, acc_ref,
                      sa_ref, sb_ref, *, blk, lam_init):
    i = pl.program_id(2)
    q = q_ref[...]
    bufs = (sa_ref, sb_ref)
    m_ref[...] = jnp.full(m_ref.shape, NEG, F32)
    acc_ref[...] = jnp.zeros_like(acc_ref)

    def scores(j, buf):
        k0 = _block_start(j, blk)
        for c in range(2):
            bufs[buf][c] = lax.dot_general(
                k_ref[pl.ds(k0, blk), c * SLOT:(c + 1) * SLOT], q, _NT,
                preferred_element_type=F32)

    def consume(j, buf, masked):
        mask = _causal_valid(i, j, blk) if masked else None
        ps, alphas = [], []
        for c in range(2):
            s = bufs[buf][c]
            if masked:
                s = jnp.where(mask, s, NEG)
            p, alpha, m_new = _softmax_block(s, m_ref[c])
            m_ref[c] = m_new
            ps.append(p)
            alphas.append(alpha)
        pv = jnp.dot(vt_ref[0, 0, j], jnp.concatenate(ps, axis=1),
                     preferred_element_type=F32)
        for c in range(2):
            acc_ref[c] = alphas[c] * acc_ref[c] + pv[:, c * blk:(c + 1) * blk]

    def mask_first():
        mask = _causal_valid(i, 0, blk)
        for c in range(2):
            sa_ref[c] = jnp.where(mask, sa_ref[c], NEG)

    _pipelined_key_blocks(i, scores, consume, mask_first)
    lv = lam_ref[...]
    lam = (jnp.exp(jnp.sum(lv[0:1] * lv[1:2], axis=-1, keepdims=True))
           - jnp.exp(jnp.sum(lv[2:3] * lv[3:4], axis=-1, keepdims=True)) + lam_init)
    o = _head_out(acc_ref[0].T) - lam * _head_out(acc_ref[1].T)
    o_ref[...] = (_head_rms(o, g_ref[...]) * (1.0 - lam_init)).astype(BF16)


def _diff_attn(dqk, dv, lam_vecs, gain, lam_init, batch, seq_len):
    n = dqk.shape[0]
    blk = ATTN_BLOCK
    nq = seq_len // blk
    return pl.pallas_call(
        functools.partial(_diff_attn_kernel, blk=blk, lam_init=lam_init),
        grid=(batch, DA_HEADS, nq),
        in_specs=[pl.BlockSpec((blk, SLOT), lambda b, h, i: (b * nq + i, h)),
                  pl.BlockSpec((seq_len, 2 * SLOT), lambda b, h, i: (b, 2 + h)),
                  _value_spec(seq_len),
                  pl.BlockSpec((8, LANES), lambda b, h, i: (0, 0)),
                  pl.BlockSpec((1, SLOT), lambda b, h, i: (0, 0))],
        out_specs=pl.BlockSpec((blk, SLOT), lambda b, h, i: (b * nq + i, h)),
        out_shape=jax.ShapeDtypeStruct((n, HEAD_W), BF16),
        scratch_shapes=[pltpu.VMEM((2, 1, blk), F32), pltpu.VMEM((2, SLOT, blk), F32),
                        pltpu.VMEM((2, blk, blk), F32), pltpu.VMEM((2, blk, blk), F32)],
        compiler_params=pltpu.CompilerParams(
            dimension_semantics=("arbitrary", "arbitrary", "arbitrary"),
            vmem_limit_bytes=VMEM_LIMIT),
        name="diff_attn",
    )(dqk, dqk, _value_blocks_t(dv, batch, seq_len), lam_vecs, gain)


def _hgrn_kernel(q_ref, i_ref, g_ref, f_ref, lb_ref, gain_ref, o_ref,
                 st_ref, gs_ref, ks_ref, *, tl):
    j = pl.program_id(2)

    @pl.when(j == 0)
    def _():
        st_ref[...] = jnp.zeros_like(st_ref)

    c_rows, sub = HG_CHUNK, HG_SUB
    tri = (lax.broadcasted_iota(jnp.int32, (c_rows, c_rows), 1)
           <= lax.broadcasted_iota(jnp.int32, (c_rows, c_rows), 0)).astype(F32)
    row_c = lax.broadcasted_iota(jnp.int32, (c_rows, 1), 0)
    t_sub = lax.broadcasted_iota(jnp.int32, (sub, 1), 0)
    lane_c = lax.broadcasted_iota(jnp.int32, (sub, c_rows), 1)

    def chunk(c, carry):
        for hh in range(HG_PER_STEP):
            head_chunk(c, hh)
        return carry

    def head_chunk(c, hh):
        r0 = pl.multiple_of(c * c_rows, c_rows)
        rows = pl.ds(r0, c_rows)
        cols = slice(hh * HG_DK, (hh + 1) * HG_DK)
        lb = lb_ref[:, cols]
        z = f_ref[rows, cols]
        sig = _sigmoid(z)
        log_f = jnp.log(jnp.maximum(lb + (1.0 - lb) * sig, TINY))
        kk = (1.0 - lb) * (1.0 - sig)
        valid = (j * tl + r0 + row_c) >= LPAD
        log_f = jnp.where(valid, log_f, 0.0)
        kk = jnp.where(valid, kk, 0.0)
        G = jnp.dot(tri, log_f, precision=lax.Precision.HIGHEST, preferred_element_type=F32)
        qf = q_ref[rows, cols].astype(F32)
        qs = qf * _sigmoid(qf) * (HG_DK ** -0.5)
        v = i_ref[rows, cols]
        gs_ref[hh] = G
        ks_ref[hh] = kk
        st = st_ref[hh]
        o_inter = lax.dot_general((qs * jnp.exp(G)).astype(BF16), st.astype(BF16), _NT,
                                  preferred_element_type=F32)
        outs = []
        for b in range(c_rows // sub):
            lo = b * sub
            q_b = qs[lo:lo + sub]
            g_b = G[lo:lo + sub]
            if b == 0:
                a_blk = jnp.zeros((sub, c_rows), F32)
            else:
                ref = G[lo - 1:lo]
                qa = q_b * jnp.exp(g_b - ref)
                kb = jnp.where(row_c < lo, kk * jnp.exp(jnp.minimum(ref - G, 0.0)), 0.0)
                a_blk = lax.dot_general(qa.astype(BF16), kb.astype(BF16), _NT,
                                        preferred_element_type=F32)

            for s in range(lo, lo + sub):
                g_s = gs_ref[hh, s:s + 1, :]
                k_s = ks_ref[hh, s:s + 1, :]
                y = q_b * k_s * jnp.exp(jnp.minimum(g_b - g_s, 0.0))
                a = jnp.sum(y, axis=-1, keepdims=True)
                a = jnp.where(t_sub + lo >= s, a, 0.0)
                a_blk = jnp.where(lane_c == s, a, a_blk)
            outs.append(o_inter[lo:lo + sub]
                        + jnp.dot(a_blk.astype(BF16), v, preferred_element_type=F32))
        o = jnp.concatenate(outs, axis=0)
        g_last = G[c_rows - 1:c_rows]
        kd = kk * jnp.exp(g_last - G)
        st_ref[hh] = st * jnp.exp(g_last) + lax.dot_general(
            v, kd.astype(BF16), _TN, preferred_element_type=F32)
        gate = g_ref[rows, cols].astype(F32)
        o = _rms(o, gain_ref[...]) * (gate * _sigmoid(gate))
        o_ref[rows, cols] = o.astype(BF16)

    lax.fori_loop(0, tl // c_rows, chunk, 0)


def _hgrn(hqig, hf, lb_row, gain, batch, seq_len):
    n = hqig.shape[0]
    tl = ROW_TILE
    nt = seq_len // tl
    w = HG_PER_STEP * HG_DK
    groups = HG_HEADS // HG_PER_STEP
    col = lambda off: pl.BlockSpec((tl, w), lambda b, h, j: (b * nt + j, off + h))
    return pl.pallas_call(
        functools.partial(_hgrn_kernel, tl=tl),
        grid=(batch, groups, nt),
        in_specs=[col(0), col(groups), col(2 * groups), col(0),
                  pl.BlockSpec((1, w), lambda b, h, j: (0, h)),
                  pl.BlockSpec((1, HG_DK), lambda b, h, j: (0, 0))],
        out_specs=col(0),
        out_shape=jax.ShapeDtypeStruct((n, HG_WIDTH), BF16),
        scratch_shapes=[pltpu.VMEM((HG_PER_STEP, HG_DK, HG_DK), F32),
                        pltpu.VMEM((HG_PER_STEP, HG_CHUNK, HG_DK), F32),
                        pltpu.VMEM((HG_PER_STEP, HG_CHUNK, HG_DK), F32)],
        compiler_params=pltpu.CompilerParams(
            dimension_semantics=("arbitrary", "arbitrary", "arbitrary")),
        name="hgrn2",
    )(hqig, hqig, hqig, hf, lb_row, gain)


def _out_router_kernel(oa_ref, ob_ref, oc_ref, h_ref, wo_ref, g_ref, wrh_ref, wrl_ref, br_ref,
                       h1_ref, u_ref, route_ref, cnt_ref, carry_ref):
    @pl.when(pl.program_id(0) == 0)
    def _():
        carry_ref[...] = jnp.zeros_like(carry_ref)

    h1 = (h_ref[...]
          + jnp.dot(oa_ref[...], wo_ref[0:512, :], preferred_element_type=F32)
          + jnp.dot(ob_ref[...], wo_ref[512:1024, :], preferred_element_type=F32)
          + jnp.dot(oc_ref[...], wo_ref[1024:1536, :], preferred_element_type=F32))
    h1_ref[...] = h1
    u = _rms(h1, g_ref[...])
    u_ref[...] = u
    u_hi = u.astype(BF16)
    u_lo = (u - u_hi.astype(F32)).astype(BF16)
    logits = (jnp.dot(u_hi, wrh_ref[...], preferred_element_type=F32)
              + (jnp.dot(u_hi, wrl_ref[...], preferred_element_type=F32)
                 + jnp.dot(u_lo, wrh_ref[...], preferred_element_type=F32))
              + br_ref[...])
    tm = logits.shape[0]
    lane = lax.broadcasted_iota(jnp.int32, (tm, LANES), 1).astype(F32)
    big = float(LANES)
    is_g = (lane >= N_EXPERTS) & (lane < N_EXPERTS + N_GROUPS)
    gl = jnp.where(is_g, logits, -jnp.inf)
    gmax = jnp.max(gl, axis=-1, keepdims=True)
    gsel = jnp.min(jnp.where(gl == gmax, lane, big), axis=-1, keepdims=True) - N_EXPERTS
    p_g = 1.0 / jnp.sum(jnp.exp(gl - gmax), axis=-1, keepdims=True)
    lo = gsel * EXPERTS_PER_GROUP
    el = jnp.where((lane >= lo) & (lane < lo + EXPERTS_PER_GROUP), logits, -jnp.inf)
    m1 = jnp.max(el, axis=-1, keepdims=True)
    i1 = jnp.min(jnp.where(el == m1, lane, big), axis=-1, keepdims=True)
    el2 = jnp.where(lane == i1, -jnp.inf, el)
    m2 = jnp.max(el2, axis=-1, keepdims=True)
    i2 = jnp.min(jnp.where(el2 == m2, lane, big), axis=-1, keepdims=True)
    r = jnp.exp(m2 - m1)
    gate1 = p_g / (1.0 + r)
    gate2 = gate1 * r
    oh1 = lane == i1
    oh2 = lane == i2
    onehot = jnp.where(oh1 | oh2, 1.0, 0.0)
    tri = (lax.broadcasted_iota(jnp.int32, (tm, tm), 1)
           < lax.broadcasted_iota(jnp.int32, (tm, tm), 0)).astype(BF16)
    before = jnp.dot(tri, onehot.astype(BF16), preferred_element_type=F32) + carry_ref[...]
    rank1 = jnp.sum(jnp.where(oh1, before, 0.0), axis=-1, keepdims=True)
    rank2 = jnp.sum(jnp.where(oh2, before, 0.0), axis=-1, keepdims=True)
    total = carry_ref[...] + jnp.sum(onehot, axis=0, keepdims=True)
    carry_ref[...] = total
    cnt_ref[...] = total
    route = jnp.where(lane == 0, i1, 0.0)
    for idx, val in ((1, i2), (2, gate1), (3, gate2), (4, rank1), (5, rank2)):
        route = jnp.where(lane == idx, val, route)
    route_ref[...] = route


def _out_router(oa, ob, oc, h, w_out, g, w_rt, b_rt):
    n = h.shape[0]
    w_rt_hi = w_rt.astype(BF16)
    w_rt_lo = (w_rt - w_rt_hi.astype(F32)).astype(BF16)
    tm = ROW_TILE
    row = lambda w: pl.BlockSpec((tm, w), lambda i: (i, 0))
    const = lambda r, c: pl.BlockSpec((r, c), lambda i: (0, 0))
    return pl.pallas_call(
        _out_router_kernel,
        grid=(n // tm,),
        in_specs=[row(HG_WIDTH), row(HEAD_W), row(HEAD_W), row(D_MODEL),
                  const(HG_WIDTH + 2 * HEAD_W, D_MODEL),
                  const(1, D_MODEL), const(D_MODEL, LANES), const(D_MODEL, LANES),
                  const(1, LANES)],
        out_specs=[row(D_MODEL), row(D_MODEL), row(LANES), const(1, LANES)],
        out_shape=[jax.ShapeDtypeStruct((n, D_MODEL), F32),
                   jax.ShapeDtypeStruct((n, D_MODEL), F32),
                   jax.ShapeDtypeStruct((n, LANES), F32),
                   jax.ShapeDtypeStruct((1, LANES), F32)],
        scratch_shapes=[pltpu.VMEM((1, LANES), F32)],
        compiler_params=pltpu.CompilerParams(dimension_semantics=("arbitrary",)),
        name="out_router",
    )(oa, ob, oc, h, w_out, g, w_rt_hi, w_rt_lo, b_rt)


def _row_scatter(src_ref, dst_ref, sem, r, d):
    return pltpu.make_async_copy(src_ref.at[pl.ds(r, 1)], dst_ref.at[pl.ds(d, 1)], sem)


def _dispatch_kernel(dest_ref, u_ref, xs_in_ref, xs_ref, sem, *, tm, n):
    del xs_in_ref
    base = pl.program_id(0) * tm

    def issue(r, carry):
        for k in range(2):
            _row_scatter(u_ref, xs_ref, sem, r, dest_ref[k * n + base + r]).start()
        return carry

    def drain(r, carry):
        for k in range(2):
            _row_scatter(u_ref, xs_ref, sem, r, dest_ref[k * n + base + r]).wait()
        return carry

    lax.fori_loop(0, tm, issue, 0)
    lax.fori_loop(0, tm, drain, 0)


def _dispatch(dest, u, xs_zero):
    n = u.shape[0]
    tm = ROW_TILE
    return pl.pallas_call(
        functools.partial(_dispatch_kernel, tm=tm, n=n),
        grid_spec=pltpu.PrefetchScalarGridSpec(
            num_scalar_prefetch=1,
            grid=(n // tm,),
            in_specs=[pl.BlockSpec((tm, D_MODEL), lambda i, d: (i, 0)),
                      pl.BlockSpec(memory_space=pl.ANY)],
            out_specs=pl.BlockSpec(memory_space=pl.ANY),
            scratch_shapes=[pltpu.SemaphoreType.DMA(())]),
        out_shape=jax.ShapeDtypeStruct(xs_zero.shape, xs_zero.dtype),
        input_output_aliases={2: 0},
        compiler_params=pltpu.CompilerParams(dimension_semantics=("arbitrary",)),
        name="moe_dispatch",
    )(dest, u, xs_zero)


def _expert_kernel(be_ref, na_ref, x_ref, w1_ref, w3_ref, w2_ref, y_ref,
                   w1b_ref, w3b_ref, w2b_ref):
    j = pl.program_id(0)
    active = j < na_ref[0]
    new_expert = (j == 0) | (be_ref[j] != be_ref[jnp.maximum(j - 1, 0)])

    @pl.when(active & new_expert)
    def _():
        w1b_ref[...] = w1_ref[0].astype(BF16)
        w3b_ref[...] = w3_ref[0].astype(BF16)
        w2b_ref[...] = w2_ref[0].astype(BF16)

    @pl.when(active)
    def _():
        x = x_ref[...].astype(BF16)
        a = jnp.dot(x, w1b_ref[...], preferred_element_type=F32)
        b = jnp.dot(x, w3b_ref[...], preferred_element_type=F32)
        act = (a * _sigmoid(a) * b).astype(BF16)
        y_ref[...] = jnp.dot(act, w2b_ref[...], preferred_element_type=F32)

    @pl.when(jnp.logical_not(active))
    def _():
        y_ref[...] = jnp.zeros_like(y_ref)


def _experts(blk_expert, n_active, xs, w1, w3, w2, layer):
    p_rows = xs.shape[0]
    tb = MOE_BLOCK
    last = lambda j, na: jnp.maximum(jnp.minimum(j, na[0] - 1), 0)
    w_spec = lambda r, c: pl.BlockSpec((None, 1, r, c), lambda j, be, na: (layer, be[j], 0, 0))
    return pl.pallas_call(
        _expert_kernel,
        grid_spec=pltpu.PrefetchScalarGridSpec(
            num_scalar_prefetch=2,
            grid=(p_rows // tb,),
            in_specs=[pl.BlockSpec((tb, D_MODEL), lambda j, be, na: (last(j, na), 0)),
                      w_spec(D_MODEL, D_EXPERT), w_spec(D_MODEL, D_EXPERT),
                      w_spec(D_EXPERT, D_MODEL)],
            out_specs=pl.BlockSpec((tb, D_MODEL), lambda j, be, na: (j, 0)),
            scratch_shapes=[pltpu.VMEM((D_MODEL, D_EXPERT), BF16),
                            pltpu.VMEM((D_MODEL, D_EXPERT), BF16),
                            pltpu.VMEM((D_EXPERT, D_MODEL), BF16)]),
        out_shape=jax.ShapeDtypeStruct((p_rows, D_MODEL), F32),
        compiler_params=pltpu.CompilerParams(dimension_semantics=("arbitrary",),
                                             vmem_limit_bytes=VMEM_LIMIT),
        name="moe_experts",
    )(blk_expert, n_active, xs, w1, w3, w2)


def _row_gather(src_ref, dst_ref, sem, d, r):
    return pltpu.make_async_copy(src_ref.at[pl.ds(d, 1)], dst_ref.at[pl.ds(r, 1)], sem)


def _combine_kernel(dest_ref, h_ref, route_ref, gfin_ref, ys_ref, o_ref,
                    y0_ref, y1_ref, sem, *, tm, n, tiles_per_seq, skip_tiles, final):
    base = (pl.program_id(0) * tiles_per_seq + skip_tiles + pl.program_id(1)) * tm
    bufs = (y0_ref, y1_ref)

    def issue(r, carry):
        for k in range(2):
            _row_gather(ys_ref, bufs[k], sem, dest_ref[k * n + base + r], r).start()
        return carry

    def drain(r, carry):
        for k in range(2):
            _row_gather(ys_ref, bufs[k], sem, dest_ref[k * n + base + r], r).wait()
        return carry

    lax.fori_loop(0, tm, issue, 0)
    lax.fori_loop(0, tm, drain, 0)
    route = route_ref[...]
    out = h_ref[...] + route[:, 2:3] * y0_ref[...] + route[:, 3:4] * y1_ref[...]
    if final:
        out = _rms(out, gfin_ref[...])
    o_ref[...] = out


def _combine(dest, h1, route, g_final, ys, batch, seq_len, final):
    n = h1.shape[0]
    tm = ROW_TILE
    tiles_per_seq = seq_len // tm
    skip_tiles = (LPAD + N_META) // tm if final else 0
    out_tiles = tiles_per_seq - skip_tiles
    in_row = lambda w: pl.BlockSpec(
        (tm, w), lambda b, i, d: (b * tiles_per_seq + skip_tiles + i, 0))
    return pl.pallas_call(
        functools.partial(_combine_kernel, tm=tm, n=n, tiles_per_seq=tiles_per_seq,
                          skip_tiles=skip_tiles, final=final),
        grid_spec=pltpu.PrefetchScalarGridSpec(
            num_scalar_prefetch=1,
            grid=(batch, out_tiles),
            in_specs=[in_row(D_MODEL), in_row(LANES),
                      pl.BlockSpec((1, D_MODEL), lambda b, i, d: (0, 0)),
                      pl.BlockSpec(memory_space=pl.ANY)],
            out_specs=pl.BlockSpec((tm, D_MODEL), lambda b, i, d: (b * out_tiles + i, 0)),
            scratch_shapes=[pltpu.VMEM((tm, D_MODEL), F32), pltpu.VMEM((tm, D_MODEL), F32),
                            pltpu.SemaphoreType.DMA(())]),
        out_shape=jax.ShapeDtypeStruct((batch * out_tiles * tm, D_MODEL), F32),
        compiler_params=pltpu.CompilerParams(dimension_semantics=("arbitrary", "arbitrary")),
        name="moe_combine",
    )(dest, h1, route, g_final, ys)


def _slot_cols():
    src = jnp.arange(4 * 64)
    return (src // 64) * SLOT + src % 64


def _diff_k_cols():
    src = jnp.arange(DA_WIDTH)
    h, c, j = src // DA_DV, (src % DA_DV) // DA_DQK, src % DA_DQK
    return h * 2 * SLOT + c * SLOT + c * DA_DQK + j


def _spread(x, cols, width):
    return jnp.zeros(x.shape[:-1] + (width,), x.dtype).at[..., cols].set(x)


def _rope_tables(seq_len):
    pos = (jnp.arange(seq_len) - LPAD).astype(F32)
    col = jnp.arange(DA_WIDTH)
    inv = ROPE_THETA ** (-(2.0 * (col % (DA_DQK // 2))).astype(F32) / DA_DQK)
    ang = pos[:, None] * inv[None, :]
    first = (col % DA_DQK) < DA_DQK // 2
    cos = jnp.cos(ang)
    sin = jnp.where(first[None, :], -jnp.sin(ang), jnp.sin(ang))
    q_scale = DA_DQK ** -0.5 * LOG2E
    lay = lambda t: jnp.concatenate(
        [_spread(t * q_scale, _slot_cols(), HEAD_W), _spread(t, _diff_k_cols(), 2 * HEAD_W)],
        axis=1)
    return lay(cos), lay(sin)


def _in_weights(w):
    hq, hf, hi, hg = (w[:, i * 512:(i + 1) * 512] for i in range(4))
    dq, dk, dv = (w[:, 2048 + i * 256:2048 + (i + 1) * 256] for i in range(3))
    fq, fk, fv = (w[:, 2816 + i * 256:2816 + (i + 1) * 256] for i in range(3))
    ff = w[:, 3584:3588]
    col = jnp.arange(DA_WIDTH)
    half = DA_DQK // 2
    partner = jnp.where((col % DA_DQK) < half, col + half, col - half)
    slots = lambda m: _spread(m, _slot_cols(), HEAD_W)
    kslots = lambda m: _spread(m, _diff_k_cols(), 2 * HEAD_W)
    cat = jnp.concatenate(
        [hq, hi, hg, hf,
         slots(dq), kslots(dk), slots(dq[:, partner]), kslots(dk[:, partner]),
         slots(dv), slots(fq * (FX_DH ** -0.5)), slots(fk), slots(fv),
         ff, jnp.zeros((D_MODEL, LANES - FX_HEADS), w.dtype)],
        axis=1)
    return cat.astype(BF16)


def _pad_lanes(v, width=LANES):
    return jnp.zeros((1, width), F32).at[0, :v.shape[0]].set(v.astype(F32))


def kernel(x, meta_tokens, norm_mix, w_in, hgrn_lb, hgrn_norm, diff_lambda, diff_norm,
           fox_bias, fox_norm, w_out, norm_ffn, w_group, b_group, w_router, b_router,
           w1, w3, w2, norm_final):
    batch, seq, d = x.shape
    depth = w_in.shape[0]
    seq_len = LPAD + N_META + seq
    n = batch * seq_len
    pad = jnp.zeros((batch, LPAD, d), x.dtype)
    meta = jnp.broadcast_to(meta_tokens.astype(x.dtype)[None], (batch, N_META, d))
    h = jnp.concatenate([pad, meta, x], axis=1).reshape(n, d)

    s_lb = jax.nn.softmax(hgrn_lb.astype(F32), axis=0)
    lb_all = jnp.cumsum(s_lb, axis=0) - s_lb[0]
    cos_t, sin_t = _rope_tables(seq_len)

    tb = MOE_BLOCK
    n_blocks = (2 * n) // tb + N_EXPERTS
    p_rows = n_blocks * tb

    for layer in range(depth):
        lam_init = 0.8 - 0.6 * math.exp(-0.3 * layer)
        hqig, hf, dqk, dv, fqk, fv, ff = _in_proj(
            h, norm_mix[layer][None, :], _in_weights(w_in[layer]), cos_t, sin_t, seq_len)

        o_a = _hgrn(hqig, hf, lb_all[layer][None, :], hgrn_norm[layer][None, :],
                    batch, seq_len)

        lam_vecs = jnp.zeros((8, LANES), F32).at[:4, :DA_DQK].set(diff_lambda[layer].astype(F32))
        o_b = _diff_attn(dqk, dv, lam_vecs, _pad_lanes(diff_norm[layer]), lam_init,
                         batch, seq_len)

        fqk_aug = _fox_prep(ff.reshape(batch, seq_len, LANES), _pad_lanes(fox_bias[layer]),
                            fqk, batch, seq_len)
        o_c = _fox_attn(fqk_aug, fv, _pad_lanes(fox_norm[layer]), batch, seq_len)

        wo = w_out[layer]
        slot_rows = lambda m: _spread(m.T, _slot_cols(), HEAD_W).T
        wo_cat = jnp.concatenate(
            [wo[:HG_WIDTH], slot_rows(wo[HG_WIDTH:HG_WIDTH + DA_WIDTH]),
             slot_rows(wo[HG_WIDTH + DA_WIDTH:])], axis=0).astype(BF16)
        w_rt = jnp.concatenate(
            [w_router[layer], w_group[layer],
             jnp.zeros((d, LANES - N_EXPERTS - N_GROUPS), F32)], axis=1)
        b_rt = _pad_lanes(jnp.concatenate([b_router[layer], b_group[layer]]))
        h1, u2, route, counts = _out_router(
            o_a, o_b, o_c, h, wo_cat, norm_ffn[layer][None, :], w_rt, b_rt)

        cnt = counts[0, :N_EXPERTS].astype(jnp.int32)
        padded = (cnt + tb - 1) // tb * tb
        p_end = jnp.cumsum(padded)
        p_start = p_end - padded
        ids = route[:, 0:2].astype(jnp.int32)
        ranks = route[:, 4:6].astype(jnp.int32)
        dest = (p_start[ids] + ranks).T.reshape(2 * n)
        blk_start = jnp.arange(n_blocks, dtype=jnp.int32) * tb
        blk_expert = jnp.minimum(
            jnp.sum((p_end[None, :] <= blk_start[:, None]).astype(jnp.int32), axis=1),
            N_EXPERTS - 1)
        n_active = (p_end[-1:] // tb).astype(jnp.int32)

        xs = _dispatch(dest, u2, jnp.zeros((p_rows, d), F32))
        ys = _experts(blk_expert, n_active, xs, w1, w3, w2, layer)
        h = _combine(dest, h1, route, norm_final[None, :], ys, batch, seq_len,
                     final=(layer == depth - 1))

    return h.reshape(batch, seq, d)
```

```python
import functools
import math

import jax
import jax.numpy as jnp
from jax import lax
from jax.experimental import pallas as pl
from jax.experimental.pallas import tpu as pltpu

F32 = jnp.float32
BF16 = jnp.bfloat16

D_MODEL = 1024
N_META = 16
HG_HEADS = 4
HG_DK = 128
HG_WIDTH = 512
HG_CHUNK = 64
HG_SUB = 8
HG_PER_STEP = 4
DA_HEADS = 4
DA_DV = 64
DA_DQK = 32
DA_WIDTH = 256
FX_HEADS = 4
FX_DH = 64
FX_WIDTH = 256
N_GROUPS = 4
EXPERTS_PER_GROUP = 8
N_EXPERTS = 32
D_EXPERT = 512
ROPE_THETA = 10000.0
EPS = 1e-6
NEG = -1e30
TINY = 1e-30
LOG2E = 1.4426950408889634

LANES = 128
SLOT = LANES
HEAD_W = 4 * SLOT
ROW_TILE = 256
PROJ_TILE = 512
ATTN_BLOCK = 512
MOE_BLOCK = 256
LPAD = ATTN_BLOCK - N_META
ONE_LANE = 64
VMEM_LIMIT = 56 * 1024 * 1024

_C_HQIG = (0, 1536)
_C_HF = (1536, 2048)
_C_DQK = (2048, 3584)
_C_DQKP = (3584, 5120)
_C_FQ = (5120, 5632)
_C_FK = (5632, 6144)
_C_FF = (6144, 6272)
IN_COLS = 6272

_NT = (((1,), (1,)), ((), ()))
_TN = (((0,), (0,)), ((), ()))


def _sigmoid(x):
    return 1.0 / (1.0 + jnp.exp(-x))


def _rms(x, g):
    return x * lax.rsqrt(jnp.mean(x * x, axis=-1, keepdims=True) + EPS) * g


def _in_proj_kernel(h_ref, g_ref, w_ref, wvt_ref, cos_ref, sin_ref,
                    hqig_ref, hf_ref, dqk_ref, fqk_ref, ff_ref, dvt_ref, fvt_ref):
    u = _rms(h_ref[...], g_ref[...]).astype(BF16)

    def mm(c):
        return jnp.dot(u, w_ref[:, c[0]:c[1]], preferred_element_type=F32)

    hqig_ref[...] = mm(_C_HQIG).astype(BF16)
    hf_ref[...] = mm(_C_HF)
    dqk_ref[...] = (mm(_C_DQK) * cos_ref[...] + mm(_C_DQKP) * sin_ref[...]).astype(BF16)
    fqk_ref[:, 0:HEAD_W] = (mm(_C_FQ) * LOG2E).astype(BF16)
    fqk_ref[:, HEAD_W:2 * HEAD_W] = mm(_C_FK).astype(BF16)
    ff_ref[...] = mm(_C_FF)
    vt = lax.dot_general(wvt_ref[...], u, _NT, preferred_element_type=F32)
    slot_row = lax.broadcasted_iota(jnp.int32, vt.shape, 0) % SLOT
    vt = jnp.where(slot_row == ONE_LANE, 1.0, vt).astype(BF16)
    for h in range(4):
        dvt_ref[0, h, 0] = vt[h * SLOT:(h + 1) * SLOT]
        fvt_ref[0, h, 0] = vt[HEAD_W + h * SLOT:HEAD_W + (h + 1) * SLOT]


def _in_proj(h, g, w_cat, w_vt, cos_t, sin_t, batch, seq_len):
    n = h.shape[0]
    tm = PROJ_TILE
    assert tm == ATTN_BLOCK
    nk = seq_len // tm
    row = lambda w: pl.BlockSpec((tm, w), lambda i: (i, 0))
    tab = pl.BlockSpec((tm, 1536), lambda i: (i % nk, 0))
    once = lambda r, c: pl.BlockSpec((r, c), lambda i: (0, 0), pipeline_mode=pl.Buffered(1))
    widths = (1536, 512, 1536, 2 * HEAD_W, LANES)
    dtypes = (BF16, F32, BF16, BF16, F32)
    vt_spec = pl.BlockSpec((1, 4, 1, SLOT, tm), lambda i: (i // nk, 0, i % nk, 0, 0))
    vt_shape = jax.ShapeDtypeStruct((batch, 4, nk, SLOT, tm), BF16)
    return pl.pallas_call(
        _in_proj_kernel,
        grid=(n // tm,),
        in_specs=[row(D_MODEL),
                  pl.BlockSpec((1, D_MODEL), lambda i: (0, 0)),
                  once(D_MODEL, IN_COLS), once(2 * HEAD_W, D_MODEL),
                  tab, tab],
        out_specs=[row(w) for w in widths] + [vt_spec, vt_spec],
        out_shape=[jax.ShapeDtypeStruct((n, w), t) for w, t in zip(widths, dtypes)]
        + [vt_shape, vt_shape],
        compiler_params=pltpu.CompilerParams(dimension_semantics=("arbitrary",),
                                             vmem_limit_bytes=VMEM_LIMIT),
        name="in_proj",
    )(h, g, w_cat, w_vt, cos_t, sin_t)


def _bf16_split3(x):
    hi = x.astype(BF16).astype(F32)
    r = x - hi
    mid = r.astype(BF16).astype(F32)
    return hi, mid, r - mid


def _fox_prep_kernel(ff_ref, b_ref, qk_ref, o_ref, carry_ref):
    @pl.when(pl.program_id(1) == 0)
    def _():
        carry_ref[...] = jnp.zeros_like(carry_ref)

    x = ff_ref[0] + b_ref[...]
    lf = jnp.minimum(x, 0.0) - jnp.log(1.0 + jnp.exp(-jnp.abs(x)))
    t = x.shape[0]
    tri = (lax.broadcasted_iota(jnp.int32, (t, t), 1)
           <= lax.broadcasted_iota(jnp.int32, (t, t), 0)).astype(F32)
    cs = jnp.dot(tri, lf, precision=lax.Precision.HIGHEST,
                 preferred_element_type=F32) + carry_ref[...]
    carry_ref[...] = cs[t - 1:t, :]
    parts = _bf16_split3(cs * LOG2E)
    lane = lax.broadcasted_iota(jnp.int32, (t, SLOT), 1)
    d0 = FX_DH
    for h in range(FX_HEADS):
        q = qk_ref[:, h * SLOT:(h + 1) * SLOT].astype(F32)
        k = qk_ref[:, HEAD_W + h * SLOT:HEAD_W + (h + 1) * SLOT].astype(F32)
        for i, part in enumerate(parts):
            col = part[:, h:h + 1]
            q = jnp.where(lane == d0 + i, col, q)
            k = jnp.where(lane == d0 + 3 + i, -col, k)
        q = jnp.where((lane >= d0 + 3) & (lane < d0 + 6), 1.0, q)
        k = jnp.where((lane >= d0) & (lane < d0 + 3), 1.0, k)
        o_ref[:, h * SLOT:(h + 1) * SLOT] = q.astype(BF16)
        o_ref[:, HEAD_W + h * SLOT:HEAD_W + (h + 1) * SLOT] = k.astype(BF16)


def _fox_prep(ff, bias_row, fqk, batch, seq_len):
    n = fqk.shape[0]
    t = ROW_TILE
    nt = seq_len // t
    return pl.pallas_call(
        _fox_prep_kernel,
        grid=(batch, nt),
        in_specs=[pl.BlockSpec((1, t, LANES), lambda b, j: (b, j, 0)),
                  pl.BlockSpec((1, LANES), lambda b, j: (0, 0)),
                  pl.BlockSpec((t, 2 * HEAD_W), lambda b, j: (b * nt + j, 0))],
        out_specs=pl.BlockSpec((t, 2 * HEAD_W), lambda b, j: (b * nt + j, 0)),
        out_shape=jax.ShapeDtypeStruct((n, 2 * HEAD_W), BF16),
        scratch_shapes=[pltpu.VMEM((1, LANES), F32)],
        compiler_params=pltpu.CompilerParams(dimension_semantics=("arbitrary", "arbitrary")),
        name="fox_prep",
    )(ff, bias_row, fqk)


def _softmax_block(s, m_prev):
    m_new = jnp.maximum(m_prev, jnp.max(s, axis=0, keepdims=True))
    alpha = jnp.exp2(m_prev - m_new)
    p = jnp.exp2(s - m_new).astype(BF16)
    return p, alpha, m_new


def _block_start(j, blk):
    return j * blk if isinstance(j, int) else pl.multiple_of(j * blk, blk)


def _causal_valid(i, j, blk):
    k_idx = j * blk + lax.broadcasted_iota(jnp.int32, (blk, blk), 0)
    q_idx = i * blk + lax.broadcasted_iota(jnp.int32, (blk, blk), 1)
    return (k_idx <= q_idx) & (k_idx >= LPAD)


def _pipelined_key_blocks(i, scores, consume, mask_first):
    scores(0, 0)

    def pair(t, carry):
        @pl.when(t == 0)
        def _():
            mask_first()

        scores(2 * t + 1, 1)
        consume(2 * t, 0, False)
        scores(2 * t + 2, 0)
        consume(2 * t + 1, 1, False)
        return carry

    lax.fori_loop(0, lax.shift_right_logical(i, 1), pair, 0)
    odd = (i & 1) == 1

    @pl.when(jnp.logical_not(odd))
    def _():
        consume(i, 0, True)

    @pl.when(odd)
    def _():
        scores(i, 1)
        consume(i - 1, 0, True)
        consume(i, 1, True)


def _head_out(acc):
    lane = lax.broadcasted_iota(jnp.int32, acc.shape, 1)
    return jnp.where(lane < ONE_LANE, acc / acc[:, ONE_LANE:ONE_LANE + 1], 0.0)


def _head_rms(o, g):
    ms = jnp.sum(o * o, axis=-1, keepdims=True) * (1.0 / ONE_LANE)
    return o * lax.rsqrt(ms + EPS) * g


def _fox_attn_kernel(q_ref, k_ref, vt_ref, g_ref, o_ref, m_ref, acc_ref, sa_ref, sb_ref,
                     *, blk):
    i = pl.program_id(2)
    q = q_ref[...]
    bufs = (sa_ref, sb_ref)
    m_ref[...] = jnp.full(m_ref.shape, NEG, F32)
    acc_ref[...] = jnp.zeros_like(acc_ref)

    def scores(j, buf):
        bufs[buf][...] = lax.dot_general(k_ref[pl.ds(_block_start(j, blk), blk), :], q, _NT,
                                         preferred_element_type=F32)

    def consume(j, buf, masked):
        s = bufs[buf][...]
        if masked:
            s = jnp.where(_causal_valid(i, j, blk), s, NEG)
        p, alpha, m_new = _softmax_block(s, m_ref[...])
        acc_ref[...] = alpha * acc_ref[...] + jnp.dot(
            vt_ref[0, 0, j], p, preferred_element_type=F32)
        m_ref[...] = m_new

    def mask_first():
        sa_ref[...] = jnp.where(_causal_valid(i, 0, blk), sa_ref[...], NEG)

    _pipelined_key_blocks(i, scores, consume, mask_first)
    o_ref[...] = _head_rms(_head_out(acc_ref[...].T), g_ref[...]).astype(BF16)


def _value_spec(seq_len):
    nk = seq_len // ATTN_BLOCK
    return pl.BlockSpec((1, 1, nk, SLOT, ATTN_BLOCK), lambda b, h, i: (b, h, 0, 0, 0))


def _fox_attn(fqk, fvt, gain, batch, seq_len):
    n = fqk.shape[0]
    blk = ATTN_BLOCK
    nq = seq_len // blk
    return pl.pallas_call(
        functools.partial(_fox_attn_kernel, blk=blk),
        grid=(batch, FX_HEADS, nq),
        in_specs=[pl.BlockSpec((blk, SLOT), lambda b, h, i: (b * nq + i, h)),
                  pl.BlockSpec((seq_len, SLOT), lambda b, h, i: (b, FX_HEADS + h)),
                  _value_spec(seq_len),
                  pl.BlockSpec((1, SLOT), lambda b, h, i: (0, 0))],
        out_specs=pl.BlockSpec((blk, SLOT), lambda b, h, i: (b * nq + i, h)),
        out_shape=jax.ShapeDtypeStruct((n, HEAD_W), BF16),
        scratch_shapes=[pltpu.VMEM((1, blk), F32), pltpu.VMEM((SLOT, blk), F32),
                        pltpu.VMEM((blk, blk), F32), pltpu.VMEM((blk, blk), F32)],
        compiler_params=pltpu.CompilerParams(
            dimension_semantics=("arbitrary", "arbitrary", "arbitrary"),
            vmem_limit_bytes=VMEM_LIMIT),
        name="fox_attn",
    )(fqk, fqk, fvt, gain)


def _diff_attn_kernel(q_ref, k_ref, vt_ref, lam_ref, g_ref, o_ref, m_ref, acc_ref,
                      sa_ref, sb_ref, *, blk, lam_init):
    i = pl.program_id(2)
    q = q_ref[...]
    bufs = (sa_ref, sb_ref)
    m_ref[...] = jnp.full(m_ref.shape, NEG, F32)
    acc_ref[...] = jnp.zeros_like(acc_ref)

    def scores(j, buf):
        k0 = _block_start(j, blk)
        for c in range(2):
            bufs[buf][c] = lax.dot_general(
                k_ref[pl.ds(k0, blk), c * SLOT:(c + 1) * SLOT], q, _NT,
                preferred_element_type=F32)

    def consume(j, buf, masked):
        mask = _causal_valid(i, j, blk) if masked else None
        ps, alphas = [], []
        for c in range(2):
            s = bufs[buf][c]
            if masked:
                s = jnp.where(mask, s, NEG)
            p, alpha, m_new = _softmax_block(s, m_ref[c])
            m_ref[c] = m_new
            ps.append(p)
            alphas.append(alpha)
        pv = jnp.dot(vt_ref[0, 0, j], jnp.concatenate(ps, axis=1),
                     preferred_element_type=F32)
        for c in range(2):
            acc_ref[c] = alphas[c] * acc_ref[c] + pv[:, c * blk:(c + 1) * blk]

    def mask_first():
        mask = _causal_valid(i, 0, blk)
        for c in range(2):
            sa_ref[c] = jnp.where(mask, sa_ref[c], NEG)

    _pipelined_key_blocks(i, scores, consume, mask_first)
    lv = lam_ref[...]
    lam = (jnp.exp(jnp.sum(lv[0:1] * lv[1:2], axis=-1, keepdims=True))
           - jnp.exp(jnp.sum(lv[2:3] * lv[3:4], axis=-1, keepdims=True)) + lam_init)
    o = _head_out(acc_ref[0].T) - lam * _head_out(acc_ref[1].T)
    o_ref[...] = (_head_rms(o, g_ref[...]) * (1.0 - lam_init)).astype(BF16)


def _diff_attn(dqk, dvt, lam_vecs, gain, lam_init, batch, seq_len):
    n = dqk.shape[0]
    blk = ATTN_BLOCK
    nq = seq_len // blk
    return pl.pallas_call(
        functools.partial(_diff_attn_kernel, blk=blk, lam_init=lam_init),
        grid=(batch, DA_HEADS, nq),
        in_specs=[pl.BlockSpec((blk, SLOT), lambda b, h, i: (b * nq + i, h)),
                  pl.BlockSpec((seq_len, 2 * SLOT), lambda b, h, i: (b, 2 + h)),
                  _value_spec(seq_len),
                  pl.BlockSpec((8, LANES), lambda b, h, i: (0, 0)),
                  pl.BlockSpec((1, SLOT), lambda b, h, i: (0, 0))],
        out_specs=pl.BlockSpec((blk, SLOT), lambda b, h, i: (b * nq + i, h)),
        out_shape=jax.ShapeDtypeStruct((n, HEAD_W), BF16),
        scratch_shapes=[pltpu.VMEM((2, 1, blk), F32), pltpu.VMEM((2, SLOT, blk), F32),
                        pltpu.VMEM((2, blk, blk), F32), pltpu.VMEM((2, blk, blk), F32)],
        compiler_params=pltpu.CompilerParams(
            dimension_semantics=("arbitrary", "arbitrary", "arbitrary"),
            vmem_limit_bytes=VMEM_LIMIT),
        name="diff_attn",
    )(dqk, dqk, dvt, lam_vecs, gain)


def _hgrn_kernel(q_ref, i_ref, g_ref, f_ref, lb_ref, gain_ref, o_ref,
                 st_ref, hs_ref, *, tl):
    j = pl.program_id(2)

    @pl.when(j == 0)
    def _():
        st_ref[...] = jnp.zeros_like(st_ref)

    c_rows, sub = HG_CHUNK, HG_SUB
    tri = (lax.broadcasted_iota(jnp.int32, (c_rows, c_rows), 1)
           <= lax.broadcasted_iota(jnp.int32, (c_rows, c_rows), 0)).astype(F32)
    row_c = lax.broadcasted_iota(jnp.int32, (c_rows, 1), 0)
    t_sub = lax.broadcasted_iota(jnp.int32, (sub, 1), 0)
    lane_c = lax.broadcasted_iota(jnp.int32, (sub, c_rows), 1)

    t_all = lax.broadcasted_iota(jnp.int32, (c_rows, c_rows), 0)
    s_all = lax.broadcasted_iota(jnp.int32, (c_rows, c_rows), 1)
    level_masks = []
    g = sub
    while g < c_rows:
        level_masks.append(((t_all // g) == (s_all // g) + 1) & ((s_all // g) % 2 == 0))
        g *= 2

    heads = range(HG_PER_STEP)
    cols = [slice(hh * HG_DK, (hh + 1) * HG_DK) for hh in heads]

    def chunk(c, carry):
        r0 = pl.multiple_of(c * c_rows, c_rows)
        rows = pl.ds(r0, c_rows)
        valid = (j * tl + r0 + row_c) >= LPAD
        G, kk, qs, v, st, o_inter = [], [], [], [], [], []
        for hh in heads:
            lb = lb_ref[:, cols[hh]]
            sig = _sigmoid(f_ref[rows, cols[hh]])
            log_f = jnp.where(valid, jnp.log2(jnp.maximum(lb + (1.0 - lb) * sig, TINY)), 0.0)
            kk.append(jnp.where(valid, (1.0 - lb) * (1.0 - sig), 0.0))
            G.append(jnp.dot(tri, log_f, precision=lax.Precision.HIGHEST,
                             preferred_element_type=F32))
            qf = q_ref[rows, cols[hh]].astype(F32)
            qs.append(qf * _sigmoid(qf) * (HG_DK ** -0.5))
            v.append(i_ref[rows, cols[hh]])
            hs_ref[hh] = G[hh] - jnp.log2(kk[hh])
            st.append(st_ref[hh])
            o_inter.append(lax.dot_general(
                (qs[hh] * jnp.exp2(G[hh])).astype(BF16), st[hh].astype(BF16), _NT,
                preferred_element_type=F32))
        a_mat = [jnp.zeros((c_rows, c_rows), F32) for _ in heads]
        g = sub
        for mask in level_masks:
            for hh in heads:
                ref = jnp.concatenate(
                    [jnp.broadcast_to(G[hh][p + g - 1:p + g], (2 * g, HG_DK))
                     for p in range(0, c_rows, 2 * g)], axis=0)
                e = jnp.exp2(-jnp.abs(G[hh] - ref))
                part = lax.dot_general((qs[hh] * e).astype(BF16), (kk[hh] * e).astype(BF16),
                                       _NT, preferred_element_type=F32)
                a_mat[hh] = jnp.where(mask, part, a_mat[hh])
            g *= 2
        a_rows = [[] for _ in heads]
        for b in range(c_rows // sub):
            lo = b * sub
            for hh in heads:
                q_b = qs[hh][lo:lo + sub]
                g_b = G[hh][lo:lo + sub]
                a_blk = a_mat[hh][lo:lo + sub]
                for s in range(lo, lo + sub):
                    y = q_b * jnp.exp2(g_b - hs_ref[hh, s:s + 1, :])
                    a_blk = jnp.where(lane_c == s, jnp.sum(y, axis=-1, keepdims=True), a_blk)
                a_rows[hh].append(jnp.where(lane_c <= t_sub + lo, a_blk, 0.0))
        for hh in heads:
            a_full = jnp.concatenate(a_rows[hh], axis=0).astype(BF16)
            o = o_inter[hh] + jnp.dot(a_full, v[hh], preferred_element_type=F32)
            g_last = G[hh][c_rows - 1:c_rows]
            kd = kk[hh] * jnp.exp2(g_last - G[hh])
            st_ref[hh] = st[hh] * jnp.exp2(g_last) + lax.dot_general(
                v[hh], kd.astype(BF16), _TN, preferred_element_type=F32)
            gate = g_ref[rows, cols[hh]].astype(F32)
            o = _rms(o, gain_ref[...]) * (gate * _sigmoid(gate))
            o_ref[rows, cols[hh]] = o.astype(BF16)
        return carry

    lax.fori_loop(0, tl // c_rows, chunk, 0)


def _hgrn(hqig, hf, lb_row, gain, batch, seq_len):
    n = hqig.shape[0]
    tl = ROW_TILE
    nt = seq_len // tl
    w = HG_PER_STEP * HG_DK
    groups = HG_HEADS // HG_PER_STEP
    col = lambda off: pl.BlockSpec((tl, w), lambda b, h, j: (b * nt + j, off + h))
    return pl.pallas_call(
        functools.partial(_hgrn_kernel, tl=tl),
        grid=(batch, groups, nt),
        in_specs=[col(0), col(groups), col(2 * groups), col(0),
                  pl.BlockSpec((1, w), lambda b, h, j: (0, h)),
                  pl.BlockSpec((1, HG_DK), lambda b, h, j: (0, 0))],
        out_specs=col(0),
        out_shape=jax.ShapeDtypeStruct((n, HG_WIDTH), BF16),
        scratch_shapes=[pltpu.VMEM((HG_PER_STEP, HG_DK, HG_DK), F32),
                        pltpu.VMEM((HG_PER_STEP, HG_CHUNK, HG_DK), F32)],
        compiler_params=pltpu.CompilerParams(
            dimension_semantics=("arbitrary", "arbitrary", "arbitrary")),
        name="hgrn2",
    )(hqig, hqig, hqig, hf, lb_row, gain)


def _out_router_kernel(oa_ref, ob_ref, oc_ref, h_ref, wo_ref, g_ref, wrh_ref, wrl_ref, br_ref,
                       h1_ref, u_ref, route_ref, cnt_ref, carry_ref):
    @pl.when(pl.program_id(0) == 0)
    def _():
        carry_ref[...] = jnp.zeros_like(carry_ref)

    h1 = (h_ref[...]
          + jnp.dot(oa_ref[...], wo_ref[0:512, :], preferred_element_type=F32)
          + jnp.dot(ob_ref[...], wo_ref[512:1024, :], preferred_element_type=F32)
          + jnp.dot(oc_ref[...], wo_ref[1024:1536, :], preferred_element_type=F32))
    h1_ref[...] = h1
    u = _rms(h1, g_ref[...])
    u_ref[...] = u
    u_hi = u.astype(BF16)
    u_lo = (u - u_hi.astype(F32)).astype(BF16)
    logits = (jnp.dot(u_hi, wrh_ref[...], preferred_element_type=F32)
              + (jnp.dot(u_hi, wrl_ref[...], preferred_element_type=F32)
                 + jnp.dot(u_lo, wrh_ref[...], preferred_element_type=F32))
              + br_ref[...])
    tm = logits.shape[0]
    lane = lax.broadcasted_iota(jnp.int32, (tm, LANES), 1).astype(F32)
    big = float(LANES)
    is_g = (lane >= N_EXPERTS) & (lane < N_EXPERTS + N_GROUPS)
    gl = jnp.where(is_g, logits, -jnp.inf)
    gmax = jnp.max(gl, axis=-1, keepdims=True)
    gsel = jnp.min(jnp.where(gl == gmax, lane, big), axis=-1, keepdims=True) - N_EXPERTS
    p_g = 1.0 / jnp.sum(jnp.exp(gl - gmax), axis=-1, keepdims=True)
    lo = gsel * EXPERTS_PER_GROUP
    el = jnp.where((lane >= lo) & (lane < lo + EXPERTS_PER_GROUP), logits, -jnp.inf)
    m1 = jnp.max(el, axis=-1, keepdims=True)
    i1 = jnp.min(jnp.where(el == m1, lane, big), axis=-1, keepdims=True)
    el2 = jnp.where(lane == i1, -jnp.inf, el)
    m2 = jnp.max(el2, axis=-1, keepdims=True)
    i2 = jnp.min(jnp.where(el2 == m2, lane, big), axis=-1, keepdims=True)
    r = jnp.exp(m2 - m1)
    gate1 = p_g / (1.0 + r)
    gate2 = gate1 * r
    oh1 = lane == i1
    oh2 = lane == i2
    onehot = jnp.where(oh1 | oh2, 1.0, 0.0)
    tri = (lax.broadcasted_iota(jnp.int32, (tm, tm), 1)
           < lax.broadcasted_iota(jnp.int32, (tm, tm), 0)).astype(BF16)
    before = jnp.dot(tri, onehot.astype(BF16), preferred_element_type=F32) + carry_ref[...]
    rank1 = jnp.sum(jnp.where(oh1, before, 0.0), axis=-1, keepdims=True)
    rank2 = jnp.sum(jnp.where(oh2, before, 0.0), axis=-1, keepdims=True)
    total = carry_ref[...] + jnp.sum(onehot, axis=0, keepdims=True)
    carry_ref[...] = total
    cnt_ref[...] = total
    route = jnp.where(lane == 0, i1, 0.0)
    for idx, val in ((1, i2), (2, gate1), (3, gate2), (4, rank1), (5, rank2)):
        route = jnp.where(lane == idx, val, route)
    route_ref[...] = route


def _out_router(oa, ob, oc, h, w_out, g, w_rt, b_rt):
    n = h.shape[0]
    w_rt_hi = w_rt.astype(BF16)
    w_rt_lo = (w_rt - w_rt_hi.astype(F32)).astype(BF16)
    tm = ROW_TILE
    row = lambda w: pl.BlockSpec((tm, w), lambda i: (i, 0))
    const = lambda r, c: pl.BlockSpec((r, c), lambda i: (0, 0))
    return pl.pallas_call(
        _out_router_kernel,
        grid=(n // tm,),
        in_specs=[row(HG_WIDTH), row(HEAD_W), row(HEAD_W), row(D_MODEL),
                  const(HG_WIDTH + 2 * HEAD_W, D_MODEL),
                  const(1, D_MODEL), const(D_MODEL, LANES), const(D_MODEL, LANES),
                  const(1, LANES)],
        out_specs=[row(D_MODEL), row(D_MODEL), row(LANES), const(1, LANES)],
        out_shape=[jax.ShapeDtypeStruct((n, D_MODEL), F32),
                   jax.ShapeDtypeStruct((n, D_MODEL), F32),
                   jax.ShapeDtypeStruct((n, LANES), F32),
                   jax.ShapeDtypeStruct((1, LANES), F32)],
        scratch_shapes=[pltpu.VMEM((1, LANES), F32)],
        compiler_params=pltpu.CompilerParams(dimension_semantics=("arbitrary",)),
        name="out_router",
    )(oa, ob, oc, h, w_out, g, w_rt_hi, w_rt_lo, b_rt)


def _row_scatter(src_ref, dst_ref, sem, r, d):
    return pltpu.make_async_copy(src_ref.at[pl.ds(r, 1)], dst_ref.at[pl.ds(d, 1)], sem)


def _dispatch_kernel(dest_ref, u_ref, xs_in_ref, xs_ref, sem, *, tm, n):
    del xs_in_ref
    base = pl.program_id(0) * tm

    def issue(r, carry):
        for k in range(2):
            _row_scatter(u_ref, xs_ref, sem, r, dest_ref[k * n + base + r]).start()
        return carry

    def drain(r, carry):
        for k in range(2):
            _row_scatter(u_ref, xs_ref, sem, r, dest_ref[k * n + base + r]).wait()
        return carry

    lax.fori_loop(0, tm, issue, 0)
    lax.fori_loop(0, tm, drain, 0)


def _dispatch(dest, u, xs_zero):
    n = u.shape[0]
    tm = ROW_TILE
    return pl.pallas_call(
        functools.partial(_dispatch_kernel, tm=tm, n=n),
        grid_spec=pltpu.PrefetchScalarGridSpec(
            num_scalar_prefetch=1,
            grid=(n // tm,),
            in_specs=[pl.BlockSpec((tm, D_MODEL), lambda i, d: (i, 0)),
                      pl.BlockSpec(memory_space=pl.ANY)],
            out_specs=pl.BlockSpec(memory_space=pl.ANY),
            scratch_shapes=[pltpu.SemaphoreType.DMA(())]),
        out_shape=jax.ShapeDtypeStruct(xs_zero.shape, xs_zero.dtype),
        input_output_aliases={2: 0},
        compiler_params=pltpu.CompilerParams(dimension_semantics=("arbitrary",)),
        name="moe_dispatch",
    )(dest, u, xs_zero)


def _expert_kernel(be_ref, na_ref, x_ref, w1_ref, w3_ref, w2_ref, y_ref,
                   w1b_ref, w3b_ref, w2b_ref):
    j = pl.program_id(0)
    active = j < na_ref[0]
    new_expert = (j == 0) | (be_ref[j] != be_ref[jnp.maximum(j - 1, 0)])

    @pl.when(active & new_expert)
    def _():
        w1b_ref[...] = w1_ref[0].astype(BF16)
        w3b_ref[...] = w3_ref[0].astype(BF16)
        w2b_ref[...] = w2_ref[0].astype(BF16)

    @pl.when(active)
    def _():
        x = x_ref[...].astype(BF16)
        a = jnp.dot(x, w1b_ref[...], preferred_element_type=F32)
        b = jnp.dot(x, w3b_ref[...], preferred_element_type=F32)
        act = (a * _sigmoid(a) * b).astype(BF16)
        y_ref[...] = jnp.dot(act, w2b_ref[...], preferred_element_type=F32)

    @pl.when(jnp.logical_not(active))
    def _():
        y_ref[...] = jnp.zeros_like(y_ref)


def _experts(blk_expert, n_active, xs, w1, w3, w2, layer):
    p_rows = xs.shape[0]
    tb = MOE_BLOCK
    last = lambda j, na: jnp.maximum(jnp.minimum(j, na[0] - 1), 0)
    w_spec = lambda r, c: pl.BlockSpec((None, 1, r, c), lambda j, be, na: (layer, be[j], 0, 0))
    return pl.pallas_call(
        _expert_kernel,
        grid_spec=pltpu.PrefetchScalarGridSpec(
            num_scalar_prefetch=2,
            grid=(p_rows // tb,),
            in_specs=[pl.BlockSpec((tb, D_MODEL), lambda j, be, na: (last(j, na), 0)),
                      w_spec(D_MODEL, D_EXPERT), w_spec(D_MODEL, D_EXPERT),
                      w_spec(D_EXPERT, D_MODEL)],
            out_specs=pl.BlockSpec((tb, D_MODEL), lambda j, be, na: (j, 0)),
            scratch_shapes=[pltpu.VMEM((D_MODEL, D_EXPERT), BF16),
                            pltpu.VMEM((D_MODEL, D_EXPERT), BF16),
                            pltpu.VMEM((D_EXPERT, D_MODEL), BF16)]),
        out_shape=jax.ShapeDtypeStruct((p_rows, D_MODEL), F32),
        compiler_params=pltpu.CompilerParams(dimension_semantics=("arbitrary",),
                                             vmem_limit_bytes=VMEM_LIMIT),
        name="moe_experts",
    )(blk_expert, n_active, xs, w1, w3, w2)


def _row_gather(src_ref, dst_ref, sem, d, r):
    return pltpu.make_async_copy(src_ref.at[pl.ds(d, 1)], dst_ref.at[pl.ds(r, 1)], sem)


def _combine_kernel(dest_ref, h_ref, route_ref, gfin_ref, ys_ref, o_ref,
                    y0_ref, y1_ref, sem, *, tm, n, tiles_per_seq, skip_tiles, final):
    base = (pl.program_id(0) * tiles_per_seq + skip_tiles + pl.program_id(1)) * tm
    bufs = (y0_ref, y1_ref)

    def issue(r, carry):
        for k in range(2):
            _row_gather(ys_ref, bufs[k], sem, dest_ref[k * n + base + r], r).start()
        return carry

    def drain(r, carry):
        for k in range(2):
            _row_gather(ys_ref, bufs[k], sem, dest_ref[k * n + base + r], r).wait()
        return carry

    lax.fori_loop(0, tm, issue, 0)
    lax.fori_loop(0, tm, drain, 0)
    route = route_ref[...]
    out = h_ref[...] + route[:, 2:3] * y0_ref[...] + route[:, 3:4] * y1_ref[...]
    if final:
        out = _rms(out, gfin_ref[...])
    o_ref[...] = out


def _combine(dest, h1, route, g_final, ys, batch, seq_len, final):
    n = h1.shape[0]
    tm = ROW_TILE
    tiles_per_seq = seq_len // tm
    skip_tiles = (LPAD + N_META) // tm if final else 0
    out_tiles = tiles_per_seq - skip_tiles
    in_row = lambda w: pl.BlockSpec(
        (tm, w), lambda b, i, d: (b * tiles_per_seq + skip_tiles + i, 0))
    return pl.pallas_call(
        functools.partial(_combine_kernel, tm=tm, n=n, tiles_per_seq=tiles_per_seq,
                          skip_tiles=skip_tiles, final=final),
        grid_spec=pltpu.PrefetchScalarGridSpec(
            num_scalar_prefetch=1,
            grid=(batch, out_tiles),
            in_specs=[in_row(D_MODEL), in_row(LANES),
                      pl.BlockSpec((1, D_MODEL), lambda b, i, d: (0, 0)),
                      pl.BlockSpec(memory_space=pl.ANY)],
            out_specs=pl.BlockSpec((tm, D_MODEL), lambda b, i, d: (b * out_tiles + i, 0)),
            scratch_shapes=[pltpu.VMEM((tm, D_MODEL), F32), pltpu.VMEM((tm, D_MODEL), F32),
                            pltpu.SemaphoreType.DMA(())]),
        out_shape=jax.ShapeDtypeStruct((batch * out_tiles * tm, D_MODEL), F32),
        compiler_params=pltpu.CompilerParams(dimension_semantics=("arbitrary", "arbitrary")),
        name="moe_combine",
    )(dest, h1, route, g_final, ys)


def _to_slots(m):
    lead = m.shape[:-1]
    x = m.reshape(lead + (4, 64))
    x = jnp.pad(x, [(0, 0)] * (len(lead) + 1) + [(0, SLOT - 64)])
    return x.reshape(lead + (HEAD_W,))


def _to_key_slots(m):
    lead = m.shape[:-1]
    x = m.reshape(lead + (DA_HEADS, 2, DA_DQK))
    keep = [(0, 0)] * (len(lead) + 1)
    c0 = jnp.pad(x[..., 0, :], keep + [(0, SLOT - DA_DQK)])
    c1 = jnp.pad(x[..., 1, :], keep + [(DA_DQK, SLOT - 2 * DA_DQK)])
    return jnp.stack([c0, c1], axis=-2).reshape(lead + (2 * HEAD_W,))


def _rope_partner(m):
    lead = m.shape[:-1]
    x = m.reshape(lead + (DA_WIDTH // DA_DQK, 2, DA_DQK // 2))
    return x[..., ::-1, :].reshape(lead + (DA_WIDTH,))


def _rope_tables(seq_len):
    half = DA_DQK // 2
    pos = (jnp.arange(seq_len) - LPAD).astype(F32)
    inv = ROPE_THETA ** (-jnp.arange(0, DA_DQK, 2, dtype=F32) / DA_DQK)
    ang = pos[:, None] * inv[None, :]
    groups = DA_WIDTH // DA_DQK
    cos = jnp.tile(jnp.cos(ang), (1, 2 * groups))
    sin = jnp.tile(jnp.concatenate([-jnp.sin(ang), jnp.sin(ang)], axis=1), (1, groups))
    q_scale = DA_DQK ** -0.5 * LOG2E
    lay = lambda t: jnp.concatenate([_to_slots(t * q_scale), _to_key_slots(t)], axis=1)
    return lay(cos), lay(sin)


def _in_weights(w):
    hq, hf, hi, hg = (w[:, i * 512:(i + 1) * 512] for i in range(4))
    dq, dk, dv = (w[:, 2048 + i * 256:2048 + (i + 1) * 256] for i in range(3))
    fq, fk, fv = (w[:, 2816 + i * 256:2816 + (i + 1) * 256] for i in range(3))
    ff = w[:, 3584:3588]
    cat = jnp.concatenate(
        [hq, hi, hg, hf,
         _to_slots(dq), _to_key_slots(dk),
         _to_slots(_rope_partner(dq)), _to_key_slots(_rope_partner(dk)),
         _to_slots(fq * (FX_DH ** -0.5)), _to_slots(fk),
         ff, jnp.zeros((D_MODEL, LANES - FX_HEADS), w.dtype)],
        axis=1)
    w_vt = jnp.concatenate([_to_slots(dv), _to_slots(fv)], axis=1).T
    return cat.astype(BF16), w_vt.astype(BF16)


def _pad_lanes(v, width=LANES):
    return jnp.zeros((1, width), F32).at[0, :v.shape[0]].set(v.astype(F32))


def kernel(x, meta_tokens, norm_mix, w_in, hgrn_lb, hgrn_norm, diff_lambda, diff_norm,
           fox_bias, fox_norm, w_out, norm_ffn, w_group, b_group, w_router, b_router,
           w1, w3, w2, norm_final):
    batch, seq, d = x.shape
    depth = w_in.shape[0]
    seq_len = LPAD + N_META + seq
    n = batch * seq_len
    pad = jnp.zeros((batch, LPAD, d), x.dtype)
    meta = jnp.broadcast_to(meta_tokens.astype(x.dtype)[None], (batch, N_META, d))
    h = jnp.concatenate([pad, meta, x], axis=1).reshape(n, d)

    s_lb = jax.nn.softmax(hgrn_lb.astype(F32), axis=0)
    lb_all = jnp.cumsum(s_lb, axis=0) - s_lb[0]
    cos_t, sin_t = _rope_tables(seq_len)

    tb = MOE_BLOCK
    n_blocks = (2 * n) // tb + N_EXPERTS
    p_rows = n_blocks * tb

    for layer in range(depth):
        lam_init = 0.8 - 0.6 * math.exp(-0.3 * layer)
        w_cat, w_vt = _in_weights(w_in[layer])
        hqig, hf, dqk, fqk, ff, dvt, fvt = _in_proj(
            h, norm_mix[layer][None, :], w_cat, w_vt, cos_t, sin_t, batch, seq_len)

        o_a = _hgrn(hqig, hf, lb_all[layer][None, :], hgrn_norm[layer][None, :],
                    batch, seq_len)

        lam_vecs = jnp.zeros((8, LANES), F32).at[:4, :DA_DQK].set(diff_lambda[layer].astype(F32))
        o_b = _diff_attn(dqk, dvt, lam_vecs, _pad_lanes(diff_norm[layer]), lam_init,
                         batch, seq_len)

        fqk_aug = _fox_prep(ff.reshape(batch, seq_len, LANES), _pad_lanes(fox_bias[layer]),
                            fqk, batch, seq_len)
        o_c = _fox_attn(fqk_aug, fvt, _pad_lanes(fox_norm[layer]), batch, seq_len)

        wo = w_out[layer]
        slot_rows = lambda m: _to_slots(m.T).T
        wo_cat = jnp.concatenate(
            [wo[:HG_WIDTH], slot_rows(wo[HG_WIDTH:HG_WIDTH + DA_WIDTH]),
             slot_rows(wo[HG_WIDTH + DA_WIDTH:])], axis=0).astype(BF16)
        w_rt = jnp.concatenate(
            [w_router[layer], w_group[layer],
             jnp.zeros((d, LANES - N_EXPERTS - N_GROUPS), F32)], axis=1)
        b_rt = _pad_lanes(jnp.concatenate([b_router[layer], b_group[layer]]))
        h1, u2, route, counts = _out_router(
            o_a, o_b, o_c, h, wo_cat, norm_ffn[layer][None, :], w_rt, b_rt)

        cnt = counts[0, :N_EXPERTS].astype(jnp.int32)
        padded = (cnt + tb - 1) // tb * tb
        p_end = jnp.cumsum(padded)
        p_start = p_end - padded
        ids = route[:, 0:2].astype(jnp.int32)
        ranks = route[:, 4:6].astype(jnp.int32)
        dest = (p_start[ids] + ranks).T.reshape(2 * n)
        blk_start = jnp.arange(n_blocks, dtype=jnp.int32) * tb
        blk_expert = jnp.minimum(
            jnp.sum((p_end[None, :] <= blk_start[:, None]).astype(jnp.int32), axis=1),
            N_EXPERTS - 1)
        n_active = (p_end[-1:] // tb).astype(jnp.int32)

        xs = _dispatch(dest, u2, jnp.zeros((p_rows, d), F32))
        ys = _experts(blk_expert, n_active, xs, w1, w3, w2, layer)
        h = _combine(dest, h1, route, norm_final[None, :], ys, batch, seq_len,
                     final=(layer == depth - 1))

    return h.reshape(batch, seq, d)
```

```python
import functools
import math

import jax
import jax.numpy as jnp
from jax import lax
from jax.experimental import pallas as pl
from jax.experimental.pallas import tpu as pltpu

F32 = jnp.float32
BF16 = jnp.bfloat16

D_MODEL = 1024
N_META = 16
HG_HEADS = 4
HG_DK = 128
HG_WIDTH = 512
HG_CHUNK = 64
HG_SUB = 8
HG_PER_STEP = 4
DA_HEADS = 4
DA_DV = 64
DA_DQK = 32
DA_WIDTH = 256
FX_HEADS = 4
FX_DH = 64
FX_WIDTH = 256
N_GROUPS = 4
EXPERTS_PER_GROUP = 8
N_EXPERTS = 32
D_EXPERT = 512
ROPE_THETA = 10000.0
EPS = 1e-6
NEG = -1e30
TINY = 1e-30
LOG2E = 1.4426950408889634

LANES = 128
SLOT = LANES
HEAD_W = 4 * SLOT
ROW_TILE = 256
PROJ_TILE = 512
ATTN_BLOCK = 512
MOE_BLOCK = 256
LPAD = ATTN_BLOCK - N_META
ONE_LANE = 64
VMEM_LIMIT = 56 * 1024 * 1024

_C_HQIG = (0, 1536)
_C_HF = (1536, 2048)
_C_DQK = (2048, 3584)
_C_DQKP = (3584, 5120)
_C_FQ = (5120, 5632)
_C_FK = (5632, 6144)
_C_FF = (6144, 6272)
IN_COLS = 6272

_NT = (((1,), (1,)), ((), ()))
_TN = (((0,), (0,)), ((), ()))


def _sigmoid(x):
    return 1.0 / (1.0 + jnp.exp(-x))


def _rms(x, g):
    return x * lax.rsqrt(jnp.mean(x * x, axis=-1, keepdims=True) + EPS) * g


def _in_proj_kernel(h_ref, g_ref, w_ref, wvt_ref, cos_ref, sin_ref,
                    hqig_ref, hf_ref, dqk_ref, fqk_ref, ff_ref, dvt_ref, fvt_ref):
    u = _rms(h_ref[...], g_ref[...]).astype(BF16)

    def mm(c):
        return jnp.dot(u, w_ref[:, c[0]:c[1]], preferred_element_type=F32)

    hqig_ref[...] = mm(_C_HQIG).astype(BF16)
    hf_ref[...] = mm(_C_HF)
    dqk_ref[...] = (mm(_C_DQK) * cos_ref[...] + mm(_C_DQKP) * sin_ref[...]).astype(BF16)
    fqk_ref[:, 0:HEAD_W] = (mm(_C_FQ) * LOG2E).astype(BF16)
    fqk_ref[:, HEAD_W:2 * HEAD_W] = mm(_C_FK).astype(BF16)
    ff_ref[...] = mm(_C_FF)
    vt = lax.dot_general(wvt_ref[...], u, _NT, preferred_element_type=F32)
    slot_row = lax.broadcasted_iota(jnp.int32, vt.shape, 0) % SLOT
    vt = jnp.where(slot_row == ONE_LANE, 1.0, vt).astype(BF16)
    for h in range(4):
        dvt_ref[0, h, 0] = vt[h * SLOT:(h + 1) * SLOT]
        fvt_ref[0, h, 0] = vt[HEAD_W + h * SLOT:HEAD_W + (h + 1) * SLOT]


def _in_proj(h, g, w_cat, w_vt, cos_t, sin_t, batch, seq_len):
    n = h.shape[0]
    tm = PROJ_TILE
    assert tm == ATTN_BLOCK
    nk = seq_len // tm
    row = lambda w: pl.BlockSpec((tm, w), lambda i: (i, 0))
    tab = pl.BlockSpec((tm, 1536), lambda i: (i % nk, 0))
    once = lambda r, c: pl.BlockSpec((r, c), lambda i: (0, 0), pipeline_mode=pl.Buffered(1))
    widths = (1536, 512, 1536, 2 * HEAD_W, LANES)
    dtypes = (BF16, F32, BF16, BF16, F32)
    vt_spec = pl.BlockSpec((1, 4, 1, SLOT, tm), lambda i: (i // nk, 0, i % nk, 0, 0))
    vt_shape = jax.ShapeDtypeStruct((batch, 4, nk, SLOT, tm), BF16)
    return pl.pallas_call(
        _in_proj_kernel,
        grid=(n // tm,),
        in_specs=[row(D_MODEL),
                  pl.BlockSpec((1, D_MODEL), lambda i: (0, 0)),
                  once(D_MODEL, IN_COLS), once(2 * HEAD_W, D_MODEL),
                  tab, tab],
        out_specs=[row(w) for w in widths] + [vt_spec, vt_spec],
        out_shape=[jax.ShapeDtypeStruct((n, w), t) for w, t in zip(widths, dtypes)]
        + [vt_shape, vt_shape],
        compiler_params=pltpu.CompilerParams(dimension_semantics=("arbitrary",),
                                             vmem_limit_bytes=VMEM_LIMIT),
        name="in_proj",
    )(h, g, w_cat, w_vt, cos_t, sin_t)


def _bf16_split3(x):
    hi = x.astype(BF16).astype(F32)
    r = x - hi
    mid = r.astype(BF16).astype(F32)
    return hi, mid, r - mid


def _fox_prep_kernel(ff_ref, b_ref, qk_ref, o_ref, carry_ref):
    @pl.when(pl.program_id(1) == 0)
    def _():
        carry_ref[...] = jnp.zeros_like(carry_ref)

    x = ff_ref[0] + b_ref[...]
    lf = jnp.minimum(x, 0.0) - jnp.log(1.0 + jnp.exp(-jnp.abs(x)))
    t = x.shape[0]
    tri = (lax.broadcasted_iota(jnp.int32, (t, t), 1)
           <= lax.broadcasted_iota(jnp.int32, (t, t), 0)).astype(F32)
    cs = jnp.dot(tri, lf, precision=lax.Precision.HIGHEST,
                 preferred_element_type=F32) + carry_ref[...]
    carry_ref[...] = cs[t - 1:t, :]
    parts = _bf16_split3(cs * LOG2E)
    lane = lax.broadcasted_iota(jnp.int32, (t, SLOT), 1)
    d0 = FX_DH
    for h in range(FX_HEADS):
        q = qk_ref[:, h * SLOT:(h + 1) * SLOT].astype(F32)
        k = qk_ref[:, HEAD_W + h * SLOT:HEAD_W + (h + 1) * SLOT].astype(F32)
        for i, part in enumerate(parts):
            col = part[:, h:h + 1]
            q = jnp.where(lane == d0 + i, col, q)
            k = jnp.where(lane == d0 + 3 + i, -col, k)
        q = jnp.where((lane >= d0 + 3) & (lane < d0 + 6), 1.0, q)
        k = jnp.where((lane >= d0) & (lane < d0 + 3), 1.0, k)
        o_ref[:, h * SLOT:(h + 1) * SLOT] = q.astype(BF16)
        o_ref[:, HEAD_W + h * SLOT:HEAD_W + (h + 1) * SLOT] = k.astype(BF16)


def _fox_prep(ff, bias_row, fqk, batch, seq_len):
    n = fqk.shape[0]
    t = ROW_TILE
    nt = seq_len // t
    return pl.pallas_call(
        _fox_prep_kernel,
        grid=(batch, nt),
        in_specs=[pl.BlockSpec((1, t, LANES), lambda b, j: (b, j, 0)),
                  pl.BlockSpec((1, LANES), lambda b, j: (0, 0)),
                  pl.BlockSpec((t, 2 * HEAD_W), lambda b, j: (b * nt + j, 0))],
        out_specs=pl.BlockSpec((t, 2 * HEAD_W), lambda b, j: (b * nt + j, 0)),
        out_shape=jax.ShapeDtypeStruct((n, 2 * HEAD_W), BF16),
        scratch_shapes=[pltpu.VMEM((1, LANES), F32)],
        compiler_params=pltpu.CompilerParams(dimension_semantics=("arbitrary", "arbitrary")),
        name="fox_prep",
    )(ff, bias_row, fqk)


def _softmax_block(s, m_prev):
    m_new = jnp.maximum(m_prev, jnp.max(s, axis=0, keepdims=True))
    alpha = jnp.exp2(m_prev - m_new)
    p = jnp.exp2(s - m_new).astype(BF16)
    return p, alpha, m_new


def _block_start(j, blk):
    return j * blk if isinstance(j, int) else pl.multiple_of(j * blk, blk)


def _causal_valid(i, j, blk):
    k_idx = j * blk + lax.broadcasted_iota(jnp.int32, (blk, blk), 0)
    q_idx = i * blk + lax.broadcasted_iota(jnp.int32, (blk, blk), 1)
    return (k_idx <= q_idx) & (k_idx >= LPAD)


def _pipelined_key_blocks(i, scores, consume, mask_first):
    scores(0, 0)

    def pair(t, carry):
        @pl.when(t == 0)
        def _():
            mask_first()

        scores(2 * t + 1, 1)
        consume(2 * t, 0, False)
        scores(2 * t + 2, 0)
        consume(2 * t + 1, 1, False)
        return carry

    lax.fori_loop(0, lax.shift_right_logical(i, 1), pair, 0)
    odd = (i & 1) == 1

    @pl.when(jnp.logical_not(odd))
    def _():
        consume(i, 0, True)

    @pl.when(odd)
    def _():
        scores(i, 1)
        consume(i - 1, 0, True)
        consume(i, 1, True)


def _head_out(acc):
    lane = lax.broadcasted_iota(jnp.int32, acc.shape, 1)
    return jnp.where(lane < ONE_LANE, acc / acc[:, ONE_LANE:ONE_LANE + 1], 0.0)


def _head_rms(o, g):
    ms = jnp.sum(o * o, axis=-1, keepdims=True) * (1.0 / ONE_LANE)
    return o * lax.rsqrt(ms + EPS) * g


def _fox_attn_kernel(q_ref, k_ref, vt_ref, g_ref, o_ref, m_ref, acc_ref, sa_ref, sb_ref,
                     *, blk):
    i = pl.program_id(2)
    q = q_ref[...]
    bufs = (sa_ref, sb_ref)
    m_ref[...] = jnp.full(m_ref.shape, NEG, F32)
    acc_ref[...] = jnp.zeros_like(acc_ref)

    def scores(j, buf):
        bufs[buf][...] = lax.dot_general(k_ref[pl.ds(_block_start(j, blk), blk), :], q, _NT,
                                         preferred_element_type=F32)

    def consume(j, buf, masked):
        s = bufs[buf][...]
        if masked:
            s = jnp.where(_causal_valid(i, j, blk), s, NEG)
        p, alpha, m_new = _softmax_block(s, m_ref[...])
        acc_ref[...] = alpha * acc_ref[...] + jnp.dot(
            vt_ref[0, 0, j], p, preferred_element_type=F32)
        m_ref[...] = m_new

    def mask_first():
        sa_ref[...] = jnp.where(_causal_valid(i, 0, blk), sa_ref[...], NEG)

    _pipelined_key_blocks(i, scores, consume, mask_first)
    o_ref[...] = _head_rms(_head_out(acc_ref[...].T), g_ref[...]).astype(BF16)


def _value_spec(seq_len):
    nk = seq_len // ATTN_BLOCK
    return pl.BlockSpec((1, 1, nk, SLOT, ATTN_BLOCK), lambda b, h, i: (b, h, 0, 0, 0))


def _fox_attn(fqk, fvt, gain, batch, seq_len):
    n = fqk.shape[0]
    blk = ATTN_BLOCK
    nq = seq_len // blk
    return pl.pallas_call(
        functools.partial(_fox_attn_kernel, blk=blk),
        grid=(batch, FX_HEADS, nq),
        in_specs=[pl.BlockSpec((blk, SLOT), lambda b, h, i: (b * nq + i, h)),
                  pl.BlockSpec((seq_len, SLOT), lambda b, h, i: (b, FX_HEADS + h)),
                  _value_spec(seq_len),
                  pl.BlockSpec((1, SLOT), lambda b, h, i: (0, 0))],
        out_specs=pl.BlockSpec((blk, SLOT), lambda b, h, i: (b * nq + i, h)),
        out_shape=jax.ShapeDtypeStruct((n, HEAD_W), BF16),
        scratch_shapes=[pltpu.VMEM((1, blk), F32), pltpu.VMEM((SLOT, blk), F32),
                        pltpu.VMEM((blk, blk), F32), pltpu.VMEM((blk, blk), F32)],
        compiler_params=pltpu.CompilerParams(
            dimension_semantics=("arbitrary", "arbitrary", "arbitrary"),
            vmem_limit_bytes=VMEM_LIMIT),
        name="fox_attn",
    )(fqk, fqk, fvt, gain)


def _diff_attn_kernel(q_ref, k_ref, vt_ref, lam_ref, g_ref, o_ref, m_ref, acc_ref,
                      sa_ref, sb_ref, *, blk, lam_init):
    i = pl.program_id(2)
    q = q_ref[...]
    bufs = (sa_ref, sb_ref)
    m_ref[...] = jnp.full(m_ref.shape, NEG, F32)
    acc_ref[...] = jnp.zeros_like(acc_ref)

    def scores(j, buf):
        k0 = _block_start(j, blk)
        for c in range(2):
            bufs[buf][c] = lax.dot_general(
                k_ref[pl.ds(k0, blk), c * SLOT:(c + 1) * SLOT], q, _NT,
                preferred_element_type=F32)

    def consume(j, buf, masked):
        mask = _causal_valid(i, j, blk) if masked else None
        ps, alphas = [], []
        for c in range(2):
            s = bufs[buf][c]
            if masked:
                s = jnp.where(mask, s, NEG)
            p, alpha, m_new = _softmax_block(s, m_ref[c])
            m_ref[c] = m_new
            ps.append(p)
            alphas.append(alpha)
        pv = jnp.dot(vt_ref[0, 0, j], jnp.concatenate(ps, axis=1),
                     preferred_element_type=F32)
        for c in range(2):
            acc_ref[c] = alphas[c] * acc_ref[c] + pv[:, c * blk:(c + 1) * blk]

    def mask_first():
        mask = _causal_valid(i, 0, blk)
        for c in range(2):
            sa_ref[c] = jnp.where(mask, sa_ref[c], NEG)

    _pipelined_key_blocks(i, scores, consume, mask_first)
    lv = lam_ref[...]
    lam = (jnp.exp(jnp.sum(lv[0:1] * lv[1:2], axis=-1, keepdims=True))
           - jnp.exp(jnp.sum(lv[2:3] * lv[3:4], axis=-1, keepdims=True)) + lam_init)
    o = _head_out(acc_ref[0].T) - lam * _head_out(acc_ref[1].T)
    o_ref[...] = (_head_rms(o, g_ref[...]) * (1.0 - lam_init)).astype(BF16)


def _diff_attn(dqk, dvt, lam_vecs, gain, lam_init, batch, seq_len):
    n = dqk.shape[0]
    blk = ATTN_BLOCK
    nq = seq_len // blk
    return pl.pallas_call(
        functools.partial(_diff_attn_kernel, blk=blk, lam_init=lam_init),
        grid=(batch, DA_HEADS, nq),
        in_specs=[pl.BlockSpec((blk, SLOT), lambda b, h, i: (b * nq + i, h)),
                  pl.BlockSpec((seq_len, 2 * SLOT), lambda b, h, i: (b, 2 + h)),
                  _value_spec(seq_len),
                  pl.BlockSpec((8, LANES), lambda b, h, i: (0, 0)),
                  pl.BlockSpec((1, SLOT), lambda b, h, i: (0, 0))],
        out_specs=pl.BlockSpec((blk, SLOT), lambda b, h, i: (b * nq + i, h)),
        out_shape=jax.ShapeDtypeStruct((n, HEAD_W), BF16),
        scratch_shapes=[pltpu.VMEM((2, 1, blk), F32), pltpu.VMEM((2, SLOT, blk), F32),
                        pltpu.VMEM((2, blk, blk), F32), pltpu.VMEM((2, blk, blk), F32)],
        compiler_params=pltpu.CompilerParams(
            dimension_semantics=("arbitrary", "arbitrary", "arbitrary"),
            vmem_limit_bytes=VMEM_LIMIT),
        name="diff_attn",
    )(dqk, dqk, dvt, lam_vecs, gain)


def _hgrn_kernel(q_ref, i_ref, g_ref, f_ref, lb_ref, gain_ref, o_ref,
                 st_ref, hs_ref, *, tl):
    j = pl.program_id(2)

    @pl.when(j == 0)
    def _():
        st_ref[...] = jnp.zeros_like(st_ref)

    c_rows, sub = HG_CHUNK, HG_SUB
    tri = (lax.broadcasted_iota(jnp.int32, (c_rows, c_rows), 1)
           <= lax.broadcasted_iota(jnp.int32, (c_rows, c_rows), 0)).astype(F32)
    row_c = lax.broadcasted_iota(jnp.int32, (c_rows, 1), 0)
    t_sub = lax.broadcasted_iota(jnp.int32, (sub, 1), 0)
    lane_c = lax.broadcasted_iota(jnp.int32, (sub, c_rows), 1)

    t_all = lax.broadcasted_iota(jnp.int32, (c_rows, c_rows), 0)
    s_all = lax.broadcasted_iota(jnp.int32, (c_rows, c_rows), 1)
    level_masks = []
    g = sub
    while g < c_rows:
        level_masks.append(((t_all // g) == (s_all // g) + 1) & ((s_all // g) % 2 == 0))
        g *= 2

    heads = range(HG_PER_STEP)
    cols = [slice(hh * HG_DK, (hh + 1) * HG_DK) for hh in heads]

    def chunk(c, carry):
        r0 = pl.multiple_of(c * c_rows, c_rows)
        rows = pl.ds(r0, c_rows)
        valid = (j * tl + r0 + row_c) >= LPAD
        G, kk, qs, v, st, o_inter = [], [], [], [], [], []
        for hh in heads:
            lb = lb_ref[:, cols[hh]]
            sig = _sigmoid(f_ref[rows, cols[hh]])
            log_f = jnp.where(valid, jnp.log2(jnp.maximum(lb + (1.0 - lb) * sig, TINY)), 0.0)
            kk.append(jnp.where(valid, (1.0 - lb) * (1.0 - sig), 0.0))
            G.append(jnp.dot(tri, log_f, precision=lax.Precision.HIGHEST,
                             preferred_element_type=F32))
            qf = q_ref[rows, cols[hh]].astype(F32)
            qs.append(qf * _sigmoid(qf) * (HG_DK ** -0.5))
            v.append(i_ref[rows, cols[hh]])
            hs_ref[hh] = G[hh] - jnp.log2(kk[hh])
            st.append(st_ref[hh])
            o_inter.append(lax.dot_general(
                (qs[hh] * jnp.exp2(G[hh])).astype(BF16), st[hh].astype(BF16), _NT,
                preferred_element_type=F32))
        a_mat = [jnp.zeros((c_rows, c_rows), F32) for _ in heads]
        g = sub
        for mask in level_masks:
            for hh in heads:
                ref = jnp.concatenate(
                    [jnp.broadcast_to(G[hh][p + g - 1:p + g], (2 * g, HG_DK))
                     for p in range(0, c_rows, 2 * g)], axis=0)
                e = jnp.exp2(-jnp.abs(G[hh] - ref))
                part = lax.dot_general((qs[hh] * e).astype(BF16), (kk[hh] * e).astype(BF16),
                                       _NT, preferred_element_type=F32)
                a_mat[hh] = jnp.where(mask, part, a_mat[hh])
            g *= 2
        a_rows = [[] for _ in heads]
        for b in range(c_rows // sub):
            lo = b * sub
            for hh in heads:
                q_b = qs[hh][lo:lo + sub]
                g_b = G[hh][lo:lo + sub]
                a_blk = a_mat[hh][lo:lo + sub]
                for s in range(lo, lo + sub):
                    y = q_b * jnp.exp2(g_b - hs_ref[hh, s:s + 1, :])
                    a_blk = jnp.where(lane_c == s, jnp.sum(y, axis=-1, keepdims=True), a_blk)
                a_rows[hh].append(jnp.where(lane_c <= t_sub + lo, a_blk, 0.0))
        for hh in heads:
            a_full = jnp.concatenate(a_rows[hh], axis=0).astype(BF16)
            o = o_inter[hh] + jnp.dot(a_full, v[hh], preferred_element_type=F32)
            g_last = G[hh][c_rows - 1:c_rows]
            kd = kk[hh] * jnp.exp2(g_last - G[hh])
            st_ref[hh] = st[hh] * jnp.exp2(g_last) + lax.dot_general(
                v[hh], kd.astype(BF16), _TN, preferred_element_type=F32)
            gate = g_ref[rows, cols[hh]].astype(F32)
            o = _rms(o, gain_ref[...]) * (gate * _sigmoid(gate))
            o_ref[rows, cols[hh]] = o.astype(BF16)
        return carry

    lax.fori_loop(0, tl // c_rows, chunk, 0)


def _hgrn(hqig, hf, lb_row, gain, batch, seq_len):
    n = hqig.shape[0]
    tl = ROW_TILE
    nt = seq_len // tl
    w = HG_PER_STEP * HG_DK
    groups = HG_HEADS // HG_PER_STEP
    col = lambda off: pl.BlockSpec((tl, w), lambda b, h, j: (b * nt + j, off + h))
    return pl.pallas_call(
        functools.partial(_hgrn_kernel, tl=tl),
        grid=(batch, groups, nt),
        in_specs=[col(0), col(groups), col(2 * groups), col(0),
                  pl.BlockSpec((1, w), lambda b, h, j: (0, h)),
                  pl.BlockSpec((1, HG_DK), lambda b, h, j: (0, 0))],
        out_specs=col(0),
        out_shape=jax.ShapeDtypeStruct((n, HG_WIDTH), BF16),
        scratch_shapes=[pltpu.VMEM((HG_PER_STEP, HG_DK, HG_DK), F32),
                        pltpu.VMEM((HG_PER_STEP, HG_CHUNK, HG_DK), F32)],
        compiler_params=pltpu.CompilerParams(
            dimension_semantics=("arbitrary", "arbitrary", "arbitrary")),
        name="hgrn2",
    )(hqig, hqig, hqig, hf, lb_row, gain)


def _out_router_kernel(oa_ref, ob_ref, oc_ref, h_ref, wo_ref, g_ref, wrh_ref, wrl_ref, br_ref,
                       h1_ref, u_ref, route_ref, cnt_ref, carry_ref):
    @pl.when(pl.program_id(0) == 0)
    def _():
        carry_ref[...] = jnp.zeros_like(carry_ref)

    h1 = (h_ref[...]
          + jnp.dot(oa_ref[...], wo_ref[0:512, :], preferred_element_type=F32)
          + jnp.dot(ob_ref[...], wo_ref[512:1024, :], preferred_element_type=F32)
          + jnp.dot(oc_ref[...], wo_ref[1024:1536, :], preferred_element_type=F32))
    h1_ref[...] = h1
    u = _rms(h1, g_ref[...])
    u_ref[...] = u
    u_hi = u.astype(BF16)
    u_lo = (u - u_hi.astype(F32)).astype(BF16)
    logits = (jnp.dot(u_hi, wrh_ref[...], preferred_element_type=F32)
              + (jnp.dot(u_hi, wrl_ref[...], preferred_element_type=F32)
                 + jnp.dot(u_lo, wrh_ref[...], preferred_element_type=F32))
              + br_ref[...])
    tm = logits.shape[0]
    lane = lax.broadcasted_iota(jnp.int32, (tm, LANES), 1).astype(F32)
    big = float(LANES)
    is_g = (lane >= N_EXPERTS) & (lane < N_EXPERTS + N_GROUPS)
    gl = jnp.where(is_g, logits, -jnp.inf)
    gmax = jnp.max(gl, axis=-1, keepdims=True)
    gsel = jnp.min(jnp.where(gl == gmax, lane, big), axis=-1, keepdims=True) - N_EXPERTS
    p_g = 1.0 / jnp.sum(jnp.exp(gl - gmax), axis=-1, keepdims=True)
    lo = gsel * EXPERTS_PER_GROUP
    el = jnp.where((lane >= lo) & (lane < lo + EXPERTS_PER_GROUP), logits, -jnp.inf)
    m1 = jnp.max(el, axis=-1, keepdims=True)
    i1 = jnp.min(jnp.where(el == m1, lane, big), axis=-1, keepdims=True)
    el2 = jnp.where(lane == i1, -jnp.inf, el)
    m2 = jnp.max(el2, axis=-1, keepdims=True)
    i2 = jnp.min(jnp.where(el2 == m2, lane, big), axis=-1, keepdims=True)
    r = jnp.exp(m2 - m1)
    gate1 = p_g / (1.0 + r)
    gate2 = gate1 * r
    oh1 = lane == i1
    oh2 = lane == i2
    onehot = jnp.where(oh1 | oh2, 1.0, 0.0)
    tri = (lax.broadcasted_iota(jnp.int32, (tm, tm), 1)
           < lax.broadcasted_iota(jnp.int32, (tm, tm), 0)).astype(BF16)
    before = jnp.dot(tri, onehot.astype(BF16), preferred_element_type=F32) + carry_ref[...]
    rank1 = jnp.sum(jnp.where(oh1, before, 0.0), axis=-1, keepdims=True)
    rank2 = jnp.sum(jnp.where(oh2, before, 0.0), axis=-1, keepdims=True)
    total = carry_ref[...] + jnp.sum(onehot, axis=0, keepdims=True)
    carry_ref[...] = total
    cnt_ref[...] = total
    route = jnp.where(lane == 0, i1, 0.0)
    for idx, val in ((1, i2), (2, gate1), (3, gate2), (4, rank1), (5, rank2)):
        route = jnp.where(lane == idx, val, route)
    route_ref[...] = route


def _out_router(oa, ob, oc, h, w_out, g, w_rt, b_rt):
    n = h.shape[0]
    w_rt_hi = w_rt.astype(BF16)
    w_rt_lo = (w_rt - w_rt_hi.astype(F32)).astype(BF16)
    tm = ROW_TILE
    row = lambda w: pl.BlockSpec((tm, w), lambda i: (i, 0))
    const = lambda r, c: pl.BlockSpec((r, c), lambda i: (0, 0))
    return pl.pallas_call(
        _out_router_kernel,
        grid=(n // tm,),
        in_specs=[row(HG_WIDTH), row(HEAD_W), row(HEAD_W), row(D_MODEL),
                  const(HG_WIDTH + 2 * HEAD_W, D_MODEL),
                  const(1, D_MODEL), const(D_MODEL, LANES), const(D_MODEL, LANES),
                  const(1, LANES)],
        out_specs=[row(D_MODEL), row(D_MODEL), row(LANES), const(1, LANES)],
        out_shape=[jax.ShapeDtypeStruct((n, D_MODEL), F32),
                   jax.ShapeDtypeStruct((n, D_MODEL), F32),
                   jax.ShapeDtypeStruct((n, LANES), F32),
                   jax.ShapeDtypeStruct((1, LANES), F32)],
        scratch_shapes=[pltpu.VMEM((1, LANES), F32)],
        compiler_params=pltpu.CompilerParams(dimension_semantics=("arbitrary",)),
        name="out_router",
    )(oa, ob, oc, h, w_out, g, w_rt_hi, w_rt_lo, b_rt)


def _expert_kernel(be_ref, na_ref, tok_ref, u_ref, w1_ref, w3_ref, w2_ref, y_ref,
                   x_ref, w1b_ref, w3b_ref, w2b_ref, sem, *, tb):
    j = pl.program_id(0)
    na = na_ref[0]
    cur = j % 2

    def gather(block, buf):
        for r in range(tb):
            pltpu.make_async_copy(u_ref.at[pl.ds(tok_ref[block * tb + r], 1)],
                                  x_ref.at[buf, pl.ds(r, 1)], sem.at[buf]).start()

    def wait_block(buf):
        pltpu.make_async_copy(u_ref.at[pl.ds(0, tb)], x_ref.at[buf], sem.at[buf]).wait()

    @pl.when(j == 0)
    def _():
        gather(0, 0)

    new_expert = (j == 0) | (be_ref[j] != be_ref[jnp.maximum(j - 1, 0)])

    @pl.when((j < na) & new_expert)
    def _():
        w1b_ref[...] = w1_ref[0].astype(BF16)
        w3b_ref[...] = w3_ref[0].astype(BF16)
        w2b_ref[...] = w2_ref[0].astype(BF16)

    @pl.when(j < na)
    def _():
        wait_block(cur)
        gather(j + 1, 1 - cur)
        x = x_ref[cur].astype(BF16)
        a = jnp.dot(x, w1b_ref[...], preferred_element_type=F32)
        b = jnp.dot(x, w3b_ref[...], preferred_element_type=F32)
        act = (a * _sigmoid(a) * b).astype(BF16)
        y_ref[...] = jnp.dot(act, w2b_ref[...], preferred_element_type=F32)

    @pl.when(j == na)
    def _():
        wait_block(cur)

    @pl.when(j >= na)
    def _():
        y_ref[...] = jnp.zeros_like(y_ref)


def _experts(blk_expert, n_active, slot_token, u, w1, w3, w2, layer):
    p_rows = slot_token.shape[0]
    tb = MOE_BLOCK
    w_spec = lambda r, c: pl.BlockSpec((None, 1, r, c),
                                       lambda j, be, na, tok: (layer, be[j], 0, 0))
    return pl.pallas_call(
        functools.partial(_expert_kernel, tb=tb),
        grid_spec=pltpu.PrefetchScalarGridSpec(
            num_scalar_prefetch=3,
            grid=(p_rows // tb,),
            in_specs=[pl.BlockSpec(memory_space=pl.ANY),
                      w_spec(D_MODEL, D_EXPERT), w_spec(D_MODEL, D_EXPERT),
                      w_spec(D_EXPERT, D_MODEL)],
            out_specs=pl.BlockSpec((tb, D_MODEL), lambda j, be, na, tok: (j, 0)),
            scratch_shapes=[pltpu.VMEM((2, tb, D_MODEL), F32),
                            pltpu.VMEM((D_MODEL, D_EXPERT), BF16),
                            pltpu.VMEM((D_MODEL, D_EXPERT), BF16),
                            pltpu.VMEM((D_EXPERT, D_MODEL), BF16),
                            pltpu.SemaphoreType.DMA((2,))]),
        out_shape=jax.ShapeDtypeStruct((p_rows, D_MODEL), F32),
        compiler_params=pltpu.CompilerParams(dimension_semantics=("arbitrary",),
                                             vmem_limit_bytes=VMEM_LIMIT),
        name="moe_experts",
    )(blk_expert, n_active, slot_token, u, w1, w3, w2)


def _row_gather(src_ref, dst_ref, sem, d, r):
    return pltpu.make_async_copy(src_ref.at[pl.ds(d, 1)], dst_ref.at[pl.ds(r, 1)], sem)


def _combine_kernel(dest_ref, h_ref, route_ref, gfin_ref, ys_ref, o_ref,
                    y0_ref, y1_ref, sem, *, tm, n, tiles_per_seq, skip_tiles, final):
    base = (pl.program_id(0) * tiles_per_seq + skip_tiles + pl.program_id(1)) * tm
    bufs = (y0_ref, y1_ref)

    def issue(r, carry):
        for k in range(2):
            _row_gather(ys_ref, bufs[k], sem, dest_ref[k * n + base + r], r).start()
        return carry

    def drain(r, carry):
        for k in range(2):
            _row_gather(ys_ref, bufs[k], sem, dest_ref[k * n + base + r], r).wait()
        return carry

    lax.fori_loop(0, tm, issue, 0)
    lax.fori_loop(0, tm, drain, 0)
    route = route_ref[...]
    out = h_ref[...] + route[:, 2:3] * y0_ref[...] + route[:, 3:4] * y1_ref[...]
    if final:
        out = _rms(out, gfin_ref[...])
    o_ref[...] = out


def _combine(dest, h1, route, g_final, ys, batch, seq_len, final):
    n = h1.shape[0]
    tm = ROW_TILE
    tiles_per_seq = seq_len // tm
    skip_tiles = (LPAD + N_META) // tm if final else 0
    out_tiles = tiles_per_seq - skip_tiles
    in_row = lambda w: pl.BlockSpec(
        (tm, w), lambda b, i, d: (b * tiles_per_seq + skip_tiles + i, 0))
    return pl.pallas_call(
        functools.partial(_combine_kernel, tm=tm, n=n, tiles_per_seq=tiles_per_seq,
                          skip_tiles=skip_tiles, final=final),
        grid_spec=pltpu.PrefetchScalarGridSpec(
            num_scalar_prefetch=1,
            grid=(batch, out_tiles),
            in_specs=[in_row(D_MODEL), in_row(LANES),
                      pl.BlockSpec((1, D_MODEL), lambda b, i, d: (0, 0)),
                      pl.BlockSpec(memory_space=pl.ANY)],
            out_specs=pl.BlockSpec((tm, D_MODEL), lambda b, i, d: (b * out_tiles + i, 0)),
            scratch_shapes=[pltpu.VMEM((tm, D_MODEL), F32), pltpu.VMEM((tm, D_MODEL), F32),
                            pltpu.SemaphoreType.DMA(())]),
        out_shape=jax.ShapeDtypeStruct((batch * out_tiles * tm, D_MODEL), F32),
        compiler_params=pltpu.CompilerParams(dimension_semantics=("arbitrary", "arbitrary")),
        name="moe_combine",
    )(dest, h1, route, g_final, ys)


def _to_slots(m):
    lead = m.shape[:-1]
    x = m.reshape(lead + (4, 64))
    x = jnp.pad(x, [(0, 0)] * (len(lead) + 1) + [(0, SLOT - 64)])
    return x.reshape(lead + (HEAD_W,))


def _to_key_slots(m):
    lead = m.shape[:-1]
    x = m.reshape(lead + (DA_HEADS, 2, DA_DQK))
    keep = [(0, 0)] * (len(lead) + 1)
    c0 = jnp.pad(x[..., 0, :], keep + [(0, SLOT - DA_DQK)])
    c1 = jnp.pad(x[..., 1, :], keep + [(DA_DQK, SLOT - 2 * DA_DQK)])
    return jnp.stack([c0, c1], axis=-2).reshape(lead + (2 * HEAD_W,))


def _rope_partner(m):
    lead = m.shape[:-1]
    x = m.reshape(lead + (DA_WIDTH // DA_DQK, 2, DA_DQK // 2))
    return x[..., ::-1, :].reshape(lead + (DA_WIDTH,))


def _rope_tables(seq_len):
    half = DA_DQK // 2
    pos = (jnp.arange(seq_len) - LPAD).astype(F32)
    inv = ROPE_THETA ** (-jnp.arange(0, DA_DQK, 2, dtype=F32) / DA_DQK)
    ang = pos[:, None] * inv[None, :]
    groups = DA_WIDTH // DA_DQK
    cos = jnp.tile(jnp.cos(ang), (1, 2 * groups))
    sin = jnp.tile(jnp.concatenate([-jnp.sin(ang), jnp.sin(ang)], axis=1), (1, groups))
    q_scale = DA_DQK ** -0.5 * LOG2E
    lay = lambda t: jnp.concatenate([_to_slots(t * q_scale), _to_key_slots(t)], axis=1)
    return lay(cos), lay(sin)


def _in_weights(w):
    hq, hf, hi, hg = (w[:, i * 512:(i + 1) * 512] for i in range(4))
    dq, dk, dv = (w[:, 2048 + i * 256:2048 + (i + 1) * 256] for i in range(3))
    fq, fk, fv = (w[:, 2816 + i * 256:2816 + (i + 1) * 256] for i in range(3))
    ff = w[:, 3584:3588]
    cat = jnp.concatenate(
        [hq, hi, hg, hf,
         _to_slots(dq), _to_key_slots(dk),
         _to_slots(_rope_partner(dq)), _to_key_slots(_rope_partner(dk)),
         _to_slots(fq * (FX_DH ** -0.5)), _to_slots(fk),
         ff, jnp.zeros((D_MODEL, LANES - FX_HEADS), w.dtype)],
        axis=1)
    w_vt = jnp.concatenate([_to_slots(dv), _to_slots(fv)], axis=1).T
    return cat.astype(BF16), w_vt.astype(BF16)


def _pad_lanes(v, width=LANES):
    return jnp.zeros((1, width), F32).at[0, :v.shape[0]].set(v.astype(F32))


def kernel(x, meta_tokens, norm_mix, w_in, hgrn_lb, hgrn_norm, diff_lambda, diff_norm,
           fox_bias, fox_norm, w_out, norm_ffn, w_group, b_group, w_router, b_router,
           w1, w3, w2, norm_final):
    batch, seq, d = x.shape
    depth = w_in.shape[0]
    seq_len = LPAD + N_META + seq
    n = batch * seq_len
    pad = jnp.zeros((batch, LPAD, d), x.dtype)
    meta = jnp.broadcast_to(meta_tokens.astype(x.dtype)[None], (batch, N_META, d))
    h = jnp.concatenate([pad, meta, x], axis=1).reshape(n, d)

    s_lb = jax.nn.softmax(hgrn_lb.astype(F32), axis=0)
    lb_all = jnp.cumsum(s_lb, axis=0) - s_lb[0]
    cos_t, sin_t = _rope_tables(seq_len)

    tb = MOE_BLOCK
    n_blocks = (2 * n) // tb + N_EXPERTS
    p_rows = n_blocks * tb

    for layer in range(depth):
        lam_init = 0.8 - 0.6 * math.exp(-0.3 * layer)
        w_cat, w_vt = _in_weights(w_in[layer])
        hqig, hf, dqk, fqk, ff, dvt, fvt = _in_proj(
            h, norm_mix[layer][None, :], w_cat, w_vt, cos_t, sin_t, batch, seq_len)

        o_a = _hgrn(hqig, hf, lb_all[layer][None, :], hgrn_norm[layer][None, :],
                    batch, seq_len)

        lam_vecs = jnp.zeros((8, LANES), F32).at[:4, :DA_DQK].set(diff_lambda[layer].astype(F32))
        o_b = _diff_attn(dqk, dvt, lam_vecs, _pad_lanes(diff_norm[layer]), lam_init,
                         batch, seq_len)

        fqk_aug = _fox_prep(ff.reshape(batch, seq_len, LANES), _pad_lanes(fox_bias[layer]),
                            fqk, batch, seq_len)
        o_c = _fox_attn(fqk_aug, fvt, _pad_lanes(fox_norm[layer]), batch, seq_len)

        wo = w_out[layer]
        slot_rows = lambda m: _to_slots(m.T).T
        wo_cat = jnp.concatenate(
            [wo[:HG_WIDTH], slot_rows(wo[HG_WIDTH:HG_WIDTH + DA_WIDTH]),
             slot_rows(wo[HG_WIDTH + DA_WIDTH:])], axis=0).astype(BF16)
        w_rt = jnp.concatenate(
            [w_router[layer], w_group[layer],
             jnp.zeros((d, LANES - N_EXPERTS - N_GROUPS), F32)], axis=1)
        b_rt = _pad_lanes(jnp.concatenate([b_router[layer], b_group[layer]]))
        h1, u2, route, counts = _out_router(
            o_a, o_b, o_c, h, wo_cat, norm_ffn[layer][None, :], w_rt, b_rt)

        cnt = counts[0, :N_EXPERTS].astype(jnp.int32)
        padded = (cnt + tb - 1) // tb * tb
        p_end = jnp.cumsum(padded)
        p_start = p_end - padded
        ids = route[:, 0:2].astype(jnp.int32)
        ranks = route[:, 4:6].astype(jnp.int32)
        dest = (p_start[ids] + ranks).T.reshape(2 * n)
        blk_start = jnp.arange(n_blocks, dtype=jnp.int32) * tb
        blk_expert = jnp.minimum(
            jnp.sum((p_end[None, :] <= blk_start[:, None]).astype(jnp.int32), axis=1),
            N_EXPERTS - 1)
        n_active = (p_end[-1:] // tb).astype(jnp.int32)

        slot_token = jnp.zeros((p_rows,), jnp.int32).at[dest].set(
            jnp.tile(jnp.arange(n, dtype=jnp.int32), 2))
        ys = _experts(blk_expert, n_active, slot_token, u2, w1, w3, w2, layer)
        h = _combine(dest, h1, route, norm_final[None, :], ys, batch, seq_len,
                     final=(layer == depth - 1))

    return h.reshape(batch, seq, d)
```

```python
import functools
import math

import jax
import jax.numpy as jnp
from jax import lax
from jax.experimental import pallas as pl
from jax.experimental.pallas import tpu as pltpu

F32 = jnp.float32
BF16 = jnp.bfloat16

D_MODEL = 1024
N_META = 16
HG_HEADS = 4
HG_DK = 128
HG_WIDTH = 512
HG_CHUNK = 64
HG_SUB = 8
HG_PER_STEP = 4
DA_HEADS = 4
DA_DV = 64
DA_DQK = 32
DA_WIDTH = 256
FX_HEADS = 4
FX_DH = 64
FX_WIDTH = 256
N_GROUPS = 4
EXPERTS_PER_GROUP = 8
N_EXPERTS = 32
D_EXPERT = 512
ROPE_THETA = 10000.0
EPS = 1e-6
NEG = -1e30
TINY = 1e-30
LOG2E = 1.4426950408889634

LANES = 128
TOKEN_TILE = (D_MODEL // LANES, LANES)
SLOT = LANES
HEAD_W = 4 * SLOT
ROW_TILE = 256
PROJ_TILE = 512
ATTN_BLOCK = 512
MOE_BLOCK = 256
LPAD = ATTN_BLOCK - N_META
ONE_LANE = 64
VMEM_LIMIT = 56 * 1024 * 1024

_C_HQIG = (0, 1536)
_C_HF = (1536, 2048)
_C_DQK = (2048, 3584)
_C_DQKP = (3584, 5120)
_C_FQ = (5120, 5632)
_C_FK = (5632, 6144)
_C_FF = (6144, 6272)
IN_COLS = 6272

_NT = (((1,), (1,)), ((), ()))
_TN = (((0,), (0,)), ((), ()))


def _store_token_tiles(ref, x):
    for c in range(TOKEN_TILE[0]):
        ref[:, c, :] = x[:, c * LANES:(c + 1) * LANES]


def _load_token_tiles(ref):
    return jnp.concatenate([ref[:, c, :] for c in range(TOKEN_TILE[0])], axis=1)


def _sigmoid(x):
    return 1.0 / (1.0 + jnp.exp(-x))


def _rms(x, g):
    return x * lax.rsqrt(jnp.mean(x * x, axis=-1, keepdims=True) + EPS) * g


def _in_proj_kernel(h_ref, g_ref, w_ref, wvt_ref, cos_ref, sin_ref,
                    hqig_ref, hf_ref, dqk_ref, fqk_ref, ff_ref, dvt_ref, fvt_ref):
    u = _rms(h_ref[...], g_ref[...]).astype(BF16)

    def mm(c):
        return jnp.dot(u, w_ref[:, c[0]:c[1]], preferred_element_type=F32)

    hqig_ref[...] = mm(_C_HQIG).astype(BF16)
    hf_ref[...] = mm(_C_HF)
    dqk_ref[...] = (mm(_C_DQK) * cos_ref[...] + mm(_C_DQKP) * sin_ref[...]).astype(BF16)
    fqk_ref[:, 0:HEAD_W] = (mm(_C_FQ) * LOG2E).astype(BF16)
    fqk_ref[:, HEAD_W:2 * HEAD_W] = mm(_C_FK).astype(BF16)
    ff_ref[...] = mm(_C_FF)
    vt = lax.dot_general(wvt_ref[...], u, _NT, preferred_element_type=F32)
    slot_row = lax.broadcasted_iota(jnp.int32, vt.shape, 0) % SLOT
    vt = jnp.where(slot_row == ONE_LANE, 1.0, vt).astype(BF16)
    for h in range(4):
        dvt_ref[0, h, 0] = vt[h * SLOT:(h + 1) * SLOT]
        fvt_ref[0, h, 0] = vt[HEAD_W + h * SLOT:HEAD_W + (h + 1) * SLOT]


def _in_proj(h, g, w_cat, w_vt, cos_t, sin_t, batch, seq_len):
    n = h.shape[0]
    tm = PROJ_TILE
    assert tm == ATTN_BLOCK
    nk = seq_len // tm
    row = lambda w: pl.BlockSpec((tm, w), lambda i: (i, 0))
    tab = pl.BlockSpec((tm, 1536), lambda i: (i % nk, 0))
    once = lambda r, c: pl.BlockSpec((r, c), lambda i: (0, 0), pipeline_mode=pl.Buffered(1))
    widths = (1536, 512, 1536, 2 * HEAD_W, LANES)
    dtypes = (BF16, F32, BF16, BF16, F32)
    vt_spec = pl.BlockSpec((1, 4, 1, SLOT, tm), lambda i: (i // nk, 0, i % nk, 0, 0))
    vt_shape = jax.ShapeDtypeStruct((batch, 4, nk, SLOT, tm), BF16)
    return pl.pallas_call(
        _in_proj_kernel,
        grid=(n // tm,),
        in_specs=[row(D_MODEL),
                  pl.BlockSpec((1, D_MODEL), lambda i: (0, 0)),
                  once(D_MODEL, IN_COLS), once(2 * HEAD_W, D_MODEL),
                  tab, tab],
        out_specs=[row(w) for w in widths] + [vt_spec, vt_spec],
        out_shape=[jax.ShapeDtypeStruct((n, w), t) for w, t in zip(widths, dtypes)]
        + [vt_shape, vt_shape],
        compiler_params=pltpu.CompilerParams(dimension_semantics=("arbitrary",),
                                             vmem_limit_bytes=VMEM_LIMIT),
        name="in_proj",
    )(h, g, w_cat, w_vt, cos_t, sin_t)


def _bf16_split3(x):
    hi = x.astype(BF16).astype(F32)
    r = x - hi
    mid = r.astype(BF16).astype(F32)
    return hi, mid, r - mid


def _fox_prep_kernel(ff_ref, b_ref, qk_ref, o_ref, carry_ref):
    @pl.when(pl.program_id(1) == 0)
    def _():
        carry_ref[...] = jnp.zeros_like(carry_ref)

    x = ff_ref[0] + b_ref[...]
    lf = jnp.minimum(x, 0.0) - jnp.log(1.0 + jnp.exp(-jnp.abs(x)))
    t = x.shape[0]
    tri = (lax.broadcasted_iota(jnp.int32, (t, t), 1)
           <= lax.broadcasted_iota(jnp.int32, (t, t), 0)).astype(F32)
    cs = jnp.dot(tri, lf, precision=lax.Precision.HIGHEST,
                 preferred_element_type=F32) + carry_ref[...]
    carry_ref[...] = cs[t - 1:t, :]
    parts = _bf16_split3(cs * LOG2E)
    lane = lax.broadcasted_iota(jnp.int32, (t, SLOT), 1)
    d0 = FX_DH
    for h in range(FX_HEADS):
        q = qk_ref[:, h * SLOT:(h + 1) * SLOT].astype(F32)
        k = qk_ref[:, HEAD_W + h * SLOT:HEAD_W + (h + 1) * SLOT].astype(F32)
        for i, part in enumerate(parts):
            col = part[:, h:h + 1]
            q = jnp.where(lane == d0 + i, col, q)
            k = jnp.where(lane == d0 + 3 + i, -col, k)
        q = jnp.where((lane >= d0 + 3) & (lane < d0 + 6), 1.0, q)
        k = jnp.where((lane >= d0) & (lane < d0 + 3), 1.0, k)
        o_ref[:, h * SLOT:(h + 1) * SLOT] = q.astype(BF16)
        o_ref[:, HEAD_W + h * SLOT:HEAD_W + (h + 1) * SLOT] = k.astype(BF16)


def _fox_prep(ff, bias_row, fqk, batch, seq_len):
    n = fqk.shape[0]
    t = ROW_TILE
    nt = seq_len // t
    return pl.pallas_call(
        _fox_prep_kernel,
        grid=(batch, nt),
        in_specs=[pl.BlockSpec((1, t, LANES), lambda b, j: (b, j, 0)),
                  pl.BlockSpec((1, LANES), lambda b, j: (0, 0)),
                  pl.BlockSpec((t, 2 * HEAD_W), lambda b, j: (b * nt + j, 0))],
        out_specs=pl.BlockSpec((t, 2 * HEAD_W), lambda b, j: (b * nt + j, 0)),
        out_shape=jax.ShapeDtypeStruct((n, 2 * HEAD_W), BF16),
        scratch_shapes=[pltpu.VMEM((1, LANES), F32)],
        compiler_params=pltpu.CompilerParams(dimension_semantics=("arbitrary", "arbitrary")),
        name="fox_prep",
    )(ff, bias_row, fqk)


def _softmax_block(s, m_prev):
    m_new = jnp.maximum(m_prev, jnp.max(s, axis=0, keepdims=True))
    alpha = jnp.exp2(m_prev - m_new)
    p = jnp.exp2(s - m_new).astype(BF16)
    return p, alpha, m_new


def _block_start(j, blk):
    return j * blk if isinstance(j, int) else pl.multiple_of(j * blk, blk)


def _causal_valid(i, j, blk):
    k_idx = j * blk + lax.broadcasted_iota(jnp.int32, (blk, blk), 0)
    q_idx = i * blk + lax.broadcasted_iota(jnp.int32, (blk, blk), 1)
    return (k_idx <= q_idx) & (k_idx >= LPAD)


def _pipelined_key_blocks(i, scores, consume, mask_first):
    scores(0, 0)

    def pair(t, carry):
        @pl.when(t == 0)
        def _():
            mask_first()

        scores(2 * t + 1, 1)
        consume(2 * t, 0, False)
        scores(2 * t + 2, 0)
        consume(2 * t + 1, 1, False)
        return carry

    lax.fori_loop(0, lax.shift_right_logical(i, 1), pair, 0)
    odd = (i & 1) == 1

    @pl.when(jnp.logical_not(odd))
    def _():
        consume(i, 0, True)

    @pl.when(odd)
    def _():
        scores(i, 1)
        consume(i - 1, 0, True)
        consume(i, 1, True)


def _head_out(acc):
    lane = lax.broadcasted_iota(jnp.int32, acc.shape, 1)
    return jnp.where(lane < ONE_LANE, acc / acc[:, ONE_LANE:ONE_LANE + 1], 0.0)


def _head_rms(o, g):
    ms = jnp.sum(o * o, axis=-1, keepdims=True) * (1.0 / ONE_LANE)
    return o * lax.rsqrt(ms + EPS) * g


def _fox_attn_kernel(q_ref, k_ref, vt_ref, g_ref, o_ref, m_ref, acc_ref, sa_ref, sb_ref,
                     *, blk):
    i = pl.program_id(2)
    q = q_ref[...]
    bufs = (sa_ref, sb_ref)
    m_ref[...] = jnp.full(m_ref.shape, NEG, F32)
    acc_ref[...] = jnp.zeros_like(acc_ref)

    def scores(j, buf):
        bufs[buf][...] = lax.dot_general(k_ref[pl.ds(_block_start(j, blk), blk), :], q, _NT,
                                         preferred_element_type=F32)

    def consume(j, buf, masked):
        s = bufs[buf][...]
        if masked:
            s = jnp.where(_causal_valid(i, j, blk), s, NEG)
        p, alpha, m_new = _softmax_block(s, m_ref[...])
        acc_ref[...] = alpha * acc_ref[...] + jnp.dot(
            vt_ref[0, 0, j], p, preferred_element_type=F32)
        m_ref[...] = m_new

    def mask_first():
        sa_ref[...] = jnp.where(_causal_valid(i, 0, blk), sa_ref[...], NEG)

    _pipelined_key_blocks(i, scores, consume, mask_first)
    o_ref[...] = _head_rms(_head_out(acc_ref[...].T), g_ref[...]).astype(BF16)


def _value_spec(seq_len):
    nk = seq_len // ATTN_BLOCK
    return pl.BlockSpec((1, 1, nk, SLOT, ATTN_BLOCK), lambda b, h, i: (b, h, 0, 0, 0))


def _fox_attn(fqk, fvt, gain, batch, seq_len):
    n = fqk.shape[0]
    blk = ATTN_BLOCK
    nq = seq_len // blk
    return pl.pallas_call(
        functools.partial(_fox_attn_kernel, blk=blk),
        grid=(batch, FX_HEADS, nq),
        in_specs=[pl.BlockSpec((blk, SLOT), lambda b, h, i: (b * nq + i, h)),
                  pl.BlockSpec((seq_len, SLOT), lambda b, h, i: (b, FX_HEADS + h)),
                  _value_spec(seq_len),
                  pl.BlockSpec((1, SLOT), lambda b, h, i: (0, 0))],
        out_specs=pl.BlockSpec((blk, SLOT), lambda b, h, i: (b * nq + i, h)),
        out_shape=jax.ShapeDtypeStruct((n, HEAD_W), BF16),
        scratch_shapes=[pltpu.VMEM((1, blk), F32), pltpu.VMEM((SLOT, blk), F32),
                        pltpu.VMEM((blk, blk), F32), pltpu.VMEM((blk, blk), F32)],
        compiler_params=pltpu.CompilerParams(
            dimension_semantics=("arbitrary", "arbitrary", "arbitrary"),
            vmem_limit_bytes=VMEM_LIMIT),
        name="fox_attn",
    )(fqk, fqk, fvt, gain)


def _diff_attn_kernel(q_ref, k_ref, vt_ref, lam_ref, g_ref, o_ref, m_ref, acc_ref,
                      sa_ref, sb_ref, *, blk, lam_init):
    i = pl.program_id(2)
    q = q_ref[...]
    bufs = (sa_ref, sb_ref)
    m_ref[...] = jnp.full(m_ref.shape, NEG, F32)
    acc_ref[...] = jnp.zeros_like(acc_ref)

    def scores(j, buf):
        k0 = _block_start(j, blk)
        for c in range(2):
            bufs[buf][c] = lax.dot_general(
                k_ref[pl.ds(k0, blk), c * SLOT:(c + 1) * SLOT], q, _NT,
                preferred_element_type=F32)

    def consume(j, buf, masked):
        mask = _causal_valid(i, j, blk) if masked else None
        ps, alphas = [], []
        for c in range(2):
            s = bufs[buf][c]
            if masked:
                s = jnp.where(mask, s, NEG)
            p, alpha, m_new = _softmax_block(s, m_ref[c])
            m_ref[c] = m_new
            ps.append(p)
            alphas.append(alpha)
        pv = jnp.dot(vt_ref[0, 0, j], jnp.concatenate(ps, axis=1),
                     preferred_element_type=F32)
        for c in range(2):
            acc_ref[c] = alphas[c] * acc_ref[c] + pv[:, c * blk:(c + 1) * blk]

    def mask_first():
        mask = _causal_valid(i, 0, blk)
        for c in range(2):
            sa_ref[c] = jnp.where(mask, sa_ref[c], NEG)

    _pipelined_key_blocks(i, scores, consume, mask_first)
    lv = lam_ref[...]
    lam = (jnp.exp(jnp.sum(lv[0:1] * lv[1:2], axis=-1, keepdims=True))
           - jnp.exp(jnp.sum(lv[2:3] * lv[3:4], axis=-1, keepdims=True)) + lam_init)
    o = _head_out(acc_ref[0].T) - lam * _head_out(acc_ref[1].T)
    o_ref[...] = (_head_rms(o, g_ref[...]) * (1.0 - lam_init)).astype(BF16)


def _diff_attn(dqk, dvt, lam_vecs, gain, lam_init, batch, seq_len):
    n = dqk.shape[0]
    blk = ATTN_BLOCK
    nq = seq_len // blk
    return pl.pallas_call(
        functools.partial(_diff_attn_kernel, blk=blk, lam_init=lam_init),
        grid=(batch, DA_HEADS, nq),
        in_specs=[pl.BlockSpec((blk, SLOT), lambda b, h, i: (b * nq + i, h)),
                  pl.BlockSpec((seq_len, 2 * SLOT), lambda b, h, i: (b, 2 + h)),
                  _value_spec(seq_len),
                  pl.BlockSpec((8, LANES), lambda b, h, i: (0, 0)),
                  pl.BlockSpec((1, SLOT), lambda b, h, i: (0, 0))],
        out_specs=pl.BlockSpec((blk, SLOT), lambda b, h, i: (b * nq + i, h)),
        out_shape=jax.ShapeDtypeStruct((n, HEAD_W), BF16),
        scratch_shapes=[pltpu.VMEM((2, 1, blk), F32), pltpu.VMEM((2, SLOT, blk), F32),
                        pltpu.VMEM((2, blk, blk), F32), pltpu.VMEM((2, blk, blk), F32)],
        compiler_params=pltpu.CompilerParams(
            dimension_semantics=("arbitrary", "arbitrary", "arbitrary"),
            vmem_limit_bytes=VMEM_LIMIT),
        name="diff_attn",
    )(dqk, dqk, dvt, lam_vecs, gain)


def _hgrn_kernel(q_ref, i_ref, g_ref, f_ref, lb_ref, gain_ref, o_ref,
                 st_ref, hs_ref, *, tl):
    j = pl.program_id(2)

    @pl.when(j == 0)
    def _():
        st_ref[...] = jnp.zeros_like(st_ref)

    c_rows, sub = HG_CHUNK, HG_SUB
    tri = (lax.broadcasted_iota(jnp.int32, (c_rows, c_rows), 1)
           <= lax.broadcasted_iota(jnp.int32, (c_rows, c_rows), 0)).astype(F32)
    row_c = lax.broadcasted_iota(jnp.int32, (c_rows, 1), 0)
    t_sub = lax.broadcasted_iota(jnp.int32, (sub, 1), 0)
    lane_c = lax.broadcasted_iota(jnp.int32, (sub, c_rows), 1)

    t_all = lax.broadcasted_iota(jnp.int32, (c_rows, c_rows), 0)
    s_all = lax.broadcasted_iota(jnp.int32, (c_rows, c_rows), 1)
    level_masks = []
    g = sub
    while g < c_rows:
        level_masks.append(((t_all // g) == (s_all // g) + 1) & ((s_all // g) % 2 == 0))
        g *= 2

    heads = range(HG_PER_STEP)
    cols = [slice(hh * HG_DK, (hh + 1) * HG_DK) for hh in heads]

    def chunk(c, carry):
        r0 = pl.multiple_of(c * c_rows, c_rows)
        rows = pl.ds(r0, c_rows)
        valid = (j * tl + r0 + row_c) >= LPAD
        G, kk, qs, v, st, o_inter = [], [], [], [], [], []
        for hh in heads:
            lb = lb_ref[:, cols[hh]]
            sig = _sigmoid(f_ref[rows, cols[hh]])
            log_f = jnp.where(valid, jnp.log2(jnp.maximum(lb + (1.0 - lb) * sig, TINY)), 0.0)
            kk.append(jnp.where(valid, (1.0 - lb) * (1.0 - sig), 0.0))
            G.append(jnp.dot(tri, log_f, precision=lax.Precision.HIGHEST,
                             preferred_element_type=F32))
            qf = q_ref[rows, cols[hh]].astype(F32)
            qs.append(qf * _sigmoid(qf) * (HG_DK ** -0.5))
            v.append(i_ref[rows, cols[hh]])
            hs_ref[hh] = G[hh] - jnp.log2(kk[hh])
            st.append(st_ref[hh])
            o_inter.append(lax.dot_general(
                (qs[hh] * jnp.exp2(G[hh])).astype(BF16), st[hh].astype(BF16), _NT,
                preferred_element_type=F32))
        a_mat = [jnp.zeros((c_rows, c_rows), F32) for _ in heads]
        g = sub
        for mask in level_masks:
            for hh in heads:
                ref = jnp.concatenate(
                    [jnp.broadcast_to(G[hh][p + g - 1:p + g], (2 * g, HG_DK))
                     for p in range(0, c_rows, 2 * g)], axis=0)
                e = jnp.exp2(-jnp.abs(G[hh] - ref))
                part = lax.dot_general((qs[hh] * e).astype(BF16), (kk[hh] * e).astype(BF16),
                                       _NT, preferred_element_type=F32)
                a_mat[hh] = jnp.where(mask, part, a_mat[hh])
            g *= 2
        a_rows = [[] for _ in heads]
        for b in range(c_rows // sub):
            lo = b * sub
            for hh in heads:
                q_b = qs[hh][lo:lo + sub]
                g_b = G[hh][lo:lo + sub]
                a_blk = a_mat[hh][lo:lo + sub]
                for s in range(lo, lo + sub):
                    y = q_b * jnp.exp2(g_b - hs_ref[hh, s:s + 1, :])
                    a_blk = jnp.where(lane_c == s, jnp.sum(y, axis=-1, keepdims=True), a_blk)
                a_rows[hh].append(jnp.where(lane_c <= t_sub + lo, a_blk, 0.0))
        for hh in heads:
            a_full = jnp.concatenate(a_rows[hh], axis=0).astype(BF16)
            o = o_inter[hh] + jnp.dot(a_full, v[hh], preferred_element_type=F32)
            g_last = G[hh][c_rows - 1:c_rows]
            kd = kk[hh] * jnp.exp2(g_last - G[hh])
            st_ref[hh] = st[hh] * jnp.exp2(g_last) + lax.dot_general(
                v[hh], kd.astype(BF16), _TN, preferred_element_type=F32)
            gate = g_ref[rows, cols[hh]].astype(F32)
            o = _rms(o, gain_ref[...]) * (gate * _sigmoid(gate))
            o_ref[rows, cols[hh]] = o.astype(BF16)
        return carry

    lax.fori_loop(0, tl // c_rows, chunk, 0)


def _hgrn(hqig, hf, lb_row, gain, batch, seq_len):
    n = hqig.shape[0]
    tl = ROW_TILE
    nt = seq_len // tl
    w = HG_PER_STEP * HG_DK
    groups = HG_HEADS // HG_PER_STEP
    col = lambda off: pl.BlockSpec((tl, w), lambda b, h, j: (b * nt + j, off + h))
    return pl.pallas_call(
        functools.partial(_hgrn_kernel, tl=tl),
        grid=(batch, groups, nt),
        in_specs=[col(0), col(groups), col(2 * groups), col(0),
                  pl.BlockSpec((1, w), lambda b, h, j: (0, h)),
                  pl.BlockSpec((1, HG_DK), lambda b, h, j: (0, 0))],
        out_specs=col(0),
        out_shape=jax.ShapeDtypeStruct((n, HG_WIDTH), BF16),
        scratch_shapes=[pltpu.VMEM((HG_PER_STEP, HG_DK, HG_DK), F32),
                        pltpu.VMEM((HG_PER_STEP, HG_CHUNK, HG_DK), F32)],
        compiler_params=pltpu.CompilerParams(
            dimension_semantics=("arbitrary", "arbitrary", "arbitrary")),
        name="hgrn2",
    )(hqig, hqig, hqig, hf, lb_row, gain)


def _out_router_kernel(oa_ref, ob_ref, oc_ref, h_ref, wo_ref, g_ref, wrh_ref, wrl_ref, br_ref,
                       h1_ref, u_ref, route_ref, cnt_ref, carry_ref):
    @pl.when(pl.program_id(0) == 0)
    def _():
        carry_ref[...] = jnp.zeros_like(carry_ref)

    h1 = (h_ref[...]
          + jnp.dot(oa_ref[...], wo_ref[0:512, :], preferred_element_type=F32)
          + jnp.dot(ob_ref[...], wo_ref[512:1024, :], preferred_element_type=F32)
          + jnp.dot(oc_ref[...], wo_ref[1024:1536, :], preferred_element_type=F32))
    h1_ref[...] = h1
    u = _rms(h1, g_ref[...])
    _store_token_tiles(u_ref, u)
    u_hi = u.astype(BF16)
    u_lo = (u - u_hi.astype(F32)).astype(BF16)
    logits = (jnp.dot(u_hi, wrh_ref[...], preferred_element_type=F32)
              + (jnp.dot(u_hi, wrl_ref[...], preferred_element_type=F32)
                 + jnp.dot(u_lo, wrh_ref[...], preferred_element_type=F32))
              + br_ref[...])
    tm = logits.shape[0]
    lane = lax.broadcasted_iota(jnp.int32, (tm, LANES), 1).astype(F32)
    big = float(LANES)
    is_g = (lane >= N_EXPERTS) & (lane < N_EXPERTS + N_GROUPS)
    gl = jnp.where(is_g, logits, -jnp.inf)
    gmax = jnp.max(gl, axis=-1, keepdims=True)
    gsel = jnp.min(jnp.where(gl == gmax, lane, big), axis=-1, keepdims=True) - N_EXPERTS
    p_g = 1.0 / jnp.sum(jnp.exp(gl - gmax), axis=-1, keepdims=True)
    lo = gsel * EXPERTS_PER_GROUP
    el = jnp.where((lane >= lo) & (lane < lo + EXPERTS_PER_GROUP), logits, -jnp.inf)
    m1 = jnp.max(el, axis=-1, keepdims=True)
    i1 = jnp.min(jnp.where(el == m1, lane, big), axis=-1, keepdims=True)
    el2 = jnp.where(lane == i1, -jnp.inf, el)
    m2 = jnp.max(el2, axis=-1, keepdims=True)
    i2 = jnp.min(jnp.where(el2 == m2, lane, big), axis=-1, keepdims=True)
    r = jnp.exp(m2 - m1)
    gate1 = p_g / (1.0 + r)
    gate2 = gate1 * r
    oh1 = lane == i1
    oh2 = lane == i2
    onehot = jnp.where(oh1 | oh2, 1.0, 0.0)
    tri = (lax.broadcasted_iota(jnp.int32, (tm, tm), 1)
           < lax.broadcasted_iota(jnp.int32, (tm, tm), 0)).astype(BF16)
    before = jnp.dot(tri, onehot.astype(BF16), preferred_element_type=F32) + carry_ref[...]
    rank1 = jnp.sum(jnp.where(oh1, before, 0.0), axis=-1, keepdims=True)
    rank2 = jnp.sum(jnp.where(oh2, before, 0.0), axis=-1, keepdims=True)
    total = carry_ref[...] + jnp.sum(onehot, axis=0, keepdims=True)
    carry_ref[...] = total
    cnt_ref[...] = total
    route = jnp.where(lane == 0, i1, 0.0)
    for idx, val in ((1, i2), (2, gate1), (3, gate2), (4, rank1), (5, rank2)):
        route = jnp.where(lane == idx, val, route)
    route_ref[...] = route


def _out_router(oa, ob, oc, h, w_out, g, w_rt, b_rt):
    n = h.shape[0]
    w_rt_hi = w_rt.astype(BF16)
    w_rt_lo = (w_rt - w_rt_hi.astype(F32)).astype(BF16)
    tm = ROW_TILE
    row = lambda w: pl.BlockSpec((tm, w), lambda i: (i, 0))
    const = lambda r, c: pl.BlockSpec((r, c), lambda i: (0, 0))
    return pl.pallas_call(
        _out_router_kernel,
        grid=(n // tm,),
        in_specs=[row(HG_WIDTH), row(HEAD_W), row(HEAD_W), row(D_MODEL),
                  const(HG_WIDTH + 2 * HEAD_W, D_MODEL),
                  const(1, D_MODEL), const(D_MODEL, LANES), const(D_MODEL, LANES),
                  const(1, LANES)],
        out_specs=[row(D_MODEL), pl.BlockSpec((tm,) + TOKEN_TILE, lambda i: (i, 0, 0)),
                   row(LANES), const(1, LANES)],
        out_shape=[jax.ShapeDtypeStruct((n, D_MODEL), F32),
                   jax.ShapeDtypeStruct((n,) + TOKEN_TILE, F32),
                   jax.ShapeDtypeStruct((n, LANES), F32),
                   jax.ShapeDtypeStruct((1, LANES), F32)],
        scratch_shapes=[pltpu.VMEM((1, LANES), F32)],
        compiler_params=pltpu.CompilerParams(dimension_semantics=("arbitrary",)),
        name="out_router",
    )(oa, ob, oc, h, w_out, g, w_rt_hi, w_rt_lo, b_rt)


def _expert_kernel(be_ref, na_ref, tok_ref, u_ref, w1_ref, w3_ref, w2_ref, y_ref,
                   x_ref, w1b_ref, w3b_ref, w2b_ref, sem, *, tb):
    j = pl.program_id(0)
    na = na_ref[0]
    cur = j % 2

    def gather(block, buf):
        for r in range(tb):
            pltpu.make_async_copy(u_ref.at[pl.ds(tok_ref[block * tb + r], 1)],
                                  x_ref.at[buf, pl.ds(r, 1)], sem.at[buf]).start()

    def wait_block(buf):
        pltpu.make_async_copy(u_ref.at[pl.ds(0, tb)], x_ref.at[buf], sem.at[buf]).wait()

    @pl.when(j == 0)
    def _():
        gather(0, 0)

    new_expert = (j == 0) | (be_ref[j] != be_ref[jnp.maximum(j - 1, 0)])

    @pl.when((j < na) & new_expert)
    def _():
        w1b_ref[...] = w1_ref[0].astype(BF16)
        w3b_ref[...] = w3_ref[0].astype(BF16)
        w2b_ref[...] = w2_ref[0].astype(BF16)

    @pl.when(j < na)
    def _():
        wait_block(cur)
        gather(j + 1, 1 - cur)
        x = _load_token_tiles(x_ref.at[cur]).astype(BF16)
        a = jnp.dot(x, w1b_ref[...], preferred_element_type=F32)
        b = jnp.dot(x, w3b_ref[...], preferred_element_type=F32)
        act = (a * _sigmoid(a) * b).astype(BF16)
        _store_token_tiles(y_ref, jnp.dot(act, w2b_ref[...], preferred_element_type=F32))

    @pl.when(j == na)
    def _():
        wait_block(cur)

    @pl.when(j >= na)
    def _():
        y_ref[...] = jnp.zeros_like(y_ref)


def _experts(blk_expert, n_active, slot_token, u, w1, w3, w2, layer):
    p_rows = slot_token.shape[0]
    tb = MOE_BLOCK
    w_spec = lambda r, c: pl.BlockSpec((None, 1, r, c),
                                       lambda j, be, na, tok: (layer, be[j], 0, 0))
    return pl.pallas_call(
        functools.partial(_expert_kernel, tb=tb),
        grid_spec=pltpu.PrefetchScalarGridSpec(
            num_scalar_prefetch=3,
            grid=(p_rows // tb,),
            in_specs=[pl.BlockSpec(memory_space=pl.ANY),
                      w_spec(D_MODEL, D_EXPERT), w_spec(D_MODEL, D_EXPERT),
                      w_spec(D_EXPERT, D_MODEL)],
            out_specs=pl.BlockSpec((tb,) + TOKEN_TILE, lambda j, be, na, tok: (j, 0, 0)),
            scratch_shapes=[pltpu.VMEM((2, tb) + TOKEN_TILE, F32),
                            pltpu.VMEM((D_MODEL, D_EXPERT), BF16),
                            pltpu.VMEM((D_MODEL, D_EXPERT), BF16),
                            pltpu.VMEM((D_EXPERT, D_MODEL), BF16),
                            pltpu.SemaphoreType.DMA((2,))]),
        out_shape=jax.ShapeDtypeStruct((p_rows,) + TOKEN_TILE, F32),
        compiler_params=pltpu.CompilerParams(dimension_semantics=("arbitrary",),
                                             vmem_limit_bytes=VMEM_LIMIT),
        name="moe_experts",
    )(blk_expert, n_active, slot_token, u, w1, w3, w2)


def _row_gather(src_ref, dst_ref, sem, d, r):
    return pltpu.make_async_copy(src_ref.at[pl.ds(d, 1)], dst_ref.at[pl.ds(r, 1)], sem)


def _combine_kernel(dest_ref, h_ref, route_ref, gfin_ref, ys_ref, o_ref,
                    y0_ref, y1_ref, sem, *, tm, n, tiles_per_seq, skip_tiles, final):
    base = (pl.program_id(0) * tiles_per_seq + skip_tiles + pl.program_id(1)) * tm
    bufs = (y0_ref, y1_ref)

    def issue(r, carry):
        for k in range(2):
            _row_gather(ys_ref, bufs[k], sem, dest_ref[k * n + base + r], r).start()
        return carry

    lax.fori_loop(0, tm, issue, 0, unroll=8)
    for k in range(2):
        pltpu.make_async_copy(ys_ref.at[pl.ds(0, tm)], bufs[k], sem).wait()
    route = route_ref[...]
    out = (h_ref[...] + route[:, 2:3] * _load_token_tiles(y0_ref)
           + route[:, 3:4] * _load_token_tiles(y1_ref))
    if final:
        out = _rms(out, gfin_ref[...])
    o_ref[...] = out


def _combine(dest, h1, route, g_final, ys, batch, seq_len, final):
    n = h1.shape[0]
    tm = ROW_TILE
    tiles_per_seq = seq_len // tm
    skip_tiles = (LPAD + N_META) // tm if final else 0
    out_tiles = tiles_per_seq - skip_tiles
    in_row = lambda w: pl.BlockSpec(
        (tm, w), lambda b, i, d: (b * tiles_per_seq + skip_tiles + i, 0))
    return pl.pallas_call(
        functools.partial(_combine_kernel, tm=tm, n=n, tiles_per_seq=tiles_per_seq,
                          skip_tiles=skip_tiles, final=final),
        grid_spec=pltpu.PrefetchScalarGridSpec(
            num_scalar_prefetch=1,
            grid=(batch, out_tiles),
            in_specs=[in_row(D_MODEL), in_row(LANES),
                      pl.BlockSpec((1, D_MODEL), lambda b, i, d: (0, 0)),
                      pl.BlockSpec(memory_space=pl.ANY)],
            out_specs=pl.BlockSpec((tm, D_MODEL), lambda b, i, d: (b * out_tiles + i, 0)),
            scratch_shapes=[pltpu.VMEM((tm,) + TOKEN_TILE, F32),
                            pltpu.VMEM((tm,) + TOKEN_TILE, F32),
                            pltpu.SemaphoreType.DMA(())]),
        out_shape=jax.ShapeDtypeStruct((batch * out_tiles * tm, D_MODEL), F32),
        compiler_params=pltpu.CompilerParams(dimension_semantics=("arbitrary", "arbitrary")),
        name="moe_combine",
    )(dest, h1, route, g_final, ys)


def _to_slots(m):
    lead = m.shape[:-1]
    x = m.reshape(lead + (4, 64))
    x = jnp.pad(x, [(0, 0)] * (len(lead) + 1) + [(0, SLOT - 64)])
    return x.reshape(lead + (HEAD_W,))


def _to_key_slots(m):
    lead = m.shape[:-1]
    x = m.reshape(lead + (DA_HEADS, 2, DA_DQK))
    keep = [(0, 0)] * (len(lead) + 1)
    c0 = jnp.pad(x[..., 0, :], keep + [(0, SLOT - DA_DQK)])
    c1 = jnp.pad(x[..., 1, :], keep + [(DA_DQK, SLOT - 2 * DA_DQK)])
    return jnp.stack([c0, c1], axis=-2).reshape(lead + (2 * HEAD_W,))


def _rope_partner(m):
    lead = m.shape[:-1]
    x = m.reshape(lead + (DA_WIDTH // DA_DQK, 2, DA_DQK // 2))
    return x[..., ::-1, :].reshape(lead + (DA_WIDTH,))


def _rope_tables(seq_len):
    half = DA_DQK // 2
    pos = (jnp.arange(seq_len) - LPAD).astype(F32)
    inv = ROPE_THETA ** (-jnp.arange(0, DA_DQK, 2, dtype=F32) / DA_DQK)
    ang = pos[:, None] * inv[None, :]
    groups = DA_WIDTH // DA_DQK
    cos = jnp.tile(jnp.cos(ang), (1, 2 * groups))
    sin = jnp.tile(jnp.concatenate([-jnp.sin(ang), jnp.sin(ang)], axis=1), (1, groups))
    q_scale = DA_DQK ** -0.5 * LOG2E
    lay = lambda t: jnp.concatenate([_to_slots(t * q_scale), _to_key_slots(t)], axis=1)
    return lay(cos), lay(sin)


def _in_weights(w):
    hq, hf, hi, hg = (w[:, i * 512:(i + 1) * 512] for i in range(4))
    dq, dk, dv = (w[:, 2048 + i * 256:2048 + (i + 1) * 256] for i in range(3))
    fq, fk, fv = (w[:, 2816 + i * 256:2816 + (i + 1) * 256] for i in range(3))
    ff = w[:, 3584:3588]
    cat = jnp.concatenate(
        [hq, hi, hg, hf,
         _to_slots(dq), _to_key_slots(dk),
         _to_slots(_rope_partner(dq)), _to_key_slots(_rope_partner(dk)),
         _to_slots(fq * (FX_DH ** -0.5)), _to_slots(fk),
         ff, jnp.zeros((D_MODEL, LANES - FX_HEADS), w.dtype)],
        axis=1)
    w_vt = jnp.concatenate([_to_slots(dv), _to_slots(fv)], axis=1).T
    return cat.astype(BF16), w_vt.astype(BF16)


def _pad_lanes(v, width=LANES):
    return jnp.zeros((1, width), F32).at[0, :v.shape[0]].set(v.astype(F32))


def kernel(x, meta_tokens, norm_mix, w_in, hgrn_lb, hgrn_norm, diff_lambda, diff_norm,
           fox_bias, fox_norm, w_out, norm_ffn, w_group, b_group, w_router, b_router,
           w1, w3, w2, norm_final):
    batch, seq, d = x.shape
    depth = w_in.shape[0]
    seq_len = LPAD + N_META + seq
    n = batch * seq_len
    pad = jnp.zeros((batch, LPAD, d), x.dtype)
    meta = jnp.broadcast_to(meta_tokens.astype(x.dtype)[None], (batch, N_META, d))
    h = jnp.concatenate([pad, meta, x], axis=1).reshape(n, d)

    s_lb = jax.nn.softmax(hgrn_lb.astype(F32), axis=0)
    lb_all = jnp.cumsum(s_lb, axis=0) - s_lb[0]
    cos_t, sin_t = _rope_tables(seq_len)

    tb = MOE_BLOCK
    n_blocks = (2 * n) // tb + N_EXPERTS
    p_rows = n_blocks * tb

    for layer in range(depth):
        lam_init = 0.8 - 0.6 * math.exp(-0.3 * layer)
        w_cat, w_vt = _in_weights(w_in[layer])
        hqig, hf, dqk, fqk, ff, dvt, fvt = _in_proj(
            h, norm_mix[layer][None, :], w_cat, w_vt, cos_t, sin_t, batch, seq_len)

        o_a = _hgrn(hqig, hf, lb_all[layer][None, :], hgrn_norm[layer][None, :],
                    batch, seq_len)

        lam_vecs = jnp.zeros((8, LANES), F32).at[:4, :DA_DQK].set(diff_lambda[layer].astype(F32))
        o_b = _diff_attn(dqk, dvt, lam_vecs, _pad_lanes(diff_norm[layer]), lam_init,
                         batch, seq_len)

        fqk_aug = _fox_prep(ff.reshape(batch, seq_len, LANES), _pad_lanes(fox_bias[layer]),
                            fqk, batch, seq_len)
        o_c = _fox_attn(fqk_aug, fvt, _pad_lanes(fox_norm[layer]), batch, seq_len)

        wo = w_out[layer]
        slot_rows = lambda m: _to_slots(m.T).T
        wo_cat = jnp.concatenate(
            [wo[:HG_WIDTH], slot_rows(wo[HG_WIDTH:HG_WIDTH + DA_WIDTH]),
             slot_rows(wo[HG_WIDTH + DA_WIDTH:])], axis=0).astype(BF16)
        w_rt = jnp.concatenate(
            [w_router[layer], w_group[layer],
             jnp.zeros((d, LANES - N_EXPERTS - N_GROUPS), F32)], axis=1)
        b_rt = _pad_lanes(jnp.concatenate([b_router[layer], b_group[layer]]))
        h1, u2, route, counts = _out_router(
            o_a, o_b, o_c, h, wo_cat, norm_ffn[layer][None, :], w_rt, b_rt)

        cnt = counts[0, :N_EXPERTS].astype(jnp.int32)
        padded = (cnt + tb - 1) // tb * tb
        p_end = jnp.cumsum(padded)
        p_start = p_end - padded
        ids = route[:, 0:2].astype(jnp.int32)
        ranks = route[:, 4:6].astype(jnp.int32)
        dest = (p_start[ids] + ranks).T.reshape(2 * n)
        blk_start = jnp.arange(n_blocks, dtype=jnp.int32) * tb
        blk_expert = jnp.minimum(
            jnp.sum((p_end[None, :] <= blk_start[:, None]).astype(jnp.int32), axis=1),
            N_EXPERTS - 1)
        n_active = (p_end[-1:] // tb).astype(jnp.int32)

        slot_token = jnp.zeros((p_rows,), jnp.int32).at[dest].set(
            jnp.tile(jnp.arange(n, dtype=jnp.int32), 2))
        ys = _experts(blk_expert, n_active, slot_token, u2, w1, w3, w2, layer)
        h = _combine(dest, h1, route, norm_final[None, :], ys, batch, seq_len,
                     final=(layer == depth - 1))

    return h.reshape(batch, seq, d)
```

```python
import functools
import math

import jax
import jax.numpy as jnp
from jax import lax
from jax.experimental import pallas as pl
from jax.experimental.pallas import tpu as pltpu

F32 = jnp.float32
BF16 = jnp.bfloat16

D_MODEL = 1024
N_META = 16
HG_HEADS = 4
HG_DK = 128
HG_WIDTH = 512
HG_CHUNK = 64
HG_SUB = 8
HG_PER_STEP = 4
DA_HEADS = 4
DA_DV = 64
DA_DQK = 32
DA_WIDTH = 256
FX_HEADS = 4
FX_DH = 64
FX_WIDTH = 256
N_GROUPS = 4
EXPERTS_PER_GROUP = 8
N_EXPERTS = 32
D_EXPERT = 512
ROPE_THETA = 10000.0
EPS = 1e-6
NEG = -1e30
TINY = 1e-30
LOG2E = 1.4426950408889634

LANES = 128
TOKEN_TILE = (D_MODEL // LANES, LANES)
SLOT = LANES
HEAD_W = 4 * SLOT
ROW_TILE = 256
PROJ_TILE = 512
ATTN_BLOCK = 512
MOE_BLOCK = 256
LPAD = ATTN_BLOCK - N_META
ONE_LANE = 64
VMEM_LIMIT = 56 * 1024 * 1024

_C_HQIG = (0, 1536)
_C_HF = (1536, 2048)
_C_DQK = (2048, 3584)
_C_DQKP = (3584, 5120)
_C_FQ = (5120, 5632)
_C_FK = (5632, 6144)
_C_FF = (6144, 6272)
IN_COLS = 6272

_NT = (((1,), (1,)), ((), ()))
_TN = (((0,), (0,)), ((), ()))


def _store_token_tiles(ref, x):
    for c in range(TOKEN_TILE[0]):
        ref[:, c, :] = x[:, c * LANES:(c + 1) * LANES]


def _load_token_tiles(ref):
    return jnp.concatenate([ref[:, c, :] for c in range(TOKEN_TILE[0])], axis=1)


def _sigmoid(x):
    return 1.0 / (1.0 + jnp.exp(-x))


def _rms(x, g):
    return x * lax.rsqrt(jnp.mean(x * x, axis=-1, keepdims=True) + EPS) * g


def _in_proj_kernel(h_ref, g_ref, w_ref, wvt_ref, cos_ref, sin_ref,
                    hqig_ref, hf_ref, dqk_ref, fqk_ref, ff_ref, dvt_ref, fvt_ref):
    u = _rms(h_ref[...], g_ref[...]).astype(BF16)

    def mm(c):
        return jnp.dot(u, w_ref[:, c[0]:c[1]], preferred_element_type=F32)

    hqig_ref[...] = mm(_C_HQIG).astype(BF16)
    hf_ref[...] = mm(_C_HF)
    dqk_ref[...] = (mm(_C_DQK) * cos_ref[...] + mm(_C_DQKP) * sin_ref[...]).astype(BF16)
    fqk_ref[:, 0:HEAD_W] = (mm(_C_FQ) * LOG2E).astype(BF16)
    fqk_ref[:, HEAD_W:2 * HEAD_W] = mm(_C_FK).astype(BF16)
    ff_ref[...] = mm(_C_FF)
    vt = lax.dot_general(wvt_ref[...], u, _NT, preferred_element_type=F32)
    slot_row = lax.broadcasted_iota(jnp.int32, vt.shape, 0) % SLOT
    vt = jnp.where(slot_row == ONE_LANE, 1.0, vt).astype(BF16)
    for h in range(4):
        dvt_ref[0, h, 0] = vt[h * SLOT:(h + 1) * SLOT]
        fvt_ref[0, h, 0] = vt[HEAD_W + h * SLOT:HEAD_W + (h + 1) * SLOT]


def _in_proj(h, g, w_cat, w_vt, cos_t, sin_t, batch, seq_len):
    n = h.shape[0]
    tm = PROJ_TILE
    assert tm == ATTN_BLOCK
    nk = seq_len // tm
    row = lambda w: pl.BlockSpec((tm, w), lambda i: (i, 0))
    tab = pl.BlockSpec((tm, 1536), lambda i: (i % nk, 0))
    once = lambda r, c: pl.BlockSpec((r, c), lambda i: (0, 0), pipeline_mode=pl.Buffered(1))
    widths = (1536, 512, 1536, 2 * HEAD_W, LANES)
    dtypes = (BF16, F32, BF16, BF16, F32)
    vt_spec = pl.BlockSpec((1, 4, 1, SLOT, tm), lambda i: (i // nk, 0, i % nk, 0, 0))
    vt_shape = jax.ShapeDtypeStruct((batch, 4, nk, SLOT, tm), BF16)
    return pl.pallas_call(
        _in_proj_kernel,
        grid=(n // tm,),
        in_specs=[row(D_MODEL),
                  pl.BlockSpec((1, D_MODEL), lambda i: (0, 0)),
                  once(D_MODEL, IN_COLS), once(2 * HEAD_W, D_MODEL),
                  tab, tab],
        out_specs=[row(w) for w in widths] + [vt_spec, vt_spec],
        out_shape=[jax.ShapeDtypeStruct((n, w), t) for w, t in zip(widths, dtypes)]
        + [vt_shape, vt_shape],
        compiler_params=pltpu.CompilerParams(dimension_semantics=("arbitrary",),
                                             vmem_limit_bytes=VMEM_LIMIT),
        name="in_proj",
    )(h, g, w_cat, w_vt, cos_t, sin_t)


def _bf16_split3(x):
    hi = x.astype(BF16).astype(F32)
    r = x - hi
    mid = r.astype(BF16).astype(F32)
    return hi, mid, r - mid


def _fox_prep_kernel(ff_ref, b_ref, qk_ref, o_ref, carry_ref):
    @pl.when(pl.program_id(1) == 0)
    def _():
        carry_ref[...] = jnp.zeros_like(carry_ref)

    x = ff_ref[0] + b_ref[...]
    lf = jnp.minimum(x, 0.0) - jnp.log(1.0 + jnp.exp(-jnp.abs(x)))
    t = x.shape[0]
    tri = (lax.broadcasted_iota(jnp.int32, (t, t), 1)
           <= lax.broadcasted_iota(jnp.int32, (t, t), 0)).astype(F32)
    cs = jnp.dot(tri, lf, precision=lax.Precision.HIGHEST,
                 preferred_element_type=F32) + carry_ref[...]
    carry_ref[...] = cs[t - 1:t, :]
    parts = _bf16_split3(cs * LOG2E)
    lane = lax.broadcasted_iota(jnp.int32, (t, SLOT), 1)
    d0 = FX_DH
    for h in range(FX_HEADS):
        q = qk_ref[:, h * SLOT:(h + 1) * SLOT].astype(F32)
        k = qk_ref[:, HEAD_W + h * SLOT:HEAD_W + (h + 1) * SLOT].astype(F32)
        for i, part in enumerate(parts):
            col = part[:, h:h + 1]
            q = jnp.where(lane == d0 + i, col, q)
            k = jnp.where(lane == d0 + 3 + i, -col, k)
        q = jnp.where((lane >= d0 + 3) & (lane < d0 + 6), 1.0, q)
        k = jnp.where((lane >= d0) & (lane < d0 + 3), 1.0, k)
        o_ref[:, h * SLOT:(h + 1) * SLOT] = q.astype(BF16)
        o_ref[:, HEAD_W + h * SLOT:HEAD_W + (h + 1) * SLOT] = k.astype(BF16)


def _fox_prep(ff, bias_row, fqk, batch, seq_len):
    n = fqk.shape[0]
    t = ROW_TILE
    nt = seq_len // t
    return pl.pallas_call(
        _fox_prep_kernel,
        grid=(batch, nt),
        in_specs=[pl.BlockSpec((1, t, LANES), lambda b, j: (b, j, 0)),
                  pl.BlockSpec((1, LANES), lambda b, j: (0, 0)),
                  pl.BlockSpec((t, 2 * HEAD_W), lambda b, j: (b * nt + j, 0))],
        out_specs=pl.BlockSpec((t, 2 * HEAD_W), lambda b, j: (b * nt + j, 0)),
        out_shape=jax.ShapeDtypeStruct((n, 2 * HEAD_W), BF16),
        scratch_shapes=[pltpu.VMEM((1, LANES), F32)],
        compiler_params=pltpu.CompilerParams(dimension_semantics=("arbitrary", "arbitrary")),
        name="fox_prep",
    )(ff, bias_row, fqk)


def _softmax_block(s, m_prev):
    m_new = jnp.maximum(m_prev, jnp.max(s, axis=0, keepdims=True))
    alpha = jnp.exp2(m_prev - m_new)
    p = jnp.exp2(s - m_new).astype(BF16)
    return p, alpha, m_new


def _block_start(j, blk):
    return j * blk if isinstance(j, int) else pl.multiple_of(j * blk, blk)


def _causal_valid(i, j, blk):
    k_idx = j * blk + lax.broadcasted_iota(jnp.int32, (blk, blk), 0)
    q_idx = i * blk + lax.broadcasted_iota(jnp.int32, (blk, blk), 1)
    return (k_idx <= q_idx) & (k_idx >= LPAD)


def _pipelined_key_blocks(i, scores, consume, mask_first):
    scores(0, 0)

    def pair(t, carry):
        @pl.when(t == 0)
        def _():
            mask_first()

        scores(2 * t + 1, 1)
        consume(2 * t, 0, False)
        scores(2 * t + 2, 0)
        consume(2 * t + 1, 1, False)
        return carry

    lax.fori_loop(0, lax.shift_right_logical(i, 1), pair, 0)
    odd = (i & 1) == 1

    @pl.when(jnp.logical_not(odd))
    def _():
        consume(i, 0, True)

    @pl.when(odd)
    def _():
        scores(i, 1)
        consume(i - 1, 0, True)
        consume(i, 1, True)


def _head_out(acc):
    lane = lax.broadcasted_iota(jnp.int32, acc.shape, 1)
    return jnp.where(lane < ONE_LANE, acc / acc[:, ONE_LANE:ONE_LANE + 1], 0.0)


def _head_rms(o, g):
    ms = jnp.sum(o * o, axis=-1, keepdims=True) * (1.0 / ONE_LANE)
    return o * lax.rsqrt(ms + EPS) * g


def _fox_attn_kernel(q_ref, k_ref, vt_ref, g_ref, o_ref, m_ref, acc_ref, sa_ref, sb_ref,
                     *, blk):
    i = pl.program_id(2)
    q = q_ref[...]
    bufs = (sa_ref, sb_ref)
    m_ref[...] = jnp.full(m_ref.shape, NEG, F32)
    acc_ref[...] = jnp.zeros_like(acc_ref)

    def scores(j, buf):
        bufs[buf][...] = lax.dot_general(k_ref[pl.ds(_block_start(j, blk), blk), :], q, _NT,
                                         preferred_element_type=F32)

    def consume(j, buf, masked):
        s = bufs[buf][...]
        if masked:
            s = jnp.where(_causal_valid(i, j, blk), s, NEG)
        p, alpha, m_new = _softmax_block(s, m_ref[...])
        acc_ref[...] = alpha * acc_ref[...] + jnp.dot(
            vt_ref[0, 0, j], p, preferred_element_type=F32)
        m_ref[...] = m_new

    def mask_first():
        sa_ref[...] = jnp.where(_causal_valid(i, 0, blk), sa_ref[...], NEG)

    _pipelined_key_blocks(i, scores, consume, mask_first)
    o_ref[...] = _head_rms(_head_out(acc_ref[...].T), g_ref[...]).astype(BF16)


def _value_spec(seq_len):
    nk = seq_len // ATTN_BLOCK
    return pl.BlockSpec((1, 1, nk, SLOT, ATTN_BLOCK), lambda b, h, i: (b, h, 0, 0, 0))


def _fox_attn(fqk, fvt, gain, batch, seq_len):
    n = fqk.shape[0]
    blk = ATTN_BLOCK
    nq = seq_len // blk
    return pl.pallas_call(
        functools.partial(_fox_attn_kernel, blk=blk),
        grid=(batch, FX_HEADS, nq),
        in_specs=[pl.BlockSpec((blk, SLOT), lambda b, h, i: (b * nq + i, h)),
                  pl.BlockSpec((seq_len, SLOT), lambda b, h, i: (b, FX_HEADS + h)),
                  _value_spec(seq_len),
                  pl.BlockSpec((1, SLOT), lambda b, h, i: (0, 0))],
        out_specs=pl.BlockSpec((blk, SLOT), lambda b, h, i: (b * nq + i, h)),
        out_shape=jax.ShapeDtypeStruct((n, HEAD_W), BF16),
        scratch_shapes=[pltpu.VMEM((1, blk), F32), pltpu.VMEM((SLOT, blk), F32),
                        pltpu.VMEM((blk, blk), F32), pltpu.VMEM((blk, blk), F32)],
        compiler_params=pltpu.CompilerParams(
            dimension_semantics=("arbitrary", "arbitrary", "arbitrary"),
            vmem_limit_bytes=VMEM_LIMIT),
        name="fox_attn",
    )(fqk, fqk, fvt, gain)


def _diff_attn_kernel(q_ref, k_ref, vt_ref, lam_ref, g_ref, o_ref, m_ref, acc_ref,
                      sa_ref, sb_ref, *, blk, lam_init):
    i = pl.program_id(2)
    q = q_ref[...]
    bufs = (sa_ref, sb_ref)
    m_ref[...] = jnp.full(m_ref.shape, NEG, F32)
    acc_ref[...] = jnp.zeros_like(acc_ref)

    def scores(j, buf):
        k0 = _block_start(j, blk)
        for c in range(2):
            bufs[buf][c] = lax.dot_general(
                k_ref[pl.ds(k0, blk), c * SLOT:(c + 1) * SLOT], q, _NT,
                preferred_element_type=F32)

    def consume(j, buf, masked):
        mask = _causal_valid(i, j, blk) if masked else None
        ps, alphas = [], []
        for c in range(2):
            s = bufs[buf][c]
            if masked:
                s = jnp.where(mask, s, NEG)
            p, alpha, m_new = _softmax_block(s, m_ref[c])
            m_ref[c] = m_new
            ps.append(p)
            alphas.append(alpha)
        pv = jnp.dot(vt_ref[0, 0, j], jnp.concatenate(ps, axis=1),
                     preferred_element_type=F32)
        for c in range(2):
            acc_ref[c] = alphas[c] * acc_ref[c] + pv[:, c * blk:(c + 1) * blk]

    def mask_first():
        mask = _causal_valid(i, 0, blk)
        for c in range(2):
            sa_ref[c] = jnp.where(mask, sa_ref[c], NEG)

    _pipelined_key_blocks(i, scores, consume, mask_first)
    lv = lam_ref[...]
    lam = (jnp.exp(jnp.sum(lv[0:1] * lv[1:2], axis=-1, keepdims=True))
           - jnp.exp(jnp.sum(lv[2:3] * lv[3:4], axis=-1, keepdims=True)) + lam_init)
    o = _head_out(acc_ref[0].T) - lam * _head_out(acc_ref[1].T)
    o_ref[...] = (_head_rms(o, g_ref[...]) * (1.0 - lam_init)).astype(BF16)


def _diff_attn(dqk, dvt, lam_vecs, gain, lam_init, batch, seq_len):
    n = dqk.shape[0]
    blk = ATTN_BLOCK
    nq = seq_len // blk
    return pl.pallas_call(
        functools.partial(_diff_attn_kernel, blk=blk, lam_init=lam_init),
        grid=(batch, DA_HEADS, nq),
        in_specs=[pl.BlockSpec((blk, SLOT), lambda b, h, i: (b * nq + i, h)),
                  pl.BlockSpec((seq_len, 2 * SLOT), lambda b, h, i: (b, 2 + h)),
                  _value_spec(seq_len),
                  pl.BlockSpec((8, LANES), lambda b, h, i: (0, 0)),
                  pl.BlockSpec((1, SLOT), lambda b, h, i: (0, 0))],
        out_specs=pl.BlockSpec((blk, SLOT), lambda b, h, i: (b * nq + i, h)),
        out_shape=jax.ShapeDtypeStruct((n, HEAD_W), BF16),
        scratch_shapes=[pltpu.VMEM((2, 1, blk), F32), pltpu.VMEM((2, SLOT, blk), F32),
                        pltpu.VMEM((2, blk, blk), F32), pltpu.VMEM((2, blk, blk), F32)],
        compiler_params=pltpu.CompilerParams(
            dimension_semantics=("arbitrary", "arbitrary", "arbitrary"),
            vmem_limit_bytes=VMEM_LIMIT),
        name="diff_attn",
    )(dqk, dqk, dvt, lam_vecs, gain)


def _hgrn_kernel(q_ref, i_ref, g_ref, f_ref, lb_ref, gain_ref, o_ref,
                 st_ref, hs_ref, *, tl):
    j = pl.program_id(2)

    @pl.when(j == 0)
    def _():
        st_ref[...] = jnp.zeros_like(st_ref)

    c_rows, sub = HG_CHUNK, HG_SUB
    tri = (lax.broadcasted_iota(jnp.int32, (c_rows, c_rows), 1)
           <= lax.broadcasted_iota(jnp.int32, (c_rows, c_rows), 0)).astype(F32)
    row_c = lax.broadcasted_iota(jnp.int32, (c_rows, 1), 0)
    t_sub = lax.broadcasted_iota(jnp.int32, (sub, 1), 0)
    lane_c = lax.broadcasted_iota(jnp.int32, (sub, c_rows), 1)

    t_all = lax.broadcasted_iota(jnp.int32, (c_rows, c_rows), 0)
    s_all = lax.broadcasted_iota(jnp.int32, (c_rows, c_rows), 1)
    level_masks = []
    g = sub
    while g < c_rows:
        level_masks.append(((t_all // g) == (s_all // g) + 1) & ((s_all // g) % 2 == 0))
        g *= 2

    heads = range(HG_PER_STEP)
    cols = [slice(hh * HG_DK, (hh + 1) * HG_DK) for hh in heads]

    def chunk(c, carry):
        r0 = pl.multiple_of(c * c_rows, c_rows)
        rows = pl.ds(r0, c_rows)
        valid = (j * tl + r0 + row_c) >= LPAD
        G, kk, qs, v, st, o_inter = [], [], [], [], [], []
        for hh in heads:
            lb = lb_ref[:, cols[hh]]
            sig = _sigmoid(f_ref[rows, cols[hh]])
            log_f = jnp.where(valid, jnp.log2(jnp.maximum(lb + (1.0 - lb) * sig, TINY)), 0.0)
            kk.append(jnp.where(valid, (1.0 - lb) * (1.0 - sig), 0.0))
            G.append(jnp.dot(tri, log_f, precision=lax.Precision.HIGHEST,
                             preferred_element_type=F32))
            qf = q_ref[rows, cols[hh]].astype(F32)
            qs.append(qf * _sigmoid(qf) * (HG_DK ** -0.5))
            v.append(i_ref[rows, cols[hh]])
            hs_ref[hh] = G[hh] - jnp.log2(kk[hh])
            st.append(st_ref[hh])
            o_inter.append(lax.dot_general(
                (qs[hh] * jnp.exp2(G[hh])).astype(BF16), st[hh].astype(BF16), _NT,
                preferred_element_type=F32))
        a_mat = [jnp.zeros((c_rows, c_rows), F32) for _ in heads]
        g = sub
        for mask in level_masks:
            for hh in heads:
                ref = jnp.concatenate(
                    [jnp.broadcast_to(G[hh][p + g - 1:p + g], (2 * g, HG_DK))
                     for p in range(0, c_rows, 2 * g)], axis=0)
                e = jnp.exp2(-jnp.abs(G[hh] - ref))
                part = lax.dot_general((qs[hh] * e).astype(BF16), (kk[hh] * e).astype(BF16),
                                       _NT, preferred_element_type=F32)
                a_mat[hh] = jnp.where(mask, part, a_mat[hh])
            g *= 2
        a_rows = [[] for _ in heads]
        for b in range(c_rows // sub):
            lo = b * sub
            for hh in heads:
                q_b = qs[hh][lo:lo + sub]
                g_b = G[hh][lo:lo + sub]
                a_blk = a_mat[hh][lo:lo + sub]
                for s in range(lo, lo + sub):
                    y = q_b * jnp.exp2(g_b - hs_ref[hh, s:s + 1, :])
                    a_blk = jnp.where(lane_c == s, jnp.sum(y, axis=-1, keepdims=True), a_blk)
                a_rows[hh].append(jnp.where(lane_c <= t_sub + lo, a_blk, 0.0))
        for hh in heads:
            a_full = jnp.concatenate(a_rows[hh], axis=0).astype(BF16)
            o = o_inter[hh] + jnp.dot(a_full, v[hh], preferred_element_type=F32)
            g_last = G[hh][c_rows - 1:c_rows]
            kd = kk[hh] * jnp.exp2(g_last - G[hh])
            st_ref[hh] = st[hh] * jnp.exp2(g_last) + lax.dot_general(
                v[hh], kd.astype(BF16), _TN, preferred_element_type=F32)
            gate = g_ref[rows, cols[hh]].astype(F32)
            o = _rms(o, gain_ref[...]) * (gate * _sigmoid(gate))
            o_ref[rows, cols[hh]] = o.astype(BF16)
        return carry

    lax.fori_loop(0, tl // c_rows, chunk, 0)


def _hgrn(hqig, hf, lb_row, gain, batch, seq_len):
    n = hqig.shape[0]
    tl = ROW_TILE
    nt = seq_len // tl
    w = HG_PER_STEP * HG_DK
    groups = HG_HEADS // HG_PER_STEP
    col = lambda off: pl.BlockSpec((tl, w), lambda b, h, j: (b * nt + j, off + h))
    return pl.pallas_call(
        functools.partial(_hgrn_kernel, tl=tl),
        grid=(batch, groups, nt),
        in_specs=[col(0), col(groups), col(2 * groups), col(0),
                  pl.BlockSpec((1, w), lambda b, h, j: (0, h)),
                  pl.BlockSpec((1, HG_DK), lambda b, h, j: (0, 0))],
        out_specs=col(0),
        out_shape=jax.ShapeDtypeStruct((n, HG_WIDTH), BF16),
        scratch_shapes=[pltpu.VMEM((HG_PER_STEP, HG_DK, HG_DK), F32),
                        pltpu.VMEM((HG_PER_STEP, HG_CHUNK, HG_DK), F32)],
        compiler_params=pltpu.CompilerParams(
            dimension_semantics=("arbitrary", "arbitrary", "arbitrary")),
        name="hgrn2",
    )(hqig, hqig, hqig, hf, lb_row, gain)


def _out_router_kernel(oa_ref, ob_ref, oc_ref, h_ref, wo_ref, g_ref, wrh_ref, wrl_ref, br_ref,
                       h1_ref, u_ref, route_ref, cnt_ref, carry_ref):
    @pl.when(pl.program_id(0) == 0)
    def _():
        carry_ref[...] = jnp.zeros_like(carry_ref)

    h1 = (h_ref[...]
          + jnp.dot(oa_ref[...], wo_ref[0:512, :], preferred_element_type=F32)
          + jnp.dot(ob_ref[...], wo_ref[512:1024, :], preferred_element_type=F32)
          + jnp.dot(oc_ref[...], wo_ref[1024:1536, :], preferred_element_type=F32))
    h1_ref[...] = h1
    u = _rms(h1, g_ref[...])
    _store_token_tiles(u_ref, u)
    u_hi = u.astype(BF16)
    u_lo = (u - u_hi.astype(F32)).astype(BF16)
    logits = (jnp.dot(u_hi, wrh_ref[...], preferred_element_type=F32)
              + (jnp.dot(u_hi, wrl_ref[...], preferred_element_type=F32)
                 + jnp.dot(u_lo, wrh_ref[...], preferred_element_type=F32))
              + br_ref[...])
    tm = logits.shape[0]
    lane = lax.broadcasted_iota(jnp.int32, (tm, LANES), 1).astype(F32)
    big = float(LANES)
    is_g = (lane >= N_EXPERTS) & (lane < N_EXPERTS + N_GROUPS)
    gl = jnp.where(is_g, logits, -jnp.inf)
    gmax = jnp.max(gl, axis=-1, keepdims=True)
    gsel = jnp.min(jnp.where(gl == gmax, lane, big), axis=-1, keepdims=True) - N_EXPERTS
    p_g = 1.0 / jnp.sum(jnp.exp(gl - gmax), axis=-1, keepdims=True)
    lo = gsel * EXPERTS_PER_GROUP
    el = jnp.where((lane >= lo) & (lane < lo + EXPERTS_PER_GROUP), logits, -jnp.inf)
    m1 = jnp.max(el, axis=-1, keepdims=True)
    i1 = jnp.min(jnp.where(el == m1, lane, big), axis=-1, keepdims=True)
    el2 = jnp.where(lane == i1, -jnp.inf, el)
    m2 = jnp.max(el2, axis=-1, keepdims=True)
    i2 = jnp.min(jnp.where(el2 == m2, lane, big), axis=-1, keepdims=True)
    r = jnp.exp(m2 - m1)
    gate1 = p_g / (1.0 + r)
    gate2 = gate1 * r
    oh1 = lane == i1
    oh2 = lane == i2
    onehot = jnp.where(oh1 | oh2, 1.0, 0.0)
    tri = (lax.broadcasted_iota(jnp.int32, (tm, tm), 1)
           < lax.broadcasted_iota(jnp.int32, (tm, tm), 0)).astype(BF16)
    before = jnp.dot(tri, onehot.astype(BF16), preferred_element_type=F32) + carry_ref[...]
    rank1 = jnp.sum(jnp.where(oh1, before, 0.0), axis=-1, keepdims=True)
    rank2 = jnp.sum(jnp.where(oh2, before, 0.0), axis=-1, keepdims=True)
    total = carry_ref[...] + jnp.sum(onehot, axis=0, keepdims=True)
    carry_ref[...] = total
    cnt_ref[...] = total
    route = jnp.where(lane == 0, i1, 0.0)
    for idx, val in ((1, i2), (2, gate1), (3, gate2), (4, rank1), (5, rank2)):
        route = jnp.where(lane == idx, val, route)
    route_ref[...] = route


def _out_router(oa, ob, oc, h, w_out, g, w_rt, b_rt):
    n = h.shape[0]
    w_rt_hi = w_rt.astype(BF16)
    w_rt_lo = (w_rt - w_rt_hi.astype(F32)).astype(BF16)
    tm = ROW_TILE
    row = lambda w: pl.BlockSpec((tm, w), lambda i: (i, 0))
    const = lambda r, c: pl.BlockSpec((r, c), lambda i: (0, 0))
    return pl.pallas_call(
        _out_router_kernel,
        grid=(n // tm,),
        in_specs=[row(HG_WIDTH), row(HEAD_W), row(HEAD_W), row(D_MODEL),
                  const(HG_WIDTH + 2 * HEAD_W, D_MODEL),
                  const(1, D_MODEL), const(D_MODEL, LANES), const(D_MODEL, LANES),
                  const(1, LANES)],
        out_specs=[row(D_MODEL), pl.BlockSpec((tm,) + TOKEN_TILE, lambda i: (i, 0, 0)),
                   row(LANES), const(1, LANES)],
        out_shape=[jax.ShapeDtypeStruct((n, D_MODEL), F32),
                   jax.ShapeDtypeStruct((n,) + TOKEN_TILE, F32),
                   jax.ShapeDtypeStruct((n, LANES), F32),
                   jax.ShapeDtypeStruct((1, LANES), F32)],
        scratch_shapes=[pltpu.VMEM((1, LANES), F32)],
        compiler_params=pltpu.CompilerParams(dimension_semantics=("arbitrary",)),
        name="out_router",
    )(oa, ob, oc, h, w_out, g, w_rt_hi, w_rt_lo, b_rt)


def _dispatch_kernel(dest_ref, u_ref, xs_in_ref, xs_ref, back_ref, sem, *, tm, n, tb):
    del xs_in_ref
    i = pl.program_id(0)
    base = i * tm

    @pl.when(i == 0)
    def _():
        def fill(s, carry):
            parity = lax.shift_right_logical(s, tb.bit_length() - 1) & 1
            back_ref[s] = 2 * n + parity * tb + (s & (tb - 1))
            return carry

        lax.fori_loop(0, back_ref.shape[0], fill, 0, unroll=8)

    def issue(r, carry):
        for k in range(2):
            src = k * n + base + r
            d = dest_ref[src]
            back_ref[d] = src
            pltpu.make_async_copy(u_ref.at[pl.ds(r, 1)], xs_ref.at[pl.ds(d, 1)], sem).start()
        return carry

    lax.fori_loop(0, tm, issue, 0, unroll=8)
    for _ in range(2):
        pltpu.make_async_copy(u_ref, xs_ref.at[pl.ds(0, tm)], sem).wait()


def _dispatch(dest, u, xs_zero):
    n = u.shape[0]
    tm = ROW_TILE
    p_rows = xs_zero.shape[0]
    return pl.pallas_call(
        functools.partial(_dispatch_kernel, tm=tm, n=n, tb=MOE_BLOCK),
        grid_spec=pltpu.PrefetchScalarGridSpec(
            num_scalar_prefetch=1,
            grid=(n // tm,),
            in_specs=[pl.BlockSpec((tm,) + TOKEN_TILE, lambda i, d: (i, 0, 0)),
                      pl.BlockSpec(memory_space=pl.ANY)],
            out_specs=[pl.BlockSpec(memory_space=pl.ANY),
                       pl.BlockSpec(memory_space=pltpu.SMEM)],
            scratch_shapes=[pltpu.SemaphoreType.DMA(())]),
        out_shape=[jax.ShapeDtypeStruct(xs_zero.shape, xs_zero.dtype),
                   jax.ShapeDtypeStruct((p_rows,), jnp.int32)],
        input_output_aliases={2: 0},
        compiler_params=pltpu.CompilerParams(dimension_semantics=("arbitrary",)),
        name="moe_dispatch",
    )(dest, u, xs_zero)


def _expert_kernel(be_ref, na_ref, back_ref, x_ref, w1_ref, w3_ref, w2_ref, out_ref,
                   y_ref, w1b_ref, w3b_ref, w2b_ref, sem, *, tb):
    j = pl.program_id(0)
    na = na_ref[0]
    cur = j % 2

    def scatter(block, buf):
        for r in range(tb):
            pltpu.make_async_copy(y_ref.at[buf, pl.ds(r, 1)],
                                  out_ref.at[pl.ds(back_ref[block * tb + r], 1)],
                                  sem.at[buf]).start()

    def compute(buf):
        x = _load_token_tiles(x_ref).astype(BF16)
        a = jnp.dot(x, w1b_ref[...], preferred_element_type=F32)
        b = jnp.dot(x, w3b_ref[...], preferred_element_type=F32)
        act = (a * _sigmoid(a) * b).astype(BF16)
        _store_token_tiles(y_ref.at[buf],
                           jnp.dot(act, w2b_ref[...], preferred_element_type=F32))

    new_expert = (j == 0) | (be_ref[j] != be_ref[jnp.maximum(j - 1, 0)])

    @pl.when((j < na) & new_expert)
    def _():
        w1b_ref[...] = w1_ref[0].astype(BF16)
        w3b_ref[...] = w3_ref[0].astype(BF16)
        w2b_ref[...] = w2_ref[0].astype(BF16)

    @pl.when((j >= 2) & (j < na + 2))
    def _():
        pltpu.make_async_copy(y_ref.at[cur], out_ref.at[pl.ds(0, tb)], sem.at[cur]).wait()

    @pl.when(j == 0)
    def _():
        y_ref[...] = jnp.zeros_like(y_ref)
        first_spare = out_ref.shape[0] - 2 * tb
        spare = [pltpu.make_async_copy(
            y_ref.at[b], out_ref.at[pl.ds(first_spare + b * tb, tb)], sem.at[b])
            for b in range(2)]
        for copy in spare:
            copy.start()
        for copy in spare:
            copy.wait()
        compute(cur)

    @pl.when((j >= 1) & (j < na))
    def _():
        scatter(j - 1, 1 - cur)
        compute(cur)

    @pl.when(j == na)
    def _():
        scatter(j - 1, 1 - cur)


def _experts(blk_expert, n_active, slot_back, xs, w1, w3, w2, layer, n):
    p_rows = xs.shape[0]
    tb = MOE_BLOCK
    last = lambda j, na: jnp.maximum(jnp.minimum(j, na[0] - 1), 0)
    w_spec = lambda r, c: pl.BlockSpec((None, 1, r, c),
                                       lambda j, be, na, back: (layer, be[j], 0, 0))
    return pl.pallas_call(
        functools.partial(_expert_kernel, tb=tb),
        grid_spec=pltpu.PrefetchScalarGridSpec(
            num_scalar_prefetch=3,
            grid=(p_rows // tb + 1,),
            in_specs=[pl.BlockSpec((tb,) + TOKEN_TILE,
                                   lambda j, be, na, back: (last(j, na), 0, 0)),
                      w_spec(D_MODEL, D_EXPERT), w_spec(D_MODEL, D_EXPERT),
                      w_spec(D_EXPERT, D_MODEL)],
            out_specs=pl.BlockSpec(memory_space=pl.ANY),
            scratch_shapes=[pltpu.VMEM((2, tb) + TOKEN_TILE, F32),
                            pltpu.VMEM((D_MODEL, D_EXPERT), BF16),
                            pltpu.VMEM((D_MODEL, D_EXPERT), BF16),
                            pltpu.VMEM((D_EXPERT, D_MODEL), BF16),
                            pltpu.SemaphoreType.DMA((2,))]),
        out_shape=jax.ShapeDtypeStruct((2 * n + 2 * tb,) + TOKEN_TILE, F32),
        compiler_params=pltpu.CompilerParams(dimension_semantics=("arbitrary",),
                                             vmem_limit_bytes=VMEM_LIMIT),
        name="moe_experts",
    )(blk_expert, n_active, slot_back, xs, w1, w3, w2)


def _combine_kernel(h_ref, route_ref, gfin_ref, y0_ref, y1_ref, o_ref, *, final):
    route = route_ref[...]
    out = (h_ref[...] + route[:, 2:3] * _load_token_tiles(y0_ref)
           + route[:, 3:4] * _load_token_tiles(y1_ref))
    if final:
        out = _rms(out, gfin_ref[...])
    o_ref[...] = out


def _combine(h1, route, g_final, ys, batch, seq_len, final):
    n = h1.shape[0]
    tm = ROW_TILE
    tiles_per_seq = seq_len // tm
    skip_tiles = (LPAD + N_META) // tm if final else 0
    out_tiles = tiles_per_seq - skip_tiles
    tile = lambda b, i: b * tiles_per_seq + skip_tiles + i
    in_row = lambda w: pl.BlockSpec((tm, w), lambda b, i: (tile(b, i), 0))
    y_rows = lambda k: pl.BlockSpec((tm,) + TOKEN_TILE,
                                    lambda b, i: (k * (n // tm) + tile(b, i), 0, 0))
    return pl.pallas_call(
        functools.partial(_combine_kernel, final=final),
        grid=(batch, out_tiles),
        in_specs=[in_row(D_MODEL), in_row(LANES),
                  pl.BlockSpec((1, D_MODEL), lambda b, i: (0, 0)),
                  y_rows(0), y_rows(1)],
        out_specs=pl.BlockSpec((tm, D_MODEL), lambda b, i: (b * out_tiles + i, 0)),
        out_shape=jax.ShapeDtypeStruct((batch * out_tiles * tm, D_MODEL), F32),
        compiler_params=pltpu.CompilerParams(dimension_semantics=("arbitrary", "arbitrary")),
        name="moe_combine",
    )(h1, route, g_final, ys, ys)


def _to_slots(m):
    lead = m.shape[:-1]
    x = m.reshape(lead + (4, 64))
    x = jnp.pad(x, [(0, 0)] * (len(lead) + 1) + [(0, SLOT - 64)])
    return x.reshape(lead + (HEAD_W,))


def _to_key_slots(m):
    lead = m.shape[:-1]
    x = m.reshape(lead + (DA_HEADS, 2, DA_DQK))
    keep = [(0, 0)] * (len(lead) + 1)
    c0 = jnp.pad(x[..., 0, :], keep + [(0, SLOT - DA_DQK)])
    c1 = jnp.pad(x[..., 1, :], keep + [(DA_DQK, SLOT - 2 * DA_DQK)])
    return jnp.stack([c0, c1], axis=-2).reshape(lead + (2 * HEAD_W,))


def _rope_partner(m):
    lead = m.shape[:-1]
    x = m.reshape(lead + (DA_WIDTH // DA_DQK, 2, DA_DQK // 2))
    return x[..., ::-1, :].reshape(lead + (DA_WIDTH,))


def _rope_tables(seq_len):
    half = DA_DQK // 2
    pos = (jnp.arange(seq_len) - LPAD).astype(F32)
    inv = ROPE_THETA ** (-jnp.arange(0, DA_DQK, 2, dtype=F32) / DA_DQK)
    ang = pos[:, None] * inv[None, :]
    groups = DA_WIDTH // DA_DQK
    cos = jnp.tile(jnp.cos(ang), (1, 2 * groups))
    sin = jnp.tile(jnp.concatenate([-jnp.sin(ang), jnp.sin(ang)], axis=1), (1, groups))
    q_scale = DA_DQK ** -0.5 * LOG2E
    lay = lambda t: jnp.concatenate([_to_slots(t * q_scale), _to_key_slots(t)], axis=1)
    return lay(cos), lay(sin)


def _in_weights(w):
    hq, hf, hi, hg = (w[:, i * 512:(i + 1) * 512] for i in range(4))
    dq, dk, dv = (w[:, 2048 + i * 256:2048 + (i + 1) * 256] for i in range(3))
    fq, fk, fv = (w[:, 2816 + i * 256:2816 + (i + 1) * 256] for i in range(3))
    ff = w[:, 3584:3588]
    cat = jnp.concatenate(
        [hq, hi, hg, hf,
         _to_slots(dq), _to_key_slots(dk),
         _to_slots(_rope_partner(dq)), _to_key_slots(_rope_partner(dk)),
         _to_slots(fq * (FX_DH ** -0.5)), _to_slots(fk),
         ff, jnp.zeros((D_MODEL, LANES - FX_HEADS), w.dtype)],
        axis=1)
    w_vt = jnp.concatenate([_to_slots(dv), _to_slots(fv)], axis=1).T
    return cat.astype(BF16), w_vt.astype(BF16)


def _pad_lanes(v, width=LANES):
    return jnp.zeros((1, width), F32).at[0, :v.shape[0]].set(v.astype(F32))


def kernel(x, meta_tokens, norm_mix, w_in, hgrn_lb, hgrn_norm, diff_lambda, diff_norm,
           fox_bias, fox_norm, w_out, norm_ffn, w_group, b_group, w_router, b_router,
           w1, w3, w2, norm_final):
    batch, seq, d = x.shape
    depth = w_in.shape[0]
    seq_len = LPAD + N_META + seq
    n = batch * seq_len
    pad = jnp.zeros((batch, LPAD, d), x.dtype)
    meta = jnp.broadcast_to(meta_tokens.astype(x.dtype)[None], (batch, N_META, d))
    h = jnp.concatenate([pad, meta, x], axis=1).reshape(n, d)

    s_lb = jax.nn.softmax(hgrn_lb.astype(F32), axis=0)
    lb_all = jnp.cumsum(s_lb, axis=0) - s_lb[0]
    cos_t, sin_t = _rope_tables(seq_len)

    tb = MOE_BLOCK
    n_blocks = (2 * n) // tb + N_EXPERTS
    p_rows = n_blocks * tb

    for layer in range(depth):
        lam_init = 0.8 - 0.6 * math.exp(-0.3 * layer)
        w_cat, w_vt = _in_weights(w_in[layer])
        hqig, hf, dqk, fqk, ff, dvt, fvt = _in_proj(
            h, norm_mix[layer][None, :], w_cat, w_vt, cos_t, sin_t, batch, seq_len)

        o_a = _hgrn(hqig, hf, lb_all[layer][None, :], hgrn_norm[layer][None, :],
                    batch, seq_len)

        lam_vecs = jnp.zeros((8, LANES), F32).at[:4, :DA_DQK].set(diff_lambda[layer].astype(F32))
        o_b = _diff_attn(dqk, dvt, lam_vecs, _pad_lanes(diff_norm[layer]), lam_init,
                         batch, seq_len)

        fqk_aug = _fox_prep(ff.reshape(batch, seq_len, LANES), _pad_lanes(fox_bias[layer]),
                            fqk, batch, seq_len)
        o_c = _fox_attn(fqk_aug, fvt, _pad_lanes(fox_norm[layer]), batch, seq_len)

        wo = w_out[layer]
        slot_rows = lambda m: _to_slots(m.T).T
        wo_cat = jnp.concatenate(
            [wo[:HG_WIDTH], slot_rows(wo[HG_WIDTH:HG_WIDTH + DA_WIDTH]),
             slot_rows(wo[HG_WIDTH + DA_WIDTH:])], axis=0).astype(BF16)
        w_rt = jnp.concatenate(
            [w_router[layer], w_group[layer],
             jnp.zeros((d, LANES - N_EXPERTS - N_GROUPS), F32)], axis=1)
        b_rt = _pad_lanes(jnp.concatenate([b_router[layer], b_group[layer]]))
        h1, u2, route, counts = _out_router(
            o_a, o_b, o_c, h, wo_cat, norm_ffn[layer][None, :], w_rt, b_rt)

        cnt = counts[0, :N_EXPERTS].astype(jnp.int32)
        padded = (cnt + tb - 1) // tb * tb
        p_end = jnp.cumsum(padded)
        p_start = p_end - padded
        ids = route[:, 0:2].astype(jnp.int32)
        ranks = route[:, 4:6].astype(jnp.int32)
        dest = (p_start[ids] + ranks).T.reshape(2 * n)
        blk_start = jnp.arange(n_blocks + 1, dtype=jnp.int32) * tb
        blk_expert = jnp.minimum(
            jnp.sum((p_end[None, :] <= blk_start[:, None]).astype(jnp.int32), axis=1),
            N_EXPERTS - 1)
        n_active = (p_end[-1:] // tb).astype(jnp.int32)

        xs, slot_back = _dispatch(dest, u2, jnp.zeros((p_rows,) + TOKEN_TILE, F32))
        ys = _experts(blk_expert, n_active, slot_back, xs, w1, w3, w2, layer, n)
        h = _combine(h1, route, norm_final[None, :], ys, batch, seq_len,
                     final=(layer == depth - 1))

    return h.reshape(batch, seq, d)
```

```python
import functools
import math

import jax
import jax.numpy as jnp
from jax import lax
from jax.experimental import pallas as pl
from jax.experimental.pallas import tpu as pltpu

F32 = jnp.float32
BF16 = jnp.bfloat16

D_MODEL = 1024
N_META = 16
HG_HEADS = 4
HG_DK = 128
HG_WIDTH = 512
HG_CHUNK = 64
HG_SUB = 8
HG_PER_STEP = 4
DA_HEADS = 4
DA_DV = 64
DA_DQK = 32
DA_WIDTH = 256
FX_HEADS = 4
FX_DH = 64
FX_WIDTH = 256
N_GROUPS = 4
EXPERTS_PER_GROUP = 8
N_EXPERTS = 32
D_EXPERT = 512
ROPE_THETA = 10000.0
EPS = 1e-6
NEG = -1e30
TINY = 1e-30
LOG2E = 1.4426950408889634

LANES = 128
TOKEN_TILE = (D_MODEL // LANES, LANES)
SLOT = LANES
HEAD_W = 4 * SLOT
ROW_TILE = 256
PROJ_TILE = 512
ATTN_BLOCK = 512
ATTN_HEADS_PER_STEP = 2
MOE_BLOCK = 256
LPAD = ATTN_BLOCK - N_META
ONE_LANE = 64
VMEM_LIMIT = 56 * 1024 * 1024

_C_HQIG = (0, 1536)
_C_HF = (1536, 2048)
_C_DQK = (2048, 3584)
_C_DQKP = (3584, 5120)
_C_FQ = (5120, 5632)
_C_FK = (5632, 6144)
_C_FF = (6144, 6272)
IN_COLS = 6272

_NT = (((1,), (1,)), ((), ()))
_TN = (((0,), (0,)), ((), ()))


def _store_token_tiles(ref, x):
    for c in range(TOKEN_TILE[0]):
        ref[:, c, :] = x[:, c * LANES:(c + 1) * LANES]


def _load_token_tiles(ref):
    return jnp.concatenate([ref[:, c, :] for c in range(TOKEN_TILE[0])], axis=1)


def _sigmoid(x):
    return 1.0 / (1.0 + jnp.exp(-x))


def _rms(x, g):
    return x * lax.rsqrt(jnp.mean(x * x, axis=-1, keepdims=True) + EPS) * g


def _in_proj_kernel(h_ref, g_ref, w_ref, wvt_ref, cos_ref, sin_ref,
                    hqig_ref, hf_ref, dqk_ref, fqk_ref, ff_ref, dvt_ref, fvt_ref):
    u = _rms(h_ref[...], g_ref[...]).astype(BF16)

    def mm(c):
        return jnp.dot(u, w_ref[:, c[0]:c[1]], preferred_element_type=F32)

    hqig_ref[...] = mm(_C_HQIG).astype(BF16)
    hf_ref[...] = mm(_C_HF)
    dqk_ref[...] = (mm(_C_DQK) * cos_ref[...] + mm(_C_DQKP) * sin_ref[...]).astype(BF16)
    fqk_ref[:, 0:HEAD_W] = (mm(_C_FQ) * LOG2E).astype(BF16)
    fqk_ref[:, HEAD_W:2 * HEAD_W] = mm(_C_FK).astype(BF16)
    ff_ref[...] = mm(_C_FF)
    vt = lax.dot_general(wvt_ref[...], u, _NT, preferred_element_type=F32)
    slot_row = lax.broadcasted_iota(jnp.int32, vt.shape, 0) % SLOT
    vt = jnp.where(slot_row == ONE_LANE, 1.0, vt).astype(BF16)
    for h in range(4):
        dvt_ref[0, h, 0] = vt[h * SLOT:(h + 1) * SLOT]
        fvt_ref[0, h, 0] = vt[HEAD_W + h * SLOT:HEAD_W + (h + 1) * SLOT]


def _in_proj(h, g, w_cat, w_vt, cos_t, sin_t, batch, seq_len):
    n = h.shape[0]
    tm = PROJ_TILE
    assert tm == ATTN_BLOCK
    nk = seq_len // tm
    row = lambda w: pl.BlockSpec((tm, w), lambda i: (i, 0))
    tab = pl.BlockSpec((tm, 1536), lambda i: (i % nk, 0))
    once = lambda r, c: pl.BlockSpec((r, c), lambda i: (0, 0), pipeline_mode=pl.Buffered(1))
    widths = (1536, 512, 1536, 2 * HEAD_W, LANES)
    dtypes = (BF16, F32, BF16, BF16, F32)
    vt_spec = pl.BlockSpec((1, 4, 1, SLOT, tm), lambda i: (i // nk, 0, i % nk, 0, 0))
    vt_shape = jax.ShapeDtypeStruct((batch, 4, nk, SLOT, tm), BF16)
    return pl.pallas_call(
        _in_proj_kernel,
        grid=(n // tm,),
        in_specs=[row(D_MODEL),
                  pl.BlockSpec((1, D_MODEL), lambda i: (0, 0)),
                  once(D_MODEL, IN_COLS), once(2 * HEAD_W, D_MODEL),
                  tab, tab],
        out_specs=[row(w) for w in widths] + [vt_spec, vt_spec],
        out_shape=[jax.ShapeDtypeStruct((n, w), t) for w, t in zip(widths, dtypes)]
        + [vt_shape, vt_shape],
        compiler_params=pltpu.CompilerParams(dimension_semantics=("arbitrary",),
                                             vmem_limit_bytes=VMEM_LIMIT),
        name="in_proj",
    )(h, g, w_cat, w_vt, cos_t, sin_t)


def _bf16_split3(x):
    hi = x.astype(BF16).astype(F32)
    r = x - hi
    mid = r.astype(BF16).astype(F32)
    return hi, mid, r - mid


def _fox_prep_kernel(ff_ref, b_ref, qk_ref, o_ref, carry_ref):
    @pl.when(pl.program_id(1) == 0)
    def _():
        carry_ref[...] = jnp.zeros_like(carry_ref)

    x = ff_ref[0] + b_ref[...]
    lf = jnp.minimum(x, 0.0) - jnp.log(1.0 + jnp.exp(-jnp.abs(x)))
    t = x.shape[0]
    tri = (lax.broadcasted_iota(jnp.int32, (t, t), 1)
           <= lax.broadcasted_iota(jnp.int32, (t, t), 0)).astype(F32)
    cs = jnp.dot(tri, lf, precision=lax.Precision.HIGHEST,
                 preferred_element_type=F32) + carry_ref[...]
    carry_ref[...] = cs[t - 1:t, :]
    parts = _bf16_split3(cs * LOG2E)
    lane = lax.broadcasted_iota(jnp.int32, (t, SLOT), 1)
    d0 = FX_DH
    for h in range(FX_HEADS):
        q = qk_ref[:, h * SLOT:(h + 1) * SLOT].astype(F32)
        k = qk_ref[:, HEAD_W + h * SLOT:HEAD_W + (h + 1) * SLOT].astype(F32)
        for i, part in enumerate(parts):
            col = part[:, h:h + 1]
            q = jnp.where(lane == d0 + i, col, q)
            k = jnp.where(lane == d0 + 3 + i, -col, k)
        q = jnp.where((lane >= d0 + 3) & (lane < d0 + 6), 1.0, q)
        k = jnp.where((lane >= d0) & (lane < d0 + 3), 1.0, k)
        o_ref[:, h * SLOT:(h + 1) * SLOT] = q.astype(BF16)
        o_ref[:, HEAD_W + h * SLOT:HEAD_W + (h + 1) * SLOT] = k.astype(BF16)


def _fox_prep(ff, bias_row, fqk, batch, seq_len):
    n = fqk.shape[0]
    t = ROW_TILE
    nt = seq_len // t
    return pl.pallas_call(
        _fox_prep_kernel,
        grid=(batch, nt),
        in_specs=[pl.BlockSpec((1, t, LANES), lambda b, j: (b, j, 0)),
                  pl.BlockSpec((1, LANES), lambda b, j: (0, 0)),
                  pl.BlockSpec((t, 2 * HEAD_W), lambda b, j: (b * nt + j, 0))],
        out_specs=pl.BlockSpec((t, 2 * HEAD_W), lambda b, j: (b * nt + j, 0)),
        out_shape=jax.ShapeDtypeStruct((n, 2 * HEAD_W), BF16),
        scratch_shapes=[pltpu.VMEM((1, LANES), F32)],
        compiler_params=pltpu.CompilerParams(dimension_semantics=("arbitrary", "arbitrary")),
        name="fox_prep",
    )(ff, bias_row, fqk)


def _softmax_block(s, m_prev):
    m_new = jnp.maximum(m_prev, jnp.max(s, axis=0, keepdims=True))
    alpha = jnp.exp2(m_prev - m_new)
    p = jnp.exp2(s - m_new).astype(BF16)
    return p, alpha, m_new


def _block_start(j, blk):
    return j * blk if isinstance(j, int) else pl.multiple_of(j * blk, blk)


def _causal_valid(i, j, blk):
    k_idx = j * blk + lax.broadcasted_iota(jnp.int32, (blk, blk), 0)
    q_idx = i * blk + lax.broadcasted_iota(jnp.int32, (blk, blk), 1)
    return (k_idx <= q_idx) & (k_idx >= LPAD)


def _pipelined_key_blocks(i, scores, consume, mask_first):
    scores(0, 0)

    def pair(t, carry):
        @pl.when(t == 0)
        def _():
            mask_first()

        scores(2 * t + 1, 1)
        consume(2 * t, 0, False)
        scores(2 * t + 2, 0)
        consume(2 * t + 1, 1, False)
        return carry

    lax.fori_loop(0, lax.shift_right_logical(i, 1), pair, 0)
    odd = (i & 1) == 1

    @pl.when(jnp.logical_not(odd))
    def _():
        consume(i, 0, True)

    @pl.when(odd)
    def _():
        scores(i, 1)
        consume(i - 1, 0, True)
        consume(i, 1, True)


def _head_out(acc):
    lane = lax.broadcasted_iota(jnp.int32, acc.shape, 1)
    return jnp.where(lane < ONE_LANE, acc / acc[:, ONE_LANE:ONE_LANE + 1], 0.0)


def _head_rms(o, g):
    ms = jnp.sum(o * o, axis=-1, keepdims=True) * (1.0 / ONE_LANE)
    return o * lax.rsqrt(ms + EPS) * g


def _fox_attn_kernel(q_ref, k_ref, vt_ref, g_ref, o_ref, m_ref, acc_ref, sa_ref, sb_ref,
                     *, blk):
    i = pl.program_id(2)
    heads = range(ATTN_HEADS_PER_STEP)
    slot = [slice(hh * SLOT, (hh + 1) * SLOT) for hh in heads]
    q = [q_ref[:, slot[hh]] for hh in heads]
    bufs = (sa_ref, sb_ref)
    m_ref[...] = jnp.full(m_ref.shape, NEG, F32)
    acc_ref[...] = jnp.zeros_like(acc_ref)

    def scores(j, buf):
        rows = pl.ds(_block_start(j, blk), blk)
        for hh in heads:
            bufs[buf][hh] = lax.dot_general(k_ref[rows, slot[hh]], q[hh], _NT,
                                            preferred_element_type=F32)

    def consume(j, buf, masked):
        mask = _causal_valid(i, j, blk) if masked else None
        for hh in heads:
            s = bufs[buf][hh]
            if masked:
                s = jnp.where(mask, s, NEG)
            p, alpha, m_new = _softmax_block(s, m_ref[hh])
            acc_ref[hh] = alpha * acc_ref[hh] + jnp.dot(
                vt_ref[0, hh, j], p, preferred_element_type=F32)
            m_ref[hh] = m_new

    def mask_first():
        mask = _causal_valid(i, 0, blk)
        for hh in heads:
            sa_ref[hh] = jnp.where(mask, sa_ref[hh], NEG)

    _pipelined_key_blocks(i, scores, consume, mask_first)
    for hh in heads:
        o_ref[:, slot[hh]] = _head_rms(_head_out(acc_ref[hh].T), g_ref[...]).astype(BF16)


def _value_spec(seq_len):
    nk = seq_len // ATTN_BLOCK
    return pl.BlockSpec((1, ATTN_HEADS_PER_STEP, nk, SLOT, ATTN_BLOCK),
                        lambda b, h, i: (b, h, 0, 0, 0))


def _fox_attn(fqk, fvt, gain, batch, seq_len):
    n = fqk.shape[0]
    blk = ATTN_BLOCK
    nq = seq_len // blk
    hps = ATTN_HEADS_PER_STEP
    groups = FX_HEADS // hps
    return pl.pallas_call(
        functools.partial(_fox_attn_kernel, blk=blk),
        grid=(batch, groups, nq),
        in_specs=[pl.BlockSpec((blk, hps * SLOT), lambda b, h, i: (b * nq + i, h)),
                  pl.BlockSpec((seq_len, hps * SLOT), lambda b, h, i: (b, groups + h)),
                  _value_spec(seq_len),
                  pl.BlockSpec((1, SLOT), lambda b, h, i: (0, 0))],
        out_specs=pl.BlockSpec((blk, hps * SLOT), lambda b, h, i: (b * nq + i, h)),
        out_shape=jax.ShapeDtypeStruct((n, HEAD_W), BF16),
        scratch_shapes=[pltpu.VMEM((hps, 1, blk), F32), pltpu.VMEM((hps, SLOT, blk), F32),
                        pltpu.VMEM((hps, blk, blk), F32), pltpu.VMEM((hps, blk, blk), F32)],
        compiler_params=pltpu.CompilerParams(
            dimension_semantics=("arbitrary", "arbitrary", "arbitrary"),
            vmem_limit_bytes=VMEM_LIMIT),
        name="fox_attn",
    )(fqk, fqk, fvt, gain)


def _diff_attn_kernel(q_ref, k_ref, vt_ref, lam_ref, g_ref, o_ref, m_ref, acc_ref,
                      sa_ref, sb_ref, *, blk, lam_init):
    i = pl.program_id(2)
    heads = range(ATTN_HEADS_PER_STEP)
    slot = lambda s: slice(s * SLOT, (s + 1) * SLOT)
    q = [q_ref[:, slot(hh)] for hh in heads]
    bufs = (sa_ref, sb_ref)
    m_ref[...] = jnp.full(m_ref.shape, NEG, F32)
    acc_ref[...] = jnp.zeros_like(acc_ref)

    def scores(j, buf):
        rows = pl.ds(_block_start(j, blk), blk)
        for hh in heads:
            for c in range(2):
                hc = 2 * hh + c
                bufs[buf][hc] = lax.dot_general(k_ref[rows, slot(hc)], q[hh], _NT,
                                                preferred_element_type=F32)

    def consume(j, buf, masked):
        mask = _causal_valid(i, j, blk) if masked else None
        for hh in heads:
            ps, alphas = [], []
            for c in range(2):
                hc = 2 * hh + c
                s = bufs[buf][hc]
                if masked:
                    s = jnp.where(mask, s, NEG)
                p, alpha, m_new = _softmax_block(s, m_ref[hc])
                m_ref[hc] = m_new
                ps.append(p)
                alphas.append(alpha)
            pv = jnp.dot(vt_ref[0, hh, j], jnp.concatenate(ps, axis=1),
                         preferred_element_type=F32)
            for c in range(2):
                hc = 2 * hh + c
                acc_ref[hc] = alphas[c] * acc_ref[hc] + pv[:, c * blk:(c + 1) * blk]

    def mask_first():
        mask = _causal_valid(i, 0, blk)
        for hc in range(2 * ATTN_HEADS_PER_STEP):
            sa_ref[hc] = jnp.where(mask, sa_ref[hc], NEG)

    _pipelined_key_blocks(i, scores, consume, mask_first)
    lv = lam_ref[...]
    lam = (jnp.exp(jnp.sum(lv[0:1] * lv[1:2], axis=-1, keepdims=True))
           - jnp.exp(jnp.sum(lv[2:3] * lv[3:4], axis=-1, keepdims=True)) + lam_init)
    for hh in heads:
        o = _head_out(acc_ref[2 * hh].T) - lam * _head_out(acc_ref[2 * hh + 1].T)
        o_ref[:, slot(hh)] = (_head_rms(o, g_ref[...]) * (1.0 - lam_init)).astype(BF16)


def _diff_attn(dqk, dvt, lam_vecs, gain, lam_init, batch, seq_len):
    n = dqk.shape[0]
    blk = ATTN_BLOCK
    nq = seq_len // blk
    hps = ATTN_HEADS_PER_STEP
    groups = DA_HEADS // hps
    k_first = HEAD_W // (2 * hps * SLOT)
    return pl.pallas_call(
        functools.partial(_diff_attn_kernel, blk=blk, lam_init=lam_init),
        grid=(batch, groups, nq),
        in_specs=[pl.BlockSpec((blk, hps * SLOT), lambda b, h, i: (b * nq + i, h)),
                  pl.BlockSpec((seq_len, 2 * hps * SLOT), lambda b, h, i: (b, k_first + h)),
                  _value_spec(seq_len),
                  pl.BlockSpec((8, LANES), lambda b, h, i: (0, 0)),
                  pl.BlockSpec((1, SLOT), lambda b, h, i: (0, 0))],
        out_specs=pl.BlockSpec((blk, hps * SLOT), lambda b, h, i: (b * nq + i, h)),
        out_shape=jax.ShapeDtypeStruct((n, HEAD_W), BF16),
        scratch_shapes=[pltpu.VMEM((2 * hps, 1, blk), F32),
                        pltpu.VMEM((2 * hps, SLOT, blk), F32),
                        pltpu.VMEM((2 * hps, blk, blk), F32),
                        pltpu.VMEM((2 * hps, blk, blk), F32)],
        compiler_params=pltpu.CompilerParams(
            dimension_semantics=("arbitrary", "arbitrary", "arbitrary"),
            vmem_limit_bytes=VMEM_LIMIT),
        name="diff_attn",
    )(dqk, dqk, dvt, lam_vecs, gain)


def _hgrn_kernel(q_ref, i_ref, g_ref, f_ref, lb_ref, gain_ref, o_ref,
                 st_ref, hs_ref, *, tl):
    j = pl.program_id(2)

    @pl.when(j == 0)
    def _():
        st_ref[...] = jnp.zeros_like(st_ref)

    c_rows, sub = HG_CHUNK, HG_SUB
    tri = (lax.broadcasted_iota(jnp.int32, (c_rows, c_rows), 1)
           <= lax.broadcasted_iota(jnp.int32, (c_rows, c_rows), 0)).astype(F32)
    row_c = lax.broadcasted_iota(jnp.int32, (c_rows, 1), 0)
    t_sub = lax.broadcasted_iota(jnp.int32, (sub, 1), 0)
    lane_c = lax.broadcasted_iota(jnp.int32, (sub, c_rows), 1)

    t_all = lax.broadcasted_iota(jnp.int32, (c_rows, c_rows), 0)
    s_all = lax.broadcasted_iota(jnp.int32, (c_rows, c_rows), 1)
    level_masks = []
    g = sub
    while g < c_rows:
        level_masks.append(((t_all // g) == (s_all // g) + 1) & ((s_all // g) % 2 == 0))
        g *= 2

    heads = range(HG_PER_STEP)
    cols = [slice(hh * HG_DK, (hh + 1) * HG_DK) for hh in heads]

    def chunk(c, carry):
        r0 = pl.multiple_of(c * c_rows, c_rows)
        rows = pl.ds(r0, c_rows)
        valid = (j * tl + r0 + row_c) >= LPAD
        G, kk, qs, v, st, o_inter = [], [], [], [], [], []
        for hh in heads:
            lb = lb_ref[:, cols[hh]]
            sig = _sigmoid(f_ref[rows, cols[hh]])
            log_f = jnp.where(valid, jnp.log2(jnp.maximum(lb + (1.0 - lb) * sig, TINY)), 0.0)
            kk.append(jnp.where(valid, (1.0 - lb) * (1.0 - sig), 0.0))
            G.append(jnp.dot(tri, log_f, precision=lax.Precision.HIGHEST,
                             preferred_element_type=F32))
            qf = q_ref[rows, cols[hh]].astype(F32)
            qs.append(qf * _sigmoid(qf) * (HG_DK ** -0.5))
            v.append(i_ref[rows, cols[hh]])
            hs_ref[hh] = G[hh] - jnp.log2(kk[hh])
            st.append(st_ref[hh])
            o_inter.append(lax.dot_general(
                (qs[hh] * jnp.exp2(G[hh])).astype(BF16), st[hh].astype(BF16), _NT,
                preferred_element_type=F32))
        a_mat = [jnp.zeros((c_rows, c_rows), F32) for _ in heads]
        g = sub
        for mask in level_masks:
            for hh in heads:
                ref = jnp.concatenate(
                    [jnp.broadcast_to(G[hh][p + g - 1:p + g], (2 * g, HG_DK))
                     for p in range(0, c_rows, 2 * g)], axis=0)
                e = jnp.exp2(-jnp.abs(G[hh] - ref))
                part = lax.dot_general((qs[hh] * e).astype(BF16), (kk[hh] * e).astype(BF16),
                                       _NT, preferred_element_type=F32)
                a_mat[hh] = jnp.where(mask, part, a_mat[hh])
            g *= 2
        a_rows = [[] for _ in heads]
        for b in range(c_rows // sub):
            lo = b * sub
            for hh in heads:
                q_b = qs[hh][lo:lo + sub]
                g_b = G[hh][lo:lo + sub]
                a_blk = a_mat[hh][lo:lo + sub]
                for s in range(lo, lo + sub):
                    y = q_b * jnp.exp2(g_b - hs_ref[hh, s:s + 1, :])
                    a_blk = jnp.where(lane_c == s, jnp.sum(y, axis=-1, keepdims=True), a_blk)
                a_rows[hh].append(jnp.where(lane_c <= t_sub + lo, a_blk, 0.0))
        for hh in heads:
            a_full = jnp.concatenate(a_rows[hh], axis=0).astype(BF16)
            o = o_inter[hh] + jnp.dot(a_full, v[hh], preferred_element_type=F32)
            g_last = G[hh][c_rows - 1:c_rows]
            kd = kk[hh] * jnp.exp2(g_last - G[hh])
            st_ref[hh] = st[hh] * jnp.exp2(g_last) + lax.dot_general(
                v[hh], kd.astype(BF16), _TN, preferred_element_type=F32)
            gate = g_ref[rows, cols[hh]].astype(F32)
            o = _rms(o, gain_ref[...]) * (gate * _sigmoid(gate))
            o_ref[rows, cols[hh]] = o.astype(BF16)
        return carry

    lax.fori_loop(0, tl // c_rows, chunk, 0)


def _hgrn(hqig, hf, lb_row, gain, batch, seq_len):
    n = hqig.shape[0]
    tl = ROW_TILE
    nt = seq_len // tl
    w = HG_PER_STEP * HG_DK
    groups = HG_HEADS // HG_PER_STEP
    col = lambda off: pl.BlockSpec((tl, w), lambda b, h, j: (b * nt + j, off + h))
    return pl.pallas_call(
        functools.partial(_hgrn_kernel, tl=tl),
        grid=(batch, groups, nt),
        in_specs=[col(0), col(groups), col(2 * groups), col(0),
                  pl.BlockSpec((1, w), lambda b, h, j: (0, h)),
                  pl.BlockSpec((1, HG_DK), lambda b, h, j: (0, 0))],
        out_specs=col(0),
        out_shape=jax.ShapeDtypeStruct((n, HG_WIDTH), BF16),
        scratch_shapes=[pltpu.VMEM((HG_PER_STEP, HG_DK, HG_DK), F32),
                        pltpu.VMEM((HG_PER_STEP, HG_CHUNK, HG_DK), F32)],
        compiler_params=pltpu.CompilerParams(
            dimension_semantics=("arbitrary", "arbitrary", "arbitrary")),
        name="hgrn2",
    )(hqig, hqig, hqig, hf, lb_row, gain)


def _out_router_kernel(oa_ref, ob_ref, oc_ref, h_ref, wo_ref, g_ref, wrh_ref, wrl_ref, br_ref,
                       h1_ref, u_ref, route_ref, cnt_ref, carry_ref):
    @pl.when(pl.program_id(0) == 0)
    def _():
        carry_ref[...] = jnp.zeros_like(carry_ref)

    h1 = (h_ref[...]
          + jnp.dot(oa_ref[...], wo_ref[0:512, :], preferred_element_type=F32)
          + jnp.dot(ob_ref[...], wo_ref[512:1024, :], preferred_element_type=F32)
          + jnp.dot(oc_ref[...], wo_ref[1024:1536, :], preferred_element_type=F32))
    h1_ref[...] = h1
    u = _rms(h1, g_ref[...])
    _store_token_tiles(u_ref, u)
    u_hi = u.astype(BF16)
    u_lo = (u - u_hi.astype(F32)).astype(BF16)
    logits = (jnp.dot(u_hi, wrh_ref[...], preferred_element_type=F32)
              + (jnp.dot(u_hi, wrl_ref[...], preferred_element_type=F32)
                 + jnp.dot(u_lo, wrh_ref[...], preferred_element_type=F32))
              + br_ref[...])
    tm = logits.shape[0]
    lane = lax.broadcasted_iota(jnp.int32, (tm, LANES), 1).astype(F32)
    big = float(LANES)
    is_g = (lane >= N_EXPERTS) & (lane < N_EXPERTS + N_GROUPS)
    gl = jnp.where(is_g, logits, -jnp.inf)
    gmax = jnp.max(gl, axis=-1, keepdims=True)
    gsel = jnp.min(jnp.where(gl == gmax, lane, big), axis=-1, keepdims=True) - N_EXPERTS
    p_g = 1.0 / jnp.sum(jnp.exp(gl - gmax), axis=-1, keepdims=True)
    lo = gsel * EXPERTS_PER_GROUP
    el = jnp.where((lane >= lo) & (lane < lo + EXPERTS_PER_GROUP), logits, -jnp.inf)
    m1 = jnp.max(el, axis=-1, keepdims=True)
    i1 = jnp.min(jnp.where(el == m1, lane, big), axis=-1, keepdims=True)
    el2 = jnp.where(lane == i1, -jnp.inf, el)
    m2 = jnp.max(el2, axis=-1, keepdims=True)
    i2 = jnp.min(jnp.where(el2 == m2, lane, big), axis=-1, keepdims=True)
    r = jnp.exp(m2 - m1)
    gate1 = p_g / (1.0 + r)
    gate2 = gate1 * r
    oh1 = lane == i1
    oh2 = lane == i2
    onehot = jnp.where(oh1 | oh2, 1.0, 0.0)
    tri = (lax.broadcasted_iota(jnp.int32, (tm, tm), 1)
           < lax.broadcasted_iota(jnp.int32, (tm, tm), 0)).astype(BF16)
    before = jnp.dot(tri, onehot.astype(BF16), preferred_element_type=F32) + carry_ref[...]
    rank1 = jnp.sum(jnp.where(oh1, before, 0.0), axis=-1, keepdims=True)
    rank2 = jnp.sum(jnp.where(oh2, before, 0.0), axis=-1, keepdims=True)
    total = carry_ref[...] + jnp.sum(onehot, axis=0, keepdims=True)
    carry_ref[...] = total
    cnt_ref[...] = total
    route = jnp.where(lane == 0, i1, 0.0)
    for idx, val in ((1, i2), (2, gate1), (3, gate2), (4, rank1), (5, rank2)):
        route = jnp.where(lane == idx, val, route)
    route_ref[...] = route


def _out_router(oa, ob, oc, h, w_out, g, w_rt, b_rt):
    n = h.shape[0]
    w_rt_hi = w_rt.astype(BF16)
    w_rt_lo = (w_rt - w_rt_hi.astype(F32)).astype(BF16)
    tm = ROW_TILE
    row = lambda w: pl.BlockSpec((tm, w), lambda i: (i, 0))
    const = lambda r, c: pl.BlockSpec((r, c), lambda i: (0, 0))
    return pl.pallas_call(
        _out_router_kernel,
        grid=(n // tm,),
        in_specs=[row(HG_WIDTH), row(HEAD_W), row(HEAD_W), row(D_MODEL),
                  const(HG_WIDTH + 2 * HEAD_W, D_MODEL),
                  const(1, D_MODEL), const(D_MODEL, LANES), const(D_MODEL, LANES),
                  const(1, LANES)],
        out_specs=[row(D_MODEL), pl.BlockSpec((tm,) + TOKEN_TILE, lambda i: (i, 0, 0)),
                   row(LANES), const(1, LANES)],
        out_shape=[jax.ShapeDtypeStruct((n, D_MODEL), F32),
                   jax.ShapeDtypeStruct((n,) + TOKEN_TILE, F32),
                   jax.ShapeDtypeStruct((n, LANES), F32),
                   jax.ShapeDtypeStruct((1, LANES), F32)],
        scratch_shapes=[pltpu.VMEM((1, LANES), F32)],
        compiler_params=pltpu.CompilerParams(dimension_semantics=("arbitrary",)),
        name="out_router",
    )(oa, ob, oc, h, w_out, g, w_rt_hi, w_rt_lo, b_rt)


def _dispatch_kernel(dest_ref, unused_ref, u_ref, xs_in_ref, xs_ref, back_ref, sem,
                     *, tm, n, tb):
    del xs_in_ref
    i = pl.program_id(0)
    base = i * tm

    @pl.when(i == 0)
    def _():
        def fill(s, carry):
            parity = lax.shift_right_logical(s, tb.bit_length() - 1) & 1
            back_ref[s] = 2 * n + parity * tb + (s & (tb - 1))
            return carry

        n_ranges = unused_ref.shape[0] // 2

        def one_range(e, carry):
            return lax.fori_loop(unused_ref[e], unused_ref[n_ranges + e], fill, carry)

        lax.fori_loop(0, n_ranges, one_range, 0)

    def issue(r, carry):
        for k in range(2):
            src = k * n + base + r
            d = dest_ref[src]
            back_ref[d] = src
            pltpu.make_async_copy(u_ref.at[pl.ds(r, 1)], xs_ref.at[pl.ds(d, 1)], sem).start()
        return carry

    lax.fori_loop(0, tm, issue, 0, unroll=8)
    for _ in range(2):
        pltpu.make_async_copy(u_ref, xs_ref.at[pl.ds(0, tm)], sem).wait()


def _dispatch(dest, unused, u, xs_zero):
    n = u.shape[0]
    tm = ROW_TILE
    p_rows = xs_zero.shape[0]
    return pl.pallas_call(
        functools.partial(_dispatch_kernel, tm=tm, n=n, tb=MOE_BLOCK),
        grid_spec=pltpu.PrefetchScalarGridSpec(
            num_scalar_prefetch=2,
            grid=(n // tm,),
            in_specs=[pl.BlockSpec((tm,) + TOKEN_TILE, lambda i, d, un: (i, 0, 0)),
                      pl.BlockSpec(memory_space=pl.ANY)],
            out_specs=[pl.BlockSpec(memory_space=pl.ANY),
                       pl.BlockSpec(memory_space=pltpu.SMEM)],
            scratch_shapes=[pltpu.SemaphoreType.DMA(())]),
        out_shape=[jax.ShapeDtypeStruct(xs_zero.shape, xs_zero.dtype),
                   jax.ShapeDtypeStruct((p_rows,), jnp.int32)],
        input_output_aliases={3: 0},
        compiler_params=pltpu.CompilerParams(dimension_semantics=("arbitrary",)),
        name="moe_dispatch",
    )(dest, unused, u, xs_zero)


def _expert_kernel(be_ref, na_ref, back_ref, x_ref, w1_ref, w3_ref, w2_ref, out_ref,
                   y_ref, w1b_ref, w3b_ref, w2b_ref, sem, *, tb):
    j = pl.program_id(0)
    na = na_ref[0]
    cur = j % 2

    def scatter(block, buf):
        for r in range(tb):
            pltpu.make_async_copy(y_ref.at[buf, pl.ds(r, 1)],
                                  out_ref.at[pl.ds(back_ref[block * tb + r], 1)],
                                  sem.at[buf]).start()

    def compute(buf):
        x = _load_token_tiles(x_ref).astype(BF16)
        a = jnp.dot(x, w1b_ref[...], preferred_element_type=F32)
        b = jnp.dot(x, w3b_ref[...], preferred_element_type=F32)
        act = (a * _sigmoid(a) * b).astype(BF16)
        _store_token_tiles(y_ref.at[buf],
                           jnp.dot(act, w2b_ref[...], preferred_element_type=F32))

    new_expert = (j == 0) | (be_ref[j] != be_ref[jnp.maximum(j - 1, 0)])

    @pl.when((j < na) & new_expert)
    def _():
        w1b_ref[...] = w1_ref[0].astype(BF16)
        w3b_ref[...] = w3_ref[0].astype(BF16)
        w2b_ref[...] = w2_ref[0].astype(BF16)

    @pl.when((j >= 2) & (j < na + 2))
    def _():
        pltpu.make_async_copy(y_ref.at[cur], out_ref.at[pl.ds(0, tb)], sem.at[cur]).wait()

    @pl.when(j == 0)
    def _():
        y_ref[...] = jnp.zeros_like(y_ref)
        first_spare = out_ref.shape[0] - 2 * tb
        spare = [pltpu.make_async_copy(
            y_ref.at[b], out_ref.at[pl.ds(first_spare + b * tb, tb)], sem.at[b])
            for b in range(2)]
        for copy in spare:
            copy.start()
        for copy in spare:
            copy.wait()
        compute(cur)

    @pl.when((j >= 1) & (j < na))
    def _():
        scatter(j - 1, 1 - cur)
        compute(cur)

    @pl.when(j == na)
    def _():
        scatter(j - 1, 1 - cur)


def _experts(blk_expert, n_active, slot_back, xs, w1, w3, w2, layer, n):
    p_rows = xs.shape[0]
    tb = MOE_BLOCK
    last = lambda j, na: jnp.maximum(jnp.minimum(j, na[0] - 1), 0)
    w_spec = lambda r, c: pl.BlockSpec((None, 1, r, c),
                                       lambda j, be, na, back: (layer, be[j], 0, 0))
    return pl.pallas_call(
        functools.partial(_expert_kernel, tb=tb),
        grid_spec=pltpu.PrefetchScalarGridSpec(
            num_scalar_prefetch=3,
            grid=(p_rows // tb + 1,),
            in_specs=[pl.BlockSpec((tb,) + TOKEN_TILE,
                                   lambda j, be, na, back: (last(j, na), 0, 0)),
                      w_spec(D_MODEL, D_EXPERT), w_spec(D_MODEL, D_EXPERT),
                      w_spec(D_EXPERT, D_MODEL)],
            out_specs=pl.BlockSpec(memory_space=pl.ANY),
            scratch_shapes=[pltpu.VMEM((2, tb) + TOKEN_TILE, F32),
                            pltpu.VMEM((D_MODEL, D_EXPERT), BF16),
                            pltpu.VMEM((D_MODEL, D_EXPERT), BF16),
                            pltpu.VMEM((D_EXPERT, D_MODEL), BF16),
                            pltpu.SemaphoreType.DMA((2,))]),
        out_shape=jax.ShapeDtypeStruct((2 * n + 2 * tb,) + TOKEN_TILE, F32),
        compiler_params=pltpu.CompilerParams(dimension_semantics=("arbitrary",),
                                             vmem_limit_bytes=VMEM_LIMIT),
        name="moe_experts",
    )(blk_expert, n_active, slot_back, xs, w1, w3, w2)


def _combine_kernel(h_ref, route_ref, gfin_ref, y0_ref, y1_ref, o_ref, *, final):
    route = route_ref[...]
    out = (h_ref[...] + route[:, 2:3] * _load_token_tiles(y0_ref)
           + route[:, 3:4] * _load_token_tiles(y1_ref))
    if final:
        out = _rms(out, gfin_ref[...])
    o_ref[...] = out


def _combine(h1, route, g_final, ys, batch, seq_len, final):
    n = h1.shape[0]
    tm = ROW_TILE
    tiles_per_seq = seq_len // tm
    skip_tiles = (LPAD + N_META) // tm if final else 0
    out_tiles = tiles_per_seq - skip_tiles
    tile = lambda b, i: b * tiles_per_seq + skip_tiles + i
    in_row = lambda w: pl.BlockSpec((tm, w), lambda b, i: (tile(b, i), 0))
    y_rows = lambda k: pl.BlockSpec((tm,) + TOKEN_TILE,
                                    lambda b, i: (k * (n // tm) + tile(b, i), 0, 0))
    return pl.pallas_call(
        functools.partial(_combine_kernel, final=final),
        grid=(batch, out_tiles),
        in_specs=[in_row(D_MODEL), in_row(LANES),
                  pl.BlockSpec((1, D_MODEL), lambda b, i: (0, 0)),
                  y_rows(0), y_rows(1)],
        out_specs=pl.BlockSpec((tm, D_MODEL), lambda b, i: (b * out_tiles + i, 0)),
        out_shape=jax.ShapeDtypeStruct((batch * out_tiles * tm, D_MODEL), F32),
        compiler_params=pltpu.CompilerParams(dimension_semantics=("arbitrary", "arbitrary")),
        name="moe_combine",
    )(h1, route, g_final, ys, ys)


def _to_slots(m):
    lead = m.shape[:-1]
    x = m.reshape(lead + (4, 64))
    x = jnp.pad(x, [(0, 0)] * (len(lead) + 1) + [(0, SLOT - 64)])
    return x.reshape(lead + (HEAD_W,))


def _to_key_slots(m):
    lead = m.shape[:-1]
    x = m.reshape(lead + (DA_HEADS, 2, DA_DQK))
    keep = [(0, 0)] * (len(lead) + 1)
    c0 = jnp.pad(x[..., 0, :], keep + [(0, SLOT - DA_DQK)])
    c1 = jnp.pad(x[..., 1, :], keep + [(DA_DQK, SLOT - 2 * DA_DQK)])
    return jnp.stack([c0, c1], axis=-2).reshape(lead + (2 * HEAD_W,))


def _rope_partner(m):
    lead = m.shape[:-1]
    x = m.reshape(lead + (DA_WIDTH // DA_DQK, 2, DA_DQK // 2))
    return x[..., ::-1, :].reshape(lead + (DA_WIDTH,))


def _rope_tables(seq_len):
    half = DA_DQK // 2
    pos = (jnp.arange(seq_len) - LPAD).astype(F32)
    inv = ROPE_THETA ** (-jnp.arange(0, DA_DQK, 2, dtype=F32) / DA_DQK)
    ang = pos[:, None] * inv[None, :]
    groups = DA_WIDTH // DA_DQK
    cos = jnp.tile(jnp.cos(ang), (1, 2 * groups))
    sin = jnp.tile(jnp.concatenate([-jnp.sin(ang), jnp.sin(ang)], axis=1), (1, groups))
    q_scale = DA_DQK ** -0.5 * LOG2E
    lay = lambda t: jnp.concatenate([_to_slots(t * q_scale), _to_key_slots(t)], axis=1)
    return lay(cos), lay(sin)


def _in_weights(w):
    hq, hf, hi, hg = (w[:, i * 512:(i + 1) * 512] for i in range(4))
    dq, dk, dv = (w[:, 2048 + i * 256:2048 + (i + 1) * 256] for i in range(3))
    fq, fk, fv = (w[:, 2816 + i * 256:2816 + (i + 1) * 256] for i in range(3))
    ff = w[:, 3584:3588]
    cat = jnp.concatenate(
        [hq, hi, hg, hf,
         _to_slots(dq), _to_key_slots(dk),
         _to_slots(_rope_partner(dq)), _to_key_slots(_rope_partner(dk)),
         _to_slots(fq * (FX_DH ** -0.5)), _to_slots(fk),
         ff, jnp.zeros((D_MODEL, LANES - FX_HEADS), w.dtype)],
        axis=1)
    w_vt = jnp.concatenate([_to_slots(dv), _to_slots(fv)], axis=1).T
    return cat.astype(BF16), w_vt.astype(BF16)


def _pad_lanes(v, width=LANES):
    return jnp.zeros((1, width), F32).at[0, :v.shape[0]].set(v.astype(F32))


def kernel(x, meta_tokens, norm_mix, w_in, hgrn_lb, hgrn_norm, diff_lambda, diff_norm,
           fox_bias, fox_norm, w_out, norm_ffn, w_group, b_group, w_router, b_router,
           w1, w3, w2, norm_final):
    batch, seq, d = x.shape
    depth = w_in.shape[0]
    seq_len = LPAD + N_META + seq
    n = batch * seq_len
    pad = jnp.zeros((batch, LPAD, d), x.dtype)
    meta = jnp.broadcast_to(meta_tokens.astype(x.dtype)[None], (batch, N_META, d))
    h = jnp.concatenate([pad, meta, x], axis=1).reshape(n, d)

    s_lb = jax.nn.softmax(hgrn_lb.astype(F32), axis=0)
    lb_all = jnp.cumsum(s_lb, axis=0) - s_lb[0]
    cos_t, sin_t = _rope_tables(seq_len)

    tb = MOE_BLOCK
    n_blocks = (2 * n) // tb + N_EXPERTS
    p_rows = n_blocks * tb

    for layer in range(depth):
        lam_init = 0.8 - 0.6 * math.exp(-0.3 * layer)
        w_cat, w_vt = _in_weights(w_in[layer])
        hqig, hf, dqk, fqk, ff, dvt, fvt = _in_proj(
            h, norm_mix[layer][None, :], w_cat, w_vt, cos_t, sin_t, batch, seq_len)

        o_a = _hgrn(hqig, hf, lb_all[layer][None, :], hgrn_norm[layer][None, :],
                    batch, seq_len)

        lam_vecs = jnp.zeros((8, LANES), F32).at[:4, :DA_DQK].set(diff_lambda[layer].astype(F32))
        o_b = _diff_attn(dqk, dvt, lam_vecs, _pad_lanes(diff_norm[layer]), lam_init,
                         batch, seq_len)

        fqk_aug = _fox_prep(ff.reshape(batch, seq_len, LANES), _pad_lanes(fox_bias[layer]),
                            fqk, batch, seq_len)
        o_c = _fox_attn(fqk_aug, fvt, _pad_lanes(fox_norm[layer]), batch, seq_len)

        wo = w_out[layer]
        slot_rows = lambda m: _to_slots(m.T).T
        wo_cat = jnp.concatenate(
            [wo[:HG_WIDTH], slot_rows(wo[HG_WIDTH:HG_WIDTH + DA_WIDTH]),
             slot_rows(wo[HG_WIDTH + DA_WIDTH:])], axis=0).astype(BF16)
        w_rt = jnp.concatenate(
            [w_router[layer], w_group[layer],
             jnp.zeros((d, LANES - N_EXPERTS - N_GROUPS), F32)], axis=1)
        b_rt = _pad_lanes(jnp.concatenate([b_router[layer], b_group[layer]]))
        h1, u2, route, counts = _out_router(
            o_a, o_b, o_c, h, wo_cat, norm_ffn[layer][None, :], w_rt, b_rt)

        cnt = counts[0, :N_EXPERTS].astype(jnp.int32)
        padded = (cnt + tb - 1) // tb * tb
        p_end = jnp.cumsum(padded)
        p_start = p_end - padded
        ids = route[:, 0:2].astype(jnp.int32)
        ranks = route[:, 4:6].astype(jnp.int32)
        dest = (p_start[ids] + ranks).T.reshape(2 * n)
        blk_start = jnp.arange(n_blocks + 1, dtype=jnp.int32) * tb
        blk_expert = jnp.minimum(
            jnp.sum((p_end[None, :] <= blk_start[:, None]).astype(jnp.int32), axis=1),
            N_EXPERTS - 1)
        n_active = (p_end[-1:] // tb).astype(jnp.int32)

        unused = jnp.concatenate(
            [p_start + cnt, p_end[-1:], p_end, jnp.full((1,), p_rows)]).astype(jnp.int32)
        xs, slot_back = _dispatch(dest, unused, u2, jnp.zeros((p_rows,) + TOKEN_TILE, F32))
        ys = _experts(blk_expert, n_active, slot_back, xs, w1, w3, w2, layer, n)
        h = _combine(h1, route, norm_final[None, :], ys, batch, seq_len,
                     final=(layer == depth - 1))

    return h.reshape(batch, seq, d)
```

```python
import functools
import math

import jax
import jax.numpy as jnp
from jax import lax
from jax.experimental import pallas as pl
from jax.experimental.pallas import tpu as pltpu

F32 = jnp.float32
BF16 = jnp.bfloat16

D_MODEL = 1024
N_META = 16
HG_HEADS = 4
HG_DK = 128
HG_WIDTH = 512
HG_CHUNK = 64
HG_SUB = 8
HG_PER_STEP = 4
DA_HEADS = 4
DA_DV = 64
DA_DQK = 32
DA_WIDTH = 256
FX_HEADS = 4
FX_DH = 64
FX_WIDTH = 256
N_GROUPS = 4
EXPERTS_PER_GROUP = 8
N_EXPERTS = 32
D_EXPERT = 512
ROPE_THETA = 10000.0
EPS = 1e-6
NEG = -1e30
TINY = 1e-30
LOG2E = 1.4426950408889634

LANES = 128
TOKEN_TILE = (D_MODEL // LANES, LANES)
SLOT = LANES
HEAD_W = 4 * SLOT
ROW_TILE = 256
PROJ_TILE = 512
ROUTER_TILE = 512
ATTN_BLOCK = 512
ATTN_HEADS_PER_STEP = 4
MOE_BLOCK = 256
LPAD = ATTN_BLOCK - N_META
ONE_LANE = 64
VMEM_LIMIT = 56 * 1024 * 1024

_C_HQIG = (0, 1536)
_C_HF = (1536, 2048)
_C_DQK = (2048, 3072)
_C_DQKP = (3072, 4096)
_C_FQ = (4096, 4608)
_C_FK = (4608, 5120)
_C_FF = (5120, 5248)
IN_COLS = 5248

_NT = (((1,), (1,)), ((), ()))
_TN = (((0,), (0,)), ((), ()))


def _store_token_tiles(ref, x):
    for c in range(TOKEN_TILE[0]):
        ref[:, c, :] = x[:, c * LANES:(c + 1) * LANES]


def _load_token_tiles(ref):
    return jnp.concatenate([ref[:, c, :] for c in range(TOKEN_TILE[0])], axis=1)


def _sigmoid(x):
    return 1.0 / (1.0 + jnp.exp(-x))


def _rms(x, g):
    return x * lax.rsqrt(jnp.mean(x * x, axis=-1, keepdims=True) + EPS) * g


def _in_proj_kernel(h_ref, g_ref, w_ref, wvt_ref, cos_ref, sin_ref,
                    hqig_ref, hf_ref, dqk_ref, fqk_ref, ff_ref, dvt_ref, fvt_ref):
    u = _rms(h_ref[...], g_ref[...]).astype(BF16)

    def mm(c):
        return jnp.dot(u, w_ref[:, c[0]:c[1]], preferred_element_type=F32)

    hqig_ref[...] = mm(_C_HQIG).astype(BF16)
    hf_ref[...] = mm(_C_HF)
    dqk_ref[...] = (mm(_C_DQK) * cos_ref[...] + mm(_C_DQKP) * sin_ref[...]).astype(BF16)
    fqk_ref[:, 0:HEAD_W] = (mm(_C_FQ) * LOG2E).astype(BF16)
    fqk_ref[:, HEAD_W:2 * HEAD_W] = mm(_C_FK).astype(BF16)
    ff_ref[...] = mm(_C_FF)
    vt = lax.dot_general(wvt_ref[...], u, _NT, preferred_element_type=F32)
    slot_row = lax.broadcasted_iota(jnp.int32, vt.shape, 0) % SLOT
    vt = jnp.where(slot_row == ONE_LANE, 1.0, vt).astype(BF16)
    for h in range(4):
        dvt_ref[0, h, 0] = vt[h * SLOT:(h + 1) * SLOT]
        fvt_ref[0, h, 0] = vt[HEAD_W + h * SLOT:HEAD_W + (h + 1) * SLOT]


def _in_proj(h, g, w_cat, w_vt, cos_t, sin_t, batch, seq_len):
    n = h.shape[0]
    tm = PROJ_TILE
    assert tm == ATTN_BLOCK
    nk = seq_len // tm
    row = lambda w: pl.BlockSpec((tm, w), lambda i: (i, 0))
    tab = pl.BlockSpec((tm, 2 * HEAD_W), lambda i: (i % nk, 0))
    once = lambda r, c: pl.BlockSpec((r, c), lambda i: (0, 0), pipeline_mode=pl.Buffered(1))
    widths = (1536, 512, 2 * HEAD_W, 2 * HEAD_W, LANES)
    dtypes = (BF16, F32, BF16, BF16, F32)
    vt_spec = pl.BlockSpec((1, 4, 1, SLOT, tm), lambda i: (i // nk, 0, i % nk, 0, 0))
    vt_shape = jax.ShapeDtypeStruct((batch, 4, nk, SLOT, tm), BF16)
    return pl.pallas_call(
        _in_proj_kernel,
        grid=(n // tm,),
        in_specs=[row(D_MODEL),
                  pl.BlockSpec((1, D_MODEL), lambda i: (0, 0)),
                  once(D_MODEL, IN_COLS), once(2 * HEAD_W, D_MODEL),
                  tab, tab],
        out_specs=[row(w) for w in widths] + [vt_spec, vt_spec],
        out_shape=[jax.ShapeDtypeStruct((n, w), t) for w, t in zip(widths, dtypes)]
        + [vt_shape, vt_shape],
        compiler_params=pltpu.CompilerParams(dimension_semantics=("arbitrary",),
                                             vmem_limit_bytes=VMEM_LIMIT),
        name="in_proj",
    )(h, g, w_cat, w_vt, cos_t, sin_t)


def _bf16_split3(x):
    hi = x.astype(BF16).astype(F32)
    r = x - hi
    mid = r.astype(BF16).astype(F32)
    return hi, mid, r - mid


def _fox_prep_kernel(ff_ref, b_ref, qk_ref, o_ref, carry_ref):
    @pl.when(pl.program_id(1) == 0)
    def _():
        carry_ref[...] = jnp.zeros_like(carry_ref)

    x = ff_ref[0] + b_ref[...]
    lf = jnp.minimum(x, 0.0) - jnp.log(1.0 + jnp.exp(-jnp.abs(x)))
    t = x.shape[0]
    tri = (lax.broadcasted_iota(jnp.int32, (t, t), 1)
           <= lax.broadcasted_iota(jnp.int32, (t, t), 0)).astype(F32)
    cs = jnp.dot(tri, lf, precision=lax.Precision.HIGHEST,
                 preferred_element_type=F32) + carry_ref[...]
    carry_ref[...] = cs[t - 1:t, :]
    parts = _bf16_split3(cs * LOG2E)
    lane = lax.broadcasted_iota(jnp.int32, (t, SLOT), 1)
    d0 = FX_DH
    for h in range(FX_HEADS):
        q = qk_ref[:, h * SLOT:(h + 1) * SLOT].astype(F32)
        k = qk_ref[:, HEAD_W + h * SLOT:HEAD_W + (h + 1) * SLOT].astype(F32)
        for i, part in enumerate(parts):
            col = part[:, h:h + 1]
            q = jnp.where(lane == d0 + i, col, q)
            k = jnp.where(lane == d0 + 3 + i, -col, k)
        q = jnp.where((lane >= d0 + 3) & (lane < d0 + 6), 1.0, q)
        k = jnp.where((lane >= d0) & (lane < d0 + 3), 1.0, k)
        o_ref[:, h * SLOT:(h + 1) * SLOT] = q.astype(BF16)
        o_ref[:, HEAD_W + h * SLOT:HEAD_W + (h + 1) * SLOT] = k.astype(BF16)


def _fox_prep(ff, bias_row, fqk, batch, seq_len):
    n = fqk.shape[0]
    t = ROW_TILE
    nt = seq_len // t
    return pl.pallas_call(
        _fox_prep_kernel,
        grid=(batch, nt),
        in_specs=[pl.BlockSpec((1, t, LANES), lambda b, j: (b, j, 0)),
                  pl.BlockSpec((1, LANES), lambda b, j: (0, 0)),
                  pl.BlockSpec((t, 2 * HEAD_W), lambda b, j: (b * nt + j, 0))],
        out_specs=pl.BlockSpec((t, 2 * HEAD_W), lambda b, j: (b * nt + j, 0)),
        out_shape=jax.ShapeDtypeStruct((n, 2 * HEAD_W), BF16),
        scratch_shapes=[pltpu.VMEM((1, LANES), F32)],
        compiler_params=pltpu.CompilerParams(dimension_semantics=("arbitrary", "arbitrary")),
        name="fox_prep",
    )(ff, bias_row, fqk)


def _softmax_block(s, m_prev):
    m_new = jnp.maximum(m_prev, jnp.max(s, axis=0, keepdims=True))
    alpha = jnp.exp2(m_prev - m_new)
    p = jnp.exp2(s - m_new).astype(BF16)
    return p, alpha, m_new


def _block_start(j, blk):
    return j * blk if isinstance(j, int) else pl.multiple_of(j * blk, blk)


def _causal_valid(i, j, blk):
    k_idx = j * blk + lax.broadcasted_iota(jnp.int32, (blk, blk), 0)
    q_idx = i * blk + lax.broadcasted_iota(jnp.int32, (blk, blk), 1)
    return (k_idx <= q_idx) & (k_idx >= LPAD)


def _pipelined_key_blocks(i, scores, consume, mask_first):
    scores(0, 0)

    def pair(t, carry):
        @pl.when(t == 0)
        def _():
            mask_first()

        scores(2 * t + 1, 1)
        consume(2 * t, 0, False)
        scores(2 * t + 2, 0)
        consume(2 * t + 1, 1, False)
        return carry

    lax.fori_loop(0, lax.shift_right_logical(i, 1), pair, 0)
    odd = (i & 1) == 1

    @pl.when(jnp.logical_not(odd))
    def _():
        consume(i, 0, True)

    @pl.when(odd)
    def _():
        scores(i, 1)
        consume(i - 1, 0, True)
        consume(i, 1, True)


def _head_out(acc):
    lane = lax.broadcasted_iota(jnp.int32, acc.shape, 1)
    return jnp.where(lane < ONE_LANE, acc / acc[:, ONE_LANE:ONE_LANE + 1], 0.0)


def _head_rms(o, g):
    ms = jnp.sum(o * o, axis=-1, keepdims=True) * (1.0 / ONE_LANE)
    return o * lax.rsqrt(ms + EPS) * g


def _fox_attn_kernel(q_ref, k_ref, vt_ref, g_ref, o_ref, m_ref, acc_ref, sa_ref, sb_ref,
                     *, blk):
    i = pl.program_id(2)
    heads = range(ATTN_HEADS_PER_STEP)
    slot = [slice(hh * SLOT, (hh + 1) * SLOT) for hh in heads]
    q = [q_ref[:, slot[hh]] for hh in heads]
    bufs = (sa_ref, sb_ref)
    m_ref[...] = jnp.full(m_ref.shape, NEG, F32)
    acc_ref[...] = jnp.zeros_like(acc_ref)

    def scores(j, buf):
        rows = pl.ds(_block_start(j, blk), blk)
        for hh in heads:
            bufs[buf][hh] = lax.dot_general(k_ref[rows, slot[hh]], q[hh], _NT,
                                            preferred_element_type=F32)

    def consume(j, buf, masked):
        mask = _causal_valid(i, j, blk) if masked else None
        for hh in heads:
            s = bufs[buf][hh]
            if masked:
                s = jnp.where(mask, s, NEG)
            p, alpha, m_new = _softmax_block(s, m_ref[hh])
            acc_ref[hh] = alpha * acc_ref[hh] + jnp.dot(
                vt_ref[0, hh, j], p, preferred_element_type=F32)
            m_ref[hh] = m_new

    def mask_first():
        mask = _causal_valid(i, 0, blk)
        for hh in heads:
            sa_ref[hh] = jnp.where(mask, sa_ref[hh], NEG)

    _pipelined_key_blocks(i, scores, consume, mask_first)
    for hh in heads:
        o_ref[:, slot[hh]] = _head_rms(_head_out(acc_ref[hh].T), g_ref[...]).astype(BF16)


def _key_spec(seq_len, groups):
    return pl.BlockSpec((seq_len, ATTN_HEADS_PER_STEP * SLOT),
                        lambda b, h, i: (b, groups + h), pipeline_mode=pl.Buffered(1))


def _value_spec(seq_len):
    nk = seq_len // ATTN_BLOCK
    return pl.BlockSpec((1, ATTN_HEADS_PER_STEP, nk, SLOT, ATTN_BLOCK),
                        lambda b, h, i: (b, h, 0, 0, 0), pipeline_mode=pl.Buffered(1))


def _fox_attn(fqk, fvt, gain, batch, seq_len):
    n = fqk.shape[0]
    blk = ATTN_BLOCK
    nq = seq_len // blk
    hps = ATTN_HEADS_PER_STEP
    groups = FX_HEADS // hps
    return pl.pallas_call(
        functools.partial(_fox_attn_kernel, blk=blk),
        grid=(batch, groups, nq),
        in_specs=[pl.BlockSpec((blk, hps * SLOT), lambda b, h, i: (b * nq + i, h)),
                  _key_spec(seq_len, groups),
                  _value_spec(seq_len),
                  pl.BlockSpec((1, SLOT), lambda b, h, i: (0, 0))],
        out_specs=pl.BlockSpec((blk, hps * SLOT), lambda b, h, i: (b * nq + i, h)),
        out_shape=jax.ShapeDtypeStruct((n, HEAD_W), BF16),
        scratch_shapes=[pltpu.VMEM((hps, 1, blk), F32), pltpu.VMEM((hps, SLOT, blk), F32),
                        pltpu.VMEM((hps, blk, blk), F32), pltpu.VMEM((hps, blk, blk), F32)],
        compiler_params=pltpu.CompilerParams(
            dimension_semantics=("arbitrary", "arbitrary", "arbitrary"),
            vmem_limit_bytes=VMEM_LIMIT),
        name="fox_attn",
    )(fqk, fqk, fvt, gain)


def _diff_attn_kernel(q_ref, k_ref, vt_ref, lam_ref, g_ref, o_ref, m_ref, acc_ref,
                      sa_ref, sb_ref, *, blk, lam_init):
    i = pl.program_id(2)
    heads = range(ATTN_HEADS_PER_STEP)
    slot = lambda s: slice(s * SLOT, (s + 1) * SLOT)
    lane = lax.broadcasted_iota(jnp.int32, (blk, SLOT), 1)
    q = []
    for hh in heads:
        q_h = q_ref[:, slot(hh)]
        q.append([jnp.where((lane >= c * DA_DQK) & (lane < (c + 1) * DA_DQK), q_h,
                            jnp.zeros_like(q_h)) for c in range(2)])
    bufs = (sa_ref, sb_ref)
    m_ref[...] = jnp.full(m_ref.shape, NEG, F32)
    acc_ref[...] = jnp.zeros_like(acc_ref)

    def scores(j, buf):
        rows = pl.ds(_block_start(j, blk), blk)
        for hh in heads:
            k_h = k_ref[rows, slot(hh)]
            for c in range(2):
                bufs[buf][2 * hh + c] = lax.dot_general(k_h, q[hh][c], _NT,
                                                        preferred_element_type=F32)

    def consume(j, buf, masked):
        mask = _causal_valid(i, j, blk) if masked else None
        for hh in heads:
            ps, alphas = [], []
            for c in range(2):
                hc = 2 * hh + c
                s = bufs[buf][hc]
                if masked:
                    s = jnp.where(mask, s, NEG)
                p, alpha, m_new = _softmax_block(s, m_ref[hc])
                m_ref[hc] = m_new
                ps.append(p)
                alphas.append(alpha)
            pv = jnp.dot(vt_ref[0, hh, j], jnp.concatenate(ps, axis=1),
                         preferred_element_type=F32)
            for c in range(2):
                hc = 2 * hh + c
                acc_ref[hc] = alphas[c] * acc_ref[hc] + pv[:, c * blk:(c + 1) * blk]

    def mask_first():
        mask = _causal_valid(i, 0, blk)
        for hc in range(2 * ATTN_HEADS_PER_STEP):
            sa_ref[hc] = jnp.where(mask, sa_ref[hc], NEG)

    _pipelined_key_blocks(i, scores, consume, mask_first)
    lv = lam_ref[...]
    lam = (jnp.exp(jnp.sum(lv[0:1] * lv[1:2], axis=-1, keepdims=True))
           - jnp.exp(jnp.sum(lv[2:3] * lv[3:4], axis=-1, keepdims=True)) + lam_init)
    for hh in heads:
        o = _head_out(acc_ref[2 * hh].T) - lam * _head_out(acc_ref[2 * hh + 1].T)
        o_ref[:, slot(hh)] = (_head_rms(o, g_ref[...]) * (1.0 - lam_init)).astype(BF16)


def _diff_attn(dqk, dvt, lam_vecs, gain, lam_init, batch, seq_len):
    n = dqk.shape[0]
    blk = ATTN_BLOCK
    nq = seq_len // blk
    hps = ATTN_HEADS_PER_STEP
    groups = DA_HEADS // hps
    return pl.pallas_call(
        functools.partial(_diff_attn_kernel, blk=blk, lam_init=lam_init),
        grid=(batch, groups, nq),
        in_specs=[pl.BlockSpec((blk, hps * SLOT), lambda b, h, i: (b * nq + i, h)),
                  _key_spec(seq_len, groups),
                  _value_spec(seq_len),
                  pl.BlockSpec((8, LANES), lambda b, h, i: (0, 0)),
                  pl.BlockSpec((1, SLOT), lambda b, h, i: (0, 0))],
        out_specs=pl.BlockSpec((blk, hps * SLOT), lambda b, h, i: (b * nq + i, h)),
        out_shape=jax.ShapeDtypeStruct((n, HEAD_W), BF16),
        scratch_shapes=[pltpu.VMEM((2 * hps, 1, blk), F32),
                        pltpu.VMEM((2 * hps, SLOT, blk), F32),
                        pltpu.VMEM((2 * hps, blk, blk), F32),
                        pltpu.VMEM((2 * hps, blk, blk), F32)],
        compiler_params=pltpu.CompilerParams(
            dimension_semantics=("arbitrary", "arbitrary", "arbitrary"),
            vmem_limit_bytes=VMEM_LIMIT),
        name="diff_attn",
    )(dqk, dqk, dvt, lam_vecs, gain)


def _hgrn_kernel(q_ref, i_ref, g_ref, f_ref, lb_ref, gain_ref, o_ref,
                 st_ref, hs_ref, *, tl):
    j = pl.program_id(2)

    @pl.when(j == 0)
    def _():
        st_ref[...] = jnp.zeros_like(st_ref)

    c_rows, sub = HG_CHUNK, HG_SUB
    tri = (lax.broadcasted_iota(jnp.int32, (c_rows, c_rows), 1)
           <= lax.broadcasted_iota(jnp.int32, (c_rows, c_rows), 0)).astype(F32)
    row_c = lax.broadcasted_iota(jnp.int32, (c_rows, 1), 0)
    t_sub = lax.broadcasted_iota(jnp.int32, (sub, 1), 0)
    lane_c = lax.broadcasted_iota(jnp.int32, (sub, c_rows), 1)

    t_all = lax.broadcasted_iota(jnp.int32, (c_rows, c_rows), 0)
    s_all = lax.broadcasted_iota(jnp.int32, (c_rows, c_rows), 1)
    level_masks = []
    g = sub
    while g < c_rows:
        level_masks.append(((t_all // g) == (s_all // g) + 1) & ((s_all // g) % 2 == 0))
        g *= 2

    heads = range(HG_PER_STEP)
    cols = [slice(hh * HG_DK, (hh + 1) * HG_DK) for hh in heads]

    def chunk(c, carry):
        r0 = pl.multiple_of(c * c_rows, c_rows)
        rows = pl.ds(r0, c_rows)
        valid = (j * tl + r0 + row_c) >= LPAD
        G, kk, qs, v, st, o_inter = [], [], [], [], [], []
        for hh in heads:
            lb = lb_ref[:, cols[hh]]
            sig = _sigmoid(f_ref[rows, cols[hh]])
            log_f = jnp.where(valid, jnp.log2(jnp.maximum(lb + (1.0 - lb) * sig, TINY)), 0.0)
            kk.append(jnp.where(valid, (1.0 - lb) * (1.0 - sig), 0.0))
            G.append(jnp.dot(tri, log_f, precision=lax.Precision.HIGHEST,
                             preferred_element_type=F32))
            qf = q_ref[rows, cols[hh]].astype(F32)
            qs.append(qf * _sigmoid(qf) * (HG_DK ** -0.5))
            v.append(i_ref[rows, cols[hh]])
            hs_ref[hh] = G[hh] - jnp.log2(kk[hh])
            st.append(st_ref[hh])
            o_inter.append(lax.dot_general(
                (qs[hh] * jnp.exp2(G[hh])).astype(BF16), st[hh].astype(BF16), _NT,
                preferred_element_type=F32))
        a_mat = [jnp.zeros((c_rows, c_rows), F32) for _ in heads]
        g = sub
        for mask in level_masks:
            for hh in heads:
                ref = jnp.concatenate(
                    [jnp.broadcast_to(G[hh][p + g - 1:p + g], (2 * g, HG_DK))
                     for p in range(0, c_rows, 2 * g)], axis=0)
                e = jnp.exp2(-jnp.abs(G[hh] - ref))
                part = lax.dot_general((qs[hh] * e).astype(BF16), (kk[hh] * e).astype(BF16),
                                       _NT, preferred_element_type=F32)
                a_mat[hh] = jnp.where(mask, part, a_mat[hh])
            g *= 2
        a_rows = [[] for _ in heads]
        for b in range(c_rows // sub):
            lo = b * sub
            for hh in heads:
                q_b = qs[hh][lo:lo + sub]
                g_b = G[hh][lo:lo + sub]
                a_blk = a_mat[hh][lo:lo + sub]
                for s in range(lo, lo + sub):
                    y = q_b * jnp.exp2(g_b - hs_ref[hh, s:s + 1, :])
                    a_blk = jnp.where(lane_c == s, jnp.sum(y, axis=-1, keepdims=True), a_blk)
                a_rows[hh].append(jnp.where(lane_c <= t_sub + lo, a_blk, 0.0))
        for hh in heads:
            a_full = jnp.concatenate(a_rows[hh], axis=0).astype(BF16)
            o = o_inter[hh] + jnp.dot(a_full, v[hh], preferred_element_type=F32)
            g_last = G[hh][c_rows - 1:c_rows]
            kd = kk[hh] * jnp.exp2(g_last - G[hh])
            st_ref[hh] = st[hh] * jnp.exp2(g_last) + lax.dot_general(
                v[hh], kd.astype(BF16), _TN, preferred_element_type=F32)
            gate = g_ref[rows, cols[hh]].astype(F32)
            o = _rms(o, gain_ref[...]) * (gate * _sigmoid(gate))
            o_ref[rows, cols[hh]] = o.astype(BF16)
        return carry

    lax.fori_loop(0, tl // c_rows, chunk, 0)


def _hgrn(hqig, hf, lb_row, gain, batch, seq_len):
    n = hqig.shape[0]
    tl = ROW_TILE
    nt = seq_len // tl
    w = HG_PER_STEP * HG_DK
    groups = HG_HEADS // HG_PER_STEP
    col = lambda off: pl.BlockSpec((tl, w), lambda b, h, j: (b * nt + j, off + h))
    return pl.pallas_call(
        functools.partial(_hgrn_kernel, tl=tl),
        grid=(batch, groups, nt),
        in_specs=[col(0), col(groups), col(2 * groups), col(0),
                  pl.BlockSpec((1, w), lambda b, h, j: (0, h)),
                  pl.BlockSpec((1, HG_DK), lambda b, h, j: (0, 0))],
        out_specs=col(0),
        out_shape=jax.ShapeDtypeStruct((n, HG_WIDTH), BF16),
        scratch_shapes=[pltpu.VMEM((HG_PER_STEP, HG_DK, HG_DK), F32),
                        pltpu.VMEM((HG_PER_STEP, HG_CHUNK, HG_DK), F32)],
        compiler_params=pltpu.CompilerParams(
            dimension_semantics=("arbitrary", "arbitrary", "arbitrary")),
        name="hgrn2",
    )(hqig, hqig, hqig, hf, lb_row, gain)


def _out_router_kernel(oa_ref, ob_ref, oc_ref, h_ref, wo_ref, g_ref, wrh_ref, wrl_ref, br_ref,
                       h1_ref, u_ref, route_ref, cnt_ref, carry_ref):
    @pl.when(pl.program_id(0) == 0)
    def _():
        carry_ref[...] = jnp.zeros_like(carry_ref)

    h1 = (h_ref[...]
          + jnp.dot(oa_ref[...], wo_ref[0:512, :], preferred_element_type=F32)
          + jnp.dot(ob_ref[...], wo_ref[512:1024, :], preferred_element_type=F32)
          + jnp.dot(oc_ref[...], wo_ref[1024:1536, :], preferred_element_type=F32))
    h1_ref[...] = h1
    u = _rms(h1, g_ref[...])
    _store_token_tiles(u_ref, u)
    u_hi = u.astype(BF16)
    u_lo = (u - u_hi.astype(F32)).astype(BF16)
    logits = (jnp.dot(u_hi, wrh_ref[...], preferred_element_type=F32)
              + (jnp.dot(u_hi, wrl_ref[...], preferred_element_type=F32)
                 + jnp.dot(u_lo, wrh_ref[...], preferred_element_type=F32))
              + br_ref[...])
    tm = logits.shape[0]
    lane = lax.broadcasted_iota(jnp.int32, (tm, LANES), 1).astype(F32)
    big = float(LANES)
    is_g = (lane >= N_EXPERTS) & (lane < N_EXPERTS + N_GROUPS)
    gl = jnp.where(is_g, logits, -jnp.inf)
    gmax = jnp.max(gl, axis=-1, keepdims=True)
    gsel = jnp.min(jnp.where(gl == gmax, lane, big), axis=-1, keepdims=True) - N_EXPERTS
    p_g = 1.0 / jnp.sum(jnp.exp(gl - gmax), axis=-1, keepdims=True)
    lo = gsel * EXPERTS_PER_GROUP
    el = jnp.where((lane >= lo) & (lane < lo + EXPERTS_PER_GROUP), logits, -jnp.inf)
    m1 = jnp.max(el, axis=-1, keepdims=True)
    i1 = jnp.min(jnp.where(el == m1, lane, big), axis=-1, keepdims=True)
    el2 = jnp.where(lane == i1, -jnp.inf, el)
    m2 = jnp.max(el2, axis=-1, keepdims=True)
    i2 = jnp.min(jnp.where(el2 == m2, lane, big), axis=-1, keepdims=True)
    r = jnp.exp(m2 - m1)
    gate1 = p_g / (1.0 + r)
    gate2 = gate1 * r
    oh1 = lane == i1
    oh2 = lane == i2
    onehot = jnp.where(oh1 | oh2, 1.0, 0.0)
    tri = (lax.broadcasted_iota(jnp.int32, (tm, tm), 1)
           < lax.broadcasted_iota(jnp.int32, (tm, tm), 0)).astype(BF16)
    before = jnp.dot(tri, onehot.astype(BF16), preferred_element_type=F32) + carry_ref[...]
    rank1 = jnp.sum(jnp.where(oh1, before, 0.0), axis=-1, keepdims=True)
    rank2 = jnp.sum(jnp.where(oh2, before, 0.0), axis=-1, keepdims=True)
    total = carry_ref[...] + jnp.sum(onehot, axis=0, keepdims=True)
    carry_ref[...] = total
    cnt_ref[...] = total
    route = jnp.where(lane == 0, i1, 0.0)
    for idx, val in ((1, i2), (2, gate1), (3, gate2), (4, rank1), (5, rank2)):
        route = jnp.where(lane == idx, val, route)
    route_ref[...] = route


def _out_router(oa, ob, oc, h, w_out, g, w_rt, b_rt):
    n = h.shape[0]
    w_rt_hi = w_rt.astype(BF16)
    w_rt_lo = (w_rt - w_rt_hi.astype(F32)).astype(BF16)
    tm = ROUTER_TILE
    row = lambda w: pl.BlockSpec((tm, w), lambda i: (i, 0))
    const = lambda r, c: pl.BlockSpec((r, c), lambda i: (0, 0))
    return pl.pallas_call(
        _out_router_kernel,
        grid=(n // tm,),
        in_specs=[row(HG_WIDTH), row(HEAD_W), row(HEAD_W), row(D_MODEL),
                  const(HG_WIDTH + 2 * HEAD_W, D_MODEL),
                  const(1, D_MODEL), const(D_MODEL, LANES), const(D_MODEL, LANES),
                  const(1, LANES)],
        out_specs=[row(D_MODEL), pl.BlockSpec((tm,) + TOKEN_TILE, lambda i: (i, 0, 0)),
                   row(LANES), const(1, LANES)],
        out_shape=[jax.ShapeDtypeStruct((n, D_MODEL), F32),
                   jax.ShapeDtypeStruct((n,) + TOKEN_TILE, F32),
                   jax.ShapeDtypeStruct((n, LANES), F32),
                   jax.ShapeDtypeStruct((1, LANES), F32)],
        scratch_shapes=[pltpu.VMEM((1, LANES), F32)],
        compiler_params=pltpu.CompilerParams(dimension_semantics=("arbitrary",)),
        name="out_router",
    )(oa, ob, oc, h, w_out, g, w_rt_hi, w_rt_lo, b_rt)


def _dispatch_kernel(dest_ref, unused_ref, u_ref, xs_in_ref, xs_ref, back_ref, sem,
                     *, tm, n, tb):
    del xs_in_ref
    i = pl.program_id(0)
    base = i * tm

    @pl.when(i == 0)
    def _():
        def fill(s, carry):
            parity = lax.shift_right_logical(s, tb.bit_length() - 1) & 1
            back_ref[s] = 2 * n + parity * tb + (s & (tb - 1))
            return carry

        n_ranges = unused_ref.shape[0] // 2

        def one_range(e, carry):
            return lax.fori_loop(unused_ref[e], unused_ref[n_ranges + e], fill, carry)

        lax.fori_loop(0, n_ranges, one_range, 0)

    def issue(r, carry):
        for k in range(2):
            src = k * n + base + r
            d = dest_ref[src]
            back_ref[d] = src
            pltpu.make_async_copy(u_ref.at[pl.ds(r, 1)], xs_ref.at[pl.ds(d, 1)], sem).start()
        return carry

    lax.fori_loop(0, tm, issue, 0, unroll=8)
    for _ in range(2):
        pltpu.make_async_copy(u_ref, xs_ref.at[pl.ds(0, tm)], sem).wait()


def _dispatch(dest, unused, u, xs_zero):
    n = u.shape[0]
    tm = ROW_TILE
    p_rows = xs_zero.shape[0]
    return pl.pallas_call(
        functools.partial(_dispatch_kernel, tm=tm, n=n, tb=MOE_BLOCK),
        grid_spec=pltpu.PrefetchScalarGridSpec(
            num_scalar_prefetch=2,
            grid=(n // tm,),
            in_specs=[pl.BlockSpec((tm,) + TOKEN_TILE, lambda i, d, un: (i, 0, 0)),
                      pl.BlockSpec(memory_space=pl.ANY)],
            out_specs=[pl.BlockSpec(memory_space=pl.ANY),
                       pl.BlockSpec(memory_space=pltpu.SMEM)],
            scratch_shapes=[pltpu.SemaphoreType.DMA(())]),
        out_shape=[jax.ShapeDtypeStruct(xs_zero.shape, xs_zero.dtype),
                   jax.ShapeDtypeStruct((p_rows,), jnp.int32)],
        input_output_aliases={3: 0},
        compiler_params=pltpu.CompilerParams(dimension_semantics=("arbitrary",)),
        name="moe_dispatch",
    )(dest, unused, u, xs_zero)


def _expert_kernel(be_ref, na_ref, back_ref, x_ref, w1_ref, w3_ref, w2_ref, out_ref,
                   y_ref, w1b_ref, w3b_ref, w2b_ref, sem, *, tb):
    j = pl.program_id(0)
    na = na_ref[0]
    cur = j % 2

    def scatter(block, buf):
        for r in range(tb):
            pltpu.make_async_copy(y_ref.at[buf, pl.ds(r, 1)],
                                  out_ref.at[pl.ds(back_ref[block * tb + r], 1)],
                                  sem.at[buf]).start()

    def compute(buf):
        x = _load_token_tiles(x_ref).astype(BF16)
        a = jnp.dot(x, w1b_ref[...], preferred_element_type=F32)
        b = jnp.dot(x, w3b_ref[...], preferred_element_type=F32)
        act = (a * _sigmoid(a) * b).astype(BF16)
        _store_token_tiles(y_ref.at[buf],
                           jnp.dot(act, w2b_ref[...], preferred_element_type=F32))

    new_expert = (j == 0) | (be_ref[j] != be_ref[jnp.maximum(j - 1, 0)])

    @pl.when((j < na) & new_expert)
    def _():
        w1b_ref[...] = w1_ref[0].astype(BF16)
        w3b_ref[...] = w3_ref[0].astype(BF16)
        w2b_ref[...] = w2_ref[0].astype(BF16)

    @pl.when((j >= 2) & (j < na + 2))
    def _():
        pltpu.make_async_copy(y_ref.at[cur], out_ref.at[pl.ds(0, tb)], sem.at[cur]).wait()

    @pl.when(j == 0)
    def _():
        y_ref[...] = jnp.zeros_like(y_ref)
        first_spare = out_ref.shape[0] - 2 * tb
        spare = [pltpu.make_async_copy(
            y_ref.at[b], out_ref.at[pl.ds(first_spare + b * tb, tb)], sem.at[b])
            for b in range(2)]
        for copy in spare:
            copy.start()
        for copy in spare:
            copy.wait()
        compute(cur)

    @pl.when((j >= 1) & (j < na))
    def _():
        scatter(j - 1, 1 - cur)
        compute(cur)

    @pl.when(j == na)
    def _():
        scatter(j - 1, 1 - cur)


def _experts(blk_expert, n_active, slot_back, xs, w1, w3, w2, layer, n):
    p_rows = xs.shape[0]
    tb = MOE_BLOCK
    last = lambda j, na: jnp.maximum(jnp.minimum(j, na[0] - 1), 0)
    w_spec = lambda r, c: pl.BlockSpec((None, 1, r, c),
                                       lambda j, be, na, back: (layer, be[j], 0, 0))
    return pl.pallas_call(
        functools.partial(_expert_kernel, tb=tb),
        grid_spec=pltpu.PrefetchScalarGridSpec(
            num_scalar_prefetch=3,
            grid=(p_rows // tb + 1,),
            in_specs=[pl.BlockSpec((tb,) + TOKEN_TILE,
                                   lambda j, be, na, back: (last(j, na), 0, 0)),
                      w_spec(D_MODEL, D_EXPERT), w_spec(D_MODEL, D_EXPERT),
                      w_spec(D_EXPERT, D_MODEL)],
            out_specs=pl.BlockSpec(memory_space=pl.ANY),
            scratch_shapes=[pltpu.VMEM((2, tb) + TOKEN_TILE, F32),
                            pltpu.VMEM((D_MODEL, D_EXPERT), BF16),
                            pltpu.VMEM((D_MODEL, D_EXPERT), BF16),
                            pltpu.VMEM((D_EXPERT, D_MODEL), BF16),
                            pltpu.SemaphoreType.DMA((2,))]),
        out_shape=jax.ShapeDtypeStruct((2 * n + 2 * tb,) + TOKEN_TILE, F32),
        compiler_params=pltpu.CompilerParams(dimension_semantics=("arbitrary",),
                                             vmem_limit_bytes=VMEM_LIMIT),
        name="moe_experts",
    )(blk_expert, n_active, slot_back, xs, w1, w3, w2)


def _combine_kernel(h_ref, route_ref, gfin_ref, y0_ref, y1_ref, o_ref, *, final):
    route = route_ref[...]
    out = (h_ref[...] + route[:, 2:3] * _load_token_tiles(y0_ref)
           + route[:, 3:4] * _load_token_tiles(y1_ref))
    if final:
        out = _rms(out, gfin_ref[...])
    o_ref[...] = out


def _combine(h1, route, g_final, ys, batch, seq_len, final):
    n = h1.shape[0]
    tm = ROW_TILE
    tiles_per_seq = seq_len // tm
    skip_tiles = (LPAD + N_META) // tm if final else 0
    out_tiles = tiles_per_seq - skip_tiles
    tile = lambda b, i: b * tiles_per_seq + skip_tiles + i
    in_row = lambda w: pl.BlockSpec((tm, w), lambda b, i: (tile(b, i), 0))
    y_rows = lambda k: pl.BlockSpec((tm,) + TOKEN_TILE,
                                    lambda b, i: (k * (n // tm) + tile(b, i), 0, 0))
    return pl.pallas_call(
        functools.partial(_combine_kernel, final=final),
        grid=(batch, out_tiles),
        in_specs=[in_row(D_MODEL), in_row(LANES),
                  pl.BlockSpec((1, D_MODEL), lambda b, i: (0, 0)),
                  y_rows(0), y_rows(1)],
        out_specs=pl.BlockSpec((tm, D_MODEL), lambda b, i: (b * out_tiles + i, 0)),
        out_shape=jax.ShapeDtypeStruct((batch * out_tiles * tm, D_MODEL), F32),
        compiler_params=pltpu.CompilerParams(dimension_semantics=("arbitrary", "arbitrary")),
        name="moe_combine",
    )(h1, route, g_final, ys, ys)


def _to_slots(m):
    lead = m.shape[:-1]
    x = m.reshape(lead + (4, 64))
    x = jnp.pad(x, [(0, 0)] * (len(lead) + 1) + [(0, SLOT - 64)])
    return x.reshape(lead + (HEAD_W,))


def _rope_partner(m):
    lead = m.shape[:-1]
    x = m.reshape(lead + (DA_WIDTH // DA_DQK, 2, DA_DQK // 2))
    return x[..., ::-1, :].reshape(lead + (DA_WIDTH,))


def _rope_tables(seq_len):
    half = DA_DQK // 2
    pos = (jnp.arange(seq_len) - LPAD).astype(F32)
    inv = ROPE_THETA ** (-jnp.arange(0, DA_DQK, 2, dtype=F32) / DA_DQK)
    ang = pos[:, None] * inv[None, :]
    groups = DA_WIDTH // DA_DQK
    cos = jnp.tile(jnp.cos(ang), (1, 2 * groups))
    sin = jnp.tile(jnp.concatenate([-jnp.sin(ang), jnp.sin(ang)], axis=1), (1, groups))
    q_scale = DA_DQK ** -0.5 * LOG2E
    lay = lambda t: jnp.concatenate([_to_slots(t * q_scale), _to_slots(t)], axis=1)
    return lay(cos), lay(sin)


def _in_weights(w):
    hq, hf, hi, hg = (w[:, i * 512:(i + 1) * 512] for i in range(4))
    dq, dk, dv = (w[:, 2048 + i * 256:2048 + (i + 1) * 256] for i in range(3))
    fq, fk, fv = (w[:, 2816 + i * 256:2816 + (i + 1) * 256] for i in range(3))
    ff = w[:, 3584:3588]
    cat = jnp.concatenate(
        [hq, hi, hg, hf,
         _to_slots(dq), _to_slots(dk),
         _to_slots(_rope_partner(dq)), _to_slots(_rope_partner(dk)),
         _to_slots(fq * (FX_DH ** -0.5)), _to_slots(fk),
         ff, jnp.zeros((D_MODEL, LANES - FX_HEADS), w.dtype)],
        axis=1)
    w_vt = jnp.concatenate([_to_slots(dv), _to_slots(fv)], axis=1).T
    return cat.astype(BF16), w_vt.astype(BF16)


def _pad_lanes(v, width=LANES):
    return jnp.zeros((1, width), F32).at[0, :v.shape[0]].set(v.astype(F32))


def kernel(x, meta_tokens, norm_mix, w_in, hgrn_lb, hgrn_norm, diff_lambda, diff_norm,
           fox_bias, fox_norm, w_out, norm_ffn, w_group, b_group, w_router, b_router,
           w1, w3, w2, norm_final):
    batch, seq, d = x.shape
    depth = w_in.shape[0]
    seq_len = LPAD + N_META + seq
    n = batch * seq_len
    pad = jnp.zeros((batch, LPAD, d), x.dtype)
    meta = jnp.broadcast_to(meta_tokens.astype(x.dtype)[None], (batch, N_META, d))
    h = jnp.concatenate([pad, meta, x], axis=1).reshape(n, d)

    s_lb = jax.nn.softmax(hgrn_lb.astype(F32), axis=0)
    lb_all = jnp.cumsum(s_lb, axis=0) - s_lb[0]
    cos_t, sin_t = _rope_tables(seq_len)

    tb = MOE_BLOCK
    n_blocks = (2 * n) // tb + N_EXPERTS
    p_rows = n_blocks * tb

    for layer in range(depth):
        lam_init = 0.8 - 0.6 * math.exp(-0.3 * layer)
        w_cat, w_vt = _in_weights(w_in[layer])
        hqig, hf, dqk, fqk, ff, dvt, fvt = _in_proj(
            h, norm_mix[layer][None, :], w_cat, w_vt, cos_t, sin_t, batch, seq_len)

        o_a = _hgrn(hqig, hf, lb_all[layer][None, :], hgrn_norm[layer][None, :],
                    batch, seq_len)

        lam_vecs = jnp.zeros((8, LANES), F32).at[:4, :DA_DQK].set(diff_lambda[layer].astype(F32))
        o_b = _diff_attn(dqk, dvt, lam_vecs, _pad_lanes(diff_norm[layer]), lam_init,
                         batch, seq_len)

        fqk_aug = _fox_prep(ff.reshape(batch, seq_len, LANES), _pad_lanes(fox_bias[layer]),
                            fqk, batch, seq_len)
        o_c = _fox_attn(fqk_aug, fvt, _pad_lanes(fox_norm[layer]), batch, seq_len)

        wo = w_out[layer]
        slot_rows = lambda m: _to_slots(m.T).T
        wo_cat = jnp.concatenate(
            [wo[:HG_WIDTH], slot_rows(wo[HG_WIDTH:HG_WIDTH + DA_WIDTH]),
             slot_rows(wo[HG_WIDTH + DA_WIDTH:])], axis=0).astype(BF16)
        w_rt = jnp.concatenate(
            [w_router[layer], w_group[layer],
             jnp.zeros((d, LANES - N_EXPERTS - N_GROUPS), F32)], axis=1)
        b_rt = _pad_lanes(jnp.concatenate([b_router[layer], b_group[layer]]))
        h1, u2, route, counts = _out_router(
            o_a, o_b, o_c, h, wo_cat, norm_ffn[layer][None, :], w_rt, b_rt)

        cnt = counts[0, :N_EXPERTS].astype(jnp.int32)
        padded = (cnt + tb - 1) // tb * tb
        p_end = jnp.cumsum(padded)
        p_start = p_end - padded
        ids = route[:, 0:2].astype(jnp.int32)
        ranks = route[:, 4:6].astype(jnp.int32)
        dest = (p_start[ids] + ranks).T.reshape(2 * n)
        blk_start = jnp.arange(n_blocks + 1, dtype=jnp.int32) * tb
        blk_expert = jnp.minimum(
            jnp.sum((p_end[None, :] <= blk_start[:, None]).astype(jnp.int32), axis=1),
            N_EXPERTS - 1)
        n_active = (p_end[-1:] // tb).astype(jnp.int32)

        unused = jnp.concatenate(
            [p_start + cnt, p_end[-1:], p_end, jnp.full((1,), p_rows)]).astype(jnp.int32)
        xs, slot_back = _dispatch(dest, unused, u2, jnp.zeros((p_rows,) + TOKEN_TILE, F32))
        ys = _experts(blk_expert, n_active, slot_back, xs, w1, w3, w2, layer, n)
        h = _combine(h1, route, norm_final[None, :], ys, batch, seq_len,
                     final=(layer == depth - 1))

    return h.reshape(batch, seq, d)
```

```python
import functools
import math

import jax
import jax.numpy as jnp
from jax import lax
from jax.experimental import pallas as pl
from jax.experimental.pallas import tpu as pltpu

F32 = jnp.float32
BF16 = jnp.bfloat16

D_MODEL = 1024
N_META = 16
HG_HEADS = 4
HG_DK = 128
HG_WIDTH = 512
HG_CHUNK = 64
HG_SUB = 8
HG_PER_STEP = 4
DA_HEADS = 4
DA_DV = 64
DA_DQK = 32
DA_WIDTH = 256
FX_HEADS = 4
FX_DH = 64
FX_WIDTH = 256
N_GROUPS = 4
EXPERTS_PER_GROUP = 8
N_EXPERTS = 32
D_EXPERT = 512
ROPE_THETA = 10000.0
EPS = 1e-6
NEG = -1e30
TINY = 1e-30
LOG2E = 1.4426950408889634

LANES = 128
TOKEN_TILE = (D_MODEL // LANES, LANES)
SLOT = LANES
HEAD_W = 4 * SLOT
ROW_TILE = 256
PROJ_TILE = 512
ROUTER_TILE = 512
ATTN_BLOCK = 512
ATTN_HEADS_PER_STEP = 4
MOE_BLOCK = 256
LPAD = ATTN_BLOCK - N_META
FIRST_KEYS = LANES
assert ATTN_BLOCK - FIRST_KEYS <= LPAD
ONE_LANE = 64
VMEM_LIMIT = 56 * 1024 * 1024

_C_HQIG = (0, 1536)
_C_HF = (1536, 2048)
_C_DQK = (2048, 3072)
_C_DQKP = (3072, 4096)
_C_FQ = (4096, 4608)
_C_FK = (4608, 5120)
_C_FF = (5120, 5248)
IN_COLS = 5248

_NT = (((1,), (1,)), ((), ()))
_TN = (((0,), (0,)), ((), ()))


def _store_token_tiles(ref, x):
    for c in range(TOKEN_TILE[0]):
        ref[:, c, :] = x[:, c * LANES:(c + 1) * LANES]


def _load_token_tiles(ref):
    return jnp.concatenate([ref[:, c, :] for c in range(TOKEN_TILE[0])], axis=1)


def _sigmoid(x):
    return 1.0 / (1.0 + jnp.exp(-x))


def _rms(x, g):
    return x * lax.rsqrt(jnp.mean(x * x, axis=-1, keepdims=True) + EPS) * g


def _in_proj_kernel(h_ref, g_ref, w_ref, wvt_ref, cos_ref, sin_ref,
                    hqig_ref, hf_ref, dqk_ref, fqk_ref, ff_ref, dvt_ref, fvt_ref):
    u = _rms(h_ref[...], g_ref[...]).astype(BF16)

    def mm(c):
        return jnp.dot(u, w_ref[:, c[0]:c[1]], preferred_element_type=F32)

    hqig_ref[...] = mm(_C_HQIG).astype(BF16)
    hf_ref[...] = mm(_C_HF)
    dqk_ref[...] = (mm(_C_DQK) * cos_ref[...] + mm(_C_DQKP) * sin_ref[...]).astype(BF16)
    fqk_ref[:, 0:HEAD_W] = (mm(_C_FQ) * LOG2E).astype(BF16)
    fqk_ref[:, HEAD_W:2 * HEAD_W] = mm(_C_FK).astype(BF16)
    ff_ref[...] = mm(_C_FF)
    vt = lax.dot_general(wvt_ref[...], u, _NT, preferred_element_type=F32)
    slot_row = lax.broadcasted_iota(jnp.int32, vt.shape, 0) % SLOT
    vt = jnp.where(slot_row == ONE_LANE, 1.0, vt).astype(BF16)
    for h in range(4):
        dvt_ref[0, h, 0] = vt[h * SLOT:(h + 1) * SLOT]
        fvt_ref[0, h, 0] = vt[HEAD_W + h * SLOT:HEAD_W + (h + 1) * SLOT]


def _in_proj(h, g, w_cat, w_vt, cos_t, sin_t, batch, seq_len):
    n = h.shape[0]
    tm = PROJ_TILE
    assert tm == ATTN_BLOCK
    nk = seq_len // tm
    row = lambda w: pl.BlockSpec((tm, w), lambda i: (i, 0))
    tab = pl.BlockSpec((tm, 2 * HEAD_W), lambda i: (i % nk, 0))
    once = lambda r, c: pl.BlockSpec((r, c), lambda i: (0, 0), pipeline_mode=pl.Buffered(1))
    widths = (1536, 512, 2 * HEAD_W, 2 * HEAD_W, LANES)
    dtypes = (BF16, F32, BF16, BF16, F32)
    vt_spec = pl.BlockSpec((1, 4, 1, SLOT, tm), lambda i: (i // nk, 0, i % nk, 0, 0))
    vt_shape = jax.ShapeDtypeStruct((batch, 4, nk, SLOT, tm), BF16)
    return pl.pallas_call(
        _in_proj_kernel,
        grid=(n // tm,),
        in_specs=[row(D_MODEL),
                  pl.BlockSpec((1, D_MODEL), lambda i: (0, 0)),
                  once(D_MODEL, IN_COLS), once(2 * HEAD_W, D_MODEL),
                  tab, tab],
        out_specs=[row(w) for w in widths] + [vt_spec, vt_spec],
        out_shape=[jax.ShapeDtypeStruct((n, w), t) for w, t in zip(widths, dtypes)]
        + [vt_shape, vt_shape],
        compiler_params=pltpu.CompilerParams(dimension_semantics=("arbitrary",),
                                             vmem_limit_bytes=VMEM_LIMIT),
        name="in_proj",
    )(h, g, w_cat, w_vt, cos_t, sin_t)


def _bf16_split3(x):
    hi = x.astype(BF16).astype(F32)
    r = x - hi
    mid = r.astype(BF16).astype(F32)
    return hi, mid, r - mid


def _fox_prep_kernel(ff_ref, b_ref, qk_ref, o_ref, carry_ref):
    @pl.when(pl.program_id(1) == 0)
    def _():
        carry_ref[...] = jnp.zeros_like(carry_ref)

    x = ff_ref[0] + b_ref[...]
    lf = jnp.minimum(x, 0.0) - jnp.log(1.0 + jnp.exp(-jnp.abs(x)))
    t = x.shape[0]
    tri = (lax.broadcasted_iota(jnp.int32, (t, t), 1)
           <= lax.broadcasted_iota(jnp.int32, (t, t), 0)).astype(F32)
    cs = jnp.dot(tri, lf, precision=lax.Precision.HIGHEST,
                 preferred_element_type=F32) + carry_ref[...]
    carry_ref[...] = cs[t - 1:t, :]
    parts = _bf16_split3(cs * LOG2E)
    lane = lax.broadcasted_iota(jnp.int32, (t, SLOT), 1)
    d0 = FX_DH
    for h in range(FX_HEADS):
        q = qk_ref[:, h * SLOT:(h + 1) * SLOT].astype(F32)
        k = qk_ref[:, HEAD_W + h * SLOT:HEAD_W + (h + 1) * SLOT].astype(F32)
        for i, part in enumerate(parts):
            col = part[:, h:h + 1]
            q = jnp.where(lane == d0 + i, col, q)
            k = jnp.where(lane == d0 + 3 + i, -col, k)
        q = jnp.where((lane >= d0 + 3) & (lane < d0 + 6), 1.0, q)
        k = jnp.where((lane >= d0) & (lane < d0 + 3), 1.0, k)
        o_ref[:, h * SLOT:(h + 1) * SLOT] = q.astype(BF16)
        o_ref[:, HEAD_W + h * SLOT:HEAD_W + (h + 1) * SLOT] = k.astype(BF16)


def _fox_prep(ff, bias_row, fqk, batch, seq_len):
    n = fqk.shape[0]
    t = ROW_TILE
    nt = seq_len // t
    return pl.pallas_call(
        _fox_prep_kernel,
        grid=(batch, nt),
        in_specs=[pl.BlockSpec((1, t, LANES), lambda b, j: (b, j, 0)),
                  pl.BlockSpec((1, LANES), lambda b, j: (0, 0)),
                  pl.BlockSpec((t, 2 * HEAD_W), lambda b, j: (b * nt + j, 0))],
        out_specs=pl.BlockSpec((t, 2 * HEAD_W), lambda b, j: (b * nt + j, 0)),
        out_shape=jax.ShapeDtypeStruct((n, 2 * HEAD_W), BF16),
        scratch_shapes=[pltpu.VMEM((1, LANES), F32)],
        compiler_params=pltpu.CompilerParams(dimension_semantics=("arbitrary", "arbitrary")),
        name="fox_prep",
    )(ff, bias_row, fqk)


def _softmax_block(s, m_prev):
    m_new = jnp.maximum(m_prev, jnp.max(s, axis=0, keepdims=True))
    alpha = jnp.exp2(m_prev - m_new)
    p = jnp.exp2(s - m_new).astype(BF16)
    return p, alpha, m_new


def _block_start(j, blk):
    return j * blk if isinstance(j, int) else pl.multiple_of(j * blk, blk)


def _causal_valid(i, j, blk, first_row=0):
    k_idx = j * blk + first_row + lax.broadcasted_iota(jnp.int32, (blk - first_row, blk), 0)
    q_idx = i * blk + lax.broadcasted_iota(jnp.int32, (blk - first_row, blk), 1)
    return (k_idx <= q_idx) & (k_idx >= LPAD)


def _pipelined_key_blocks(i, first, scores, consume):
    first()

    @pl.when(i >= 1)
    def _():
        scores(1, 0)

    def pair(t, carry):
        scores(2 * t + 2, 1)
        consume(2 * t + 1, 0, False)
        scores(2 * t + 3, 0)
        consume(2 * t + 2, 1, False)
        return carry

    lax.fori_loop(0, lax.shift_right_logical(jnp.maximum(i - 1, 0), 1), pair, 0)
    odd = (i & 1) == 1

    @pl.when(odd)
    def _():
        consume(i, 0, True)

    @pl.when((i >= 2) & jnp.logical_not(odd))
    def _():
        scores(i, 1)
        consume(i - 1, 0, False)
        consume(i, 1, True)


def _head_out(acc):
    lane = lax.broadcasted_iota(jnp.int32, acc.shape, 1)
    return jnp.where(lane < ONE_LANE, acc / acc[:, ONE_LANE:ONE_LANE + 1], 0.0)


def _head_rms(o, g):
    ms = jnp.sum(o * o, axis=-1, keepdims=True) * (1.0 / ONE_LANE)
    return o * lax.rsqrt(ms + EPS) * g


def _fox_attn_kernel(q_ref, k_ref, vt_ref, g_ref, o_ref, m_ref, acc_ref, sa_ref, sb_ref,
                     *, blk):
    i = pl.program_id(2)
    heads = range(ATTN_HEADS_PER_STEP)
    slot = [slice(hh * SLOT, (hh + 1) * SLOT) for hh in heads]
    q = [q_ref[:, slot[hh]] for hh in heads]
    bufs = (sa_ref, sb_ref)
    m_ref[...] = jnp.full(m_ref.shape, NEG, F32)
    acc_ref[...] = jnp.zeros_like(acc_ref)

    def scores(j, buf):
        rows = pl.ds(_block_start(j, blk), blk)
        for hh in heads:
            bufs[buf][hh] = lax.dot_general(k_ref[rows, slot[hh]], q[hh], _NT,
                                            preferred_element_type=F32)

    def consume(j, buf, masked):
        mask = _causal_valid(i, j, blk) if masked else None
        for hh in heads:
            s = bufs[buf][hh]
            if masked:
                s = jnp.where(mask, s, NEG)
            p, alpha, m_new = _softmax_block(s, m_ref[hh])
            acc_ref[hh] = alpha * acc_ref[hh] + jnp.dot(
                vt_ref[0, hh, j], p, preferred_element_type=F32)
            m_ref[hh] = m_new

    def first():
        lo = blk - FIRST_KEYS
        mask = _causal_valid(i, 0, blk, lo)
        for hh in heads:
            s = lax.dot_general(k_ref[lo:blk, slot[hh]], q[hh], _NT,
                                preferred_element_type=F32)
            p, alpha, m_new = _softmax_block(jnp.where(mask, s, NEG), m_ref[hh])
            acc_ref[hh] = alpha * acc_ref[hh] + jnp.dot(
                vt_ref[0, hh, 0, :, lo:blk], p, preferred_element_type=F32)
            m_ref[hh] = m_new

    _pipelined_key_blocks(i, first, scores, consume)
    for hh in heads:
        o_ref[:, slot[hh]] = _head_rms(_head_out(acc_ref[hh].T), g_ref[...]).astype(BF16)


def _key_spec(seq_len, groups):
    return pl.BlockSpec((seq_len, ATTN_HEADS_PER_STEP * SLOT),
                        lambda b, h, i: (b, groups + h), pipeline_mode=pl.Buffered(1))


def _value_spec(seq_len):
    nk = seq_len // ATTN_BLOCK
    return pl.BlockSpec((1, ATTN_HEADS_PER_STEP, nk, SLOT, ATTN_BLOCK),
                        lambda b, h, i: (b, h, 0, 0, 0), pipeline_mode=pl.Buffered(1))


def _fox_attn(fqk, fvt, gain, batch, seq_len):
    n = fqk.shape[0]
    blk = ATTN_BLOCK
    nq = seq_len // blk
    hps = ATTN_HEADS_PER_STEP
    groups = FX_HEADS // hps
    return pl.pallas_call(
        functools.partial(_fox_attn_kernel, blk=blk),
        grid=(batch, groups, nq),
        in_specs=[pl.BlockSpec((blk, hps * SLOT), lambda b, h, i: (b * nq + i, h)),
                  _key_spec(seq_len, groups),
                  _value_spec(seq_len),
                  pl.BlockSpec((1, SLOT), lambda b, h, i: (0, 0))],
        out_specs=pl.BlockSpec((blk, hps * SLOT), lambda b, h, i: (b * nq + i, h)),
        out_shape=jax.ShapeDtypeStruct((n, HEAD_W), BF16),
        scratch_shapes=[pltpu.VMEM((hps, 1, blk), F32), pltpu.VMEM((hps, SLOT, blk), F32),
                        pltpu.VMEM((hps, blk, blk), F32), pltpu.VMEM((hps, blk, blk), F32)],
        compiler_params=pltpu.CompilerParams(
            dimension_semantics=("arbitrary", "arbitrary", "arbitrary"),
            vmem_limit_bytes=VMEM_LIMIT),
        name="fox_attn",
    )(fqk, fqk, fvt, gain)


def _diff_attn_kernel(q_ref, k_ref, vt_ref, lam_ref, g_ref, o_ref, m_ref, acc_ref,
                      sa_ref, sb_ref, *, blk, lam_init):
    i = pl.program_id(2)
    heads = range(ATTN_HEADS_PER_STEP)
    slot = lambda s: slice(s * SLOT, (s + 1) * SLOT)
    lane = lax.broadcasted_iota(jnp.int32, (blk, SLOT), 1)
    q = []
    for hh in heads:
        q_h = q_ref[:, slot(hh)]
        q.append([jnp.where((lane >= c * DA_DQK) & (lane < (c + 1) * DA_DQK), q_h,
                            jnp.zeros_like(q_h)) for c in range(2)])
    bufs = (sa_ref, sb_ref)
    m_ref[...] = jnp.full(m_ref.shape, NEG, F32)
    acc_ref[...] = jnp.zeros_like(acc_ref)

    def scores(j, buf):
        rows = pl.ds(_block_start(j, blk), blk)
        for hh in heads:
            k_h = k_ref[rows, slot(hh)]
            for c in range(2):
                bufs[buf][2 * hh + c] = lax.dot_general(k_h, q[hh][c], _NT,
                                                        preferred_element_type=F32)

    def consume(j, buf, masked):
        mask = _causal_valid(i, j, blk) if masked else None
        for hh in heads:
            ps, alphas = [], []
            for c in range(2):
                hc = 2 * hh + c
                s = bufs[buf][hc]
                if masked:
                    s = jnp.where(mask, s, NEG)
                p, alpha, m_new = _softmax_block(s, m_ref[hc])
                m_ref[hc] = m_new
                ps.append(p)
                alphas.append(alpha)
            pv = jnp.dot(vt_ref[0, hh, j], jnp.concatenate(ps, axis=1),
                         preferred_element_type=F32)
            for c in range(2):
                hc = 2 * hh + c
                acc_ref[hc] = alphas[c] * acc_ref[hc] + pv[:, c * blk:(c + 1) * blk]

    def first():
        lo = blk - FIRST_KEYS
        mask = _causal_valid(i, 0, blk, lo)
        for hh in heads:
            k_h = k_ref[lo:blk, slot(hh)]
            ps, alphas = [], []
            for c in range(2):
                hc = 2 * hh + c
                s = lax.dot_general(k_h, q[hh][c], _NT, preferred_element_type=F32)
                p, alpha, m_new = _softmax_block(jnp.where(mask, s, NEG), m_ref[hc])
                m_ref[hc] = m_new
                ps.append(p)
                alphas.append(alpha)
            pv = jnp.dot(vt_ref[0, hh, 0, :, lo:blk], jnp.concatenate(ps, axis=1),
                         preferred_element_type=F32)
            for c in range(2):
                hc = 2 * hh + c
                acc_ref[hc] = alphas[c] * acc_ref[hc] + pv[:, c * blk:(c + 1) * blk]

    _pipelined_key_blocks(i, first, scores, consume)
    lv = lam_ref[...]
    lam = (jnp.exp(jnp.sum(lv[0:1] * lv[1:2], axis=-1, keepdims=True))
           - jnp.exp(jnp.sum(lv[2:3] * lv[3:4], axis=-1, keepdims=True)) + lam_init)
    for hh in heads:
        o = _head_out(acc_ref[2 * hh].T) - lam * _head_out(acc_ref[2 * hh + 1].T)
        o_ref[:, slot(hh)] = (_head_rms(o, g_ref[...]) * (1.0 - lam_init)).astype(BF16)


def _diff_attn(dqk, dvt, lam_vecs, gain, lam_init, batch, seq_len):
    n = dqk.shape[0]
    blk = ATTN_BLOCK
    nq = seq_len // blk
    hps = ATTN_HEADS_PER_STEP
    groups = DA_HEADS // hps
    return pl.pallas_call(
        functools.partial(_diff_attn_kernel, blk=blk, lam_init=lam_init),
        grid=(batch, groups, nq),
        in_specs=[pl.BlockSpec((blk, hps * SLOT), lambda b, h, i: (b * nq + i, h)),
                  _key_spec(seq_len, groups),
                  _value_spec(seq_len),
                  pl.BlockSpec((8, LANES), lambda b, h, i: (0, 0)),
                  pl.BlockSpec((1, SLOT), lambda b, h, i: (0, 0))],
        out_specs=pl.BlockSpec((blk, hps * SLOT), lambda b, h, i: (b * nq + i, h)),
        out_shape=jax.ShapeDtypeStruct((n, HEAD_W), BF16),
        scratch_shapes=[pltpu.VMEM((2 * hps, 1, blk), F32),
                        pltpu.VMEM((2 * hps, SLOT, blk), F32),
                        pltpu.VMEM((2 * hps, blk, blk), F32),
                        pltpu.VMEM((2 * hps, blk, blk), F32)],
        compiler_params=pltpu.CompilerParams(
            dimension_semantics=("arbitrary", "arbitrary", "arbitrary"),
            vmem_limit_bytes=VMEM_LIMIT),
        name="diff_attn",
    )(dqk, dqk, dvt, lam_vecs, gain)


def _hgrn_kernel(q_ref, i_ref, g_ref, f_ref, lb_ref, gain_ref, o_ref,
                 st_ref, hs_ref, *, tl):
    j = pl.program_id(2)

    @pl.when(j == 0)
    def _():
        st_ref[...] = jnp.zeros_like(st_ref)

    c_rows, sub = HG_CHUNK, HG_SUB
    tri = (lax.broadcasted_iota(jnp.int32, (c_rows, c_rows), 1)
           <= lax.broadcasted_iota(jnp.int32, (c_rows, c_rows), 0)).astype(F32)
    row_c = lax.broadcasted_iota(jnp.int32, (c_rows, 1), 0)
    t_sub = lax.broadcasted_iota(jnp.int32, (sub, 1), 0)
    lane_c = lax.broadcasted_iota(jnp.int32, (sub, c_rows), 1)

    t_all = lax.broadcasted_iota(jnp.int32, (c_rows, c_rows), 0)
    s_all = lax.broadcasted_iota(jnp.int32, (c_rows, c_rows), 1)
    level_masks = []
    g = sub
    while g < c_rows:
        level_masks.append(((t_all // g) == (s_all // g) + 1) & ((s_all // g) % 2 == 0))
        g *= 2

    heads = range(HG_PER_STEP)
    cols = [slice(hh * HG_DK, (hh + 1) * HG_DK) for hh in heads]

    def chunk(c, carry):
        r0 = pl.multiple_of(c * c_rows, c_rows)
        rows = pl.ds(r0, c_rows)
        valid = (j * tl + r0 + row_c) >= LPAD
        G, kk, qs, v, st, o_inter = [], [], [], [], [], []
        for hh in heads:
            lb = lb_ref[:, cols[hh]]
            sig = _sigmoid(f_ref[rows, cols[hh]])
            log_f = jnp.where(valid, jnp.log2(jnp.maximum(lb + (1.0 - lb) * sig, TINY)), 0.0)
            kk.append(jnp.where(valid, (1.0 - lb) * (1.0 - sig), 0.0))
            G.append(jnp.dot(tri, log_f, precision=lax.Precision.HIGHEST,
                             preferred_element_type=F32))
            qf = q_ref[rows, cols[hh]].astype(F32)
            qs.append(qf * _sigmoid(qf) * (HG_DK ** -0.5))
            v.append(i_ref[rows, cols[hh]])
            hs_ref[hh] = G[hh] - jnp.log2(kk[hh])
            st.append(st_ref[hh])
            o_inter.append(lax.dot_general(
                (qs[hh] * jnp.exp2(G[hh])).astype(BF16), st[hh].astype(BF16), _NT,
                preferred_element_type=F32))
        a_mat = [jnp.zeros((c_rows, c_rows), F32) for _ in heads]
        g = sub
        for mask in level_masks:
            for hh in heads:
                ref = jnp.concatenate(
                    [jnp.broadcast_to(G[hh][p + g - 1:p + g], (2 * g, HG_DK))
                     for p in range(0, c_rows, 2 * g)], axis=0)
                e = jnp.exp2(-jnp.abs(G[hh] - ref))
                part = lax.dot_general((qs[hh] * e).astype(BF16), (kk[hh] * e).astype(BF16),
                                       _NT, preferred_element_type=F32)
                a_mat[hh] = jnp.where(mask, part, a_mat[hh])
            g *= 2
        a_rows = [[] for _ in heads]
        for b in range(c_rows // sub):
            lo = b * sub
            for hh in heads:
                q_b = qs[hh][lo:lo + sub]
                g_b = G[hh][lo:lo + sub]
                a_blk = a_mat[hh][lo:lo + sub]
                for s in range(lo, lo + sub):
                    y = q_b * jnp.exp2(g_b - hs_ref[hh, s:s + 1, :])
                    a_blk = jnp.where(lane_c == s, jnp.sum(y, axis=-1, keepdims=True), a_blk)
                a_rows[hh].append(jnp.where(lane_c <= t_sub + lo, a_blk, 0.0))
        for hh in heads:
            a_full = jnp.concatenate(a_rows[hh], axis=0).astype(BF16)
            o = o_inter[hh] + jnp.dot(a_full, v[hh], preferred_element_type=F32)
            g_last = G[hh][c_rows - 1:c_rows]
            kd = kk[hh] * jnp.exp2(g_last - G[hh])
            st_ref[hh] = st[hh] * jnp.exp2(g_last) + lax.dot_general(
                v[hh], kd.astype(BF16), _TN, preferred_element_type=F32)
            gate = g_ref[rows, cols[hh]].astype(F32)
            o = _rms(o, gain_ref[...]) * (gate * _sigmoid(gate))
            o_ref[rows, cols[hh]] = o.astype(BF16)
        return carry

    lax.fori_loop(0, tl // c_rows, chunk, 0)


def _hgrn(hqig, hf, lb_row, gain, batch, seq_len):
    n = hqig.shape[0]
    tl = ROW_TILE
    nt = seq_len // tl
    w = HG_PER_STEP * HG_DK
    groups = HG_HEADS // HG_PER_STEP
    col = lambda off: pl.BlockSpec((tl, w), lambda b, h, j: (b * nt + j, off + h))
    return pl.pallas_call(
        functools.partial(_hgrn_kernel, tl=tl),
        grid=(batch, groups, nt),
        in_specs=[col(0), col(groups), col(2 * groups), col(0),
                  pl.BlockSpec((1, w), lambda b, h, j: (0, h)),
                  pl.BlockSpec((1, HG_DK), lambda b, h, j: (0, 0))],
        out_specs=col(0),
        out_shape=jax.ShapeDtypeStruct((n, HG_WIDTH), BF16),
        scratch_shapes=[pltpu.VMEM((HG_PER_STEP, HG_DK, HG_DK), F32),
                        pltpu.VMEM((HG_PER_STEP, HG_CHUNK, HG_DK), F32)],
        compiler_params=pltpu.CompilerParams(
            dimension_semantics=("arbitrary", "arbitrary", "arbitrary")),
        name="hgrn2",
    )(hqig, hqig, hqig, hf, lb_row, gain)


def _out_router_kernel(oa_ref, ob_ref, oc_ref, h_ref, wo_ref, g_ref, wrh_ref, wrl_ref, br_ref,
                       h1_ref, u_ref, route_ref, cnt_ref, carry_ref):
    @pl.when(pl.program_id(0) == 0)
    def _():
        carry_ref[...] = jnp.zeros_like(carry_ref)

    h1 = (h_ref[...]
          + jnp.dot(oa_ref[...], wo_ref[0:512, :], preferred_element_type=F32)
          + jnp.dot(ob_ref[...], wo_ref[512:1024, :], preferred_element_type=F32)
          + jnp.dot(oc_ref[...], wo_ref[1024:1536, :], preferred_element_type=F32))
    h1_ref[...] = h1
    u = _rms(h1, g_ref[...])
    _store_token_tiles(u_ref, u)
    u_hi = u.astype(BF16)
    u_lo = (u - u_hi.astype(F32)).astype(BF16)
    logits = (jnp.dot(u_hi, wrh_ref[...], preferred_element_type=F32)
              + (jnp.dot(u_hi, wrl_ref[...], preferred_element_type=F32)
                 + jnp.dot(u_lo, wrh_ref[...], preferred_element_type=F32))
              + br_ref[...])
    tm = logits.shape[0]
    lane = lax.broadcasted_iota(jnp.int32, (tm, LANES), 1).astype(F32)
    big = float(LANES)
    is_g = (lane >= N_EXPERTS) & (lane < N_EXPERTS + N_GROUPS)
    gl = jnp.where(is_g, logits, -jnp.inf)
    gmax = jnp.max(gl, axis=-1, keepdims=True)
    gsel = jnp.min(jnp.where(gl == gmax, lane, big), axis=-1, keepdims=True) - N_EXPERTS
    p_g = 1.0 / jnp.sum(jnp.exp(gl - gmax), axis=-1, keepdims=True)
    lo = gsel * EXPERTS_PER_GROUP
    el = jnp.where((lane >= lo) & (lane < lo + EXPERTS_PER_GROUP), logits, -jnp.inf)
    m1 = jnp.max(el, axis=-1, keepdims=True)
    i1 = jnp.min(jnp.where(el == m1, lane, big), axis=-1, keepdims=True)
    el2 = jnp.where(lane == i1, -jnp.inf, el)
    m2 = jnp.max(el2, axis=-1, keepdims=True)
    i2 = jnp.min(jnp.where(el2 == m2, lane, big), axis=-1, keepdims=True)
    r = jnp.exp(m2 - m1)
    gate1 = p_g / (1.0 + r)
    gate2 = gate1 * r
    oh1 = lane == i1
    oh2 = lane == i2
    onehot = jnp.where(oh1 | oh2, 1.0, 0.0)
    tri = (lax.broadcasted_iota(jnp.int32, (tm, tm), 1)
           < lax.broadcasted_iota(jnp.int32, (tm, tm), 0)).astype(BF16)
    before = jnp.dot(tri, onehot.astype(BF16), preferred_element_type=F32) + carry_ref[...]
    rank1 = jnp.sum(jnp.where(oh1, before, 0.0), axis=-1, keepdims=True)
    rank2 = jnp.sum(jnp.where(oh2, before, 0.0), axis=-1, keepdims=True)
    total = carry_ref[...] + jnp.sum(onehot, axis=0, keepdims=True)
    carry_ref[...] = total
    cnt_ref[...] = total
    route = jnp.where(lane == 0, i1, 0.0)
    for idx, val in ((1, i2), (2, gate1), (3, gate2), (4, rank1), (5, rank2)):
        route = jnp.where(lane == idx, val, route)
    route_ref[...] = route


def _out_router(oa, ob, oc, h, w_out, g, w_rt, b_rt):
    n = h.shape[0]
    w_rt_hi = w_rt.astype(BF16)
    w_rt_lo = (w_rt - w_rt_hi.astype(F32)).astype(BF16)
    tm = ROUTER_TILE
    row = lambda w: pl.BlockSpec((tm, w), lambda i: (i, 0))
    const = lambda r, c: pl.BlockSpec((r, c), lambda i: (0, 0))
    return pl.pallas_call(
        _out_router_kernel,
        grid=(n // tm,),
        in_specs=[row(HG_WIDTH), row(HEAD_W), row(HEAD_W), row(D_MODEL),
                  const(HG_WIDTH + 2 * HEAD_W, D_MODEL),
                  const(1, D_MODEL), const(D_MODEL, LANES), const(D_MODEL, LANES),
                  const(1, LANES)],
        out_specs=[row(D_MODEL), pl.BlockSpec((tm,) + TOKEN_TILE, lambda i: (i, 0, 0)),
                   row(LANES), const(1, LANES)],
        out_shape=[jax.ShapeDtypeStruct((n, D_MODEL), F32),
                   jax.ShapeDtypeStruct((n,) + TOKEN_TILE, F32),
                   jax.ShapeDtypeStruct((n, LANES), F32),
                   jax.ShapeDtypeStruct((1, LANES), F32)],
        scratch_shapes=[pltpu.VMEM((1, LANES), F32)],
        compiler_params=pltpu.CompilerParams(dimension_semantics=("arbitrary",)),
        name="out_router",
    )(oa, ob, oc, h, w_out, g, w_rt_hi, w_rt_lo, b_rt)


def _dispatch_kernel(dest_ref, unused_ref, u_ref, xs_in_ref, xs_ref, back_ref, sem,
                     *, tm, n, tb):
    del xs_in_ref
    i = pl.program_id(0)
    base = i * tm

    @pl.when(i == 0)
    def _():
        def fill(s, carry):
            parity = lax.shift_right_logical(s, tb.bit_length() - 1) & 1
            back_ref[s] = 2 * n + parity * tb + (s & (tb - 1))
            return carry

        n_ranges = unused_ref.shape[0] // 2

        def one_range(e, carry):
            return lax.fori_loop(unused_ref[e], unused_ref[n_ranges + e], fill, carry)

        lax.fori_loop(0, n_ranges, one_range, 0)

    def issue(r, carry):
        for k in range(2):
            src = k * n + base + r
            d = dest_ref[src]
            back_ref[d] = src
            pltpu.make_async_copy(u_ref.at[pl.ds(r, 1)], xs_ref.at[pl.ds(d, 1)], sem).start()
        return carry

    lax.fori_loop(0, tm, issue, 0, unroll=8)
    for _ in range(2):
        pltpu.make_async_copy(u_ref, xs_ref.at[pl.ds(0, tm)], sem).wait()


def _dispatch(dest, unused, u, xs_zero):
    n = u.shape[0]
    tm = ROW_TILE
    p_rows = xs_zero.shape[0]
    return pl.pallas_call(
        functools.partial(_dispatch_kernel, tm=tm, n=n, tb=MOE_BLOCK),
        grid_spec=pltpu.PrefetchScalarGridSpec(
            num_scalar_prefetch=2,
            grid=(n // tm,),
            in_specs=[pl.BlockSpec((tm,) + TOKEN_TILE, lambda i, d, un: (i, 0, 0)),
                      pl.BlockSpec(memory_space=pl.ANY)],
            out_specs=[pl.BlockSpec(memory_space=pl.ANY),
                       pl.BlockSpec(memory_space=pltpu.SMEM)],
            scratch_shapes=[pltpu.SemaphoreType.DMA(())]),
        out_shape=[jax.ShapeDtypeStruct(xs_zero.shape, xs_zero.dtype),
                   jax.ShapeDtypeStruct((p_rows,), jnp.int32)],
        input_output_aliases={3: 0},
        compiler_params=pltpu.CompilerParams(dimension_semantics=("arbitrary",)),
        name="moe_dispatch",
    )(dest, unused, u, xs_zero)


def _expert_kernel(be_ref, na_ref, back_ref, x_ref, w1_ref, w3_ref, w2_ref, out_ref,
                   y_ref, w1b_ref, w3b_ref, w2b_ref, sem, *, tb):
    j = pl.program_id(0)
    na = na_ref[0]
    cur = j % 2

    def scatter(block, buf):
        for r in range(tb):
            pltpu.make_async_copy(y_ref.at[buf, pl.ds(r, 1)],
                                  out_ref.at[pl.ds(back_ref[block * tb + r], 1)],
                                  sem.at[buf]).start()

    def compute(buf):
        x = _load_token_tiles(x_ref).astype(BF16)
        a = jnp.dot(x, w1b_ref[...], preferred_element_type=F32)
        b = jnp.dot(x, w3b_ref[...], preferred_element_type=F32)
        act = (a * _sigmoid(a) * b).astype(BF16)
        _store_token_tiles(y_ref.at[buf],
                           jnp.dot(act, w2b_ref[...], preferred_element_type=F32))

    new_expert = (j == 0) | (be_ref[j] != be_ref[jnp.maximum(j - 1, 0)])

    @pl.when((j < na) & new_expert)
    def _():
        w1b_ref[...] = w1_ref[0].astype(BF16)
        w3b_ref[...] = w3_ref[0].astype(BF16)
        w2b_ref[...] = w2_ref[0].astype(BF16)

    @pl.when((j >= 2) & (j < na + 2))
    def _():
        pltpu.make_async_copy(y_ref.at[cur], out_ref.at[pl.ds(0, tb)], sem.at[cur]).wait()

    @pl.when(j == 0)
    def _():
        y_ref[...] = jnp.zeros_like(y_ref)
        first_spare = out_ref.shape[0] - 2 * tb
        spare = [pltpu.make_async_copy(
            y_ref.at[b], out_ref.at[pl.ds(first_spare + b * tb, tb)], sem.at[b])
            for b in range(2)]
        for copy in spare:
            copy.start()
        for copy in spare:
            copy.wait()
        compute(cur)

    @pl.when((j >= 1) & (j < na))
    def _():
        scatter(j - 1, 1 - cur)
        compute(cur)

    @pl.when(j == na)
    def _():
        scatter(j - 1, 1 - cur)


def _experts(blk_expert, n_active, slot_back, xs, w1, w3, w2, layer, n):
    p_rows = xs.shape[0]
    tb = MOE_BLOCK
    last = lambda j, na: jnp.maximum(jnp.minimum(j, na[0] - 1), 0)
    w_spec = lambda r, c: pl.BlockSpec((None, 1, r, c),
                                       lambda j, be, na, back: (layer, be[j], 0, 0))
    return pl.pallas_call(
        functools.partial(_expert_kernel, tb=tb),
        grid_spec=pltpu.PrefetchScalarGridSpec(
            num_scalar_prefetch=3,
            grid=(p_rows // tb + 1,),
            in_specs=[pl.BlockSpec((tb,) + TOKEN_TILE,
                                   lambda j, be, na, back: (last(j, na), 0, 0)),
                      w_spec(D_MODEL, D_EXPERT), w_spec(D_MODEL, D_EXPERT),
                      w_spec(D_EXPERT, D_MODEL)],
            out_specs=pl.BlockSpec(memory_space=pl.ANY),
            scratch_shapes=[pltpu.VMEM((2, tb) + TOKEN_TILE, F32),
                            pltpu.VMEM((D_MODEL, D_EXPERT), BF16),
                            pltpu.VMEM((D_MODEL, D_EXPERT), BF16),
                            pltpu.VMEM((D_EXPERT, D_MODEL), BF16),
                            pltpu.SemaphoreType.DMA((2,))]),
        out_shape=jax.ShapeDtypeStruct((2 * n + 2 * tb,) + TOKEN_TILE, F32),
        compiler_params=pltpu.CompilerParams(dimension_semantics=("arbitrary",),
                                             vmem_limit_bytes=VMEM_LIMIT),
        name="moe_experts",
    )(blk_expert, n_active, slot_back, xs, w1, w3, w2)


def _combine_kernel(h_ref, route_ref, gfin_ref, y0_ref, y1_ref, o_ref, *, final):
    route = route_ref[...]
    out = (h_ref[...] + route[:, 2:3] * _load_token_tiles(y0_ref)
           + route[:, 3:4] * _load_token_tiles(y1_ref))
    if final:
        out = _rms(out, gfin_ref[...])
    o_ref[...] = out


def _combine(h1, route, g_final, ys, batch, seq_len, final):
    n = h1.shape[0]
    tm = ROW_TILE
    tiles_per_seq = seq_len // tm
    skip_tiles = (LPAD + N_META) // tm if final else 0
    out_tiles = tiles_per_seq - skip_tiles
    tile = lambda b, i: b * tiles_per_seq + skip_tiles + i
    in_row = lambda w: pl.BlockSpec((tm, w), lambda b, i: (tile(b, i), 0))
    y_rows = lambda k: pl.BlockSpec((tm,) + TOKEN_TILE,
                                    lambda b, i: (k * (n // tm) + tile(b, i), 0, 0))
    return pl.pallas_call(
        functools.partial(_combine_kernel, final=final),
        grid=(batch, out_tiles),
        in_specs=[in_row(D_MODEL), in_row(LANES),
                  pl.BlockSpec((1, D_MODEL), lambda b, i: (0, 0)),
                  y_rows(0), y_rows(1)],
        out_specs=pl.BlockSpec((tm, D_MODEL), lambda b, i: (b * out_tiles + i, 0)),
        out_shape=jax.ShapeDtypeStruct((batch * out_tiles * tm, D_MODEL), F32),
        compiler_params=pltpu.CompilerParams(dimension_semantics=("arbitrary", "arbitrary")),
        name="moe_combine",
    )(h1, route, g_final, ys, ys)


def _to_slots(m):
    lead = m.shape[:-1]
    x = m.reshape(lead + (4, 64))
    x = jnp.pad(x, [(0, 0)] * (len(lead) + 1) + [(0, SLOT - 64)])
    return x.reshape(lead + (HEAD_W,))


def _rope_partner(m):
    lead = m.shape[:-1]
    x = m.reshape(lead + (DA_WIDTH // DA_DQK, 2, DA_DQK // 2))
    return x[..., ::-1, :].reshape(lead + (DA_WIDTH,))


def _rope_tables(seq_len):
    half = DA_DQK // 2
    pos = (jnp.arange(seq_len) - LPAD).astype(F32)
    inv = ROPE_THETA ** (-jnp.arange(0, DA_DQK, 2, dtype=F32) / DA_DQK)
    ang = pos[:, None] * inv[None, :]
    groups = DA_WIDTH // DA_DQK
    cos = jnp.tile(jnp.cos(ang), (1, 2 * groups))
    sin = jnp.tile(jnp.concatenate([-jnp.sin(ang), jnp.sin(ang)], axis=1), (1, groups))
    q_scale = DA_DQK ** -0.5 * LOG2E
    lay = lambda t: jnp.concatenate([_to_slots(t * q_scale), _to_slots(t)], axis=1)
    return lay(cos), lay(sin)


def _in_weights(w):
    hq, hf, hi, hg = (w[:, i * 512:(i + 1) * 512] for i in range(4))
    dq, dk, dv = (w[:, 2048 + i * 256:2048 + (i + 1) * 256] for i in range(3))
    fq, fk, fv = (w[:, 2816 + i * 256:2816 + (i + 1) * 256] for i in range(3))
    ff = w[:, 3584:3588]
    cat = jnp.concatenate(
        [hq, hi, hg, hf,
         _to_slots(dq), _to_slots(dk),
         _to_slots(_rope_partner(dq)), _to_slots(_rope_partner(dk)),
         _to_slots(fq * (FX_DH ** -0.5)), _to_slots(fk),
         ff, jnp.zeros((D_MODEL, LANES - FX_HEADS), w.dtype)],
        axis=1)
    w_vt = jnp.concatenate([_to_slots(dv), _to_slots(fv)], axis=1).T
    return cat.astype(BF16), w_vt.astype(BF16)


def _pad_lanes(v, width=LANES):
    return jnp.zeros((1, width), F32).at[0, :v.shape[0]].set(v.astype(F32))


def kernel(x, meta_tokens, norm_mix, w_in, hgrn_lb, hgrn_norm, diff_lambda, diff_norm,
           fox_bias, fox_norm, w_out, norm_ffn, w_group, b_group, w_router, b_router,
           w1, w3, w2, norm_final):
    batch, seq, d = x.shape
    depth = w_in.shape[0]
    seq_len = LPAD + N_META + seq
    n = batch * seq_len
    pad = jnp.zeros((batch, LPAD, d), x.dtype)
    meta = jnp.broadcast_to(meta_tokens.astype(x.dtype)[None], (batch, N_META, d))
    h = jnp.concatenate([pad, meta, x], axis=1).reshape(n, d)

    s_lb = jax.nn.softmax(hgrn_lb.astype(F32), axis=0)
    lb_all = jnp.cumsum(s_lb, axis=0) - s_lb[0]
    cos_t, sin_t = _rope_tables(seq_len)

    tb = MOE_BLOCK
    n_blocks = (2 * n) // tb + N_EXPERTS
    p_rows = n_blocks * tb

    for layer in range(depth):
        lam_init = 0.8 - 0.6 * math.exp(-0.3 * layer)
        w_cat, w_vt = _in_weights(w_in[layer])
        hqig, hf, dqk, fqk, ff, dvt, fvt = _in_proj(
            h, norm_mix[layer][None, :], w_cat, w_vt, cos_t, sin_t, batch, seq_len)

        o_a = _hgrn(hqig, hf, lb_all[layer][None, :], hgrn_norm[layer][None, :],
                    batch, seq_len)

        lam_vecs = jnp.zeros((8, LANES), F32).at[:4, :DA_DQK].set(diff_lambda[layer].astype(F32))
        o_b = _diff_attn(dqk, dvt, lam_vecs, _pad_lanes(diff_norm[layer]), lam_init,
                         batch, seq_len)

        fqk_aug = _fox_prep(ff.reshape(batch, seq_len, LANES), _pad_lanes(fox_bias[layer]),
                            fqk, batch, seq_len)
        o_c = _fox_attn(fqk_aug, fvt, _pad_lanes(fox_norm[layer]), batch, seq_len)

        wo = w_out[layer]
        slot_rows = lambda m: _to_slots(m.T).T
        wo_cat = jnp.concatenate(
            [wo[:HG_WIDTH], slot_rows(wo[HG_WIDTH:HG_WIDTH + DA_WIDTH]),
             slot_rows(wo[HG_WIDTH + DA_WIDTH:])], axis=0).astype(BF16)
        w_rt = jnp.concatenate(
            [w_router[layer], w_group[layer],
             jnp.zeros((d, LANES - N_EXPERTS - N_GROUPS), F32)], axis=1)
        b_rt = _pad_lanes(jnp.concatenate([b_router[layer], b_group[layer]]))
        h1, u2, route, counts = _out_router(
            o_a, o_b, o_c, h, wo_cat, norm_ffn[layer][None, :], w_rt, b_rt)

        cnt = counts[0, :N_EXPERTS].astype(jnp.int32)
        padded = (cnt + tb - 1) // tb * tb
        p_end = jnp.cumsum(padded)
        p_start = p_end - padded
        ids = route[:, 0:2].astype(jnp.int32)
        ranks = route[:, 4:6].astype(jnp.int32)
        dest = (p_start[ids] + ranks).T.reshape(2 * n)
        blk_start = jnp.arange(n_blocks + 1, dtype=jnp.int32) * tb
        blk_expert = jnp.minimum(
            jnp.sum((p_end[None, :] <= blk_start[:, None]).astype(jnp.int32), axis=1),
            N_EXPERTS - 1)
        n_active = (p_end[-1:] // tb).astype(jnp.int32)

        unused = jnp.concatenate(
            [p_start + cnt, p_end[-1:], p_end, jnp.full((1,), p_rows)]).astype(jnp.int32)
        xs, slot_back = _dispatch(dest, unused, u2, jnp.zeros((p_rows,) + TOKEN_TILE, F32))
        ys = _experts(blk_expert, n_active, slot_back, xs, w1, w3, w2, layer, n)
        h = _combine(h1, route, norm_final[None, :], ys, batch, seq_len,
                     final=(layer == depth - 1))

    return h.reshape(batch, seq, d)
```

```python
import functools
import math

import jax
import jax.numpy as jnp
from jax import lax
from jax.experimental import pallas as pl
from jax.experimental.pallas import tpu as pltpu

F32 = jnp.float32
BF16 = jnp.bfloat16

D_MODEL = 1024
N_META = 16
HG_HEADS = 4
HG_DK = 128
HG_WIDTH = 512
HG_CHUNK = 64
HG_SUB = 8
HG_PER_STEP = 4
DA_HEADS = 4
DA_DV = 64
DA_DQK = 32
DA_WIDTH = 256
FX_HEADS = 4
FX_DH = 64
FX_WIDTH = 256
N_GROUPS = 4
EXPERTS_PER_GROUP = 8
N_EXPERTS = 32
D_EXPERT = 512
ROPE_THETA = 10000.0
EPS = 1e-6
NEG = -1e30
TINY = 1e-30
LOG2E = 1.4426950408889634

LANES = 128
TOKEN_TILE = (D_MODEL // LANES, LANES)
SLOT = LANES
HEAD_W = 4 * SLOT
ROW_TILE = 256
PROJ_TILE = 512
ROUTER_TILE = 512
ATTN_BLOCK = 512
ATTN_HEADS_PER_STEP = 4
MOE_BLOCK = 256
LPAD = ATTN_BLOCK - N_META
FIRST_KEYS = LANES
assert ATTN_BLOCK - FIRST_KEYS <= LPAD
ONE_LANE = 64
VMEM_LIMIT = 56 * 1024 * 1024

_C_HQIG = (0, 1536)
_C_HF = (1536, 2048)
_C_DQK = (2048, 3072)
_C_DQKP = (3072, 4096)
_C_FQ = (4096, 4608)
_C_FK = (4608, 5120)
_C_FF = (5120, 5248)
IN_COLS = 5248

_NT = (((1,), (1,)), ((), ()))
_TN = (((0,), (0,)), ((), ()))


def _store_token_tiles(ref, x):
    for c in range(TOKEN_TILE[0]):
        ref[:, c, :] = x[:, c * LANES:(c + 1) * LANES]


def _load_token_tiles(ref):
    return jnp.concatenate([ref[:, c, :] for c in range(TOKEN_TILE[0])], axis=1)


def _sigmoid(x):
    return 1.0 / (1.0 + jnp.exp(-x))


def _rms(x, g):
    return x * lax.rsqrt(jnp.mean(x * x, axis=-1, keepdims=True) + EPS) * g


def _in_proj_kernel(h_ref, g_ref, w_ref, wvt_ref, cos_ref, sin_ref,
                    hqig_ref, hf_ref, dqk_ref, fqk_ref, ff_ref, dvt_ref, fvt_ref):
    u = _rms(h_ref[...], g_ref[...]).astype(BF16)

    def mm(c):
        return jnp.dot(u, w_ref[:, c[0]:c[1]], preferred_element_type=F32)

    hqig_ref[...] = mm(_C_HQIG).astype(BF16)
    hf_ref[...] = mm(_C_HF)
    def per_slot(t_ref):
        return jnp.concatenate([t_ref[:, 0:SLOT]] * 4 + [t_ref[:, SLOT:2 * SLOT]] * 4, axis=1)

    dqk_ref[...] = (mm(_C_DQK) * per_slot(cos_ref)
                    + mm(_C_DQKP) * per_slot(sin_ref)).astype(BF16)
    fqk_ref[:, 0:HEAD_W] = (mm(_C_FQ) * LOG2E).astype(BF16)
    fqk_ref[:, HEAD_W:2 * HEAD_W] = mm(_C_FK).astype(BF16)
    ff_ref[...] = mm(_C_FF)
    vt = lax.dot_general(wvt_ref[...], u, _NT, preferred_element_type=F32)
    slot_row = lax.broadcasted_iota(jnp.int32, vt.shape, 0) % SLOT
    vt = jnp.where(slot_row == ONE_LANE, 1.0, vt).astype(BF16)
    for h in range(4):
        dvt_ref[0, h, 0] = vt[h * SLOT:(h + 1) * SLOT]
        fvt_ref[0, h, 0] = vt[HEAD_W + h * SLOT:HEAD_W + (h + 1) * SLOT]


def _in_proj(h, g, w_cat, w_vt, cos_t, sin_t, batch, seq_len):
    n = h.shape[0]
    tm = PROJ_TILE
    assert tm == ATTN_BLOCK
    nk = seq_len // tm
    row = lambda w: pl.BlockSpec((tm, w), lambda i: (i, 0))
    tab = pl.BlockSpec((tm, 2 * SLOT), lambda i: (i % nk, 0))
    once = lambda r, c: pl.BlockSpec((r, c), lambda i: (0, 0), pipeline_mode=pl.Buffered(1))
    widths = (1536, 512, 2 * HEAD_W, 2 * HEAD_W, LANES)
    dtypes = (BF16, F32, BF16, BF16, F32)
    vt_spec = pl.BlockSpec((1, 4, 1, SLOT, tm), lambda i: (i // nk, 0, i % nk, 0, 0))
    vt_shape = jax.ShapeDtypeStruct((batch, 4, nk, SLOT, tm), BF16)
    return pl.pallas_call(
        _in_proj_kernel,
        grid=(n // tm,),
        in_specs=[row(D_MODEL),
                  pl.BlockSpec((1, D_MODEL), lambda i: (0, 0)),
                  once(D_MODEL, IN_COLS), once(2 * HEAD_W, D_MODEL),
                  tab, tab],
        out_specs=[row(w) for w in widths] + [vt_spec, vt_spec],
        out_shape=[jax.ShapeDtypeStruct((n, w), t) for w, t in zip(widths, dtypes)]
        + [vt_shape, vt_shape],
        compiler_params=pltpu.CompilerParams(dimension_semantics=("arbitrary",),
                                             vmem_limit_bytes=VMEM_LIMIT),
        name="in_proj",
    )(h, g, w_cat, w_vt, cos_t, sin_t)


def _bf16_split3(x):
    hi = x.astype(BF16).astype(F32)
    r = x - hi
    mid = r.astype(BF16).astype(F32)
    return hi, mid, r - mid


def _fox_prep_kernel(ff_ref, b_ref, qk_ref, o_ref, carry_ref):
    @pl.when(pl.program_id(1) == 0)
    def _():
        carry_ref[...] = jnp.zeros_like(carry_ref)

    x = ff_ref[0] + b_ref[...]
    lf = jnp.minimum(x, 0.0) - jnp.log(1.0 + jnp.exp(-jnp.abs(x)))
    t = x.shape[0]
    tri = (lax.broadcasted_iota(jnp.int32, (t, t), 1)
           <= lax.broadcasted_iota(jnp.int32, (t, t), 0)).astype(F32)
    cs = jnp.dot(tri, lf, precision=lax.Precision.HIGHEST,
                 preferred_element_type=F32) + carry_ref[...]
    carry_ref[...] = cs[t - 1:t, :]
    parts = _bf16_split3(cs * LOG2E)
    lane = lax.broadcasted_iota(jnp.int32, (t, SLOT), 1)
    d0 = FX_DH
    for h in range(FX_HEADS):
        q = qk_ref[:, h * SLOT:(h + 1) * SLOT].astype(F32)
        k = qk_ref[:, HEAD_W + h * SLOT:HEAD_W + (h + 1) * SLOT].astype(F32)
        for i, part in enumerate(parts):
            col = part[:, h:h + 1]
            q = jnp.where(lane == d0 + i, col, q)
            k = jnp.where(lane == d0 + 3 + i, -col, k)
        q = jnp.where((lane >= d0 + 3) & (lane < d0 + 6), 1.0, q)
        k = jnp.where((lane >= d0) & (lane < d0 + 3), 1.0, k)
        o_ref[:, h * SLOT:(h + 1) * SLOT] = q.astype(BF16)
        o_ref[:, HEAD_W + h * SLOT:HEAD_W + (h + 1) * SLOT] = k.astype(BF16)


def _fox_prep(ff, bias_row, fqk, batch, seq_len):
    n = fqk.shape[0]
    t = ROW_TILE
    nt = seq_len // t
    return pl.pallas_call(
        _fox_prep_kernel,
        grid=(batch, nt),
        in_specs=[pl.BlockSpec((1, t, LANES), lambda b, j: (b, j, 0)),
                  pl.BlockSpec((1, LANES), lambda b, j: (0, 0)),
                  pl.BlockSpec((t, 2 * HEAD_W), lambda b, j: (b * nt + j, 0))],
        out_specs=pl.BlockSpec((t, 2 * HEAD_W), lambda b, j: (b * nt + j, 0)),
        out_shape=jax.ShapeDtypeStruct((n, 2 * HEAD_W), BF16),
        scratch_shapes=[pltpu.VMEM((1, LANES), F32)],
        compiler_params=pltpu.CompilerParams(dimension_semantics=("arbitrary", "arbitrary")),
        name="fox_prep",
    )(ff, bias_row, fqk)


def _softmax_block(s, m_prev):
    m_new = jnp.maximum(m_prev, jnp.max(s, axis=0, keepdims=True))
    alpha = jnp.exp2(m_prev - m_new)
    p = jnp.exp2(s - m_new).astype(BF16)
    return p, alpha, m_new


def _block_start(j, blk):
    return j * blk if isinstance(j, int) else pl.multiple_of(j * blk, blk)


def _causal_valid(i, j, blk, first_row=0):
    k_idx = j * blk + first_row + lax.broadcasted_iota(jnp.int32, (blk - first_row, blk), 0)
    q_idx = i * blk + lax.broadcasted_iota(jnp.int32, (blk - first_row, blk), 1)
    return (k_idx <= q_idx) & (k_idx >= LPAD)


def _pipelined_key_blocks(i, first, scores, consume):
    first()

    @pl.when(i >= 1)
    def _():
        scores(1, 0)

    def pair(t, carry):
        scores(2 * t + 2, 1)
        consume(2 * t + 1, 0, False)
        scores(2 * t + 3, 0)
        consume(2 * t + 2, 1, False)
        return carry

    lax.fori_loop(0, lax.shift_right_logical(jnp.maximum(i - 1, 0), 1), pair, 0)
    odd = (i & 1) == 1

    @pl.when(odd)
    def _():
        consume(i, 0, True)

    @pl.when((i >= 2) & jnp.logical_not(odd))
    def _():
        scores(i, 1)
        consume(i - 1, 0, False)
        consume(i, 1, True)


def _head_out(acc):
    lane = lax.broadcasted_iota(jnp.int32, acc.shape, 1)
    return jnp.where(lane < ONE_LANE, acc / acc[:, ONE_LANE:ONE_LANE + 1], 0.0)


def _head_rms(o, g):
    ms = jnp.sum(o * o, axis=-1, keepdims=True) * (1.0 / ONE_LANE)
    return o * lax.rsqrt(ms + EPS) * g


def _fox_attn_kernel(q_ref, k_ref, vt_ref, g_ref, o_ref, m_ref, acc_ref, sa_ref, sb_ref,
                     *, blk):
    i = pl.program_id(2)
    heads = range(ATTN_HEADS_PER_STEP)
    slot = [slice(hh * SLOT, (hh + 1) * SLOT) for hh in heads]
    q = [q_ref[:, slot[hh]] for hh in heads]
    bufs = (sa_ref, sb_ref)
    m_ref[...] = jnp.full(m_ref.shape, NEG, F32)
    acc_ref[...] = jnp.zeros_like(acc_ref)

    def scores(j, buf):
        rows = pl.ds(_block_start(j, blk), blk)
        for hh in heads:
            bufs[buf][hh] = lax.dot_general(k_ref[rows, slot[hh]], q[hh], _NT,
                                            preferred_element_type=F32)

    def consume(j, buf, masked):
        mask = _causal_valid(i, j, blk) if masked else None
        for hh in heads:
            s = bufs[buf][hh]
            if masked:
                s = jnp.where(mask, s, NEG)
            p, alpha, m_new = _softmax_block(s, m_ref[hh])
            acc_ref[hh] = alpha * acc_ref[hh] + jnp.dot(
                vt_ref[0, hh, j], p, preferred_element_type=F32)
            m_ref[hh] = m_new

    def first():
        lo = blk - FIRST_KEYS
        mask = _causal_valid(i, 0, blk, lo)
        for hh in heads:
            s = lax.dot_general(k_ref[lo:blk, slot[hh]], q[hh], _NT,
                                preferred_element_type=F32)
            p, alpha, m_new = _softmax_block(jnp.where(mask, s, NEG), m_ref[hh])
            acc_ref[hh] = alpha * acc_ref[hh] + jnp.dot(
                vt_ref[0, hh, 0, :, lo:blk], p, preferred_element_type=F32)
            m_ref[hh] = m_new

    _pipelined_key_blocks(i, first, scores, consume)
    for hh in heads:
        o_ref[:, slot[hh]] = _head_rms(_head_out(acc_ref[hh].T), g_ref[...]).astype(BF16)


def _key_spec(seq_len, groups):
    return pl.BlockSpec((seq_len, ATTN_HEADS_PER_STEP * SLOT),
                        lambda b, h, i: (b, groups + h), pipeline_mode=pl.Buffered(1))


def _value_spec(seq_len):
    nk = seq_len // ATTN_BLOCK
    return pl.BlockSpec((1, ATTN_HEADS_PER_STEP, nk, SLOT, ATTN_BLOCK),
                        lambda b, h, i: (b, h, 0, 0, 0), pipeline_mode=pl.Buffered(1))


def _fox_attn(fqk, fvt, gain, batch, seq_len):
    n = fqk.shape[0]
    blk = ATTN_BLOCK
    nq = seq_len // blk
    hps = ATTN_HEADS_PER_STEP
    groups = FX_HEADS // hps
    return pl.pallas_call(
        functools.partial(_fox_attn_kernel, blk=blk),
        grid=(batch, groups, nq),
        in_specs=[pl.BlockSpec((blk, hps * SLOT), lambda b, h, i: (b * nq + i, h)),
                  _key_spec(seq_len, groups),
                  _value_spec(seq_len),
                  pl.BlockSpec((1, SLOT), lambda b, h, i: (0, 0))],
        out_specs=pl.BlockSpec((blk, hps * SLOT), lambda b, h, i: (b * nq + i, h)),
        out_shape=jax.ShapeDtypeStruct((n, HEAD_W), BF16),
        scratch_shapes=[pltpu.VMEM((hps, 1, blk), F32), pltpu.VMEM((hps, SLOT, blk), F32),
                        pltpu.VMEM((hps, blk, blk), F32), pltpu.VMEM((hps, blk, blk), F32)],
        compiler_params=pltpu.CompilerParams(
            dimension_semantics=("arbitrary", "arbitrary", "arbitrary"),
            vmem_limit_bytes=VMEM_LIMIT),
        name="fox_attn",
    )(fqk, fqk, fvt, gain)


def _diff_attn_kernel(q_ref, k_ref, vt_ref, lam_ref, g_ref, o_ref, m_ref, acc_ref,
                      sa_ref, sb_ref, *, blk, lam_init):
    i = pl.program_id(2)
    heads = range(ATTN_HEADS_PER_STEP)
    slot = lambda s: slice(s * SLOT, (s + 1) * SLOT)
    lane = lax.broadcasted_iota(jnp.int32, (blk, SLOT), 1)
    q = []
    for hh in heads:
        q_h = q_ref[:, slot(hh)]
        q.append([jnp.where((lane >= c * DA_DQK) & (lane < (c + 1) * DA_DQK), q_h,
                            jnp.zeros_like(q_h)) for c in range(2)])
    bufs = (sa_ref, sb_ref)
    m_ref[...] = jnp.full(m_ref.shape, NEG, F32)
    acc_ref[...] = jnp.zeros_like(acc_ref)

    def scores(j, buf):
        rows = pl.ds(_block_start(j, blk), blk)
        for hh in heads:
            k_h = k_ref[rows, slot(hh)]
            for c in range(2):
                bufs[buf][2 * hh + c] = lax.dot_general(k_h, q[hh][c], _NT,
                                                        preferred_element_type=F32)

    def consume(j, buf, masked):
        mask = _causal_valid(i, j, blk) if masked else None
        for hh in heads:
            ps, alphas = [], []
            for c in range(2):
                hc = 2 * hh + c
                s = bufs[buf][hc]
                if masked:
                    s = jnp.where(mask, s, NEG)
                p, alpha, m_new = _softmax_block(s, m_ref[hc])
                m_ref[hc] = m_new
                ps.append(p)
                alphas.append(alpha)
            pv = jnp.dot(vt_ref[0, hh, j], jnp.concatenate(ps, axis=1),
                         preferred_element_type=F32)
            for c in range(2):
                hc = 2 * hh + c
                acc_ref[hc] = alphas[c] * acc_ref[hc] + pv[:, c * blk:(c + 1) * blk]

    def first():
        lo = blk - FIRST_KEYS
        mask = _causal_valid(i, 0, blk, lo)
        for hh in heads:
            k_h = k_ref[lo:blk, slot(hh)]
            ps, alphas = [], []
            for c in range(2):
                hc = 2 * hh + c
                s = lax.dot_general(k_h, q[hh][c], _NT, preferred_element_type=F32)
                p, alpha, m_new = _softmax_block(jnp.where(mask, s, NEG), m_ref[hc])
                m_ref[hc] = m_new
                ps.append(p)
                alphas.append(alpha)
            pv = jnp.dot(vt_ref[0, hh, 0, :, lo:blk], jnp.concatenate(ps, axis=1),
                         preferred_element_type=F32)
            for c in range(2):
                hc = 2 * hh + c
                acc_ref[hc] = alphas[c] * acc_ref[hc] + pv[:, c * blk:(c + 1) * blk]

    _pipelined_key_blocks(i, first, scores, consume)
    lv = lam_ref[...]
    lam = (jnp.exp(jnp.sum(lv[0:1] * lv[1:2], axis=-1, keepdims=True))
           - jnp.exp(jnp.sum(lv[2:3] * lv[3:4], axis=-1, keepdims=True)) + lam_init)
    for hh in heads:
        o = _head_out(acc_ref[2 * hh].T) - lam * _head_out(acc_ref[2 * hh + 1].T)
        o_ref[:, slot(hh)] = (_head_rms(o, g_ref[...]) * (1.0 - lam_init)).astype(BF16)


def _diff_attn(dqk, dvt, lam_vecs, gain, lam_init, batch, seq_len):
    n = dqk.shape[0]
    blk = ATTN_BLOCK
    nq = seq_len // blk
    hps = ATTN_HEADS_PER_STEP
    groups = DA_HEADS // hps
    return pl.pallas_call(
        functools.partial(_diff_attn_kernel, blk=blk, lam_init=lam_init),
        grid=(batch, groups, nq),
        in_specs=[pl.BlockSpec((blk, hps * SLOT), lambda b, h, i: (b * nq + i, h)),
                  _key_spec(seq_len, groups),
                  _value_spec(seq_len),
                  pl.BlockSpec((8, LANES), lambda b, h, i: (0, 0)),
                  pl.BlockSpec((1, SLOT), lambda b, h, i: (0, 0))],
        out_specs=pl.BlockSpec((blk, hps * SLOT), lambda b, h, i: (b * nq + i, h)),
        out_shape=jax.ShapeDtypeStruct((n, HEAD_W), BF16),
        scratch_shapes=[pltpu.VMEM((2 * hps, 1, blk), F32),
                        pltpu.VMEM((2 * hps, SLOT, blk), F32),
                        pltpu.VMEM((2 * hps, blk, blk), F32),
                        pltpu.VMEM((2 * hps, blk, blk), F32)],
        compiler_params=pltpu.CompilerParams(
            dimension_semantics=("arbitrary", "arbitrary", "arbitrary"),
            vmem_limit_bytes=VMEM_LIMIT),
        name="diff_attn",
    )(dqk, dqk, dvt, lam_vecs, gain)


def _hgrn_kernel(q_ref, i_ref, g_ref, f_ref, lb_ref, gain_ref, o_ref,
                 st_ref, hs_ref, *, tl):
    j = pl.program_id(2)

    @pl.when(j == 0)
    def _():
        st_ref[...] = jnp.zeros_like(st_ref)

    c_rows, sub = HG_CHUNK, HG_SUB
    tri = (lax.broadcasted_iota(jnp.int32, (c_rows, c_rows), 1)
           <= lax.broadcasted_iota(jnp.int32, (c_rows, c_rows), 0)).astype(F32)
    row_c = lax.broadcasted_iota(jnp.int32, (c_rows, 1), 0)
    t_sub = lax.broadcasted_iota(jnp.int32, (sub, 1), 0)
    lane_c = lax.broadcasted_iota(jnp.int32, (sub, c_rows), 1)

    t_all = lax.broadcasted_iota(jnp.int32, (c_rows, c_rows), 0)
    s_all = lax.broadcasted_iota(jnp.int32, (c_rows, c_rows), 1)
    level_masks = []
    g = sub
    while g < c_rows:
        level_masks.append(((t_all // g) == (s_all // g) + 1) & ((s_all // g) % 2 == 0))
        g *= 2

    heads = range(HG_PER_STEP)
    cols = [slice(hh * HG_DK, (hh + 1) * HG_DK) for hh in heads]

    def chunk(c, carry):
        r0 = pl.multiple_of(c * c_rows, c_rows)
        rows = pl.ds(r0, c_rows)
        valid = (j * tl + r0 + row_c) >= LPAD
        G, kk, qs, v, st, o_inter = [], [], [], [], [], []
        for hh in heads:
            lb = lb_ref[:, cols[hh]]
            sig = _sigmoid(f_ref[rows, cols[hh]])
            log_f = jnp.where(valid, jnp.log2(jnp.maximum(lb + (1.0 - lb) * sig, TINY)), 0.0)
            kk.append(jnp.where(valid, (1.0 - lb) * (1.0 - sig), 0.0))
            G.append(jnp.dot(tri, log_f, precision=lax.Precision.HIGHEST,
                             preferred_element_type=F32))
            qf = q_ref[rows, cols[hh]].astype(F32)
            qs.append(qf * _sigmoid(qf) * (HG_DK ** -0.5))
            v.append(i_ref[rows, cols[hh]])
            hs_ref[hh] = G[hh] - jnp.log2(kk[hh])
            st.append(st_ref[hh])
            o_inter.append(lax.dot_general(
                (qs[hh] * jnp.exp2(G[hh])).astype(BF16), st[hh].astype(BF16), _NT,
                preferred_element_type=F32))
        a_mat = [jnp.zeros((c_rows, c_rows), F32) for _ in heads]
        g = sub
        for mask in level_masks:
            for hh in heads:
                ref = jnp.concatenate(
                    [jnp.broadcast_to(G[hh][p + g - 1:p + g], (2 * g, HG_DK))
                     for p in range(0, c_rows, 2 * g)], axis=0)
                e = jnp.exp2(-jnp.abs(G[hh] - ref))
                part = lax.dot_general((qs[hh] * e).astype(BF16), (kk[hh] * e).astype(BF16),
                                       _NT, preferred_element_type=F32)
                a_mat[hh] = jnp.where(mask, part, a_mat[hh])
            g *= 2
        a_rows = [[] for _ in heads]
        for b in range(c_rows // sub):
            lo = b * sub
            for hh in heads:
                q_b = qs[hh][lo:lo + sub]
                g_b = G[hh][lo:lo + sub]
                a_blk = a_mat[hh][lo:lo + sub]
                for s in range(lo, lo + sub):
                    y = q_b * jnp.exp2(g_b - hs_ref[hh, s:s + 1, :])
                    a_blk = jnp.where(lane_c == s, jnp.sum(y, axis=-1, keepdims=True), a_blk)
                a_rows[hh].append(jnp.where(lane_c <= t_sub + lo, a_blk, 0.0))
        for hh in heads:
            a_full = jnp.concatenate(a_rows[hh], axis=0).astype(BF16)
            o = o_inter[hh] + jnp.dot(a_full, v[hh], preferred_element_type=F32)
            g_last = G[hh][c_rows - 1:c_rows]
            kd = kk[hh] * jnp.exp2(g_last - G[hh])
            st_ref[hh] = st[hh] * jnp.exp2(g_last) + lax.dot_general(
                v[hh], kd.astype(BF16), _TN, preferred_element_type=F32)
            gate = g_ref[rows, cols[hh]].astype(F32)
            o = _rms(o, gain_ref[...]) * (gate * _sigmoid(gate))
            o_ref[rows, cols[hh]] = o.astype(BF16)
        return carry

    lax.fori_loop(0, tl // c_rows, chunk, 0)


def _hgrn(hqig, hf, lb_row, gain, batch, seq_len):
    n = hqig.shape[0]
    tl = ROW_TILE
    nt = seq_len // tl
    w = HG_PER_STEP * HG_DK
    groups = HG_HEADS // HG_PER_STEP
    col = lambda off: pl.BlockSpec((tl, w), lambda b, h, j: (b * nt + j, off + h))
    return pl.pallas_call(
        functools.partial(_hgrn_kernel, tl=tl),
        grid=(batch, groups, nt),
        in_specs=[col(0), col(groups), col(2 * groups), col(0),
                  pl.BlockSpec((1, w), lambda b, h, j: (0, h)),
                  pl.BlockSpec((1, HG_DK), lambda b, h, j: (0, 0))],
        out_specs=col(0),
        out_shape=jax.ShapeDtypeStruct((n, HG_WIDTH), BF16),
        scratch_shapes=[pltpu.VMEM((HG_PER_STEP, HG_DK, HG_DK), F32),
                        pltpu.VMEM((HG_PER_STEP, HG_CHUNK, HG_DK), F32)],
        compiler_params=pltpu.CompilerParams(
            dimension_semantics=("arbitrary", "arbitrary", "arbitrary")),
        name="hgrn2",
    )(hqig, hqig, hqig, hf, lb_row, gain)


def _out_router_kernel(oa_ref, ob_ref, oc_ref, h_ref, wo_ref, g_ref, wrh_ref, wrl_ref, br_ref,
                       h1_ref, route_ref, cnt_ref, carry_ref):
    @pl.when(pl.program_id(0) == 0)
    def _():
        carry_ref[...] = jnp.zeros_like(carry_ref)

    h1 = (h_ref[...]
          + jnp.dot(oa_ref[...], wo_ref[0:512, :], preferred_element_type=F32)
          + jnp.dot(ob_ref[...], wo_ref[512:1024, :], preferred_element_type=F32)
          + jnp.dot(oc_ref[...], wo_ref[1024:1536, :], preferred_element_type=F32))
    h1_ref[...] = h1
    u = _rms(h1, g_ref[...])
    u_hi = u.astype(BF16)
    u_lo = (u - u_hi.astype(F32)).astype(BF16)
    logits = (jnp.dot(u_hi, wrh_ref[...], preferred_element_type=F32)
              + (jnp.dot(u_hi, wrl_ref[...], preferred_element_type=F32)
                 + jnp.dot(u_lo, wrh_ref[...], preferred_element_type=F32))
              + br_ref[...])
    tm = logits.shape[0]
    lane = lax.broadcasted_iota(jnp.int32, (tm, LANES), 1).astype(F32)
    big = float(LANES)
    is_g = (lane >= N_EXPERTS) & (lane < N_EXPERTS + N_GROUPS)
    gl = jnp.where(is_g, logits, -jnp.inf)
    gmax = jnp.max(gl, axis=-1, keepdims=True)
    gsel = jnp.min(jnp.where(gl == gmax, lane, big), axis=-1, keepdims=True) - N_EXPERTS
    p_g = 1.0 / jnp.sum(jnp.exp(gl - gmax), axis=-1, keepdims=True)
    lo = gsel * EXPERTS_PER_GROUP
    el = jnp.where((lane >= lo) & (lane < lo + EXPERTS_PER_GROUP), logits, -jnp.inf)
    m1 = jnp.max(el, axis=-1, keepdims=True)
    i1 = jnp.min(jnp.where(el == m1, lane, big), axis=-1, keepdims=True)
    el2 = jnp.where(lane == i1, -jnp.inf, el)
    m2 = jnp.max(el2, axis=-1, keepdims=True)
    i2 = jnp.min(jnp.where(el2 == m2, lane, big), axis=-1, keepdims=True)
    r = jnp.exp(m2 - m1)
    gate1 = p_g / (1.0 + r)
    gate2 = gate1 * r
    oh1 = lane == i1
    oh2 = lane == i2
    onehot = jnp.where(oh1 | oh2, 1.0, 0.0)
    tri = (lax.broadcasted_iota(jnp.int32, (tm, tm), 1)
           < lax.broadcasted_iota(jnp.int32, (tm, tm), 0)).astype(BF16)
    before = jnp.dot(tri, onehot.astype(BF16), preferred_element_type=F32) + carry_ref[...]
    rank1 = jnp.sum(jnp.where(oh1, before, 0.0), axis=-1, keepdims=True)
    rank2 = jnp.sum(jnp.where(oh2, before, 0.0), axis=-1, keepdims=True)
    total = carry_ref[...] + jnp.sum(onehot, axis=0, keepdims=True)
    carry_ref[...] = total
    cnt_ref[...] = total
    route = jnp.where(lane == 0, i1, 0.0)
    for idx, val in ((1, i2), (2, gate1), (3, gate2), (4, rank1), (5, rank2)):
        route = jnp.where(lane == idx, val, route)
    route_ref[...] = route


def _out_router(oa, ob, oc, h, w_out, g, w_rt, b_rt):
    n = h.shape[0]
    w_rt_hi = w_rt.astype(BF16)
    w_rt_lo = (w_rt - w_rt_hi.astype(F32)).astype(BF16)
    tm = ROUTER_TILE
    row = lambda w: pl.BlockSpec((tm, w), lambda i: (i, 0))
    const = lambda r, c: pl.BlockSpec((r, c), lambda i: (0, 0))
    return pl.pallas_call(
        _out_router_kernel,
        grid=(n // tm,),
        in_specs=[row(HG_WIDTH), row(HEAD_W), row(HEAD_W), row(D_MODEL),
                  const(HG_WIDTH + 2 * HEAD_W, D_MODEL),
                  const(1, D_MODEL), const(D_MODEL, LANES), const(D_MODEL, LANES),
                  const(1, LANES)],
        out_specs=[row(D_MODEL), row(LANES), const(1, LANES)],
        out_shape=[jax.ShapeDtypeStruct((n, D_MODEL), F32),
                   jax.ShapeDtypeStruct((n, LANES), F32),
                   jax.ShapeDtypeStruct((1, LANES), F32)],
        scratch_shapes=[pltpu.VMEM((1, LANES), F32)],
        compiler_params=pltpu.CompilerParams(dimension_semantics=("arbitrary",)),
        name="out_router",
    )(oa, ob, oc, h, w_out, g, w_rt_hi, w_rt_lo, b_rt)


def _dispatch_kernel(dest_ref, unused_ref, h_ref, g_ref, xs_in_ref, xs_ref, back_ref,
                     u_ref, sem, *, tm, n, tb):
    del xs_in_ref
    i = pl.program_id(0)
    base = i * tm
    _store_token_tiles(u_ref, _rms(h_ref[...], g_ref[...]))

    @pl.when(i == 0)
    def _():
        def fill(s, carry):
            parity = lax.shift_right_logical(s, tb.bit_length() - 1) & 1
            back_ref[s] = 2 * n + parity * tb + (s & (tb - 1))
            return carry

        n_ranges = unused_ref.shape[0] // 2

        def one_range(e, carry):
            return lax.fori_loop(unused_ref[e], unused_ref[n_ranges + e], fill, carry)

        lax.fori_loop(0, n_ranges, one_range, 0)

    def issue(r, carry):
        for k in range(2):
            src = k * n + base + r
            d = dest_ref[src]
            back_ref[d] = src
            pltpu.make_async_copy(u_ref.at[pl.ds(r, 1)], xs_ref.at[pl.ds(d, 1)], sem).start()
        return carry

    lax.fori_loop(0, tm, issue, 0, unroll=8)
    for _ in range(2):
        pltpu.make_async_copy(u_ref, xs_ref.at[pl.ds(0, tm)], sem).wait()


def _dispatch(dest, unused, h1, g, xs_init):
    n = h1.shape[0]
    tm = ROW_TILE
    p_rows = xs_init.shape[0]
    return pl.pallas_call(
        functools.partial(_dispatch_kernel, tm=tm, n=n, tb=MOE_BLOCK),
        grid_spec=pltpu.PrefetchScalarGridSpec(
            num_scalar_prefetch=2,
            grid=(n // tm,),
            in_specs=[pl.BlockSpec((tm, D_MODEL), lambda i, d, un: (i, 0)),
                      pl.BlockSpec((1, D_MODEL), lambda i, d, un: (0, 0)),
                      pl.BlockSpec(memory_space=pl.ANY)],
            out_specs=[pl.BlockSpec(memory_space=pl.ANY),
                       pl.BlockSpec(memory_space=pltpu.SMEM)],
            scratch_shapes=[pltpu.VMEM((tm,) + TOKEN_TILE, F32),
                            pltpu.SemaphoreType.DMA(())]),
        out_shape=[jax.ShapeDtypeStruct(xs_init.shape, xs_init.dtype),
                   jax.ShapeDtypeStruct((p_rows,), jnp.int32)],
        input_output_aliases={4: 0},
        compiler_params=pltpu.CompilerParams(dimension_semantics=("arbitrary",)),
        name="moe_dispatch",
    )(dest, unused, h1, g, xs_init)


def _expert_kernel(be_ref, na_ref, back_ref, x_ref, w1_ref, w3_ref, w2_ref, out_ref,
                   y_ref, w1b_ref, w3b_ref, w2b_ref, sem, *, tb):
    j = pl.program_id(0)
    na = na_ref[0]
    cur = j % 2

    def scatter(block, buf):
        for r in range(tb):
            pltpu.make_async_copy(y_ref.at[buf, pl.ds(r, 1)],
                                  out_ref.at[pl.ds(back_ref[block * tb + r], 1)],
                                  sem.at[buf]).start()

    def compute(buf):
        x = _load_token_tiles(x_ref).astype(BF16)
        a = jnp.dot(x, w1b_ref[...], preferred_element_type=F32)
        b = jnp.dot(x, w3b_ref[...], preferred_element_type=F32)
        act = (a * _sigmoid(a) * b).astype(BF16)
        _store_token_tiles(y_ref.at[buf],
                           jnp.dot(act, w2b_ref[...], preferred_element_type=F32))

    new_expert = (j == 0) | (be_ref[j] != be_ref[jnp.maximum(j - 1, 0)])

    @pl.when((j < na) & new_expert)
    def _():
        w1b_ref[...] = w1_ref[0].astype(BF16)
        w3b_ref[...] = w3_ref[0].astype(BF16)
        w2b_ref[...] = w2_ref[0].astype(BF16)

    @pl.when((j >= 2) & (j < na + 2))
    def _():
        pltpu.make_async_copy(y_ref.at[cur], out_ref.at[pl.ds(0, tb)], sem.at[cur]).wait()

    @pl.when(j == 0)
    def _():
        y_ref[...] = jnp.zeros_like(y_ref)
        first_spare = out_ref.shape[0] - 2 * tb
        spare = [pltpu.make_async_copy(
            y_ref.at[b], out_ref.at[pl.ds(first_spare + b * tb, tb)], sem.at[b])
            for b in range(2)]
        for copy in spare:
            copy.start()
        for copy in spare:
            copy.wait()
        compute(cur)

    @pl.when((j >= 1) & (j < na))
    def _():
        scatter(j - 1, 1 - cur)
        compute(cur)

    @pl.when(j == na)
    def _():
        scatter(j - 1, 1 - cur)


def _experts(blk_expert, n_active, slot_back, xs, w1, w3, w2, layer, n):
    p_rows = xs.shape[0]
    tb = MOE_BLOCK
    last = lambda j, na: jnp.maximum(jnp.minimum(j, na[0] - 1), 0)
    w_spec = lambda r, c: pl.BlockSpec((None, 1, r, c),
                                       lambda j, be, na, back: (layer, be[j], 0, 0))
    return pl.pallas_call(
        functools.partial(_expert_kernel, tb=tb),
        grid_spec=pltpu.PrefetchScalarGridSpec(
            num_scalar_prefetch=3,
            grid=(p_rows // tb + 1,),
            in_specs=[pl.BlockSpec((tb,) + TOKEN_TILE,
                                   lambda j, be, na, back: (last(j, na), 0, 0)),
                      w_spec(D_MODEL, D_EXPERT), w_spec(D_MODEL, D_EXPERT),
                      w_spec(D_EXPERT, D_MODEL)],
            out_specs=pl.BlockSpec(memory_space=pl.ANY),
            scratch_shapes=[pltpu.VMEM((2, tb) + TOKEN_TILE, F32),
                            pltpu.VMEM((D_MODEL, D_EXPERT), BF16),
                            pltpu.VMEM((D_MODEL, D_EXPERT), BF16),
                            pltpu.VMEM((D_EXPERT, D_MODEL), BF16),
                            pltpu.SemaphoreType.DMA((2,))]),
        out_shape=jax.ShapeDtypeStruct((2 * n + 2 * tb,) + TOKEN_TILE, F32),
        compiler_params=pltpu.CompilerParams(dimension_semantics=("arbitrary",),
                                             vmem_limit_bytes=VMEM_LIMIT),
        name="moe_experts",
    )(blk_expert, n_active, slot_back, xs, w1, w3, w2)


def _combine_kernel(h_ref, route_ref, gfin_ref, y0_ref, y1_ref, o_ref, *, final):
    route = route_ref[...]
    out = (h_ref[...] + route[:, 2:3] * _load_token_tiles(y0_ref)
           + route[:, 3:4] * _load_token_tiles(y1_ref))
    if final:
        out = _rms(out, gfin_ref[...])
    o_ref[...] = out


def _combine(h1, route, g_final, ys, batch, seq_len, final):
    n = h1.shape[0]
    tm = ROW_TILE
    tiles_per_seq = seq_len // tm
    skip_tiles = (LPAD + N_META) // tm if final else 0
    out_tiles = tiles_per_seq - skip_tiles
    tile = lambda b, i: b * tiles_per_seq + skip_tiles + i
    in_row = lambda w: pl.BlockSpec((tm, w), lambda b, i: (tile(b, i), 0))
    y_rows = lambda k: pl.BlockSpec((tm,) + TOKEN_TILE,
                                    lambda b, i: (k * (n // tm) + tile(b, i), 0, 0))
    return pl.pallas_call(
        functools.partial(_combine_kernel, final=final),
        grid=(batch, out_tiles),
        in_specs=[in_row(D_MODEL), in_row(LANES),
                  pl.BlockSpec((1, D_MODEL), lambda b, i: (0, 0)),
                  y_rows(0), y_rows(1)],
        out_specs=pl.BlockSpec((tm, D_MODEL), lambda b, i: (b * out_tiles + i, 0)),
        out_shape=jax.ShapeDtypeStruct((batch * out_tiles * tm, D_MODEL), F32),
        compiler_params=pltpu.CompilerParams(dimension_semantics=("arbitrary", "arbitrary")),
        name="moe_combine",
    )(h1, route, g_final, ys, ys)


def _to_slots(m):
    lead = m.shape[:-1]
    x = m.reshape(lead + (4, 64))
    x = jnp.pad(x, [(0, 0)] * (len(lead) + 1) + [(0, SLOT - 64)])
    return x.reshape(lead + (HEAD_W,))


def _rope_partner(m):
    lead = m.shape[:-1]
    x = m.reshape(lead + (DA_WIDTH // DA_DQK, 2, DA_DQK // 2))
    return x[..., ::-1, :].reshape(lead + (DA_WIDTH,))


def _rope_tables(seq_len):
    pos = (jnp.arange(seq_len) - LPAD).astype(F32)
    inv = ROPE_THETA ** (-jnp.arange(0, DA_DQK, 2, dtype=F32) / DA_DQK)
    ang = pos[:, None] * inv[None, :]
    comps = DA_DV // DA_DQK
    cos = jnp.tile(jnp.cos(ang), (1, 2 * comps))
    sin = jnp.tile(jnp.concatenate([-jnp.sin(ang), jnp.sin(ang)], axis=1), (1, comps))
    q_scale = DA_DQK ** -0.5 * LOG2E
    pad = lambda t: jnp.pad(t, ((0, 0), (0, SLOT - DA_DV)))
    lay = lambda t: jnp.concatenate([pad(t * q_scale), pad(t)], axis=1)
    return lay(cos), lay(sin)


def _in_weights(w):
    hq, hf, hi, hg = (w[:, i * 512:(i + 1) * 512] for i in range(4))
    dq, dk, dv = (w[:, 2048 + i * 256:2048 + (i + 1) * 256] for i in range(3))
    fq, fk, fv = (w[:, 2816 + i * 256:2816 + (i + 1) * 256] for i in range(3))
    ff = w[:, 3584:3588]
    cat = jnp.concatenate(
        [hq, hi, hg, hf,
         _to_slots(dq), _to_slots(dk),
         _to_slots(_rope_partner(dq)), _to_slots(_rope_partner(dk)),
         _to_slots(fq * (FX_DH ** -0.5)), _to_slots(fk),
         ff, jnp.zeros((D_MODEL, LANES - FX_HEADS), w.dtype)],
        axis=1)
    w_vt = jnp.concatenate([_to_slots(dv), _to_slots(fv)], axis=1).T
    return cat.astype(BF16), w_vt.astype(BF16)


def _pad_lanes(v, width=LANES):
    return jnp.zeros((1, width), F32).at[0, :v.shape[0]].set(v.astype(F32))


def kernel(x, meta_tokens, norm_mix, w_in, hgrn_lb, hgrn_norm, diff_lambda, diff_norm,
           fox_bias, fox_norm, w_out, norm_ffn, w_group, b_group, w_router, b_router,
           w1, w3, w2, norm_final):
    batch, seq, d = x.shape
    depth = w_in.shape[0]
    seq_len = LPAD + N_META + seq
    n = batch * seq_len
    pad = jnp.zeros((batch, LPAD, d), x.dtype)
    meta = jnp.broadcast_to(meta_tokens.astype(x.dtype)[None], (batch, N_META, d))
    h = jnp.concatenate([pad, meta, x], axis=1).reshape(n, d)

    s_lb = jax.nn.softmax(hgrn_lb.astype(F32), axis=0)
    lb_all = jnp.cumsum(s_lb, axis=0) - s_lb[0]
    cos_t, sin_t = _rope_tables(seq_len)

    tb = MOE_BLOCK
    n_blocks = (2 * n) // tb + N_EXPERTS
    p_rows = n_blocks * tb

    xs = jnp.zeros((p_rows,) + TOKEN_TILE, F32)
    for layer in range(depth):
        lam_init = 0.8 - 0.6 * math.exp(-0.3 * layer)
        w_cat, w_vt = _in_weights(w_in[layer])
        hqig, hf, dqk, fqk, ff, dvt, fvt = _in_proj(
            h, norm_mix[layer][None, :], w_cat, w_vt, cos_t, sin_t, batch, seq_len)

        o_a = _hgrn(hqig, hf, lb_all[layer][None, :], hgrn_norm[layer][None, :],
                    batch, seq_len)

        lam_vecs = jnp.zeros((8, LANES), F32).at[:4, :DA_DQK].set(diff_lambda[layer].astype(F32))
        o_b = _diff_attn(dqk, dvt, lam_vecs, _pad_lanes(diff_norm[layer]), lam_init,
                         batch, seq_len)

        fqk_aug = _fox_prep(ff.reshape(batch, seq_len, LANES), _pad_lanes(fox_bias[layer]),
                            fqk, batch, seq_len)
        o_c = _fox_attn(fqk_aug, fvt, _pad_lanes(fox_norm[layer]), batch, seq_len)

        wo = w_out[layer]
        slot_rows = lambda m: _to_slots(m.T).T
        wo_cat = jnp.concatenate(
            [wo[:HG_WIDTH], slot_rows(wo[HG_WIDTH:HG_WIDTH + DA_WIDTH]),
             slot_rows(wo[HG_WIDTH + DA_WIDTH:])], axis=0).astype(BF16)
        w_rt = jnp.concatenate(
            [w_router[layer], w_group[layer],
             jnp.zeros((d, LANES - N_EXPERTS - N_GROUPS), F32)], axis=1)
        b_rt = _pad_lanes(jnp.concatenate([b_router[layer], b_group[layer]]))
        h1, route, counts = _out_router(
            o_a, o_b, o_c, h, wo_cat, norm_ffn[layer][None, :], w_rt, b_rt)

        cnt = counts[0, :N_EXPERTS].astype(jnp.int32)
        padded = (cnt + tb - 1) // tb * tb
        p_end = jnp.cumsum(padded)
        p_start = p_end - padded
        ids = route[:, 0:2].astype(jnp.int32)
        ranks = route[:, 4:6].astype(jnp.int32)
        dest = (p_start[ids] + ranks).T.reshape(2 * n)
        blk_start = jnp.arange(n_blocks + 1, dtype=jnp.int32) * tb
        blk_expert = jnp.minimum(
            jnp.sum((p_end[None, :] <= blk_start[:, None]).astype(jnp.int32), axis=1),
            N_EXPERTS - 1)
        n_active = (p_end[-1:] // tb).astype(jnp.int32)

        unused = jnp.concatenate(
            [p_start + cnt, p_end[-1:], p_end, jnp.full((1,), p_rows)]).astype(jnp.int32)
        xs, slot_back = _dispatch(dest, unused, h1, norm_ffn[layer][None, :], xs)
        ys = _experts(blk_expert, n_active, slot_back, xs, w1, w3, w2, layer, n)
        h = _combine(h1, route, norm_final[None, :], ys, batch, seq_len,
                     final=(layer == depth - 1))

    return h.reshape(batch, seq, d)
```

```python
import functools
import math

import jax
import jax.numpy as jnp
from jax import lax
from jax.experimental import pallas as pl
from jax.experimental.pallas import tpu as pltpu

F32 = jnp.float32
BF16 = jnp.bfloat16

D_MODEL = 1024
N_META = 16
HG_HEADS = 4
HG_DK = 128
HG_WIDTH = 512
HG_CHUNK = 64
HG_SUB = 8
HG_PER_STEP = 4
DA_HEADS = 4
DA_DV = 64
DA_DQK = 32
DA_WIDTH = 256
FX_HEADS = 4
FX_DH = 64
FX_WIDTH = 256
N_GROUPS = 4
EXPERTS_PER_GROUP = 8
N_EXPERTS = 32
D_EXPERT = 512
ROPE_THETA = 10000.0
EPS = 1e-6
NEG = -1e30
TINY = 1e-30
LOG2E = 1.4426950408889634

LANES = 128
TOKEN_TILE = (D_MODEL // LANES, LANES)
SLOT = LANES
HEAD_W = 4 * SLOT
ROW_TILE = 256
PROJ_TILE = 512
ROUTER_TILE = 512
ATTN_BLOCK = 512
ATTN_HEADS_PER_STEP = 4
MOE_BLOCK = 256
LPAD = ATTN_BLOCK - N_META
FIRST_KEYS = LANES
assert ATTN_BLOCK - FIRST_KEYS <= LPAD
ONE_LANE = 64
VMEM_LIMIT = 56 * 1024 * 1024

_C_HQIG = (0, 1536)
_C_HF = (1536, 2048)
_C_DQK = (2048, 3072)
_C_DQKP = (3072, 4096)
_C_FQ = (4096, 4608)
_C_FK = (4608, 5120)
_C_FF = (5120, 5248)
IN_COLS = 5248

_NT = (((1,), (1,)), ((), ()))
_TN = (((0,), (0,)), ((), ()))


def _store_token_tiles(ref, x):
    for c in range(TOKEN_TILE[0]):
        ref[:, c, :] = x[:, c * LANES:(c + 1) * LANES]


def _load_token_tiles(ref):
    return jnp.concatenate([ref[:, c, :] for c in range(TOKEN_TILE[0])], axis=1)


def _sigmoid(x):
    return 1.0 / (1.0 + jnp.exp(-x))


def _rms(x, g):
    return x * lax.rsqrt(jnp.mean(x * x, axis=-1, keepdims=True) + EPS) * g


def _in_proj_kernel(h_ref, g_ref, w_ref, wvt_ref, cos_ref, sin_ref,
                    hqig_ref, hf_ref, dqk_ref, fqk_ref, ff_ref, dvt_ref, fvt_ref):
    u = _rms(h_ref[...], g_ref[...]).astype(BF16)

    def mm(c):
        return jnp.dot(u, w_ref[:, c[0]:c[1]], preferred_element_type=F32)

    hqig_ref[...] = mm(_C_HQIG).astype(BF16)
    hf_ref[...] = mm(_C_HF)
    def per_slot(t_ref):
        return jnp.concatenate([t_ref[:, 0:SLOT]] * 4 + [t_ref[:, SLOT:2 * SLOT]] * 4, axis=1)

    dqk_ref[...] = (mm(_C_DQK) * per_slot(cos_ref)
                    + mm(_C_DQKP) * per_slot(sin_ref)).astype(BF16)
    fqk_ref[:, 0:HEAD_W] = (mm(_C_FQ) * LOG2E).astype(BF16)
    fqk_ref[:, HEAD_W:2 * HEAD_W] = mm(_C_FK).astype(BF16)
    ff_ref[...] = mm(_C_FF)
    vt = lax.dot_general(wvt_ref[...], u, _NT, preferred_element_type=F32)
    slot_row = lax.broadcasted_iota(jnp.int32, vt.shape, 0) % SLOT
    vt = jnp.where(slot_row == ONE_LANE, 1.0, vt).astype(BF16)
    for h in range(4):
        dvt_ref[0, h, 0] = vt[h * SLOT:(h + 1) * SLOT]
        fvt_ref[0, h, 0] = vt[HEAD_W + h * SLOT:HEAD_W + (h + 1) * SLOT]


def _in_proj(h, g, w_cat, w_vt, cos_t, sin_t, batch, seq_len):
    n = h.shape[0]
    tm = PROJ_TILE
    assert tm == ATTN_BLOCK
    nk = seq_len // tm
    row = lambda w: pl.BlockSpec((tm, w), lambda i: (i, 0))
    tab = pl.BlockSpec((tm, 2 * SLOT), lambda i: (i % nk, 0))
    once = lambda r, c: pl.BlockSpec((r, c), lambda i: (0, 0), pipeline_mode=pl.Buffered(1))
    widths = (1536, 512, 2 * HEAD_W, 2 * HEAD_W, LANES)
    dtypes = (BF16, F32, BF16, BF16, F32)
    vt_spec = pl.BlockSpec((1, 4, 1, SLOT, tm), lambda i: (i // nk, 0, i % nk, 0, 0))
    vt_shape = jax.ShapeDtypeStruct((batch, 4, nk, SLOT, tm), BF16)
    return pl.pallas_call(
        _in_proj_kernel,
        grid=(n // tm,),
        in_specs=[row(D_MODEL),
                  pl.BlockSpec((1, D_MODEL), lambda i: (0, 0)),
                  once(D_MODEL, IN_COLS), once(2 * HEAD_W, D_MODEL),
                  tab, tab],
        out_specs=[row(w) for w in widths] + [vt_spec, vt_spec],
        out_shape=[jax.ShapeDtypeStruct((n, w), t) for w, t in zip(widths, dtypes)]
        + [vt_shape, vt_shape],
        compiler_params=pltpu.CompilerParams(dimension_semantics=("arbitrary",),
                                             vmem_limit_bytes=VMEM_LIMIT),
        name="in_proj",
    )(h, g, w_cat, w_vt, cos_t, sin_t)


def _bf16_split3(x):
    hi = x.astype(BF16).astype(F32)
    r = x - hi
    mid = r.astype(BF16).astype(F32)
    return hi, mid, r - mid


def _fox_prep_kernel(ff_ref, b_ref, qk_ref, o_ref, carry_ref):
    @pl.when(pl.program_id(1) == 0)
    def _():
        carry_ref[...] = jnp.zeros_like(carry_ref)

    x = ff_ref[0] + b_ref[...]
    lf = jnp.minimum(x, 0.0) - jnp.log(1.0 + jnp.exp(-jnp.abs(x)))
    t = x.shape[0]
    tri = (lax.broadcasted_iota(jnp.int32, (t, t), 1)
           <= lax.broadcasted_iota(jnp.int32, (t, t), 0)).astype(F32)
    cs = jnp.dot(tri, lf, precision=lax.Precision.HIGHEST,
                 preferred_element_type=F32) + carry_ref[...]
    carry_ref[...] = cs[t - 1:t, :]
    parts = _bf16_split3(cs * LOG2E)
    lane = lax.broadcasted_iota(jnp.int32, (t, SLOT), 1)
    d0 = FX_DH
    for h in range(FX_HEADS):
        q = qk_ref[:, h * SLOT:(h + 1) * SLOT].astype(F32)
        k = qk_ref[:, HEAD_W + h * SLOT:HEAD_W + (h + 1) * SLOT].astype(F32)
        for i, part in enumerate(parts):
            col = part[:, h:h + 1]
            q = jnp.where(lane == d0 + i, col, q)
            k = jnp.where(lane == d0 + 3 + i, -col, k)
        q = jnp.where((lane >= d0 + 3) & (lane < d0 + 6), 1.0, q)
        k = jnp.where((lane >= d0) & (lane < d0 + 3), 1.0, k)
        o_ref[:, h * SLOT:(h + 1) * SLOT] = q.astype(BF16)
        o_ref[:, HEAD_W + h * SLOT:HEAD_W + (h + 1) * SLOT] = k.astype(BF16)


def _fox_prep(ff, bias_row, fqk, batch, seq_len):
    n = fqk.shape[0]
    t = ROW_TILE
    nt = seq_len // t
    return pl.pallas_call(
        _fox_prep_kernel,
        grid=(batch, nt),
        in_specs=[pl.BlockSpec((1, t, LANES), lambda b, j: (b, j, 0)),
                  pl.BlockSpec((1, LANES), lambda b, j: (0, 0)),
                  pl.BlockSpec((t, 2 * HEAD_W), lambda b, j: (b * nt + j, 0))],
        out_specs=pl.BlockSpec((t, 2 * HEAD_W), lambda b, j: (b * nt + j, 0)),
        out_shape=jax.ShapeDtypeStruct((n, 2 * HEAD_W), BF16),
        scratch_shapes=[pltpu.VMEM((1, LANES), F32)],
        compiler_params=pltpu.CompilerParams(dimension_semantics=("arbitrary", "arbitrary")),
        name="fox_prep",
    )(ff, bias_row, fqk)


def _softmax_block(s, m_prev):
    m_new = jnp.maximum(m_prev, jnp.max(s, axis=0, keepdims=True))
    alpha = jnp.exp2(m_prev - m_new)
    p = jnp.exp2(s - m_new).astype(BF16)
    return p, alpha, m_new


def _block_start(j, blk):
    return j * blk if isinstance(j, int) else pl.multiple_of(j * blk, blk)


def _causal_valid(i, j, blk, first_row=0):
    k_idx = j * blk + first_row + lax.broadcasted_iota(jnp.int32, (blk - first_row, blk), 0)
    q_idx = i * blk + lax.broadcasted_iota(jnp.int32, (blk - first_row, blk), 1)
    return (k_idx <= q_idx) & (k_idx >= LPAD)


def _pipelined_key_blocks(i, first, scores, consume):
    first()

    @pl.when(i >= 1)
    def _():
        scores(1, 0)

    def pair(t, carry):
        scores(2 * t + 2, 1)
        consume(2 * t + 1, 0, False)
        scores(2 * t + 3, 0)
        consume(2 * t + 2, 1, False)
        return carry

    lax.fori_loop(0, lax.shift_right_logical(jnp.maximum(i - 1, 0), 1), pair, 0)
    odd = (i & 1) == 1

    @pl.when(odd)
    def _():
        consume(i, 0, True)

    @pl.when((i >= 2) & jnp.logical_not(odd))
    def _():
        scores(i, 1)
        consume(i - 1, 0, False)
        consume(i, 1, True)


def _head_out(acc):
    lane = lax.broadcasted_iota(jnp.int32, acc.shape, 1)
    return jnp.where(lane < ONE_LANE, acc / acc[:, ONE_LANE:ONE_LANE + 1], 0.0)


def _head_rms(o, g):
    ms = jnp.sum(o * o, axis=-1, keepdims=True) * (1.0 / ONE_LANE)
    return o * lax.rsqrt(ms + EPS) * g


def _fox_attn_kernel(q_ref, k_ref, vt_ref, g_ref, o_ref, m_ref, acc_ref, sa_ref, sb_ref,
                     *, blk):
    i = pl.program_id(2)
    heads = range(ATTN_HEADS_PER_STEP)
    slot = [slice(hh * SLOT, (hh + 1) * SLOT) for hh in heads]
    q = [q_ref[:, slot[hh]] for hh in heads]
    bufs = (sa_ref, sb_ref)
    m_ref[...] = jnp.full(m_ref.shape, NEG, F32)
    acc_ref[...] = jnp.zeros_like(acc_ref)

    def scores(j, buf):
        rows = pl.ds(_block_start(j, blk), blk)
        for hh in heads:
            bufs[buf][hh] = lax.dot_general(k_ref[rows, slot[hh]], q[hh], _NT,
                                            preferred_element_type=F32)

    def consume(j, buf, masked):
        mask = _causal_valid(i, j, blk) if masked else None
        for hh in heads:
            s = bufs[buf][hh]
            if masked:
                s = jnp.where(mask, s, NEG)
            p, alpha, m_new = _softmax_block(s, m_ref[hh])
            acc_ref[hh] = alpha * acc_ref[hh] + jnp.dot(
                vt_ref[0, hh, j], p, preferred_element_type=F32)
            m_ref[hh] = m_new

    def first():
        lo = blk - FIRST_KEYS
        mask = _causal_valid(i, 0, blk, lo)
        for hh in heads:
            s = lax.dot_general(k_ref[lo:blk, slot[hh]], q[hh], _NT,
                                preferred_element_type=F32)
            p, alpha, m_new = _softmax_block(jnp.where(mask, s, NEG), m_ref[hh])
            acc_ref[hh] = alpha * acc_ref[hh] + jnp.dot(
                vt_ref[0, hh, 0, :, lo:blk], p, preferred_element_type=F32)
            m_ref[hh] = m_new

    _pipelined_key_blocks(i, first, scores, consume)
    for hh in heads:
        o_ref[:, slot[hh]] = _head_rms(_head_out(acc_ref[hh].T), g_ref[...]).astype(BF16)


def _key_spec(seq_len, groups):
    return pl.BlockSpec((seq_len, ATTN_HEADS_PER_STEP * SLOT),
                        lambda b, h, i: (b, groups + h), pipeline_mode=pl.Buffered(1))


def _value_spec(seq_len):
    nk = seq_len // ATTN_BLOCK
    return pl.BlockSpec((1, ATTN_HEADS_PER_STEP, nk, SLOT, ATTN_BLOCK),
                        lambda b, h, i: (b, h, 0, 0, 0), pipeline_mode=pl.Buffered(1))


def _fox_attn(fqk, fvt, gain, batch, seq_len):
    n = fqk.shape[0]
    blk = ATTN_BLOCK
    nq = seq_len // blk
    hps = ATTN_HEADS_PER_STEP
    groups = FX_HEADS // hps
    return pl.pallas_call(
        functools.partial(_fox_attn_kernel, blk=blk),
        grid=(batch, groups, nq),
        in_specs=[pl.BlockSpec((blk, hps * SLOT), lambda b, h, i: (b * nq + i, h)),
                  _key_spec(seq_len, groups),
                  _value_spec(seq_len),
                  pl.BlockSpec((1, SLOT), lambda b, h, i: (0, 0))],
        out_specs=pl.BlockSpec((blk, hps * SLOT), lambda b, h, i: (b * nq + i, h)),
        out_shape=jax.ShapeDtypeStruct((n, HEAD_W), BF16),
        scratch_shapes=[pltpu.VMEM((hps, 1, blk), F32), pltpu.VMEM((hps, SLOT, blk), F32),
                        pltpu.VMEM((hps, blk, blk), F32), pltpu.VMEM((hps, blk, blk), F32)],
        compiler_params=pltpu.CompilerParams(
            dimension_semantics=("arbitrary", "arbitrary", "arbitrary"),
            vmem_limit_bytes=VMEM_LIMIT),
        name="fox_attn",
    )(fqk, fqk, fvt, gain)


def _diff_attn_kernel(q_ref, k_ref, vt_ref, lam_ref, g_ref, o_ref, m_ref, acc_ref,
                      sa_ref, sb_ref, *, blk, lam_init):
    i = pl.program_id(2)
    heads = range(ATTN_HEADS_PER_STEP)
    slot = lambda s: slice(s * SLOT, (s + 1) * SLOT)
    lane = lax.broadcasted_iota(jnp.int32, (blk, SLOT), 1)
    q = []
    for hh in heads:
        q_h = q_ref[:, slot(hh)]
        q.append([jnp.where((lane >= c * DA_DQK) & (lane < (c + 1) * DA_DQK), q_h,
                            jnp.zeros_like(q_h)) for c in range(2)])
    bufs = (sa_ref, sb_ref)
    m_ref[...] = jnp.full(m_ref.shape, NEG, F32)
    acc_ref[...] = jnp.zeros_like(acc_ref)

    def scores(j, buf):
        rows = pl.ds(_block_start(j, blk), blk)
        for hh in heads:
            k_h = k_ref[rows, slot(hh)]
            for c in range(2):
                bufs[buf][2 * hh + c] = lax.dot_general(k_h, q[hh][c], _NT,
                                                        preferred_element_type=F32)

    def consume(j, buf, masked):
        mask = _causal_valid(i, j, blk) if masked else None
        for hh in heads:
            ps, alphas = [], []
            for c in range(2):
                hc = 2 * hh + c
                s = bufs[buf][hc]
                if masked:
                    s = jnp.where(mask, s, NEG)
                p, alpha, m_new = _softmax_block(s, m_ref[hc])
                m_ref[hc] = m_new
                ps.append(p)
                alphas.append(alpha)
            pv = jnp.dot(vt_ref[0, hh, j], jnp.concatenate(ps, axis=1),
                         preferred_element_type=F32)
            for c in range(2):
                hc = 2 * hh + c
                acc_ref[hc] = alphas[c] * acc_ref[hc] + pv[:, c * blk:(c + 1) * blk]

    def first():
        lo = blk - FIRST_KEYS
        mask = _causal_valid(i, 0, blk, lo)
        for hh in heads:
            k_h = k_ref[lo:blk, slot(hh)]
            ps, alphas = [], []
            for c in range(2):
                hc = 2 * hh + c
                s = lax.dot_general(k_h, q[hh][c], _NT, preferred_element_type=F32)
                p, alpha, m_new = _softmax_block(jnp.where(mask, s, NEG), m_ref[hc])
                m_ref[hc] = m_new
                ps.append(p)
                alphas.append(alpha)
            pv = jnp.dot(vt_ref[0, hh, 0, :, lo:blk], jnp.concatenate(ps, axis=1),
                         preferred_element_type=F32)
            for c in range(2):
                hc = 2 * hh + c
                acc_ref[hc] = alphas[c] * acc_ref[hc] + pv[:, c * blk:(c + 1) * blk]

    _pipelined_key_blocks(i, first, scores, consume)
    lv = lam_ref[...]
    lam = (jnp.exp(jnp.sum(lv[0:1] * lv[1:2], axis=-1, keepdims=True))
           - jnp.exp(jnp.sum(lv[2:3] * lv[3:4], axis=-1, keepdims=True)) + lam_init)
    for hh in heads:
        o = _head_out(acc_ref[2 * hh].T) - lam * _head_out(acc_ref[2 * hh + 1].T)
        o_ref[:, slot(hh)] = (_head_rms(o, g_ref[...]) * (1.0 - lam_init)).astype(BF16)


def _diff_attn(dqk, dvt, lam_vecs, gain, lam_init, batch, seq_len):
    n = dqk.shape[0]
    blk = ATTN_BLOCK
    nq = seq_len // blk
    hps = ATTN_HEADS_PER_STEP
    groups = DA_HEADS // hps
    return pl.pallas_call(
        functools.partial(_diff_attn_kernel, blk=blk, lam_init=lam_init),
        grid=(batch, groups, nq),
        in_specs=[pl.BlockSpec((blk, hps * SLOT), lambda b, h, i: (b * nq + i, h)),
                  _key_spec(seq_len, groups),
                  _value_spec(seq_len),
                  pl.BlockSpec((8, LANES), lambda b, h, i: (0, 0)),
                  pl.BlockSpec((1, SLOT), lambda b, h, i: (0, 0))],
        out_specs=pl.BlockSpec((blk, hps * SLOT), lambda b, h, i: (b * nq + i, h)),
        out_shape=jax.ShapeDtypeStruct((n, HEAD_W), BF16),
        scratch_shapes=[pltpu.VMEM((2 * hps, 1, blk), F32),
                        pltpu.VMEM((2 * hps, SLOT, blk), F32),
                        pltpu.VMEM((2 * hps, blk, blk), F32),
                        pltpu.VMEM((2 * hps, blk, blk), F32)],
        compiler_params=pltpu.CompilerParams(
            dimension_semantics=("arbitrary", "arbitrary", "arbitrary"),
            vmem_limit_bytes=VMEM_LIMIT),
        name="diff_attn",
    )(dqk, dqk, dvt, lam_vecs, gain)


def _hgrn_kernel(q_ref, i_ref, g_ref, f_ref, lb_ref, gain_ref, o_ref,
                 st_ref, hs_ref, *, tl):
    j = pl.program_id(2)

    @pl.when(j == 0)
    def _():
        st_ref[...] = jnp.zeros_like(st_ref)

    c_rows, sub = HG_CHUNK, HG_SUB
    tri = (lax.broadcasted_iota(jnp.int32, (c_rows, c_rows), 1)
           <= lax.broadcasted_iota(jnp.int32, (c_rows, c_rows), 0)).astype(F32)
    row_c = lax.broadcasted_iota(jnp.int32, (c_rows, 1), 0)
    t_sub = lax.broadcasted_iota(jnp.int32, (sub, 1), 0)
    lane_c = lax.broadcasted_iota(jnp.int32, (sub, c_rows), 1)

    t_all = lax.broadcasted_iota(jnp.int32, (c_rows, c_rows), 0)
    s_all = lax.broadcasted_iota(jnp.int32, (c_rows, c_rows), 1)
    level_masks = []
    g = sub
    while g < c_rows:
        level_masks.append(((t_all // g) == (s_all // g) + 1) & ((s_all // g) % 2 == 0))
        g *= 2

    heads = range(HG_PER_STEP)
    cols = [slice(hh * HG_DK, (hh + 1) * HG_DK) for hh in heads]

    def chunk(c, carry):
        r0 = pl.multiple_of(c * c_rows, c_rows)
        rows = pl.ds(r0, c_rows)
        valid = (j * tl + r0 + row_c) >= LPAD
        G, kk, qs, v, st, o_inter = [], [], [], [], [], []
        for hh in heads:
            lb = lb_ref[:, cols[hh]]
            sig = _sigmoid(f_ref[rows, cols[hh]])
            log_f = jnp.where(valid, jnp.log2(jnp.maximum(lb + (1.0 - lb) * sig, TINY)), 0.0)
            kk.append(jnp.where(valid, (1.0 - lb) * (1.0 - sig), 0.0))
            G.append(jnp.dot(tri, log_f, precision=lax.Precision.HIGHEST,
                             preferred_element_type=F32))
            qf = q_ref[rows, cols[hh]].astype(F32)
            qs.append(qf * _sigmoid(qf) * (HG_DK ** -0.5))
            v.append(i_ref[rows, cols[hh]])
            hs_ref[hh] = G[hh] - jnp.log2(kk[hh])
            st.append(st_ref[hh])
            o_inter.append(lax.dot_general(
                (qs[hh] * jnp.exp2(G[hh])).astype(BF16), st[hh].astype(BF16), _NT,
                preferred_element_type=F32))
        a_mat = [jnp.zeros((c_rows, c_rows), F32) for _ in heads]
        g = sub
        for mask in level_masks:
            for hh in heads:
                ref = jnp.concatenate(
                    [jnp.broadcast_to(G[hh][p + g - 1:p + g], (2 * g, HG_DK))
                     for p in range(0, c_rows, 2 * g)], axis=0)
                e = jnp.exp2(-jnp.abs(G[hh] - ref))
                part = lax.dot_general((qs[hh] * e).astype(BF16), (kk[hh] * e).astype(BF16),
                                       _NT, preferred_element_type=F32)
                a_mat[hh] = jnp.where(mask, part, a_mat[hh])
            g *= 2
        a_rows = [[] for _ in heads]
        for b in range(c_rows // sub):
            lo = b * sub
            for hh in heads:
                q_b = qs[hh][lo:lo + sub]
                g_b = G[hh][lo:lo + sub]
                a_blk = a_mat[hh][lo:lo + sub]
                for s in range(lo, lo + sub):
                    y = q_b * jnp.exp2(g_b - hs_ref[hh, s:s + 1, :])
                    a_blk = jnp.where(lane_c == s, jnp.sum(y, axis=-1, keepdims=True), a_blk)
                a_rows[hh].append(jnp.where(lane_c <= t_sub + lo, a_blk, 0.0))
        for hh in heads:
            a_full = jnp.concatenate(a_rows[hh], axis=0).astype(BF16)
            o = o_inter[hh] + jnp.dot(a_full, v[hh], preferred_element_type=F32)
            g_last = G[hh][c_rows - 1:c_rows]
            kd = kk[hh] * jnp.exp2(g_last - G[hh])
            st_ref[hh] = st[hh] * jnp.exp2(g_last) + lax.dot_general(
                v[hh], kd.astype(BF16), _TN, preferred_element_type=F32)
            gate = g_ref[rows, cols[hh]].astype(F32)
            o = _rms(o, gain_ref[...]) * (gate * _sigmoid(gate))
            o_ref[rows, cols[hh]] = o.astype(BF16)
        return carry

    lax.fori_loop(0, tl // c_rows, chunk, 0)


def _hgrn(hqig, hf, lb_row, gain, batch, seq_len):
    n = hqig.shape[0]
    tl = ROW_TILE
    nt = seq_len // tl
    w = HG_PER_STEP * HG_DK
    groups = HG_HEADS // HG_PER_STEP
    col = lambda off: pl.BlockSpec((tl, w), lambda b, h, j: (b * nt + j, off + h))
    return pl.pallas_call(
        functools.partial(_hgrn_kernel, tl=tl),
        grid=(batch, groups, nt),
        in_specs=[col(0), col(groups), col(2 * groups), col(0),
                  pl.BlockSpec((1, w), lambda b, h, j: (0, h)),
                  pl.BlockSpec((1, HG_DK), lambda b, h, j: (0, 0))],
        out_specs=col(0),
        out_shape=jax.ShapeDtypeStruct((n, HG_WIDTH), BF16),
        scratch_shapes=[pltpu.VMEM((HG_PER_STEP, HG_DK, HG_DK), F32),
                        pltpu.VMEM((HG_PER_STEP, HG_CHUNK, HG_DK), F32)],
        compiler_params=pltpu.CompilerParams(
            dimension_semantics=("arbitrary", "arbitrary", "arbitrary")),
        name="hgrn2",
    )(hqig, hqig, hqig, hf, lb_row, gain)


def _out_router_kernel(oa_ref, ob_ref, oc_ref, h_ref, wo_ref, g_ref, wrh_ref, wrl_ref, br_ref,
                       h1_ref, route_ref, cnt_ref, carry_ref):
    @pl.when(pl.program_id(0) == 0)
    def _():
        carry_ref[...] = jnp.zeros_like(carry_ref)

    h1 = (h_ref[...]
          + jnp.dot(oa_ref[...], wo_ref[0:512, :], preferred_element_type=F32)
          + jnp.dot(ob_ref[...], wo_ref[512:1024, :], preferred_element_type=F32)
          + jnp.dot(oc_ref[...], wo_ref[1024:1536, :], preferred_element_type=F32))
    h1_ref[...] = h1
    u = _rms(h1, g_ref[...])
    u_hi = u.astype(BF16)
    u_lo = (u - u_hi.astype(F32)).astype(BF16)
    logits = (jnp.dot(u_hi, wrh_ref[...], preferred_element_type=F32)
              + (jnp.dot(u_hi, wrl_ref[...], preferred_element_type=F32)
                 + jnp.dot(u_lo, wrh_ref[...], preferred_element_type=F32))
              + br_ref[...])
    tm = logits.shape[0]
    lane = lax.broadcasted_iota(jnp.int32, (tm, LANES), 1).astype(F32)
    big = float(LANES)
    is_g = (lane >= N_EXPERTS) & (lane < N_EXPERTS + N_GROUPS)
    gl = jnp.where(is_g, logits, -jnp.inf)
    gmax = jnp.max(gl, axis=-1, keepdims=True)
    gsel = jnp.min(jnp.where(gl == gmax, lane, big), axis=-1, keepdims=True) - N_EXPERTS
    p_g = 1.0 / jnp.sum(jnp.exp(gl - gmax), axis=-1, keepdims=True)
    lo = gsel * EXPERTS_PER_GROUP
    el = jnp.where((lane >= lo) & (lane < lo + EXPERTS_PER_GROUP), logits, -jnp.inf)
    m1 = jnp.max(el, axis=-1, keepdims=True)
    i1 = jnp.min(jnp.where(el == m1, lane, big), axis=-1, keepdims=True)
    el2 = jnp.where(lane == i1, -jnp.inf, el)
    m2 = jnp.max(el2, axis=-1, keepdims=True)
    i2 = jnp.min(jnp.where(el2 == m2, lane, big), axis=-1, keepdims=True)
    r = jnp.exp(m2 - m1)
    gate1 = p_g / (1.0 + r)
    gate2 = gate1 * r
    oh1 = lane == i1
    oh2 = lane == i2
    onehot = jnp.where(oh1 | oh2, 1.0, 0.0)
    tri = (lax.broadcasted_iota(jnp.int32, (tm, tm), 1)
           < lax.broadcasted_iota(jnp.int32, (tm, tm), 0)).astype(BF16)
    before = jnp.dot(tri, onehot.astype(BF16), preferred_element_type=F32) + carry_ref[...]
    rank1 = jnp.sum(jnp.where(oh1, before, 0.0), axis=-1, keepdims=True)
    rank2 = jnp.sum(jnp.where(oh2, before, 0.0), axis=-1, keepdims=True)
    total = carry_ref[...] + jnp.sum(onehot, axis=0, keepdims=True)
    carry_ref[...] = total
    cnt_ref[...] = total
    route = jnp.where(lane == 0, i1, 0.0)
    for idx, val in ((1, i2), (2, gate1), (3, gate2), (4, rank1), (5, rank2)):
        route = jnp.where(lane == idx, val, route)
    route_ref[...] = route


def _out_router(oa, ob, oc, h, w_out, g, w_rt, b_rt):
    n = h.shape[0]
    w_rt_hi = w_rt.astype(BF16)
    w_rt_lo = (w_rt - w_rt_hi.astype(F32)).astype(BF16)
    tm = ROUTER_TILE
    row = lambda w: pl.BlockSpec((tm, w), lambda i: (i, 0))
    const = lambda r, c: pl.BlockSpec((r, c), lambda i: (0, 0))
    return pl.pallas_call(
        _out_router_kernel,
        grid=(n // tm,),
        in_specs=[row(HG_WIDTH), row(HEAD_W), row(HEAD_W), row(D_MODEL),
                  const(HG_WIDTH + 2 * HEAD_W, D_MODEL),
                  const(1, D_MODEL), const(D_MODEL, LANES), const(D_MODEL, LANES),
                  const(1, LANES)],
        out_specs=[row(D_MODEL), row(LANES), const(1, LANES)],
        out_shape=[jax.ShapeDtypeStruct((n, D_MODEL), F32),
                   jax.ShapeDtypeStruct((n, LANES), F32),
                   jax.ShapeDtypeStruct((1, LANES), F32)],
        scratch_shapes=[pltpu.VMEM((1, LANES), F32)],
        compiler_params=pltpu.CompilerParams(dimension_semantics=("arbitrary",)),
        name="out_router",
    )(oa, ob, oc, h, w_out, g, w_rt_hi, w_rt_lo, b_rt)


def _dispatch_kernel(dest_ref, unused_ref, h_ref, g_ref, xs_in_ref, xs_ref, back_ref,
                     u_ref, sem, *, tm, n, tb):
    del xs_in_ref
    i = pl.program_id(0)
    n_tiles = n // tm
    base = (i - 1) * tm
    cur = i % 2

    @pl.when(i == 0)
    def _():
        def fill(s, carry):
            parity = lax.shift_right_logical(s, tb.bit_length() - 1) & 1
            back_ref[s] = 2 * n + parity * tb + (s & (tb - 1))
            return carry

        n_ranges = unused_ref.shape[0] // 2

        def one_range(e, carry):
            return lax.fori_loop(unused_ref[e], unused_ref[n_ranges + e], fill, carry)

        lax.fori_loop(0, n_ranges, one_range, 0)

    def issue(r, carry):
        for k in range(2):
            src = k * n + base + r
            d = dest_ref[src]
            back_ref[d] = src
            pltpu.make_async_copy(u_ref.at[1 - cur, pl.ds(r, 1)], xs_ref.at[pl.ds(d, 1)],
                                  sem).start()
        return carry

    @pl.when(i >= 1)
    def _():
        lax.fori_loop(0, tm, issue, 0, unroll=8)

    @pl.when(i < n_tiles)
    def _():
        _store_token_tiles(u_ref.at[cur], _rms(h_ref[...], g_ref[...]))

    @pl.when(i >= 1)
    def _():
        for _ in range(2):
            pltpu.make_async_copy(u_ref.at[1 - cur], xs_ref.at[pl.ds(0, tm)], sem).wait()


def _dispatch(dest, unused, h1, g, xs_init):
    n = h1.shape[0]
    tm = ROW_TILE
    p_rows = xs_init.shape[0]
    return pl.pallas_call(
        functools.partial(_dispatch_kernel, tm=tm, n=n, tb=MOE_BLOCK),
        grid_spec=pltpu.PrefetchScalarGridSpec(
            num_scalar_prefetch=2,
            grid=(n // tm + 1,),
            in_specs=[pl.BlockSpec((tm, D_MODEL),
                                   lambda i, d, un: (jnp.minimum(i, n // tm - 1), 0)),
                      pl.BlockSpec((1, D_MODEL), lambda i, d, un: (0, 0)),
                      pl.BlockSpec(memory_space=pl.ANY)],
            out_specs=[pl.BlockSpec(memory_space=pl.ANY),
                       pl.BlockSpec(memory_space=pltpu.SMEM)],
            scratch_shapes=[pltpu.VMEM((2, tm) + TOKEN_TILE, F32),
                            pltpu.SemaphoreType.DMA(())]),
        out_shape=[jax.ShapeDtypeStruct(xs_init.shape, xs_init.dtype),
                   jax.ShapeDtypeStruct((p_rows,), jnp.int32)],
        input_output_aliases={4: 0},
        compiler_params=pltpu.CompilerParams(dimension_semantics=("arbitrary",)),
        name="moe_dispatch",
    )(dest, unused, h1, g, xs_init)


def _expert_kernel(be_ref, na_ref, back_ref, x_ref, w1_ref, w3_ref, w2_ref, out_ref,
                   y_ref, w1b_ref, w3b_ref, w2b_ref, sem, *, tb):
    j = pl.program_id(0)
    na = na_ref[0]
    cur = j % 2

    def scatter(block, buf):
        for r in range(tb):
            pltpu.make_async_copy(y_ref.at[buf, pl.ds(r, 1)],
                                  out_ref.at[pl.ds(back_ref[block * tb + r], 1)],
                                  sem.at[buf]).start()

    def compute(buf):
        x = _load_token_tiles(x_ref).astype(BF16)
        a = jnp.dot(x, w1b_ref[...], preferred_element_type=F32)
        b = jnp.dot(x, w3b_ref[...], preferred_element_type=F32)
        act = (a * _sigmoid(a) * b).astype(BF16)
        _store_token_tiles(y_ref.at[buf],
                           jnp.dot(act, w2b_ref[...], preferred_element_type=F32))

    new_expert = (j == 0) | (be_ref[j] != be_ref[jnp.maximum(j - 1, 0)])

    @pl.when((j < na) & new_expert)
    def _():
        w1b_ref[...] = w1_ref[0].astype(BF16)
        w3b_ref[...] = w3_ref[0].astype(BF16)
        w2b_ref[...] = w2_ref[0].astype(BF16)

    @pl.when((j >= 2) & (j < na + 2))
    def _():
        pltpu.make_async_copy(y_ref.at[cur], out_ref.at[pl.ds(0, tb)], sem.at[cur]).wait()

    @pl.when(j == 0)
    def _():
        y_ref[...] = jnp.zeros_like(y_ref)
        first_spare = out_ref.shape[0] - 2 * tb
        spare = [pltpu.make_async_copy(
            y_ref.at[b], out_ref.at[pl.ds(first_spare + b * tb, tb)], sem.at[b])
            for b in range(2)]
        for copy in spare:
            copy.start()
        for copy in spare:
            copy.wait()
        compute(cur)

    @pl.when((j >= 1) & (j < na))
    def _():
        scatter(j - 1, 1 - cur)
        compute(cur)

    @pl.when(j == na)
    def _():
        scatter(j - 1, 1 - cur)


def _experts(blk_expert, n_active, slot_back, xs, w1, w3, w2, layer, n):
    p_rows = xs.shape[0]
    tb = MOE_BLOCK
    last = lambda j, na: jnp.maximum(jnp.minimum(j, na[0] - 1), 0)
    w_spec = lambda r, c: pl.BlockSpec((None, 1, r, c),
                                       lambda j, be, na, back: (layer, be[j], 0, 0))
    return pl.pallas_call(
        functools.partial(_expert_kernel, tb=tb),
        grid_spec=pltpu.PrefetchScalarGridSpec(
            num_scalar_prefetch=3,
            grid=(p_rows // tb + 1,),
            in_specs=[pl.BlockSpec((tb,) + TOKEN_TILE,
                                   lambda j, be, na, back: (last(j, na), 0, 0)),
                      w_spec(D_MODEL, D_EXPERT), w_spec(D_MODEL, D_EXPERT),
                      w_spec(D_EXPERT, D_MODEL)],
            out_specs=pl.BlockSpec(memory_space=pl.ANY),
            scratch_shapes=[pltpu.VMEM((2, tb) + TOKEN_TILE, F32),
                            pltpu.VMEM((D_MODEL, D_EXPERT), BF16),
                            pltpu.VMEM((D_MODEL, D_EXPERT), BF16),
                            pltpu.VMEM((D_EXPERT, D_MODEL), BF16),
                            pltpu.SemaphoreType.DMA((2,))]),
        out_shape=jax.ShapeDtypeStruct((2 * n + 2 * tb,) + TOKEN_TILE, F32),
        compiler_params=pltpu.CompilerParams(dimension_semantics=("arbitrary",),
                                             vmem_limit_bytes=VMEM_LIMIT),
        name="moe_experts",
    )(blk_expert, n_active, slot_back, xs, w1, w3, w2)


def _combine_kernel(h_ref, route_ref, gfin_ref, y0_ref, y1_ref, o_ref, *, final):
    route = route_ref[...]
    out = (h_ref[...] + route[:, 2:3] * _load_token_tiles(y0_ref)
           + route[:, 3:4] * _load_token_tiles(y1_ref))
    if final:
        out = _rms(out, gfin_ref[...])
    o_ref[...] = out


def _combine(h1, route, g_final, ys, batch, seq_len, final):
    n = h1.shape[0]
    tm = ROW_TILE
    tiles_per_seq = seq_len // tm
    skip_tiles = (LPAD + N_META) // tm if final else 0
    out_tiles = tiles_per_seq - skip_tiles
    tile = lambda b, i: b * tiles_per_seq + skip_tiles + i
    in_row = lambda w: pl.BlockSpec((tm, w), lambda b, i: (tile(b, i), 0))
    y_rows = lambda k: pl.BlockSpec((tm,) + TOKEN_TILE,
                                    lambda b, i: (k * (n // tm) + tile(b, i), 0, 0))
    return pl.pallas_call(
        functools.partial(_combine_kernel, final=final),
        grid=(batch, out_tiles),
        in_specs=[in_row(D_MODEL), in_row(LANES),
                  pl.BlockSpec((1, D_MODEL), lambda b, i: (0, 0)),
                  y_rows(0), y_rows(1)],
        out_specs=pl.BlockSpec((tm, D_MODEL), lambda b, i: (b * out_tiles + i, 0)),
        out_shape=jax.ShapeDtypeStruct((batch * out_tiles * tm, D_MODEL), F32),
        compiler_params=pltpu.CompilerParams(dimension_semantics=("arbitrary", "arbitrary")),
        name="moe_combine",
    )(h1, route, g_final, ys, ys)


def _to_slots(m):
    lead = m.shape[:-1]
    x = m.reshape(lead + (4, 64))
    x = jnp.pad(x, [(0, 0)] * (len(lead) + 1) + [(0, SLOT - 64)])
    return x.reshape(lead + (HEAD_W,))


def _rope_partner(m):
    lead = m.shape[:-1]
    x = m.reshape(lead + (DA_WIDTH // DA_DQK, 2, DA_DQK // 2))
    return x[..., ::-1, :].reshape(lead + (DA_WIDTH,))


def _rope_tables(seq_len):
    pos = (jnp.arange(seq_len) - LPAD).astype(F32)
    inv = ROPE_THETA ** (-jnp.arange(0, DA_DQK, 2, dtype=F32) / DA_DQK)
    ang = pos[:, None] * inv[None, :]
    comps = DA_DV // DA_DQK
    cos = jnp.tile(jnp.cos(ang), (1, 2 * comps))
    sin = jnp.tile(jnp.concatenate([-jnp.sin(ang), jnp.sin(ang)], axis=1), (1, comps))
    q_scale = DA_DQK ** -0.5 * LOG2E
    pad = lambda t: jnp.pad(t, ((0, 0), (0, SLOT - DA_DV)))
    lay = lambda t: jnp.concatenate([pad(t * q_scale), pad(t)], axis=1)
    return lay(cos), lay(sin)


def _in_weights(w):
    hq, hf, hi, hg = (w[:, i * 512:(i + 1) * 512] for i in range(4))
    dq, dk, dv = (w[:, 2048 + i * 256:2048 + (i + 1) * 256] for i in range(3))
    fq, fk, fv = (w[:, 2816 + i * 256:2816 + (i + 1) * 256] for i in range(3))
    ff = w[:, 3584:3588]
    cat = jnp.concatenate(
        [hq, hi, hg, hf,
         _to_slots(dq), _to_slots(dk),
         _to_slots(_rope_partner(dq)), _to_slots(_rope_partner(dk)),
         _to_slots(fq * (FX_DH ** -0.5)), _to_slots(fk),
         ff, jnp.zeros((D_MODEL, LANES - FX_HEADS), w.dtype)],
        axis=1)
    w_vt = jnp.concatenate([_to_slots(dv), _to_slots(fv)], axis=1).T
    return cat.astype(BF16), w_vt.astype(BF16)


def _pad_lanes(v, width=LANES):
    return jnp.zeros((1, width), F32).at[0, :v.shape[0]].set(v.astype(F32))


def kernel(x, meta_tokens, norm_mix, w_in, hgrn_lb, hgrn_norm, diff_lambda, diff_norm,
           fox_bias, fox_norm, w_out, norm_ffn, w_group, b_group, w_router, b_router,
           w1, w3, w2, norm_final):
    batch, seq, d = x.shape
    depth = w_in.shape[0]
    seq_len = LPAD + N_META + seq
    n = batch * seq_len
    pad = jnp.zeros((batch, LPAD, d), x.dtype)
    meta = jnp.broadcast_to(meta_tokens.astype(x.dtype)[None], (batch, N_META, d))
    h = jnp.concatenate([pad, meta, x], axis=1).reshape(n, d)

    s_lb = jax.nn.softmax(hgrn_lb.astype(F32), axis=0)
    lb_all = jnp.cumsum(s_lb, axis=0) - s_lb[0]
    cos_t, sin_t = _rope_tables(seq_len)

    tb = MOE_BLOCK
    n_blocks = (2 * n) // tb + N_EXPERTS
    p_rows = n_blocks * tb

    xs = jnp.zeros((p_rows,) + TOKEN_TILE, F32)
    for layer in range(depth):
        lam_init = 0.8 - 0.6 * math.exp(-0.3 * layer)
        w_cat, w_vt = _in_weights(w_in[layer])
        hqig, hf, dqk, fqk, ff, dvt, fvt = _in_proj(
            h, norm_mix[layer][None, :], w_cat, w_vt, cos_t, sin_t, batch, seq_len)

        o_a = _hgrn(hqig, hf, lb_all[layer][None, :], hgrn_norm[layer][None, :],
                    batch, seq_len)

        lam_vecs = jnp.zeros((8, LANES), F32).at[:4, :DA_DQK].set(diff_lambda[layer].astype(F32))
        o_b = _diff_attn(dqk, dvt, lam_vecs, _pad_lanes(diff_norm[layer]), lam_init,
                         batch, seq_len)

        fqk_aug = _fox_prep(ff.reshape(batch, seq_len, LANES), _pad_lanes(fox_bias[layer]),
                            fqk, batch, seq_len)
        o_c = _fox_attn(fqk_aug, fvt, _pad_lanes(fox_norm[layer]), batch, seq_len)

        wo = w_out[layer]
        slot_rows = lambda m: _to_slots(m.T).T
        wo_cat = jnp.concatenate(
            [wo[:HG_WIDTH], slot_rows(wo[HG_WIDTH:HG_WIDTH + DA_WIDTH]),
             slot_rows(wo[HG_WIDTH + DA_WIDTH:])], axis=0).astype(BF16)
        w_rt = jnp.concatenate(
            [w_router[layer], w_group[layer],
             jnp.zeros((d, LANES - N_EXPERTS - N_GROUPS), F32)], axis=1)
        b_rt = _pad_lanes(jnp.concatenate([b_router[layer], b_group[layer]]))
        h1, route, counts = _out_router(
            o_a, o_b, o_c, h, wo_cat, norm_ffn[layer][None, :], w_rt, b_rt)

        cnt = counts[0, :N_EXPERTS].astype(jnp.int32)
        padded = (cnt + tb - 1) // tb * tb
        p_end = jnp.cumsum(padded)
        p_start = p_end - padded
        ids = route[:, 0:2].astype(jnp.int32)
        ranks = route[:, 4:6].astype(jnp.int32)
        dest = (p_start[ids] + ranks).T.reshape(2 * n)
        blk_start = jnp.arange(n_blocks + 1, dtype=jnp.int32) * tb
        blk_expert = jnp.minimum(
            jnp.sum((p_end[None, :] <= blk_start[:, None]).astype(jnp.int32), axis=1),
            N_EXPERTS - 1)
        n_active = (p_end[-1:] // tb).astype(jnp.int32)

        unused = jnp.concatenate(
            [p_start + cnt, p_end[-1:], p_end, jnp.full((1,), p_rows)]).astype(jnp.int32)
        xs, slot_back = _dispatch(dest, unused, h1, norm_ffn[layer][None, :], xs)
        ys = _experts(blk_expert, n_active, slot_back, xs, w1, w3, w2, layer, n)
        h = _combine(h1, route, norm_final[None, :], ys, batch, seq_len,
                     final=(layer == depth - 1))

    return h.reshape(batch, seq, d)
```

```python
import functools
import math

import jax
import jax.numpy as jnp
from jax import lax
from jax.experimental import pallas as pl
from jax.experimental.pallas import tpu as pltpu

F32 = jnp.float32
BF16 = jnp.bfloat16

D_MODEL = 1024
N_META = 16
HG_HEADS = 4
HG_DK = 128
HG_WIDTH = 512
HG_CHUNK = 64
HG_SUB = 8
HG_PER_STEP = 4
DA_HEADS = 4
DA_DV = 64
DA_DQK = 32
DA_WIDTH = 256
FX_HEADS = 4
FX_DH = 64
FX_WIDTH = 256
N_GROUPS = 4
EXPERTS_PER_GROUP = 8
N_EXPERTS = 32
D_EXPERT = 512
ROPE_THETA = 10000.0
EPS = 1e-6
NEG = -1e30
TINY = 1e-30
LOG2E = 1.4426950408889634

LANES = 128
TOKEN_TILE = (D_MODEL,)
_TOKEN_ORIGIN = (0,) * len(TOKEN_TILE)
SLOT = LANES
HEAD_W = 4 * SLOT
ROW_TILE = 256
PROJ_TILE = 512
ROUTER_TILE = 512
ATTN_BLOCK = 512
ATTN_HEADS_PER_STEP = 4
MOE_BLOCK = 256
LPAD = ATTN_BLOCK - N_META
FIRST_KEYS = LANES
assert ATTN_BLOCK - FIRST_KEYS <= LPAD
ONE_LANE = 64
VMEM_LIMIT = 56 * 1024 * 1024

_C_HQIG = (0, 1536)
_C_HF = (1536, 2048)
_C_DQK = (2048, 3072)
_C_DQKP = (3072, 4096)
_C_FQ = (4096, 4608)
_C_FK = (4608, 5120)
_C_FF = (5120, 5248)
IN_COLS = 5248

_NT = (((1,), (1,)), ((), ()))
_TN = (((0,), (0,)), ((), ()))


def _store_token_tiles(ref, x):
    ref[...] = x


def _load_token_tiles(ref):
    return ref[...]


def _sigmoid(x):
    return 1.0 / (1.0 + jnp.exp(-x))


def _rms(x, g):
    return x * lax.rsqrt(jnp.mean(x * x, axis=-1, keepdims=True) + EPS) * g


def _in_proj_kernel(h_ref, g_ref, w_ref, wvt_ref, cos_ref, sin_ref,
                    hqig_ref, hf_ref, dqk_ref, fqk_ref, ff_ref, dvt_ref, fvt_ref):
    u = _rms(h_ref[...], g_ref[...]).astype(BF16)

    def mm(c):
        return jnp.dot(u, w_ref[:, c[0]:c[1]], preferred_element_type=F32)

    hqig_ref[...] = mm(_C_HQIG).astype(BF16)
    hf_ref[...] = mm(_C_HF)
    def per_slot(t_ref):
        return jnp.concatenate([t_ref[:, 0:SLOT]] * 4 + [t_ref[:, SLOT:2 * SLOT]] * 4, axis=1)

    dqk_ref[...] = (mm(_C_DQK) * per_slot(cos_ref)
                    + mm(_C_DQKP) * per_slot(sin_ref)).astype(BF16)
    fqk_ref[:, 0:HEAD_W] = (mm(_C_FQ) * LOG2E).astype(BF16)
    fqk_ref[:, HEAD_W:2 * HEAD_W] = mm(_C_FK).astype(BF16)
    ff_ref[...] = mm(_C_FF)
    vt = lax.dot_general(wvt_ref[...], u, _NT, preferred_element_type=F32)
    slot_row = lax.broadcasted_iota(jnp.int32, vt.shape, 0) % SLOT
    vt = jnp.where(slot_row == ONE_LANE, 1.0, vt).astype(BF16)
    for h in range(4):
        dvt_ref[0, h, 0] = vt[h * SLOT:(h + 1) * SLOT]
        fvt_ref[0, h, 0] = vt[HEAD_W + h * SLOT:HEAD_W + (h + 1) * SLOT]


def _in_proj(h, g, w_cat, w_vt, cos_t, sin_t, batch, seq_len):
    n = h.shape[0]
    tm = PROJ_TILE
    assert tm == ATTN_BLOCK
    nk = seq_len // tm
    row = lambda w: pl.BlockSpec((tm, w), lambda i: (i, 0))
    tab = pl.BlockSpec((tm, 2 * SLOT), lambda i: (i % nk, 0))
    once = lambda r, c: pl.BlockSpec((r, c), lambda i: (0, 0), pipeline_mode=pl.Buffered(1))
    widths = (1536, 512, 2 * HEAD_W, 2 * HEAD_W, LANES)
    dtypes = (BF16, F32, BF16, BF16, F32)
    vt_spec = pl.BlockSpec((1, 4, 1, SLOT, tm), lambda i: (i // nk, 0, i % nk, 0, 0))
    vt_shape = jax.ShapeDtypeStruct((batch, 4, nk, SLOT, tm), BF16)
    return pl.pallas_call(
        _in_proj_kernel,
        grid=(n // tm,),
        in_specs=[row(D_MODEL),
                  pl.BlockSpec((1, D_MODEL), lambda i: (0, 0)),
                  once(D_MODEL, IN_COLS), once(2 * HEAD_W, D_MODEL),
                  tab, tab],
        out_specs=[row(w) for w in widths] + [vt_spec, vt_spec],
        out_shape=[jax.ShapeDtypeStruct((n, w), t) for w, t in zip(widths, dtypes)]
        + [vt_shape, vt_shape],
        compiler_params=pltpu.CompilerParams(dimension_semantics=("arbitrary",),
                                             vmem_limit_bytes=VMEM_LIMIT),
        name="in_proj",
    )(h, g, w_cat, w_vt, cos_t, sin_t)


def _bf16_split3(x):
    hi = x.astype(BF16).astype(F32)
    r = x - hi
    mid = r.astype(BF16).astype(F32)
    return hi, mid, r - mid


def _fox_prep_kernel(ff_ref, b_ref, qk_ref, o_ref, carry_ref):
    @pl.when(pl.program_id(1) == 0)
    def _():
        carry_ref[...] = jnp.zeros_like(carry_ref)

    x = ff_ref[0] + b_ref[...]
    lf = jnp.minimum(x, 0.0) - jnp.log(1.0 + jnp.exp(-jnp.abs(x)))
    t = x.shape[0]
    tri = (lax.broadcasted_iota(jnp.int32, (t, t), 1)
           <= lax.broadcasted_iota(jnp.int32, (t, t), 0)).astype(F32)
    cs = jnp.dot(tri, lf, precision=lax.Precision.HIGHEST,
                 preferred_element_type=F32) + carry_ref[...]
    carry_ref[...] = cs[t - 1:t, :]
    parts = _bf16_split3(cs * LOG2E)
    lane = lax.broadcasted_iota(jnp.int32, (t, SLOT), 1)
    d0 = FX_DH
    for h in range(FX_HEADS):
        q = qk_ref[:, h * SLOT:(h + 1) * SLOT].astype(F32)
        k = qk_ref[:, HEAD_W + h * SLOT:HEAD_W + (h + 1) * SLOT].astype(F32)
        for i, part in enumerate(parts):
            col = part[:, h:h + 1]
            q = jnp.where(lane == d0 + i, col, q)
            k = jnp.where(lane == d0 + 3 + i, -col, k)
        q = jnp.where((lane >= d0 + 3) & (lane < d0 + 6), 1.0, q)
        k = jnp.where((lane >= d0) & (lane < d0 + 3), 1.0, k)
        o_ref[:, h * SLOT:(h + 1) * SLOT] = q.astype(BF16)
        o_ref[:, HEAD_W + h * SLOT:HEAD_W + (h + 1) * SLOT] = k.astype(BF16)


def _fox_prep(ff, bias_row, fqk, batch, seq_len):
    n = fqk.shape[0]
    t = PROJ_TILE
    nt = seq_len // t
    return pl.pallas_call(
        _fox_prep_kernel,
        grid=(batch, nt),
        in_specs=[pl.BlockSpec((1, t, LANES), lambda b, j: (b, j, 0)),
                  pl.BlockSpec((1, LANES), lambda b, j: (0, 0)),
                  pl.BlockSpec((t, 2 * HEAD_W), lambda b, j: (b * nt + j, 0))],
        out_specs=pl.BlockSpec((t, 2 * HEAD_W), lambda b, j: (b * nt + j, 0)),
        out_shape=jax.ShapeDtypeStruct((n, 2 * HEAD_W), BF16),
        scratch_shapes=[pltpu.VMEM((1, LANES), F32)],
        compiler_params=pltpu.CompilerParams(dimension_semantics=("arbitrary", "arbitrary")),
        name="fox_prep",
    )(ff, bias_row, fqk)


def _softmax_block(s, m_prev):
    m_new = jnp.maximum(m_prev, jnp.max(s, axis=0, keepdims=True))
    alpha = jnp.exp2(m_prev - m_new)
    p = jnp.exp2(s - m_new).astype(BF16)
    return p, alpha, m_new


def _block_start(j, blk):
    return j * blk if isinstance(j, int) else pl.multiple_of(j * blk, blk)


def _causal_valid(i, j, blk, first_row=0):
    k_idx = j * blk + first_row + lax.broadcasted_iota(jnp.int32, (blk - first_row, blk), 0)
    q_idx = i * blk + lax.broadcasted_iota(jnp.int32, (blk - first_row, blk), 1)
    return (k_idx <= q_idx) & (k_idx >= LPAD)


def _pipelined_key_blocks(i, first, scores, consume):
    first()

    @pl.when(i >= 1)
    def _():
        scores(1, 0)

    def pair(t, carry):
        scores(2 * t + 2, 1)
        consume(2 * t + 1, 0, False)
        scores(2 * t + 3, 0)
        consume(2 * t + 2, 1, False)
        return carry

    lax.fori_loop(0, lax.shift_right_logical(jnp.maximum(i - 1, 0), 1), pair, 0)
    odd = (i & 1) == 1

    @pl.when(odd)
    def _():
        consume(i, 0, True)

    @pl.when((i >= 2) & jnp.logical_not(odd))
    def _():
        scores(i, 1)
        consume(i - 1, 0, False)
        consume(i, 1, True)


def _head_out(acc):
    lane = lax.broadcasted_iota(jnp.int32, acc.shape, 1)
    return jnp.where(lane < ONE_LANE, acc / acc[:, ONE_LANE:ONE_LANE + 1], 0.0)


def _head_rms(o, g):
    ms = jnp.sum(o * o, axis=-1, keepdims=True) * (1.0 / ONE_LANE)
    return o * lax.rsqrt(ms + EPS) * g


def _fox_attn_kernel(q_ref, k_ref, vt_ref, g_ref, o_ref, m_ref, acc_ref, sa_ref, sb_ref,
                     *, blk):
    i = pl.program_id(2)
    heads = range(ATTN_HEADS_PER_STEP)
    slot = [slice(hh * SLOT, (hh + 1) * SLOT) for hh in heads]
    q = [q_ref[:, slot[hh]] for hh in heads]
    bufs = (sa_ref, sb_ref)
    m_ref[...] = jnp.full(m_ref.shape, NEG, F32)
    acc_ref[...] = jnp.zeros_like(acc_ref)

    def scores(j, buf):
        rows = pl.ds(_block_start(j, blk), blk)
        for hh in heads:
            bufs[buf][hh] = lax.dot_general(k_ref[rows, slot[hh]], q[hh], _NT,
                                            preferred_element_type=F32)

    def consume(j, buf, masked):
        mask = _causal_valid(i, j, blk) if masked else None
        for hh in heads:
            s = bufs[buf][hh]
            if masked:
                s = jnp.where(mask, s, NEG)
            p, alpha, m_new = _softmax_block(s, m_ref[hh])
            acc_ref[hh] = alpha * acc_ref[hh] + jnp.dot(
                vt_ref[0, hh, j], p, preferred_element_type=F32)
            m_ref[hh] = m_new

    def first():
        lo = blk - FIRST_KEYS
        mask = _causal_valid(i, 0, blk, lo)
        for hh in heads:
            s = lax.dot_general(k_ref[lo:blk, slot[hh]], q[hh], _NT,
                                preferred_element_type=F32)
            p, alpha, m_new = _softmax_block(jnp.where(mask, s, NEG), m_ref[hh])
            acc_ref[hh] = alpha * acc_ref[hh] + jnp.dot(
                vt_ref[0, hh, 0, :, lo:blk], p, preferred_element_type=F32)
            m_ref[hh] = m_new

    _pipelined_key_blocks(i, first, scores, consume)
    for hh in heads:
        o_ref[:, slot[hh]] = _head_rms(_head_out(acc_ref[hh].T), g_ref[...]).astype(BF16)


def _key_spec(seq_len, groups):
    return pl.BlockSpec((seq_len, ATTN_HEADS_PER_STEP * SLOT),
                        lambda b, h, i: (b, groups + h), pipeline_mode=pl.Buffered(1))


def _value_spec(seq_len):
    nk = seq_len // ATTN_BLOCK
    return pl.BlockSpec((1, ATTN_HEADS_PER_STEP, nk, SLOT, ATTN_BLOCK),
                        lambda b, h, i: (b, h, 0, 0, 0), pipeline_mode=pl.Buffered(1))


def _fox_attn(fqk, fvt, gain, batch, seq_len):
    n = fqk.shape[0]
    blk = ATTN_BLOCK
    nq = seq_len // blk
    hps = ATTN_HEADS_PER_STEP
    groups = FX_HEADS // hps
    return pl.pallas_call(
        functools.partial(_fox_attn_kernel, blk=blk),
        grid=(batch, groups, nq),
        in_specs=[pl.BlockSpec((blk, hps * SLOT), lambda b, h, i: (b * nq + i, h)),
                  _key_spec(seq_len, groups),
                  _value_spec(seq_len),
                  pl.BlockSpec((1, SLOT), lambda b, h, i: (0, 0))],
        out_specs=pl.BlockSpec((blk, hps * SLOT), lambda b, h, i: (b * nq + i, h)),
        out_shape=jax.ShapeDtypeStruct((n, HEAD_W), BF16),
        scratch_shapes=[pltpu.VMEM((hps, 1, blk), F32), pltpu.VMEM((hps, SLOT, blk), F32),
                        pltpu.VMEM((hps, blk, blk), F32), pltpu.VMEM((hps, blk, blk), F32)],
        compiler_params=pltpu.CompilerParams(
            dimension_semantics=("arbitrary", "arbitrary", "arbitrary"),
            vmem_limit_bytes=VMEM_LIMIT),
        name="fox_attn",
    )(fqk, fqk, fvt, gain)


def _diff_attn_kernel(q_ref, k_ref, vt_ref, lam_ref, g_ref, o_ref, m_ref, acc_ref,
                      sa_ref, sb_ref, *, blk, lam_init):
    i = pl.program_id(2)
    heads = range(ATTN_HEADS_PER_STEP)
    slot = lambda s: slice(s * SLOT, (s + 1) * SLOT)
    lane = lax.broadcasted_iota(jnp.int32, (blk, SLOT), 1)
    q = []
    for hh in heads:
        q_h = q_ref[:, slot(hh)]
        q.append([jnp.where((lane >= c * DA_DQK) & (lane < (c + 1) * DA_DQK), q_h,
                            jnp.zeros_like(q_h)) for c in range(2)])
    bufs = (sa_ref, sb_ref)
    m_ref[...] = jnp.full(m_ref.shape, NEG, F32)
    acc_ref[...] = jnp.zeros_like(acc_ref)

    def scores(j, buf):
        rows = pl.ds(_block_start(j, blk), blk)
        for hh in heads:
            k_h = k_ref[rows, slot(hh)]
            for c in range(2):
                bufs[buf][2 * hh + c] = lax.dot_general(k_h, q[hh][c], _NT,
                                                        preferred_element_type=F32)

    def consume(j, buf, masked):
        mask = _causal_valid(i, j, blk) if masked else None
        for hh in heads:
            ps, alphas = [], []
            for c in range(2):
                hc = 2 * hh + c
                s = bufs[buf][hc]
                if masked:
                    s = jnp.where(mask, s, NEG)
                p, alpha, m_new = _softmax_block(s, m_ref[hc])
                m_ref[hc] = m_new
                ps.append(p)
                alphas.append(alpha)
            pv = jnp.dot(vt_ref[0, hh, j], jnp.concatenate(ps, axis=1),
                         preferred_element_type=F32)
            for c in range(2):
                hc = 2 * hh + c
                acc_ref[hc] = alphas[c] * acc_ref[hc] + pv[:, c * blk:(c + 1) * blk]

    def first():
        lo = blk - FIRST_KEYS
        mask = _causal_valid(i, 0, blk, lo)
        for hh in heads:
            k_h = k_ref[lo:blk, slot(hh)]
            ps, alphas = [], []
            for c in range(2):
                hc = 2 * hh + c
                s = lax.dot_general(k_h, q[hh][c], _NT, preferred_element_type=F32)
                p, alpha, m_new = _softmax_block(jnp.where(mask, s, NEG), m_ref[hc])
                m_ref[hc] = m_new
                ps.append(p)
                alphas.append(alpha)
            pv = jnp.dot(vt_ref[0, hh, 0, :, lo:blk], jnp.concatenate(ps, axis=1),
                         preferred_element_type=F32)
            for c in range(2):
                hc = 2 * hh + c
                acc_ref[hc] = alphas[c] * acc_ref[hc] + pv[:, c * blk:(c + 1) * blk]

    _pipelined_key_blocks(i, first, scores, consume)
    lv = lam_ref[...]
    lam = (jnp.exp(jnp.sum(lv[0:1] * lv[1:2], axis=-1, keepdims=True))
           - jnp.exp(jnp.sum(lv[2:3] * lv[3:4], axis=-1, keepdims=True)) + lam_init)
    for hh in heads:
        o = _head_out(acc_ref[2 * hh].T) - lam * _head_out(acc_ref[2 * hh + 1].T)
        o_ref[:, slot(hh)] = (_head_rms(o, g_ref[...]) * (1.0 - lam_init)).astype(BF16)


def _diff_attn(dqk, dvt, lam_vecs, gain, lam_init, batch, seq_len):
    n = dqk.shape[0]
    blk = ATTN_BLOCK
    nq = seq_len // blk
    hps = ATTN_HEADS_PER_STEP
    groups = DA_HEADS // hps
    return pl.pallas_call(
        functools.partial(_diff_attn_kernel, blk=blk, lam_init=lam_init),
        grid=(batch, groups, nq),
        in_specs=[pl.BlockSpec((blk, hps * SLOT), lambda b, h, i: (b * nq + i, h)),
                  _key_spec(seq_len, groups),
                  _value_spec(seq_len),
                  pl.BlockSpec((8, LANES), lambda b, h, i: (0, 0)),
                  pl.BlockSpec((1, SLOT), lambda b, h, i: (0, 0))],
        out_specs=pl.BlockSpec((blk, hps * SLOT), lambda b, h, i: (b * nq + i, h)),
        out_shape=jax.ShapeDtypeStruct((n, HEAD_W), BF16),
        scratch_shapes=[pltpu.VMEM((2 * hps, 1, blk), F32),
                        pltpu.VMEM((2 * hps, SLOT, blk), F32),
                        pltpu.VMEM((2 * hps, blk, blk), F32),
                        pltpu.VMEM((2 * hps, blk, blk), F32)],
        compiler_params=pltpu.CompilerParams(
            dimension_semantics=("arbitrary", "arbitrary", "arbitrary"),
            vmem_limit_bytes=VMEM_LIMIT),
        name="diff_attn",
    )(dqk, dqk, dvt, lam_vecs, gain)


def _hgrn_kernel(q_ref, i_ref, g_ref, f_ref, lb_ref, gain_ref, o_ref,
                 st_ref, hs_ref, *, tl):
    j = pl.program_id(2)

    @pl.when(j == 0)
    def _():
        st_ref[...] = jnp.zeros_like(st_ref)

    c_rows, sub = HG_CHUNK, HG_SUB
    tri = (lax.broadcasted_iota(jnp.int32, (c_rows, c_rows), 1)
           <= lax.broadcasted_iota(jnp.int32, (c_rows, c_rows), 0)).astype(F32)
    row_c = lax.broadcasted_iota(jnp.int32, (c_rows, 1), 0)
    t_sub = lax.broadcasted_iota(jnp.int32, (sub, 1), 0)
    lane_c = lax.broadcasted_iota(jnp.int32, (sub, c_rows), 1)

    t_all = lax.broadcasted_iota(jnp.int32, (c_rows, c_rows), 0)
    s_all = lax.broadcasted_iota(jnp.int32, (c_rows, c_rows), 1)
    level_masks = []
    g = sub
    while g < c_rows:
        level_masks.append(((t_all // g) == (s_all // g) + 1) & ((s_all // g) % 2 == 0))
        g *= 2

    heads = range(HG_PER_STEP)
    cols = [slice(hh * HG_DK, (hh + 1) * HG_DK) for hh in heads]

    def chunk(c, carry):
        r0 = pl.multiple_of(c * c_rows, c_rows)
        rows = pl.ds(r0, c_rows)
        valid = (j * tl + r0 + row_c) >= LPAD
        G, kk, qs, v, st, o_inter = [], [], [], [], [], []
        for hh in heads:
            lb = lb_ref[:, cols[hh]]
            sig = _sigmoid(f_ref[rows, cols[hh]])
            log_f = jnp.where(valid, jnp.log2(jnp.maximum(lb + (1.0 - lb) * sig, TINY)), 0.0)
            kk.append(jnp.where(valid, (1.0 - lb) * (1.0 - sig), 0.0))
            G.append(jnp.dot(tri, log_f, precision=lax.Precision.HIGHEST,
                             preferred_element_type=F32))
            qf = q_ref[rows, cols[hh]].astype(F32)
            qs.append(qf * _sigmoid(qf) * (HG_DK ** -0.5))
            v.append(i_ref[rows, cols[hh]])
            hs_ref[hh] = G[hh] - jnp.log2(kk[hh])
            st.append(st_ref[hh])
            o_inter.append(lax.dot_general(
                (qs[hh] * jnp.exp2(G[hh])).astype(BF16), st[hh].astype(BF16), _NT,
                preferred_element_type=F32))
        a_mat = [jnp.zeros((c_rows, c_rows), F32) for _ in heads]
        g = sub
        for mask in level_masks:
            for hh in heads:
                ref = jnp.concatenate(
                    [jnp.broadcast_to(G[hh][p + g - 1:p + g], (2 * g, HG_DK))
                     for p in range(0, c_rows, 2 * g)], axis=0)
                e = jnp.exp2(-jnp.abs(G[hh] - ref))
                part = lax.dot_general((qs[hh] * e).astype(BF16), (kk[hh] * e).astype(BF16),
                                       _NT, preferred_element_type=F32)
                a_mat[hh] = jnp.where(mask, part, a_mat[hh])
            g *= 2
        a_rows = [[] for _ in heads]
        for b in range(c_rows // sub):
            lo = b * sub
            for hh in heads:
                q_b = qs[hh][lo:lo + sub]
                g_b = G[hh][lo:lo + sub]
                a_blk = a_mat[hh][lo:lo + sub]
                for s in range(lo, lo + sub):
                    y = q_b * jnp.exp2(g_b - hs_ref[hh, s:s + 1, :])
                    a_blk = jnp.where(lane_c == s, jnp.sum(y, axis=-1, keepdims=True), a_blk)
                a_rows[hh].append(jnp.where(lane_c <= t_sub + lo, a_blk, 0.0))
        for hh in heads:
            a_full = jnp.concatenate(a_rows[hh], axis=0).astype(BF16)
            o = o_inter[hh] + jnp.dot(a_full, v[hh], preferred_element_type=F32)
            g_last = G[hh][c_rows - 1:c_rows]
            kd = kk[hh] * jnp.exp2(g_last - G[hh])
            st_ref[hh] = st[hh] * jnp.exp2(g_last) + lax.dot_general(
                v[hh], kd.astype(BF16), _TN, preferred_element_type=F32)
            gate = g_ref[rows, cols[hh]].astype(F32)
            o = _rms(o, gain_ref[...]) * (gate * _sigmoid(gate))
            o_ref[rows, cols[hh]] = o.astype(BF16)
        return carry

    lax.fori_loop(0, tl // c_rows, chunk, 0)


def _hgrn(hqig, hf, lb_row, gain, batch, seq_len):
    n = hqig.shape[0]
    tl = ROW_TILE
    nt = seq_len // tl
    w = HG_PER_STEP * HG_DK
    groups = HG_HEADS // HG_PER_STEP
    col = lambda off: pl.BlockSpec((tl, w), lambda b, h, j: (b * nt + j, off + h))
    return pl.pallas_call(
        functools.partial(_hgrn_kernel, tl=tl),
        grid=(batch, groups, nt),
        in_specs=[col(0), col(groups), col(2 * groups), col(0),
                  pl.BlockSpec((1, w), lambda b, h, j: (0, h)),
                  pl.BlockSpec((1, HG_DK), lambda b, h, j: (0, 0))],
        out_specs=col(0),
        out_shape=jax.ShapeDtypeStruct((n, HG_WIDTH), BF16),
        scratch_shapes=[pltpu.VMEM((HG_PER_STEP, HG_DK, HG_DK), F32),
                        pltpu.VMEM((HG_PER_STEP, HG_CHUNK, HG_DK), F32)],
        compiler_params=pltpu.CompilerParams(
            dimension_semantics=("arbitrary", "arbitrary", "arbitrary")),
        name="hgrn2",
    )(hqig, hqig, hqig, hf, lb_row, gain)


def _out_router_kernel(oa_ref, ob_ref, oc_ref, h_ref, wo_ref, g_ref, wrh_ref, wrl_ref, br_ref,
                       h1_ref, route_ref, cnt_ref, carry_ref):
    @pl.when(pl.program_id(0) == 0)
    def _():
        carry_ref[...] = jnp.zeros_like(carry_ref)

    h1 = (h_ref[...]
          + jnp.dot(oa_ref[...], wo_ref[0:512, :], preferred_element_type=F32)
          + jnp.dot(ob_ref[...], wo_ref[512:1024, :], preferred_element_type=F32)
          + jnp.dot(oc_ref[...], wo_ref[1024:1536, :], preferred_element_type=F32))
    h1_ref[...] = h1
    u = _rms(h1, g_ref[...])
    u_hi = u.astype(BF16)
    u_lo = (u - u_hi.astype(F32)).astype(BF16)
    logits = (jnp.dot(u_hi, wrh_ref[...], preferred_element_type=F32)
              + (jnp.dot(u_hi, wrl_ref[...], preferred_element_type=F32)
                 + jnp.dot(u_lo, wrh_ref[...], preferred_element_type=F32))
              + br_ref[...])
    tm = logits.shape[0]
    lane = lax.broadcasted_iota(jnp.int32, (tm, LANES), 1).astype(F32)
    big = float(LANES)
    is_g = (lane >= N_EXPERTS) & (lane < N_EXPERTS + N_GROUPS)
    gl = jnp.where(is_g, logits, -jnp.inf)
    gmax = jnp.max(gl, axis=-1, keepdims=True)
    gsel = jnp.min(jnp.where(gl == gmax, lane, big), axis=-1, keepdims=True) - N_EXPERTS
    p_g = 1.0 / jnp.sum(jnp.exp(gl - gmax), axis=-1, keepdims=True)
    lo = gsel * EXPERTS_PER_GROUP
    el = jnp.where((lane >= lo) & (lane < lo + EXPERTS_PER_GROUP), logits, -jnp.inf)
    m1 = jnp.max(el, axis=-1, keepdims=True)
    i1 = jnp.min(jnp.where(el == m1, lane, big), axis=-1, keepdims=True)
    el2 = jnp.where(lane == i1, -jnp.inf, el)
    m2 = jnp.max(el2, axis=-1, keepdims=True)
    i2 = jnp.min(jnp.where(el2 == m2, lane, big), axis=-1, keepdims=True)
    r = jnp.exp(m2 - m1)
    gate1 = p_g / (1.0 + r)
    gate2 = gate1 * r
    oh1 = lane == i1
    oh2 = lane == i2
    onehot = jnp.where(oh1 | oh2, 1.0, 0.0)
    tri = (lax.broadcasted_iota(jnp.int32, (tm, tm), 1)
           < lax.broadcasted_iota(jnp.int32, (tm, tm), 0)).astype(BF16)
    before = jnp.dot(tri, onehot.astype(BF16), preferred_element_type=F32) + carry_ref[...]
    rank1 = jnp.sum(jnp.where(oh1, before, 0.0), axis=-1, keepdims=True)
    rank2 = jnp.sum(jnp.where(oh2, before, 0.0), axis=-1, keepdims=True)
    total = carry_ref[...] + jnp.sum(onehot, axis=0, keepdims=True)
    carry_ref[...] = total
    cnt_ref[...] = total
    route = jnp.where(lane == 0, i1, 0.0)
    for idx, val in ((1, i2), (2, gate1), (3, gate2), (4, rank1), (5, rank2)):
        route = jnp.where(lane == idx, val, route)
    route_ref[...] = route


def _out_router(oa, ob, oc, h, w_out, g, w_rt, b_rt):
    n = h.shape[0]
    w_rt_hi = w_rt.astype(BF16)
    w_rt_lo = (w_rt - w_rt_hi.astype(F32)).astype(BF16)
    tm = ROUTER_TILE
    row = lambda w: pl.BlockSpec((tm, w), lambda i: (i, 0))
    const = lambda r, c: pl.BlockSpec((r, c), lambda i: (0, 0))
    return pl.pallas_call(
        _out_router_kernel,
        grid=(n // tm,),
        in_specs=[row(HG_WIDTH), row(HEAD_W), row(HEAD_W), row(D_MODEL),
                  const(HG_WIDTH + 2 * HEAD_W, D_MODEL),
                  const(1, D_MODEL), const(D_MODEL, LANES), const(D_MODEL, LANES),
                  const(1, LANES)],
        out_specs=[row(D_MODEL), row(LANES), const(1, LANES)],
        out_shape=[jax.ShapeDtypeStruct((n, D_MODEL), F32),
                   jax.ShapeDtypeStruct((n, LANES), F32),
                   jax.ShapeDtypeStruct((1, LANES), F32)],
        scratch_shapes=[pltpu.VMEM((1, LANES), F32)],
        compiler_params=pltpu.CompilerParams(dimension_semantics=("arbitrary",)),
        name="out_router",
    )(oa, ob, oc, h, w_out, g, w_rt_hi, w_rt_lo, b_rt)


def _dispatch_kernel(dest_ref, unused_ref, h_ref, g_ref, xs_in_ref, xs_ref, back_ref,
                     u_ref, sem, *, tm, n, tb):
    del xs_in_ref
    i = pl.program_id(0)
    n_tiles = n // tm
    base = (i - 1) * tm
    cur = i % 2

    @pl.when(i == 0)
    def _():
        def fill(s, carry):
            parity = lax.shift_right_logical(s, tb.bit_length() - 1) & 1
            back_ref[s] = 2 * n + parity * tb + (s & (tb - 1))
            return carry

        n_ranges = unused_ref.shape[0] // 2

        def one_range(e, carry):
            return lax.fori_loop(unused_ref[e], unused_ref[n_ranges + e], fill, carry)

        lax.fori_loop(0, n_ranges, one_range, 0)

    def issue(r, carry):
        for k in range(2):
            src = k * n + base + r
            d = dest_ref[src]
            back_ref[d] = src
            pltpu.make_async_copy(u_ref.at[1 - cur, pl.ds(r, 1)], xs_ref.at[pl.ds(d, 1)],
                                  sem).start()
        return carry

    @pl.when(i >= 1)
    def _():
        lax.fori_loop(0, tm, issue, 0, unroll=8)

    @pl.when(i < n_tiles)
    def _():
        _store_token_tiles(u_ref.at[cur], _rms(h_ref[...], g_ref[...]))

    @pl.when(i >= 1)
    def _():
        for _ in range(2):
            pltpu.make_async_copy(u_ref.at[1 - cur], xs_ref.at[pl.ds(0, tm)], sem).wait()


def _dispatch(dest, unused, h1, g, xs_init):
    n = h1.shape[0]
    tm = ROW_TILE
    p_rows = xs_init.shape[0]
    return pl.pallas_call(
        functools.partial(_dispatch_kernel, tm=tm, n=n, tb=MOE_BLOCK),
        grid_spec=pltpu.PrefetchScalarGridSpec(
            num_scalar_prefetch=2,
            grid=(n // tm + 1,),
            in_specs=[pl.BlockSpec((tm, D_MODEL),
                                   lambda i, d, un: (jnp.minimum(i, n // tm - 1), 0)),
                      pl.BlockSpec((1, D_MODEL), lambda i, d, un: (0, 0)),
                      pl.BlockSpec(memory_space=pl.ANY)],
            out_specs=[pl.BlockSpec(memory_space=pl.ANY),
                       pl.BlockSpec(memory_space=pltpu.SMEM)],
            scratch_shapes=[pltpu.VMEM((2, tm) + TOKEN_TILE, F32),
                            pltpu.SemaphoreType.DMA(())]),
        out_shape=[jax.ShapeDtypeStruct(xs_init.shape, xs_init.dtype),
                   jax.ShapeDtypeStruct((p_rows,), jnp.int32)],
        input_output_aliases={4: 0},
        compiler_params=pltpu.CompilerParams(dimension_semantics=("arbitrary",)),
        name="moe_dispatch",
    )(dest, unused, h1, g, xs_init)


def _expert_kernel(be_ref, na_ref, back_ref, x_ref, w1_ref, w3_ref, w2_ref, out_ref,
                   y_ref, w1b_ref, w3b_ref, w2b_ref, sem, *, tb):
    j = pl.program_id(0)
    na = na_ref[0]
    cur = j % 2

    def scatter(block, buf):
        for r in range(tb):
            pltpu.make_async_copy(y_ref.at[buf, pl.ds(r, 1)],
                                  out_ref.at[pl.ds(back_ref[block * tb + r], 1)],
                                  sem.at[buf]).start()

    def compute(buf):
        x = _load_token_tiles(x_ref).astype(BF16)
        a = jnp.dot(x, w1b_ref[...], preferred_element_type=F32)
        b = jnp.dot(x, w3b_ref[...], preferred_element_type=F32)
        act = (a * _sigmoid(a) * b).astype(BF16)
        _store_token_tiles(y_ref.at[buf],
                           jnp.dot(act, w2b_ref[...], preferred_element_type=F32))

    new_expert = (j == 0) | (be_ref[j] != be_ref[jnp.maximum(j - 1, 0)])

    @pl.when((j < na) & new_expert)
    def _():
        w1b_ref[...] = w1_ref[0].astype(BF16)
        w3b_ref[...] = w3_ref[0].astype(BF16)
        w2b_ref[...] = w2_ref[0].astype(BF16)

    @pl.when((j >= 2) & (j < na + 2))
    def _():
        pltpu.make_async_copy(y_ref.at[cur], out_ref.at[pl.ds(0, tb)], sem.at[cur]).wait()

    @pl.when(j == 0)
    def _():
        y_ref[...] = jnp.zeros_like(y_ref)
        first_spare = out_ref.shape[0] - 2 * tb
        spare = [pltpu.make_async_copy(
            y_ref.at[b], out_ref.at[pl.ds(first_spare + b * tb, tb)], sem.at[b])
            for b in range(2)]
        for copy in spare:
            copy.start()
        for copy in spare:
            copy.wait()
        compute(cur)

    @pl.when((j >= 1) & (j < na))
    def _():
        scatter(j - 1, 1 - cur)
        compute(cur)

    @pl.when(j == na)
    def _():
        scatter(j - 1, 1 - cur)


def _experts(blk_expert, n_active, slot_back, xs, w1, w3, w2, layer, n):
    p_rows = xs.shape[0]
    tb = MOE_BLOCK
    last = lambda j, na: jnp.maximum(jnp.minimum(j, na[0] - 1), 0)
    w_spec = lambda r, c: pl.BlockSpec((None, 1, r, c),
                                       lambda j, be, na, back: (layer, be[j], 0, 0))
    return pl.pallas_call(
        functools.partial(_expert_kernel, tb=tb),
        grid_spec=pltpu.PrefetchScalarGridSpec(
            num_scalar_prefetch=3,
            grid=(p_rows // tb + 1,),
            in_specs=[pl.BlockSpec((tb,) + TOKEN_TILE,
                                   lambda j, be, na, back: (last(j, na),) + _TOKEN_ORIGIN),
                      w_spec(D_MODEL, D_EXPERT), w_spec(D_MODEL, D_EXPERT),
                      w_spec(D_EXPERT, D_MODEL)],
            out_specs=pl.BlockSpec(memory_space=pl.ANY),
            scratch_shapes=[pltpu.VMEM((2, tb) + TOKEN_TILE, F32),
                            pltpu.VMEM((D_MODEL, D_EXPERT), BF16),
                            pltpu.VMEM((D_MODEL, D_EXPERT), BF16),
                            pltpu.VMEM((D_EXPERT, D_MODEL), BF16),
                            pltpu.SemaphoreType.DMA((2,))]),
        out_shape=jax.ShapeDtypeStruct((2 * n + 2 * tb,) + TOKEN_TILE, F32),
        compiler_params=pltpu.CompilerParams(dimension_semantics=("arbitrary",),
                                             vmem_limit_bytes=VMEM_LIMIT),
        name="moe_experts",
    )(blk_expert, n_active, slot_back, xs, w1, w3, w2)


def _combine_kernel(h_ref, route_ref, gfin_ref, y0_ref, y1_ref, o_ref, *, final):
    route = route_ref[...]
    out = (h_ref[...] + route[:, 2:3] * _load_token_tiles(y0_ref)
           + route[:, 3:4] * _load_token_tiles(y1_ref))
    if final:
        out = _rms(out, gfin_ref[...])
    o_ref[...] = out


def _combine(h1, route, g_final, ys, batch, seq_len, final):
    n = h1.shape[0]
    tm = ROW_TILE
    tiles_per_seq = seq_len // tm
    skip_tiles = (LPAD + N_META) // tm if final else 0
    out_tiles = tiles_per_seq - skip_tiles
    tile = lambda b, i: b * tiles_per_seq + skip_tiles + i
    in_row = lambda w: pl.BlockSpec((tm, w), lambda b, i: (tile(b, i), 0))
    y_rows = lambda k: pl.BlockSpec((tm,) + TOKEN_TILE,
                                    lambda b, i: (k * (n // tm) + tile(b, i),) + _TOKEN_ORIGIN)
    return pl.pallas_call(
        functools.partial(_combine_kernel, final=final),
        grid=(batch, out_tiles),
        in_specs=[in_row(D_MODEL), in_row(LANES),
                  pl.BlockSpec((1, D_MODEL), lambda b, i: (0, 0)),
                  y_rows(0), y_rows(1)],
        out_specs=pl.BlockSpec((tm, D_MODEL), lambda b, i: (b * out_tiles + i, 0)),
        out_shape=jax.ShapeDtypeStruct((batch * out_tiles * tm, D_MODEL), F32),
        compiler_params=pltpu.CompilerParams(dimension_semantics=("arbitrary", "arbitrary")),
        name="moe_combine",
    )(h1, route, g_final, ys, ys)


def _to_slots(m):
    lead = m.shape[:-1]
    x = m.reshape(lead + (4, 64))
    x = jnp.pad(x, [(0, 0)] * (len(lead) + 1) + [(0, SLOT - 64)])
    return x.reshape(lead + (HEAD_W,))


def _rope_partner(m):
    lead = m.shape[:-1]
    x = m.reshape(lead + (DA_WIDTH // DA_DQK, 2, DA_DQK // 2))
    return x[..., ::-1, :].reshape(lead + (DA_WIDTH,))


def _rope_tables(seq_len):
    pos = (jnp.arange(seq_len) - LPAD).astype(F32)
    inv = ROPE_THETA ** (-jnp.arange(0, DA_DQK, 2, dtype=F32) / DA_DQK)
    ang = pos[:, None] * inv[None, :]
    comps = DA_DV // DA_DQK
    cos = jnp.tile(jnp.cos(ang), (1, 2 * comps))
    sin = jnp.tile(jnp.concatenate([-jnp.sin(ang), jnp.sin(ang)], axis=1), (1, comps))
    q_scale = DA_DQK ** -0.5 * LOG2E
    pad = lambda t: jnp.pad(t, ((0, 0), (0, SLOT - DA_DV)))
    lay = lambda t: jnp.concatenate([pad(t * q_scale), pad(t)], axis=1)
    return lay(cos), lay(sin)


def _in_weights(w):
    hq, hf, hi, hg = (w[:, i * 512:(i + 1) * 512] for i in range(4))
    dq, dk, dv = (w[:, 2048 + i * 256:2048 + (i + 1) * 256] for i in range(3))
    fq, fk, fv = (w[:, 2816 + i * 256:2816 + (i + 1) * 256] for i in range(3))
    ff = w[:, 3584:3588]
    cat = jnp.concatenate(
        [hq, hi, hg, hf,
         _to_slots(dq), _to_slots(dk),
         _to_slots(_rope_partner(dq)), _to_slots(_rope_partner(dk)),
         _to_slots(fq * (FX_DH ** -0.5)), _to_slots(fk),
         ff, jnp.zeros((D_MODEL, LANES - FX_HEADS), w.dtype)],
        axis=1)
    w_vt = jnp.concatenate([_to_slots(dv), _to_slots(fv)], axis=1).T
    return cat.astype(BF16), w_vt.astype(BF16)


def _pad_lanes(v, width=LANES):
    return jnp.zeros((1, width), F32).at[0, :v.shape[0]].set(v.astype(F32))


def kernel(x, meta_tokens, norm_mix, w_in, hgrn_lb, hgrn_norm, diff_lambda, diff_norm,
           fox_bias, fox_norm, w_out, norm_ffn, w_group, b_group, w_router, b_router,
           w1, w3, w2, norm_final):
    batch, seq, d = x.shape
    depth = w_in.shape[0]
    seq_len = LPAD + N_META + seq
    n = batch * seq_len
    pad = jnp.zeros((batch, LPAD, d), x.dtype)
    meta = jnp.broadcast_to(meta_tokens.astype(x.dtype)[None], (batch, N_META, d))
    h = jnp.concatenate([pad, meta, x], axis=1).reshape(n, d)

    s_lb = jax.nn.softmax(hgrn_lb.astype(F32), axis=0)
    lb_all = jnp.cumsum(s_lb, axis=0) - s_lb[0]
    cos_t, sin_t = _rope_tables(seq_len)

    tb = MOE_BLOCK
    n_blocks = (2 * n) // tb + N_EXPERTS
    p_rows = n_blocks * tb

    xs = jnp.zeros((p_rows,) + TOKEN_TILE, F32)
    for layer in range(depth):
        lam_init = 0.8 - 0.6 * math.exp(-0.3 * layer)
        w_cat, w_vt = _in_weights(w_in[layer])
        hqig, hf, dqk, fqk, ff, dvt, fvt = _in_proj(
            h, norm_mix[layer][None, :], w_cat, w_vt, cos_t, sin_t, batch, seq_len)

        o_a = _hgrn(hqig, hf, lb_all[layer][None, :], hgrn_norm[layer][None, :],
                    batch, seq_len)

        lam_vecs = jnp.zeros((8, LANES), F32).at[:4, :DA_DQK].set(diff_lambda[layer].astype(F32))
        o_b = _diff_attn(dqk, dvt, lam_vecs, _pad_lanes(diff_norm[layer]), lam_init,
                         batch, seq_len)

        fqk_aug = _fox_prep(ff.reshape(batch, seq_len, LANES), _pad_lanes(fox_bias[layer]),
                            fqk, batch, seq_len)
        o_c = _fox_attn(fqk_aug, fvt, _pad_lanes(fox_norm[layer]), batch, seq_len)

        wo = w_out[layer]
        slot_rows = lambda m: _to_slots(m.T).T
        wo_cat = jnp.concatenate(
            [wo[:HG_WIDTH], slot_rows(wo[HG_WIDTH:HG_WIDTH + DA_WIDTH]),
             slot_rows(wo[HG_WIDTH + DA_WIDTH:])], axis=0).astype(BF16)
        w_rt = jnp.concatenate(
            [w_router[layer], w_group[layer],
             jnp.zeros((d, LANES - N_EXPERTS - N_GROUPS), F32)], axis=1)
        b_rt = _pad_lanes(jnp.concatenate([b_router[layer], b_group[layer]]))
        h1, route, counts = _out_router(
            o_a, o_b, o_c, h, wo_cat, norm_ffn[layer][None, :], w_rt, b_rt)

        cnt = counts[0, :N_EXPERTS].astype(jnp.int32)
        padded = (cnt + tb - 1) // tb * tb
        p_end = jnp.cumsum(padded)
        p_start = p_end - padded
        ids = route[:, 0:2].astype(jnp.int32)
        ranks = route[:, 4:6].astype(jnp.int32)
        dest = (p_start[ids] + ranks).T.reshape(2 * n)
        blk_start = jnp.arange(n_blocks + 1, dtype=jnp.int32) * tb
        blk_expert = jnp.minimum(
            jnp.sum((p_end[None, :] <= blk_start[:, None]).astype(jnp.int32), axis=1),
            N_EXPERTS - 1)
        n_active = (p_end[-1:] // tb).astype(jnp.int32)

        unused = jnp.concatenate(
            [p_start + cnt, p_end[-1:], p_end, jnp.full((1,), p_rows)]).astype(jnp.int32)
        xs, slot_back = _dispatch(dest, unused, h1, norm_ffn[layer][None, :], xs)
        ys = _experts(blk_expert, n_active, slot_back, xs, w1, w3, w2, layer, n)
        h = _combine(h1, route, norm_final[None, :], ys, batch, seq_len,
                     final=(layer == depth - 1))

    return h.reshape(batch, seq, d)
```

```python
import functools
import math

import jax
import jax.numpy as jnp
from jax import lax
from jax.experimental import pallas as pl
from jax.experimental.pallas import tpu as pltpu

F32 = jnp.float32
BF16 = jnp.bfloat16

D_MODEL = 1024
N_META = 16
HG_HEADS = 4
HG_DK = 128
HG_WIDTH = 512
HG_CHUNK = 64
HG_SUB = 8
HG_PER_STEP = 4
DA_HEADS = 4
DA_DV = 64
DA_DQK = 32
DA_WIDTH = 256
FX_HEADS = 4
FX_DH = 64
FX_WIDTH = 256
N_GROUPS = 4
EXPERTS_PER_GROUP = 8
N_EXPERTS = 32
D_EXPERT = 512
ROPE_THETA = 10000.0
EPS = 1e-6
NEG = -1e30
TINY = 1e-30
LOG2E = 1.4426950408889634

LANES = 128
TOKEN_TILE = (D_MODEL,)
_TOKEN_ORIGIN = (0,) * len(TOKEN_TILE)
SLOT = LANES
HEAD_W = 4 * SLOT
ROW_TILE = 256
PROJ_TILE = 512
ROUTER_TILE = 512
CUMSUM_ROWS = 128
ATTN_BLOCK = 512
ATTN_HEADS_PER_STEP = 4
MOE_BLOCK = 256
LPAD = ATTN_BLOCK - N_META
FIRST_KEYS = LANES
assert ATTN_BLOCK - FIRST_KEYS <= LPAD
ONE_LANE = 64
VMEM_LIMIT = 56 * 1024 * 1024

_C_HQIG = (0, 1536)
_C_HF = (1536, 2048)
_C_DQK = (2048, 3072)
_C_DQKP = (3072, 4096)
_C_FQ = (4096, 4608)
_C_FK = (4608, 5120)
_C_FF = (5120, 5248)
IN_COLS = 5248

_NT = (((1,), (1,)), ((), ()))
_TN = (((0,), (0,)), ((), ()))


def _store_token_tiles(ref, x):
    ref[...] = x


def _load_token_tiles(ref):
    return ref[...]


def _sigmoid(x):
    return 1.0 / (1.0 + jnp.exp(-x))


def _rms(x, g):
    return x * lax.rsqrt(jnp.mean(x * x, axis=-1, keepdims=True) + EPS) * g


def _in_proj_kernel(h_ref, g_ref, w_ref, wvt_ref, cos_ref, sin_ref, fb_ref,
                    hqig_ref, hf_ref, dqk_ref, fqk_ref, dvt_ref, fvt_ref, carry_ref,
                    *, tiles_per_seq):
    u = _rms(h_ref[...], g_ref[...]).astype(BF16)
    tm = u.shape[0]

    def mm(c):
        return jnp.dot(u, w_ref[:, c[0]:c[1]], preferred_element_type=F32)

    hqig_ref[...] = mm(_C_HQIG).astype(BF16)
    hf_ref[...] = mm(_C_HF)
    def per_slot(t_ref):
        return jnp.concatenate([t_ref[:, 0:SLOT]] * 4 + [t_ref[:, SLOT:2 * SLOT]] * 4, axis=1)

    dqk_ref[...] = (mm(_C_DQK) * per_slot(cos_ref)
                    + mm(_C_DQKP) * per_slot(sin_ref)).astype(BF16)

    @pl.when(pl.program_id(0) % tiles_per_seq == 0)
    def _():
        carry_ref[...] = jnp.zeros_like(carry_ref)

    x = mm(_C_FF) + fb_ref[...]
    lf = jnp.minimum(x, 0.0) - jnp.log(1.0 + jnp.exp(-jnp.abs(x)))
    tri = (lax.broadcasted_iota(jnp.int32, (CUMSUM_ROWS, CUMSUM_ROWS), 1)
           <= lax.broadcasted_iota(jnp.int32, (CUMSUM_ROWS, CUMSUM_ROWS), 0)).astype(BF16)
    lf_terms = [t.astype(BF16) for t in _bf16_split3(lf)]
    total = carry_ref[...]
    groups = []
    for r0 in range(0, tm, CUMSUM_ROWS):
        part = sum(jnp.dot(tri, t[r0:r0 + CUMSUM_ROWS], preferred_element_type=F32)
                   for t in lf_terms) + total
        total = part[CUMSUM_ROWS - 1:CUMSUM_ROWS, :]
        groups.append(part)
    cs = jnp.concatenate(groups, axis=0)
    carry_ref[...] = total
    parts = _bf16_split3(cs * LOG2E)
    fq = mm(_C_FQ) * LOG2E
    fk = mm(_C_FK)
    lane = lax.broadcasted_iota(jnp.int32, (tm, SLOT), 1)
    d0 = FX_DH
    for h in range(FX_HEADS):
        q = fq[:, h * SLOT:(h + 1) * SLOT]
        k = fk[:, h * SLOT:(h + 1) * SLOT]
        for i, part in enumerate(parts):
            col = part[:, h:h + 1]
            q = jnp.where(lane == d0 + i, col, q)
            k = jnp.where(lane == d0 + 3 + i, -col, k)
        q = jnp.where((lane >= d0 + 3) & (lane < d0 + 6), 1.0, q)
        k = jnp.where((lane >= d0) & (lane < d0 + 3), 1.0, k)
        fqk_ref[:, h * SLOT:(h + 1) * SLOT] = q.astype(BF16)
        fqk_ref[:, HEAD_W + h * SLOT:HEAD_W + (h + 1) * SLOT] = k.astype(BF16)

    vt = lax.dot_general(wvt_ref[...], u, _NT, preferred_element_type=F32)
    slot_row = lax.broadcasted_iota(jnp.int32, vt.shape, 0) % SLOT
    vt = jnp.where(slot_row == ONE_LANE, 1.0, vt).astype(BF16)
    for h in range(4):
        dvt_ref[0, h, 0] = vt[h * SLOT:(h + 1) * SLOT]
        fvt_ref[0, h, 0] = vt[HEAD_W + h * SLOT:HEAD_W + (h + 1) * SLOT]


def _in_proj(h, g, w_cat, w_vt, cos_t, sin_t, fox_bias_row, batch, seq_len):
    n = h.shape[0]
    tm = PROJ_TILE
    assert tm == ATTN_BLOCK
    nk = seq_len // tm
    row = lambda w: pl.BlockSpec((tm, w), lambda i: (i, 0))
    tab = pl.BlockSpec((tm, 2 * SLOT), lambda i: (i % nk, 0))
    once = lambda r, c: pl.BlockSpec((r, c), lambda i: (0, 0), pipeline_mode=pl.Buffered(1))
    widths = (1536, 512, 2 * HEAD_W, 2 * HEAD_W)
    dtypes = (BF16, F32, BF16, BF16)
    vt_spec = pl.BlockSpec((1, 4, 1, SLOT, tm), lambda i: (i // nk, 0, i % nk, 0, 0))
    vt_shape = jax.ShapeDtypeStruct((batch, 4, nk, SLOT, tm), BF16)
    return pl.pallas_call(
        functools.partial(_in_proj_kernel, tiles_per_seq=nk),
        grid=(n // tm,),
        in_specs=[row(D_MODEL),
                  pl.BlockSpec((1, D_MODEL), lambda i: (0, 0)),
                  once(D_MODEL, IN_COLS), once(2 * HEAD_W, D_MODEL),
                  tab, tab,
                  pl.BlockSpec((1, LANES), lambda i: (0, 0))],
        out_specs=[row(w) for w in widths] + [vt_spec, vt_spec],
        out_shape=[jax.ShapeDtypeStruct((n, w), t) for w, t in zip(widths, dtypes)]
        + [vt_shape, vt_shape],
        scratch_shapes=[pltpu.VMEM((1, LANES), F32)],
        compiler_params=pltpu.CompilerParams(dimension_semantics=("arbitrary",),
                                             vmem_limit_bytes=VMEM_LIMIT),
        name="in_proj",
    )(h, g, w_cat, w_vt, cos_t, sin_t, fox_bias_row)


def _bf16_split3(x):
    hi = x.astype(BF16).astype(F32)
    r = x - hi
    mid = r.astype(BF16).astype(F32)
    return hi, mid, r - mid


def _softmax_block(s, m_prev):
    m_new = jnp.maximum(m_prev, jnp.max(s, axis=0, keepdims=True))
    alpha = jnp.exp2(m_prev - m_new)
    p = jnp.exp2(s - m_new).astype(BF16)
    return p, alpha, m_new


def _block_start(j, blk):
    return j * blk if isinstance(j, int) else pl.multiple_of(j * blk, blk)


def _causal_valid(i, j, blk, first_row=0):
    k_idx = j * blk + first_row + lax.broadcasted_iota(jnp.int32, (blk - first_row, blk), 0)
    q_idx = i * blk + lax.broadcasted_iota(jnp.int32, (blk - first_row, blk), 1)
    return (k_idx <= q_idx) & (k_idx >= LPAD)


def _pipelined_key_blocks(i, first, scores, consume):
    first()

    @pl.when(i >= 1)
    def _():
        scores(1, 0)

    def pair(t, carry):
        scores(2 * t + 2, 1)
        consume(2 * t + 1, 0, False)
        scores(2 * t + 3, 0)
        consume(2 * t + 2, 1, False)
        return carry

    lax.fori_loop(0, lax.shift_right_logical(jnp.maximum(i - 1, 0), 1), pair, 0)
    odd = (i & 1) == 1

    @pl.when(odd)
    def _():
        consume(i, 0, True)

    @pl.when((i >= 2) & jnp.logical_not(odd))
    def _():
        scores(i, 1)
        consume(i - 1, 0, False)
        consume(i, 1, True)


def _head_out(acc):
    lane = lax.broadcasted_iota(jnp.int32, acc.shape, 1)
    return jnp.where(lane < ONE_LANE, acc / acc[:, ONE_LANE:ONE_LANE + 1], 0.0)


def _head_rms(o, g):
    ms = jnp.sum(o * o, axis=-1, keepdims=True) * (1.0 / ONE_LANE)
    return o * lax.rsqrt(ms + EPS) * g


def _fox_attn_kernel(q_ref, k_ref, vt_ref, g_ref, o_ref, m_ref, acc_ref, sa_ref, sb_ref,
                     *, blk):
    i = pl.program_id(2)
    heads = range(ATTN_HEADS_PER_STEP)
    slot = [slice(hh * SLOT, (hh + 1) * SLOT) for hh in heads]
    q = [q_ref[:, slot[hh]] for hh in heads]
    bufs = (sa_ref, sb_ref)
    m_ref[...] = jnp.full(m_ref.shape, NEG, F32)
    acc_ref[...] = jnp.zeros_like(acc_ref)

    def scores(j, buf):
        rows = pl.ds(_block_start(j, blk), blk)
        for hh in heads:
            bufs[buf][hh] = lax.dot_general(k_ref[rows, slot[hh]], q[hh], _NT,
                                            preferred_element_type=F32)

    def consume(j, buf, masked):
        mask = _causal_valid(i, j, blk) if masked else None
        for hh in heads:
            s = bufs[buf][hh]
            if masked:
                s = jnp.where(mask, s, NEG)
            p, alpha, m_new = _softmax_block(s, m_ref[hh])
            acc_ref[hh] = alpha * acc_ref[hh] + jnp.dot(
                vt_ref[0, hh, j], p, preferred_element_type=F32)
            m_ref[hh] = m_new

    def first():
        lo = blk - FIRST_KEYS
        mask = _causal_valid(i, 0, blk, lo)
        for hh in heads:
            s = lax.dot_general(k_ref[lo:blk, slot[hh]], q[hh], _NT,
                                preferred_element_type=F32)
            p, alpha, m_new = _softmax_block(jnp.where(mask, s, NEG), m_ref[hh])
            acc_ref[hh] = alpha * acc_ref[hh] + jnp.dot(
                vt_ref[0, hh, 0, :, lo:blk], p, preferred_element_type=F32)
            m_ref[hh] = m_new

    _pipelined_key_blocks(i, first, scores, consume)
    for hh in heads:
        o_ref[:, slot[hh]] = _head_rms(_head_out(acc_ref[hh].T), g_ref[...]).astype(BF16)


def _key_spec(seq_len, groups):
    return pl.BlockSpec((seq_len, ATTN_HEADS_PER_STEP * SLOT),
                        lambda b, h, i: (b, groups + h), pipeline_mode=pl.Buffered(1))


def _value_spec(seq_len):
    nk = seq_len // ATTN_BLOCK
    return pl.BlockSpec((1, ATTN_HEADS_PER_STEP, nk, SLOT, ATTN_BLOCK),
                        lambda b, h, i: (b, h, 0, 0, 0), pipeline_mode=pl.Buffered(1))


def _fox_attn(fqk, fvt, gain, batch, seq_len):
    n = fqk.shape[0]
    blk = ATTN_BLOCK
    nq = seq_len // blk
    hps = ATTN_HEADS_PER_STEP
    groups = FX_HEADS // hps
    return pl.pallas_call(
        functools.partial(_fox_attn_kernel, blk=blk),
        grid=(batch, groups, nq),
        in_specs=[pl.BlockSpec((blk, hps * SLOT), lambda b, h, i: (b * nq + i, h)),
                  _key_spec(seq_len, groups),
                  _value_spec(seq_len),
                  pl.BlockSpec((1, SLOT), lambda b, h, i: (0, 0))],
        out_specs=pl.BlockSpec((blk, hps * SLOT), lambda b, h, i: (b * nq + i, h)),
        out_shape=jax.ShapeDtypeStruct((n, HEAD_W), BF16),
        scratch_shapes=[pltpu.VMEM((hps, 1, blk), F32), pltpu.VMEM((hps, SLOT, blk), F32),
                        pltpu.VMEM((hps, blk, blk), F32), pltpu.VMEM((hps, blk, blk), F32)],
        compiler_params=pltpu.CompilerParams(
            dimension_semantics=("arbitrary", "arbitrary", "arbitrary"),
            vmem_limit_bytes=VMEM_LIMIT),
        name="fox_attn",
    )(fqk, fqk, fvt, gain)


def _diff_attn_kernel(q_ref, k_ref, vt_ref, lam_ref, g_ref, o_ref, m_ref, acc_ref,
                      sa_ref, sb_ref, *, blk, lam_init):
    i = pl.program_id(2)
    heads = range(ATTN_HEADS_PER_STEP)
    slot = lambda s: slice(s * SLOT, (s + 1) * SLOT)
    lane = lax.broadcasted_iota(jnp.int32, (blk, SLOT), 1)
    q = []
    for hh in heads:
        q_h = q_ref[:, slot(hh)]
        q.append([jnp.where((lane >= c * DA_DQK) & (lane < (c + 1) * DA_DQK), q_h,
                            jnp.zeros_like(q_h)) for c in range(2)])
    bufs = (sa_ref, sb_ref)
    m_ref[...] = jnp.full(m_ref.shape, NEG, F32)
    acc_ref[...] = jnp.zeros_like(acc_ref)

    def scores(j, buf):
        rows = pl.ds(_block_start(j, blk), blk)
        for hh in heads:
            k_h = k_ref[rows, slot(hh)]
            for c in range(2):
                bufs[buf][2 * hh + c] = lax.dot_general(k_h, q[hh][c], _NT,
                                                        preferred_element_type=F32)

    def consume(j, buf, masked):
        mask = _causal_valid(i, j, blk) if masked else None
        for hh in heads:
            ps, alphas = [], []
            for c in range(2):
                hc = 2 * hh + c
                s = bufs[buf][hc]
                if masked:
                    s = jnp.where(mask, s, NEG)
                p, alpha, m_new = _softmax_block(s, m_ref[hc])
                m_ref[hc] = m_new
                ps.append(p)
                alphas.append(alpha)
            pv = jnp.dot(vt_ref[0, hh, j], jnp.concatenate(ps, axis=1),
                         preferred_element_type=F32)
            for c in range(2):
                hc = 2 * hh + c
                acc_ref[hc] = alphas[c] * acc_ref[hc] + pv[:, c * blk:(c + 1) * blk]

    def first():
        lo = blk - FIRST_KEYS
        mask = _causal_valid(i, 0, blk, lo)
        for hh in heads:
            k_h = k_ref[lo:blk, slot(hh)]
            ps, alphas = [], []
            for c in range(2):
                hc = 2 * hh + c
                s = lax.dot_general(k_h, q[hh][c], _NT, preferred_element_type=F32)
                p, alpha, m_new = _softmax_block(jnp.where(mask, s, NEG), m_ref[hc])
                m_ref[hc] = m_new
                ps.append(p)
                alphas.append(alpha)
            pv = jnp.dot(vt_ref[0, hh, 0, :, lo:blk], jnp.concatenate(ps, axis=1),
                         preferred_element_type=F32)
            for c in range(2):
                hc = 2 * hh + c
                acc_ref[hc] = alphas[c] * acc_ref[hc] + pv[:, c * blk:(c + 1) * blk]

    _pipelined_key_blocks(i, first, scores, consume)
    lv = lam_ref[...]
    lam = (jnp.exp(jnp.sum(lv[0:1] * lv[1:2], axis=-1, keepdims=True))
           - jnp.exp(jnp.sum(lv[2:3] * lv[3:4], axis=-1, keepdims=True)) + lam_init)
    for hh in heads:
        o = _head_out(acc_ref[2 * hh].T) - lam * _head_out(acc_ref[2 * hh + 1].T)
        o_ref[:, slot(hh)] = (_head_rms(o, g_ref[...]) * (1.0 - lam_init)).astype(BF16)


def _diff_attn(dqk, dvt, lam_vecs, gain, lam_init, batch, seq_len):
    n = dqk.shape[0]
    blk = ATTN_BLOCK
    nq = seq_len // blk
    hps = ATTN_HEADS_PER_STEP
    groups = DA_HEADS // hps
    return pl.pallas_call(
        functools.partial(_diff_attn_kernel, blk=blk, lam_init=lam_init),
        grid=(batch, groups, nq),
        in_specs=[pl.BlockSpec((blk, hps * SLOT), lambda b, h, i: (b * nq + i, h)),
                  _key_spec(seq_len, groups),
                  _value_spec(seq_len),
                  pl.BlockSpec((8, LANES), lambda b, h, i: (0, 0)),
                  pl.BlockSpec((1, SLOT), lambda b, h, i: (0, 0))],
        out_specs=pl.BlockSpec((blk, hps * SLOT), lambda b, h, i: (b * nq + i, h)),
        out_shape=jax.ShapeDtypeStruct((n, HEAD_W), BF16),
        scratch_shapes=[pltpu.VMEM((2 * hps, 1, blk), F32),
                        pltpu.VMEM((2 * hps, SLOT, blk), F32),
                        pltpu.VMEM((2 * hps, blk, blk), F32),
                        pltpu.VMEM((2 * hps, blk, blk), F32)],
        compiler_params=pltpu.CompilerParams(
            dimension_semantics=("arbitrary", "arbitrary", "arbitrary"),
            vmem_limit_bytes=VMEM_LIMIT),
        name="diff_attn",
    )(dqk, dqk, dvt, lam_vecs, gain)


def _hgrn_kernel(q_ref, i_ref, g_ref, f_ref, lb_ref, gain_ref, o_ref,
                 st_ref, hs_ref, *, tl):
    j = pl.program_id(2)

    @pl.when(j == 0)
    def _():
        st_ref[...] = jnp.zeros_like(st_ref)

    c_rows, sub = HG_CHUNK, HG_SUB
    tri = (lax.broadcasted_iota(jnp.int32, (c_rows, c_rows), 1)
           <= lax.broadcasted_iota(jnp.int32, (c_rows, c_rows), 0)).astype(F32)
    row_c = lax.broadcasted_iota(jnp.int32, (c_rows, 1), 0)
    t_sub = lax.broadcasted_iota(jnp.int32, (sub, 1), 0)
    lane_c = lax.broadcasted_iota(jnp.int32, (sub, c_rows), 1)

    t_all = lax.broadcasted_iota(jnp.int32, (c_rows, c_rows), 0)
    s_all = lax.broadcasted_iota(jnp.int32, (c_rows, c_rows), 1)
    level_masks = []
    g = sub
    while g < c_rows:
        level_masks.append(((t_all // g) == (s_all // g) + 1) & ((s_all // g) % 2 == 0))
        g *= 2

    heads = range(HG_PER_STEP)
    cols = [slice(hh * HG_DK, (hh + 1) * HG_DK) for hh in heads]

    def chunk(c, carry):
        r0 = pl.multiple_of(c * c_rows, c_rows)
        rows = pl.ds(r0, c_rows)
        valid = (j * tl + r0 + row_c) >= LPAD
        G, kk, qs, v, st, o_inter = [], [], [], [], [], []
        for hh in heads:
            lb = lb_ref[:, cols[hh]]
            sig = _sigmoid(f_ref[rows, cols[hh]])
            log_f = jnp.where(valid, jnp.log2(jnp.maximum(lb + (1.0 - lb) * sig, TINY)), 0.0)
            kk.append(jnp.where(valid, (1.0 - lb) * (1.0 - sig), 0.0))
            G.append(jnp.dot(tri, log_f, precision=lax.Precision.HIGHEST,
                             preferred_element_type=F32))
            qf = q_ref[rows, cols[hh]].astype(F32)
            qs.append(qf * _sigmoid(qf) * (HG_DK ** -0.5))
            v.append(i_ref[rows, cols[hh]])
            hs_ref[hh] = G[hh] - jnp.log2(kk[hh])
            st.append(st_ref[hh])
            o_inter.append(lax.dot_general(
                (qs[hh] * jnp.exp2(G[hh])).astype(BF16), st[hh].astype(BF16), _NT,
                preferred_element_type=F32))
        a_mat = [jnp.zeros((c_rows, c_rows), F32) for _ in heads]
        g = sub
        for mask in level_masks:
            for hh in heads:
                ref = jnp.concatenate(
                    [jnp.broadcast_to(G[hh][p + g - 1:p + g], (2 * g, HG_DK))
                     for p in range(0, c_rows, 2 * g)], axis=0)
                e = jnp.exp2(-jnp.abs(G[hh] - ref))
                part = lax.dot_general((qs[hh] * e).astype(BF16), (kk[hh] * e).astype(BF16),
                                       _NT, preferred_element_type=F32)
                a_mat[hh] = jnp.where(mask, part, a_mat[hh])
            g *= 2
        a_rows = [[] for _ in heads]
        for b in range(c_rows // sub):
            lo = b * sub
            for hh in heads:
                q_b = qs[hh][lo:lo + sub]
                g_b = G[hh][lo:lo + sub]
                a_blk = a_mat[hh][lo:lo + sub]
                for s in range(lo, lo + sub):
                    y = q_b * jnp.exp2(g_b - hs_ref[hh, s:s + 1, :])
                    a_blk = jnp.where(lane_c == s, jnp.sum(y, axis=-1, keepdims=True), a_blk)
                a_rows[hh].append(jnp.where(lane_c <= t_sub + lo, a_blk, 0.0))
        for hh in heads:
            a_full = jnp.concatenate(a_rows[hh], axis=0).astype(BF16)
            o = o_inter[hh] + jnp.dot(a_full, v[hh], preferred_element_type=F32)
            g_last = G[hh][c_rows - 1:c_rows]
            kd = kk[hh] * jnp.exp2(g_last - G[hh])
            st_ref[hh] = st[hh] * jnp.exp2(g_last) + lax.dot_general(
                v[hh], kd.astype(BF16), _TN, preferred_element_type=F32)
            gate = g_ref[rows, cols[hh]].astype(F32)
            o = _rms(o, gain_ref[...]) * (gate * _sigmoid(gate))
            o_ref[rows, cols[hh]] = o.astype(BF16)
        return carry

    lax.fori_loop(0, tl // c_rows, chunk, 0)


def _hgrn(hqig, hf, lb_row, gain, batch, seq_len):
    n = hqig.shape[0]
    tl = ROW_TILE
    nt = seq_len // tl
    w = HG_PER_STEP * HG_DK
    groups = HG_HEADS // HG_PER_STEP
    col = lambda off: pl.BlockSpec((tl, w), lambda b, h, j: (b * nt + j, off + h))
    return pl.pallas_call(
        functools.partial(_hgrn_kernel, tl=tl),
        grid=(batch, groups, nt),
        in_specs=[col(0), col(groups), col(2 * groups), col(0),
                  pl.BlockSpec((1, w), lambda b, h, j: (0, h)),
                  pl.BlockSpec((1, HG_DK), lambda b, h, j: (0, 0))],
        out_specs=col(0),
        out_shape=jax.ShapeDtypeStruct((n, HG_WIDTH), BF16),
        scratch_shapes=[pltpu.VMEM((HG_PER_STEP, HG_DK, HG_DK), F32),
                        pltpu.VMEM((HG_PER_STEP, HG_CHUNK, HG_DK), F32)],
        compiler_params=pltpu.CompilerParams(
            dimension_semantics=("arbitrary", "arbitrary", "arbitrary")),
        name="hgrn2",
    )(hqig, hqig, hqig, hf, lb_row, gain)


def _out_router_kernel(oa_ref, ob_ref, oc_ref, h_ref, wo_ref, g_ref, wrh_ref, wrl_ref, br_ref,
                       h1_ref, route_ref, cnt_ref, carry_ref):
    @pl.when(pl.program_id(0) == 0)
    def _():
        carry_ref[...] = jnp.zeros_like(carry_ref)

    h1 = (h_ref[...]
          + jnp.dot(oa_ref[...], wo_ref[0:512, :], preferred_element_type=F32)
          + jnp.dot(ob_ref[...], wo_ref[512:1024, :], preferred_element_type=F32)
          + jnp.dot(oc_ref[...], wo_ref[1024:1536, :], preferred_element_type=F32))
    h1_ref[...] = h1
    u = _rms(h1, g_ref[...])
    u_hi = u.astype(BF16)
    u_lo = (u - u_hi.astype(F32)).astype(BF16)
    logits = (jnp.dot(u_hi, wrh_ref[...], preferred_element_type=F32)
              + (jnp.dot(u_hi, wrl_ref[...], preferred_element_type=F32)
                 + jnp.dot(u_lo, wrh_ref[...], preferred_element_type=F32))
              + br_ref[...])
    tm = logits.shape[0]
    lane = lax.broadcasted_iota(jnp.int32, (tm, LANES), 1).astype(F32)
    big = float(LANES)
    is_g = (lane >= N_EXPERTS) & (lane < N_EXPERTS + N_GROUPS)
    gl = jnp.where(is_g, logits, -jnp.inf)
    gmax = jnp.max(gl, axis=-1, keepdims=True)
    gsel = jnp.min(jnp.where(gl == gmax, lane, big), axis=-1, keepdims=True) - N_EXPERTS
    p_g = 1.0 / jnp.sum(jnp.exp(gl - gmax), axis=-1, keepdims=True)
    lo = gsel * EXPERTS_PER_GROUP
    el = jnp.where((lane >= lo) & (lane < lo + EXPERTS_PER_GROUP), logits, -jnp.inf)
    m1 = jnp.max(el, axis=-1, keepdims=True)
    i1 = jnp.min(jnp.where(el == m1, lane, big), axis=-1, keepdims=True)
    el2 = jnp.where(lane == i1, -jnp.inf, el)
    m2 = jnp.max(el2, axis=-1, keepdims=True)
    i2 = jnp.min(jnp.where(el2 == m2, lane, big), axis=-1, keepdims=True)
    r = jnp.exp(m2 - m1)
    gate1 = p_g / (1.0 + r)
    gate2 = gate1 * r
    oh1 = lane == i1
    oh2 = lane == i2
    onehot = jnp.where(oh1 | oh2, 1.0, 0.0)
    tri = (lax.broadcasted_iota(jnp.int32, (tm, tm), 1)
           < lax.broadcasted_iota(jnp.int32, (tm, tm), 0)).astype(BF16)
    before = jnp.dot(tri, onehot.astype(BF16), preferred_element_type=F32) + carry_ref[...]
    rank1 = jnp.sum(jnp.where(oh1, before, 0.0), axis=-1, keepdims=True)
    rank2 = jnp.sum(jnp.where(oh2, before, 0.0), axis=-1, keepdims=True)
    total = carry_ref[...] + jnp.sum(onehot, axis=0, keepdims=True)
    carry_ref[...] = total
    cnt_ref[...] = total
    route = jnp.where(lane == 0, i1, 0.0)
    for idx, val in ((1, i2), (2, gate1), (3, gate2), (4, rank1), (5, rank2)):
        route = jnp.where(lane == idx, val, route)
    route_ref[...] = route


def _out_router(oa, ob, oc, h, w_out, g, w_rt, b_rt):
    n = h.shape[0]
    w_rt_hi = w_rt.astype(BF16)
    w_rt_lo = (w_rt - w_rt_hi.astype(F32)).astype(BF16)
    tm = ROUTER_TILE
    row = lambda w: pl.BlockSpec((tm, w), lambda i: (i, 0))
    const = lambda r, c: pl.BlockSpec((r, c), lambda i: (0, 0))
    return pl.pallas_call(
        _out_router_kernel,
        grid=(n // tm,),
        in_specs=[row(HG_WIDTH), row(HEAD_W), row(HEAD_W), row(D_MODEL),
                  const(HG_WIDTH + 2 * HEAD_W, D_MODEL),
                  const(1, D_MODEL), const(D_MODEL, LANES), const(D_MODEL, LANES),
                  const(1, LANES)],
        out_specs=[row(D_MODEL), row(LANES), const(1, LANES)],
        out_shape=[jax.ShapeDtypeStruct((n, D_MODEL), F32),
                   jax.ShapeDtypeStruct((n, LANES), F32),
                   jax.ShapeDtypeStruct((1, LANES), F32)],
        scratch_shapes=[pltpu.VMEM((1, LANES), F32)],
        compiler_params=pltpu.CompilerParams(dimension_semantics=("arbitrary",)),
        name="out_router",
    )(oa, ob, oc, h, w_out, g, w_rt_hi, w_rt_lo, b_rt)


def _dispatch_kernel(dest_ref, unused_ref, h_ref, g_ref, xs_in_ref, xs_ref, back_ref,
                     u_ref, sem, *, tm, n, tb):
    del xs_in_ref
    i = pl.program_id(0)
    n_tiles = n // tm
    base = (i - 1) * tm
    cur = i % 2

    @pl.when(i == 0)
    def _():
        def fill(s, carry):
            parity = lax.shift_right_logical(s, tb.bit_length() - 1) & 1
            back_ref[s] = 2 * n + parity * tb + (s & (tb - 1))
            return carry

        n_ranges = unused_ref.shape[0] // 2

        def one_range(e, carry):
            return lax.fori_loop(unused_ref[e], unused_ref[n_ranges + e], fill, carry)

        lax.fori_loop(0, n_ranges, one_range, 0)

    def issue(r, carry):
        for k in range(2):
            src = k * n + base + r
            d = dest_ref[src]
            back_ref[d] = src
            pltpu.make_async_copy(u_ref.at[1 - cur, pl.ds(r, 1)], xs_ref.at[pl.ds(d, 1)],
                                  sem).start()
        return carry

    @pl.when(i >= 1)
    def _():
        lax.fori_loop(0, tm, issue, 0, unroll=8)

    @pl.when(i < n_tiles)
    def _():
        _store_token_tiles(u_ref.at[cur], _rms(h_ref[...], g_ref[...]))

    @pl.when(i >= 1)
    def _():
        for _ in range(2):
            pltpu.make_async_copy(u_ref.at[1 - cur], xs_ref.at[pl.ds(0, tm)], sem).wait()


def _dispatch(dest, unused, h1, g, xs_init):
    n = h1.shape[0]
    tm = ROW_TILE
    p_rows = xs_init.shape[0]
    return pl.pallas_call(
        functools.partial(_dispatch_kernel, tm=tm, n=n, tb=MOE_BLOCK),
        grid_spec=pltpu.PrefetchScalarGridSpec(
            num_scalar_prefetch=2,
            grid=(n // tm + 1,),
            in_specs=[pl.BlockSpec((tm, D_MODEL),
                                   lambda i, d, un: (jnp.minimum(i, n // tm - 1), 0)),
                      pl.BlockSpec((1, D_MODEL), lambda i, d, un: (0, 0)),
                      pl.BlockSpec(memory_space=pl.ANY)],
            out_specs=[pl.BlockSpec(memory_space=pl.ANY),
                       pl.BlockSpec(memory_space=pltpu.SMEM)],
            scratch_shapes=[pltpu.VMEM((2, tm) + TOKEN_TILE, F32),
                            pltpu.SemaphoreType.DMA(())]),
        out_shape=[jax.ShapeDtypeStruct(xs_init.shape, xs_init.dtype),
                   jax.ShapeDtypeStruct((p_rows,), jnp.int32)],
        input_output_aliases={4: 0},
        compiler_params=pltpu.CompilerParams(dimension_semantics=("arbitrary",)),
        name="moe_dispatch",
    )(dest, unused, h1, g, xs_init)


def _expert_kernel(be_ref, na_ref, back_ref, x_ref, w1_ref, w3_ref, w2_ref, out_ref,
                   y_ref, w1b_ref, w3b_ref, w2b_ref, sem, *, tb):
    j = pl.program_id(0)
    na = na_ref[0]
    cur = j % 2

    def scatter(block, buf):
        for r in range(tb):
            pltpu.make_async_copy(y_ref.at[buf, pl.ds(r, 1)],
                                  out_ref.at[pl.ds(back_ref[block * tb + r], 1)],
                                  sem.at[buf]).start()

    def compute(buf):
        x = _load_token_tiles(x_ref).astype(BF16)
        a = jnp.dot(x, w1b_ref[...], preferred_element_type=F32)
        b = jnp.dot(x, w3b_ref[...], preferred_element_type=F32)
        act = (a * _sigmoid(a) * b).astype(BF16)
        _store_token_tiles(y_ref.at[buf],
                           jnp.dot(act, w2b_ref[...], preferred_element_type=F32))

    new_expert = (j == 0) | (be_ref[j] != be_ref[jnp.maximum(j - 1, 0)])

    @pl.when((j < na) & new_expert)
    def _():
        w1b_ref[...] = w1_ref[0].astype(BF16)
        w3b_ref[...] = w3_ref[0].astype(BF16)
        w2b_ref[...] = w2_ref[0].astype(BF16)

    @pl.when((j >= 2) & (j < na + 2))
    def _():
        pltpu.make_async_copy(y_ref.at[cur], out_ref.at[pl.ds(0, tb)], sem.at[cur]).wait()

    @pl.when(j == 0)
    def _():
        y_ref[...] = jnp.zeros_like(y_ref)
        first_spare = out_ref.shape[0] - 2 * tb
        spare = [pltpu.make_async_copy(
            y_ref.at[b], out_ref.at[pl.ds(first_spare + b * tb, tb)], sem.at[b])
            for b in range(2)]
        for copy in spare:
            copy.start()
        for copy in spare:
            copy.wait()
        compute(cur)

    @pl.when((j >= 1) & (j < na))
    def _():
        scatter(j - 1, 1 - cur)
        compute(cur)

    @pl.when(j == na)
    def _():
        scatter(j - 1, 1 - cur)


def _experts(blk_expert, n_active, slot_back, xs, w1, w3, w2, layer, n):
    p_rows = xs.shape[0]
    tb = MOE_BLOCK
    last = lambda j, na: jnp.maximum(jnp.minimum(j, na[0] - 1), 0)
    w_spec = lambda r, c: pl.BlockSpec((None, 1, r, c),
                                       lambda j, be, na, back: (layer, be[j], 0, 0))
    return pl.pallas_call(
        functools.partial(_expert_kernel, tb=tb),
        grid_spec=pltpu.PrefetchScalarGridSpec(
            num_scalar_prefetch=3,
            grid=(p_rows // tb + 1,),
            in_specs=[pl.BlockSpec((tb,) + TOKEN_TILE,
                                   lambda j, be, na, back: (last(j, na),) + _TOKEN_ORIGIN),
                      w_spec(D_MODEL, D_EXPERT), w_spec(D_MODEL, D_EXPERT),
                      w_spec(D_EXPERT, D_MODEL)],
            out_specs=pl.BlockSpec(memory_space=pl.ANY),
            scratch_shapes=[pltpu.VMEM((2, tb) + TOKEN_TILE, F32),
                            pltpu.VMEM((D_MODEL, D_EXPERT), BF16),
                            pltpu.VMEM((D_MODEL, D_EXPERT), BF16),
                            pltpu.VMEM((D_EXPERT, D_MODEL), BF16),
                            pltpu.SemaphoreType.DMA((2,))]),
        out_shape=jax.ShapeDtypeStruct((2 * n + 2 * tb,) + TOKEN_TILE, F32),
        compiler_params=pltpu.CompilerParams(dimension_semantics=("arbitrary",),
                                             vmem_limit_bytes=VMEM_LIMIT),
        name="moe_experts",
    )(blk_expert, n_active, slot_back, xs, w1, w3, w2)


def _combine_kernel(h_ref, route_ref, gfin_ref, y0_ref, y1_ref, o_ref, *, final):
    route = route_ref[...]
    out = (h_ref[...] + route[:, 2:3] * _load_token_tiles(y0_ref)
           + route[:, 3:4] * _load_token_tiles(y1_ref))
    if final:
        out = _rms(out, gfin_ref[...])
    o_ref[...] = out


def _combine(h1, route, g_final, ys, batch, seq_len, final):
    n = h1.shape[0]
    tm = ROW_TILE
    tiles_per_seq = seq_len // tm
    skip_tiles = (LPAD + N_META) // tm if final else 0
    out_tiles = tiles_per_seq - skip_tiles
    tile = lambda b, i: b * tiles_per_seq + skip_tiles + i
    in_row = lambda w: pl.BlockSpec((tm, w), lambda b, i: (tile(b, i), 0))
    y_rows = lambda k: pl.BlockSpec((tm,) + TOKEN_TILE,
                                    lambda b, i: (k * (n // tm) + tile(b, i),) + _TOKEN_ORIGIN)
    return pl.pallas_call(
        functools.partial(_combine_kernel, final=final),
        grid=(batch, out_tiles),
        in_specs=[in_row(D_MODEL), in_row(LANES),
                  pl.BlockSpec((1, D_MODEL), lambda b, i: (0, 0)),
                  y_rows(0), y_rows(1)],
        out_specs=pl.BlockSpec((tm, D_MODEL), lambda b, i: (b * out_tiles + i, 0)),
        out_shape=jax.ShapeDtypeStruct((batch * out_tiles * tm, D_MODEL), F32),
        compiler_params=pltpu.CompilerParams(dimension_semantics=("arbitrary", "arbitrary")),
        name="moe_combine",
    )(h1, route, g_final, ys, ys)


def _to_slots(m):
    lead = m.shape[:-1]
    x = m.reshape(lead + (4, 64))
    x = jnp.pad(x, [(0, 0)] * (len(lead) + 1) + [(0, SLOT - 64)])
    return x.reshape(lead + (HEAD_W,))


def _rope_partner(m):
    lead = m.shape[:-1]
    x = m.reshape(lead + (DA_WIDTH // DA_DQK, 2, DA_DQK // 2))
    return x[..., ::-1, :].reshape(lead + (DA_WIDTH,))


def _rope_tables(seq_len):
    pos = (jnp.arange(seq_len) - LPAD).astype(F32)
    inv = ROPE_THETA ** (-jnp.arange(0, DA_DQK, 2, dtype=F32) / DA_DQK)
    ang = pos[:, None] * inv[None, :]
    comps = DA_DV // DA_DQK
    cos = jnp.tile(jnp.cos(ang), (1, 2 * comps))
    sin = jnp.tile(jnp.concatenate([-jnp.sin(ang), jnp.sin(ang)], axis=1), (1, comps))
    q_scale = DA_DQK ** -0.5 * LOG2E
    pad = lambda t: jnp.pad(t, ((0, 0), (0, SLOT - DA_DV)))
    lay = lambda t: jnp.concatenate([pad(t * q_scale), pad(t)], axis=1)
    return lay(cos), lay(sin)


def _in_weights(w):
    hq, hf, hi, hg = (w[:, i * 512:(i + 1) * 512] for i in range(4))
    dq, dk, dv = (w[:, 2048 + i * 256:2048 + (i + 1) * 256] for i in range(3))
    fq, fk, fv = (w[:, 2816 + i * 256:2816 + (i + 1) * 256] for i in range(3))
    ff = w[:, 3584:3588]
    cat = jnp.concatenate(
        [hq, hi, hg, hf,
         _to_slots(dq), _to_slots(dk),
         _to_slots(_rope_partner(dq)), _to_slots(_rope_partner(dk)),
         _to_slots(fq * (FX_DH ** -0.5)), _to_slots(fk),
         ff, jnp.zeros((D_MODEL, LANES - FX_HEADS), w.dtype)],
        axis=1)
    w_vt = jnp.concatenate([_to_slots(dv), _to_slots(fv)], axis=1).T
    return cat.astype(BF16), w_vt.astype(BF16)


def _pad_lanes(v, width=LANES):
    return jnp.zeros((1, width), F32).at[0, :v.shape[0]].set(v.astype(F32))


def kernel(x, meta_tokens, norm_mix, w_in, hgrn_lb, hgrn_norm, diff_lambda, diff_norm,
           fox_bias, fox_norm, w_out, norm_ffn, w_group, b_group, w_router, b_router,
           w1, w3, w2, norm_final):
    batch, seq, d = x.shape
    depth = w_in.shape[0]
    seq_len = LPAD + N_META + seq
    n = batch * seq_len
    pad = jnp.zeros((batch, LPAD, d), x.dtype)
    meta = jnp.broadcast_to(meta_tokens.astype(x.dtype)[None], (batch, N_META, d))
    h = jnp.concatenate([pad, meta, x], axis=1).reshape(n, d)

    s_lb = jax.nn.softmax(hgrn_lb.astype(F32), axis=0)
    lb_all = jnp.cumsum(s_lb, axis=0) - s_lb[0]
    cos_t, sin_t = _rope_tables(seq_len)

    tb = MOE_BLOCK
    n_blocks = (2 * n) // tb + N_EXPERTS
    p_rows = n_blocks * tb

    xs = jnp.zeros((p_rows,) + TOKEN_TILE, F32)
    for layer in range(depth):
        lam_init = 0.8 - 0.6 * math.exp(-0.3 * layer)
        w_cat, w_vt = _in_weights(w_in[layer])
        hqig, hf, dqk, fqk, dvt, fvt = _in_proj(
            h, norm_mix[layer][None, :], w_cat, w_vt, cos_t, sin_t,
            _pad_lanes(fox_bias[layer]), batch, seq_len)

        o_a = _hgrn(hqig, hf, lb_all[layer][None, :], hgrn_norm[layer][None, :],
                    batch, seq_len)

        lam_vecs = jnp.zeros((8, LANES), F32).at[:4, :DA_DQK].set(diff_lambda[layer].astype(F32))
        o_b = _diff_attn(dqk, dvt, lam_vecs, _pad_lanes(diff_norm[layer]), lam_init,
                         batch, seq_len)

        o_c = _fox_attn(fqk, fvt, _pad_lanes(fox_norm[layer]), batch, seq_len)

        wo = w_out[layer]
        slot_rows = lambda m: _to_slots(m.T).T
        wo_cat = jnp.concatenate(
            [wo[:HG_WIDTH], slot_rows(wo[HG_WIDTH:HG_WIDTH + DA_WIDTH]),
             slot_rows(wo[HG_WIDTH + DA_WIDTH:])], axis=0).astype(BF16)
        w_rt = jnp.concatenate(
            [w_router[layer], w_group[layer],
             jnp.zeros((d, LANES - N_EXPERTS - N_GROUPS), F32)], axis=1)
        b_rt = _pad_lanes(jnp.concatenate([b_router[layer], b_group[layer]]))
        h1, route, counts = _out_router(
            o_a, o_b, o_c, h, wo_cat, norm_ffn[layer][None, :], w_rt, b_rt)

        cnt = counts[0, :N_EXPERTS].astype(jnp.int32)
        padded = (cnt + tb - 1) // tb * tb
        p_end = jnp.cumsum(padded)
        p_start = p_end - padded
        ids = route[:, 0:2].astype(jnp.int32)
        ranks = route[:, 4:6].astype(jnp.int32)
        dest = (p_start[ids] + ranks).T.reshape(2 * n)
        blk_start = jnp.arange(n_blocks + 1, dtype=jnp.int32) * tb
        blk_expert = jnp.minimum(
            jnp.sum((p_end[None, :] <= blk_start[:, None]).astype(jnp.int32), axis=1),
            N_EXPERTS - 1)
        n_active = (p_end[-1:] // tb).astype(jnp.int32)

        unused = jnp.concatenate(
            [p_start + cnt, p_end[-1:], p_end, jnp.full((1,), p_rows)]).astype(jnp.int32)
        xs, slot_back = _dispatch(dest, unused, h1, norm_ffn[layer][None, :], xs)
        ys = _experts(blk_expert, n_active, slot_back, xs, w1, w3, w2, layer, n)
        h = _combine(h1, route, norm_final[None, :], ys, batch, seq_len,
                     final=(layer == depth - 1))

    return h.reshape(batch, seq, d)
```

```python
import functools
import math

import jax
import jax.numpy as jnp
from jax import lax
from jax.experimental import pallas as pl
from jax.experimental.pallas import tpu as pltpu

F32 = jnp.float32
BF16 = jnp.bfloat16

D_MODEL = 1024
N_META = 16
HG_HEADS = 4
HG_DK = 128
HG_WIDTH = 512
HG_CHUNK = 64
HG_SUB = 8
HG_PER_STEP = 4
DA_HEADS = 4
DA_DV = 64
DA_DQK = 32
DA_WIDTH = 256
FX_HEADS = 4
FX_DH = 64
FX_WIDTH = 256
N_GROUPS = 4
EXPERTS_PER_GROUP = 8
N_EXPERTS = 32
D_EXPERT = 512
ROPE_THETA = 10000.0
EPS = 1e-6
NEG = -1e30
TINY = 1e-30
LOG2E = 1.4426950408889634

LANES = 128
TOKEN_TILE = (D_MODEL,)
_TOKEN_ORIGIN = (0,) * len(TOKEN_TILE)
SLOT = LANES
HEAD_W = 4 * SLOT
ROW_TILE = 256
PROJ_TILE = 512
ROUTER_TILE = 512
CUMSUM_ROWS = 128
ATTN_BLOCK = 512
ATTN_HEADS_PER_STEP = 4
MOE_BLOCK = 256
LPAD = ATTN_BLOCK - N_META
FIRST_KEYS = LANES
assert ATTN_BLOCK - FIRST_KEYS <= LPAD
ONE_LANE = 64
VMEM_LIMIT = 56 * 1024 * 1024

_C_HQIG = (0, 1536)
_C_HF = (1536, 2048)
_C_DQK = (2048, 2560)
_C_DQKP = (2560, 3072)
_C_FQK = (3072, 3584)
_C_FF = (3584, 3712)
IN_COLS = 3712

_NT = (((1,), (1,)), ((), ()))
_TN = (((0,), (0,)), ((), ()))


def _store_token_tiles(ref, x):
    ref[...] = x


def _load_token_tiles(ref):
    return ref[...]


def _sigmoid(x):
    return 1.0 / (1.0 + jnp.exp(-x))


def _rms(x, g):
    return x * lax.rsqrt(jnp.mean(x * x, axis=-1, keepdims=True) + EPS) * g


def _in_proj_kernel(h_ref, g_ref, w_ref, wvt_ref, cos_ref, sin_ref, fb_ref,
                    hqig_ref, hf_ref, dqk_ref, fqk_ref, dvt_ref, fvt_ref, carry_ref,
                    *, tiles_per_seq):
    u = _rms(h_ref[...], g_ref[...]).astype(BF16)
    tm = u.shape[0]

    def mm(c):
        return jnp.dot(u, w_ref[:, c[0]:c[1]], preferred_element_type=F32)

    hqig_ref[...] = mm(_C_HQIG).astype(BF16)
    hf_ref[...] = mm(_C_HF)
    half_lane = lax.broadcasted_iota(jnp.int32, (tm, LANES), 1) < DA_DV

    def spread(x):
        tiles = []
        for t in range(x.shape[1] // LANES):
            tile = x[:, t * LANES:(t + 1) * LANES]
            tiles.append(jnp.where(half_lane, tile, 0.0))
            tiles.append(jnp.where(half_lane, pltpu.roll(tile, LANES // 2, axis=1), 0.0))
        return jnp.concatenate(tiles, axis=1)

    def per_tile(t_ref):
        return jnp.concatenate([t_ref[:, 0:LANES]] * 2 + [t_ref[:, LANES:2 * LANES]] * 2, axis=1)

    rot = mm(_C_DQK) * per_tile(cos_ref) + mm(_C_DQKP) * per_tile(sin_ref)
    dqk_ref[:, 0:HEAD_W] = spread(rot[:, 0:DA_WIDTH]).astype(BF16)
    dqk_ref[:, HEAD_W:2 * HEAD_W] = spread(rot[:, DA_WIDTH:2 * DA_WIDTH]).astype(BF16)

    @pl.when(pl.program_id(0) % tiles_per_seq == 0)
    def _():
        carry_ref[...] = jnp.zeros_like(carry_ref)

    x = mm(_C_FF) + fb_ref[...]
    lf = jnp.minimum(x, 0.0) - jnp.log(1.0 + jnp.exp(-jnp.abs(x)))
    tri = (lax.broadcasted_iota(jnp.int32, (CUMSUM_ROWS, CUMSUM_ROWS), 1)
           <= lax.broadcasted_iota(jnp.int32, (CUMSUM_ROWS, CUMSUM_ROWS), 0)).astype(BF16)
    lf_terms = [t.astype(BF16) for t in _bf16_split3(lf)]
    total = carry_ref[...]
    groups = []
    for r0 in range(0, tm, CUMSUM_ROWS):
        part = sum(jnp.dot(tri, t[r0:r0 + CUMSUM_ROWS], preferred_element_type=F32)
                   for t in lf_terms) + total
        total = part[CUMSUM_ROWS - 1:CUMSUM_ROWS, :]
        groups.append(part)
    cs = jnp.concatenate(groups, axis=0)
    carry_ref[...] = total
    parts = _bf16_split3(cs * LOG2E)
    fqk = mm(_C_FQK)
    fq = spread(fqk[:, 0:FX_WIDTH]) * LOG2E
    fk = spread(fqk[:, FX_WIDTH:2 * FX_WIDTH])
    lane = lax.broadcasted_iota(jnp.int32, (tm, SLOT), 1)
    d0 = FX_DH
    for h in range(FX_HEADS):
        q = fq[:, h * SLOT:(h + 1) * SLOT]
        k = fk[:, h * SLOT:(h + 1) * SLOT]
        for i, part in enumerate(parts):
            col = part[:, h:h + 1]
            q = jnp.where(lane == d0 + i, col, q)
            k = jnp.where(lane == d0 + 3 + i, -col, k)
        q = jnp.where((lane >= d0 + 3) & (lane < d0 + 6), 1.0, q)
        k = jnp.where((lane >= d0) & (lane < d0 + 3), 1.0, k)
        fqk_ref[:, h * SLOT:(h + 1) * SLOT] = q.astype(BF16)
        fqk_ref[:, HEAD_W + h * SLOT:HEAD_W + (h + 1) * SLOT] = k.astype(BF16)

    vt = lax.dot_general(wvt_ref[...], u, _NT, preferred_element_type=F32).astype(BF16)
    spare = SLOT - DA_DV
    ones_then_zeros = jnp.where(lax.broadcasted_iota(jnp.int32, (spare, tm), 0) == 0,
                                1.0, 0.0).astype(BF16)
    for h in range(4):
        for ref, first in ((dvt_ref, 0), (fvt_ref, DA_WIDTH)):
            ref[0, h, 0, 0:DA_DV, :] = vt[first + h * DA_DV:first + (h + 1) * DA_DV]
            ref[0, h, 0, DA_DV:SLOT, :] = ones_then_zeros


def _in_proj(h, g, w_cat, w_vt, cos_t, sin_t, fox_bias_row, batch, seq_len):
    n = h.shape[0]
    tm = PROJ_TILE
    assert tm == ATTN_BLOCK
    nk = seq_len // tm
    row = lambda w: pl.BlockSpec((tm, w), lambda i: (i, 0))
    tab = pl.BlockSpec((tm, 2 * SLOT), lambda i: (i % nk, 0))
    once = lambda r, c: pl.BlockSpec((r, c), lambda i: (0, 0), pipeline_mode=pl.Buffered(1))
    widths = (1536, 512, 2 * HEAD_W, 2 * HEAD_W)
    dtypes = (BF16, F32, BF16, BF16)
    vt_spec = pl.BlockSpec((1, 4, 1, SLOT, tm), lambda i: (i // nk, 0, i % nk, 0, 0))
    vt_shape = jax.ShapeDtypeStruct((batch, 4, nk, SLOT, tm), BF16)
    return pl.pallas_call(
        functools.partial(_in_proj_kernel, tiles_per_seq=nk),
        grid=(n // tm,),
        in_specs=[row(D_MODEL),
                  pl.BlockSpec((1, D_MODEL), lambda i: (0, 0)),
                  once(D_MODEL, IN_COLS), once(DA_WIDTH + FX_WIDTH, D_MODEL),
                  tab, tab,
                  pl.BlockSpec((1, LANES), lambda i: (0, 0))],
        out_specs=[row(w) for w in widths] + [vt_spec, vt_spec],
        out_shape=[jax.ShapeDtypeStruct((n, w), t) for w, t in zip(widths, dtypes)]
        + [vt_shape, vt_shape],
        scratch_shapes=[pltpu.VMEM((1, LANES), F32)],
        compiler_params=pltpu.CompilerParams(dimension_semantics=("arbitrary",),
                                             vmem_limit_bytes=VMEM_LIMIT),
        name="in_proj",
    )(h, g, w_cat, w_vt, cos_t, sin_t, fox_bias_row)


def _bf16_split3(x):
    hi = x.astype(BF16).astype(F32)
    r = x - hi
    mid = r.astype(BF16).astype(F32)
    return hi, mid, r - mid


def _softmax_block(s, m_prev):
    m_new = jnp.maximum(m_prev, jnp.max(s, axis=0, keepdims=True))
    alpha = jnp.exp2(m_prev - m_new)
    p = jnp.exp2(s - m_new).astype(BF16)
    return p, alpha, m_new


def _block_start(j, blk):
    return j * blk if isinstance(j, int) else pl.multiple_of(j * blk, blk)


def _causal_valid(i, j, blk, first_row=0):
    k_idx = j * blk + first_row + lax.broadcasted_iota(jnp.int32, (blk - first_row, blk), 0)
    q_idx = i * blk + lax.broadcasted_iota(jnp.int32, (blk - first_row, blk), 1)
    return (k_idx <= q_idx) & (k_idx >= LPAD)


def _pipelined_key_blocks(i, first, scores, consume):
    first()

    @pl.when(i >= 1)
    def _():
        scores(1, 0)

    def pair(t, carry):
        scores(2 * t + 2, 1)
        consume(2 * t + 1, 0, False)
        scores(2 * t + 3, 0)
        consume(2 * t + 2, 1, False)
        return carry

    lax.fori_loop(0, lax.shift_right_logical(jnp.maximum(i - 1, 0), 1), pair, 0)
    odd = (i & 1) == 1

    @pl.when(odd)
    def _():
        consume(i, 0, True)

    @pl.when((i >= 2) & jnp.logical_not(odd))
    def _():
        scores(i, 1)
        consume(i - 1, 0, False)
        consume(i, 1, True)


def _head_out(acc):
    lane = lax.broadcasted_iota(jnp.int32, acc.shape, 1)
    return jnp.where(lane < ONE_LANE, acc / acc[:, ONE_LANE:ONE_LANE + 1], 0.0)


def _head_rms(o, g):
    ms = jnp.sum(o * o, axis=-1, keepdims=True) * (1.0 / ONE_LANE)
    return o * lax.rsqrt(ms + EPS) * g


def _fox_attn_kernel(q_ref, k_ref, vt_ref, g_ref, o_ref, m_ref, acc_ref, sa_ref, sb_ref,
                     *, blk):
    i = pl.program_id(2)
    heads = range(ATTN_HEADS_PER_STEP)
    slot = [slice(hh * SLOT, (hh + 1) * SLOT) for hh in heads]
    q = [q_ref[:, slot[hh]] for hh in heads]
    bufs = (sa_ref, sb_ref)
    m_ref[...] = jnp.full(m_ref.shape, NEG, F32)
    acc_ref[...] = jnp.zeros_like(acc_ref)

    def scores(j, buf):
        rows = pl.ds(_block_start(j, blk), blk)
        for hh in heads:
            bufs[buf][hh] = lax.dot_general(k_ref[rows, slot[hh]], q[hh], _NT,
                                            preferred_element_type=F32)

    def consume(j, buf, masked):
        mask = _causal_valid(i, j, blk) if masked else None
        for hh in heads:
            s = bufs[buf][hh]
            if masked:
                s = jnp.where(mask, s, NEG)
            p, alpha, m_new = _softmax_block(s, m_ref[hh])
            acc_ref[hh] = alpha * acc_ref[hh] + jnp.dot(
                vt_ref[0, hh, j], p, preferred_element_type=F32)
            m_ref[hh] = m_new

    def first():
        lo = blk - FIRST_KEYS
        mask = _causal_valid(i, 0, blk, lo)
        for hh in heads:
            s = lax.dot_general(k_ref[lo:blk, slot[hh]], q[hh], _NT,
                                preferred_element_type=F32)
            p, alpha, m_new = _softmax_block(jnp.where(mask, s, NEG), m_ref[hh])
            acc_ref[hh] = alpha * acc_ref[hh] + jnp.dot(
                vt_ref[0, hh, 0, :, lo:blk], p, preferred_element_type=F32)
            m_ref[hh] = m_new

    _pipelined_key_blocks(i, first, scores, consume)
    for hh in heads:
        o_ref[:, slot[hh]] = _head_rms(_head_out(acc_ref[hh].T), g_ref[...]).astype(BF16)


def _key_spec(seq_len, groups):
    return pl.BlockSpec((seq_len, ATTN_HEADS_PER_STEP * SLOT),
                        lambda b, h, i: (b, groups + h), pipeline_mode=pl.Buffered(1))


def _value_spec(seq_len):
    nk = seq_len // ATTN_BLOCK
    return pl.BlockSpec((1, ATTN_HEADS_PER_STEP, nk, SLOT, ATTN_BLOCK),
                        lambda b, h, i: (b, h, 0, 0, 0), pipeline_mode=pl.Buffered(1))


def _fox_attn(fqk, fvt, gain, batch, seq_len):
    n = fqk.shape[0]
    blk = ATTN_BLOCK
    nq = seq_len // blk
    hps = ATTN_HEADS_PER_STEP
    groups = FX_HEADS // hps
    return pl.pallas_call(
        functools.partial(_fox_attn_kernel, blk=blk),
        grid=(batch, groups, nq),
        in_specs=[pl.BlockSpec((blk, hps * SLOT), lambda b, h, i: (b * nq + i, h)),
                  _key_spec(seq_len, groups),
                  _value_spec(seq_len),
                  pl.BlockSpec((1, SLOT), lambda b, h, i: (0, 0))],
        out_specs=pl.BlockSpec((blk, hps * SLOT), lambda b, h, i: (b * nq + i, h)),
        out_shape=jax.ShapeDtypeStruct((n, HEAD_W), BF16),
        scratch_shapes=[pltpu.VMEM((hps, 1, blk), F32), pltpu.VMEM((hps, SLOT, blk), F32),
                        pltpu.VMEM((hps, blk, blk), F32), pltpu.VMEM((hps, blk, blk), F32)],
        compiler_params=pltpu.CompilerParams(
            dimension_semantics=("arbitrary", "arbitrary", "arbitrary"),
            vmem_limit_bytes=VMEM_LIMIT),
        name="fox_attn",
    )(fqk, fqk, fvt, gain)


def _diff_attn_kernel(q_ref, k_ref, vt_ref, lam_ref, g_ref, o_ref, m_ref, acc_ref,
                      sa_ref, sb_ref, *, blk, lam_init):
    i = pl.program_id(2)
    heads = range(ATTN_HEADS_PER_STEP)
    slot = lambda s: slice(s * SLOT, (s + 1) * SLOT)
    lane = lax.broadcasted_iota(jnp.int32, (blk, SLOT), 1)
    q = []
    for hh in heads:
        q_h = q_ref[:, slot(hh)]
        q.append([jnp.where((lane >= c * DA_DQK) & (lane < (c + 1) * DA_DQK), q_h,
                            jnp.zeros_like(q_h)) for c in range(2)])
    bufs = (sa_ref, sb_ref)
    m_ref[...] = jnp.full(m_ref.shape, NEG, F32)
    acc_ref[...] = jnp.zeros_like(acc_ref)

    def scores(j, buf):
        rows = pl.ds(_block_start(j, blk), blk)
        for hh in heads:
            k_h = k_ref[rows, slot(hh)]
            for c in range(2):
                bufs[buf][2 * hh + c] = lax.dot_general(k_h, q[hh][c], _NT,
                                                        preferred_element_type=F32)

    def consume(j, buf, masked):
        mask = _causal_valid(i, j, blk) if masked else None
        for hh in heads:
            ps, alphas = [], []
            for c in range(2):
                hc = 2 * hh + c
                s = bufs[buf][hc]
                if masked:
                    s = jnp.where(mask, s, NEG)
                p, alpha, m_new = _softmax_block(s, m_ref[hc])
                m_ref[hc] = m_new
                ps.append(p)
                alphas.append(alpha)
            pv = jnp.dot(vt_ref[0, hh, j], jnp.concatenate(ps, axis=1),
                         preferred_element_type=F32)
            for c in range(2):
                hc = 2 * hh + c
                acc_ref[hc] = alphas[c] * acc_ref[hc] + pv[:, c * blk:(c + 1) * blk]

    def first():
        lo = blk - FIRST_KEYS
        mask = _causal_valid(i, 0, blk, lo)
        for hh in heads:
            k_h = k_ref[lo:blk, slot(hh)]
            ps, alphas = [], []
            for c in range(2):
                hc = 2 * hh + c
                s = lax.dot_general(k_h, q[hh][c], _NT, preferred_element_type=F32)
                p, alpha, m_new = _softmax_block(jnp.where(mask, s, NEG), m_ref[hc])
                m_ref[hc] = m_new
                ps.append(p)
                alphas.append(alpha)
            pv = jnp.dot(vt_ref[0, hh, 0, :, lo:blk], jnp.concatenate(ps, axis=1),
                         preferred_element_type=F32)
            for c in range(2):
                hc = 2 * hh + c
                acc_ref[hc] = alphas[c] * acc_ref[hc] + pv[:, c * blk:(c + 1) * blk]

    _pipelined_key_blocks(i, first, scores, consume)
    lv = lam_ref[...]
    lam = (jnp.exp(jnp.sum(lv[0:1] * lv[1:2], axis=-1, keepdims=True))
           - jnp.exp(jnp.sum(lv[2:3] * lv[3:4], axis=-1, keepdims=True)) + lam_init)
    for hh in heads:
        o = _head_out(acc_ref[2 * hh].T) - lam * _head_out(acc_ref[2 * hh + 1].T)
        o_ref[:, slot(hh)] = (_head_rms(o, g_ref[...]) * (1.0 - lam_init)).astype(BF16)


def _diff_attn(dqk, dvt, lam_vecs, gain, lam_init, batch, seq_len):
    n = dqk.shape[0]
    blk = ATTN_BLOCK
    nq = seq_len // blk
    hps = ATTN_HEADS_PER_STEP
    groups = DA_HEADS // hps
    return pl.pallas_call(
        functools.partial(_diff_attn_kernel, blk=blk, lam_init=lam_init),
        grid=(batch, groups, nq),
        in_specs=[pl.BlockSpec((blk, hps * SLOT), lambda b, h, i: (b * nq + i, h)),
                  _key_spec(seq_len, groups),
                  _value_spec(seq_len),
                  pl.BlockSpec((8, LANES), lambda b, h, i: (0, 0)),
                  pl.BlockSpec((1, SLOT), lambda b, h, i: (0, 0))],
        out_specs=pl.BlockSpec((blk, hps * SLOT), lambda b, h, i: (b * nq + i, h)),
        out_shape=jax.ShapeDtypeStruct((n, HEAD_W), BF16),
        scratch_shapes=[pltpu.VMEM((2 * hps, 1, blk), F32),
                        pltpu.VMEM((2 * hps, SLOT, blk), F32),
                        pltpu.VMEM((2 * hps, blk, blk), F32),
                        pltpu.VMEM((2 * hps, blk, blk), F32)],
        compiler_params=pltpu.CompilerParams(
            dimension_semantics=("arbitrary", "arbitrary", "arbitrary"),
            vmem_limit_bytes=VMEM_LIMIT),
        name="diff_attn",
    )(dqk, dqk, dvt, lam_vecs, gain)


def _hgrn_kernel(q_ref, i_ref, g_ref, f_ref, lb_ref, gain_ref, o_ref,
                 st_ref, hs_ref, *, tl):
    j = pl.program_id(2)

    @pl.when(j == 0)
    def _():
        st_ref[...] = jnp.zeros_like(st_ref)

    c_rows, sub = HG_CHUNK, HG_SUB
    tri = (lax.broadcasted_iota(jnp.int32, (c_rows, c_rows), 1)
           <= lax.broadcasted_iota(jnp.int32, (c_rows, c_rows), 0)).astype(BF16)
    row_c = lax.broadcasted_iota(jnp.int32, (c_rows, 1), 0)
    t_sub = lax.broadcasted_iota(jnp.int32, (sub, 1), 0)
    lane_c = lax.broadcasted_iota(jnp.int32, (sub, c_rows), 1)

    t_all = lax.broadcasted_iota(jnp.int32, (c_rows, c_rows), 0)
    s_all = lax.broadcasted_iota(jnp.int32, (c_rows, c_rows), 1)
    level_masks = []
    g = sub
    while g < c_rows:
        level_masks.append(((t_all // g) == (s_all // g) + 1) & ((s_all // g) % 2 == 0))
        g *= 2

    heads = range(HG_PER_STEP)
    cols = [slice(hh * HG_DK, (hh + 1) * HG_DK) for hh in heads]

    def chunk(c, carry):
        r0 = pl.multiple_of(c * c_rows, c_rows)
        rows = pl.ds(r0, c_rows)
        valid = (j * tl + r0 + row_c) >= LPAD
        G, kk, qs, v, st, o_inter = [], [], [], [], [], []
        for hh in heads:
            lb = lb_ref[:, cols[hh]]
            sig = _sigmoid(f_ref[rows, cols[hh]])
            log_f = jnp.where(valid, jnp.log2(jnp.maximum(lb + (1.0 - lb) * sig, TINY)), 0.0)
            kk.append(jnp.where(valid, (1.0 - lb) * (1.0 - sig), 0.0))
            G.append(sum(jnp.dot(tri, t.astype(BF16), preferred_element_type=F32)
                         for t in _bf16_split3(log_f)))
            qf = q_ref[rows, cols[hh]].astype(F32)
            qs.append(qf * _sigmoid(qf) * (HG_DK ** -0.5))
            v.append(i_ref[rows, cols[hh]])
            hs_ref[hh] = G[hh] - jnp.log2(kk[hh])
            st.append(st_ref[hh])
            o_inter.append(lax.dot_general(
                (qs[hh] * jnp.exp2(G[hh])).astype(BF16), st[hh].astype(BF16), _NT,
                preferred_element_type=F32))
        a_mat = [jnp.zeros((c_rows, c_rows), F32) for _ in heads]
        g = sub
        for mask in level_masks:
            for hh in heads:
                ref = jnp.concatenate(
                    [jnp.broadcast_to(G[hh][p + g - 1:p + g], (2 * g, HG_DK))
                     for p in range(0, c_rows, 2 * g)], axis=0)
                e = jnp.exp2(-jnp.abs(G[hh] - ref))
                part = lax.dot_general((qs[hh] * e).astype(BF16), (kk[hh] * e).astype(BF16),
                                       _NT, preferred_element_type=F32)
                a_mat[hh] = jnp.where(mask, part, a_mat[hh])
            g *= 2
        a_rows = [[] for _ in heads]
        for b in range(c_rows // sub):
            lo = b * sub
            for hh in heads:
                q_b = qs[hh][lo:lo + sub]
                g_b = G[hh][lo:lo + sub]
                a_blk = a_mat[hh][lo:lo + sub]
                for s in range(lo, lo + sub):
                    y = q_b * jnp.exp2(g_b - hs_ref[hh, s:s + 1, :])
                    a_blk = jnp.where(lane_c == s, jnp.sum(y, axis=-1, keepdims=True), a_blk)
                a_rows[hh].append(jnp.where(lane_c <= t_sub + lo, a_blk, 0.0))
        for hh in heads:
            a_full = jnp.concatenate(a_rows[hh], axis=0).astype(BF16)
            o = o_inter[hh] + jnp.dot(a_full, v[hh], preferred_element_type=F32)
            g_last = G[hh][c_rows - 1:c_rows]
            kd = kk[hh] * jnp.exp2(g_last - G[hh])
            st_ref[hh] = st[hh] * jnp.exp2(g_last) + lax.dot_general(
                v[hh], kd.astype(BF16), _TN, preferred_element_type=F32)
            gate = g_ref[rows, cols[hh]].astype(F32)
            o = _rms(o, gain_ref[...]) * (gate * _sigmoid(gate))
            o_ref[rows, cols[hh]] = o.astype(BF16)
        return carry

    lax.fori_loop(0, tl // c_rows, chunk, 0)


def _hgrn(hqig, hf, lb_row, gain, batch, seq_len):
    n = hqig.shape[0]
    tl = ROW_TILE
    nt = seq_len // tl
    w = HG_PER_STEP * HG_DK
    groups = HG_HEADS // HG_PER_STEP
    col = lambda off: pl.BlockSpec((tl, w), lambda b, h, j: (b * nt + j, off + h))
    return pl.pallas_call(
        functools.partial(_hgrn_kernel, tl=tl),
        grid=(batch, groups, nt),
        in_specs=[col(0), col(groups), col(2 * groups), col(0),
                  pl.BlockSpec((1, w), lambda b, h, j: (0, h)),
                  pl.BlockSpec((1, HG_DK), lambda b, h, j: (0, 0))],
        out_specs=col(0),
        out_shape=jax.ShapeDtypeStruct((n, HG_WIDTH), BF16),
        scratch_shapes=[pltpu.VMEM((HG_PER_STEP, HG_DK, HG_DK), F32),
                        pltpu.VMEM((HG_PER_STEP, HG_CHUNK, HG_DK), F32)],
        compiler_params=pltpu.CompilerParams(
            dimension_semantics=("arbitrary", "arbitrary", "arbitrary")),
        name="hgrn2",
    )(hqig, hqig, hqig, hf, lb_row, gain)


def _out_router_kernel(oa_ref, ob_ref, oc_ref, h_ref, wo_ref, g_ref, wrh_ref, wrl_ref, br_ref,
                       h1_ref, route_ref, cnt_ref, carry_ref):
    @pl.when(pl.program_id(0) == 0)
    def _():
        carry_ref[...] = jnp.zeros_like(carry_ref)

    h1 = (h_ref[...]
          + jnp.dot(oa_ref[...], wo_ref[0:512, :], preferred_element_type=F32)
          + jnp.dot(ob_ref[...], wo_ref[512:1024, :], preferred_element_type=F32)
          + jnp.dot(oc_ref[...], wo_ref[1024:1536, :], preferred_element_type=F32))
    h1_ref[...] = h1
    u = _rms(h1, g_ref[...])
    u_hi = u.astype(BF16)
    u_lo = (u - u_hi.astype(F32)).astype(BF16)
    logits = (jnp.dot(u_hi, wrh_ref[...], preferred_element_type=F32)
              + (jnp.dot(u_hi, wrl_ref[...], preferred_element_type=F32)
                 + jnp.dot(u_lo, wrh_ref[...], preferred_element_type=F32))
              + br_ref[...])
    tm = logits.shape[0]
    lane = lax.broadcasted_iota(jnp.int32, (tm, LANES), 1).astype(F32)
    big = float(LANES)
    is_g = (lane >= N_EXPERTS) & (lane < N_EXPERTS + N_GROUPS)
    gl = jnp.where(is_g, logits, -jnp.inf)
    gmax = jnp.max(gl, axis=-1, keepdims=True)
    gsel = jnp.min(jnp.where(gl == gmax, lane, big), axis=-1, keepdims=True) - N_EXPERTS
    p_g = 1.0 / jnp.sum(jnp.exp(gl - gmax), axis=-1, keepdims=True)
    lo = gsel * EXPERTS_PER_GROUP
    el = jnp.where((lane >= lo) & (lane < lo + EXPERTS_PER_GROUP), logits, -jnp.inf)
    m1 = jnp.max(el, axis=-1, keepdims=True)
    i1 = jnp.min(jnp.where(el == m1, lane, big), axis=-1, keepdims=True)
    el2 = jnp.where(lane == i1, -jnp.inf, el)
    m2 = jnp.max(el2, axis=-1, keepdims=True)
    i2 = jnp.min(jnp.where(el2 == m2, lane, big), axis=-1, keepdims=True)
    r = jnp.exp(m2 - m1)
    gate1 = p_g / (1.0 + r)
    gate2 = gate1 * r
    oh1 = lane == i1
    oh2 = lane == i2
    onehot = jnp.where(oh1 | oh2, 1.0, 0.0)
    tri = (lax.broadcasted_iota(jnp.int32, (tm, tm), 1)
           < lax.broadcasted_iota(jnp.int32, (tm, tm), 0)).astype(BF16)
    before = jnp.dot(tri, onehot.astype(BF16), preferred_element_type=F32) + carry_ref[...]
    rank1 = jnp.sum(jnp.where(oh1, before, 0.0), axis=-1, keepdims=True)
    rank2 = jnp.sum(jnp.where(oh2, before, 0.0), axis=-1, keepdims=True)
    total = carry_ref[...] + jnp.sum(onehot, axis=0, keepdims=True)
    carry_ref[...] = total
    cnt_ref[...] = total
    route = jnp.where(lane == 0, i1, 0.0)
    for idx, val in ((1, i2), (2, gate1), (3, gate2), (4, rank1), (5, rank2)):
        route = jnp.where(lane == idx, val, route)
    route_ref[...] = route


def _out_router(oa, ob, oc, h, w_out, g, w_rt, b_rt):
    n = h.shape[0]
    w_rt_hi = w_rt.astype(BF16)
    w_rt_lo = (w_rt - w_rt_hi.astype(F32)).astype(BF16)
    tm = ROUTER_TILE
    row = lambda w: pl.BlockSpec((tm, w), lambda i: (i, 0))
    const = lambda r, c: pl.BlockSpec((r, c), lambda i: (0, 0))
    return pl.pallas_call(
        _out_router_kernel,
        grid=(n // tm,),
        in_specs=[row(HG_WIDTH), row(HEAD_W), row(HEAD_W), row(D_MODEL),
                  const(HG_WIDTH + 2 * HEAD_W, D_MODEL),
                  const(1, D_MODEL), const(D_MODEL, LANES), const(D_MODEL, LANES),
                  const(1, LANES)],
        out_specs=[row(D_MODEL), row(LANES), const(1, LANES)],
        out_shape=[jax.ShapeDtypeStruct((n, D_MODEL), F32),
                   jax.ShapeDtypeStruct((n, LANES), F32),
                   jax.ShapeDtypeStruct((1, LANES), F32)],
        scratch_shapes=[pltpu.VMEM((1, LANES), F32)],
        compiler_params=pltpu.CompilerParams(dimension_semantics=("arbitrary",)),
        name="out_router",
    )(oa, ob, oc, h, w_out, g, w_rt_hi, w_rt_lo, b_rt)


def _dispatch_kernel(dest_ref, unused_ref, h_ref, g_ref, xs_in_ref, xs_ref, back_ref,
                     u_ref, sem, *, tm, n, tb):
    del xs_in_ref
    i = pl.program_id(0)
    n_tiles = n // tm
    base = (i - 1) * tm
    cur = i % 2

    @pl.when(i == 0)
    def _():
        def fill(s, carry):
            parity = lax.shift_right_logical(s, tb.bit_length() - 1) & 1
            back_ref[s] = 2 * n + parity * tb + (s & (tb - 1))
            return carry

        n_ranges = unused_ref.shape[0] // 2

        def one_range(e, carry):
            return lax.fori_loop(unused_ref[e], unused_ref[n_ranges + e], fill, carry)

        lax.fori_loop(0, n_ranges, one_range, 0)

    def issue(r, carry):
        for k in range(2):
            src = k * n + base + r
            d = dest_ref[src]
            back_ref[d] = src
            pltpu.make_async_copy(u_ref.at[1 - cur, pl.ds(r, 1)], xs_ref.at[pl.ds(d, 1)],
                                  sem).start()
        return carry

    @pl.when(i >= 1)
    def _():
        lax.fori_loop(0, tm, issue, 0, unroll=8)

    @pl.when(i < n_tiles)
    def _():
        _store_token_tiles(u_ref.at[cur], _rms(h_ref[...], g_ref[...]))

    @pl.when(i >= 1)
    def _():
        for _ in range(2):
            pltpu.make_async_copy(u_ref.at[1 - cur], xs_ref.at[pl.ds(0, tm)], sem).wait()


def _dispatch(dest, unused, h1, g, xs_init):
    n = h1.shape[0]
    tm = ROW_TILE
    p_rows = xs_init.shape[0]
    return pl.pallas_call(
        functools.partial(_dispatch_kernel, tm=tm, n=n, tb=MOE_BLOCK),
        grid_spec=pltpu.PrefetchScalarGridSpec(
            num_scalar_prefetch=2,
            grid=(n // tm + 1,),
            in_specs=[pl.BlockSpec((tm, D_MODEL),
                                   lambda i, d, un: (jnp.minimum(i, n // tm - 1), 0)),
                      pl.BlockSpec((1, D_MODEL), lambda i, d, un: (0, 0)),
                      pl.BlockSpec(memory_space=pl.ANY)],
            out_specs=[pl.BlockSpec(memory_space=pl.ANY),
                       pl.BlockSpec(memory_space=pltpu.SMEM)],
            scratch_shapes=[pltpu.VMEM((2, tm) + TOKEN_TILE, F32),
                            pltpu.SemaphoreType.DMA(())]),
        out_shape=[jax.ShapeDtypeStruct(xs_init.shape, xs_init.dtype),
                   jax.ShapeDtypeStruct((p_rows,), jnp.int32)],
        input_output_aliases={4: 0},
        compiler_params=pltpu.CompilerParams(dimension_semantics=("arbitrary",)),
        name="moe_dispatch",
    )(dest, unused, h1, g, xs_init)


def _expert_kernel(be_ref, na_ref, back_ref, x_ref, w1_ref, w3_ref, w2_ref, out_ref,
                   y_ref, w1b_ref, w3b_ref, w2b_ref, sem, *, tb):
    j = pl.program_id(0)
    na = na_ref[0]
    cur = j % 2

    def scatter(block, buf):
        for r in range(tb):
            pltpu.make_async_copy(y_ref.at[buf, pl.ds(r, 1)],
                                  out_ref.at[pl.ds(back_ref[block * tb + r], 1)],
                                  sem.at[buf]).start()

    def compute(buf):
        x = _load_token_tiles(x_ref).astype(BF16)
        a = jnp.dot(x, w1b_ref[...], preferred_element_type=F32)
        b = jnp.dot(x, w3b_ref[...], preferred_element_type=F32)
        act = (a * _sigmoid(a) * b).astype(BF16)
        _store_token_tiles(y_ref.at[buf],
                           jnp.dot(act, w2b_ref[...], preferred_element_type=F32))

    new_expert = (j == 0) | (be_ref[j] != be_ref[jnp.maximum(j - 1, 0)])

    @pl.when((j < na) & new_expert)
    def _():
        w1b_ref[...] = w1_ref[0].astype(BF16)
        w3b_ref[...] = w3_ref[0].astype(BF16)
        w2b_ref[...] = w2_ref[0].astype(BF16)

    @pl.when((j >= 2) & (j < na + 2))
    def _():
        pltpu.make_async_copy(y_ref.at[cur], out_ref.at[pl.ds(0, tb)], sem.at[cur]).wait()

    @pl.when(j == 0)
    def _():
        y_ref[...] = jnp.zeros_like(y_ref)
        first_spare = out_ref.shape[0] - 2 * tb
        spare = [pltpu.make_async_copy(
            y_ref.at[b], out_ref.at[pl.ds(first_spare + b * tb, tb)], sem.at[b])
            for b in range(2)]
        for copy in spare:
            copy.start()
        for copy in spare:
            copy.wait()
        compute(cur)

    @pl.when((j >= 1) & (j < na))
    def _():
        scatter(j - 1, 1 - cur)
        compute(cur)

    @pl.when(j == na)
    def _():
        scatter(j - 1, 1 - cur)


def _experts(blk_expert, n_active, slot_back, xs, w1, w3, w2, layer, n):
    p_rows = xs.shape[0]
    tb = MOE_BLOCK
    last = lambda j, na: jnp.maximum(jnp.minimum(j, na[0] - 1), 0)
    w_spec = lambda r, c: pl.BlockSpec((None, 1, r, c),
                                       lambda j, be, na, back: (layer, be[j], 0, 0))
    return pl.pallas_call(
        functools.partial(_expert_kernel, tb=tb),
        grid_spec=pltpu.PrefetchScalarGridSpec(
            num_scalar_prefetch=3,
            grid=(p_rows // tb + 1,),
            in_specs=[pl.BlockSpec((tb,) + TOKEN_TILE,
                                   lambda j, be, na, back: (last(j, na),) + _TOKEN_ORIGIN),
                      w_spec(D_MODEL, D_EXPERT), w_spec(D_MODEL, D_EXPERT),
                      w_spec(D_EXPERT, D_MODEL)],
            out_specs=pl.BlockSpec(memory_space=pl.ANY),
            scratch_shapes=[pltpu.VMEM((2, tb) + TOKEN_TILE, F32),
                            pltpu.VMEM((D_MODEL, D_EXPERT), BF16),
                            pltpu.VMEM((D_MODEL, D_EXPERT), BF16),
                            pltpu.VMEM((D_EXPERT, D_MODEL), BF16),
                            pltpu.SemaphoreType.DMA((2,))]),
        out_shape=jax.ShapeDtypeStruct((2 * n + 2 * tb,) + TOKEN_TILE, F32),
        compiler_params=pltpu.CompilerParams(dimension_semantics=("arbitrary",),
                                             vmem_limit_bytes=VMEM_LIMIT),
        name="moe_experts",
    )(blk_expert, n_active, slot_back, xs, w1, w3, w2)


def _combine_kernel(h_ref, route_ref, gfin_ref, y0_ref, y1_ref, o_ref, *, final):
    route = route_ref[...]
    out = (h_ref[...] + route[:, 2:3] * _load_token_tiles(y0_ref)
           + route[:, 3:4] * _load_token_tiles(y1_ref))
    if final:
        out = _rms(out, gfin_ref[...])
    o_ref[...] = out


def _combine(h1, route, g_final, ys, batch, seq_len, final):
    n = h1.shape[0]
    tm = ROW_TILE
    tiles_per_seq = seq_len // tm
    skip_tiles = (LPAD + N_META) // tm if final else 0
    out_tiles = tiles_per_seq - skip_tiles
    tile = lambda b, i: b * tiles_per_seq + skip_tiles + i
    in_row = lambda w: pl.BlockSpec((tm, w), lambda b, i: (tile(b, i), 0))
    y_rows = lambda k: pl.BlockSpec((tm,) + TOKEN_TILE,
                                    lambda b, i: (k * (n // tm) + tile(b, i),) + _TOKEN_ORIGIN)
    return pl.pallas_call(
        functools.partial(_combine_kernel, final=final),
        grid=(batch, out_tiles),
        in_specs=[in_row(D_MODEL), in_row(LANES),
                  pl.BlockSpec((1, D_MODEL), lambda b, i: (0, 0)),
                  y_rows(0), y_rows(1)],
        out_specs=pl.BlockSpec((tm, D_MODEL), lambda b, i: (b * out_tiles + i, 0)),
        out_shape=jax.ShapeDtypeStruct((batch * out_tiles * tm, D_MODEL), F32),
        compiler_params=pltpu.CompilerParams(dimension_semantics=("arbitrary", "arbitrary")),
        name="moe_combine",
    )(h1, route, g_final, ys, ys)


def _to_slots(m):
    lead = m.shape[:-1]
    x = m.reshape(lead + (4, 64))
    x = jnp.pad(x, [(0, 0)] * (len(lead) + 1) + [(0, SLOT - 64)])
    return x.reshape(lead + (HEAD_W,))


def _rope_partner(m):
    lead = m.shape[:-1]
    x = m.reshape(lead + (DA_WIDTH // DA_DQK, 2, DA_DQK // 2))
    return x[..., ::-1, :].reshape(lead + (DA_WIDTH,))


def _rope_tables(seq_len):
    pos = (jnp.arange(seq_len) - LPAD).astype(F32)
    inv = ROPE_THETA ** (-jnp.arange(0, DA_DQK, 2, dtype=F32) / DA_DQK)
    ang = pos[:, None] * inv[None, :]
    comps = DA_DV // DA_DQK
    cos = jnp.tile(jnp.cos(ang), (1, 2 * comps))
    sin = jnp.tile(jnp.concatenate([-jnp.sin(ang), jnp.sin(ang)], axis=1), (1, comps))
    q_scale = DA_DQK ** -0.5 * LOG2E
    two = lambda t: jnp.tile(t, (1, LANES // DA_DV))
    lay = lambda t: jnp.concatenate([two(t * q_scale), two(t)], axis=1)
    return lay(cos), lay(sin)


def _in_weights(w):
    hq, hf, hi, hg = (w[:, i * 512:(i + 1) * 512] for i in range(4))
    dq, dk, dv = (w[:, 2048 + i * 256:2048 + (i + 1) * 256] for i in range(3))
    fq, fk, fv = (w[:, 2816 + i * 256:2816 + (i + 1) * 256] for i in range(3))
    ff = w[:, 3584:3588]
    cat = jnp.concatenate(
        [hq, hi, hg, hf,
         dq, dk, _rope_partner(dq), _rope_partner(dk),
         fq * (FX_DH ** -0.5), fk,
         ff, jnp.zeros((D_MODEL, LANES - FX_HEADS), w.dtype)],
        axis=1)
    w_vt = jnp.concatenate([dv, fv], axis=1).T
    return cat.astype(BF16), w_vt.astype(BF16)


def _pad_lanes(v, width=LANES):
    return jnp.zeros((1, width), F32).at[0, :v.shape[0]].set(v.astype(F32))


def kernel(x, meta_tokens, norm_mix, w_in, hgrn_lb, hgrn_norm, diff_lambda, diff_norm,
           fox_bias, fox_norm, w_out, norm_ffn, w_group, b_group, w_router, b_router,
           w1, w3, w2, norm_final):
    batch, seq, d = x.shape
    depth = w_in.shape[0]
    seq_len = LPAD + N_META + seq
    n = batch * seq_len
    pad = jnp.zeros((batch, LPAD, d), x.dtype)
    meta = jnp.broadcast_to(meta_tokens.astype(x.dtype)[None], (batch, N_META, d))
    h = jnp.concatenate([pad, meta, x], axis=1).reshape(n, d)

    s_lb = jax.nn.softmax(hgrn_lb.astype(F32), axis=0)
    lb_all = jnp.cumsum(s_lb, axis=0) - s_lb[0]
    cos_t, sin_t = _rope_tables(seq_len)

    tb = MOE_BLOCK
    n_blocks = (2 * n) // tb + N_EXPERTS
    p_rows = n_blocks * tb

    xs = jnp.zeros((p_rows,) + TOKEN_TILE, F32)
    for layer in range(depth):
        lam_init = 0.8 - 0.6 * math.exp(-0.3 * layer)
        w_cat, w_vt = _in_weights(w_in[layer])
        hqig, hf, dqk, fqk, dvt, fvt = _in_proj(
            h, norm_mix[layer][None, :], w_cat, w_vt, cos_t, sin_t,
            _pad_lanes(fox_bias[layer]), batch, seq_len)

        o_a = _hgrn(hqig, hf, lb_all[layer][None, :], hgrn_norm[layer][None, :],
                    batch, seq_len)

        lam_vecs = jnp.zeros((8, LANES), F32).at[:4, :DA_DQK].set(diff_lambda[layer].astype(F32))
        o_b = _diff_attn(dqk, dvt, lam_vecs, _pad_lanes(diff_norm[layer]), lam_init,
                         batch, seq_len)

        o_c = _fox_attn(fqk, fvt, _pad_lanes(fox_norm[layer]), batch, seq_len)

        wo = w_out[layer]
        slot_rows = lambda m: _to_slots(m.T).T
        wo_cat = jnp.concatenate(
            [wo[:HG_WIDTH], slot_rows(wo[HG_WIDTH:HG_WIDTH + DA_WIDTH]),
             slot_rows(wo[HG_WIDTH + DA_WIDTH:])], axis=0).astype(BF16)
        w_rt = jnp.concatenate(
            [w_router[layer], w_group[layer],
             jnp.zeros((d, LANES - N_EXPERTS - N_GROUPS), F32)], axis=1)
        b_rt = _pad_lanes(jnp.concatenate([b_router[layer], b_group[layer]]))
        h1, route, counts = _out_router(
            o_a, o_b, o_c, h, wo_cat, norm_ffn[layer][None, :], w_rt, b_rt)

        cnt = counts[0, :N_EXPERTS].astype(jnp.int32)
        padded = (cnt + tb - 1) // tb * tb
        p_end = jnp.cumsum(padded)
        p_start = p_end - padded
        ids = route[:, 0:2].astype(jnp.int32)
        ranks = route[:, 4:6].astype(jnp.int32)
        dest = (p_start[ids] + ranks).T.reshape(2 * n)
        blk_start = jnp.arange(n_blocks + 1, dtype=jnp.int32) * tb
        blk_expert = jnp.minimum(
            jnp.sum((p_end[None, :] <= blk_start[:, None]).astype(jnp.int32), axis=1),
            N_EXPERTS - 1)
        n_active = (p_end[-1:] // tb).astype(jnp.int32)

        unused = jnp.concatenate(
            [p_start + cnt, p_end[-1:], p_end, jnp.full((1,), p_rows)]).astype(jnp.int32)
        xs, slot_back = _dispatch(dest, unused, h1, norm_ffn[layer][None, :], xs)
        ys = _experts(blk_expert, n_active, slot_back, xs, w1, w3, w2, layer, n)
        h = _combine(h1, route, norm_final[None, :], ys, batch, seq_len,
                     final=(layer == depth - 1))

    return h.reshape(batch, seq, d)
```

```python
import functools
import math

import jax
import jax.numpy as jnp
from jax import lax
from jax.experimental import pallas as pl
from jax.experimental.pallas import tpu as pltpu

F32 = jnp.float32
BF16 = jnp.bfloat16

D_MODEL = 1024
N_META = 16
HG_HEADS = 4
HG_DK = 128
HG_WIDTH = 512
HG_CHUNK = 64
HG_SUB = 8
HG_PER_STEP = 4
DA_HEADS = 4
DA_DV = 64
DA_DQK = 32
DA_WIDTH = 256
FX_HEADS = 4
FX_DH = 64
FX_WIDTH = 256
N_GROUPS = 4
EXPERTS_PER_GROUP = 8
N_EXPERTS = 32
D_EXPERT = 512
ROPE_THETA = 10000.0
EPS = 1e-6
NEG = -1e30
TINY = 1e-30
LOG2E = 1.4426950408889634

LANES = 128
TOKEN_TILE = (D_MODEL,)
_TOKEN_ORIGIN = (0,) * len(TOKEN_TILE)
SLOT = LANES
HEAD_W = 4 * SLOT
ROW_TILE = 256
PROJ_TILE = 512
ROUTER_TILE = 512
CUMSUM_ROWS = 128
ATTN_BLOCK = 512
ATTN_HEADS_PER_STEP = 4
MOE_BLOCK = 256
LPAD = ATTN_BLOCK - N_META
FIRST_KEYS = LANES
assert ATTN_BLOCK - FIRST_KEYS <= LPAD
ONE_LANE = 64
VMEM_LIMIT = 56 * 1024 * 1024

_C_HQIG = (0, 1536)
_C_HF = (1536, 2048)
_C_DQK = (2048, 2560)
_C_DQKP = (2560, 3072)
_C_FQK = (3072, 3584)
_C_FF = (3584, 3712)
IN_COLS = 3712

_NT = (((1,), (1,)), ((), ()))
_TN = (((0,), (0,)), ((), ()))


def _store_token_tiles(ref, x):
    ref[...] = x


def _load_token_tiles(ref):
    return ref[...]


def _sigmoid(x):
    return 1.0 / (1.0 + jnp.exp(-x))


def _rms(x, g):
    return x * lax.rsqrt(jnp.mean(x * x, axis=-1, keepdims=True) + EPS) * g


def _in_proj_kernel(h_ref, g_ref, w_ref, wvt_ref, cos_ref, sin_ref, fb_ref, lb_ref,
                    hqig_ref, hlf_ref, hkk_ref, dqk_ref, fqk_ref, dvt_ref, fvt_ref,
                    carry_ref, *, tiles_per_seq):
    u = _rms(h_ref[...], g_ref[...]).astype(BF16)
    tm = u.shape[0]

    def mm(c):
        return jnp.dot(u, w_ref[:, c[0]:c[1]], preferred_element_type=F32)

    hg = mm(_C_HQIG)
    q_raw, g_raw = hg[:, 0:HG_WIDTH], hg[:, 2 * HG_WIDTH:3 * HG_WIDTH]
    hqig_ref[:, 0:HG_WIDTH] = (q_raw * _sigmoid(q_raw) * (HG_DK ** -0.5)).astype(BF16)
    hqig_ref[:, HG_WIDTH:2 * HG_WIDTH] = hg[:, HG_WIDTH:2 * HG_WIDTH].astype(BF16)
    hqig_ref[:, 2 * HG_WIDTH:3 * HG_WIDTH] = (g_raw * _sigmoid(g_raw)).astype(BF16)
    lb = lb_ref[...]
    sig = _sigmoid(mm(_C_HF))
    row = ((pl.program_id(0) % tiles_per_seq) * tm
           + lax.broadcasted_iota(jnp.int32, (tm, 1), 0))
    valid = row >= LPAD
    hlf_ref[...] = jnp.where(
        valid, jnp.log2(jnp.maximum(lb + (1.0 - lb) * sig, TINY)), 0.0)
    hkk_ref[...] = jnp.where(valid, (1.0 - lb) * (1.0 - sig), 0.0)
    half_lane = lax.broadcasted_iota(jnp.int32, (tm, LANES), 1) < DA_DV

    def spread(x):
        tiles = []
        for t in range(x.shape[1] // LANES):
            tile = x[:, t * LANES:(t + 1) * LANES]
            tiles.append(jnp.where(half_lane, tile, 0.0))
            tiles.append(jnp.where(half_lane, pltpu.roll(tile, LANES // 2, axis=1), 0.0))
        return jnp.concatenate(tiles, axis=1)

    def per_tile(t_ref):
        return jnp.concatenate([t_ref[:, 0:LANES]] * 2 + [t_ref[:, LANES:2 * LANES]] * 2, axis=1)

    rot = mm(_C_DQK) * per_tile(cos_ref) + mm(_C_DQKP) * per_tile(sin_ref)
    dqk_ref[:, 0:HEAD_W] = spread(rot[:, 0:DA_WIDTH]).astype(BF16)
    dqk_ref[:, HEAD_W:2 * HEAD_W] = spread(rot[:, DA_WIDTH:2 * DA_WIDTH]).astype(BF16)

    @pl.when(pl.program_id(0) % tiles_per_seq == 0)
    def _():
        carry_ref[...] = jnp.zeros_like(carry_ref)

    x = mm(_C_FF) + fb_ref[...]
    lf = jnp.minimum(x, 0.0) - jnp.log(1.0 + jnp.exp(-jnp.abs(x)))
    tri = (lax.broadcasted_iota(jnp.int32, (CUMSUM_ROWS, CUMSUM_ROWS), 1)
           <= lax.broadcasted_iota(jnp.int32, (CUMSUM_ROWS, CUMSUM_ROWS), 0)).astype(BF16)
    lf_terms = [t.astype(BF16) for t in _bf16_split3(lf)]
    total = carry_ref[...]
    groups = []
    for r0 in range(0, tm, CUMSUM_ROWS):
        part = sum(jnp.dot(tri, t[r0:r0 + CUMSUM_ROWS], preferred_element_type=F32)
                   for t in lf_terms) + total
        total = part[CUMSUM_ROWS - 1:CUMSUM_ROWS, :]
        groups.append(part)
    cs = jnp.concatenate(groups, axis=0)
    carry_ref[...] = total
    parts = _bf16_split3(cs * LOG2E)
    fqk = mm(_C_FQK)
    fq = spread(fqk[:, 0:FX_WIDTH]) * LOG2E
    fk = spread(fqk[:, FX_WIDTH:2 * FX_WIDTH])
    lane = lax.broadcasted_iota(jnp.int32, (tm, SLOT), 1)
    d0 = FX_DH
    for h in range(FX_HEADS):
        q = fq[:, h * SLOT:(h + 1) * SLOT]
        k = fk[:, h * SLOT:(h + 1) * SLOT]
        for i, part in enumerate(parts):
            col = part[:, h:h + 1]
            q = jnp.where(lane == d0 + i, col, q)
            k = jnp.where(lane == d0 + 3 + i, -col, k)
        q = jnp.where((lane >= d0 + 3) & (lane < d0 + 6), 1.0, q)
        k = jnp.where((lane >= d0) & (lane < d0 + 3), 1.0, k)
        fqk_ref[:, h * SLOT:(h + 1) * SLOT] = q.astype(BF16)
        fqk_ref[:, HEAD_W + h * SLOT:HEAD_W + (h + 1) * SLOT] = k.astype(BF16)

    vt = lax.dot_general(wvt_ref[...], u, _NT, preferred_element_type=F32).astype(BF16)
    spare = SLOT - DA_DV
    ones_then_zeros = jnp.where(lax.broadcasted_iota(jnp.int32, (spare, tm), 0) == 0,
                                1.0, 0.0).astype(BF16)
    for h in range(4):
        for ref, first in ((dvt_ref, 0), (fvt_ref, DA_WIDTH)):
            ref[0, h, 0, 0:DA_DV, :] = vt[first + h * DA_DV:first + (h + 1) * DA_DV]
            ref[0, h, 0, DA_DV:SLOT, :] = ones_then_zeros


def _in_proj(h, g, w_cat, w_vt, cos_t, sin_t, fox_bias_row, lb_row, batch, seq_len):
    n = h.shape[0]
    tm = PROJ_TILE
    assert tm == ATTN_BLOCK
    nk = seq_len // tm
    row = lambda w: pl.BlockSpec((tm, w), lambda i: (i, 0))
    tab = pl.BlockSpec((tm, 2 * SLOT), lambda i: (i % nk, 0))
    once = lambda r, c: pl.BlockSpec((r, c), lambda i: (0, 0), pipeline_mode=pl.Buffered(1))
    widths = (3 * HG_WIDTH, HG_WIDTH, HG_WIDTH, 2 * HEAD_W, 2 * HEAD_W)
    dtypes = (BF16, F32, F32, BF16, BF16)
    vt_spec = pl.BlockSpec((1, 4, 1, SLOT, tm), lambda i: (i // nk, 0, i % nk, 0, 0))
    vt_shape = jax.ShapeDtypeStruct((batch, 4, nk, SLOT, tm), BF16)
    return pl.pallas_call(
        functools.partial(_in_proj_kernel, tiles_per_seq=nk),
        grid=(n // tm,),
        in_specs=[row(D_MODEL),
                  pl.BlockSpec((1, D_MODEL), lambda i: (0, 0)),
                  once(D_MODEL, IN_COLS), once(DA_WIDTH + FX_WIDTH, D_MODEL),
                  tab, tab,
                  pl.BlockSpec((1, LANES), lambda i: (0, 0)),
                  pl.BlockSpec((1, HG_WIDTH), lambda i: (0, 0))],
        out_specs=[row(w) for w in widths] + [vt_spec, vt_spec],
        out_shape=[jax.ShapeDtypeStruct((n, w), t) for w, t in zip(widths, dtypes)]
        + [vt_shape, vt_shape],
        scratch_shapes=[pltpu.VMEM((1, LANES), F32)],
        compiler_params=pltpu.CompilerParams(dimension_semantics=("arbitrary",),
                                             vmem_limit_bytes=VMEM_LIMIT),
        name="in_proj",
    )(h, g, w_cat, w_vt, cos_t, sin_t, fox_bias_row, lb_row)


def _bf16_split3(x):
    hi = x.astype(BF16).astype(F32)
    r = x - hi
    mid = r.astype(BF16).astype(F32)
    return hi, mid, r - mid


def _softmax_block(s, m_prev):
    m_new = jnp.maximum(m_prev, jnp.max(s, axis=0, keepdims=True))
    alpha = jnp.exp2(m_prev - m_new)
    p = jnp.exp2(s - m_new).astype(BF16)
    return p, alpha, m_new


def _block_start(j, blk):
    return j * blk if isinstance(j, int) else pl.multiple_of(j * blk, blk)


def _causal_valid(i, j, blk, first_row=0):
    k_idx = j * blk + first_row + lax.broadcasted_iota(jnp.int32, (blk - first_row, blk), 0)
    q_idx = i * blk + lax.broadcasted_iota(jnp.int32, (blk - first_row, blk), 1)
    return (k_idx <= q_idx) & (k_idx >= LPAD)


def _pipelined_key_blocks(i, first, scores, consume):
    first()

    @pl.when(i >= 1)
    def _():
        scores(1, 0)

    def pair(t, carry):
        scores(2 * t + 2, 1)
        consume(2 * t + 1, 0, False)
        scores(2 * t + 3, 0)
        consume(2 * t + 2, 1, False)
        return carry

    lax.fori_loop(0, lax.shift_right_logical(jnp.maximum(i - 1, 0), 1), pair, 0)
    odd = (i & 1) == 1

    @pl.when(odd)
    def _():
        consume(i, 0, True)

    @pl.when((i >= 2) & jnp.logical_not(odd))
    def _():
        scores(i, 1)
        consume(i - 1, 0, False)
        consume(i, 1, True)


def _head_out(acc):
    lane = lax.broadcasted_iota(jnp.int32, acc.shape, 1)
    return jnp.where(lane < ONE_LANE, acc / acc[:, ONE_LANE:ONE_LANE + 1], 0.0)


def _store_head_pairs(o_ref, outs):
    for p in range(len(outs) // 2):
        pair = outs[2 * p] + pltpu.roll(outs[2 * p + 1], ONE_LANE, axis=1)
        o_ref[:, p * LANES:(p + 1) * LANES] = pair.astype(BF16)


def _head_rms(o, g):
    ms = jnp.sum(o * o, axis=-1, keepdims=True) * (1.0 / ONE_LANE)
    return o * lax.rsqrt(ms + EPS) * g


def _fox_attn_kernel(q_ref, k_ref, vt_ref, g_ref, o_ref, m_ref, acc_ref, sa_ref, sb_ref,
                     *, blk):
    i = pl.program_id(2)
    heads = range(ATTN_HEADS_PER_STEP)
    slot = [slice(hh * SLOT, (hh + 1) * SLOT) for hh in heads]
    q = [q_ref[:, slot[hh]] for hh in heads]
    bufs = (sa_ref, sb_ref)
    m_ref[...] = jnp.full(m_ref.shape, NEG, F32)
    acc_ref[...] = jnp.zeros_like(acc_ref)

    def scores(j, buf):
        rows = pl.ds(_block_start(j, blk), blk)
        for hh in heads:
            bufs[buf][hh] = lax.dot_general(k_ref[rows, slot[hh]], q[hh], _NT,
                                            preferred_element_type=F32)

    def consume(j, buf, masked):
        mask = _causal_valid(i, j, blk) if masked else None
        for hh in heads:
            s = bufs[buf][hh]
            if masked:
                s = jnp.where(mask, s, NEG)
            p, alpha, m_new = _softmax_block(s, m_ref[hh])
            acc_ref[hh] = alpha * acc_ref[hh] + jnp.dot(
                vt_ref[0, hh, j], p, preferred_element_type=F32)
            m_ref[hh] = m_new

    def first():
        lo = blk - FIRST_KEYS
        mask = _causal_valid(i, 0, blk, lo)
        for hh in heads:
            s = lax.dot_general(k_ref[lo:blk, slot[hh]], q[hh], _NT,
                                preferred_element_type=F32)
            p, alpha, m_new = _softmax_block(jnp.where(mask, s, NEG), m_ref[hh])
            acc_ref[hh] = alpha * acc_ref[hh] + jnp.dot(
                vt_ref[0, hh, 0, :, lo:blk], p, preferred_element_type=F32)
            m_ref[hh] = m_new

    _pipelined_key_blocks(i, first, scores, consume)
    _store_head_pairs(
        o_ref, [_head_rms(_head_out(acc_ref[hh].T), g_ref[...]) for hh in heads])


def _key_spec(seq_len, groups):
    return pl.BlockSpec((seq_len, ATTN_HEADS_PER_STEP * SLOT),
                        lambda b, h, i: (b, groups + h), pipeline_mode=pl.Buffered(1))


def _value_spec(seq_len):
    nk = seq_len // ATTN_BLOCK
    return pl.BlockSpec((1, ATTN_HEADS_PER_STEP, nk, SLOT, ATTN_BLOCK),
                        lambda b, h, i: (b, h, 0, 0, 0), pipeline_mode=pl.Buffered(1))


def _fox_attn(fqk, fvt, gain, batch, seq_len):
    n = fqk.shape[0]
    blk = ATTN_BLOCK
    nq = seq_len // blk
    hps = ATTN_HEADS_PER_STEP
    groups = FX_HEADS // hps
    return pl.pallas_call(
        functools.partial(_fox_attn_kernel, blk=blk),
        grid=(batch, groups, nq),
        in_specs=[pl.BlockSpec((blk, hps * SLOT), lambda b, h, i: (b * nq + i, h)),
                  _key_spec(seq_len, groups),
                  _value_spec(seq_len),
                  pl.BlockSpec((1, SLOT), lambda b, h, i: (0, 0))],
        out_specs=pl.BlockSpec((blk, hps * ONE_LANE), lambda b, h, i: (b * nq + i, h)),
        out_shape=jax.ShapeDtypeStruct((n, 4 * ONE_LANE), BF16),
        scratch_shapes=[pltpu.VMEM((hps, 1, blk), F32), pltpu.VMEM((hps, SLOT, blk), F32),
                        pltpu.VMEM((hps, blk, blk), F32), pltpu.VMEM((hps, blk, blk), F32)],
        compiler_params=pltpu.CompilerParams(
            dimension_semantics=("arbitrary", "arbitrary", "arbitrary"),
            vmem_limit_bytes=VMEM_LIMIT),
        name="fox_attn",
    )(fqk, fqk, fvt, gain)


def _diff_attn_kernel(q_ref, k_ref, vt_ref, lam_ref, g_ref, o_ref, m_ref, acc_ref,
                      sa_ref, sb_ref, *, blk, lam_init):
    i = pl.program_id(2)
    heads = range(ATTN_HEADS_PER_STEP)
    slot = lambda s: slice(s * SLOT, (s + 1) * SLOT)
    lane = lax.broadcasted_iota(jnp.int32, (blk, SLOT), 1)
    q = []
    for hh in heads:
        q_h = q_ref[:, slot(hh)]
        q.append([jnp.where((lane >= c * DA_DQK) & (lane < (c + 1) * DA_DQK), q_h,
                            jnp.zeros_like(q_h)) for c in range(2)])
    bufs = (sa_ref, sb_ref)
    m_ref[...] = jnp.full(m_ref.shape, NEG, F32)
    acc_ref[...] = jnp.zeros_like(acc_ref)

    def scores(j, buf):
        rows = pl.ds(_block_start(j, blk), blk)
        for hh in heads:
            k_h = k_ref[rows, slot(hh)]
            for c in range(2):
                bufs[buf][2 * hh + c] = lax.dot_general(k_h, q[hh][c], _NT,
                                                        preferred_element_type=F32)

    def consume(j, buf, masked):
        mask = _causal_valid(i, j, blk) if masked else None
        for hh in heads:
            ps, alphas = [], []
            for c in range(2):
                hc = 2 * hh + c
                s = bufs[buf][hc]
                if masked:
                    s = jnp.where(mask, s, NEG)
                p, alpha, m_new = _softmax_block(s, m_ref[hc])
                m_ref[hc] = m_new
                ps.append(p)
                alphas.append(alpha)
            pv = jnp.dot(vt_ref[0, hh, j], jnp.concatenate(ps, axis=1),
                         preferred_element_type=F32)
            for c in range(2):
                hc = 2 * hh + c
                acc_ref[hc] = alphas[c] * acc_ref[hc] + pv[:, c * blk:(c + 1) * blk]

    def first():
        lo = blk - FIRST_KEYS
        mask = _causal_valid(i, 0, blk, lo)
        for hh in heads:
            k_h = k_ref[lo:blk, slot(hh)]
            ps, alphas = [], []
            for c in range(2):
                hc = 2 * hh + c
                s = lax.dot_general(k_h, q[hh][c], _NT, preferred_element_type=F32)
                p, alpha, m_new = _softmax_block(jnp.where(mask, s, NEG), m_ref[hc])
                m_ref[hc] = m_new
                ps.append(p)
                alphas.append(alpha)
            pv = jnp.dot(vt_ref[0, hh, 0, :, lo:blk], jnp.concatenate(ps, axis=1),
                         preferred_element_type=F32)
            for c in range(2):
                hc = 2 * hh + c
                acc_ref[hc] = alphas[c] * acc_ref[hc] + pv[:, c * blk:(c + 1) * blk]

    _pipelined_key_blocks(i, first, scores, consume)
    lv = lam_ref[...]
    lam = (jnp.exp(jnp.sum(lv[0:1] * lv[1:2], axis=-1, keepdims=True))
           - jnp.exp(jnp.sum(lv[2:3] * lv[3:4], axis=-1, keepdims=True)) + lam_init)
    outs = []
    for hh in heads:
        o = _head_out(acc_ref[2 * hh].T) - lam * _head_out(acc_ref[2 * hh + 1].T)
        outs.append(_head_rms(o, g_ref[...]) * (1.0 - lam_init))
    _store_head_pairs(o_ref, outs)


def _diff_attn(dqk, dvt, lam_vecs, gain, lam_init, batch, seq_len):
    n = dqk.shape[0]
    blk = ATTN_BLOCK
    nq = seq_len // blk
    hps = ATTN_HEADS_PER_STEP
    groups = DA_HEADS // hps
    return pl.pallas_call(
        functools.partial(_diff_attn_kernel, blk=blk, lam_init=lam_init),
        grid=(batch, groups, nq),
        in_specs=[pl.BlockSpec((blk, hps * SLOT), lambda b, h, i: (b * nq + i, h)),
                  _key_spec(seq_len, groups),
                  _value_spec(seq_len),
                  pl.BlockSpec((8, LANES), lambda b, h, i: (0, 0)),
                  pl.BlockSpec((1, SLOT), lambda b, h, i: (0, 0))],
        out_specs=pl.BlockSpec((blk, hps * ONE_LANE), lambda b, h, i: (b * nq + i, h)),
        out_shape=jax.ShapeDtypeStruct((n, 4 * ONE_LANE), BF16),
        scratch_shapes=[pltpu.VMEM((2 * hps, 1, blk), F32),
                        pltpu.VMEM((2 * hps, SLOT, blk), F32),
                        pltpu.VMEM((2 * hps, blk, blk), F32),
                        pltpu.VMEM((2 * hps, blk, blk), F32)],
        compiler_params=pltpu.CompilerParams(
            dimension_semantics=("arbitrary", "arbitrary", "arbitrary"),
            vmem_limit_bytes=VMEM_LIMIT),
        name="diff_attn",
    )(dqk, dqk, dvt, lam_vecs, gain)


def _hgrn_kernel(q_ref, i_ref, g_ref, lf_ref, kk_ref, gain_ref, o_ref,
                 st_ref, hs_ref, *, tl):
    @pl.when(pl.program_id(2) == 0)
    def _():
        st_ref[...] = jnp.zeros_like(st_ref)

    c_rows, sub = HG_CHUNK, HG_SUB
    tri = (lax.broadcasted_iota(jnp.int32, (c_rows, c_rows), 1)
           <= lax.broadcasted_iota(jnp.int32, (c_rows, c_rows), 0)).astype(BF16)
    t_sub = lax.broadcasted_iota(jnp.int32, (sub, 1), 0)
    lane_c = lax.broadcasted_iota(jnp.int32, (sub, c_rows), 1)

    t_all = lax.broadcasted_iota(jnp.int32, (c_rows, c_rows), 0)
    s_all = lax.broadcasted_iota(jnp.int32, (c_rows, c_rows), 1)
    level_masks = []
    g = sub
    while g < c_rows:
        level_masks.append(((t_all // g) == (s_all // g) + 1) & ((s_all // g) % 2 == 0))
        g *= 2

    heads = range(HG_PER_STEP)
    cols = [slice(hh * HG_DK, (hh + 1) * HG_DK) for hh in heads]

    def chunk(c, carry):
        r0 = pl.multiple_of(c * c_rows, c_rows)
        rows = pl.ds(r0, c_rows)
        G, kk, qs, v, st, o_inter = [], [], [], [], [], []
        for hh in heads:
            kk.append(kk_ref[rows, cols[hh]])
            G.append(sum(jnp.dot(tri, t.astype(BF16), preferred_element_type=F32)
                         for t in _bf16_split3(lf_ref[rows, cols[hh]])))
            qs.append(q_ref[rows, cols[hh]].astype(F32))
            v.append(i_ref[rows, cols[hh]])
            hs_ref[hh] = G[hh] - jnp.log2(kk[hh])
            st.append(st_ref[hh])
            o_inter.append(lax.dot_general(
                (qs[hh] * jnp.exp2(G[hh])).astype(BF16), st[hh].astype(BF16), _NT,
                preferred_element_type=F32))
        a_mat = [jnp.zeros((c_rows, c_rows), F32) for _ in heads]
        g = sub
        for mask in level_masks:
            for hh in heads:
                ref = jnp.concatenate(
                    [jnp.broadcast_to(G[hh][p + g - 1:p + g], (2 * g, HG_DK))
                     for p in range(0, c_rows, 2 * g)], axis=0)
                e = jnp.exp2(-jnp.abs(G[hh] - ref))
                part = lax.dot_general((qs[hh] * e).astype(BF16), (kk[hh] * e).astype(BF16),
                                       _NT, preferred_element_type=F32)
                a_mat[hh] = jnp.where(mask, part, a_mat[hh])
            g *= 2
        a_rows = [[] for _ in heads]
        for b in range(c_rows // sub):
            lo = b * sub
            for hh in heads:
                q_b = qs[hh][lo:lo + sub]
                g_b = G[hh][lo:lo + sub]
                a_blk = a_mat[hh][lo:lo + sub]
                for s in range(lo, lo + sub):
                    y = q_b * jnp.exp2(g_b - hs_ref[hh, s:s + 1, :])
                    a_blk = jnp.where(lane_c == s, jnp.sum(y, axis=-1, keepdims=True), a_blk)
                a_rows[hh].append(jnp.where(lane_c <= t_sub + lo, a_blk, 0.0))
        for hh in heads:
            a_full = jnp.concatenate(a_rows[hh], axis=0).astype(BF16)
            o = o_inter[hh] + jnp.dot(a_full, v[hh], preferred_element_type=F32)
            g_last = G[hh][c_rows - 1:c_rows]
            kd = kk[hh] * jnp.exp2(g_last - G[hh])
            st_ref[hh] = st[hh] * jnp.exp2(g_last) + lax.dot_general(
                v[hh], kd.astype(BF16), _TN, preferred_element_type=F32)
            o = _rms(o, gain_ref[...]) * g_ref[rows, cols[hh]].astype(F32)
            o_ref[rows, cols[hh]] = o.astype(BF16)
        return carry

    lax.fori_loop(0, tl // c_rows, chunk, 0)


def _hgrn(hqig, hlf, hkk, gain, batch, seq_len):
    n = hqig.shape[0]
    tl = ROW_TILE
    nt = seq_len // tl
    w = HG_PER_STEP * HG_DK
    groups = HG_HEADS // HG_PER_STEP
    col = lambda off: pl.BlockSpec((tl, w), lambda b, h, j: (b * nt + j, off + h))
    return pl.pallas_call(
        functools.partial(_hgrn_kernel, tl=tl),
        grid=(batch, groups, nt),
        in_specs=[col(0), col(groups), col(2 * groups), col(0), col(0),
                  pl.BlockSpec((1, HG_DK), lambda b, h, j: (0, 0))],
        out_specs=col(0),
        out_shape=jax.ShapeDtypeStruct((n, HG_WIDTH), BF16),
        scratch_shapes=[pltpu.VMEM((HG_PER_STEP, HG_DK, HG_DK), F32),
                        pltpu.VMEM((HG_PER_STEP, HG_CHUNK, HG_DK), F32)],
        compiler_params=pltpu.CompilerParams(
            dimension_semantics=("arbitrary", "arbitrary", "arbitrary")),
        name="hgrn2",
    )(hqig, hqig, hqig, hlf, hkk, gain)


def _out_router_kernel(oa_ref, ob_ref, oc_ref, h_ref, wo_ref, g_ref, wrh_ref, wrl_ref, br_ref,
                       h1_ref, route_ref, cnt_ref, carry_ref):
    @pl.when(pl.program_id(0) == 0)
    def _():
        carry_ref[...] = jnp.zeros_like(carry_ref)

    h1 = (h_ref[...]
          + jnp.dot(oa_ref[...], wo_ref[0:512, :], preferred_element_type=F32)
          + jnp.dot(ob_ref[...], wo_ref[512:768, :], preferred_element_type=F32)
          + jnp.dot(oc_ref[...], wo_ref[768:1024, :], preferred_element_type=F32))
    h1_ref[...] = h1
    u = _rms(h1, g_ref[...])
    u_hi = u.astype(BF16)
    u_lo = (u - u_hi.astype(F32)).astype(BF16)
    logits = (jnp.dot(u_hi, wrh_ref[...], preferred_element_type=F32)
              + (jnp.dot(u_hi, wrl_ref[...], preferred_element_type=F32)
                 + jnp.dot(u_lo, wrh_ref[...], preferred_element_type=F32))
              + br_ref[...])
    tm = logits.shape[0]
    lane = lax.broadcasted_iota(jnp.int32, (tm, LANES), 1).astype(F32)
    big = float(LANES)
    is_g = (lane >= N_EXPERTS) & (lane < N_EXPERTS + N_GROUPS)
    gl = jnp.where(is_g, logits, -jnp.inf)
    gmax = jnp.max(gl, axis=-1, keepdims=True)
    gsel = jnp.min(jnp.where(gl == gmax, lane, big), axis=-1, keepdims=True) - N_EXPERTS
    p_g = 1.0 / jnp.sum(jnp.exp(gl - gmax), axis=-1, keepdims=True)
    lo = gsel * EXPERTS_PER_GROUP
    el = jnp.where((lane >= lo) & (lane < lo + EXPERTS_PER_GROUP), logits, -jnp.inf)
    m1 = jnp.max(el, axis=-1, keepdims=True)
    i1 = jnp.min(jnp.where(el == m1, lane, big), axis=-1, keepdims=True)
    el2 = jnp.where(lane == i1, -jnp.inf, el)
    m2 = jnp.max(el2, axis=-1, keepdims=True)
    i2 = jnp.min(jnp.where(el2 == m2, lane, big), axis=-1, keepdims=True)
    r = jnp.exp(m2 - m1)
    gate1 = p_g / (1.0 + r)
    gate2 = gate1 * r
    oh1 = lane == i1
    oh2 = lane == i2
    onehot = jnp.where(oh1 | oh2, 1.0, 0.0)
    tri = (lax.broadcasted_iota(jnp.int32, (tm, tm), 1)
           < lax.broadcasted_iota(jnp.int32, (tm, tm), 0)).astype(BF16)
    before = jnp.dot(tri, onehot.astype(BF16), preferred_element_type=F32) + carry_ref[...]
    rank1 = jnp.sum(jnp.where(oh1, before, 0.0), axis=-1, keepdims=True)
    rank2 = jnp.sum(jnp.where(oh2, before, 0.0), axis=-1, keepdims=True)
    total = carry_ref[...] + jnp.sum(onehot, axis=0, keepdims=True)
    carry_ref[...] = total
    cnt_ref[...] = total
    route = jnp.where(lane == 0, i1, 0.0)
    for idx, val in ((1, i2), (2, gate1), (3, gate2), (4, rank1), (5, rank2)):
        route = jnp.where(lane == idx, val, route)
    route_ref[...] = route


def _out_router(oa, ob, oc, h, w_out, g, w_rt, b_rt):
    n = h.shape[0]
    w_rt_hi = w_rt.astype(BF16)
    w_rt_lo = (w_rt - w_rt_hi.astype(F32)).astype(BF16)
    tm = ROUTER_TILE
    row = lambda w: pl.BlockSpec((tm, w), lambda i: (i, 0))
    const = lambda r, c: pl.BlockSpec((r, c), lambda i: (0, 0))
    return pl.pallas_call(
        _out_router_kernel,
        grid=(n // tm,),
        in_specs=[row(HG_WIDTH), row(DA_WIDTH), row(FX_WIDTH), row(D_MODEL),
                  const(D_MODEL, D_MODEL),
                  const(1, D_MODEL), const(D_MODEL, LANES), const(D_MODEL, LANES),
                  const(1, LANES)],
        out_specs=[row(D_MODEL), row(LANES), const(1, LANES)],
        out_shape=[jax.ShapeDtypeStruct((n, D_MODEL), F32),
                   jax.ShapeDtypeStruct((n, LANES), F32),
                   jax.ShapeDtypeStruct((1, LANES), F32)],
        scratch_shapes=[pltpu.VMEM((1, LANES), F32)],
        compiler_params=pltpu.CompilerParams(dimension_semantics=("arbitrary",)),
        name="out_router",
    )(oa, ob, oc, h, w_out, g, w_rt_hi, w_rt_lo, b_rt)


def _dispatch_kernel(dest_ref, unused_ref, h_ref, g_ref, xs_in_ref, xs_ref, back_ref,
                     u_ref, sem, *, tm, n, tb):
    del xs_in_ref
    i = pl.program_id(0)
    n_tiles = n // tm
    base = (i - 1) * tm
    cur = i % 2

    @pl.when(i == 0)
    def _():
        def fill(s, carry):
            parity = lax.shift_right_logical(s, tb.bit_length() - 1) & 1
            back_ref[s] = 2 * n + parity * tb + (s & (tb - 1))
            return carry

        n_ranges = unused_ref.shape[0] // 2

        def one_range(e, carry):
            return lax.fori_loop(unused_ref[e], unused_ref[n_ranges + e], fill, carry)

        lax.fori_loop(0, n_ranges, one_range, 0)

    def issue(r, carry):
        for k in range(2):
            src = k * n + base + r
            d = dest_ref[src]
            back_ref[d] = src
            pltpu.make_async_copy(u_ref.at[1 - cur, pl.ds(r, 1)], xs_ref.at[pl.ds(d, 1)],
                                  sem).start()
        return carry

    @pl.when(i >= 1)
    def _():
        lax.fori_loop(0, tm, issue, 0, unroll=8)

    @pl.when(i < n_tiles)
    def _():
        _store_token_tiles(u_ref.at[cur], _rms(h_ref[...], g_ref[...]))

    @pl.when(i >= 1)
    def _():
        for _ in range(2):
            pltpu.make_async_copy(u_ref.at[1 - cur], xs_ref.at[pl.ds(0, tm)], sem).wait()


def _dispatch(dest, unused, h1, g, xs_init):
    n = h1.shape[0]
    tm = ROW_TILE
    p_rows = xs_init.shape[0]
    return pl.pallas_call(
        functools.partial(_dispatch_kernel, tm=tm, n=n, tb=MOE_BLOCK),
        grid_spec=pltpu.PrefetchScalarGridSpec(
            num_scalar_prefetch=2,
            grid=(n // tm + 1,),
            in_specs=[pl.BlockSpec((tm, D_MODEL),
                                   lambda i, d, un: (jnp.minimum(i, n // tm - 1), 0)),
                      pl.BlockSpec((1, D_MODEL), lambda i, d, un: (0, 0)),
                      pl.BlockSpec(memory_space=pl.ANY)],
            out_specs=[pl.BlockSpec(memory_space=pl.ANY),
                       pl.BlockSpec(memory_space=pltpu.SMEM)],
            scratch_shapes=[pltpu.VMEM((2, tm) + TOKEN_TILE, F32),
                            pltpu.SemaphoreType.DMA(())]),
        out_shape=[jax.ShapeDtypeStruct(xs_init.shape, xs_init.dtype),
                   jax.ShapeDtypeStruct((p_rows,), jnp.int32)],
        input_output_aliases={4: 0},
        compiler_params=pltpu.CompilerParams(dimension_semantics=("arbitrary",)),
        name="moe_dispatch",
    )(dest, unused, h1, g, xs_init)


def _expert_kernel(be_ref, na_ref, back_ref, x_ref, w1_ref, w3_ref, w2_ref, out_ref,
                   y_ref, w1b_ref, w3b_ref, w2b_ref, sem, *, tb):
    j = pl.program_id(0)
    na = na_ref[0]
    cur = j % 2

    def scatter(block, buf):
        for r in range(tb):
            pltpu.make_async_copy(y_ref.at[buf, pl.ds(r, 1)],
                                  out_ref.at[pl.ds(back_ref[block * tb + r], 1)],
                                  sem.at[buf]).start()

    def compute(buf):
        x = _load_token_tiles(x_ref).astype(BF16)
        a = jnp.dot(x, w1b_ref[...], preferred_element_type=F32)
        b = jnp.dot(x, w3b_ref[...], preferred_element_type=F32)
        act = (a * _sigmoid(a) * b).astype(BF16)
        _store_token_tiles(y_ref.at[buf],
                           jnp.dot(act, w2b_ref[...], preferred_element_type=F32))

    new_expert = (j == 0) | (be_ref[j] != be_ref[jnp.maximum(j - 1, 0)])

    @pl.when((j < na) & new_expert)
    def _():
        w1b_ref[...] = w1_ref[0].astype(BF16)
        w3b_ref[...] = w3_ref[0].astype(BF16)
        w2b_ref[...] = w2_ref[0].astype(BF16)

    @pl.when((j >= 2) & (j < na + 2))
    def _():
        pltpu.make_async_copy(y_ref.at[cur], out_ref.at[pl.ds(0, tb)], sem.at[cur]).wait()

    @pl.when(j == 0)
    def _():
        y_ref[...] = jnp.zeros_like(y_ref)
        first_spare = out_ref.shape[0] - 2 * tb
        spare = [pltpu.make_async_copy(
            y_ref.at[b], out_ref.at[pl.ds(first_spare + b * tb, tb)], sem.at[b])
            for b in range(2)]
        for copy in spare:
            copy.start()
        for copy in spare:
            copy.wait()
        compute(cur)

    @pl.when((j >= 1) & (j < na))
    def _():
        scatter(j - 1, 1 - cur)
        compute(cur)

    @pl.when(j == na)
    def _():
        scatter(j - 1, 1 - cur)


def _experts(blk_expert, n_active, slot_back, xs, w1, w3, w2, layer, n):
    p_rows = xs.shape[0]
    tb = MOE_BLOCK
    last = lambda j, na: jnp.maximum(jnp.minimum(j, na[0] - 1), 0)
    w_spec = lambda r, c: pl.BlockSpec((None, 1, r, c),
                                       lambda j, be, na, back: (layer, be[j], 0, 0))
    return pl.pallas_call(
        functools.partial(_expert_kernel, tb=tb),
        grid_spec=pltpu.PrefetchScalarGridSpec(
            num_scalar_prefetch=3,
            grid=(p_rows // tb + 1,),
            in_specs=[pl.BlockSpec((tb,) + TOKEN_TILE,
                                   lambda j, be, na, back: (last(j, na),) + _TOKEN_ORIGIN),
                      w_spec(D_MODEL, D_EXPERT), w_spec(D_MODEL, D_EXPERT),
                      w_spec(D_EXPERT, D_MODEL)],
            out_specs=pl.BlockSpec(memory_space=pl.ANY),
            scratch_shapes=[pltpu.VMEM((2, tb) + TOKEN_TILE, F32),
                            pltpu.VMEM((D_MODEL, D_EXPERT), BF16),
                            pltpu.VMEM((D_MODEL, D_EXPERT), BF16),
                            pltpu.VMEM((D_EXPERT, D_MODEL), BF16),
                            pltpu.SemaphoreType.DMA((2,))]),
        out_shape=jax.ShapeDtypeStruct((2 * n + 2 * tb,) + TOKEN_TILE, F32),
        compiler_params=pltpu.CompilerParams(dimension_semantics=("arbitrary",),
                                             vmem_limit_bytes=VMEM_LIMIT),
        name="moe_experts",
    )(blk_expert, n_active, slot_back, xs, w1, w3, w2)


def _combine_kernel(h_ref, route_ref, gfin_ref, y0_ref, y1_ref, o_ref, *, final):
    route = route_ref[...]
    out = (h_ref[...] + route[:, 2:3] * _load_token_tiles(y0_ref)
           + route[:, 3:4] * _load_token_tiles(y1_ref))
    if final:
        out = _rms(out, gfin_ref[...])
    o_ref[...] = out


def _combine(h1, route, g_final, ys, batch, seq_len, final):
    n = h1.shape[0]
    tm = ROW_TILE
    tiles_per_seq = seq_len // tm
    skip_tiles = (LPAD + N_META) // tm if final else 0
    out_tiles = tiles_per_seq - skip_tiles
    tile = lambda b, i: b * tiles_per_seq + skip_tiles + i
    in_row = lambda w: pl.BlockSpec((tm, w), lambda b, i: (tile(b, i), 0))
    y_rows = lambda k: pl.BlockSpec((tm,) + TOKEN_TILE,
                                    lambda b, i: (k * (n // tm) + tile(b, i),) + _TOKEN_ORIGIN)
    return pl.pallas_call(
        functools.partial(_combine_kernel, final=final),
        grid=(batch, out_tiles),
        in_specs=[in_row(D_MODEL), in_row(LANES),
                  pl.BlockSpec((1, D_MODEL), lambda b, i: (0, 0)),
                  y_rows(0), y_rows(1)],
        out_specs=pl.BlockSpec((tm, D_MODEL), lambda b, i: (b * out_tiles + i, 0)),
        out_shape=jax.ShapeDtypeStruct((batch * out_tiles * tm, D_MODEL), F32),
        compiler_params=pltpu.CompilerParams(dimension_semantics=("arbitrary", "arbitrary")),
        name="moe_combine",
    )(h1, route, g_final, ys, ys)


def _rope_partner(m):
    lead = m.shape[:-1]
    x = m.reshape(lead + (DA_WIDTH // DA_DQK, 2, DA_DQK // 2))
    return x[..., ::-1, :].reshape(lead + (DA_WIDTH,))


def _rope_tables(seq_len):
    pos = (jnp.arange(seq_len) - LPAD).astype(F32)
    inv = ROPE_THETA ** (-jnp.arange(0, DA_DQK, 2, dtype=F32) / DA_DQK)
    ang = pos[:, None] * inv[None, :]
    comps = DA_DV // DA_DQK
    cos = jnp.tile(jnp.cos(ang), (1, 2 * comps))
    sin = jnp.tile(jnp.concatenate([-jnp.sin(ang), jnp.sin(ang)], axis=1), (1, comps))
    q_scale = DA_DQK ** -0.5 * LOG2E
    two = lambda t: jnp.tile(t, (1, LANES // DA_DV))
    lay = lambda t: jnp.concatenate([two(t * q_scale), two(t)], axis=1)
    return lay(cos), lay(sin)


def _in_weights(w):
    hq, hf, hi, hg = (w[:, i * 512:(i + 1) * 512] for i in range(4))
    dq, dk, dv = (w[:, 2048 + i * 256:2048 + (i + 1) * 256] for i in range(3))
    fq, fk, fv = (w[:, 2816 + i * 256:2816 + (i + 1) * 256] for i in range(3))
    ff = w[:, 3584:3588]
    cat = jnp.concatenate(
        [hq, hi, hg, hf,
         dq, dk, _rope_partner(dq), _rope_partner(dk),
         fq * (FX_DH ** -0.5), fk,
         ff, jnp.zeros((D_MODEL, LANES - FX_HEADS), w.dtype)],
        axis=1)
    w_vt = jnp.concatenate([dv, fv], axis=1).T
    return cat.astype(BF16), w_vt.astype(BF16)


def _pad_lanes(v, width=LANES):
    return jnp.zeros((1, width), F32).at[0, :v.shape[0]].set(v.astype(F32))


def kernel(x, meta_tokens, norm_mix, w_in, hgrn_lb, hgrn_norm, diff_lambda, diff_norm,
           fox_bias, fox_norm, w_out, norm_ffn, w_group, b_group, w_router, b_router,
           w1, w3, w2, norm_final):
    batch, seq, d = x.shape
    depth = w_in.shape[0]
    seq_len = LPAD + N_META + seq
    n = batch * seq_len
    pad = jnp.zeros((batch, LPAD, d), x.dtype)
    meta = jnp.broadcast_to(meta_tokens.astype(x.dtype)[None], (batch, N_META, d))
    h = jnp.concatenate([pad, meta, x], axis=1).reshape(n, d)

    s_lb = jax.nn.softmax(hgrn_lb.astype(F32), axis=0)
    lb_all = jnp.cumsum(s_lb, axis=0) - s_lb[0]
    cos_t, sin_t = _rope_tables(seq_len)

    tb = MOE_BLOCK
    n_blocks = (2 * n) // tb + N_EXPERTS
    p_rows = n_blocks * tb

    xs = jnp.zeros((p_rows,) + TOKEN_TILE, F32)
    for layer in range(depth):
        lam_init = 0.8 - 0.6 * math.exp(-0.3 * layer)
        w_cat, w_vt = _in_weights(w_in[layer])
        hqig, hlf, hkk, dqk, fqk, dvt, fvt = _in_proj(
            h, norm_mix[layer][None, :], w_cat, w_vt, cos_t, sin_t,
            _pad_lanes(fox_bias[layer]), lb_all[layer][None, :], batch, seq_len)

        o_a = _hgrn(hqig, hlf, hkk, hgrn_norm[layer][None, :], batch, seq_len)

        lam_vecs = jnp.zeros((8, LANES), F32).at[:4, :DA_DQK].set(diff_lambda[layer].astype(F32))
        o_b = _diff_attn(dqk, dvt, lam_vecs, _pad_lanes(diff_norm[layer]), lam_init,
                         batch, seq_len)

        o_c = _fox_attn(fqk, fvt, _pad_lanes(fox_norm[layer]), batch, seq_len)

        w_rt = jnp.concatenate(
            [w_router[layer], w_group[layer],
             jnp.zeros((d, LANES - N_EXPERTS - N_GROUPS), F32)], axis=1)
        b_rt = _pad_lanes(jnp.concatenate([b_router[layer], b_group[layer]]))
        h1, route, counts = _out_router(
            o_a, o_b, o_c, h, w_out[layer].astype(BF16), norm_ffn[layer][None, :], w_rt, b_rt)

        cnt = counts[0, :N_EXPERTS].astype(jnp.int32)
        padded = (cnt + tb - 1) // tb * tb
        p_end = jnp.cumsum(padded)
        p_start = p_end - padded
        ids = route[:, 0:2].astype(jnp.int32)
        ranks = route[:, 4:6].astype(jnp.int32)
        dest = (p_start[ids] + ranks).T.reshape(2 * n)
        blk_start = jnp.arange(n_blocks + 1, dtype=jnp.int32) * tb
        blk_expert = jnp.minimum(
            jnp.sum((p_end[None, :] <= blk_start[:, None]).astype(jnp.int32), axis=1),
            N_EXPERTS - 1)
        n_active = (p_end[-1:] // tb).astype(jnp.int32)

        unused = jnp.concatenate(
            [p_start + cnt, p_end[-1:], p_end, jnp.full((1,), p_rows)]).astype(jnp.int32)
        xs, slot_back = _dispatch(dest, unused, h1, norm_ffn[layer][None, :], xs)
        ys = _experts(blk_expert, n_active, slot_back, xs, w1, w3, w2, layer, n)
        h = _combine(h1, route, norm_final[None, :], ys, batch, seq_len,
                     final=(layer == depth - 1))

    return h.reshape(batch, seq, d)
```

```python
import functools
import math

import jax
import jax.numpy as jnp
from jax import lax
from jax.experimental import pallas as pl
from jax.experimental.pallas import tpu as pltpu

F32 = jnp.float32
BF16 = jnp.bfloat16

D_MODEL = 1024
N_META = 16
HG_HEADS = 4
HG_DK = 128
HG_WIDTH = 512
HG_CHUNK = 64
HG_SUB = 8
HG_PER_STEP = 4
DA_HEADS = 4
DA_DV = 64
DA_DQK = 32
DA_WIDTH = 256
FX_HEADS = 4
FX_DH = 64
FX_WIDTH = 256
N_GROUPS = 4
EXPERTS_PER_GROUP = 8
N_EXPERTS = 32
D_EXPERT = 512
ROPE_THETA = 10000.0
EPS = 1e-6
NEG = -1e30
TINY = 1e-30
LOG2E = 1.4426950408889634

LANES = 128
SLOT_ROW = (D_MODEL // LANES, LANES)
SLOT = LANES
HEAD_W = 4 * SLOT
ROW_TILE = 256
PROJ_TILE = 512
ROUTER_TILE = 512
CUMSUM_ROWS = 128
ATTN_BLOCK = 512
ATTN_HEADS_PER_STEP = 4
MOE_BLOCK = 256
LPAD = ATTN_BLOCK - N_META
FIRST_KEYS = LANES
assert ATTN_BLOCK - FIRST_KEYS <= LPAD
ONE_LANE = 64
VMEM_LIMIT = 56 * 1024 * 1024

_C_HQIG = (0, 1536)
_C_HF = (1536, 2048)
_C_DQK = (2048, 2560)
_C_DQKP = (2560, 3072)
_C_FQK = (3072, 3584)
_C_FF = (3584, 3712)
IN_COLS = 3712

_NT = (((1,), (1,)), ((), ()))
_TN = (((0,), (0,)), ((), ()))


def _rows_to_tiles(ref, x):
    for c in range(SLOT_ROW[0]):
        ref[:, c, :] = x[:, c * LANES:(c + 1) * LANES]


def _tiles_to_rows(ref):
    return jnp.concatenate([ref[:, c, :] for c in range(SLOT_ROW[0])], axis=1)


def _sigmoid(x):
    return 1.0 / (1.0 + jnp.exp(-x))


def _rms(x, g):
    return x * lax.rsqrt(jnp.mean(x * x, axis=-1, keepdims=True) + EPS) * g


def _in_proj_kernel(h_ref, g_ref, w_ref, wvt_ref, cos_ref, sin_ref, fb_ref, lb_ref,
                    hqig_ref, hlf_ref, hkk_ref, dqk_ref, fqk_ref, dvt_ref, fvt_ref,
                    carry_ref, *, tiles_per_seq):
    u = _rms(h_ref[...], g_ref[...]).astype(BF16)
    tm = u.shape[0]

    def mm(c):
        return jnp.dot(u, w_ref[:, c[0]:c[1]], preferred_element_type=F32)

    hg = mm(_C_HQIG)
    q_raw, g_raw = hg[:, 0:HG_WIDTH], hg[:, 2 * HG_WIDTH:3 * HG_WIDTH]
    hqig_ref[:, 0:HG_WIDTH] = (q_raw * _sigmoid(q_raw) * (HG_DK ** -0.5)).astype(BF16)
    hqig_ref[:, HG_WIDTH:2 * HG_WIDTH] = hg[:, HG_WIDTH:2 * HG_WIDTH].astype(BF16)
    hqig_ref[:, 2 * HG_WIDTH:3 * HG_WIDTH] = (g_raw * _sigmoid(g_raw)).astype(BF16)
    lb = lb_ref[...]
    sig = _sigmoid(mm(_C_HF))
    row = ((pl.program_id(0) % tiles_per_seq) * tm
           + lax.broadcasted_iota(jnp.int32, (tm, 1), 0))
    valid = row >= LPAD
    hlf_ref[...] = jnp.where(
        valid, jnp.log2(jnp.maximum(lb + (1.0 - lb) * sig, TINY)), 0.0)
    hkk_ref[...] = jnp.where(valid, (1.0 - lb) * (1.0 - sig), 0.0)
    half_lane = lax.broadcasted_iota(jnp.int32, (tm, LANES), 1) < DA_DV

    def spread(x):
        tiles = []
        for t in range(x.shape[1] // LANES):
            tile = x[:, t * LANES:(t + 1) * LANES]
            tiles.append(jnp.where(half_lane, tile, 0.0))
            tiles.append(jnp.where(half_lane, pltpu.roll(tile, LANES // 2, axis=1), 0.0))
        return jnp.concatenate(tiles, axis=1)

    def per_tile(t_ref):
        return jnp.concatenate([t_ref[:, 0:LANES]] * 2 + [t_ref[:, LANES:2 * LANES]] * 2, axis=1)

    rot = mm(_C_DQK) * per_tile(cos_ref) + mm(_C_DQKP) * per_tile(sin_ref)
    dqk_ref[:, 0:HEAD_W] = spread(rot[:, 0:DA_WIDTH]).astype(BF16)
    dqk_ref[:, HEAD_W:2 * HEAD_W] = spread(rot[:, DA_WIDTH:2 * DA_WIDTH]).astype(BF16)

    @pl.when(pl.program_id(0) % tiles_per_seq == 0)
    def _():
        carry_ref[...] = jnp.zeros_like(carry_ref)

    x = mm(_C_FF) + fb_ref[...]
    lf = jnp.minimum(x, 0.0) - jnp.log(1.0 + jnp.exp(-jnp.abs(x)))
    tri = (lax.broadcasted_iota(jnp.int32, (CUMSUM_ROWS, CUMSUM_ROWS), 1)
           <= lax.broadcasted_iota(jnp.int32, (CUMSUM_ROWS, CUMSUM_ROWS), 0)).astype(BF16)
    lf_terms = [t.astype(BF16) for t in _bf16_split3(lf)]
    total = carry_ref[...]
    groups = []
    for r0 in range(0, tm, CUMSUM_ROWS):
        part = sum(jnp.dot(tri, t[r0:r0 + CUMSUM_ROWS], preferred_element_type=F32)
                   for t in lf_terms) + total
        total = part[CUMSUM_ROWS - 1:CUMSUM_ROWS, :]
        groups.append(part)
    cs = jnp.concatenate(groups, axis=0)
    carry_ref[...] = total
    parts = _bf16_split3(cs * LOG2E)
    fqk = mm(_C_FQK)
    fq = spread(fqk[:, 0:FX_WIDTH]) * LOG2E
    fk = spread(fqk[:, FX_WIDTH:2 * FX_WIDTH])
    lane = lax.broadcasted_iota(jnp.int32, (tm, SLOT), 1)
    d0 = FX_DH
    for h in range(FX_HEADS):
        q = fq[:, h * SLOT:(h + 1) * SLOT]
        k = fk[:, h * SLOT:(h + 1) * SLOT]
        for i, part in enumerate(parts):
            col = part[:, h:h + 1]
            q = jnp.where(lane == d0 + i, col, q)
            k = jnp.where(lane == d0 + 3 + i, -col, k)
        q = jnp.where((lane >= d0 + 3) & (lane < d0 + 6), 1.0, q)
        k = jnp.where((lane >= d0) & (lane < d0 + 3), 1.0, k)
        fqk_ref[:, h * SLOT:(h + 1) * SLOT] = q.astype(BF16)
        fqk_ref[:, HEAD_W + h * SLOT:HEAD_W + (h + 1) * SLOT] = k.astype(BF16)

    vt = lax.dot_general(wvt_ref[...], u, _NT, preferred_element_type=F32).astype(BF16)
    spare = SLOT - DA_DV
    ones_then_zeros = jnp.where(lax.broadcasted_iota(jnp.int32, (spare, tm), 0) == 0,
                                1.0, 0.0).astype(BF16)
    for h in range(4):
        for ref, first in ((dvt_ref, 0), (fvt_ref, DA_WIDTH)):
            ref[0, h, 0, 0:DA_DV, :] = vt[first + h * DA_DV:first + (h + 1) * DA_DV]
            ref[0, h, 0, DA_DV:SLOT, :] = ones_then_zeros


def _in_proj(h, g, w_cat, w_vt, cos_t, sin_t, fox_bias_row, lb_row, batch, seq_len):
    n = h.shape[0]
    tm = PROJ_TILE
    assert tm == ATTN_BLOCK
    nk = seq_len // tm
    row = lambda w: pl.BlockSpec((tm, w), lambda i: (i, 0))
    tab = pl.BlockSpec((tm, 2 * SLOT), lambda i: (i % nk, 0))
    once = lambda r, c: pl.BlockSpec((r, c), lambda i: (0, 0), pipeline_mode=pl.Buffered(1))
    widths = (3 * HG_WIDTH, HG_WIDTH, HG_WIDTH, 2 * HEAD_W, 2 * HEAD_W)
    dtypes = (BF16, F32, F32, BF16, BF16)
    vt_spec = pl.BlockSpec((1, 4, 1, SLOT, tm), lambda i: (i // nk, 0, i % nk, 0, 0))
    vt_shape = jax.ShapeDtypeStruct((batch, 4, nk, SLOT, tm), BF16)
    return pl.pallas_call(
        functools.partial(_in_proj_kernel, tiles_per_seq=nk),
        grid=(n // tm,),
        in_specs=[row(D_MODEL),
                  pl.BlockSpec((1, D_MODEL), lambda i: (0, 0)),
                  once(D_MODEL, IN_COLS), once(DA_WIDTH + FX_WIDTH, D_MODEL),
                  tab, tab,
                  pl.BlockSpec((1, LANES), lambda i: (0, 0)),
                  pl.BlockSpec((1, HG_WIDTH), lambda i: (0, 0))],
        out_specs=[row(w) for w in widths] + [vt_spec, vt_spec],
        out_shape=[jax.ShapeDtypeStruct((n, w), t) for w, t in zip(widths, dtypes)]
        + [vt_shape, vt_shape],
        scratch_shapes=[pltpu.VMEM((1, LANES), F32)],
        compiler_params=pltpu.CompilerParams(dimension_semantics=("arbitrary",),
                                             vmem_limit_bytes=VMEM_LIMIT),
        name="in_proj",
    )(h, g, w_cat, w_vt, cos_t, sin_t, fox_bias_row, lb_row)


def _bf16_split3(x):
    hi = x.astype(BF16).astype(F32)
    r = x - hi
    mid = r.astype(BF16).astype(F32)
    return hi, mid, r - mid


def _softmax_block(s, m_prev):
    m_new = jnp.maximum(m_prev, jnp.max(s, axis=0, keepdims=True))
    alpha = jnp.exp2(m_prev - m_new)
    p = jnp.exp2(s - m_new).astype(BF16)
    return p, alpha, m_new


def _block_start(j, blk):
    return j * blk if isinstance(j, int) else pl.multiple_of(j * blk, blk)


def _causal_valid(i, j, blk, first_row=0):
    k_idx = j * blk + first_row + lax.broadcasted_iota(jnp.int32, (blk - first_row, blk), 0)
    q_idx = i * blk + lax.broadcasted_iota(jnp.int32, (blk - first_row, blk), 1)
    return (k_idx <= q_idx) & (k_idx >= LPAD)


def _pipelined_key_blocks(i, first, scores, consume):
    first()

    @pl.when(i >= 1)
    def _():
        scores(1, 0)

    def pair(t, carry):
        scores(2 * t + 2, 1)
        consume(2 * t + 1, 0, False)
        scores(2 * t + 3, 0)
        consume(2 * t + 2, 1, False)
        return carry

    lax.fori_loop(0, lax.shift_right_logical(jnp.maximum(i - 1, 0), 1), pair, 0)
    odd = (i & 1) == 1

    @pl.when(odd)
    def _():
        consume(i, 0, True)

    @pl.when((i >= 2) & jnp.logical_not(odd))
    def _():
        scores(i, 1)
        consume(i - 1, 0, False)
        consume(i, 1, True)


def _head_out(acc):
    lane = lax.broadcasted_iota(jnp.int32, acc.shape, 1)
    return jnp.where(lane < ONE_LANE, acc / acc[:, ONE_LANE:ONE_LANE + 1], 0.0)


def _store_head_pairs(o_ref, outs):
    for p in range(len(outs) // 2):
        pair = outs[2 * p] + pltpu.roll(outs[2 * p + 1], ONE_LANE, axis=1)
        o_ref[:, p * LANES:(p + 1) * LANES] = pair.astype(BF16)


def _head_rms(o, g):
    ms = jnp.sum(o * o, axis=-1, keepdims=True) * (1.0 / ONE_LANE)
    return o * lax.rsqrt(ms + EPS) * g


def _fox_attn_kernel(q_ref, k_ref, vt_ref, g_ref, o_ref, m_ref, acc_ref, sa_ref, sb_ref,
                     *, blk):
    i = pl.program_id(2)
    heads = range(ATTN_HEADS_PER_STEP)
    slot = [slice(hh * SLOT, (hh + 1) * SLOT) for hh in heads]
    q = [q_ref[:, slot[hh]] for hh in heads]
    bufs = (sa_ref, sb_ref)
    m_ref[...] = jnp.full(m_ref.shape, NEG, F32)
    acc_ref[...] = jnp.zeros_like(acc_ref)

    def scores(j, buf):
        rows = pl.ds(_block_start(j, blk), blk)
        for hh in heads:
            bufs[buf][hh] = lax.dot_general(k_ref[rows, slot[hh]], q[hh], _NT,
                                            preferred_element_type=F32)

    def consume(j, buf, masked):
        mask = _causal_valid(i, j, blk) if masked else None
        for hh in heads:
            s = bufs[buf][hh]
            if masked:
                s = jnp.where(mask, s, NEG)
            p, alpha, m_new = _softmax_block(s, m_ref[hh])
            acc_ref[hh] = alpha * acc_ref[hh] + jnp.dot(
                vt_ref[0, hh, j], p, preferred_element_type=F32)
            m_ref[hh] = m_new

    def first():
        lo = blk - FIRST_KEYS
        mask = _causal_valid(i, 0, blk, lo)
        for hh in heads:
            s = lax.dot_general(k_ref[lo:blk, slot[hh]], q[hh], _NT,
                                preferred_element_type=F32)
            p, alpha, m_new = _softmax_block(jnp.where(mask, s, NEG), m_ref[hh])
            acc_ref[hh] = alpha * acc_ref[hh] + jnp.dot(
                vt_ref[0, hh, 0, :, lo:blk], p, preferred_element_type=F32)
            m_ref[hh] = m_new

    _pipelined_key_blocks(i, first, scores, consume)
    _store_head_pairs(
        o_ref, [_head_rms(_head_out(acc_ref[hh].T), g_ref[...]) for hh in heads])


def _key_spec(seq_len, groups):
    return pl.BlockSpec((seq_len, ATTN_HEADS_PER_STEP * SLOT),
                        lambda b, h, i: (b, groups + h), pipeline_mode=pl.Buffered(1))


def _value_spec(seq_len):
    nk = seq_len // ATTN_BLOCK
    return pl.BlockSpec((1, ATTN_HEADS_PER_STEP, nk, SLOT, ATTN_BLOCK),
                        lambda b, h, i: (b, h, 0, 0, 0), pipeline_mode=pl.Buffered(1))


def _fox_attn(fqk, fvt, gain, batch, seq_len):
    n = fqk.shape[0]
    blk = ATTN_BLOCK
    nq = seq_len // blk
    hps = ATTN_HEADS_PER_STEP
    groups = FX_HEADS // hps
    return pl.pallas_call(
        functools.partial(_fox_attn_kernel, blk=blk),
        grid=(batch, groups, nq),
        in_specs=[pl.BlockSpec((blk, hps * SLOT), lambda b, h, i: (b * nq + i, h)),
                  _key_spec(seq_len, groups),
                  _value_spec(seq_len),
                  pl.BlockSpec((1, SLOT), lambda b, h, i: (0, 0))],
        out_specs=pl.BlockSpec((blk, hps * ONE_LANE), lambda b, h, i: (b * nq + i, h)),
        out_shape=jax.ShapeDtypeStruct((n, 4 * ONE_LANE), BF16),
        scratch_shapes=[pltpu.VMEM((hps, 1, blk), F32), pltpu.VMEM((hps, SLOT, blk), F32),
                        pltpu.VMEM((hps, blk, blk), F32), pltpu.VMEM((hps, blk, blk), F32)],
        compiler_params=pltpu.CompilerParams(
            dimension_semantics=("arbitrary", "arbitrary", "arbitrary"),
            vmem_limit_bytes=VMEM_LIMIT),
        name="fox_attn",
    )(fqk, fqk, fvt, gain)


def _diff_attn_kernel(q_ref, k_ref, vt_ref, lam_ref, g_ref, o_ref, m_ref, acc_ref,
                      sa_ref, sb_ref, *, blk, lam_init):
    i = pl.program_id(2)
    heads = range(ATTN_HEADS_PER_STEP)
    slot = lambda s: slice(s * SLOT, (s + 1) * SLOT)
    lane = lax.broadcasted_iota(jnp.int32, (blk, SLOT), 1)
    q = []
    for hh in heads:
        q_h = q_ref[:, slot(hh)]
        q.append([jnp.where((lane >= c * DA_DQK) & (lane < (c + 1) * DA_DQK), q_h,
                            jnp.zeros_like(q_h)) for c in range(2)])
    bufs = (sa_ref, sb_ref)
    m_ref[...] = jnp.full(m_ref.shape, NEG, F32)
    acc_ref[...] = jnp.zeros_like(acc_ref)

    def scores(j, buf):
        rows = pl.ds(_block_start(j, blk), blk)
        for hh in heads:
            k_h = k_ref[rows, slot(hh)]
            for c in range(2):
                bufs[buf][2 * hh + c] = lax.dot_general(k_h, q[hh][c], _NT,
                                                        preferred_element_type=F32)

    def consume(j, buf, masked):
        mask = _causal_valid(i, j, blk) if masked else None
        for hh in heads:
            ps, alphas = [], []
            for c in range(2):
                hc = 2 * hh + c
                s = bufs[buf][hc]
                if masked:
                    s = jnp.where(mask, s, NEG)
                p, alpha, m_new = _softmax_block(s, m_ref[hc])
                m_ref[hc] = m_new
                ps.append(p)
                alphas.append(alpha)
            pv = jnp.dot(vt_ref[0, hh, j], jnp.concatenate(ps, axis=1),
                         preferred_element_type=F32)
            for c in range(2):
                hc = 2 * hh + c
                acc_ref[hc] = alphas[c] * acc_ref[hc] + pv[:, c * blk:(c + 1) * blk]

    def first():
        lo = blk - FIRST_KEYS
        mask = _causal_valid(i, 0, blk, lo)
        for hh in heads:
            k_h = k_ref[lo:blk, slot(hh)]
            ps, alphas = [], []
            for c in range(2):
                hc = 2 * hh + c
                s = lax.dot_general(k_h, q[hh][c], _NT, preferred_element_type=F32)
                p, alpha, m_new = _softmax_block(jnp.where(mask, s, NEG), m_ref[hc])
                m_ref[hc] = m_new
                ps.append(p)
                alphas.append(alpha)
            pv = jnp.dot(vt_ref[0, hh, 0, :, lo:blk], jnp.concatenate(ps, axis=1),
                         preferred_element_type=F32)
            for c in range(2):
                hc = 2 * hh + c
                acc_ref[hc] = alphas[c] * acc_ref[hc] + pv[:, c * blk:(c + 1) * blk]

    _pipelined_key_blocks(i, first, scores, consume)
    lv = lam_ref[...]
    lam = (jnp.exp(jnp.sum(lv[0:1] * lv[1:2], axis=-1, keepdims=True))
           - jnp.exp(jnp.sum(lv[2:3] * lv[3:4], axis=-1, keepdims=True)) + lam_init)
    outs = []
    for hh in heads:
        o = _head_out(acc_ref[2 * hh].T) - lam * _head_out(acc_ref[2 * hh + 1].T)
        outs.append(_head_rms(o, g_ref[...]) * (1.0 - lam_init))
    _store_head_pairs(o_ref, outs)


def _diff_attn(dqk, dvt, lam_vecs, gain, lam_init, batch, seq_len):
    n = dqk.shape[0]
    blk = ATTN_BLOCK
    nq = seq_len // blk
    hps = ATTN_HEADS_PER_STEP
    groups = DA_HEADS // hps
    return pl.pallas_call(
        functools.partial(_diff_attn_kernel, blk=blk, lam_init=lam_init),
        grid=(batch, groups, nq),
        in_specs=[pl.BlockSpec((blk, hps * SLOT), lambda b, h, i: (b * nq + i, h)),
                  _key_spec(seq_len, groups),
                  _value_spec(seq_len),
                  pl.BlockSpec((8, LANES), lambda b, h, i: (0, 0)),
                  pl.BlockSpec((1, SLOT), lambda b, h, i: (0, 0))],
        out_specs=pl.BlockSpec((blk, hps * ONE_LANE), lambda b, h, i: (b * nq + i, h)),
        out_shape=jax.ShapeDtypeStruct((n, 4 * ONE_LANE), BF16),
        scratch_shapes=[pltpu.VMEM((2 * hps, 1, blk), F32),
                        pltpu.VMEM((2 * hps, SLOT, blk), F32),
                        pltpu.VMEM((2 * hps, blk, blk), F32),
                        pltpu.VMEM((2 * hps, blk, blk), F32)],
        compiler_params=pltpu.CompilerParams(
            dimension_semantics=("arbitrary", "arbitrary", "arbitrary"),
            vmem_limit_bytes=VMEM_LIMIT),
        name="diff_attn",
    )(dqk, dqk, dvt, lam_vecs, gain)


def _hgrn_kernel(q_ref, i_ref, g_ref, lf_ref, kk_ref, gain_ref, o_ref,
                 st_ref, hs_ref, *, tl):
    @pl.when(pl.program_id(2) == 0)
    def _():
        st_ref[...] = jnp.zeros_like(st_ref)

    c_rows, sub = HG_CHUNK, HG_SUB
    tri = (lax.broadcasted_iota(jnp.int32, (c_rows, c_rows), 1)
           <= lax.broadcasted_iota(jnp.int32, (c_rows, c_rows), 0)).astype(BF16)
    t_sub = lax.broadcasted_iota(jnp.int32, (sub, 1), 0)
    lane_c = lax.broadcasted_iota(jnp.int32, (sub, c_rows), 1)

    t_all = lax.broadcasted_iota(jnp.int32, (c_rows, c_rows), 0)
    s_all = lax.broadcasted_iota(jnp.int32, (c_rows, c_rows), 1)
    level_masks = []
    g = sub
    while g < c_rows:
        level_masks.append(((t_all // g) == (s_all // g) + 1) & ((s_all // g) % 2 == 0))
        g *= 2

    heads = range(HG_PER_STEP)
    cols = [slice(hh * HG_DK, (hh + 1) * HG_DK) for hh in heads]

    def chunk(c, carry):
        r0 = pl.multiple_of(c * c_rows, c_rows)
        rows = pl.ds(r0, c_rows)
        G, kk, qs, v, st, o_inter = [], [], [], [], [], []
        for hh in heads:
            kk.append(kk_ref[rows, cols[hh]])
            G.append(sum(jnp.dot(tri, t.astype(BF16), preferred_element_type=F32)
                         for t in _bf16_split3(lf_ref[rows, cols[hh]])))
            qs.append(q_ref[rows, cols[hh]].astype(F32))
            v.append(i_ref[rows, cols[hh]])
            hs_ref[hh] = G[hh] - jnp.log2(kk[hh])
            st.append(st_ref[hh])
            o_inter.append(lax.dot_general(
                (qs[hh] * jnp.exp2(G[hh])).astype(BF16), st[hh].astype(BF16), _NT,
                preferred_element_type=F32))
        a_mat = [jnp.zeros((c_rows, c_rows), F32) for _ in heads]
        g = sub
        for mask in level_masks:
            for hh in heads:
                ref = jnp.concatenate(
                    [jnp.broadcast_to(G[hh][p + g - 1:p + g], (2 * g, HG_DK))
                     for p in range(0, c_rows, 2 * g)], axis=0)
                e = jnp.exp2(-jnp.abs(G[hh] - ref))
                part = lax.dot_general((qs[hh] * e).astype(BF16), (kk[hh] * e).astype(BF16),
                                       _NT, preferred_element_type=F32)
                a_mat[hh] = jnp.where(mask, part, a_mat[hh])
            g *= 2
        a_rows = [[] for _ in heads]
        for b in range(c_rows // sub):
            lo = b * sub
            for hh in heads:
                q_b = qs[hh][lo:lo + sub]
                g_b = G[hh][lo:lo + sub]
                a_blk = a_mat[hh][lo:lo + sub]
                for s in range(lo, lo + sub):
                    y = q_b * jnp.exp2(g_b - hs_ref[hh, s:s + 1, :])
                    a_blk = jnp.where(lane_c == s, jnp.sum(y, axis=-1, keepdims=True), a_blk)
                a_rows[hh].append(jnp.where(lane_c <= t_sub + lo, a_blk, 0.0))
        for hh in heads:
            a_full = jnp.concatenate(a_rows[hh], axis=0).astype(BF16)
            o = o_inter[hh] + jnp.dot(a_full, v[hh], preferred_element_type=F32)
            g_last = G[hh][c_rows - 1:c_rows]
            kd = kk[hh] * jnp.exp2(g_last - G[hh])
            st_ref[hh] = st[hh] * jnp.exp2(g_last) + lax.dot_general(
                v[hh], kd.astype(BF16), _TN, preferred_element_type=F32)
            o = _rms(o, gain_ref[...]) * g_ref[rows, cols[hh]].astype(F32)
            o_ref[rows, cols[hh]] = o.astype(BF16)
        return carry

    lax.fori_loop(0, tl // c_rows, chunk, 0)


def _hgrn(hqig, hlf, hkk, gain, batch, seq_len):
    n = hqig.shape[0]
    tl = ROW_TILE
    nt = seq_len // tl
    w = HG_PER_STEP * HG_DK
    groups = HG_HEADS // HG_PER_STEP
    col = lambda off: pl.BlockSpec((tl, w), lambda b, h, j: (b * nt + j, off + h))
    return pl.pallas_call(
        functools.partial(_hgrn_kernel, tl=tl),
        grid=(batch, groups, nt),
        in_specs=[col(0), col(groups), col(2 * groups), col(0), col(0),
                  pl.BlockSpec((1, HG_DK), lambda b, h, j: (0, 0))],
        out_specs=col(0),
        out_shape=jax.ShapeDtypeStruct((n, HG_WIDTH), BF16),
        scratch_shapes=[pltpu.VMEM((HG_PER_STEP, HG_DK, HG_DK), F32),
                        pltpu.VMEM((HG_PER_STEP, HG_CHUNK, HG_DK), F32)],
        compiler_params=pltpu.CompilerParams(
            dimension_semantics=("arbitrary", "arbitrary", "arbitrary")),
        name="hgrn2",
    )(hqig, hqig, hqig, hlf, hkk, gain)


def _out_router_kernel(oa_ref, ob_ref, oc_ref, h_ref, wo_ref, g_ref, wrh_ref, wrl_ref, br_ref,
                       h1_ref, route_ref, cnt_ref, carry_ref):
    @pl.when(pl.program_id(0) == 0)
    def _():
        carry_ref[...] = jnp.zeros_like(carry_ref)

    h1 = (h_ref[...]
          + jnp.dot(oa_ref[...], wo_ref[0:512, :], preferred_element_type=F32)
          + jnp.dot(ob_ref[...], wo_ref[512:768, :], preferred_element_type=F32)
          + jnp.dot(oc_ref[...], wo_ref[768:1024, :], preferred_element_type=F32))
    h1_ref[...] = h1
    u = _rms(h1, g_ref[...])
    u_hi = u.astype(BF16)
    u_lo = (u - u_hi.astype(F32)).astype(BF16)
    logits = (jnp.dot(u_hi, wrh_ref[...], preferred_element_type=F32)
              + (jnp.dot(u_hi, wrl_ref[...], preferred_element_type=F32)
                 + jnp.dot(u_lo, wrh_ref[...], preferred_element_type=F32))
              + br_ref[...])
    tm = logits.shape[0]
    lane = lax.broadcasted_iota(jnp.int32, (tm, LANES), 1).astype(F32)
    big = float(LANES)
    is_g = (lane >= N_EXPERTS) & (lane < N_EXPERTS + N_GROUPS)
    gl = jnp.where(is_g, logits, -jnp.inf)
    gmax = jnp.max(gl, axis=-1, keepdims=True)
    gsel = jnp.min(jnp.where(gl == gmax, lane, big), axis=-1, keepdims=True) - N_EXPERTS
    p_g = 1.0 / jnp.sum(jnp.exp(gl - gmax), axis=-1, keepdims=True)
    lo = gsel * EXPERTS_PER_GROUP
    el = jnp.where((lane >= lo) & (lane < lo + EXPERTS_PER_GROUP), logits, -jnp.inf)
    m1 = jnp.max(el, axis=-1, keepdims=True)
    i1 = jnp.min(jnp.where(el == m1, lane, big), axis=-1, keepdims=True)
    el2 = jnp.where(lane == i1, -jnp.inf, el)
    m2 = jnp.max(el2, axis=-1, keepdims=True)
    i2 = jnp.min(jnp.where(el2 == m2, lane, big), axis=-1, keepdims=True)
    r = jnp.exp(m2 - m1)
    gate1 = p_g / (1.0 + r)
    gate2 = gate1 * r
    oh1 = lane == i1
    oh2 = lane == i2
    onehot = jnp.where(oh1 | oh2, 1.0, 0.0)
    tri = (lax.broadcasted_iota(jnp.int32, (tm, tm), 1)
           < lax.broadcasted_iota(jnp.int32, (tm, tm), 0)).astype(BF16)
    before = jnp.dot(tri, onehot.astype(BF16), preferred_element_type=F32) + carry_ref[...]
    rank1 = jnp.sum(jnp.where(oh1, before, 0.0), axis=-1, keepdims=True)
    rank2 = jnp.sum(jnp.where(oh2, before, 0.0), axis=-1, keepdims=True)
    total = carry_ref[...] + jnp.sum(onehot, axis=0, keepdims=True)
    carry_ref[...] = total
    cnt_ref[...] = total
    route = jnp.where(lane == 0, i1, 0.0)
    for idx, val in ((1, i2), (2, gate1), (3, gate2), (4, rank1), (5, rank2)):
        route = jnp.where(lane == idx, val, route)
    route_ref[...] = route


def _out_router(oa, ob, oc, h, w_out, g, w_rt, b_rt):
    n = h.shape[0]
    w_rt_hi = w_rt.astype(BF16)
    w_rt_lo = (w_rt - w_rt_hi.astype(F32)).astype(BF16)
    tm = ROUTER_TILE
    row = lambda w: pl.BlockSpec((tm, w), lambda i: (i, 0))
    const = lambda r, c: pl.BlockSpec((r, c), lambda i: (0, 0))
    return pl.pallas_call(
        _out_router_kernel,
        grid=(n // tm,),
        in_specs=[row(HG_WIDTH), row(DA_WIDTH), row(FX_WIDTH), row(D_MODEL),
                  const(D_MODEL, D_MODEL),
                  const(1, D_MODEL), const(D_MODEL, LANES), const(D_MODEL, LANES),
                  const(1, LANES)],
        out_specs=[row(D_MODEL), row(LANES), const(1, LANES)],
        out_shape=[jax.ShapeDtypeStruct((n, D_MODEL), F32),
                   jax.ShapeDtypeStruct((n, LANES), F32),
                   jax.ShapeDtypeStruct((1, LANES), F32)],
        scratch_shapes=[pltpu.VMEM((1, LANES), F32)],
        compiler_params=pltpu.CompilerParams(dimension_semantics=("arbitrary",)),
        name="out_router",
    )(oa, ob, oc, h, w_out, g, w_rt_hi, w_rt_lo, b_rt)


def _dispatch_kernel(dest_ref, unused_ref, h_ref, g_ref, xs_in_ref, xs_ref, back_ref,
                     u_ref, sem, *, tm, n, tb):
    del xs_in_ref
    i = pl.program_id(0)
    n_tiles = n // tm
    base = (i - 1) * tm
    cur = i % 2

    @pl.when(i == 0)
    def _():
        def fill(s, carry):
            parity = lax.shift_right_logical(s, tb.bit_length() - 1) & 1
            back_ref[s] = 2 * n + parity * tb + (s & (tb - 1))
            return carry

        n_ranges = unused_ref.shape[0] // 2

        def one_range(e, carry):
            return lax.fori_loop(unused_ref[e], unused_ref[n_ranges + e], fill, carry)

        lax.fori_loop(0, n_ranges, one_range, 0)

    def issue(r, carry):
        for k in range(2):
            src = k * n + base + r
            d = dest_ref[src]
            back_ref[d] = src
            pltpu.make_async_copy(u_ref.at[1 - cur, pl.ds(r, 1)], xs_ref.at[pl.ds(d, 1)],
                                  sem).start()
        return carry

    @pl.when(i >= 1)
    def _():
        lax.fori_loop(0, tm, issue, 0, unroll=8)

    @pl.when(i < n_tiles)
    def _():
        _rows_to_tiles(u_ref.at[cur], _rms(h_ref[...], g_ref[...]))

    @pl.when(i >= 1)
    def _():
        for _ in range(2):
            pltpu.make_async_copy(u_ref.at[1 - cur], xs_ref.at[pl.ds(0, tm)], sem).wait()


def _dispatch(dest, unused, h1, g, xs_init):
    n = h1.shape[0]
    tm = ROW_TILE
    p_rows = xs_init.shape[0]
    return pl.pallas_call(
        functools.partial(_dispatch_kernel, tm=tm, n=n, tb=MOE_BLOCK),
        grid_spec=pltpu.PrefetchScalarGridSpec(
            num_scalar_prefetch=2,
            grid=(n // tm + 1,),
            in_specs=[pl.BlockSpec((tm, D_MODEL),
                                   lambda i, d, un: (jnp.minimum(i, n // tm - 1), 0)),
                      pl.BlockSpec((1, D_MODEL), lambda i, d, un: (0, 0)),
                      pl.BlockSpec(memory_space=pl.ANY)],
            out_specs=[pl.BlockSpec(memory_space=pl.ANY),
                       pl.BlockSpec(memory_space=pltpu.SMEM)],
            scratch_shapes=[pltpu.VMEM((2, tm) + SLOT_ROW, F32),
                            pltpu.SemaphoreType.DMA(())]),
        out_shape=[jax.ShapeDtypeStruct(xs_init.shape, xs_init.dtype),
                   jax.ShapeDtypeStruct((p_rows,), jnp.int32)],
        input_output_aliases={4: 0},
        compiler_params=pltpu.CompilerParams(dimension_semantics=("arbitrary",)),
        name="moe_dispatch",
    )(dest, unused, h1, g, xs_init)


def _expert_kernel(be_ref, na_ref, back_ref, x_ref, w1_ref, w3_ref, w2_ref, out_ref,
                   y_ref, w1b_ref, w3b_ref, w2b_ref, sem, *, tb):
    j = pl.program_id(0)
    na = na_ref[0]
    cur = j % 2

    def scatter(block, buf):
        for r in range(tb):
            pltpu.make_async_copy(y_ref.at[buf, pl.ds(r, 1)],
                                  out_ref.at[pl.ds(back_ref[block * tb + r], 1)],
                                  sem.at[buf]).start()

    def compute(buf):
        x = _tiles_to_rows(x_ref).astype(BF16)
        a = jnp.dot(x, w1b_ref[...], preferred_element_type=F32)
        b = jnp.dot(x, w3b_ref[...], preferred_element_type=F32)
        act = (a * _sigmoid(a) * b).astype(BF16)
        y_ref[buf] = jnp.dot(act, w2b_ref[...], preferred_element_type=F32)

    new_expert = (j == 0) | (be_ref[j] != be_ref[jnp.maximum(j - 1, 0)])

    @pl.when((j < na) & new_expert)
    def _():
        w1b_ref[...] = w1_ref[0].astype(BF16)
        w3b_ref[...] = w3_ref[0].astype(BF16)
        w2b_ref[...] = w2_ref[0].astype(BF16)

    @pl.when((j >= 2) & (j < na + 2))
    def _():
        pltpu.make_async_copy(y_ref.at[cur], out_ref.at[pl.ds(0, tb)], sem.at[cur]).wait()

    @pl.when(j == 0)
    def _():
        y_ref[...] = jnp.zeros_like(y_ref)
        first_spare = out_ref.shape[0] - 2 * tb
        spare = [pltpu.make_async_copy(
            y_ref.at[b], out_ref.at[pl.ds(first_spare + b * tb, tb)], sem.at[b])
            for b in range(2)]
        for copy in spare:
            copy.start()
        for copy in spare:
            copy.wait()
        compute(cur)

    @pl.when((j >= 1) & (j < na))
    def _():
        scatter(j - 1, 1 - cur)
        compute(cur)

    @pl.when(j == na)
    def _():
        scatter(j - 1, 1 - cur)


def _experts(blk_expert, n_active, slot_back, xs, w1, w3, w2, layer, n):
    p_rows = xs.shape[0]
    tb = MOE_BLOCK
    last = lambda j, na: jnp.maximum(jnp.minimum(j, na[0] - 1), 0)
    w_spec = lambda r, c: pl.BlockSpec((None, 1, r, c),
                                       lambda j, be, na, back: (layer, be[j], 0, 0))
    return pl.pallas_call(
        functools.partial(_expert_kernel, tb=tb),
        grid_spec=pltpu.PrefetchScalarGridSpec(
            num_scalar_prefetch=3,
            grid=(p_rows // tb + 1,),
            in_specs=[pl.BlockSpec((tb,) + SLOT_ROW,
                                   lambda j, be, na, back: (last(j, na), 0, 0)),
                      w_spec(D_MODEL, D_EXPERT), w_spec(D_MODEL, D_EXPERT),
                      w_spec(D_EXPERT, D_MODEL)],
            out_specs=pl.BlockSpec(memory_space=pl.ANY),
            scratch_shapes=[pltpu.VMEM((2, tb, D_MODEL), F32),
                            pltpu.VMEM((D_MODEL, D_EXPERT), BF16),
                            pltpu.VMEM((D_MODEL, D_EXPERT), BF16),
                            pltpu.VMEM((D_EXPERT, D_MODEL), BF16),
                            pltpu.SemaphoreType.DMA((2,))]),
        out_shape=jax.ShapeDtypeStruct((2 * n + 2 * tb, D_MODEL), F32),
        compiler_params=pltpu.CompilerParams(dimension_semantics=("arbitrary",),
                                             vmem_limit_bytes=VMEM_LIMIT),
        name="moe_experts",
    )(blk_expert, n_active, slot_back, xs, w1, w3, w2)


def _combine_kernel(h_ref, route_ref, gfin_ref, y0_ref, y1_ref, o_ref, *, final):
    route = route_ref[...]
    out = h_ref[...] + route[:, 2:3] * y0_ref[...] + route[:, 3:4] * y1_ref[...]
    if final:
        out = _rms(out, gfin_ref[...])
    o_ref[...] = out


def _combine(h1, route, g_final, ys, batch, seq_len, final):
    n = h1.shape[0]
    tm = ROW_TILE
    tiles_per_seq = seq_len // tm
    skip_tiles = (LPAD + N_META) // tm if final else 0
    out_tiles = tiles_per_seq - skip_tiles
    tile = lambda b, i: b * tiles_per_seq + skip_tiles + i
    in_row = lambda w: pl.BlockSpec((tm, w), lambda b, i: (tile(b, i), 0))
    y_rows = lambda k: pl.BlockSpec((tm, D_MODEL),
                                    lambda b, i: (k * (n // tm) + tile(b, i), 0))
    return pl.pallas_call(
        functools.partial(_combine_kernel, final=final),
        grid=(batch, out_tiles),
        in_specs=[in_row(D_MODEL), in_row(LANES),
                  pl.BlockSpec((1, D_MODEL), lambda b, i: (0, 0)),
                  y_rows(0), y_rows(1)],
        out_specs=pl.BlockSpec((tm, D_MODEL), lambda b, i: (b * out_tiles + i, 0)),
        out_shape=jax.ShapeDtypeStruct((batch * out_tiles * tm, D_MODEL), F32),
        compiler_params=pltpu.CompilerParams(dimension_semantics=("arbitrary", "arbitrary")),
        name="moe_combine",
    )(h1, route, g_final, ys, ys)


def _rope_partner(m):
    lead = m.shape[:-1]
    x = m.reshape(lead + (DA_WIDTH // DA_DQK, 2, DA_DQK // 2))
    return x[..., ::-1, :].reshape(lead + (DA_WIDTH,))


def _rope_tables(seq_len):
    pos = (jnp.arange(seq_len) - LPAD).astype(F32)
    inv = ROPE_THETA ** (-jnp.arange(0, DA_DQK, 2, dtype=F32) / DA_DQK)
    ang = pos[:, None] * inv[None, :]
    comps = DA_DV // DA_DQK
    cos = jnp.tile(jnp.cos(ang), (1, 2 * comps))
    sin = jnp.tile(jnp.concatenate([-jnp.sin(ang), jnp.sin(ang)], axis=1), (1, comps))
    q_scale = DA_DQK ** -0.5 * LOG2E
    two = lambda t: jnp.tile(t, (1, LANES // DA_DV))
    lay = lambda t: jnp.concatenate([two(t * q_scale), two(t)], axis=1)
    return lay(cos), lay(sin)


def _in_weights(w):
    hq, hf, hi, hg = (w[:, i * 512:(i + 1) * 512] for i in range(4))
    dq, dk, dv = (w[:, 2048 + i * 256:2048 + (i + 1) * 256] for i in range(3))
    fq, fk, fv = (w[:, 2816 + i * 256:2816 + (i + 1) * 256] for i in range(3))
    ff = w[:, 3584:3588]
    cat = jnp.concatenate(
        [hq, hi, hg, hf,
         dq, dk, _rope_partner(dq), _rope_partner(dk),
         fq * (FX_DH ** -0.5), fk,
         ff, jnp.zeros((D_MODEL, LANES - FX_HEADS), w.dtype)],
        axis=1)
    w_vt = jnp.concatenate([dv, fv], axis=1).T
    return cat.astype(BF16), w_vt.astype(BF16)


def _pad_lanes(v, width=LANES):
    return jnp.zeros((1, width), F32).at[0, :v.shape[0]].set(v.astype(F32))


def kernel(x, meta_tokens, norm_mix, w_in, hgrn_lb, hgrn_norm, diff_lambda, diff_norm,
           fox_bias, fox_norm, w_out, norm_ffn, w_group, b_group, w_router, b_router,
           w1, w3, w2, norm_final):
    batch, seq, d = x.shape
    depth = w_in.shape[0]
    seq_len = LPAD + N_META + seq
    n = batch * seq_len
    pad = jnp.zeros((batch, LPAD, d), x.dtype)
    meta = jnp.broadcast_to(meta_tokens.astype(x.dtype)[None], (batch, N_META, d))
    h = jnp.concatenate([pad, meta, x], axis=1).reshape(n, d)

    s_lb = jax.nn.softmax(hgrn_lb.astype(F32), axis=0)
    lb_all = jnp.cumsum(s_lb, axis=0) - s_lb[0]
    cos_t, sin_t = _rope_tables(seq_len)

    tb = MOE_BLOCK
    n_blocks = (2 * n) // tb + N_EXPERTS
    p_rows = n_blocks * tb

    xs = jnp.zeros((p_rows,) + SLOT_ROW, F32)
    for layer in range(depth):
        lam_init = 0.8 - 0.6 * math.exp(-0.3 * layer)
        w_cat, w_vt = _in_weights(w_in[layer])
        hqig, hlf, hkk, dqk, fqk, dvt, fvt = _in_proj(
            h, norm_mix[layer][None, :], w_cat, w_vt, cos_t, sin_t,
            _pad_lanes(fox_bias[layer]), lb_all[layer][None, :], batch, seq_len)

        o_a = _hgrn(hqig, hlf, hkk, hgrn_norm[layer][None, :], batch, seq_len)

        lam_vecs = jnp.zeros((8, LANES), F32).at[:4, :DA_DQK].set(diff_lambda[layer].astype(F32))
        o_b = _diff_attn(dqk, dvt, lam_vecs, _pad_lanes(diff_norm[layer]), lam_init,
                         batch, seq_len)

        o_c = _fox_attn(fqk, fvt, _pad_lanes(fox_norm[layer]), batch, seq_len)

        w_rt = jnp.concatenate(
            [w_router[layer], w_group[layer],
             jnp.zeros((d, LANES - N_EXPERTS - N_GROUPS), F32)], axis=1)
        b_rt = _pad_lanes(jnp.concatenate([b_router[layer], b_group[layer]]))
        h1, route, counts = _out_router(
            o_a, o_b, o_c, h, w_out[layer].astype(BF16), norm_ffn[layer][None, :], w_rt, b_rt)

        cnt = counts[0, :N_EXPERTS].astype(jnp.int32)
        padded = (cnt + tb - 1) // tb * tb
        p_end = jnp.cumsum(padded)
        p_start = p_end - padded
        ids = route[:, 0:2].astype(jnp.int32)
        ranks = route[:, 4:6].astype(jnp.int32)
        dest = (p_start[ids] + ranks).T.reshape(2 * n)
        blk_start = jnp.arange(n_blocks + 1, dtype=jnp.int32) * tb
        blk_expert = jnp.minimum(
            jnp.sum((p_end[None, :] <= blk_start[:, None]).astype(jnp.int32), axis=1),
            N_EXPERTS - 1)
        n_active = (p_end[-1:] // tb).astype(jnp.int32)

        unused = jnp.concatenate(
            [p_start + cnt, p_end[-1:], p_end, jnp.full((1,), p_rows)]).astype(jnp.int32)
        xs, slot_back = _dispatch(dest, unused, h1, norm_ffn[layer][None, :], xs)
        ys = _experts(blk_expert, n_active, slot_back, xs, w1, w3, w2, layer, n)
        h = _combine(h1, route, norm_final[None, :], ys, batch, seq_len,
                     final=(layer == depth - 1))

    return h.reshape(batch, seq, d)
```

```python
import functools
import math

import jax
import jax.numpy as jnp
from jax import lax
from jax.experimental import pallas as pl
from jax.experimental.pallas import tpu as pltpu

F32 = jnp.float32
BF16 = jnp.bfloat16

D_MODEL = 1024
N_META = 16
HG_HEADS = 4
HG_DK = 128
HG_WIDTH = 512
HG_CHUNK = 64
HG_SUB = 8
HG_PER_STEP = 4
DA_HEADS = 4
DA_DV = 64
DA_DQK = 32
DA_WIDTH = 256
FX_HEADS = 4
FX_DH = 64
FX_WIDTH = 256
N_GROUPS = 4
EXPERTS_PER_GROUP = 8
N_EXPERTS = 32
D_EXPERT = 512
ROPE_THETA = 10000.0
EPS = 1e-6
NEG = -1e30
TINY = 1e-30
LOG2E = 1.4426950408889634

LANES = 128
SLOT = LANES
HEAD_W = 4 * SLOT
ROW_TILE = 256
PROJ_TILE = 512
ROUTER_TILE = 512
CUMSUM_ROWS = 128
ATTN_BLOCK = 512
ATTN_HEADS_PER_STEP = 4
MOE_BLOCK = 256
LPAD = ATTN_BLOCK - N_META
FIRST_KEYS = LANES
assert ATTN_BLOCK - FIRST_KEYS <= LPAD
ONE_LANE = 64
VMEM_LIMIT = 56 * 1024 * 1024

_C_HQIG = (0, 1536)
_C_HF = (1536, 2048)
_C_DQK = (2048, 2560)
_C_DQKP = (2560, 3072)
_C_FQK = (3072, 3584)
_C_FF = (3584, 3712)
IN_COLS = 3712

_NT = (((1,), (1,)), ((), ()))
_TN = (((0,), (0,)), ((), ()))


def _sigmoid(x):
    return 1.0 / (1.0 + jnp.exp(-x))


def _rms(x, g):
    return x * lax.rsqrt(jnp.mean(x * x, axis=-1, keepdims=True) + EPS) * g


def _in_proj_kernel(h_ref, g_ref, w_ref, wvt_ref, cos_ref, sin_ref, fb_ref, lb_ref,
                    hqig_ref, hlf_ref, hkk_ref, dqk_ref, fqk_ref, dvt_ref, fvt_ref,
                    carry_ref, *, tiles_per_seq):
    u = _rms(h_ref[...], g_ref[...]).astype(BF16)
    tm = u.shape[0]

    def mm(c):
        return jnp.dot(u, w_ref[:, c[0]:c[1]], preferred_element_type=F32)

    hg = mm(_C_HQIG)
    q_raw, g_raw = hg[:, 0:HG_WIDTH], hg[:, 2 * HG_WIDTH:3 * HG_WIDTH]
    hqig_ref[:, 0:HG_WIDTH] = (q_raw * _sigmoid(q_raw) * (HG_DK ** -0.5)).astype(BF16)
    hqig_ref[:, HG_WIDTH:2 * HG_WIDTH] = hg[:, HG_WIDTH:2 * HG_WIDTH].astype(BF16)
    hqig_ref[:, 2 * HG_WIDTH:3 * HG_WIDTH] = (g_raw * _sigmoid(g_raw)).astype(BF16)
    lb = lb_ref[...]
    sig = _sigmoid(mm(_C_HF))
    row = ((pl.program_id(0) % tiles_per_seq) * tm
           + lax.broadcasted_iota(jnp.int32, (tm, 1), 0))
    valid = row >= LPAD
    hlf_ref[...] = jnp.where(
        valid, jnp.log2(jnp.maximum(lb + (1.0 - lb) * sig, TINY)), 0.0)
    hkk_ref[...] = jnp.where(valid, (1.0 - lb) * (1.0 - sig), 0.0)
    half_lane = lax.broadcasted_iota(jnp.int32, (tm, LANES), 1) < DA_DV

    def spread(x):
        tiles = []
        for t in range(x.shape[1] // LANES):
            tile = x[:, t * LANES:(t + 1) * LANES]
            tiles.append(jnp.where(half_lane, tile, 0.0))
            tiles.append(jnp.where(half_lane, pltpu.roll(tile, LANES // 2, axis=1), 0.0))
        return jnp.concatenate(tiles, axis=1)

    def per_tile(t_ref):
        return jnp.concatenate([t_ref[:, 0:LANES]] * 2 + [t_ref[:, LANES:2 * LANES]] * 2, axis=1)

    rot = mm(_C_DQK) * per_tile(cos_ref) + mm(_C_DQKP) * per_tile(sin_ref)
    dqk_ref[:, 0:HEAD_W] = spread(rot[:, 0:DA_WIDTH]).astype(BF16)
    dqk_ref[:, HEAD_W:2 * HEAD_W] = spread(rot[:, DA_WIDTH:2 * DA_WIDTH]).astype(BF16)

    @pl.when(pl.program_id(0) % tiles_per_seq == 0)
    def _():
        carry_ref[...] = jnp.zeros_like(carry_ref)

    x = mm(_C_FF) + fb_ref[...]
    lf = jnp.minimum(x, 0.0) - jnp.log(1.0 + jnp.exp(-jnp.abs(x)))
    tri = (lax.broadcasted_iota(jnp.int32, (CUMSUM_ROWS, CUMSUM_ROWS), 1)
           <= lax.broadcasted_iota(jnp.int32, (CUMSUM_ROWS, CUMSUM_ROWS), 0)).astype(BF16)
    lf_terms = [t.astype(BF16) for t in _bf16_split3(lf)]
    total = carry_ref[...]
    groups = []
    for r0 in range(0, tm, CUMSUM_ROWS):
        part = sum(jnp.dot(tri, t[r0:r0 + CUMSUM_ROWS], preferred_element_type=F32)
                   for t in lf_terms) + total
        total = part[CUMSUM_ROWS - 1:CUMSUM_ROWS, :]
        groups.append(part)
    cs = jnp.concatenate(groups, axis=0)
    carry_ref[...] = total
    parts = _bf16_split3(cs * LOG2E)
    fqk = mm(_C_FQK)
    fq = spread(fqk[:, 0:FX_WIDTH]) * LOG2E
    fk = spread(fqk[:, FX_WIDTH:2 * FX_WIDTH])
    lane = lax.broadcasted_iota(jnp.int32, (tm, SLOT), 1)
    d0 = FX_DH
    for h in range(FX_HEADS):
        q = fq[:, h * SLOT:(h + 1) * SLOT]
        k = fk[:, h * SLOT:(h + 1) * SLOT]
        for i, part in enumerate(parts):
            col = part[:, h:h + 1]
            q = jnp.where(lane == d0 + i, col, q)
            k = jnp.where(lane == d0 + 3 + i, -col, k)
        q = jnp.where((lane >= d0 + 3) & (lane < d0 + 6), 1.0, q)
        k = jnp.where((lane >= d0) & (lane < d0 + 3), 1.0, k)
        fqk_ref[:, h * SLOT:(h + 1) * SLOT] = q.astype(BF16)
        fqk_ref[:, HEAD_W + h * SLOT:HEAD_W + (h + 1) * SLOT] = k.astype(BF16)

    vt = lax.dot_general(wvt_ref[...], u, _NT, preferred_element_type=F32).astype(BF16)
    spare = SLOT - DA_DV
    ones_then_zeros = jnp.where(lax.broadcasted_iota(jnp.int32, (spare, tm), 0) == 0,
                                1.0, 0.0).astype(BF16)
    for h in range(4):
        for ref, first in ((dvt_ref, 0), (fvt_ref, DA_WIDTH)):
            ref[0, h, 0, 0:DA_DV, :] = vt[first + h * DA_DV:first + (h + 1) * DA_DV]
            ref[0, h, 0, DA_DV:SLOT, :] = ones_then_zeros


def _in_proj(h, g, w_cat, w_vt, cos_t, sin_t, fox_bias_row, lb_row, batch, seq_len):
    n = h.shape[0]
    tm = PROJ_TILE
    assert tm == ATTN_BLOCK
    nk = seq_len // tm
    row = lambda w: pl.BlockSpec((tm, w), lambda i: (i, 0))
    tab = pl.BlockSpec((tm, 2 * SLOT), lambda i: (i % nk, 0))
    once = lambda r, c: pl.BlockSpec((r, c), lambda i: (0, 0), pipeline_mode=pl.Buffered(1))
    widths = (3 * HG_WIDTH, HG_WIDTH, HG_WIDTH, 2 * HEAD_W, 2 * HEAD_W)
    dtypes = (BF16, F32, F32, BF16, BF16)
    vt_spec = pl.BlockSpec((1, 4, 1, SLOT, tm), lambda i: (i // nk, 0, i % nk, 0, 0))
    vt_shape = jax.ShapeDtypeStruct((batch, 4, nk, SLOT, tm), BF16)
    return pl.pallas_call(
        functools.partial(_in_proj_kernel, tiles_per_seq=nk),
        grid=(n // tm,),
        in_specs=[row(D_MODEL),
                  pl.BlockSpec((1, D_MODEL), lambda i: (0, 0)),
                  once(D_MODEL, IN_COLS), once(DA_WIDTH + FX_WIDTH, D_MODEL),
                  tab, tab,
                  pl.BlockSpec((1, LANES), lambda i: (0, 0)),
                  pl.BlockSpec((1, HG_WIDTH), lambda i: (0, 0))],
        out_specs=[row(w) for w in widths] + [vt_spec, vt_spec],
        out_shape=[jax.ShapeDtypeStruct((n, w), t) for w, t in zip(widths, dtypes)]
        + [vt_shape, vt_shape],
        scratch_shapes=[pltpu.VMEM((1, LANES), F32)],
        compiler_params=pltpu.CompilerParams(dimension_semantics=("arbitrary",),
                                             vmem_limit_bytes=VMEM_LIMIT),
        name="in_proj",
    )(h, g, w_cat, w_vt, cos_t, sin_t, fox_bias_row, lb_row)


def _bf16_split3(x):
    hi = x.astype(BF16).astype(F32)
    r = x - hi
    mid = r.astype(BF16).astype(F32)
    return hi, mid, r - mid


def _softmax_block(s, m_prev):
    m_new = jnp.maximum(m_prev, jnp.max(s, axis=0, keepdims=True))
    alpha = jnp.exp2(m_prev - m_new)
    p = jnp.exp2(s - m_new).astype(BF16)
    return p, alpha, m_new


def _block_start(j, blk):
    return j * blk if isinstance(j, int) else pl.multiple_of(j * blk, blk)


def _causal_valid(i, j, blk, first_row=0):
    k_idx = j * blk + first_row + lax.broadcasted_iota(jnp.int32, (blk - first_row, blk), 0)
    q_idx = i * blk + lax.broadcasted_iota(jnp.int32, (blk - first_row, blk), 1)
    return (k_idx <= q_idx) & (k_idx >= LPAD)


def _pipelined_key_blocks(i, first, scores, consume):
    first()

    @pl.when(i >= 1)
    def _():
        scores(1, 0)

    def pair(t, carry):
        scores(2 * t + 2, 1)
        consume(2 * t + 1, 0, False)
        scores(2 * t + 3, 0)
        consume(2 * t + 2, 1, False)
        return carry

    lax.fori_loop(0, lax.shift_right_logical(jnp.maximum(i - 1, 0), 1), pair, 0)
    odd = (i & 1) == 1

    @pl.when(odd)
    def _():
        consume(i, 0, True)

    @pl.when((i >= 2) & jnp.logical_not(odd))
    def _():
        scores(i, 1)
        consume(i - 1, 0, False)
        consume(i, 1, True)


def _head_out(acc):
    lane = lax.broadcasted_iota(jnp.int32, acc.shape, 1)
    return jnp.where(lane < ONE_LANE, acc / acc[:, ONE_LANE:ONE_LANE + 1], 0.0)


def _store_head_pairs(o_ref, outs):
    for p in range(len(outs) // 2):
        pair = outs[2 * p] + pltpu.roll(outs[2 * p + 1], ONE_LANE, axis=1)
        o_ref[:, p * LANES:(p + 1) * LANES] = pair.astype(BF16)


def _head_rms(o, g):
    ms = jnp.sum(o * o, axis=-1, keepdims=True) * (1.0 / ONE_LANE)
    return o * lax.rsqrt(ms + EPS) * g


def _fox_attn_kernel(q_ref, k_ref, vt_ref, g_ref, o_ref, m_ref, acc_ref, sa_ref, sb_ref,
                     *, blk):
    i = pl.program_id(2)
    heads = range(ATTN_HEADS_PER_STEP)
    slot = [slice(hh * SLOT, (hh + 1) * SLOT) for hh in heads]
    q = [q_ref[:, slot[hh]] for hh in heads]
    bufs = (sa_ref, sb_ref)
    m_ref[...] = jnp.full(m_ref.shape, NEG, F32)
    acc_ref[...] = jnp.zeros_like(acc_ref)

    def scores(j, buf):
        rows = pl.ds(_block_start(j, blk), blk)
        for hh in heads:
            bufs[buf][hh] = lax.dot_general(k_ref[rows, slot[hh]], q[hh], _NT,
                                            preferred_element_type=F32)

    def consume(j, buf, masked):
        mask = _causal_valid(i, j, blk) if masked else None
        for hh in heads:
            s = bufs[buf][hh]
            if masked:
                s = jnp.where(mask, s, NEG)
            p, alpha, m_new = _softmax_block(s, m_ref[hh])
            acc_ref[hh] = alpha * acc_ref[hh] + jnp.dot(
                vt_ref[0, hh, j], p, preferred_element_type=F32)
            m_ref[hh] = m_new

    def first():
        lo = blk - FIRST_KEYS
        mask = _causal_valid(i, 0, blk, lo)
        for hh in heads:
            s = lax.dot_general(k_ref[lo:blk, slot[hh]], q[hh], _NT,
                                preferred_element_type=F32)
            p, alpha, m_new = _softmax_block(jnp.where(mask, s, NEG), m_ref[hh])
            acc_ref[hh] = alpha * acc_ref[hh] + jnp.dot(
                vt_ref[0, hh, 0, :, lo:blk], p, preferred_element_type=F32)
            m_ref[hh] = m_new

    _pipelined_key_blocks(i, first, scores, consume)
    _store_head_pairs(
        o_ref, [_head_rms(_head_out(acc_ref[hh].T), g_ref[...]) for hh in heads])


def _key_spec(seq_len, groups):
    return pl.BlockSpec((seq_len, ATTN_HEADS_PER_STEP * SLOT),
                        lambda b, h, i: (b, groups + h), pipeline_mode=pl.Buffered(1))


def _value_spec(seq_len):
    nk = seq_len // ATTN_BLOCK
    return pl.BlockSpec((1, ATTN_HEADS_PER_STEP, nk, SLOT, ATTN_BLOCK),
                        lambda b, h, i: (b, h, 0, 0, 0), pipeline_mode=pl.Buffered(1))


def _fox_attn(fqk, fvt, gain, batch, seq_len):
    n = fqk.shape[0]
    blk = ATTN_BLOCK
    nq = seq_len // blk
    hps = ATTN_HEADS_PER_STEP
    groups = FX_HEADS // hps
    return pl.pallas_call(
        functools.partial(_fox_attn_kernel, blk=blk),
        grid=(batch, groups, nq),
        in_specs=[pl.BlockSpec((blk, hps * SLOT), lambda b, h, i: (b * nq + i, h)),
                  _key_spec(seq_len, groups),
                  _value_spec(seq_len),
                  pl.BlockSpec((1, SLOT), lambda b, h, i: (0, 0))],
        out_specs=pl.BlockSpec((blk, hps * ONE_LANE), lambda b, h, i: (b * nq + i, h)),
        out_shape=jax.ShapeDtypeStruct((n, 4 * ONE_LANE), BF16),
        scratch_shapes=[pltpu.VMEM((hps, 1, blk), F32), pltpu.VMEM((hps, SLOT, blk), F32),
                        pltpu.VMEM((hps, blk, blk), F32), pltpu.VMEM((hps, blk, blk), F32)],
        compiler_params=pltpu.CompilerParams(
            dimension_semantics=("arbitrary", "arbitrary", "arbitrary"),
            vmem_limit_bytes=VMEM_LIMIT),
        name="fox_attn",
    )(fqk, fqk, fvt, gain)


def _diff_attn_kernel(q_ref, k_ref, vt_ref, lam_ref, g_ref, o_ref, m_ref, acc_ref,
                      sa_ref, sb_ref, *, blk, lam_init):
    i = pl.program_id(2)
    heads = range(ATTN_HEADS_PER_STEP)
    slot = lambda s: slice(s * SLOT, (s + 1) * SLOT)
    lane = lax.broadcasted_iota(jnp.int32, (blk, SLOT), 1)
    q = []
    for hh in heads:
        q_h = q_ref[:, slot(hh)]
        q.append([jnp.where((lane >= c * DA_DQK) & (lane < (c + 1) * DA_DQK), q_h,
                            jnp.zeros_like(q_h)) for c in range(2)])
    bufs = (sa_ref, sb_ref)
    m_ref[...] = jnp.full(m_ref.shape, NEG, F32)
    acc_ref[...] = jnp.zeros_like(acc_ref)

    def scores(j, buf):
        rows = pl.ds(_block_start(j, blk), blk)
        for hh in heads:
            k_h = k_ref[rows, slot(hh)]
            for c in range(2):
                bufs[buf][2 * hh + c] = lax.dot_general(k_h, q[hh][c], _NT,
                                                        preferred_element_type=F32)

    def consume(j, buf, masked):
        mask = _causal_valid(i, j, blk) if masked else None
        for hh in heads:
            ps, alphas = [], []
            for c in range(2):
                hc = 2 * hh + c
                s = bufs[buf][hc]
                if masked:
                    s = jnp.where(mask, s, NEG)
                p, alpha, m_new = _softmax_block(s, m_ref[hc])
                m_ref[hc] = m_new
                ps.append(p)
                alphas.append(alpha)
            pv = jnp.dot(vt_ref[0, hh, j], jnp.concatenate(ps, axis=1),
                         preferred_element_type=F32)
            for c in range(2):
                hc = 2 * hh + c
                acc_ref[hc] = alphas[c] * acc_ref[hc] + pv[:, c * blk:(c + 1) * blk]

    def first():
        lo = blk - FIRST_KEYS
        mask = _causal_valid(i, 0, blk, lo)
        for hh in heads:
            k_h = k_ref[lo:blk, slot(hh)]
            ps, alphas = [], []
            for c in range(2):
                hc = 2 * hh + c
                s = lax.dot_general(k_h, q[hh][c], _NT, preferred_element_type=F32)
                p, alpha, m_new = _softmax_block(jnp.where(mask, s, NEG), m_ref[hc])
                m_ref[hc] = m_new
                ps.append(p)
                alphas.append(alpha)
            pv = jnp.dot(vt_ref[0, hh, 0, :, lo:blk], jnp.concatenate(ps, axis=1),
                         preferred_element_type=F32)
            for c in range(2):
                hc = 2 * hh + c
                acc_ref[hc] = alphas[c] * acc_ref[hc] + pv[:, c * blk:(c + 1) * blk]

    _pipelined_key_blocks(i, first, scores, consume)
    lv = lam_ref[...]
    lam = (jnp.exp(jnp.sum(lv[0:1] * lv[1:2], axis=-1, keepdims=True))
           - jnp.exp(jnp.sum(lv[2:3] * lv[3:4], axis=-1, keepdims=True)) + lam_init)
    outs = []
    for hh in heads:
        o = _head_out(acc_ref[2 * hh].T) - lam * _head_out(acc_ref[2 * hh + 1].T)
        outs.append(_head_rms(o, g_ref[...]) * (1.0 - lam_init))
    _store_head_pairs(o_ref, outs)


def _diff_attn(dqk, dvt, lam_vecs, gain, lam_init, batch, seq_len):
    n = dqk.shape[0]
    blk = ATTN_BLOCK
    nq = seq_len // blk
    hps = ATTN_HEADS_PER_STEP
    groups = DA_HEADS // hps
    return pl.pallas_call(
        functools.partial(_diff_attn_kernel, blk=blk, lam_init=lam_init),
        grid=(batch, groups, nq),
        in_specs=[pl.BlockSpec((blk, hps * SLOT), lambda b, h, i: (b * nq + i, h)),
                  _key_spec(seq_len, groups),
                  _value_spec(seq_len),
                  pl.BlockSpec((8, LANES), lambda b, h, i: (0, 0)),
                  pl.BlockSpec((1, SLOT), lambda b, h, i: (0, 0))],
        out_specs=pl.BlockSpec((blk, hps * ONE_LANE), lambda b, h, i: (b * nq + i, h)),
        out_shape=jax.ShapeDtypeStruct((n, 4 * ONE_LANE), BF16),
        scratch_shapes=[pltpu.VMEM((2 * hps, 1, blk), F32),
                        pltpu.VMEM((2 * hps, SLOT, blk), F32),
                        pltpu.VMEM((2 * hps, blk, blk), F32),
                        pltpu.VMEM((2 * hps, blk, blk), F32)],
        compiler_params=pltpu.CompilerParams(
            dimension_semantics=("arbitrary", "arbitrary", "arbitrary"),
            vmem_limit_bytes=VMEM_LIMIT),
        name="diff_attn",
    )(dqk, dqk, dvt, lam_vecs, gain)


def _hgrn_kernel(q_ref, i_ref, g_ref, lf_ref, kk_ref, gain_ref, o_ref,
                 st_ref, hs_ref, *, tl):
    @pl.when(pl.program_id(2) == 0)
    def _():
        st_ref[...] = jnp.zeros_like(st_ref)

    c_rows, sub = HG_CHUNK, HG_SUB
    tri = (lax.broadcasted_iota(jnp.int32, (c_rows, c_rows), 1)
           <= lax.broadcasted_iota(jnp.int32, (c_rows, c_rows), 0)).astype(BF16)
    t_sub = lax.broadcasted_iota(jnp.int32, (sub, 1), 0)
    lane_c = lax.broadcasted_iota(jnp.int32, (sub, c_rows), 1)

    t_all = lax.broadcasted_iota(jnp.int32, (c_rows, c_rows), 0)
    s_all = lax.broadcasted_iota(jnp.int32, (c_rows, c_rows), 1)
    level_masks = []
    g = sub
    while g < c_rows:
        level_masks.append(((t_all // g) == (s_all // g) + 1) & ((s_all // g) % 2 == 0))
        g *= 2

    heads = range(HG_PER_STEP)
    cols = [slice(hh * HG_DK, (hh + 1) * HG_DK) for hh in heads]

    def chunk(c, carry):
        r0 = pl.multiple_of(c * c_rows, c_rows)
        rows = pl.ds(r0, c_rows)
        G, kk, qs, v, st, o_inter = [], [], [], [], [], []
        for hh in heads:
            kk.append(kk_ref[rows, cols[hh]])
            G.append(sum(jnp.dot(tri, t.astype(BF16), preferred_element_type=F32)
                         for t in _bf16_split3(lf_ref[rows, cols[hh]])))
            qs.append(q_ref[rows, cols[hh]].astype(F32))
            v.append(i_ref[rows, cols[hh]])
            hs_ref[hh] = G[hh] - jnp.log2(kk[hh])
            st.append(st_ref[hh])
            o_inter.append(lax.dot_general(
                (qs[hh] * jnp.exp2(G[hh])).astype(BF16), st[hh].astype(BF16), _NT,
                preferred_element_type=F32))
        a_mat = [jnp.zeros((c_rows, c_rows), F32) for _ in heads]
        g = sub
        for mask in level_masks:
            for hh in heads:
                ref = jnp.concatenate(
                    [jnp.broadcast_to(G[hh][p + g - 1:p + g], (2 * g, HG_DK))
                     for p in range(0, c_rows, 2 * g)], axis=0)
                e = jnp.exp2(-jnp.abs(G[hh] - ref))
                part = lax.dot_general((qs[hh] * e).astype(BF16), (kk[hh] * e).astype(BF16),
                                       _NT, preferred_element_type=F32)
                a_mat[hh] = jnp.where(mask, part, a_mat[hh])
            g *= 2
        a_rows = [[] for _ in heads]
        for b in range(c_rows // sub):
            lo = b * sub
            for hh in heads:
                q_b = qs[hh][lo:lo + sub]
                g_b = G[hh][lo:lo + sub]
                a_blk = a_mat[hh][lo:lo + sub]
                for s in range(lo, lo + sub):
                    y = q_b * jnp.exp2(g_b - hs_ref[hh, s:s + 1, :])
                    a_blk = jnp.where(lane_c == s, jnp.sum(y, axis=-1, keepdims=True), a_blk)
                a_rows[hh].append(jnp.where(lane_c <= t_sub + lo, a_blk, 0.0))
        for hh in heads:
            a_full = jnp.concatenate(a_rows[hh], axis=0).astype(BF16)
            o = o_inter[hh] + jnp.dot(a_full, v[hh], preferred_element_type=F32)
            g_last = G[hh][c_rows - 1:c_rows]
            kd = kk[hh] * jnp.exp2(g_last - G[hh])
            st_ref[hh] = st[hh] * jnp.exp2(g_last) + lax.dot_general(
                v[hh], kd.astype(BF16), _TN, preferred_element_type=F32)
            o = _rms(o, gain_ref[...]) * g_ref[rows, cols[hh]].astype(F32)
            o_ref[rows, cols[hh]] = o.astype(BF16)
        return carry

    lax.fori_loop(0, tl // c_rows, chunk, 0)


def _hgrn(hqig, hlf, hkk, gain, batch, seq_len):
    n = hqig.shape[0]
    tl = ROW_TILE
    nt = seq_len // tl
    w = HG_PER_STEP * HG_DK
    groups = HG_HEADS // HG_PER_STEP
    col = lambda off: pl.BlockSpec((tl, w), lambda b, h, j: (b * nt + j, off + h))
    return pl.pallas_call(
        functools.partial(_hgrn_kernel, tl=tl),
        grid=(batch, groups, nt),
        in_specs=[col(0), col(groups), col(2 * groups), col(0), col(0),
                  pl.BlockSpec((1, HG_DK), lambda b, h, j: (0, 0))],
        out_specs=col(0),
        out_shape=jax.ShapeDtypeStruct((n, HG_WIDTH), BF16),
        scratch_shapes=[pltpu.VMEM((HG_PER_STEP, HG_DK, HG_DK), F32),
                        pltpu.VMEM((HG_PER_STEP, HG_CHUNK, HG_DK), F32)],
        compiler_params=pltpu.CompilerParams(
            dimension_semantics=("arbitrary", "arbitrary", "arbitrary")),
        name="hgrn2",
    )(hqig, hqig, hqig, hlf, hkk, gain)


def _out_router_kernel(oa_ref, ob_ref, oc_ref, h_ref, wo_ref, g_ref, wrh_ref, wrl_ref, br_ref,
                       h1_ref, route_ref, cnt_ref, carry_ref):
    @pl.when(pl.program_id(0) == 0)
    def _():
        carry_ref[...] = jnp.zeros_like(carry_ref)

    h1 = (h_ref[...]
          + jnp.dot(oa_ref[...], wo_ref[0:512, :], preferred_element_type=F32)
          + jnp.dot(ob_ref[...], wo_ref[512:768, :], preferred_element_type=F32)
          + jnp.dot(oc_ref[...], wo_ref[768:1024, :], preferred_element_type=F32))
    h1_ref[...] = h1
    u = _rms(h1, g_ref[...])
    u_hi = u.astype(BF16)
    u_lo = (u - u_hi.astype(F32)).astype(BF16)
    logits = (jnp.dot(u_hi, wrh_ref[...], preferred_element_type=F32)
              + (jnp.dot(u_hi, wrl_ref[...], preferred_element_type=F32)
                 + jnp.dot(u_lo, wrh_ref[...], preferred_element_type=F32))
              + br_ref[...])
    tm = logits.shape[0]
    lane = lax.broadcasted_iota(jnp.int32, (tm, LANES), 1).astype(F32)
    big = float(LANES)
    is_g = (lane >= N_EXPERTS) & (lane < N_EXPERTS + N_GROUPS)
    gl = jnp.where(is_g, logits, -jnp.inf)
    gmax = jnp.max(gl, axis=-1, keepdims=True)
    gsel = jnp.min(jnp.where(gl == gmax, lane, big), axis=-1, keepdims=True) - N_EXPERTS
    p_g = 1.0 / jnp.sum(jnp.exp(gl - gmax), axis=-1, keepdims=True)
    lo = gsel * EXPERTS_PER_GROUP
    el = jnp.where((lane >= lo) & (lane < lo + EXPERTS_PER_GROUP), logits, -jnp.inf)
    m1 = jnp.max(el, axis=-1, keepdims=True)
    i1 = jnp.min(jnp.where(el == m1, lane, big), axis=-1, keepdims=True)
    el2 = jnp.where(lane == i1, -jnp.inf, el)
    m2 = jnp.max(el2, axis=-1, keepdims=True)
    i2 = jnp.min(jnp.where(el2 == m2, lane, big), axis=-1, keepdims=True)
    r = jnp.exp(m2 - m1)
    gate1 = p_g / (1.0 + r)
    gate2 = gate1 * r
    oh1 = lane == i1
    oh2 = lane == i2
    onehot = jnp.where(oh1 | oh2, 1.0, 0.0)
    tri = (lax.broadcasted_iota(jnp.int32, (tm, tm), 1)
           < lax.broadcasted_iota(jnp.int32, (tm, tm), 0)).astype(BF16)
    before = jnp.dot(tri, onehot.astype(BF16), preferred_element_type=F32) + carry_ref[...]
    rank1 = jnp.sum(jnp.where(oh1, before, 0.0), axis=-1, keepdims=True)
    rank2 = jnp.sum(jnp.where(oh2, before, 0.0), axis=-1, keepdims=True)
    total = carry_ref[...] + jnp.sum(onehot, axis=0, keepdims=True)
    carry_ref[...] = total
    cnt_ref[...] = total
    route = jnp.where(lane == 0, i1, 0.0)
    for idx, val in ((1, i2), (2, gate1), (3, gate2), (4, rank1), (5, rank2)):
        route = jnp.where(lane == idx, val, route)
    route_ref[...] = route


def _out_router(oa, ob, oc, h, w_out, g, w_rt, b_rt):
    n = h.shape[0]
    w_rt_hi = w_rt.astype(BF16)
    w_rt_lo = (w_rt - w_rt_hi.astype(F32)).astype(BF16)
    tm = ROUTER_TILE
    row = lambda w: pl.BlockSpec((tm, w), lambda i: (i, 0))
    const = lambda r, c: pl.BlockSpec((r, c), lambda i: (0, 0))
    return pl.pallas_call(
        _out_router_kernel,
        grid=(n // tm,),
        in_specs=[row(HG_WIDTH), row(DA_WIDTH), row(FX_WIDTH), row(D_MODEL),
                  const(D_MODEL, D_MODEL),
                  const(1, D_MODEL), const(D_MODEL, LANES), const(D_MODEL, LANES),
                  const(1, LANES)],
        out_specs=[row(D_MODEL), row(LANES), const(1, LANES)],
        out_shape=[jax.ShapeDtypeStruct((n, D_MODEL), F32),
                   jax.ShapeDtypeStruct((n, LANES), F32),
                   jax.ShapeDtypeStruct((1, LANES), F32)],
        scratch_shapes=[pltpu.VMEM((1, LANES), F32)],
        compiler_params=pltpu.CompilerParams(dimension_semantics=("arbitrary",)),
        name="out_router",
    )(oa, ob, oc, h, w_out, g, w_rt_hi, w_rt_lo, b_rt)


def _dispatch_kernel(dest_ref, unused_ref, h_ref, g_ref, xs_in_ref, xs_ref, back_ref,
                     u_ref, sem, *, tm, n, tb):
    del xs_in_ref
    i = pl.program_id(0)
    n_tiles = n // tm
    base = (i - 1) * tm
    cur = i % 2

    @pl.when(i == 0)
    def _():
        def fill(s, carry):
            parity = lax.shift_right_logical(s, tb.bit_length() - 1) & 1
            back_ref[s] = 2 * n + parity * tb + (s & (tb - 1))
            return carry

        n_ranges = unused_ref.shape[0] // 2

        def one_range(e, carry):
            return lax.fori_loop(unused_ref[e], unused_ref[n_ranges + e], fill, carry)

        lax.fori_loop(0, n_ranges, one_range, 0)

    def issue(r, carry):
        for k in range(2):
            src = k * n + base + r
            d = dest_ref[src]
            back_ref[d] = src
            pltpu.make_async_copy(u_ref.at[1 - cur, pl.ds(r, 1)], xs_ref.at[pl.ds(d, 1)],
                                  sem).start()
        return carry

    @pl.when(i >= 1)
    def _():
        lax.fori_loop(0, tm, issue, 0, unroll=8)

    @pl.when(i < n_tiles)
    def _():
        u_ref[cur] = _rms(h_ref[...], g_ref[...])

    @pl.when(i >= 1)
    def _():
        for _ in range(2):
            pltpu.make_async_copy(u_ref.at[1 - cur], xs_ref.at[pl.ds(0, tm)], sem).wait()


def _dispatch(dest, unused, h1, g, xs_init):
    n = h1.shape[0]
    tm = ROW_TILE
    p_rows = xs_init.shape[0]
    return pl.pallas_call(
        functools.partial(_dispatch_kernel, tm=tm, n=n, tb=MOE_BLOCK),
        grid_spec=pltpu.PrefetchScalarGridSpec(
            num_scalar_prefetch=2,
            grid=(n // tm + 1,),
            in_specs=[pl.BlockSpec((tm, D_MODEL),
                                   lambda i, d, un: (jnp.minimum(i, n // tm - 1), 0)),
                      pl.BlockSpec((1, D_MODEL), lambda i, d, un: (0, 0)),
                      pl.BlockSpec(memory_space=pl.ANY)],
            out_specs=[pl.BlockSpec(memory_space=pl.ANY),
                       pl.BlockSpec(memory_space=pltpu.SMEM)],
            scratch_shapes=[pltpu.VMEM((2, tm, D_MODEL), F32),
                            pltpu.SemaphoreType.DMA(())]),
        out_shape=[jax.ShapeDtypeStruct(xs_init.shape, xs_init.dtype),
                   jax.ShapeDtypeStruct((p_rows,), jnp.int32)],
        input_output_aliases={4: 0},
        compiler_params=pltpu.CompilerParams(dimension_semantics=("arbitrary",)),
        name="moe_dispatch",
    )(dest, unused, h1, g, xs_init)


def _expert_kernel(be_ref, na_ref, back_ref, x_ref, w1_ref, w3_ref, w2_ref, out_ref,
                   y_ref, w1b_ref, w3b_ref, w2b_ref, sem, *, tb):
    j = pl.program_id(0)
    na = na_ref[0]
    cur = j % 2

    def scatter(block, buf):
        for r in range(tb):
            pltpu.make_async_copy(y_ref.at[buf, pl.ds(r, 1)],
                                  out_ref.at[pl.ds(back_ref[block * tb + r], 1)],
                                  sem.at[buf]).start()

    def compute(buf):
        x = x_ref[...].astype(BF16)
        a = jnp.dot(x, w1b_ref[...], preferred_element_type=F32)
        b = jnp.dot(x, w3b_ref[...], preferred_element_type=F32)
        act = (a * _sigmoid(a) * b).astype(BF16)
        y_ref[buf] = jnp.dot(act, w2b_ref[...], preferred_element_type=F32)

    new_expert = (j == 0) | (be_ref[j] != be_ref[jnp.maximum(j - 1, 0)])

    @pl.when((j < na) & new_expert)
    def _():
        w1b_ref[...] = w1_ref[0].astype(BF16)
        w3b_ref[...] = w3_ref[0].astype(BF16)
        w2b_ref[...] = w2_ref[0].astype(BF16)

    @pl.when((j >= 2) & (j < na + 2))
    def _():
        pltpu.make_async_copy(y_ref.at[cur], out_ref.at[pl.ds(0, tb)], sem.at[cur]).wait()

    @pl.when(j == 0)
    def _():
        y_ref[...] = jnp.zeros_like(y_ref)
        first_spare = out_ref.shape[0] - 2 * tb
        spare = [pltpu.make_async_copy(
            y_ref.at[b], out_ref.at[pl.ds(first_spare + b * tb, tb)], sem.at[b])
            for b in range(2)]
        for copy in spare:
            copy.start()
        for copy in spare:
            copy.wait()
        compute(cur)

    @pl.when((j >= 1) & (j < na))
    def _():
        scatter(j - 1, 1 - cur)
        compute(cur)

    @pl.when(j == na)
    def _():
        scatter(j - 1, 1 - cur)


def _experts(blk_expert, n_active, slot_back, xs, w1, w3, w2, layer, n):
    p_rows = xs.shape[0]
    tb = MOE_BLOCK
    last = lambda j, na: jnp.maximum(jnp.minimum(j, na[0] - 1), 0)
    w_spec = lambda r, c: pl.BlockSpec((None, 1, r, c),
                                       lambda j, be, na, back: (layer, be[j], 0, 0))
    return pl.pallas_call(
        functools.partial(_expert_kernel, tb=tb),
        grid_spec=pltpu.PrefetchScalarGridSpec(
            num_scalar_prefetch=3,
            grid=(p_rows // tb + 1,),
            in_specs=[pl.BlockSpec((tb, D_MODEL), lambda j, be, na, back: (last(j, na), 0)),
                      w_spec(D_MODEL, D_EXPERT), w_spec(D_MODEL, D_EXPERT),
                      w_spec(D_EXPERT, D_MODEL)],
            out_specs=pl.BlockSpec(memory_space=pl.ANY),
            scratch_shapes=[pltpu.VMEM((2, tb, D_MODEL), F32),
                            pltpu.VMEM((D_MODEL, D_EXPERT), BF16),
                            pltpu.VMEM((D_MODEL, D_EXPERT), BF16),
                            pltpu.VMEM((D_EXPERT, D_MODEL), BF16),
                            pltpu.SemaphoreType.DMA((2,))]),
        out_shape=jax.ShapeDtypeStruct((2 * n + 2 * tb, D_MODEL), F32),
        compiler_params=pltpu.CompilerParams(dimension_semantics=("arbitrary",),
                                             vmem_limit_bytes=VMEM_LIMIT),
        name="moe_experts",
    )(blk_expert, n_active, slot_back, xs, w1, w3, w2)


def _combine_kernel(h_ref, route_ref, gfin_ref, y0_ref, y1_ref, o_ref, *, final):
    route = route_ref[...]
    out = h_ref[...] + route[:, 2:3] * y0_ref[...] + route[:, 3:4] * y1_ref[...]
    if final:
        out = _rms(out, gfin_ref[...])
    o_ref[...] = out


def _combine(h1, route, g_final, ys, batch, seq_len, final):
    n = h1.shape[0]
    tm = ROW_TILE
    tiles_per_seq = seq_len // tm
    skip_tiles = (LPAD + N_META) // tm if final else 0
    out_tiles = tiles_per_seq - skip_tiles
    tile = lambda b, i: b * tiles_per_seq + skip_tiles + i
    in_row = lambda w: pl.BlockSpec((tm, w), lambda b, i: (tile(b, i), 0))
    y_rows = lambda k: pl.BlockSpec((tm, D_MODEL),
                                    lambda b, i: (k * (n // tm) + tile(b, i), 0))
    return pl.pallas_call(
        functools.partial(_combine_kernel, final=final),
        grid=(batch, out_tiles),
        in_specs=[in_row(D_MODEL), in_row(LANES),
                  pl.BlockSpec((1, D_MODEL), lambda b, i: (0, 0)),
                  y_rows(0), y_rows(1)],
        out_specs=pl.BlockSpec((tm, D_MODEL), lambda b, i: (b * out_tiles + i, 0)),
        out_shape=jax.ShapeDtypeStruct((batch * out_tiles * tm, D_MODEL), F32),
        compiler_params=pltpu.CompilerParams(dimension_semantics=("arbitrary", "arbitrary")),
        name="moe_combine",
    )(h1, route, g_final, ys, ys)


def _rope_partner(m):
    lead = m.shape[:-1]
    x = m.reshape(lead + (DA_WIDTH // DA_DQK, 2, DA_DQK // 2))
    return x[..., ::-1, :].reshape(lead + (DA_WIDTH,))


def _rope_tables(seq_len):
    pos = (jnp.arange(seq_len) - LPAD).astype(F32)
    inv = ROPE_THETA ** (-jnp.arange(0, DA_DQK, 2, dtype=F32) / DA_DQK)
    ang = pos[:, None] * inv[None, :]
    comps = DA_DV // DA_DQK
    cos = jnp.tile(jnp.cos(ang), (1, 2 * comps))
    sin = jnp.tile(jnp.concatenate([-jnp.sin(ang), jnp.sin(ang)], axis=1), (1, comps))
    q_scale = DA_DQK ** -0.5 * LOG2E
    two = lambda t: jnp.tile(t, (1, LANES // DA_DV))
    lay = lambda t: jnp.concatenate([two(t * q_scale), two(t)], axis=1)
    return lay(cos), lay(sin)


def _in_weights(w):
    hq, hf, hi, hg = (w[:, i * 512:(i + 1) * 512] for i in range(4))
    dq, dk, dv = (w[:, 2048 + i * 256:2048 + (i + 1) * 256] for i in range(3))
    fq, fk, fv = (w[:, 2816 + i * 256:2816 + (i + 1) * 256] for i in range(3))
    ff = w[:, 3584:3588]
    cat = jnp.concatenate(
        [hq, hi, hg, hf,
         dq, dk, _rope_partner(dq), _rope_partner(dk),
         fq * (FX_DH ** -0.5), fk,
         ff, jnp.zeros((D_MODEL, LANES - FX_HEADS), w.dtype)],
        axis=1)
    w_vt = jnp.concatenate([dv, fv], axis=1).T
    return cat.astype(BF16), w_vt.astype(BF16)


def _pad_lanes(v, width=LANES):
    return jnp.zeros((1, width), F32).at[0, :v.shape[0]].set(v.astype(F32))


def kernel(x, meta_tokens, norm_mix, w_in, hgrn_lb, hgrn_norm, diff_lambda, diff_norm,
           fox_bias, fox_norm, w_out, norm_ffn, w_group, b_group, w_router, b_router,
           w1, w3, w2, norm_final):
    batch, seq, d = x.shape
    depth = w_in.shape[0]
    seq_len = LPAD + N_META + seq
    n = batch * seq_len
    pad = jnp.zeros((batch, LPAD, d), x.dtype)
    meta = jnp.broadcast_to(meta_tokens.astype(x.dtype)[None], (batch, N_META, d))
    h = jnp.concatenate([pad, meta, x], axis=1).reshape(n, d)

    s_lb = jax.nn.softmax(hgrn_lb.astype(F32), axis=0)
    lb_all = jnp.cumsum(s_lb, axis=0) - s_lb[0]
    cos_t, sin_t = _rope_tables(seq_len)

    tb = MOE_BLOCK
    n_blocks = (2 * n) // tb + N_EXPERTS
    p_rows = n_blocks * tb

    xs = jnp.zeros((p_rows, d), F32)
    for layer in range(depth):
        lam_init = 0.8 - 0.6 * math.exp(-0.3 * layer)
        w_cat, w_vt = _in_weights(w_in[layer])
        hqig, hlf, hkk, dqk, fqk, dvt, fvt = _in_proj(
            h, norm_mix[layer][None, :], w_cat, w_vt, cos_t, sin_t,
            _pad_lanes(fox_bias[layer]), lb_all[layer][None, :], batch, seq_len)

        o_a = _hgrn(hqig, hlf, hkk, hgrn_norm[layer][None, :], batch, seq_len)

        lam_vecs = jnp.zeros((8, LANES), F32).at[:4, :DA_DQK].set(diff_lambda[layer].astype(F32))
        o_b = _diff_attn(dqk, dvt, lam_vecs, _pad_lanes(diff_norm[layer]), lam_init,
                         batch, seq_len)

        o_c = _fox_attn(fqk, fvt, _pad_lanes(fox_norm[layer]), batch, seq_len)

        w_rt = jnp.concatenate(
            [w_router[layer], w_group[layer],
             jnp.zeros((d, LANES - N_EXPERTS - N_GROUPS), F32)], axis=1)
        b_rt = _pad_lanes(jnp.concatenate([b_router[layer], b_group[layer]]))
        h1, route, counts = _out_router(
            o_a, o_b, o_c, h, w_out[layer].astype(BF16), norm_ffn[layer][None, :], w_rt, b_rt)

        cnt = counts[0, :N_EXPERTS].astype(jnp.int32)
        padded = (cnt + tb - 1) // tb * tb
        p_end = jnp.cumsum(padded)
        p_start = p_end - padded
        ids = route[:, 0:2].astype(jnp.int32)
        ranks = route[:, 4:6].astype(jnp.int32)
        dest = (p_start[ids] + ranks).T.reshape(2 * n)
        blk_start = jnp.arange(n_blocks + 1, dtype=jnp.int32) * tb
        blk_expert = jnp.minimum(
            jnp.sum((p_end[None, :] <= blk_start[:, None]).astype(jnp.int32), axis=1),
            N_EXPERTS - 1)
        n_active = (p_end[-1:] // tb).astype(jnp.int32)

        unused = jnp.concatenate(
            [p_start + cnt, p_end[-1:], p_end, jnp.full((1,), p_rows)]).astype(jnp.int32)
        xs, slot_back = _dispatch(dest, unused, h1, norm_ffn[layer][None, :], xs)
        ys = _experts(blk_expert, n_active, slot_back, xs, w1, w3, w2, layer, n)
        h = _combine(h1, route, norm_final[None, :], ys, batch, seq_len,
                     final=(layer == depth - 1))

    return h.reshape(batch, seq, d)
```

```python
import functools
import math

import jax
import jax.numpy as jnp
from jax import lax
from jax.experimental import pallas as pl
from jax.experimental.pallas import tpu as pltpu

F32 = jnp.float32
BF16 = jnp.bfloat16

D_MODEL = 1024
N_META = 16
HG_HEADS = 4
HG_DK = 128
HG_WIDTH = 512
HG_CHUNK = 64
HG_SUB = 8
HG_PER_STEP = 4
DA_HEADS = 4
DA_DV = 64
DA_DQK = 32
DA_WIDTH = 256
FX_HEADS = 4
FX_DH = 64
FX_WIDTH = 256
N_GROUPS = 4
EXPERTS_PER_GROUP = 8
N_EXPERTS = 32
D_EXPERT = 512
ROPE_THETA = 10000.0
EPS = 1e-6
NEG = -1e30
TINY = 1e-30
LOG2E = 1.4426950408889634

LANES = 128
SLOT = LANES
HEAD_W = 4 * SLOT
ROW_TILE = 512
PROJ_TILE = 512
ROUTER_TILE = 512
CUMSUM_ROWS = 128
ATTN_BLOCK = 512
ATTN_HEADS_PER_STEP = 4
MOE_BLOCK = 256
LPAD = ATTN_BLOCK - N_META
FIRST_KEYS = LANES
assert ATTN_BLOCK - FIRST_KEYS <= LPAD
ONE_LANE = 64
VMEM_LIMIT = 56 * 1024 * 1024

_C_HQIG = (0, 1536)
_C_HF = (1536, 2048)
_C_DQK = (2048, 2560)
_C_DQKP = (2560, 3072)
_C_FQK = (3072, 3584)
_C_FF = (3584, 3712)
IN_COLS = 3712

_NT = (((1,), (1,)), ((), ()))
_TN = (((0,), (0,)), ((), ()))


def _sigmoid(x):
    return 1.0 / (1.0 + jnp.exp(-x))


def _rms(x, g):
    return x * lax.rsqrt(jnp.mean(x * x, axis=-1, keepdims=True) + EPS) * g


def _in_proj_kernel(h_ref, g_ref, w_ref, wvt_ref, cos_ref, sin_ref, fb_ref, lb_ref,
                    hqig_ref, hlf_ref, hkk_ref, dqk_ref, fqk_ref, dvt_ref, fvt_ref,
                    carry_ref, *, tiles_per_seq):
    u = _rms(h_ref[...], g_ref[...]).astype(BF16)
    tm = u.shape[0]

    def mm(c):
        return jnp.dot(u, w_ref[:, c[0]:c[1]], preferred_element_type=F32)

    hg = mm(_C_HQIG)
    q_raw, g_raw = hg[:, 0:HG_WIDTH], hg[:, 2 * HG_WIDTH:3 * HG_WIDTH]
    hqig_ref[:, 0:HG_WIDTH] = (q_raw * _sigmoid(q_raw) * (HG_DK ** -0.5)).astype(BF16)
    hqig_ref[:, HG_WIDTH:2 * HG_WIDTH] = hg[:, HG_WIDTH:2 * HG_WIDTH].astype(BF16)
    hqig_ref[:, 2 * HG_WIDTH:3 * HG_WIDTH] = (g_raw * _sigmoid(g_raw)).astype(BF16)
    lb = lb_ref[...]
    sig = _sigmoid(mm(_C_HF))
    row = ((pl.program_id(0) % tiles_per_seq) * tm
           + lax.broadcasted_iota(jnp.int32, (tm, 1), 0))
    valid = row >= LPAD
    hlf_ref[...] = jnp.where(
        valid, jnp.log2(jnp.maximum(lb + (1.0 - lb) * sig, TINY)), 0.0)
    hkk_ref[...] = jnp.where(valid, (1.0 - lb) * (1.0 - sig), 0.0)
    half_lane = lax.broadcasted_iota(jnp.int32, (tm, LANES), 1) < DA_DV

    def spread(x):
        tiles = []
        for t in range(x.shape[1] // LANES):
            tile = x[:, t * LANES:(t + 1) * LANES]
            tiles.append(jnp.where(half_lane, tile, 0.0))
            tiles.append(jnp.where(half_lane, pltpu.roll(tile, LANES // 2, axis=1), 0.0))
        return jnp.concatenate(tiles, axis=1)

    def per_tile(t_ref):
        return jnp.concatenate([t_ref[:, 0:LANES]] * 2 + [t_ref[:, LANES:2 * LANES]] * 2, axis=1)

    rot = mm(_C_DQK) * per_tile(cos_ref) + mm(_C_DQKP) * per_tile(sin_ref)
    dqk_ref[:, 0:HEAD_W] = spread(rot[:, 0:DA_WIDTH]).astype(BF16)
    dqk_ref[:, HEAD_W:2 * HEAD_W] = spread(rot[:, DA_WIDTH:2 * DA_WIDTH]).astype(BF16)

    @pl.when(pl.program_id(0) % tiles_per_seq == 0)
    def _():
        carry_ref[...] = jnp.zeros_like(carry_ref)

    x = mm(_C_FF) + fb_ref[...]
    lf = jnp.minimum(x, 0.0) - jnp.log(1.0 + jnp.exp(-jnp.abs(x)))
    tri = (lax.broadcasted_iota(jnp.int32, (CUMSUM_ROWS, CUMSUM_ROWS), 1)
           <= lax.broadcasted_iota(jnp.int32, (CUMSUM_ROWS, CUMSUM_ROWS), 0)).astype(BF16)
    lf_terms = [t.astype(BF16) for t in _bf16_split3(lf)]
    total = carry_ref[...]
    groups = []
    for r0 in range(0, tm, CUMSUM_ROWS):
        part = sum(jnp.dot(tri, t[r0:r0 + CUMSUM_ROWS], preferred_element_type=F32)
                   for t in lf_terms) + total
        total = part[CUMSUM_ROWS - 1:CUMSUM_ROWS, :]
        groups.append(part)
    cs = jnp.concatenate(groups, axis=0)
    carry_ref[...] = total
    parts = _bf16_split3(cs * LOG2E)
    fqk = mm(_C_FQK)
    fq = spread(fqk[:, 0:FX_WIDTH]) * LOG2E
    fk = spread(fqk[:, FX_WIDTH:2 * FX_WIDTH])
    lane = lax.broadcasted_iota(jnp.int32, (tm, SLOT), 1)
    d0 = FX_DH
    for h in range(FX_HEADS):
        q = fq[:, h * SLOT:(h + 1) * SLOT]
        k = fk[:, h * SLOT:(h + 1) * SLOT]
        for i, part in enumerate(parts):
            col = part[:, h:h + 1]
            q = jnp.where(lane == d0 + i, col, q)
            k = jnp.where(lane == d0 + 3 + i, -col, k)
        q = jnp.where((lane >= d0 + 3) & (lane < d0 + 6), 1.0, q)
        k = jnp.where((lane >= d0) & (lane < d0 + 3), 1.0, k)
        fqk_ref[:, h * SLOT:(h + 1) * SLOT] = q.astype(BF16)
        fqk_ref[:, HEAD_W + h * SLOT:HEAD_W + (h + 1) * SLOT] = k.astype(BF16)

    vt = lax.dot_general(wvt_ref[...], u, _NT, preferred_element_type=F32).astype(BF16)
    spare = SLOT - DA_DV
    ones_then_zeros = jnp.where(lax.broadcasted_iota(jnp.int32, (spare, tm), 0) == 0,
                                1.0, 0.0).astype(BF16)
    for h in range(4):
        for ref, first in ((dvt_ref, 0), (fvt_ref, DA_WIDTH)):
            ref[0, h, 0, 0:DA_DV, :] = vt[first + h * DA_DV:first + (h + 1) * DA_DV]
            ref[0, h, 0, DA_DV:SLOT, :] = ones_then_zeros


def _in_proj(h, g, w_cat, w_vt, cos_t, sin_t, fox_bias_row, lb_row, batch, seq_len):
    n = h.shape[0]
    tm = PROJ_TILE
    assert tm == ATTN_BLOCK
    nk = seq_len // tm
    row = lambda w: pl.BlockSpec((tm, w), lambda i: (i, 0))
    tab = pl.BlockSpec((tm, 2 * SLOT), lambda i: (i % nk, 0))
    once = lambda r, c: pl.BlockSpec((r, c), lambda i: (0, 0), pipeline_mode=pl.Buffered(1))
    widths = (3 * HG_WIDTH, HG_WIDTH, HG_WIDTH, 2 * HEAD_W, 2 * HEAD_W)
    dtypes = (BF16, F32, F32, BF16, BF16)
    vt_spec = pl.BlockSpec((1, 4, 1, SLOT, tm), lambda i: (i // nk, 0, i % nk, 0, 0))
    vt_shape = jax.ShapeDtypeStruct((batch, 4, nk, SLOT, tm), BF16)
    return pl.pallas_call(
        functools.partial(_in_proj_kernel, tiles_per_seq=nk),
        grid=(n // tm,),
        in_specs=[row(D_MODEL),
                  pl.BlockSpec((1, D_MODEL), lambda i: (0, 0)),
                  once(D_MODEL, IN_COLS), once(DA_WIDTH + FX_WIDTH, D_MODEL),
                  tab, tab,
                  pl.BlockSpec((1, LANES), lambda i: (0, 0)),
                  pl.BlockSpec((1, HG_WIDTH), lambda i: (0, 0))],
        out_specs=[row(w) for w in widths] + [vt_spec, vt_spec],
        out_shape=[jax.ShapeDtypeStruct((n, w), t) for w, t in zip(widths, dtypes)]
        + [vt_shape, vt_shape],
        scratch_shapes=[pltpu.VMEM((1, LANES), F32)],
        compiler_params=pltpu.CompilerParams(dimension_semantics=("arbitrary",),
                                             vmem_limit_bytes=VMEM_LIMIT),
        name="in_proj",
    )(h, g, w_cat, w_vt, cos_t, sin_t, fox_bias_row, lb_row)


def _bf16_split3(x):
    hi = x.astype(BF16).astype(F32)
    r = x - hi
    mid = r.astype(BF16).astype(F32)
    return hi, mid, r - mid


def _softmax_block(s, m_prev):
    m_new = jnp.maximum(m_prev, jnp.max(s, axis=0, keepdims=True))
    alpha = jnp.exp2(m_prev - m_new)
    p = jnp.exp2(s - m_new).astype(BF16)
    return p, alpha, m_new


def _block_start(j, blk):
    return j * blk if isinstance(j, int) else pl.multiple_of(j * blk, blk)


def _causal_valid(i, j, blk, first_row=0):
    k_idx = j * blk + first_row + lax.broadcasted_iota(jnp.int32, (blk - first_row, blk), 0)
    q_idx = i * blk + lax.broadcasted_iota(jnp.int32, (blk - first_row, blk), 1)
    return (k_idx <= q_idx) & (k_idx >= LPAD)


def _pipelined_key_blocks(i, first, scores, consume):
    first()

    @pl.when(i >= 1)
    def _():
        scores(1, 0)

    def pair(t, carry):
        scores(2 * t + 2, 1)
        consume(2 * t + 1, 0, False)
        scores(2 * t + 3, 0)
        consume(2 * t + 2, 1, False)
        return carry

    lax.fori_loop(0, lax.shift_right_logical(jnp.maximum(i - 1, 0), 1), pair, 0)
    odd = (i & 1) == 1

    @pl.when(odd)
    def _():
        consume(i, 0, True)

    @pl.when((i >= 2) & jnp.logical_not(odd))
    def _():
        scores(i, 1)
        consume(i - 1, 0, False)
        consume(i, 1, True)


def _head_out(acc):
    lane = lax.broadcasted_iota(jnp.int32, acc.shape, 1)
    return jnp.where(lane < ONE_LANE, acc / acc[:, ONE_LANE:ONE_LANE + 1], 0.0)


def _store_head_pairs(o_ref, outs):
    for p in range(len(outs) // 2):
        pair = outs[2 * p] + pltpu.roll(outs[2 * p + 1], ONE_LANE, axis=1)
        o_ref[:, p * LANES:(p + 1) * LANES] = pair.astype(BF16)


def _head_rms(o, g):
    ms = jnp.sum(o * o, axis=-1, keepdims=True) * (1.0 / ONE_LANE)
    return o * lax.rsqrt(ms + EPS) * g


def _fox_attn_kernel(q_ref, k_ref, vt_ref, g_ref, o_ref, m_ref, acc_ref, sa_ref, sb_ref,
                     *, blk):
    i = pl.program_id(2)
    heads = range(ATTN_HEADS_PER_STEP)
    slot = [slice(hh * SLOT, (hh + 1) * SLOT) for hh in heads]
    q = [q_ref[:, slot[hh]] for hh in heads]
    bufs = (sa_ref, sb_ref)
    m_ref[...] = jnp.full(m_ref.shape, NEG, F32)
    acc_ref[...] = jnp.zeros_like(acc_ref)

    def scores(j, buf):
        rows = pl.ds(_block_start(j, blk), blk)
        for hh in heads:
            bufs[buf][hh] = lax.dot_general(k_ref[rows, slot[hh]], q[hh], _NT,
                                            preferred_element_type=F32)

    def consume(j, buf, masked):
        mask = _causal_valid(i, j, blk) if masked else None
        for hh in heads:
            s = bufs[buf][hh]
            if masked:
                s = jnp.where(mask, s, NEG)
            p, alpha, m_new = _softmax_block(s, m_ref[hh])
            acc_ref[hh] = alpha * acc_ref[hh] + jnp.dot(
                vt_ref[0, hh, j], p, preferred_element_type=F32)
            m_ref[hh] = m_new

    def first():
        lo = blk - FIRST_KEYS
        mask = _causal_valid(i, 0, blk, lo)
        for hh in heads:
            s = lax.dot_general(k_ref[lo:blk, slot[hh]], q[hh], _NT,
                                preferred_element_type=F32)
            p, alpha, m_new = _softmax_block(jnp.where(mask, s, NEG), m_ref[hh])
            acc_ref[hh] = alpha * acc_ref[hh] + jnp.dot(
                vt_ref[0, hh, 0, :, lo:blk], p, preferred_element_type=F32)
            m_ref[hh] = m_new

    _pipelined_key_blocks(i, first, scores, consume)
    _store_head_pairs(
        o_ref, [_head_rms(_head_out(acc_ref[hh].T), g_ref[...]) for hh in heads])


def _key_spec(seq_len, groups):
    return pl.BlockSpec((seq_len, ATTN_HEADS_PER_STEP * SLOT),
                        lambda b, h, i: (b, groups + h), pipeline_mode=pl.Buffered(1))


def _value_spec(seq_len):
    nk = seq_len // ATTN_BLOCK
    return pl.BlockSpec((1, ATTN_HEADS_PER_STEP, nk, SLOT, ATTN_BLOCK),
                        lambda b, h, i: (b, h, 0, 0, 0), pipeline_mode=pl.Buffered(1))


def _fox_attn(fqk, fvt, gain, batch, seq_len):
    n = fqk.shape[0]
    blk = ATTN_BLOCK
    nq = seq_len // blk
    hps = ATTN_HEADS_PER_STEP
    groups = FX_HEADS // hps
    return pl.pallas_call(
        functools.partial(_fox_attn_kernel, blk=blk),
        grid=(batch, groups, nq),
        in_specs=[pl.BlockSpec((blk, hps * SLOT), lambda b, h, i: (b * nq + i, h)),
                  _key_spec(seq_len, groups),
                  _value_spec(seq_len),
                  pl.BlockSpec((1, SLOT), lambda b, h, i: (0, 0))],
        out_specs=pl.BlockSpec((blk, hps * ONE_LANE), lambda b, h, i: (b * nq + i, h)),
        out_shape=jax.ShapeDtypeStruct((n, 4 * ONE_LANE), BF16),
        scratch_shapes=[pltpu.VMEM((hps, 1, blk), F32), pltpu.VMEM((hps, SLOT, blk), F32),
                        pltpu.VMEM((hps, blk, blk), F32), pltpu.VMEM((hps, blk, blk), F32)],
        compiler_params=pltpu.CompilerParams(
            dimension_semantics=("arbitrary", "arbitrary", "arbitrary"),
            vmem_limit_bytes=VMEM_LIMIT),
        name="fox_attn",
    )(fqk, fqk, fvt, gain)


def _diff_attn_kernel(q_ref, k_ref, vt_ref, lam_ref, g_ref, o_ref, m_ref, acc_ref,
                      sa_ref, sb_ref, *, blk, lam_init):
    i = pl.program_id(2)
    heads = range(ATTN_HEADS_PER_STEP)
    slot = lambda s: slice(s * SLOT, (s + 1) * SLOT)
    lane = lax.broadcasted_iota(jnp.int32, (blk, SLOT), 1)
    q = []
    for hh in heads:
        q_h = q_ref[:, slot(hh)]
        q.append([jnp.where((lane >= c * DA_DQK) & (lane < (c + 1) * DA_DQK), q_h,
                            jnp.zeros_like(q_h)) for c in range(2)])
    bufs = (sa_ref, sb_ref)
    m_ref[...] = jnp.full(m_ref.shape, NEG, F32)
    acc_ref[...] = jnp.zeros_like(acc_ref)

    def scores(j, buf):
        rows = pl.ds(_block_start(j, blk), blk)
        for hh in heads:
            k_h = k_ref[rows, slot(hh)]
            for c in range(2):
                bufs[buf][2 * hh + c] = lax.dot_general(k_h, q[hh][c], _NT,
                                                        preferred_element_type=F32)

    def consume(j, buf, masked):
        mask = _causal_valid(i, j, blk) if masked else None
        for hh in heads:
            ps, alphas = [], []
            for c in range(2):
                hc = 2 * hh + c
                s = bufs[buf][hc]
                if masked:
                    s = jnp.where(mask, s, NEG)
                p, alpha, m_new = _softmax_block(s, m_ref[hc])
                m_ref[hc] = m_new
                ps.append(p)
                alphas.append(alpha)
            pv = jnp.dot(vt_ref[0, hh, j], jnp.concatenate(ps, axis=1),
                         preferred_element_type=F32)
            for c in range(2):
                hc = 2 * hh + c
                acc_ref[hc] = alphas[c] * acc_ref[hc] + pv[:, c * blk:(c + 1) * blk]

    def first():
        lo = blk - FIRST_KEYS
        mask = _causal_valid(i, 0, blk, lo)
        for hh in heads:
            k_h = k_ref[lo:blk, slot(hh)]
            ps, alphas = [], []
            for c in range(2):
                hc = 2 * hh + c
                s = lax.dot_general(k_h, q[hh][c], _NT, preferred_element_type=F32)
                p, alpha, m_new = _softmax_block(jnp.where(mask, s, NEG), m_ref[hc])
                m_ref[hc] = m_new
                ps.append(p)
                alphas.append(alpha)
            pv = jnp.dot(vt_ref[0, hh, 0, :, lo:blk], jnp.concatenate(ps, axis=1),
                         preferred_element_type=F32)
            for c in range(2):
                hc = 2 * hh + c
                acc_ref[hc] = alphas[c] * acc_ref[hc] + pv[:, c * blk:(c + 1) * blk]

    _pipelined_key_blocks(i, first, scores, consume)
    lv = lam_ref[...]
    lam = (jnp.exp(jnp.sum(lv[0:1] * lv[1:2], axis=-1, keepdims=True))
           - jnp.exp(jnp.sum(lv[2:3] * lv[3:4], axis=-1, keepdims=True)) + lam_init)
    outs = []
    for hh in heads:
        o = _head_out(acc_ref[2 * hh].T) - lam * _head_out(acc_ref[2 * hh + 1].T)
        outs.append(_head_rms(o, g_ref[...]) * (1.0 - lam_init))
    _store_head_pairs(o_ref, outs)


def _diff_attn(dqk, dvt, lam_vecs, gain, lam_init, batch, seq_len):
    n = dqk.shape[0]
    blk = ATTN_BLOCK
    nq = seq_len // blk
    hps = ATTN_HEADS_PER_STEP
    groups = DA_HEADS // hps
    return pl.pallas_call(
        functools.partial(_diff_attn_kernel, blk=blk, lam_init=lam_init),
        grid=(batch, groups, nq),
        in_specs=[pl.BlockSpec((blk, hps * SLOT), lambda b, h, i: (b * nq + i, h)),
                  _key_spec(seq_len, groups),
                  _value_spec(seq_len),
                  pl.BlockSpec((8, LANES), lambda b, h, i: (0, 0)),
                  pl.BlockSpec((1, SLOT), lambda b, h, i: (0, 0))],
        out_specs=pl.BlockSpec((blk, hps * ONE_LANE), lambda b, h, i: (b * nq + i, h)),
        out_shape=jax.ShapeDtypeStruct((n, 4 * ONE_LANE), BF16),
        scratch_shapes=[pltpu.VMEM((2 * hps, 1, blk), F32),
                        pltpu.VMEM((2 * hps, SLOT, blk), F32),
                        pltpu.VMEM((2 * hps, blk, blk), F32),
                        pltpu.VMEM((2 * hps, blk, blk), F32)],
        compiler_params=pltpu.CompilerParams(
            dimension_semantics=("arbitrary", "arbitrary", "arbitrary"),
            vmem_limit_bytes=VMEM_LIMIT),
        name="diff_attn",
    )(dqk, dqk, dvt, lam_vecs, gain)


def _hgrn_kernel(q_ref, i_ref, g_ref, lf_ref, kk_ref, gain_ref, o_ref,
                 st_ref, hs_ref, *, tl):
    @pl.when(pl.program_id(2) == 0)
    def _():
        st_ref[...] = jnp.zeros_like(st_ref)

    c_rows, sub = HG_CHUNK, HG_SUB
    tri = (lax.broadcasted_iota(jnp.int32, (c_rows, c_rows), 1)
           <= lax.broadcasted_iota(jnp.int32, (c_rows, c_rows), 0)).astype(BF16)
    t_sub = lax.broadcasted_iota(jnp.int32, (sub, 1), 0)
    lane_c = lax.broadcasted_iota(jnp.int32, (sub, c_rows), 1)

    t_all = lax.broadcasted_iota(jnp.int32, (c_rows, c_rows), 0)
    s_all = lax.broadcasted_iota(jnp.int32, (c_rows, c_rows), 1)
    level_masks = []
    g = sub
    while g < c_rows:
        level_masks.append(((t_all // g) == (s_all // g) + 1) & ((s_all // g) % 2 == 0))
        g *= 2

    heads = range(HG_PER_STEP)
    cols = [slice(hh * HG_DK, (hh + 1) * HG_DK) for hh in heads]

    def chunk(c, carry):
        r0 = pl.multiple_of(c * c_rows, c_rows)
        rows = pl.ds(r0, c_rows)
        G, kk, qs, v, st, o_inter = [], [], [], [], [], []
        for hh in heads:
            kk.append(kk_ref[rows, cols[hh]])
            G.append(sum(jnp.dot(tri, t.astype(BF16), preferred_element_type=F32)
                         for t in _bf16_split3(lf_ref[rows, cols[hh]])))
            qs.append(q_ref[rows, cols[hh]].astype(F32))
            v.append(i_ref[rows, cols[hh]])
            hs_ref[hh] = G[hh] - jnp.log2(kk[hh])
            st.append(st_ref[hh])
            o_inter.append(lax.dot_general(
                (qs[hh] * jnp.exp2(G[hh])).astype(BF16), st[hh].astype(BF16), _NT,
                preferred_element_type=F32))
        a_mat = [jnp.zeros((c_rows, c_rows), F32) for _ in heads]
        g = sub
        for mask in level_masks:
            for hh in heads:
                ref = jnp.concatenate(
                    [jnp.broadcast_to(G[hh][p + g - 1:p + g], (2 * g, HG_DK))
                     for p in range(0, c_rows, 2 * g)], axis=0)
                e = jnp.exp2(-jnp.abs(G[hh] - ref))
                part = lax.dot_general((qs[hh] * e).astype(BF16), (kk[hh] * e).astype(BF16),
                                       _NT, preferred_element_type=F32)
                a_mat[hh] = jnp.where(mask, part, a_mat[hh])
            g *= 2
        a_rows = [[] for _ in heads]
        for b in range(c_rows // sub):
            lo = b * sub
            for hh in heads:
                q_b = qs[hh][lo:lo + sub]
                g_b = G[hh][lo:lo + sub]
                a_blk = a_mat[hh][lo:lo + sub]
                for s in range(lo, lo + sub):
                    y = q_b * jnp.exp2(g_b - hs_ref[hh, s:s + 1, :])
                    a_blk = jnp.where(lane_c == s, jnp.sum(y, axis=-1, keepdims=True), a_blk)
                a_rows[hh].append(jnp.where(lane_c <= t_sub + lo, a_blk, 0.0))
        for hh in heads:
            a_full = jnp.concatenate(a_rows[hh], axis=0).astype(BF16)
            o = o_inter[hh] + jnp.dot(a_full, v[hh], preferred_element_type=F32)
            g_last = G[hh][c_rows - 1:c_rows]
            kd = kk[hh] * jnp.exp2(g_last - G[hh])
            st_ref[hh] = st[hh] * jnp.exp2(g_last) + lax.dot_general(
                v[hh], kd.astype(BF16), _TN, preferred_element_type=F32)
            o = _rms(o, gain_ref[...]) * g_ref[rows, cols[hh]].astype(F32)
            o_ref[rows, cols[hh]] = o.astype(BF16)
        return carry

    lax.fori_loop(0, tl // c_rows, chunk, 0)


def _hgrn(hqig, hlf, hkk, gain, batch, seq_len):
    n = hqig.shape[0]
    tl = ROW_TILE
    nt = seq_len // tl
    w = HG_PER_STEP * HG_DK
    groups = HG_HEADS // HG_PER_STEP
    col = lambda off: pl.BlockSpec((tl, w), lambda b, h, j: (b * nt + j, off + h))
    return pl.pallas_call(
        functools.partial(_hgrn_kernel, tl=tl),
        grid=(batch, groups, nt),
        in_specs=[col(0), col(groups), col(2 * groups), col(0), col(0),
                  pl.BlockSpec((1, HG_DK), lambda b, h, j: (0, 0))],
        out_specs=col(0),
        out_shape=jax.ShapeDtypeStruct((n, HG_WIDTH), BF16),
        scratch_shapes=[pltpu.VMEM((HG_PER_STEP, HG_DK, HG_DK), F32),
                        pltpu.VMEM((HG_PER_STEP, HG_CHUNK, HG_DK), F32)],
        compiler_params=pltpu.CompilerParams(
            dimension_semantics=("arbitrary", "arbitrary", "arbitrary")),
        name="hgrn2",
    )(hqig, hqig, hqig, hlf, hkk, gain)


def _out_router_kernel(oa_ref, ob_ref, oc_ref, h_ref, wo_ref, g_ref, wrh_ref, wrl_ref, br_ref,
                       h1_ref, route_ref, cnt_ref, carry_ref):
    @pl.when(pl.program_id(0) == 0)
    def _():
        carry_ref[...] = jnp.zeros_like(carry_ref)

    h1 = (h_ref[...]
          + jnp.dot(oa_ref[...], wo_ref[0:512, :], preferred_element_type=F32)
          + jnp.dot(ob_ref[...], wo_ref[512:768, :], preferred_element_type=F32)
          + jnp.dot(oc_ref[...], wo_ref[768:1024, :], preferred_element_type=F32))
    h1_ref[...] = h1
    u = _rms(h1, g_ref[...])
    u_hi = u.astype(BF16)
    u_lo = (u - u_hi.astype(F32)).astype(BF16)
    logits = (jnp.dot(u_hi, wrh_ref[...], preferred_element_type=F32)
              + (jnp.dot(u_hi, wrl_ref[...], preferred_element_type=F32)
                 + jnp.dot(u_lo, wrh_ref[...], preferred_element_type=F32))
              + br_ref[...])
    tm = logits.shape[0]
    lane = lax.broadcasted_iota(jnp.int32, (tm, LANES), 1).astype(F32)
    big = float(LANES)
    is_g = (lane >= N_EXPERTS) & (lane < N_EXPERTS + N_GROUPS)
    gl = jnp.where(is_g, logits, -jnp.inf)
    gmax = jnp.max(gl, axis=-1, keepdims=True)
    gsel = jnp.min(jnp.where(gl == gmax, lane, big), axis=-1, keepdims=True) - N_EXPERTS
    p_g = 1.0 / jnp.sum(jnp.exp(gl - gmax), axis=-1, keepdims=True)
    lo = gsel * EXPERTS_PER_GROUP
    el = jnp.where((lane >= lo) & (lane < lo + EXPERTS_PER_GROUP), logits, -jnp.inf)
    m1 = jnp.max(el, axis=-1, keepdims=True)
    i1 = jnp.min(jnp.where(el == m1, lane, big), axis=-1, keepdims=True)
    el2 = jnp.where(lane == i1, -jnp.inf, el)
    m2 = jnp.max(el2, axis=-1, keepdims=True)
    i2 = jnp.min(jnp.where(el2 == m2, lane, big), axis=-1, keepdims=True)
    r = jnp.exp(m2 - m1)
    gate1 = p_g / (1.0 + r)
    gate2 = gate1 * r
    oh1 = lane == i1
    oh2 = lane == i2
    onehot = jnp.where(oh1 | oh2, 1.0, 0.0)
    tri = (lax.broadcasted_iota(jnp.int32, (tm, tm), 1)
           < lax.broadcasted_iota(jnp.int32, (tm, tm), 0)).astype(BF16)
    before = jnp.dot(tri, onehot.astype(BF16), preferred_element_type=F32) + carry_ref[...]
    rank1 = jnp.sum(jnp.where(oh1, before, 0.0), axis=-1, keepdims=True)
    rank2 = jnp.sum(jnp.where(oh2, before, 0.0), axis=-1, keepdims=True)
    total = carry_ref[...] + jnp.sum(onehot, axis=0, keepdims=True)
    carry_ref[...] = total
    cnt_ref[...] = total
    route = jnp.where(lane == 0, i1, 0.0)
    for idx, val in ((1, i2), (2, gate1), (3, gate2), (4, rank1), (5, rank2)):
        route = jnp.where(lane == idx, val, route)
    route_ref[...] = route


def _out_router(oa, ob, oc, h, w_out, g, w_rt, b_rt):
    n = h.shape[0]
    w_rt_hi = w_rt.astype(BF16)
    w_rt_lo = (w_rt - w_rt_hi.astype(F32)).astype(BF16)
    tm = ROUTER_TILE
    row = lambda w: pl.BlockSpec((tm, w), lambda i: (i, 0))
    const = lambda r, c: pl.BlockSpec((r, c), lambda i: (0, 0))
    return pl.pallas_call(
        _out_router_kernel,
        grid=(n // tm,),
        in_specs=[row(HG_WIDTH), row(DA_WIDTH), row(FX_WIDTH), row(D_MODEL),
                  const(D_MODEL, D_MODEL),
                  const(1, D_MODEL), const(D_MODEL, LANES), const(D_MODEL, LANES),
                  const(1, LANES)],
        out_specs=[row(D_MODEL), row(LANES), const(1, LANES)],
        out_shape=[jax.ShapeDtypeStruct((n, D_MODEL), F32),
                   jax.ShapeDtypeStruct((n, LANES), F32),
                   jax.ShapeDtypeStruct((1, LANES), F32)],
        scratch_shapes=[pltpu.VMEM((1, LANES), F32)],
        compiler_params=pltpu.CompilerParams(dimension_semantics=("arbitrary",)),
        name="out_router",
    )(oa, ob, oc, h, w_out, g, w_rt_hi, w_rt_lo, b_rt)


def _dispatch_kernel(dest_ref, unused_ref, h_ref, g_ref, xs_in_ref, xs_ref, back_ref,
                     u_ref, sem, *, tm, n, tb):
    del xs_in_ref
    i = pl.program_id(0)
    n_tiles = n // tm
    base = (i - 1) * tm
    cur = i % 2

    @pl.when(i == 0)
    def _():
        def fill(s, carry):
            parity = lax.shift_right_logical(s, tb.bit_length() - 1) & 1
            back_ref[s] = 2 * n + parity * tb + (s & (tb - 1))
            return carry

        n_ranges = unused_ref.shape[0] // 2

        def one_range(e, carry):
            return lax.fori_loop(unused_ref[e], unused_ref[n_ranges + e], fill, carry)

        lax.fori_loop(0, n_ranges, one_range, 0)

    def issue(r, carry):
        for k in range(2):
            src = k * n + base + r
            d = dest_ref[src]
            back_ref[d] = src
            pltpu.make_async_copy(u_ref.at[1 - cur, pl.ds(r, 1)], xs_ref.at[pl.ds(d, 1)],
                                  sem).start()
        return carry

    @pl.when(i >= 1)
    def _():
        lax.fori_loop(0, tm, issue, 0, unroll=8)

    @pl.when(i < n_tiles)
    def _():
        u_ref[cur] = _rms(h_ref[...], g_ref[...])

    @pl.when(i >= 1)
    def _():
        for _ in range(2):
            pltpu.make_async_copy(u_ref.at[1 - cur], xs_ref.at[pl.ds(0, tm)], sem).wait()


def _dispatch(dest, unused, h1, g, xs_init):
    n = h1.shape[0]
    tm = ROW_TILE
    p_rows = xs_init.shape[0]
    return pl.pallas_call(
        functools.partial(_dispatch_kernel, tm=tm, n=n, tb=MOE_BLOCK),
        grid_spec=pltpu.PrefetchScalarGridSpec(
            num_scalar_prefetch=2,
            grid=(n // tm + 1,),
            in_specs=[pl.BlockSpec((tm, D_MODEL),
                                   lambda i, d, un: (jnp.minimum(i, n // tm - 1), 0)),
                      pl.BlockSpec((1, D_MODEL), lambda i, d, un: (0, 0)),
                      pl.BlockSpec(memory_space=pl.ANY)],
            out_specs=[pl.BlockSpec(memory_space=pl.ANY),
                       pl.BlockSpec(memory_space=pltpu.SMEM)],
            scratch_shapes=[pltpu.VMEM((2, tm, D_MODEL), F32),
                            pltpu.SemaphoreType.DMA(())]),
        out_shape=[jax.ShapeDtypeStruct(xs_init.shape, xs_init.dtype),
                   jax.ShapeDtypeStruct((p_rows,), jnp.int32)],
        input_output_aliases={4: 0},
        compiler_params=pltpu.CompilerParams(dimension_semantics=("arbitrary",)),
        name="moe_dispatch",
    )(dest, unused, h1, g, xs_init)


def _expert_kernel(be_ref, na_ref, back_ref, x_ref, w1_ref, w3_ref, w2_ref, out_ref,
                   y_ref, w1b_ref, w3b_ref, w2b_ref, sem, *, tb):
    j = pl.program_id(0)
    na = na_ref[0]
    cur = j % 2

    def scatter(block, buf):
        for r in range(tb):
            pltpu.make_async_copy(y_ref.at[buf, pl.ds(r, 1)],
                                  out_ref.at[pl.ds(back_ref[block * tb + r], 1)],
                                  sem.at[buf]).start()

    def compute(buf):
        x = x_ref[...].astype(BF16)
        a = jnp.dot(x, w1b_ref[...], preferred_element_type=F32)
        b = jnp.dot(x, w3b_ref[...], preferred_element_type=F32)
        act = (a * _sigmoid(a) * b).astype(BF16)
        y_ref[buf] = jnp.dot(act, w2b_ref[...], preferred_element_type=F32)

    new_expert = (j == 0) | (be_ref[j] != be_ref[jnp.maximum(j - 1, 0)])

    @pl.when((j < na) & new_expert)
    def _():
        w1b_ref[...] = w1_ref[0].astype(BF16)
        w3b_ref[...] = w3_ref[0].astype(BF16)
        w2b_ref[...] = w2_ref[0].astype(BF16)

    @pl.when((j >= 2) & (j < na + 2))
    def _():
        pltpu.make_async_copy(y_ref.at[cur], out_ref.at[pl.ds(0, tb)], sem.at[cur]).wait()

    @pl.when(j == 0)
    def _():
        y_ref[...] = jnp.zeros_like(y_ref)
        first_spare = out_ref.shape[0] - 2 * tb
        spare = [pltpu.make_async_copy(
            y_ref.at[b], out_ref.at[pl.ds(first_spare + b * tb, tb)], sem.at[b])
            for b in range(2)]
        for copy in spare:
            copy.start()
        for copy in spare:
            copy.wait()
        compute(cur)

    @pl.when((j >= 1) & (j < na))
    def _():
        scatter(j - 1, 1 - cur)
        compute(cur)

    @pl.when(j == na)
    def _():
        scatter(j - 1, 1 - cur)


def _experts(blk_expert, n_active, slot_back, xs, w1, w3, w2, layer, n):
    p_rows = xs.shape[0]
    tb = MOE_BLOCK
    last = lambda j, na: jnp.maximum(jnp.minimum(j, na[0] - 1), 0)
    w_spec = lambda r, c: pl.BlockSpec((None, 1, r, c),
                                       lambda j, be, na, back: (layer, be[j], 0, 0))
    return pl.pallas_call(
        functools.partial(_expert_kernel, tb=tb),
        grid_spec=pltpu.PrefetchScalarGridSpec(
            num_scalar_prefetch=3,
            grid=(p_rows // tb + 1,),
            in_specs=[pl.BlockSpec((tb, D_MODEL), lambda j, be, na, back: (last(j, na), 0)),
                      w_spec(D_MODEL, D_EXPERT), w_spec(D_MODEL, D_EXPERT),
                      w_spec(D_EXPERT, D_MODEL)],
            out_specs=pl.BlockSpec(memory_space=pl.ANY),
            scratch_shapes=[pltpu.VMEM((2, tb, D_MODEL), F32),
                            pltpu.VMEM((D_MODEL, D_EXPERT), BF16),
                            pltpu.VMEM((D_MODEL, D_EXPERT), BF16),
                            pltpu.VMEM((D_EXPERT, D_MODEL), BF16),
                            pltpu.SemaphoreType.DMA((2,))]),
        out_shape=jax.ShapeDtypeStruct((2 * n + 2 * tb, D_MODEL), F32),
        compiler_params=pltpu.CompilerParams(dimension_semantics=("arbitrary",),
                                             vmem_limit_bytes=VMEM_LIMIT),
        name="moe_experts",
    )(blk_expert, n_active, slot_back, xs, w1, w3, w2)


def _combine_kernel(h_ref, route_ref, gfin_ref, y0_ref, y1_ref, o_ref, *, final):
    route = route_ref[...]
    out = h_ref[...] + route[:, 2:3] * y0_ref[...] + route[:, 3:4] * y1_ref[...]
    if final:
        out = _rms(out, gfin_ref[...])
    o_ref[...] = out


def _combine(h1, route, g_final, ys, batch, seq_len, final):
    n = h1.shape[0]
    tm = ROW_TILE
    tiles_per_seq = seq_len // tm
    skip_tiles = (LPAD + N_META) // tm if final else 0
    out_tiles = tiles_per_seq - skip_tiles
    tile = lambda b, i: b * tiles_per_seq + skip_tiles + i
    in_row = lambda w: pl.BlockSpec((tm, w), lambda b, i: (tile(b, i), 0))
    y_rows = lambda k: pl.BlockSpec((tm, D_MODEL),
                                    lambda b, i: (k * (n // tm) + tile(b, i), 0))
    return pl.pallas_call(
        functools.partial(_combine_kernel, final=final),
        grid=(batch, out_tiles),
        in_specs=[in_row(D_MODEL), in_row(LANES),
                  pl.BlockSpec((1, D_MODEL), lambda b, i: (0, 0)),
                  y_rows(0), y_rows(1)],
        out_specs=pl.BlockSpec((tm, D_MODEL), lambda b, i: (b * out_tiles + i, 0)),
        out_shape=jax.ShapeDtypeStruct((batch * out_tiles * tm, D_MODEL), F32),
        compiler_params=pltpu.CompilerParams(dimension_semantics=("arbitrary", "arbitrary")),
        name="moe_combine",
    )(h1, route, g_final, ys, ys)


def _rope_partner(m):
    lead = m.shape[:-1]
    x = m.reshape(lead + (DA_WIDTH // DA_DQK, 2, DA_DQK // 2))
    return x[..., ::-1, :].reshape(lead + (DA_WIDTH,))


def _rope_tables(seq_len):
    pos = (jnp.arange(seq_len) - LPAD).astype(F32)
    inv = ROPE_THETA ** (-jnp.arange(0, DA_DQK, 2, dtype=F32) / DA_DQK)
    ang = pos[:, None] * inv[None, :]
    comps = DA_DV // DA_DQK
    cos = jnp.tile(jnp.cos(ang), (1, 2 * comps))
    sin = jnp.tile(jnp.concatenate([-jnp.sin(ang), jnp.sin(ang)], axis=1), (1, comps))
    q_scale = DA_DQK ** -0.5 * LOG2E
    two = lambda t: jnp.tile(t, (1, LANES // DA_DV))
    lay = lambda t: jnp.concatenate([two(t * q_scale), two(t)], axis=1)
    return lay(cos), lay(sin)


def _in_weights(w):
    hq, hf, hi, hg = (w[:, i * 512:(i + 1) * 512] for i in range(4))
    dq, dk, dv = (w[:, 2048 + i * 256:2048 + (i + 1) * 256] for i in range(3))
    fq, fk, fv = (w[:, 2816 + i * 256:2816 + (i + 1) * 256] for i in range(3))
    ff = w[:, 3584:3588]
    cat = jnp.concatenate(
        [hq, hi, hg, hf,
         dq, dk, _rope_partner(dq), _rope_partner(dk),
         fq * (FX_DH ** -0.5), fk,
         ff, jnp.zeros((D_MODEL, LANES - FX_HEADS), w.dtype)],
        axis=1)
    w_vt = jnp.concatenate([dv, fv], axis=1).T
    return cat.astype(BF16), w_vt.astype(BF16)


def _pad_lanes(v, width=LANES):
    return jnp.zeros((1, width), F32).at[0, :v.shape[0]].set(v.astype(F32))


def kernel(x, meta_tokens, norm_mix, w_in, hgrn_lb, hgrn_norm, diff_lambda, diff_norm,
           fox_bias, fox_norm, w_out, norm_ffn, w_group, b_group, w_router, b_router,
           w1, w3, w2, norm_final):
    batch, seq, d = x.shape
    depth = w_in.shape[0]
    seq_len = LPAD + N_META + seq
    n = batch * seq_len
    pad = jnp.zeros((batch, LPAD, d), x.dtype)
    meta = jnp.broadcast_to(meta_tokens.astype(x.dtype)[None], (batch, N_META, d))
    h = jnp.concatenate([pad, meta, x], axis=1).reshape(n, d)

    s_lb = jax.nn.softmax(hgrn_lb.astype(F32), axis=0)
    lb_all = jnp.cumsum(s_lb, axis=0) - s_lb[0]
    cos_t, sin_t = _rope_tables(seq_len)

    tb = MOE_BLOCK
    n_blocks = (2 * n) // tb + N_EXPERTS
    p_rows = n_blocks * tb

    xs = jnp.zeros((p_rows, d), F32)
    for layer in range(depth):
        lam_init = 0.8 - 0.6 * math.exp(-0.3 * layer)
        w_cat, w_vt = _in_weights(w_in[layer])
        hqig, hlf, hkk, dqk, fqk, dvt, fvt = _in_proj(
            h, norm_mix[layer][None, :], w_cat, w_vt, cos_t, sin_t,
            _pad_lanes(fox_bias[layer]), lb_all[layer][None, :], batch, seq_len)

        o_a = _hgrn(hqig, hlf, hkk, hgrn_norm[layer][None, :], batch, seq_len)

        lam_vecs = jnp.zeros((8, LANES), F32).at[:4, :DA_DQK].set(diff_lambda[layer].astype(F32))
        o_b = _diff_attn(dqk, dvt, lam_vecs, _pad_lanes(diff_norm[layer]), lam_init,
                         batch, seq_len)

        o_c = _fox_attn(fqk, fvt, _pad_lanes(fox_norm[layer]), batch, seq_len)

        w_rt = jnp.concatenate(
            [w_router[layer], w_group[layer],
             jnp.zeros((d, LANES - N_EXPERTS - N_GROUPS), F32)], axis=1)
        b_rt = _pad_lanes(jnp.concatenate([b_router[layer], b_group[layer]]))
        h1, route, counts = _out_router(
            o_a, o_b, o_c, h, w_out[layer].astype(BF16), norm_ffn[layer][None, :], w_rt, b_rt)

        cnt = counts[0, :N_EXPERTS].astype(jnp.int32)
        padded = (cnt + tb - 1) // tb * tb
        p_end = jnp.cumsum(padded)
        p_start = p_end - padded
        ids = route[:, 0:2].astype(jnp.int32)
        ranks = route[:, 4:6].astype(jnp.int32)
        dest = (p_start[ids] + ranks).T.reshape(2 * n)
        blk_start = jnp.arange(n_blocks + 1, dtype=jnp.int32) * tb
        blk_expert = jnp.minimum(
            jnp.sum((p_end[None, :] <= blk_start[:, None]).astype(jnp.int32), axis=1),
            N_EXPERTS - 1)
        n_active = (p_end[-1:] // tb).astype(jnp.int32)

        unused = jnp.concatenate(
            [p_start + cnt, p_end[-1:], p_end, jnp.full((1,), p_rows)]).astype(jnp.int32)
        xs, slot_back = _dispatch(dest, unused, h1, norm_ffn[layer][None, :], xs)
        ys = _experts(blk_expert, n_active, slot_back, xs, w1, w3, w2, layer, n)
        h = _combine(h1, route, norm_final[None, :], ys, batch, seq_len,
                     final=(layer == depth - 1))

    return h.reshape(batch, seq, d)
```

```python
import functools
import math

import jax
import jax.numpy as jnp
from jax import lax
from jax.experimental import pallas as pl
from jax.experimental.pallas import tpu as pltpu

F32 = jnp.float32
BF16 = jnp.bfloat16

D_MODEL = 1024
N_META = 16
HG_HEADS = 4
HG_DK = 128
HG_WIDTH = 512
HG_CHUNK = 64
HG_SUB = 8
HG_PER_STEP = 4
DA_HEADS = 4
DA_DV = 64
DA_DQK = 32
DA_WIDTH = 256
FX_HEADS = 4
FX_DH = 64
FX_WIDTH = 256
N_GROUPS = 4
EXPERTS_PER_GROUP = 8
N_EXPERTS = 32
D_EXPERT = 512
ROPE_THETA = 10000.0
EPS = 1e-6
NEG = -1e30
TINY = 1e-30
LOG2E = 1.4426950408889634

LANES = 128
SLOT = LANES
HEAD_W = 4 * SLOT
ROW_TILE = 512
PROJ_TILE = 512
ROUTER_TILE = 512
CUMSUM_ROWS = 128
ATTN_BLOCK = 512
ATTN_HEADS_PER_STEP = 4
MOE_BLOCK = 256
LPAD = ATTN_BLOCK - N_META
FIRST_KEYS = LANES
assert ATTN_BLOCK - FIRST_KEYS <= LPAD
ONE_LANE = 64
VMEM_LIMIT = 56 * 1024 * 1024

_C_HQIG = (0, 1536)
_C_HF = (1536, 2048)
_C_DQK = (2048, 2560)
_C_DQKP = (2560, 3072)
_C_FQK = (3072, 3584)
_C_FF = (3584, 3712)
IN_COLS = 3712

_NT = (((1,), (1,)), ((), ()))
_TN = (((0,), (0,)), ((), ()))


def _sigmoid(x):
    return 1.0 / (1.0 + jnp.exp(-x))


def _rms(x, g):
    return x * lax.rsqrt(jnp.mean(x * x, axis=-1, keepdims=True) + EPS) * g


def _in_proj_kernel(h_ref, g_ref, w_ref, wvt_ref, cos_ref, sin_ref, fb_ref, lb_ref,
                    hqig_ref, hlf_ref, hkk_ref, dqk_ref, fqk_ref, dvt_ref, fvt_ref,
                    carry_ref, *, tiles_per_seq):
    u = _rms(h_ref[...], g_ref[...]).astype(BF16)
    tm = u.shape[0]

    def mm(c):
        return jnp.dot(u, w_ref[:, c[0]:c[1]], preferred_element_type=F32)

    hg = mm(_C_HQIG)
    q_raw, g_raw = hg[:, 0:HG_WIDTH], hg[:, 2 * HG_WIDTH:3 * HG_WIDTH]
    hqig_ref[:, 0:HG_WIDTH] = (q_raw * _sigmoid(q_raw) * (HG_DK ** -0.5)).astype(BF16)
    hqig_ref[:, HG_WIDTH:2 * HG_WIDTH] = hg[:, HG_WIDTH:2 * HG_WIDTH].astype(BF16)
    hqig_ref[:, 2 * HG_WIDTH:3 * HG_WIDTH] = (g_raw * _sigmoid(g_raw)).astype(BF16)
    lb = lb_ref[...]
    sig = _sigmoid(mm(_C_HF))
    row = ((pl.program_id(0) % tiles_per_seq) * tm
           + lax.broadcasted_iota(jnp.int32, (tm, 1), 0))
    valid = row >= LPAD
    hlf_ref[...] = jnp.where(
        valid, jnp.log2(jnp.maximum(lb + (1.0 - lb) * sig, TINY)), 0.0)
    hkk_ref[...] = jnp.where(valid, (1.0 - lb) * (1.0 - sig), 0.0)
    half_lane = lax.broadcasted_iota(jnp.int32, (tm, LANES), 1) < DA_DV

    def spread(x):
        tiles = []
        for t in range(x.shape[1] // LANES):
            tile = x[:, t * LANES:(t + 1) * LANES]
            tiles.append(jnp.where(half_lane, tile, 0.0))
            tiles.append(jnp.where(half_lane, pltpu.roll(tile, LANES // 2, axis=1), 0.0))
        return jnp.concatenate(tiles, axis=1)

    def per_tile(t_ref):
        return jnp.concatenate([t_ref[:, 0:LANES]] * 2 + [t_ref[:, LANES:2 * LANES]] * 2, axis=1)

    rot = mm(_C_DQK) * per_tile(cos_ref) + mm(_C_DQKP) * per_tile(sin_ref)
    dqk_ref[:, 0:HEAD_W] = spread(rot[:, 0:DA_WIDTH]).astype(BF16)
    dqk_ref[:, HEAD_W:2 * HEAD_W] = spread(rot[:, DA_WIDTH:2 * DA_WIDTH]).astype(BF16)

    @pl.when(pl.program_id(0) % tiles_per_seq == 0)
    def _():
        carry_ref[...] = jnp.zeros_like(carry_ref)

    x = mm(_C_FF) + fb_ref[...]
    lf = jnp.minimum(x, 0.0) - jnp.log(1.0 + jnp.exp(-jnp.abs(x)))
    tri = (lax.broadcasted_iota(jnp.int32, (CUMSUM_ROWS, CUMSUM_ROWS), 1)
           <= lax.broadcasted_iota(jnp.int32, (CUMSUM_ROWS, CUMSUM_ROWS), 0)).astype(BF16)
    lf_terms = [t.astype(BF16) for t in _bf16_split3(lf)]
    total = carry_ref[...]
    groups = []
    for r0 in range(0, tm, CUMSUM_ROWS):
        part = sum(jnp.dot(tri, t[r0:r0 + CUMSUM_ROWS], preferred_element_type=F32)
                   for t in lf_terms) + total
        total = part[CUMSUM_ROWS - 1:CUMSUM_ROWS, :]
        groups.append(part)
    cs = jnp.concatenate(groups, axis=0)
    carry_ref[...] = total
    parts = _bf16_split3(cs * LOG2E)
    fqk = mm(_C_FQK)
    fq = spread(fqk[:, 0:FX_WIDTH]) * LOG2E
    fk = spread(fqk[:, FX_WIDTH:2 * FX_WIDTH])
    lane = lax.broadcasted_iota(jnp.int32, (tm, SLOT), 1)
    d0 = FX_DH
    for h in range(FX_HEADS):
        q = fq[:, h * SLOT:(h + 1) * SLOT]
        k = fk[:, h * SLOT:(h + 1) * SLOT]
        for i, part in enumerate(parts):
            col = part[:, h:h + 1]
            q = jnp.where(lane == d0 + i, col, q)
            k = jnp.where(lane == d0 + 3 + i, -col, k)
        q = jnp.where((lane >= d0 + 3) & (lane < d0 + 6), 1.0, q)
        k = jnp.where((lane >= d0) & (lane < d0 + 3), 1.0, k)
        fqk_ref[:, h * SLOT:(h + 1) * SLOT] = q.astype(BF16)
        fqk_ref[:, HEAD_W + h * SLOT:HEAD_W + (h + 1) * SLOT] = k.astype(BF16)

    vt = lax.dot_general(wvt_ref[...], u, _NT, preferred_element_type=F32).astype(BF16)
    spare = SLOT - DA_DV
    ones_then_zeros = jnp.where(lax.broadcasted_iota(jnp.int32, (spare, tm), 0) == 0,
                                1.0, 0.0).astype(BF16)
    for h in range(4):
        for ref, first in ((dvt_ref, 0), (fvt_ref, DA_WIDTH)):
            ref[0, h, 0, 0:DA_DV, :] = vt[first + h * DA_DV:first + (h + 1) * DA_DV]
            ref[0, h, 0, DA_DV:SLOT, :] = ones_then_zeros


def _in_proj(h, g, w_cat, w_vt, cos_t, sin_t, fox_bias_row, lb_row, batch, seq_len):
    n = h.shape[0]
    tm = PROJ_TILE
    assert tm == ATTN_BLOCK
    nk = seq_len // tm
    row = lambda w: pl.BlockSpec((tm, w), lambda i: (i, 0))
    tab = pl.BlockSpec((tm, 2 * SLOT), lambda i: (i % nk, 0))
    once = lambda r, c: pl.BlockSpec((r, c), lambda i: (0, 0), pipeline_mode=pl.Buffered(1))
    widths = (3 * HG_WIDTH, HG_WIDTH, HG_WIDTH, 2 * HEAD_W, 2 * HEAD_W)
    dtypes = (BF16, F32, F32, BF16, BF16)
    vt_spec = pl.BlockSpec((1, 4, 1, SLOT, tm), lambda i: (i // nk, 0, i % nk, 0, 0))
    vt_shape = jax.ShapeDtypeStruct((batch, 4, nk, SLOT, tm), BF16)
    return pl.pallas_call(
        functools.partial(_in_proj_kernel, tiles_per_seq=nk),
        grid=(n // tm,),
        in_specs=[row(D_MODEL),
                  pl.BlockSpec((1, D_MODEL), lambda i: (0, 0)),
                  once(D_MODEL, IN_COLS), once(DA_WIDTH + FX_WIDTH, D_MODEL),
                  tab, tab,
                  pl.BlockSpec((1, LANES), lambda i: (0, 0)),
                  pl.BlockSpec((1, HG_WIDTH), lambda i: (0, 0))],
        out_specs=[row(w) for w in widths] + [vt_spec, vt_spec],
        out_shape=[jax.ShapeDtypeStruct((n, w), t) for w, t in zip(widths, dtypes)]
        + [vt_shape, vt_shape],
        scratch_shapes=[pltpu.VMEM((1, LANES), F32)],
        compiler_params=pltpu.CompilerParams(dimension_semantics=("arbitrary",),
                                             vmem_limit_bytes=VMEM_LIMIT),
        name="in_proj",
    )(h, g, w_cat, w_vt, cos_t, sin_t, fox_bias_row, lb_row)


def _bf16_split3(x):
    hi = x.astype(BF16).astype(F32)
    r = x - hi
    mid = r.astype(BF16).astype(F32)
    return hi, mid, r - mid


def _softmax_block(s, m_prev):
    m_new = jnp.maximum(m_prev, jnp.max(s, axis=0, keepdims=True))
    alpha = jnp.exp2(m_prev - m_new)
    p = jnp.exp2(s - m_new).astype(BF16)
    return p, alpha, m_new


def _block_start(j, blk):
    return j * blk if isinstance(j, int) else pl.multiple_of(j * blk, blk)


def _causal_valid(i, j, blk, first_row=0):
    k_idx = j * blk + first_row + lax.broadcasted_iota(jnp.int32, (blk - first_row, blk), 0)
    q_idx = i * blk + lax.broadcasted_iota(jnp.int32, (blk - first_row, blk), 1)
    return (k_idx <= q_idx) & (k_idx >= LPAD)


def _pipelined_key_blocks(i, first, scores, consume):
    first()

    @pl.when(i >= 1)
    def _():
        scores(1, 0)

    def pair(t, carry):
        scores(2 * t + 2, 1)
        consume(2 * t + 1, 0, False)
        scores(2 * t + 3, 0)
        consume(2 * t + 2, 1, False)
        return carry

    lax.fori_loop(0, lax.shift_right_logical(jnp.maximum(i - 1, 0), 1), pair, 0)
    odd = (i & 1) == 1

    @pl.when(odd)
    def _():
        consume(i, 0, True)

    @pl.when((i >= 2) & jnp.logical_not(odd))
    def _():
        scores(i, 1)
        consume(i - 1, 0, False)
        consume(i, 1, True)


def _head_out(acc):
    return acc[0:ONE_LANE] / acc[ONE_LANE:ONE_LANE + 1]


def _head_rms(o, g_col):
    return o * lax.rsqrt(jnp.mean(o * o, axis=0, keepdims=True) + EPS) * g_col


def _store_head_pairs(o_ref, outs):
    for p in range(len(outs) // 2):
        pair = jnp.concatenate([outs[2 * p], outs[2 * p + 1]], axis=0)
        o_ref[:, p * LANES:(p + 1) * LANES] = pair.T.astype(BF16)


def _fox_attn_kernel(q_ref, k_ref, vt_ref, g_ref, o_ref, m_ref, acc_ref, sa_ref, sb_ref,
                     *, blk):
    i = pl.program_id(2)
    heads = range(ATTN_HEADS_PER_STEP)
    slot = [slice(hh * SLOT, (hh + 1) * SLOT) for hh in heads]
    q = [q_ref[:, slot[hh]] for hh in heads]
    bufs = (sa_ref, sb_ref)
    m_ref[...] = jnp.full(m_ref.shape, NEG, F32)
    acc_ref[...] = jnp.zeros_like(acc_ref)

    def scores(j, buf):
        rows = pl.ds(_block_start(j, blk), blk)
        for hh in heads:
            bufs[buf][hh] = lax.dot_general(k_ref[rows, slot[hh]], q[hh], _NT,
                                            preferred_element_type=F32)

    def consume(j, buf, masked):
        mask = _causal_valid(i, j, blk) if masked else None
        for hh in heads:
            s = bufs[buf][hh]
            if masked:
                s = jnp.where(mask, s, NEG)
            p, alpha, m_new = _softmax_block(s, m_ref[hh])
            acc_ref[hh] = alpha * acc_ref[hh] + jnp.dot(
                vt_ref[0, hh, j], p, preferred_element_type=F32)
            m_ref[hh] = m_new

    def first():
        lo = blk - FIRST_KEYS
        mask = _causal_valid(i, 0, blk, lo)
        for hh in heads:
            s = lax.dot_general(k_ref[lo:blk, slot[hh]], q[hh], _NT,
                                preferred_element_type=F32)
            p, alpha, m_new = _softmax_block(jnp.where(mask, s, NEG), m_ref[hh])
            acc_ref[hh] = alpha * acc_ref[hh] + jnp.dot(
                vt_ref[0, hh, 0, :, lo:blk], p, preferred_element_type=F32)
            m_ref[hh] = m_new

    _pipelined_key_blocks(i, first, scores, consume)
    _store_head_pairs(
        o_ref, [_head_rms(_head_out(acc_ref[hh]), g_ref[...]) for hh in heads])


def _key_spec(seq_len, groups):
    return pl.BlockSpec((seq_len, ATTN_HEADS_PER_STEP * SLOT),
                        lambda b, h, i: (b, groups + h), pipeline_mode=pl.Buffered(1))


def _value_spec(seq_len):
    nk = seq_len // ATTN_BLOCK
    return pl.BlockSpec((1, ATTN_HEADS_PER_STEP, nk, SLOT, ATTN_BLOCK),
                        lambda b, h, i: (b, h, 0, 0, 0), pipeline_mode=pl.Buffered(1))


def _fox_attn(fqk, fvt, gain, batch, seq_len):
    n = fqk.shape[0]
    blk = ATTN_BLOCK
    nq = seq_len // blk
    hps = ATTN_HEADS_PER_STEP
    groups = FX_HEADS // hps
    return pl.pallas_call(
        functools.partial(_fox_attn_kernel, blk=blk),
        grid=(batch, groups, nq),
        in_specs=[pl.BlockSpec((blk, hps * SLOT), lambda b, h, i: (b * nq + i, h)),
                  _key_spec(seq_len, groups),
                  _value_spec(seq_len),
                  pl.BlockSpec((ONE_LANE, 1), lambda b, h, i: (0, 0))],
        out_specs=pl.BlockSpec((blk, hps * ONE_LANE), lambda b, h, i: (b * nq + i, h)),
        out_shape=jax.ShapeDtypeStruct((n, 4 * ONE_LANE), BF16),
        scratch_shapes=[pltpu.VMEM((hps, 1, blk), F32), pltpu.VMEM((hps, SLOT, blk), F32),
                        pltpu.VMEM((hps, blk, blk), F32), pltpu.VMEM((hps, blk, blk), F32)],
        compiler_params=pltpu.CompilerParams(
            dimension_semantics=("arbitrary", "arbitrary", "arbitrary"),
            vmem_limit_bytes=VMEM_LIMIT),
        name="fox_attn",
    )(fqk, fqk, fvt, gain)


def _diff_attn_kernel(q_ref, k_ref, vt_ref, lam_ref, g_ref, o_ref, m_ref, acc_ref,
                      sa_ref, sb_ref, *, blk, lam_init):
    i = pl.program_id(2)
    heads = range(ATTN_HEADS_PER_STEP)
    slot = lambda s: slice(s * SLOT, (s + 1) * SLOT)
    lane = lax.broadcasted_iota(jnp.int32, (blk, SLOT), 1)
    q = []
    for hh in heads:
        q_h = q_ref[:, slot(hh)]
        q.append([jnp.where((lane >= c * DA_DQK) & (lane < (c + 1) * DA_DQK), q_h,
                            jnp.zeros_like(q_h)) for c in range(2)])
    bufs = (sa_ref, sb_ref)
    m_ref[...] = jnp.full(m_ref.shape, NEG, F32)
    acc_ref[...] = jnp.zeros_like(acc_ref)

    def scores(j, buf):
        rows = pl.ds(_block_start(j, blk), blk)
        for hh in heads:
            k_h = k_ref[rows, slot(hh)]
            for c in range(2):
                bufs[buf][2 * hh + c] = lax.dot_general(k_h, q[hh][c], _NT,
                                                        preferred_element_type=F32)

    def consume(j, buf, masked):
        mask = _causal_valid(i, j, blk) if masked else None
        for hh in heads:
            ps, alphas = [], []
            for c in range(2):
                hc = 2 * hh + c
                s = bufs[buf][hc]
                if masked:
                    s = jnp.where(mask, s, NEG)
                p, alpha, m_new = _softmax_block(s, m_ref[hc])
                m_ref[hc] = m_new
                ps.append(p)
                alphas.append(alpha)
            pv = jnp.dot(vt_ref[0, hh, j], jnp.concatenate(ps, axis=1),
                         preferred_element_type=F32)
            for c in range(2):
                hc = 2 * hh + c
                acc_ref[hc] = alphas[c] * acc_ref[hc] + pv[:, c * blk:(c + 1) * blk]

    def first():
        lo = blk - FIRST_KEYS
        mask = _causal_valid(i, 0, blk, lo)
        for hh in heads:
            k_h = k_ref[lo:blk, slot(hh)]
            ps, alphas = [], []
            for c in range(2):
                hc = 2 * hh + c
                s = lax.dot_general(k_h, q[hh][c], _NT, preferred_element_type=F32)
                p, alpha, m_new = _softmax_block(jnp.where(mask, s, NEG), m_ref[hc])
                m_ref[hc] = m_new
                ps.append(p)
                alphas.append(alpha)
            pv = jnp.dot(vt_ref[0, hh, 0, :, lo:blk], jnp.concatenate(ps, axis=1),
                         preferred_element_type=F32)
            for c in range(2):
                hc = 2 * hh + c
                acc_ref[hc] = alphas[c] * acc_ref[hc] + pv[:, c * blk:(c + 1) * blk]

    _pipelined_key_blocks(i, first, scores, consume)
    lv = lam_ref[...]
    lam = (jnp.exp(jnp.sum(lv[0:1] * lv[1:2], axis=-1, keepdims=True))
           - jnp.exp(jnp.sum(lv[2:3] * lv[3:4], axis=-1, keepdims=True)) + lam_init)
    outs = []
    for hh in heads:
        o = _head_out(acc_ref[2 * hh]) - lam * _head_out(acc_ref[2 * hh + 1])
        outs.append(_head_rms(o, g_ref[...]) * (1.0 - lam_init))
    _store_head_pairs(o_ref, outs)


def _diff_attn(dqk, dvt, lam_vecs, gain, lam_init, batch, seq_len):
    n = dqk.shape[0]
    blk = ATTN_BLOCK
    nq = seq_len // blk
    hps = ATTN_HEADS_PER_STEP
    groups = DA_HEADS // hps
    return pl.pallas_call(
        functools.partial(_diff_attn_kernel, blk=blk, lam_init=lam_init),
        grid=(batch, groups, nq),
        in_specs=[pl.BlockSpec((blk, hps * SLOT), lambda b, h, i: (b * nq + i, h)),
                  _key_spec(seq_len, groups),
                  _value_spec(seq_len),
                  pl.BlockSpec((8, LANES), lambda b, h, i: (0, 0)),
                  pl.BlockSpec((ONE_LANE, 1), lambda b, h, i: (0, 0))],
        out_specs=pl.BlockSpec((blk, hps * ONE_LANE), lambda b, h, i: (b * nq + i, h)),
        out_shape=jax.ShapeDtypeStruct((n, 4 * ONE_LANE), BF16),
        scratch_shapes=[pltpu.VMEM((2 * hps, 1, blk), F32),
                        pltpu.VMEM((2 * hps, SLOT, blk), F32),
                        pltpu.VMEM((2 * hps, blk, blk), F32),
                        pltpu.VMEM((2 * hps, blk, blk), F32)],
        compiler_params=pltpu.CompilerParams(
            dimension_semantics=("arbitrary", "arbitrary", "arbitrary"),
            vmem_limit_bytes=VMEM_LIMIT),
        name="diff_attn",
    )(dqk, dqk, dvt, lam_vecs, gain)


def _hgrn_kernel(q_ref, i_ref, g_ref, lf_ref, kk_ref, gain_ref, o_ref,
                 st_ref, hs_ref, *, tl):
    @pl.when(pl.program_id(2) == 0)
    def _():
        st_ref[...] = jnp.zeros_like(st_ref)

    c_rows, sub = HG_CHUNK, HG_SUB
    tri = (lax.broadcasted_iota(jnp.int32, (c_rows, c_rows), 1)
           <= lax.broadcasted_iota(jnp.int32, (c_rows, c_rows), 0)).astype(BF16)
    t_sub = lax.broadcasted_iota(jnp.int32, (sub, 1), 0)
    lane_c = lax.broadcasted_iota(jnp.int32, (sub, c_rows), 1)

    t_all = lax.broadcasted_iota(jnp.int32, (c_rows, c_rows), 0)
    s_all = lax.broadcasted_iota(jnp.int32, (c_rows, c_rows), 1)
    level_masks = []
    g = sub
    while g < c_rows:
        level_masks.append(((t_all // g) == (s_all // g) + 1) & ((s_all // g) % 2 == 0))
        g *= 2

    heads = range(HG_PER_STEP)
    cols = [slice(hh * HG_DK, (hh + 1) * HG_DK) for hh in heads]

    def chunk(c, carry):
        r0 = pl.multiple_of(c * c_rows, c_rows)
        rows = pl.ds(r0, c_rows)
        G, kk, qs, v, st, o_inter = [], [], [], [], [], []
        for hh in heads:
            kk.append(kk_ref[rows, cols[hh]])
            G.append(sum(jnp.dot(tri, t.astype(BF16), preferred_element_type=F32)
                         for t in _bf16_split3(lf_ref[rows, cols[hh]])))
            qs.append(q_ref[rows, cols[hh]].astype(F32))
            v.append(i_ref[rows, cols[hh]])
            hs_ref[hh] = G[hh] - jnp.log2(kk[hh])
            st.append(st_ref[hh])
            o_inter.append(lax.dot_general(
                (qs[hh] * jnp.exp2(G[hh])).astype(BF16), st[hh].astype(BF16), _NT,
                preferred_element_type=F32))
        a_mat = [jnp.zeros((c_rows, c_rows), F32) for _ in heads]
        g = sub
        for mask in level_masks:
            for hh in heads:
                ref = jnp.concatenate(
                    [jnp.broadcast_to(G[hh][p + g - 1:p + g], (2 * g, HG_DK))
                     for p in range(0, c_rows, 2 * g)], axis=0)
                e = jnp.exp2(-jnp.abs(G[hh] - ref))
                part = lax.dot_general((qs[hh] * e).astype(BF16), (kk[hh] * e).astype(BF16),
                                       _NT, preferred_element_type=F32)
                a_mat[hh] = jnp.where(mask, part, a_mat[hh])
            g *= 2
        a_rows = [[] for _ in heads]
        for b in range(c_rows // sub):
            lo = b * sub
            for hh in heads:
                q_b = qs[hh][lo:lo + sub]
                g_b = G[hh][lo:lo + sub]
                a_blk = a_mat[hh][lo:lo + sub]
                for s in range(lo, lo + sub):
                    y = q_b * jnp.exp2(g_b - hs_ref[hh, s:s + 1, :])
                    a_blk = jnp.where(lane_c == s, jnp.sum(y, axis=-1, keepdims=True), a_blk)
                a_rows[hh].append(jnp.where(lane_c <= t_sub + lo, a_blk, 0.0))
        for hh in heads:
            a_full = jnp.concatenate(a_rows[hh], axis=0).astype(BF16)
            o = o_inter[hh] + jnp.dot(a_full, v[hh], preferred_element_type=F32)
            g_last = G[hh][c_rows - 1:c_rows]
            kd = kk[hh] * jnp.exp2(g_last - G[hh])
            st_ref[hh] = st[hh] * jnp.exp2(g_last) + lax.dot_general(
                v[hh], kd.astype(BF16), _TN, preferred_element_type=F32)
            o = _rms(o, gain_ref[...]) * g_ref[rows, cols[hh]].astype(F32)
            o_ref[rows, cols[hh]] = o.astype(BF16)
        return carry

    lax.fori_loop(0, tl // c_rows, chunk, 0)


def _hgrn(hqig, hlf, hkk, gain, batch, seq_len):
    n = hqig.shape[0]
    tl = ROW_TILE
    nt = seq_len // tl
    w = HG_PER_STEP * HG_DK
    groups = HG_HEADS // HG_PER_STEP
    col = lambda off: pl.BlockSpec((tl, w), lambda b, h, j: (b * nt + j, off + h))
    return pl.pallas_call(
        functools.partial(_hgrn_kernel, tl=tl),
        grid=(batch, groups, nt),
        in_specs=[col(0), col(groups), col(2 * groups), col(0), col(0),
                  pl.BlockSpec((1, HG_DK), lambda b, h, j: (0, 0))],
        out_specs=col(0),
        out_shape=jax.ShapeDtypeStruct((n, HG_WIDTH), BF16),
        scratch_shapes=[pltpu.VMEM((HG_PER_STEP, HG_DK, HG_DK), F32),
                        pltpu.VMEM((HG_PER_STEP, HG_CHUNK, HG_DK), F32)],
        compiler_params=pltpu.CompilerParams(
            dimension_semantics=("arbitrary", "arbitrary", "arbitrary")),
        name="hgrn2",
    )(hqig, hqig, hqig, hlf, hkk, gain)


def _out_router_kernel(oa_ref, ob_ref, oc_ref, h_ref, wo_ref, g_ref, wrh_ref, wrl_ref, br_ref,
                       h1_ref, route_ref, cnt_ref, carry_ref):
    @pl.when(pl.program_id(0) == 0)
    def _():
        carry_ref[...] = jnp.zeros_like(carry_ref)

    h1 = (h_ref[...]
          + jnp.dot(oa_ref[...], wo_ref[0:512, :], preferred_element_type=F32)
          + jnp.dot(ob_ref[...], wo_ref[512:768, :], preferred_element_type=F32)
          + jnp.dot(oc_ref[...], wo_ref[768:1024, :], preferred_element_type=F32))
    h1_ref[...] = h1
    u = _rms(h1, g_ref[...])
    u_hi = u.astype(BF16)
    u_lo = (u - u_hi.astype(F32)).astype(BF16)
    logits = (jnp.dot(u_hi, wrh_ref[...], preferred_element_type=F32)
              + (jnp.dot(u_hi, wrl_ref[...], preferred_element_type=F32)
                 + jnp.dot(u_lo, wrh_ref[...], preferred_element_type=F32))
              + br_ref[...])
    tm = logits.shape[0]
    lane = lax.broadcasted_iota(jnp.int32, (tm, LANES), 1).astype(F32)
    big = float(LANES)
    is_g = (lane >= N_EXPERTS) & (lane < N_EXPERTS + N_GROUPS)
    gl = jnp.where(is_g, logits, -jnp.inf)
    gmax = jnp.max(gl, axis=-1, keepdims=True)
    gsel = jnp.min(jnp.where(gl == gmax, lane, big), axis=-1, keepdims=True) - N_EXPERTS
    p_g = 1.0 / jnp.sum(jnp.exp(gl - gmax), axis=-1, keepdims=True)
    lo = gsel * EXPERTS_PER_GROUP
    el = jnp.where((lane >= lo) & (lane < lo + EXPERTS_PER_GROUP), logits, -jnp.inf)
    m1 = jnp.max(el, axis=-1, keepdims=True)
    i1 = jnp.min(jnp.where(el == m1, lane, big), axis=-1, keepdims=True)
    el2 = jnp.where(lane == i1, -jnp.inf, el)
    m2 = jnp.max(el2, axis=-1, keepdims=True)
    i2 = jnp.min(jnp.where(el2 == m2, lane, big), axis=-1, keepdims=True)
    r = jnp.exp(m2 - m1)
    gate1 = p_g / (1.0 + r)
    gate2 = gate1 * r
    oh1 = lane == i1
    oh2 = lane == i2
    onehot = jnp.where(oh1 | oh2, 1.0, 0.0)
    tri = (lax.broadcasted_iota(jnp.int32, (tm, tm), 1)
           < lax.broadcasted_iota(jnp.int32, (tm, tm), 0)).astype(BF16)
    before = jnp.dot(tri, onehot.astype(BF16), preferred_element_type=F32) + carry_ref[...]
    rank1 = jnp.sum(jnp.where(oh1, before, 0.0), axis=-1, keepdims=True)
    rank2 = jnp.sum(jnp.where(oh2, before, 0.0), axis=-1, keepdims=True)
    total = carry_ref[...] + jnp.sum(onehot, axis=0, keepdims=True)
    carry_ref[...] = total
    cnt_ref[...] = total
    route = jnp.where(lane == 0, i1, 0.0)
    for idx, val in ((1, i2), (2, gate1), (3, gate2), (4, rank1), (5, rank2)):
        route = jnp.where(lane == idx, val, route)
    route_ref[...] = route


def _out_router(oa, ob, oc, h, w_out, g, w_rt, b_rt):
    n = h.shape[0]
    w_rt_hi = w_rt.astype(BF16)
    w_rt_lo = (w_rt - w_rt_hi.astype(F32)).astype(BF16)
    tm = ROUTER_TILE
    row = lambda w: pl.BlockSpec((tm, w), lambda i: (i, 0))
    const = lambda r, c: pl.BlockSpec((r, c), lambda i: (0, 0))
    return pl.pallas_call(
        _out_router_kernel,
        grid=(n // tm,),
        in_specs=[row(HG_WIDTH), row(DA_WIDTH), row(FX_WIDTH), row(D_MODEL),
                  const(D_MODEL, D_MODEL),
                  const(1, D_MODEL), const(D_MODEL, LANES), const(D_MODEL, LANES),
                  const(1, LANES)],
        out_specs=[row(D_MODEL), row(LANES), const(1, LANES)],
        out_shape=[jax.ShapeDtypeStruct((n, D_MODEL), F32),
                   jax.ShapeDtypeStruct((n, LANES), F32),
                   jax.ShapeDtypeStruct((1, LANES), F32)],
        scratch_shapes=[pltpu.VMEM((1, LANES), F32)],
        compiler_params=pltpu.CompilerParams(dimension_semantics=("arbitrary",)),
        name="out_router",
    )(oa, ob, oc, h, w_out, g, w_rt_hi, w_rt_lo, b_rt)


def _dispatch_kernel(dest_ref, unused_ref, h_ref, g_ref, xs_in_ref, xs_ref, back_ref,
                     u_ref, sem, *, tm, n, tb):
    del xs_in_ref
    i = pl.program_id(0)
    n_tiles = n // tm
    base = (i - 1) * tm
    cur = i % 2

    @pl.when(i == 0)
    def _():
        def fill(s, carry):
            parity = lax.shift_right_logical(s, tb.bit_length() - 1) & 1
            back_ref[s] = 2 * n + parity * tb + (s & (tb - 1))
            return carry

        n_ranges = unused_ref.shape[0] // 2

        def one_range(e, carry):
            return lax.fori_loop(unused_ref[e], unused_ref[n_ranges + e], fill, carry)

        lax.fori_loop(0, n_ranges, one_range, 0)

    def issue(r, carry):
        for k in range(2):
            src = k * n + base + r
            d = dest_ref[src]
            back_ref[d] = src
            pltpu.make_async_copy(u_ref.at[1 - cur, pl.ds(r, 1)], xs_ref.at[pl.ds(d, 1)],
                                  sem).start()
        return carry

    @pl.when(i >= 1)
    def _():
        lax.fori_loop(0, tm, issue, 0, unroll=8)

    @pl.when(i < n_tiles)
    def _():
        u_ref[cur] = _rms(h_ref[...], g_ref[...])

    @pl.when(i >= 1)
    def _():
        for _ in range(2):
            pltpu.make_async_copy(u_ref.at[1 - cur], xs_ref.at[pl.ds(0, tm)], sem).wait()


def _dispatch(dest, unused, h1, g, xs_init):
    n = h1.shape[0]
    tm = ROW_TILE
    p_rows = xs_init.shape[0]
    return pl.pallas_call(
        functools.partial(_dispatch_kernel, tm=tm, n=n, tb=MOE_BLOCK),
        grid_spec=pltpu.PrefetchScalarGridSpec(
            num_scalar_prefetch=2,
            grid=(n // tm + 1,),
            in_specs=[pl.BlockSpec((tm, D_MODEL),
                                   lambda i, d, un: (jnp.minimum(i, n // tm - 1), 0)),
                      pl.BlockSpec((1, D_MODEL), lambda i, d, un: (0, 0)),
                      pl.BlockSpec(memory_space=pl.ANY)],
            out_specs=[pl.BlockSpec(memory_space=pl.ANY),
                       pl.BlockSpec(memory_space=pltpu.SMEM)],
            scratch_shapes=[pltpu.VMEM((2, tm, D_MODEL), F32),
                            pltpu.SemaphoreType.DMA(())]),
        out_shape=[jax.ShapeDtypeStruct(xs_init.shape, xs_init.dtype),
                   jax.ShapeDtypeStruct((p_rows,), jnp.int32)],
        input_output_aliases={4: 0},
        compiler_params=pltpu.CompilerParams(dimension_semantics=("arbitrary",)),
        name="moe_dispatch",
    )(dest, unused, h1, g, xs_init)


def _expert_kernel(be_ref, na_ref, back_ref, x_ref, w1_ref, w3_ref, w2_ref, out_ref,
                   y_ref, w1b_ref, w3b_ref, w2b_ref, sem, *, tb):
    j = pl.program_id(0)
    na = na_ref[0]
    cur = j % 2

    def scatter(block, buf):
        for r in range(tb):
            pltpu.make_async_copy(y_ref.at[buf, pl.ds(r, 1)],
                                  out_ref.at[pl.ds(back_ref[block * tb + r], 1)],
                                  sem.at[buf]).start()

    def compute(buf):
        x = x_ref[...].astype(BF16)
        a = jnp.dot(x, w1b_ref[...], preferred_element_type=F32)
        b = jnp.dot(x, w3b_ref[...], preferred_element_type=F32)
        act = (a * _sigmoid(a) * b).astype(BF16)
        y_ref[buf] = jnp.dot(act, w2b_ref[...], preferred_element_type=F32)

    new_expert = (j == 0) | (be_ref[j] != be_ref[jnp.maximum(j - 1, 0)])

    @pl.when((j < na) & new_expert)
    def _():
        w1b_ref[...] = w1_ref[0].astype(BF16)
        w3b_ref[...] = w3_ref[0].astype(BF16)
        w2b_ref[...] = w2_ref[0].astype(BF16)

    @pl.when((j >= 2) & (j < na + 2))
    def _():
        pltpu.make_async_copy(y_ref.at[cur], out_ref.at[pl.ds(0, tb)], sem.at[cur]).wait()

    @pl.when(j == 0)
    def _():
        y_ref[...] = jnp.zeros_like(y_ref)
        first_spare = out_ref.shape[0] - 2 * tb
        spare = [pltpu.make_async_copy(
            y_ref.at[b], out_ref.at[pl.ds(first_spare + b * tb, tb)], sem.at[b])
            for b in range(2)]
        for copy in spare:
            copy.start()
        for copy in spare:
            copy.wait()
        compute(cur)

    @pl.when((j >= 1) & (j < na))
    def _():
        scatter(j - 1, 1 - cur)
        compute(cur)

    @pl.when(j == na)
    def _():
        scatter(j - 1, 1 - cur)


def _experts(blk_expert, n_active, slot_back, xs, w1, w3, w2, layer, n):
    p_rows = xs.shape[0]
    tb = MOE_BLOCK
    last = lambda j, na: jnp.maximum(jnp.minimum(j, na[0] - 1), 0)
    w_spec = lambda r, c: pl.BlockSpec((None, 1, r, c),
                                       lambda j, be, na, back: (layer, be[j], 0, 0))
    return pl.pallas_call(
        functools.partial(_expert_kernel, tb=tb),
        grid_spec=pltpu.PrefetchScalarGridSpec(
            num_scalar_prefetch=3,
            grid=(p_rows // tb + 1,),
            in_specs=[pl.BlockSpec((tb, D_MODEL), lambda j, be, na, back: (last(j, na), 0)),
                      w_spec(D_MODEL, D_EXPERT), w_spec(D_MODEL, D_EXPERT),
                      w_spec(D_EXPERT, D_MODEL)],
            out_specs=pl.BlockSpec(memory_space=pl.ANY),
            scratch_shapes=[pltpu.VMEM((2, tb, D_MODEL), F32),
                            pltpu.VMEM((D_MODEL, D_EXPERT), BF16),
                            pltpu.VMEM((D_MODEL, D_EXPERT), BF16),
                            pltpu.VMEM((D_EXPERT, D_MODEL), BF16),
                            pltpu.SemaphoreType.DMA((2,))]),
        out_shape=jax.ShapeDtypeStruct((2 * n + 2 * tb, D_MODEL), F32),
        compiler_params=pltpu.CompilerParams(dimension_semantics=("arbitrary",),
                                             vmem_limit_bytes=VMEM_LIMIT),
        name="moe_experts",
    )(blk_expert, n_active, slot_back, xs, w1, w3, w2)


def _combine_kernel(h_ref, route_ref, gfin_ref, y0_ref, y1_ref, o_ref, *, final):
    route = route_ref[...]
    out = h_ref[...] + route[:, 2:3] * y0_ref[...] + route[:, 3:4] * y1_ref[...]
    if final:
        out = _rms(out, gfin_ref[...])
    o_ref[...] = out


def _combine(h1, route, g_final, ys, batch, seq_len, final):
    n = h1.shape[0]
    tm = ROW_TILE
    tiles_per_seq = seq_len // tm
    skip_tiles = (LPAD + N_META) // tm if final else 0
    out_tiles = tiles_per_seq - skip_tiles
    tile = lambda b, i: b * tiles_per_seq + skip_tiles + i
    in_row = lambda w: pl.BlockSpec((tm, w), lambda b, i: (tile(b, i), 0))
    y_rows = lambda k: pl.BlockSpec((tm, D_MODEL),
                                    lambda b, i: (k * (n // tm) + tile(b, i), 0))
    return pl.pallas_call(
        functools.partial(_combine_kernel, final=final),
        grid=(batch, out_tiles),
        in_specs=[in_row(D_MODEL), in_row(LANES),
                  pl.BlockSpec((1, D_MODEL), lambda b, i: (0, 0)),
                  y_rows(0), y_rows(1)],
        out_specs=pl.BlockSpec((tm, D_MODEL), lambda b, i: (b * out_tiles + i, 0)),
        out_shape=jax.ShapeDtypeStruct((batch * out_tiles * tm, D_MODEL), F32),
        compiler_params=pltpu.CompilerParams(dimension_semantics=("arbitrary", "arbitrary")),
        name="moe_combine",
    )(h1, route, g_final, ys, ys)


def _rope_partner(m):
    lead = m.shape[:-1]
    x = m.reshape(lead + (DA_WIDTH // DA_DQK, 2, DA_DQK // 2))
    return x[..., ::-1, :].reshape(lead + (DA_WIDTH,))


def _rope_tables(seq_len):
    pos = (jnp.arange(seq_len) - LPAD).astype(F32)
    inv = ROPE_THETA ** (-jnp.arange(0, DA_DQK, 2, dtype=F32) / DA_DQK)
    ang = pos[:, None] * inv[None, :]
    comps = DA_DV // DA_DQK
    cos = jnp.tile(jnp.cos(ang), (1, 2 * comps))
    sin = jnp.tile(jnp.concatenate([-jnp.sin(ang), jnp.sin(ang)], axis=1), (1, comps))
    q_scale = DA_DQK ** -0.5 * LOG2E
    two = lambda t: jnp.tile(t, (1, LANES // DA_DV))
    lay = lambda t: jnp.concatenate([two(t * q_scale), two(t)], axis=1)
    return lay(cos), lay(sin)


def _in_weights(w):
    hq, hf, hi, hg = (w[:, i * 512:(i + 1) * 512] for i in range(4))
    dq, dk, dv = (w[:, 2048 + i * 256:2048 + (i + 1) * 256] for i in range(3))
    fq, fk, fv = (w[:, 2816 + i * 256:2816 + (i + 1) * 256] for i in range(3))
    ff = w[:, 3584:3588]
    cat = jnp.concatenate(
        [hq, hi, hg, hf,
         dq, dk, _rope_partner(dq), _rope_partner(dk),
         fq * (FX_DH ** -0.5), fk,
         ff, jnp.zeros((D_MODEL, LANES - FX_HEADS), w.dtype)],
        axis=1)
    w_vt = jnp.concatenate([dv, fv], axis=1).T
    return cat.astype(BF16), w_vt.astype(BF16)


def _pad_lanes(v, width=LANES):
    return jnp.zeros((1, width), F32).at[0, :v.shape[0]].set(v.astype(F32))


def kernel(x, meta_tokens, norm_mix, w_in, hgrn_lb, hgrn_norm, diff_lambda, diff_norm,
           fox_bias, fox_norm, w_out, norm_ffn, w_group, b_group, w_router, b_router,
           w1, w3, w2, norm_final):
    batch, seq, d = x.shape
    depth = w_in.shape[0]
    seq_len = LPAD + N_META + seq
    n = batch * seq_len
    pad = jnp.zeros((batch, LPAD, d), x.dtype)
    meta = jnp.broadcast_to(meta_tokens.astype(x.dtype)[None], (batch, N_META, d))
    h = jnp.concatenate([pad, meta, x], axis=1).reshape(n, d)

    s_lb = jax.nn.softmax(hgrn_lb.astype(F32), axis=0)
    lb_all = jnp.cumsum(s_lb, axis=0) - s_lb[0]
    cos_t, sin_t = _rope_tables(seq_len)

    tb = MOE_BLOCK
    n_blocks = (2 * n) // tb + N_EXPERTS
    p_rows = n_blocks * tb

    xs = jnp.zeros((p_rows, d), F32)
    for layer in range(depth):
        lam_init = 0.8 - 0.6 * math.exp(-0.3 * layer)
        w_cat, w_vt = _in_weights(w_in[layer])
        hqig, hlf, hkk, dqk, fqk, dvt, fvt = _in_proj(
            h, norm_mix[layer][None, :], w_cat, w_vt, cos_t, sin_t,
            _pad_lanes(fox_bias[layer]), lb_all[layer][None, :], batch, seq_len)

        o_a = _hgrn(hqig, hlf, hkk, hgrn_norm[layer][None, :], batch, seq_len)

        lam_vecs = jnp.zeros((8, LANES), F32).at[:4, :DA_DQK].set(diff_lambda[layer].astype(F32))
        o_b = _diff_attn(dqk, dvt, lam_vecs, diff_norm[layer][:, None], lam_init,
                         batch, seq_len)

        o_c = _fox_attn(fqk, fvt, fox_norm[layer][:, None], batch, seq_len)

        w_rt = jnp.concatenate(
            [w_router[layer], w_group[layer],
             jnp.zeros((d, LANES - N_EXPERTS - N_GROUPS), F32)], axis=1)
        b_rt = _pad_lanes(jnp.concatenate([b_router[layer], b_group[layer]]))
        h1, route, counts = _out_router(
            o_a, o_b, o_c, h, w_out[layer].astype(BF16), norm_ffn[layer][None, :], w_rt, b_rt)

        cnt = counts[0, :N_EXPERTS].astype(jnp.int32)
        padded = (cnt + tb - 1) // tb * tb
        p_end = jnp.cumsum(padded)
        p_start = p_end - padded
        ids = route[:, 0:2].astype(jnp.int32)
        ranks = route[:, 4:6].astype(jnp.int32)
        dest = (p_start[ids] + ranks).T.reshape(2 * n)
        blk_start = jnp.arange(n_blocks + 1, dtype=jnp.int32) * tb
        blk_expert = jnp.minimum(
            jnp.sum((p_end[None, :] <= blk_start[:, None]).astype(jnp.int32), axis=1),
            N_EXPERTS - 1)
        n_active = (p_end[-1:] // tb).astype(jnp.int32)

        unused = jnp.concatenate(
            [p_start + cnt, p_end[-1:], p_end, jnp.full((1,), p_rows)]).astype(jnp.int32)
        xs, slot_back = _dispatch(dest, unused, h1, norm_ffn[layer][None, :], xs)
        ys = _experts(blk_expert, n_active, slot_back, xs, w1, w3, w2, layer, n)
        h = _combine(h1, route, norm_final[None, :], ys, batch, seq_len,
                     final=(layer == depth - 1))

    return h.reshape(batch, seq, d)
```

```python
import functools
import math

import jax
import jax.numpy as jnp
from jax import lax
from jax.experimental import pallas as pl
from jax.experimental.pallas import tpu as pltpu

F32 = jnp.float32
BF16 = jnp.bfloat16

D_MODEL = 1024
N_META = 16
HG_HEADS = 4
HG_DK = 128
HG_WIDTH = 512
HG_CHUNK = 64
HG_SUB = 8
HG_PER_STEP = 4
DA_HEADS = 4
DA_DV = 64
DA_DQK = 32
DA_WIDTH = 256
FX_HEADS = 4
FX_DH = 64
FX_WIDTH = 256
N_GROUPS = 4
EXPERTS_PER_GROUP = 8
N_EXPERTS = 32
D_EXPERT = 512
ROPE_THETA = 10000.0
EPS = 1e-6
NEG = -1e30
TINY = 1e-30
LOG2E = 1.4426950408889634

LANES = 128
SLOT = LANES
HEAD_W = 4 * SLOT
ROW_TILE = 512
PROJ_TILE = 512
ROUTER_TILE = 512
CUMSUM_ROWS = 128
ATTN_BLOCK = 512
ATTN_HEADS_PER_STEP = 4
MOE_BLOCK = 256
LPAD = ATTN_BLOCK - N_META
FIRST_KEYS = LANES
assert ATTN_BLOCK - FIRST_KEYS <= LPAD
ONE_LANE = 64
VMEM_LIMIT = 56 * 1024 * 1024

_C_HQIG = (0, 1536)
_C_HF = (1536, 2048)
_C_DQK = (2048, 2560)
_C_DQKP = (2560, 3072)
_C_FQK = (3072, 3584)
_C_FF = (3584, 3712)
IN_COLS = 3712

_NT = (((1,), (1,)), ((), ()))
_TN = (((0,), (0,)), ((), ()))


def _sigmoid(x):
    return 1.0 / (1.0 + jnp.exp(-x))


def _rms(x, g):
    return x * lax.rsqrt(jnp.mean(x * x, axis=-1, keepdims=True) + EPS) * g


def _in_proj_kernel(h_ref, g_ref, w_ref, wvt_ref, cos_ref, sin_ref, fb_ref, lb_ref,
                    hqig_ref, hlf_ref, hkk_ref, dqk_ref, fqk_ref, dvt_ref, fvt_ref,
                    carry_ref, *, tiles_per_seq):
    u = _rms(h_ref[...], g_ref[...]).astype(BF16)
    tm = u.shape[0]

    def mm(c):
        return jnp.dot(u, w_ref[:, c[0]:c[1]], preferred_element_type=F32)

    hg = mm(_C_HQIG)
    q_raw, g_raw = hg[:, 0:HG_WIDTH], hg[:, 2 * HG_WIDTH:3 * HG_WIDTH]
    hqig_ref[:, 0:HG_WIDTH] = (q_raw * _sigmoid(q_raw) * (HG_DK ** -0.5)).astype(BF16)
    hqig_ref[:, HG_WIDTH:2 * HG_WIDTH] = hg[:, HG_WIDTH:2 * HG_WIDTH].astype(BF16)
    hqig_ref[:, 2 * HG_WIDTH:3 * HG_WIDTH] = (g_raw * _sigmoid(g_raw)).astype(BF16)
    lb = lb_ref[...]
    sig = _sigmoid(mm(_C_HF))
    row = ((pl.program_id(0) % tiles_per_seq) * tm
           + lax.broadcasted_iota(jnp.int32, (tm, 1), 0))
    valid = row >= LPAD
    hlf_ref[...] = jnp.where(
        valid, jnp.log2(jnp.maximum(lb + (1.0 - lb) * sig, TINY)), 0.0)
    hkk_ref[...] = jnp.where(valid, (1.0 - lb) * (1.0 - sig), 0.0)
    half_lane = lax.broadcasted_iota(jnp.int32, (tm, LANES), 1) < DA_DV

    def spread(x):
        tiles = []
        for t in range(x.shape[1] // LANES):
            tile = x[:, t * LANES:(t + 1) * LANES]
            tiles.append(jnp.where(half_lane, tile, 0.0))
            tiles.append(jnp.where(half_lane, pltpu.roll(tile, LANES // 2, axis=1), 0.0))
        return jnp.concatenate(tiles, axis=1)

    def per_tile(t_ref):
        return jnp.concatenate([t_ref[:, 0:LANES]] * 2 + [t_ref[:, LANES:2 * LANES]] * 2, axis=1)

    rot = mm(_C_DQK) * per_tile(cos_ref) + mm(_C_DQKP) * per_tile(sin_ref)
    dqk_ref[:, 0:HEAD_W] = spread(rot[:, 0:DA_WIDTH]).astype(BF16)
    dqk_ref[:, HEAD_W:2 * HEAD_W] = spread(rot[:, DA_WIDTH:2 * DA_WIDTH]).astype(BF16)

    @pl.when(pl.program_id(0) % tiles_per_seq == 0)
    def _():
        carry_ref[...] = jnp.zeros_like(carry_ref)

    x = mm(_C_FF) + fb_ref[...]
    lf = jnp.minimum(x, 0.0) - jnp.log(1.0 + jnp.exp(-jnp.abs(x)))
    tri = (lax.broadcasted_iota(jnp.int32, (CUMSUM_ROWS, CUMSUM_ROWS), 1)
           <= lax.broadcasted_iota(jnp.int32, (CUMSUM_ROWS, CUMSUM_ROWS), 0)).astype(BF16)
    lf_terms = [t.astype(BF16) for t in _bf16_split3(lf)]
    total = carry_ref[...]
    groups = []
    for r0 in range(0, tm, CUMSUM_ROWS):
        part = sum(jnp.dot(tri, t[r0:r0 + CUMSUM_ROWS], preferred_element_type=F32)
                   for t in lf_terms) + total
        total = part[CUMSUM_ROWS - 1:CUMSUM_ROWS, :]
        groups.append(part)
    cs = jnp.concatenate(groups, axis=0)
    carry_ref[...] = total
    parts = _bf16_split3(cs * LOG2E)
    fqk = mm(_C_FQK)
    fq = spread(fqk[:, 0:FX_WIDTH]) * LOG2E
    fk = spread(fqk[:, FX_WIDTH:2 * FX_WIDTH])
    lane = lax.broadcasted_iota(jnp.int32, (tm, SLOT), 1)
    d0 = FX_DH
    for h in range(FX_HEADS):
        q = fq[:, h * SLOT:(h + 1) * SLOT]
        k = fk[:, h * SLOT:(h + 1) * SLOT]
        for i, part in enumerate(parts):
            col = part[:, h:h + 1]
            q = jnp.where(lane == d0 + i, col, q)
            k = jnp.where(lane == d0 + 3 + i, -col, k)
        q = jnp.where((lane >= d0 + 3) & (lane < d0 + 6), 1.0, q)
        k = jnp.where((lane >= d0) & (lane < d0 + 3), 1.0, k)
        fqk_ref[:, h * SLOT:(h + 1) * SLOT] = q.astype(BF16)
        fqk_ref[:, HEAD_W + h * SLOT:HEAD_W + (h + 1) * SLOT] = k.astype(BF16)

    vt = lax.dot_general(wvt_ref[...], u, _NT, preferred_element_type=F32).astype(BF16)
    spare = SLOT - DA_DV
    ones_then_zeros = jnp.where(lax.broadcasted_iota(jnp.int32, (spare, tm), 0) == 0,
                                1.0, 0.0).astype(BF16)
    for h in range(4):
        for ref, first in ((dvt_ref, 0), (fvt_ref, DA_WIDTH)):
            ref[0, h, 0, 0:DA_DV, :] = vt[first + h * DA_DV:first + (h + 1) * DA_DV]
            ref[0, h, 0, DA_DV:SLOT, :] = ones_then_zeros


def _in_proj(h, g, w_cat, w_vt, cos_t, sin_t, fox_bias_row, lb_row, batch, seq_len):
    n = h.shape[0]
    tm = PROJ_TILE
    assert tm == ATTN_BLOCK
    nk = seq_len // tm
    row = lambda w: pl.BlockSpec((tm, w), lambda i: (i, 0))
    tab = pl.BlockSpec((tm, 2 * SLOT), lambda i: (i % nk, 0))
    once = lambda r, c: pl.BlockSpec((r, c), lambda i: (0, 0), pipeline_mode=pl.Buffered(1))
    widths = (3 * HG_WIDTH, HG_WIDTH, HG_WIDTH, 2 * HEAD_W, 2 * HEAD_W)
    dtypes = (BF16, F32, F32, BF16, BF16)
    vt_spec = pl.BlockSpec((1, 4, 1, SLOT, tm), lambda i: (i // nk, 0, i % nk, 0, 0))
    vt_shape = jax.ShapeDtypeStruct((batch, 4, nk, SLOT, tm), BF16)
    return pl.pallas_call(
        functools.partial(_in_proj_kernel, tiles_per_seq=nk),
        grid=(n // tm,),
        in_specs=[row(D_MODEL),
                  pl.BlockSpec((1, D_MODEL), lambda i: (0, 0)),
                  once(D_MODEL, IN_COLS), once(DA_WIDTH + FX_WIDTH, D_MODEL),
                  tab, tab,
                  pl.BlockSpec((1, LANES), lambda i: (0, 0)),
                  pl.BlockSpec((1, HG_WIDTH), lambda i: (0, 0))],
        out_specs=[row(w) for w in widths] + [vt_spec, vt_spec],
        out_shape=[jax.ShapeDtypeStruct((n, w), t) for w, t in zip(widths, dtypes)]
        + [vt_shape, vt_shape],
        scratch_shapes=[pltpu.VMEM((1, LANES), F32)],
        compiler_params=pltpu.CompilerParams(dimension_semantics=("arbitrary",),
                                             vmem_limit_bytes=VMEM_LIMIT),
        name="in_proj",
    )(h, g, w_cat, w_vt, cos_t, sin_t, fox_bias_row, lb_row)


def _bf16_split3(x):
    hi = x.astype(BF16).astype(F32)
    r = x - hi
    mid = r.astype(BF16).astype(F32)
    return hi, mid, r - mid


def _softmax_block(s, m_prev):
    m_new = jnp.maximum(m_prev, jnp.max(s, axis=0, keepdims=True))
    alpha = jnp.exp2(m_prev - m_new)
    p = jnp.exp2(s - m_new).astype(BF16)
    return p, alpha, m_new


def _block_start(j, blk):
    return j * blk if isinstance(j, int) else pl.multiple_of(j * blk, blk)


def _causal_valid(i, j, blk, first_row=0):
    k_idx = j * blk + first_row + lax.broadcasted_iota(jnp.int32, (blk - first_row, blk), 0)
    q_idx = i * blk + lax.broadcasted_iota(jnp.int32, (blk - first_row, blk), 1)
    return (k_idx <= q_idx) & (k_idx >= LPAD)


def _pipelined_key_blocks(i, first, scores, consume):
    first()

    @pl.when(i >= 1)
    def _():
        scores(1, 0)

    def pair(t, carry):
        scores(2 * t + 2, 1)
        consume(2 * t + 1, 0, False)
        scores(2 * t + 3, 0)
        consume(2 * t + 2, 1, False)
        return carry

    lax.fori_loop(0, lax.shift_right_logical(jnp.maximum(i - 1, 0), 1), pair, 0)
    odd = (i & 1) == 1

    @pl.when(odd)
    def _():
        consume(i, 0, True)

    @pl.when((i >= 2) & jnp.logical_not(odd))
    def _():
        scores(i, 1)
        consume(i - 1, 0, False)
        consume(i, 1, True)


def _head_out(acc):
    return acc[0:ONE_LANE] / acc[ONE_LANE:ONE_LANE + 1]


def _head_rms(o, g_col):
    return o * lax.rsqrt(jnp.mean(o * o, axis=0, keepdims=True) + EPS) * g_col


def _store_head_pairs(o_ref, outs):
    for p in range(len(outs) // 2):
        pair = jnp.concatenate([outs[2 * p], outs[2 * p + 1]], axis=0)
        o_ref[:, p * LANES:(p + 1) * LANES] = pair.T.astype(BF16)


def _fox_attn_kernel(q_ref, k_ref, vt_ref, g_ref, o_ref, m_ref, acc_ref, sa_ref, sb_ref,
                     *, blk):
    i = pl.program_id(2)
    heads = range(ATTN_HEADS_PER_STEP)
    slot = [slice(hh * SLOT, (hh + 1) * SLOT) for hh in heads]
    q = [q_ref[:, slot[hh]] for hh in heads]
    bufs = (sa_ref, sb_ref)
    m_ref[...] = jnp.full(m_ref.shape, NEG, F32)
    acc_ref[...] = jnp.zeros_like(acc_ref)

    def scores(j, buf):
        rows = pl.ds(_block_start(j, blk), blk)
        for hh in heads:
            bufs[buf][hh] = lax.dot_general(k_ref[rows, slot[hh]], q[hh], _NT,
                                            preferred_element_type=F32)

    def consume(j, buf, masked):
        mask = _causal_valid(i, j, blk) if masked else None
        for hh in heads:
            s = bufs[buf][hh]
            if masked:
                s = jnp.where(mask, s, NEG)
            p, alpha, m_new = _softmax_block(s, m_ref[hh])
            acc_ref[hh] = alpha * acc_ref[hh] + jnp.dot(
                vt_ref[0, hh, j], p, preferred_element_type=F32)
            m_ref[hh] = m_new

    def first():
        lo = blk - FIRST_KEYS
        mask = _causal_valid(i, 0, blk, lo)
        for hh in heads:
            s = lax.dot_general(k_ref[lo:blk, slot[hh]], q[hh], _NT,
                                preferred_element_type=F32)
            p, alpha, m_new = _softmax_block(jnp.where(mask, s, NEG), m_ref[hh])
            acc_ref[hh] = alpha * acc_ref[hh] + jnp.dot(
                vt_ref[0, hh, 0, :, lo:blk], p, preferred_element_type=F32)
            m_ref[hh] = m_new

    _pipelined_key_blocks(i, first, scores, consume)
    _store_head_pairs(
        o_ref, [_head_rms(_head_out(acc_ref[hh]), g_ref[...]) for hh in heads])


def _key_spec(seq_len, groups):
    return pl.BlockSpec((seq_len, ATTN_HEADS_PER_STEP * SLOT),
                        lambda b, h, i: (b, groups + h), pipeline_mode=pl.Buffered(1))


def _value_spec(seq_len):
    nk = seq_len // ATTN_BLOCK
    return pl.BlockSpec((1, ATTN_HEADS_PER_STEP, nk, SLOT, ATTN_BLOCK),
                        lambda b, h, i: (b, h, 0, 0, 0), pipeline_mode=pl.Buffered(1))


def _fox_attn(fqk, fvt, gain, batch, seq_len):
    n = fqk.shape[0]
    blk = ATTN_BLOCK
    nq = seq_len // blk
    hps = ATTN_HEADS_PER_STEP
    groups = FX_HEADS // hps
    return pl.pallas_call(
        functools.partial(_fox_attn_kernel, blk=blk),
        grid=(batch, groups, nq),
        in_specs=[pl.BlockSpec((blk, hps * SLOT), lambda b, h, i: (b * nq + i, h)),
                  _key_spec(seq_len, groups),
                  _value_spec(seq_len),
                  pl.BlockSpec((ONE_LANE, 1), lambda b, h, i: (0, 0))],
        out_specs=pl.BlockSpec((blk, hps * ONE_LANE), lambda b, h, i: (b * nq + i, h)),
        out_shape=jax.ShapeDtypeStruct((n, 4 * ONE_LANE), BF16),
        scratch_shapes=[pltpu.VMEM((hps, 1, blk), F32), pltpu.VMEM((hps, SLOT, blk), F32),
                        pltpu.VMEM((hps, blk, blk), F32), pltpu.VMEM((hps, blk, blk), F32)],
        compiler_params=pltpu.CompilerParams(
            dimension_semantics=("arbitrary", "arbitrary", "arbitrary"),
            vmem_limit_bytes=VMEM_LIMIT),
        name="fox_attn",
    )(fqk, fqk, fvt, gain)


def _diff_attn_kernel(q_ref, k_ref, vt_ref, lam_ref, g_ref, o_ref, m_ref, acc_ref,
                      sa_ref, sb_ref, *, blk, lam_init):
    i = pl.program_id(2)
    heads = range(ATTN_HEADS_PER_STEP)
    slot = lambda s: slice(s * SLOT, (s + 1) * SLOT)
    lane = lax.broadcasted_iota(jnp.int32, (blk, SLOT), 1)
    q = []
    for hh in heads:
        q_h = q_ref[:, slot(hh)]
        q.append([jnp.where((lane >= c * DA_DQK) & (lane < (c + 1) * DA_DQK), q_h,
                            jnp.zeros_like(q_h)) for c in range(2)])
    bufs = (sa_ref, sb_ref)
    m_ref[...] = jnp.full(m_ref.shape, NEG, F32)
    acc_ref[...] = jnp.zeros_like(acc_ref)

    def scores(j, buf):
        rows = pl.ds(_block_start(j, blk), blk)
        for hh in heads:
            k_h = k_ref[rows, slot(hh)]
            for c in range(2):
                bufs[buf][2 * hh + c] = lax.dot_general(k_h, q[hh][c], _NT,
                                                        preferred_element_type=F32)

    def consume(j, buf, masked):
        mask = _causal_valid(i, j, blk) if masked else None
        for hh in heads:
            ps, alphas = [], []
            for c in range(2):
                hc = 2 * hh + c
                s = bufs[buf][hc]
                if masked:
                    s = jnp.where(mask, s, NEG)
                p, alpha, m_new = _softmax_block(s, m_ref[hc])
                m_ref[hc] = m_new
                ps.append(p)
                alphas.append(alpha)
            pv = jnp.dot(vt_ref[0, hh, j], jnp.concatenate(ps, axis=1),
                         preferred_element_type=F32)
            for c in range(2):
                hc = 2 * hh + c
                acc_ref[hc] = alphas[c] * acc_ref[hc] + pv[:, c * blk:(c + 1) * blk]

    def first():
        lo = blk - FIRST_KEYS
        mask = _causal_valid(i, 0, blk, lo)
        for hh in heads:
            k_h = k_ref[lo:blk, slot(hh)]
            ps, alphas = [], []
            for c in range(2):
                hc = 2 * hh + c
                s = lax.dot_general(k_h, q[hh][c], _NT, preferred_element_type=F32)
                p, alpha, m_new = _softmax_block(jnp.where(mask, s, NEG), m_ref[hc])
                m_ref[hc] = m_new
                ps.append(p)
                alphas.append(alpha)
            pv = jnp.dot(vt_ref[0, hh, 0, :, lo:blk], jnp.concatenate(ps, axis=1),
                         preferred_element_type=F32)
            for c in range(2):
                hc = 2 * hh + c
                acc_ref[hc] = alphas[c] * acc_ref[hc] + pv[:, c * blk:(c + 1) * blk]

    _pipelined_key_blocks(i, first, scores, consume)
    lv = lam_ref[...]
    lam = (jnp.exp(jnp.sum(lv[0:1] * lv[1:2], axis=-1, keepdims=True))
           - jnp.exp(jnp.sum(lv[2:3] * lv[3:4], axis=-1, keepdims=True)) + lam_init)
    outs = []
    for hh in heads:
        o = _head_out(acc_ref[2 * hh]) - lam * _head_out(acc_ref[2 * hh + 1])
        outs.append(_head_rms(o, g_ref[...]) * (1.0 - lam_init))
    _store_head_pairs(o_ref, outs)


def _diff_attn(dqk, dvt, lam_vecs, gain, lam_init, batch, seq_len):
    n = dqk.shape[0]
    blk = ATTN_BLOCK
    nq = seq_len // blk
    hps = ATTN_HEADS_PER_STEP
    groups = DA_HEADS // hps
    return pl.pallas_call(
        functools.partial(_diff_attn_kernel, blk=blk, lam_init=lam_init),
        grid=(batch, groups, nq),
        in_specs=[pl.BlockSpec((blk, hps * SLOT), lambda b, h, i: (b * nq + i, h)),
                  _key_spec(seq_len, groups),
                  _value_spec(seq_len),
                  pl.BlockSpec((8, LANES), lambda b, h, i: (0, 0)),
                  pl.BlockSpec((ONE_LANE, 1), lambda b, h, i: (0, 0))],
        out_specs=pl.BlockSpec((blk, hps * ONE_LANE), lambda b, h, i: (b * nq + i, h)),
        out_shape=jax.ShapeDtypeStruct((n, 4 * ONE_LANE), BF16),
        scratch_shapes=[pltpu.VMEM((2 * hps, 1, blk), F32),
                        pltpu.VMEM((2 * hps, SLOT, blk), F32),
                        pltpu.VMEM((2 * hps, blk, blk), F32),
                        pltpu.VMEM((2 * hps, blk, blk), F32)],
        compiler_params=pltpu.CompilerParams(
            dimension_semantics=("arbitrary", "arbitrary", "arbitrary"),
            vmem_limit_bytes=VMEM_LIMIT),
        name="diff_attn",
    )(dqk, dqk, dvt, lam_vecs, gain)


def _hgrn_kernel(q_ref, i_ref, g_ref, lf_ref, kk_ref, gain_ref, o_ref,
                 st_ref, hs_ref, *, tl):
    @pl.when(pl.program_id(2) == 0)
    def _():
        st_ref[...] = jnp.zeros_like(st_ref)

    c_rows, sub = HG_CHUNK, HG_SUB
    tri = (lax.broadcasted_iota(jnp.int32, (c_rows, c_rows), 1)
           <= lax.broadcasted_iota(jnp.int32, (c_rows, c_rows), 0)).astype(BF16)
    t_sub = lax.broadcasted_iota(jnp.int32, (sub, 1), 0)
    lane_c = lax.broadcasted_iota(jnp.int32, (sub, c_rows), 1)

    t_all = lax.broadcasted_iota(jnp.int32, (c_rows, c_rows), 0)
    s_all = lax.broadcasted_iota(jnp.int32, (c_rows, c_rows), 1)
    level_masks = []
    g = sub
    while g < c_rows:
        level_masks.append(((t_all // g) == (s_all // g) + 1) & ((s_all // g) % 2 == 0))
        g *= 2

    heads = range(HG_PER_STEP)
    cols = [slice(hh * HG_DK, (hh + 1) * HG_DK) for hh in heads]

    def chunk(c, carry):
        r0 = pl.multiple_of(c * c_rows, c_rows)
        rows = pl.ds(r0, c_rows)
        G, kk, qs, v, st, o_inter = [], [], [], [], [], []
        for hh in heads:
            kk.append(kk_ref[rows, cols[hh]])
            G.append(sum(jnp.dot(tri, t.astype(BF16), preferred_element_type=F32)
                         for t in _bf16_split3(lf_ref[rows, cols[hh]])))
            qs.append(q_ref[rows, cols[hh]].astype(F32))
            v.append(i_ref[rows, cols[hh]])
            hs_ref[hh] = G[hh] - jnp.log2(kk[hh])
            st.append(st_ref[hh])
            o_inter.append(lax.dot_general(
                (qs[hh] * jnp.exp2(G[hh])).astype(BF16), st[hh].astype(BF16), _NT,
                preferred_element_type=F32))
        a_mat = [jnp.zeros((c_rows, c_rows), F32) for _ in heads]
        g = sub
        for mask in level_masks:
            for hh in heads:
                ref = jnp.concatenate(
                    [jnp.broadcast_to(G[hh][p + g - 1:p + g], (2 * g, HG_DK))
                     for p in range(0, c_rows, 2 * g)], axis=0)
                e = jnp.exp2(-jnp.abs(G[hh] - ref))
                part = lax.dot_general((qs[hh] * e).astype(BF16), (kk[hh] * e).astype(BF16),
                                       _NT, preferred_element_type=F32)
                a_mat[hh] = jnp.where(mask, part, a_mat[hh])
            g *= 2
        a_rows = [[] for _ in heads]
        for b in range(c_rows // sub):
            lo = b * sub
            for hh in heads:
                q_b = qs[hh][lo:lo + sub]
                g_b = G[hh][lo:lo + sub]
                a_blk = a_mat[hh][lo:lo + sub]
                for s in range(lo, lo + sub):
                    y = q_b * jnp.exp2(g_b - hs_ref[hh, s:s + 1, :])
                    a_blk = jnp.where(lane_c == s, jnp.sum(y, axis=-1, keepdims=True), a_blk)
                a_rows[hh].append(jnp.where(lane_c <= t_sub + lo, a_blk, 0.0))
        for hh in heads:
            a_full = jnp.concatenate(a_rows[hh], axis=0).astype(BF16)
            o = o_inter[hh] + jnp.dot(a_full, v[hh], preferred_element_type=F32)
            g_last = G[hh][c_rows - 1:c_rows]
            kd = kk[hh] * jnp.exp2(g_last - G[hh])
            st_ref[hh] = st[hh] * jnp.exp2(g_last) + lax.dot_general(
                v[hh], kd.astype(BF16), _TN, preferred_element_type=F32)
            o = _rms(o, gain_ref[...]) * g_ref[rows, cols[hh]].astype(F32)
            o_ref[rows, cols[hh]] = o.astype(BF16)
        return carry

    lax.fori_loop(0, tl // c_rows, chunk, 0, unroll=2)


def _hgrn(hqig, hlf, hkk, gain, batch, seq_len):
    n = hqig.shape[0]
    tl = ROW_TILE
    nt = seq_len // tl
    w = HG_PER_STEP * HG_DK
    groups = HG_HEADS // HG_PER_STEP
    col = lambda off: pl.BlockSpec((tl, w), lambda b, h, j: (b * nt + j, off + h))
    return pl.pallas_call(
        functools.partial(_hgrn_kernel, tl=tl),
        grid=(batch, groups, nt),
        in_specs=[col(0), col(groups), col(2 * groups), col(0), col(0),
                  pl.BlockSpec((1, HG_DK), lambda b, h, j: (0, 0))],
        out_specs=col(0),
        out_shape=jax.ShapeDtypeStruct((n, HG_WIDTH), BF16),
        scratch_shapes=[pltpu.VMEM((HG_PER_STEP, HG_DK, HG_DK), F32),
                        pltpu.VMEM((HG_PER_STEP, HG_CHUNK, HG_DK), F32)],
        compiler_params=pltpu.CompilerParams(
            dimension_semantics=("arbitrary", "arbitrary", "arbitrary")),
        name="hgrn2",
    )(hqig, hqig, hqig, hlf, hkk, gain)


def _out_router_kernel(oa_ref, ob_ref, oc_ref, h_ref, wo_ref, g_ref, wrh_ref, wrl_ref, br_ref,
                       h1_ref, route_ref, cnt_ref, carry_ref):
    @pl.when(pl.program_id(0) == 0)
    def _():
        carry_ref[...] = jnp.zeros_like(carry_ref)

    h1 = (h_ref[...]
          + jnp.dot(oa_ref[...], wo_ref[0:512, :], preferred_element_type=F32)
          + jnp.dot(ob_ref[...], wo_ref[512:768, :], preferred_element_type=F32)
          + jnp.dot(oc_ref[...], wo_ref[768:1024, :], preferred_element_type=F32))
    h1_ref[...] = h1
    u = _rms(h1, g_ref[...])
    u_hi = u.astype(BF16)
    u_lo = (u - u_hi.astype(F32)).astype(BF16)
    logits = (jnp.dot(u_hi, wrh_ref[...], preferred_element_type=F32)
              + (jnp.dot(u_hi, wrl_ref[...], preferred_element_type=F32)
                 + jnp.dot(u_lo, wrh_ref[...], preferred_element_type=F32))
              + br_ref[...])
    tm = logits.shape[0]
    lane = lax.broadcasted_iota(jnp.int32, (tm, LANES), 1).astype(F32)
    big = float(LANES)
    is_g = (lane >= N_EXPERTS) & (lane < N_EXPERTS + N_GROUPS)
    gl = jnp.where(is_g, logits, -jnp.inf)
    gmax = jnp.max(gl, axis=-1, keepdims=True)
    gsel = jnp.min(jnp.where(gl == gmax, lane, big), axis=-1, keepdims=True) - N_EXPERTS
    p_g = 1.0 / jnp.sum(jnp.exp(gl - gmax), axis=-1, keepdims=True)
    lo = gsel * EXPERTS_PER_GROUP
    el = jnp.where((lane >= lo) & (lane < lo + EXPERTS_PER_GROUP), logits, -jnp.inf)
    m1 = jnp.max(el, axis=-1, keepdims=True)
    i1 = jnp.min(jnp.where(el == m1, lane, big), axis=-1, keepdims=True)
    el2 = jnp.where(lane == i1, -jnp.inf, el)
    m2 = jnp.max(el2, axis=-1, keepdims=True)
    i2 = jnp.min(jnp.where(el2 == m2, lane, big), axis=-1, keepdims=True)
    r = jnp.exp(m2 - m1)
    gate1 = p_g / (1.0 + r)
    gate2 = gate1 * r
    oh1 = lane == i1
    oh2 = lane == i2
    onehot = jnp.where(oh1 | oh2, 1.0, 0.0)
    tri = (lax.broadcasted_iota(jnp.int32, (tm, tm), 1)
           < lax.broadcasted_iota(jnp.int32, (tm, tm), 0)).astype(BF16)
    before = jnp.dot(tri, onehot.astype(BF16), preferred_element_type=F32) + carry_ref[...]
    rank1 = jnp.sum(jnp.where(oh1, before, 0.0), axis=-1, keepdims=True)
    rank2 = jnp.sum(jnp.where(oh2, before, 0.0), axis=-1, keepdims=True)
    total = carry_ref[...] + jnp.sum(onehot, axis=0, keepdims=True)
    carry_ref[...] = total
    cnt_ref[...] = total
    route = jnp.where(lane == 0, i1, 0.0)
    for idx, val in ((1, i2), (2, gate1), (3, gate2), (4, rank1), (5, rank2)):
        route = jnp.where(lane == idx, val, route)
    route_ref[...] = route


def _out_router(oa, ob, oc, h, w_out, g, w_rt, b_rt):
    n = h.shape[0]
    w_rt_hi = w_rt.astype(BF16)
    w_rt_lo = (w_rt - w_rt_hi.astype(F32)).astype(BF16)
    tm = ROUTER_TILE
    row = lambda w: pl.BlockSpec((tm, w), lambda i: (i, 0))
    const = lambda r, c: pl.BlockSpec((r, c), lambda i: (0, 0))
    return pl.pallas_call(
        _out_router_kernel,
        grid=(n // tm,),
        in_specs=[row(HG_WIDTH), row(DA_WIDTH), row(FX_WIDTH), row(D_MODEL),
                  const(D_MODEL, D_MODEL),
                  const(1, D_MODEL), const(D_MODEL, LANES), const(D_MODEL, LANES),
                  const(1, LANES)],
        out_specs=[row(D_MODEL), row(LANES), const(1, LANES)],
        out_shape=[jax.ShapeDtypeStruct((n, D_MODEL), F32),
                   jax.ShapeDtypeStruct((n, LANES), F32),
                   jax.ShapeDtypeStruct((1, LANES), F32)],
        scratch_shapes=[pltpu.VMEM((1, LANES), F32)],
        compiler_params=pltpu.CompilerParams(dimension_semantics=("arbitrary",)),
        name="out_router",
    )(oa, ob, oc, h, w_out, g, w_rt_hi, w_rt_lo, b_rt)


def _dispatch_kernel(dest_ref, unused_ref, h_ref, g_ref, xs_in_ref, xs_ref, back_ref,
                     u_ref, sem, *, tm, n, tb):
    del xs_in_ref
    i = pl.program_id(0)
    n_tiles = n // tm
    base = (i - 1) * tm
    cur = i % 2

    @pl.when(i == 0)
    def _():
        def fill(s, carry):
            parity = lax.shift_right_logical(s, tb.bit_length() - 1) & 1
            back_ref[s] = 2 * n + parity * tb + (s & (tb - 1))
            return carry

        n_ranges = unused_ref.shape[0] // 2

        def one_range(e, carry):
            return lax.fori_loop(unused_ref[e], unused_ref[n_ranges + e], fill, carry)

        lax.fori_loop(0, n_ranges, one_range, 0)

    def issue(r, carry):
        for k in range(2):
            src = k * n + base + r
            d = dest_ref[src]
            back_ref[d] = src
            pltpu.make_async_copy(u_ref.at[1 - cur, pl.ds(r, 1)], xs_ref.at[pl.ds(d, 1)],
                                  sem).start()
        return carry

    @pl.when(i >= 1)
    def _():
        lax.fori_loop(0, tm, issue, 0, unroll=8)

    @pl.when(i < n_tiles)
    def _():
        u_ref[cur] = _rms(h_ref[...], g_ref[...])

    @pl.when(i >= 1)
    def _():
        for _ in range(2):
            pltpu.make_async_copy(u_ref.at[1 - cur], xs_ref.at[pl.ds(0, tm)], sem).wait()


def _dispatch(dest, unused, h1, g, xs_init):
    n = h1.shape[0]
    tm = ROW_TILE
    p_rows = xs_init.shape[0]
    return pl.pallas_call(
        functools.partial(_dispatch_kernel, tm=tm, n=n, tb=MOE_BLOCK),
        grid_spec=pltpu.PrefetchScalarGridSpec(
            num_scalar_prefetch=2,
            grid=(n // tm + 1,),
            in_specs=[pl.BlockSpec((tm, D_MODEL),
                                   lambda i, d, un: (jnp.minimum(i, n // tm - 1), 0)),
                      pl.BlockSpec((1, D_MODEL), lambda i, d, un: (0, 0)),
                      pl.BlockSpec(memory_space=pl.ANY)],
            out_specs=[pl.BlockSpec(memory_space=pl.ANY),
                       pl.BlockSpec(memory_space=pltpu.SMEM)],
            scratch_shapes=[pltpu.VMEM((2, tm, D_MODEL), F32),
                            pltpu.SemaphoreType.DMA(())]),
        out_shape=[jax.ShapeDtypeStruct(xs_init.shape, xs_init.dtype),
                   jax.ShapeDtypeStruct((p_rows,), jnp.int32)],
        input_output_aliases={4: 0},
        compiler_params=pltpu.CompilerParams(dimension_semantics=("arbitrary",)),
        name="moe_dispatch",
    )(dest, unused, h1, g, xs_init)


def _expert_kernel(be_ref, na_ref, back_ref, x_ref, w1_ref, w3_ref, w2_ref, out_ref,
                   y_ref, w1b_ref, w3b_ref, w2b_ref, sem, *, tb):
    j = pl.program_id(0)
    na = na_ref[0]
    cur = j % 2

    def scatter(block, buf):
        for r in range(tb):
            pltpu.make_async_copy(y_ref.at[buf, pl.ds(r, 1)],
                                  out_ref.at[pl.ds(back_ref[block * tb + r], 1)],
                                  sem.at[buf]).start()

    def compute(buf):
        x = x_ref[...].astype(BF16)
        a = jnp.dot(x, w1b_ref[...], preferred_element_type=F32)
        b = jnp.dot(x, w3b_ref[...], preferred_element_type=F32)
        act = (a * _sigmoid(a) * b).astype(BF16)
        y_ref[buf] = jnp.dot(act, w2b_ref[...], preferred_element_type=F32)

    new_expert = (j == 0) | (be_ref[j] != be_ref[jnp.maximum(j - 1, 0)])

    @pl.when((j < na) & new_expert)
    def _():
        w1b_ref[...] = w1_ref[0].astype(BF16)
        w3b_ref[...] = w3_ref[0].astype(BF16)
        w2b_ref[...] = w2_ref[0].astype(BF16)

    @pl.when((j >= 2) & (j < na + 2))
    def _():
        pltpu.make_async_copy(y_ref.at[cur], out_ref.at[pl.ds(0, tb)], sem.at[cur]).wait()

    @pl.when(j == 0)
    def _():
        y_ref[...] = jnp.zeros_like(y_ref)
        first_spare = out_ref.shape[0] - 2 * tb
        spare = [pltpu.make_async_copy(
            y_ref.at[b], out_ref.at[pl.ds(first_spare + b * tb, tb)], sem.at[b])
            for b in range(2)]
        for copy in spare:
            copy.start()
        for copy in spare:
            copy.wait()
        compute(cur)

    @pl.when((j >= 1) & (j < na))
    def _():
        scatter(j - 1, 1 - cur)
        compute(cur)

    @pl.when(j == na)
    def _():
        scatter(j - 1, 1 - cur)


def _experts(blk_expert, n_active, slot_back, xs, w1, w3, w2, layer, n):
    p_rows = xs.shape[0]
    tb = MOE_BLOCK
    last = lambda j, na: jnp.maximum(jnp.minimum(j, na[0] - 1), 0)
    w_spec = lambda r, c: pl.BlockSpec((None, 1, r, c),
                                       lambda j, be, na, back: (layer, be[j], 0, 0))
    return pl.pallas_call(
        functools.partial(_expert_kernel, tb=tb),
        grid_spec=pltpu.PrefetchScalarGridSpec(
            num_scalar_prefetch=3,
            grid=(p_rows // tb + 1,),
            in_specs=[pl.BlockSpec((tb, D_MODEL), lambda j, be, na, back: (last(j, na), 0)),
                      w_spec(D_MODEL, D_EXPERT), w_spec(D_MODEL, D_EXPERT),
                      w_spec(D_EXPERT, D_MODEL)],
            out_specs=pl.BlockSpec(memory_space=pl.ANY),
            scratch_shapes=[pltpu.VMEM((2, tb, D_MODEL), F32),
                            pltpu.VMEM((D_MODEL, D_EXPERT), BF16),
                            pltpu.VMEM((D_MODEL, D_EXPERT), BF16),
                            pltpu.VMEM((D_EXPERT, D_MODEL), BF16),
                            pltpu.SemaphoreType.DMA((2,))]),
        out_shape=jax.ShapeDtypeStruct((2 * n + 2 * tb, D_MODEL), F32),
        compiler_params=pltpu.CompilerParams(dimension_semantics=("arbitrary",),
                                             vmem_limit_bytes=VMEM_LIMIT),
        name="moe_experts",
    )(blk_expert, n_active, slot_back, xs, w1, w3, w2)


def _combine_kernel(h_ref, route_ref, gfin_ref, y0_ref, y1_ref, o_ref, *, final):
    route = route_ref[...]
    out = h_ref[...] + route[:, 2:3] * y0_ref[...] + route[:, 3:4] * y1_ref[...]
    if final:
        out = _rms(out, gfin_ref[...])
    o_ref[...] = out


def _combine(h1, route, g_final, ys, batch, seq_len, final):
    n = h1.shape[0]
    tm = ROW_TILE
    tiles_per_seq = seq_len // tm
    skip_tiles = (LPAD + N_META) // tm if final else 0
    out_tiles = tiles_per_seq - skip_tiles
    tile = lambda b, i: b * tiles_per_seq + skip_tiles + i
    in_row = lambda w: pl.BlockSpec((tm, w), lambda b, i: (tile(b, i), 0))
    y_rows = lambda k: pl.BlockSpec((tm, D_MODEL),
                                    lambda b, i: (k * (n // tm) + tile(b, i), 0))
    return pl.pallas_call(
        functools.partial(_combine_kernel, final=final),
        grid=(batch, out_tiles),
        in_specs=[in_row(D_MODEL), in_row(LANES),
                  pl.BlockSpec((1, D_MODEL), lambda b, i: (0, 0)),
                  y_rows(0), y_rows(1)],
        out_specs=pl.BlockSpec((tm, D_MODEL), lambda b, i: (b * out_tiles + i, 0)),
        out_shape=jax.ShapeDtypeStruct((batch * out_tiles * tm, D_MODEL), F32),
        compiler_params=pltpu.CompilerParams(dimension_semantics=("arbitrary", "arbitrary")),
        name="moe_combine",
    )(h1, route, g_final, ys, ys)


def _rope_partner(m):
    lead = m.shape[:-1]
    x = m.reshape(lead + (DA_WIDTH // DA_DQK, 2, DA_DQK // 2))
    return x[..., ::-1, :].reshape(lead + (DA_WIDTH,))


def _rope_tables(seq_len):
    pos = (jnp.arange(seq_len) - LPAD).astype(F32)
    inv = ROPE_THETA ** (-jnp.arange(0, DA_DQK, 2, dtype=F32) / DA_DQK)
    ang = pos[:, None] * inv[None, :]
    comps = DA_DV // DA_DQK
    cos = jnp.tile(jnp.cos(ang), (1, 2 * comps))
    sin = jnp.tile(jnp.concatenate([-jnp.sin(ang), jnp.sin(ang)], axis=1), (1, comps))
    q_scale = DA_DQK ** -0.5 * LOG2E
    two = lambda t: jnp.tile(t, (1, LANES // DA_DV))
    lay = lambda t: jnp.concatenate([two(t * q_scale), two(t)], axis=1)
    return lay(cos), lay(sin)


def _in_weights(w):
    hq, hf, hi, hg = (w[:, i * 512:(i + 1) * 512] for i in range(4))
    dq, dk, dv = (w[:, 2048 + i * 256:2048 + (i + 1) * 256] for i in range(3))
    fq, fk, fv = (w[:, 2816 + i * 256:2816 + (i + 1) * 256] for i in range(3))
    ff = w[:, 3584:3588]
    cat = jnp.concatenate(
        [hq, hi, hg, hf,
         dq, dk, _rope_partner(dq), _rope_partner(dk),
         fq * (FX_DH ** -0.5), fk,
         ff, jnp.zeros((D_MODEL, LANES - FX_HEADS), w.dtype)],
        axis=1)
    w_vt = jnp.concatenate([dv, fv], axis=1).T
    return cat.astype(BF16), w_vt.astype(BF16)


def _pad_lanes(v, width=LANES):
    return jnp.zeros((1, width), F32).at[0, :v.shape[0]].set(v.astype(F32))


def kernel(x, meta_tokens, norm_mix, w_in, hgrn_lb, hgrn_norm, diff_lambda, diff_norm,
           fox_bias, fox_norm, w_out, norm_ffn, w_group, b_group, w_router, b_router,
           w1, w3, w2, norm_final):
    batch, seq, d = x.shape
    depth = w_in.shape[0]
    seq_len = LPAD + N_META + seq
    n = batch * seq_len
    pad = jnp.zeros((batch, LPAD, d), x.dtype)
    meta = jnp.broadcast_to(meta_tokens.astype(x.dtype)[None], (batch, N_META, d))
    h = jnp.concatenate([pad, meta, x], axis=1).reshape(n, d)

    s_lb = jax.nn.softmax(hgrn_lb.astype(F32), axis=0)
    lb_all = jnp.cumsum(s_lb, axis=0) - s_lb[0]
    cos_t, sin_t = _rope_tables(seq_len)

    tb = MOE_BLOCK
    n_blocks = (2 * n) // tb + N_EXPERTS
    p_rows = n_blocks * tb

    xs = jnp.zeros((p_rows, d), F32)
    for layer in range(depth):
        lam_init = 0.8 - 0.6 * math.exp(-0.3 * layer)
        w_cat, w_vt = _in_weights(w_in[layer])
        hqig, hlf, hkk, dqk, fqk, dvt, fvt = _in_proj(
            h, norm_mix[layer][None, :], w_cat, w_vt, cos_t, sin_t,
            _pad_lanes(fox_bias[layer]), lb_all[layer][None, :], batch, seq_len)

        o_a = _hgrn(hqig, hlf, hkk, hgrn_norm[layer][None, :], batch, seq_len)

        lam_vecs = jnp.zeros((8, LANES), F32).at[:4, :DA_DQK].set(diff_lambda[layer].astype(F32))
        o_b = _diff_attn(dqk, dvt, lam_vecs, diff_norm[layer][:, None], lam_init,
                         batch, seq_len)

        o_c = _fox_attn(fqk, fvt, fox_norm[layer][:, None], batch, seq_len)

        w_rt = jnp.concatenate(
            [w_router[layer], w_group[layer],
             jnp.zeros((d, LANES - N_EXPERTS - N_GROUPS), F32)], axis=1)
        b_rt = _pad_lanes(jnp.concatenate([b_router[layer], b_group[layer]]))
        h1, route, counts = _out_router(
            o_a, o_b, o_c, h, w_out[layer].astype(BF16), norm_ffn[layer][None, :], w_rt, b_rt)

        cnt = counts[0, :N_EXPERTS].astype(jnp.int32)
        padded = (cnt + tb - 1) // tb * tb
        p_end = jnp.cumsum(padded)
        p_start = p_end - padded
        ids = route[:, 0:2].astype(jnp.int32)
        ranks = route[:, 4:6].astype(jnp.int32)
        dest = (p_start[ids] + ranks).T.reshape(2 * n)
        blk_start = jnp.arange(n_blocks + 1, dtype=jnp.int32) * tb
        blk_expert = jnp.minimum(
            jnp.sum((p_end[None, :] <= blk_start[:, None]).astype(jnp.int32), axis=1),
            N_EXPERTS - 1)
        n_active = (p_end[-1:] // tb).astype(jnp.int32)

        unused = jnp.concatenate(
            [p_start + cnt, p_end[-1:], p_end, jnp.full((1,), p_rows)]).astype(jnp.int32)
        xs, slot_back = _dispatch(dest, unused, h1, norm_ffn[layer][None, :], xs)
        ys = _experts(blk_expert, n_active, slot_back, xs, w1, w3, w2, layer, n)
        h = _combine(h1, route, norm_final[None, :], ys, batch, seq_len,
                     final=(layer == depth - 1))

    return h.reshape(batch, seq, d)
```

```python
import functools
import math

import jax
import jax.numpy as jnp
from jax import lax
from jax.experimental import pallas as pl
from jax.experimental.pallas import tpu as pltpu

F32 = jnp.float32
BF16 = jnp.bfloat16

D_MODEL = 1024
N_META = 16
HG_HEADS = 4
HG_DK = 128
HG_WIDTH = 512
HG_CHUNK = 64
HG_SUB = 8
HG_PER_STEP = 4
DA_HEADS = 4
DA_DV = 64
DA_DQK = 32
DA_WIDTH = 256
FX_HEADS = 4
FX_DH = 64
FX_WIDTH = 256
N_GROUPS = 4
EXPERTS_PER_GROUP = 8
N_EXPERTS = 32
D_EXPERT = 512
ROPE_THETA = 10000.0
EPS = 1e-6
NEG = -1e30
TINY = 1e-30
LOG2E = 1.4426950408889634

LANES = 128
SLOT = LANES
HEAD_W = 4 * SLOT
ROW_TILE = 512
PROJ_TILE = 512
ROUTER_TILE = 512
CUMSUM_ROWS = 128
ATTN_BLOCK = 512
ATTN_HEADS_PER_STEP = 4
MOE_BLOCK = 256
LPAD = ATTN_BLOCK - N_META
FIRST_KEYS = LANES
assert ATTN_BLOCK - FIRST_KEYS <= LPAD
ONE_LANE = 64
VMEM_LIMIT = 56 * 1024 * 1024

_C_HQIG = (0, 1536)
_C_HF = (1536, 2048)
_C_DQK = (2048, 2560)
_C_DQKP = (2560, 3072)
_C_FQK = (3072, 3584)
_C_FF = (3584, 3712)
IN_COLS = 3712

_NT = (((1,), (1,)), ((), ()))
_TN = (((0,), (0,)), ((), ()))


def _sigmoid(x):
    return 1.0 / (1.0 + jnp.exp(-x))


def _rms(x, g):
    return x * lax.rsqrt(jnp.mean(x * x, axis=-1, keepdims=True) + EPS) * g


def _in_proj_kernel(h_ref, g_ref, w_ref, wvt_ref, cos_ref, sin_ref, fb_ref, lb_ref,
                    hqig_ref, hlf_ref, hkk_ref, dqk_ref, fqk_ref, dvt_ref, fvt_ref,
                    carry_ref, *, tiles_per_seq):
    u = _rms(h_ref[...], g_ref[...]).astype(BF16)
    tm = u.shape[0]

    def mm(c):
        return jnp.dot(u, w_ref[:, c[0]:c[1]], preferred_element_type=F32)

    hg = mm(_C_HQIG)
    q_raw, g_raw = hg[:, 0:HG_WIDTH], hg[:, 2 * HG_WIDTH:3 * HG_WIDTH]
    hqig_ref[:, 0:HG_WIDTH] = (q_raw * _sigmoid(q_raw) * (HG_DK ** -0.5)).astype(BF16)
    hqig_ref[:, HG_WIDTH:2 * HG_WIDTH] = hg[:, HG_WIDTH:2 * HG_WIDTH].astype(BF16)
    hqig_ref[:, 2 * HG_WIDTH:3 * HG_WIDTH] = (g_raw * _sigmoid(g_raw)).astype(BF16)
    lb = lb_ref[...]
    sig = _sigmoid(mm(_C_HF))
    row = ((pl.program_id(0) % tiles_per_seq) * tm
           + lax.broadcasted_iota(jnp.int32, (tm, 1), 0))
    valid = row >= LPAD
    hlf_ref[...] = jnp.where(
        valid, jnp.log2(jnp.maximum(lb + (1.0 - lb) * sig, TINY)), 0.0)
    hkk_ref[...] = jnp.where(valid, (1.0 - lb) * (1.0 - sig), 0.0)
    half_lane = lax.broadcasted_iota(jnp.int32, (tm, LANES), 1) < DA_DV

    def spread(x):
        tiles = []
        for t in range(x.shape[1] // LANES):
            tile = x[:, t * LANES:(t + 1) * LANES]
            tiles.append(jnp.where(half_lane, tile, 0.0))
            tiles.append(jnp.where(half_lane, pltpu.roll(tile, LANES // 2, axis=1), 0.0))
        return jnp.concatenate(tiles, axis=1)

    def per_tile(t_ref):
        return jnp.concatenate([t_ref[:, 0:LANES]] * 2 + [t_ref[:, LANES:2 * LANES]] * 2, axis=1)

    rot = mm(_C_DQK) * per_tile(cos_ref) + mm(_C_DQKP) * per_tile(sin_ref)
    dqk_ref[:, 0:HEAD_W] = spread(rot[:, 0:DA_WIDTH]).astype(BF16)
    dqk_ref[:, HEAD_W:2 * HEAD_W] = spread(rot[:, DA_WIDTH:2 * DA_WIDTH]).astype(BF16)

    @pl.when(pl.program_id(0) % tiles_per_seq == 0)
    def _():
        carry_ref[...] = jnp.zeros_like(carry_ref)

    x = mm(_C_FF) + fb_ref[...]
    lf = jnp.minimum(x, 0.0) - jnp.log(1.0 + jnp.exp(-jnp.abs(x)))
    tri = (lax.broadcasted_iota(jnp.int32, (CUMSUM_ROWS, CUMSUM_ROWS), 1)
           <= lax.broadcasted_iota(jnp.int32, (CUMSUM_ROWS, CUMSUM_ROWS), 0)).astype(BF16)
    lf_terms = [t.astype(BF16) for t in _bf16_split3(lf)]
    total = carry_ref[...]
    groups = []
    for r0 in range(0, tm, CUMSUM_ROWS):
        part = sum(jnp.dot(tri, t[r0:r0 + CUMSUM_ROWS], preferred_element_type=F32)
                   for t in lf_terms) + total
        total = part[CUMSUM_ROWS - 1:CUMSUM_ROWS, :]
        groups.append(part)
    cs = jnp.concatenate(groups, axis=0)
    carry_ref[...] = total
    parts = _bf16_split3(cs * LOG2E)
    fqk = mm(_C_FQK)
    fq = spread(fqk[:, 0:FX_WIDTH]) * LOG2E
    fk = spread(fqk[:, FX_WIDTH:2 * FX_WIDTH])
    lane = lax.broadcasted_iota(jnp.int32, (tm, SLOT), 1)
    d0 = FX_DH
    for h in range(FX_HEADS):
        q = fq[:, h * SLOT:(h + 1) * SLOT]
        k = fk[:, h * SLOT:(h + 1) * SLOT]
        for i, part in enumerate(parts):
            col = part[:, h:h + 1]
            q = jnp.where(lane == d0 + i, col, q)
            k = jnp.where(lane == d0 + 3 + i, -col, k)
        q = jnp.where((lane >= d0 + 3) & (lane < d0 + 6), 1.0, q)
        k = jnp.where((lane >= d0) & (lane < d0 + 3), 1.0, k)
        fqk_ref[:, h * SLOT:(h + 1) * SLOT] = q.astype(BF16)
        fqk_ref[:, HEAD_W + h * SLOT:HEAD_W + (h + 1) * SLOT] = k.astype(BF16)

    vt = lax.dot_general(wvt_ref[...], u, _NT, preferred_element_type=F32).astype(BF16)
    spare = SLOT - DA_DV
    ones_then_zeros = jnp.where(lax.broadcasted_iota(jnp.int32, (spare, tm), 0) == 0,
                                1.0, 0.0).astype(BF16)
    for h in range(4):
        for ref, first in ((dvt_ref, 0), (fvt_ref, DA_WIDTH)):
            ref[0, h, 0, 0:DA_DV, :] = vt[first + h * DA_DV:first + (h + 1) * DA_DV]
            ref[0, h, 0, DA_DV:SLOT, :] = ones_then_zeros


def _in_proj(h, g, w_cat, w_vt, cos_t, sin_t, fox_bias_row, lb_row, batch, seq_len):
    n = h.shape[0]
    tm = PROJ_TILE
    assert tm == ATTN_BLOCK
    nk = seq_len // tm
    row = lambda w: pl.BlockSpec((tm, w), lambda i: (i, 0))
    tab = pl.BlockSpec((tm, 2 * SLOT), lambda i: (i % nk, 0))
    once = lambda r, c: pl.BlockSpec((r, c), lambda i: (0, 0), pipeline_mode=pl.Buffered(1))
    widths = (3 * HG_WIDTH, HG_WIDTH, HG_WIDTH, 2 * HEAD_W, 2 * HEAD_W)
    dtypes = (BF16, F32, F32, BF16, BF16)
    vt_spec = pl.BlockSpec((1, 4, 1, SLOT, tm), lambda i: (i // nk, 0, i % nk, 0, 0))
    vt_shape = jax.ShapeDtypeStruct((batch, 4, nk, SLOT, tm), BF16)
    return pl.pallas_call(
        functools.partial(_in_proj_kernel, tiles_per_seq=nk),
        grid=(n // tm,),
        in_specs=[row(D_MODEL),
                  pl.BlockSpec((1, D_MODEL), lambda i: (0, 0)),
                  once(D_MODEL, IN_COLS), once(DA_WIDTH + FX_WIDTH, D_MODEL),
                  tab, tab,
                  pl.BlockSpec((1, LANES), lambda i: (0, 0)),
                  pl.BlockSpec((1, HG_WIDTH), lambda i: (0, 0))],
        out_specs=[row(w) for w in widths] + [vt_spec, vt_spec],
        out_shape=[jax.ShapeDtypeStruct((n, w), t) for w, t in zip(widths, dtypes)]
        + [vt_shape, vt_shape],
        scratch_shapes=[pltpu.VMEM((1, LANES), F32)],
        compiler_params=pltpu.CompilerParams(dimension_semantics=("arbitrary",),
                                             vmem_limit_bytes=VMEM_LIMIT),
        name="in_proj",
    )(h, g, w_cat, w_vt, cos_t, sin_t, fox_bias_row, lb_row)


def _bf16_split3(x):
    hi = x.astype(BF16).astype(F32)
    r = x - hi
    mid = r.astype(BF16).astype(F32)
    return hi, mid, r - mid


def _softmax_block(s, m_prev):
    m_new = jnp.maximum(m_prev, jnp.max(s, axis=0, keepdims=True))
    alpha = jnp.exp2(m_prev - m_new)
    p = jnp.exp2(s - m_new).astype(BF16)
    return p, alpha, m_new


def _block_start(j, blk):
    return j * blk if isinstance(j, int) else pl.multiple_of(j * blk, blk)


def _causal_valid(i, j, blk, first_row=0):
    k_idx = j * blk + first_row + lax.broadcasted_iota(jnp.int32, (blk - first_row, blk), 0)
    q_idx = i * blk + lax.broadcasted_iota(jnp.int32, (blk - first_row, blk), 1)
    return (k_idx <= q_idx) & (k_idx >= LPAD)


def _pipelined_key_blocks(i, first, scores, consume):
    first()

    @pl.when(i >= 1)
    def _():
        scores(1, 0)

    def pair(t, carry):
        scores(2 * t + 2, 1)
        consume(2 * t + 1, 0, False)
        scores(2 * t + 3, 0)
        consume(2 * t + 2, 1, False)
        return carry

    lax.fori_loop(0, lax.shift_right_logical(jnp.maximum(i - 1, 0), 1), pair, 0)
    odd = (i & 1) == 1

    @pl.when(odd)
    def _():
        consume(i, 0, True)

    @pl.when((i >= 2) & jnp.logical_not(odd))
    def _():
        scores(i, 1)
        consume(i - 1, 0, False)
        consume(i, 1, True)


def _head_out(acc):
    return acc[0:ONE_LANE] / acc[ONE_LANE:ONE_LANE + 1]


def _head_rms(o, g_col):
    return o * lax.rsqrt(jnp.mean(o * o, axis=0, keepdims=True) + EPS) * g_col


def _store_head_pairs(o_ref, outs):
    for p in range(len(outs) // 2):
        pair = jnp.concatenate([outs[2 * p], outs[2 * p + 1]], axis=0)
        o_ref[:, p * LANES:(p + 1) * LANES] = pair.T.astype(BF16)


def _fox_attn_kernel(q_ref, k_ref, vt_ref, g_ref, o_ref, m_ref, acc_ref, sa_ref, sb_ref,
                     *, blk):
    i = pl.program_id(2)
    heads = range(ATTN_HEADS_PER_STEP)
    slot = [slice(hh * SLOT, (hh + 1) * SLOT) for hh in heads]
    q = [q_ref[:, slot[hh]] for hh in heads]
    bufs = (sa_ref, sb_ref)
    m_ref[...] = jnp.full(m_ref.shape, NEG, F32)
    acc_ref[...] = jnp.zeros_like(acc_ref)

    def scores(j, buf):
        rows = pl.ds(_block_start(j, blk), blk)
        for hh in heads:
            bufs[buf][hh] = lax.dot_general(k_ref[rows, slot[hh]], q[hh], _NT,
                                            preferred_element_type=F32)

    def consume(j, buf, masked):
        mask = _causal_valid(i, j, blk) if masked else None
        for hh in heads:
            s = bufs[buf][hh]
            if masked:
                s = jnp.where(mask, s, NEG)
            p, alpha, m_new = _softmax_block(s, m_ref[hh])
            acc_ref[hh] = alpha * acc_ref[hh] + jnp.dot(
                vt_ref[0, hh, j], p, preferred_element_type=F32)
            m_ref[hh] = m_new

    def first():
        lo = blk - FIRST_KEYS
        mask = _causal_valid(i, 0, blk, lo)
        for hh in heads:
            s = lax.dot_general(k_ref[lo:blk, slot[hh]], q[hh], _NT,
                                preferred_element_type=F32)
            p, alpha, m_new = _softmax_block(jnp.where(mask, s, NEG), m_ref[hh])
            acc_ref[hh] = alpha * acc_ref[hh] + jnp.dot(
                vt_ref[0, hh, 0, :, lo:blk], p, preferred_element_type=F32)
            m_ref[hh] = m_new

    _pipelined_key_blocks(i, first, scores, consume)
    _store_head_pairs(
        o_ref, [_head_rms(_head_out(acc_ref[hh]), g_ref[...]) for hh in heads])


def _key_spec(seq_len, groups):
    return pl.BlockSpec((seq_len, ATTN_HEADS_PER_STEP * SLOT),
                        lambda b, h, i: (b, groups + h), pipeline_mode=pl.Buffered(1))


def _value_spec(seq_len):
    nk = seq_len // ATTN_BLOCK
    return pl.BlockSpec((1, ATTN_HEADS_PER_STEP, nk, SLOT, ATTN_BLOCK),
                        lambda b, h, i: (b, h, 0, 0, 0), pipeline_mode=pl.Buffered(1))


def _fox_attn(fqk, fvt, gain, batch, seq_len):
    n = fqk.shape[0]
    blk = ATTN_BLOCK
    nq = seq_len // blk
    hps = ATTN_HEADS_PER_STEP
    groups = FX_HEADS // hps
    return pl.pallas_call(
        functools.partial(_fox_attn_kernel, blk=blk),
        grid=(batch, groups, nq),
        in_specs=[pl.BlockSpec((blk, hps * SLOT), lambda b, h, i: (b * nq + i, h)),
                  _key_spec(seq_len, groups),
                  _value_spec(seq_len),
                  pl.BlockSpec((ONE_LANE, 1), lambda b, h, i: (0, 0))],
        out_specs=pl.BlockSpec((blk, hps * ONE_LANE), lambda b, h, i: (b * nq + i, h)),
        out_shape=jax.ShapeDtypeStruct((n, 4 * ONE_LANE), BF16),
        scratch_shapes=[pltpu.VMEM((hps, 1, blk), F32), pltpu.VMEM((hps, SLOT, blk), F32),
                        pltpu.VMEM((hps, blk, blk), F32), pltpu.VMEM((hps, blk, blk), F32)],
        compiler_params=pltpu.CompilerParams(
            dimension_semantics=("arbitrary", "arbitrary", "arbitrary"),
            vmem_limit_bytes=VMEM_LIMIT),
        name="fox_attn",
    )(fqk, fqk, fvt, gain)


def _diff_attn_kernel(q_ref, k_ref, vt_ref, lam_ref, g_ref, o_ref, m_ref, acc_ref,
                      sa_ref, sb_ref, *, blk, lam_init):
    i = pl.program_id(2)
    heads = range(ATTN_HEADS_PER_STEP)
    slot = lambda s: slice(s * SLOT, (s + 1) * SLOT)
    lane = lax.broadcasted_iota(jnp.int32, (blk, SLOT), 1)
    q = []
    for hh in heads:
        q_h = q_ref[:, slot(hh)]
        q.append([jnp.where((lane >= c * DA_DQK) & (lane < (c + 1) * DA_DQK), q_h,
                            jnp.zeros_like(q_h)) for c in range(2)])
    bufs = (sa_ref, sb_ref)
    m_ref[...] = jnp.full(m_ref.shape, NEG, F32)
    acc_ref[...] = jnp.zeros_like(acc_ref)

    def scores(j, buf):
        rows = pl.ds(_block_start(j, blk), blk)
        for hh in heads:
            k_h = k_ref[rows, slot(hh)]
            for c in range(2):
                bufs[buf][2 * hh + c] = lax.dot_general(k_h, q[hh][c], _NT,
                                                        preferred_element_type=F32)

    def consume(j, buf, masked):
        mask = _causal_valid(i, j, blk) if masked else None
        for hh in heads:
            ps, alphas = [], []
            for c in range(2):
                hc = 2 * hh + c
                s = bufs[buf][hc]
                if masked:
                    s = jnp.where(mask, s, NEG)
                p, alpha, m_new = _softmax_block(s, m_ref[hc])
                m_ref[hc] = m_new
                ps.append(p)
                alphas.append(alpha)
            pv = jnp.dot(vt_ref[0, hh, j], jnp.concatenate(ps, axis=1),
                         preferred_element_type=F32)
            for c in range(2):
                hc = 2 * hh + c
                acc_ref[hc] = alphas[c] * acc_ref[hc] + pv[:, c * blk:(c + 1) * blk]

    def first():
        lo = blk - FIRST_KEYS
        mask = _causal_valid(i, 0, blk, lo)
        for hh in heads:
            k_h = k_ref[lo:blk, slot(hh)]
            ps, alphas = [], []
            for c in range(2):
                hc = 2 * hh + c
                s = lax.dot_general(k_h, q[hh][c], _NT, preferred_element_type=F32)
                p, alpha, m_new = _softmax_block(jnp.where(mask, s, NEG), m_ref[hc])
                m_ref[hc] = m_new
                ps.append(p)
                alphas.append(alpha)
            pv = jnp.dot(vt_ref[0, hh, 0, :, lo:blk], jnp.concatenate(ps, axis=1),
                         preferred_element_type=F32)
            for c in range(2):
                hc = 2 * hh + c
                acc_ref[hc] = alphas[c] * acc_ref[hc] + pv[:, c * blk:(c + 1) * blk]

    _pipelined_key_blocks(i, first, scores, consume)
    lv = lam_ref[...]
    lam = (jnp.exp(jnp.sum(lv[0:1] * lv[1:2], axis=-1, keepdims=True))
           - jnp.exp(jnp.sum(lv[2:3] * lv[3:4], axis=-1, keepdims=True)) + lam_init)
    outs = []
    for hh in heads:
        o = _head_out(acc_ref[2 * hh]) - lam * _head_out(acc_ref[2 * hh + 1])
        outs.append(_head_rms(o, g_ref[...]) * (1.0 - lam_init))
    _store_head_pairs(o_ref, outs)


def _diff_attn(dqk, dvt, lam_vecs, gain, lam_init, batch, seq_len):
    n = dqk.shape[0]
    blk = ATTN_BLOCK
    nq = seq_len // blk
    hps = ATTN_HEADS_PER_STEP
    groups = DA_HEADS // hps
    return pl.pallas_call(
        functools.partial(_diff_attn_kernel, blk=blk, lam_init=lam_init),
        grid=(batch, groups, nq),
        in_specs=[pl.BlockSpec((blk, hps * SLOT), lambda b, h, i: (b * nq + i, h)),
                  _key_spec(seq_len, groups),
                  _value_spec(seq_len),
                  pl.BlockSpec((8, LANES), lambda b, h, i: (0, 0)),
                  pl.BlockSpec((ONE_LANE, 1), lambda b, h, i: (0, 0))],
        out_specs=pl.BlockSpec((blk, hps * ONE_LANE), lambda b, h, i: (b * nq + i, h)),
        out_shape=jax.ShapeDtypeStruct((n, 4 * ONE_LANE), BF16),
        scratch_shapes=[pltpu.VMEM((2 * hps, 1, blk), F32),
                        pltpu.VMEM((2 * hps, SLOT, blk), F32),
                        pltpu.VMEM((2 * hps, blk, blk), F32),
                        pltpu.VMEM((2 * hps, blk, blk), F32)],
        compiler_params=pltpu.CompilerParams(
            dimension_semantics=("arbitrary", "arbitrary", "arbitrary"),
            vmem_limit_bytes=VMEM_LIMIT),
        name="diff_attn",
    )(dqk, dqk, dvt, lam_vecs, gain)


def _hgrn_kernel(q_ref, i_ref, g_ref, lf_ref, kk_ref, gain_ref, o_ref,
                 st_ref, hs_ref, *, tl):
    @pl.when(pl.program_id(2) == 0)
    def _():
        st_ref[...] = jnp.zeros_like(st_ref)

    c_rows, sub = HG_CHUNK, HG_SUB
    tri = (lax.broadcasted_iota(jnp.int32, (c_rows, c_rows), 1)
           <= lax.broadcasted_iota(jnp.int32, (c_rows, c_rows), 0)).astype(BF16)
    t_sub = lax.broadcasted_iota(jnp.int32, (sub, 1), 0)
    lane_c = lax.broadcasted_iota(jnp.int32, (sub, c_rows), 1)

    t_all = lax.broadcasted_iota(jnp.int32, (c_rows, c_rows), 0)
    s_all = lax.broadcasted_iota(jnp.int32, (c_rows, c_rows), 1)
    level_masks = []
    g = sub
    while g < c_rows:
        level_masks.append(((t_all // g) == (s_all // g) + 1) & ((s_all // g) % 2 == 0))
        g *= 2

    heads = range(HG_PER_STEP)
    cols = [slice(hh * HG_DK, (hh + 1) * HG_DK) for hh in heads]

    def chunk(c, carry):
        r0 = pl.multiple_of(c * c_rows, c_rows)
        rows = pl.ds(r0, c_rows)
        G, kk, qs, v, st, o_inter = [], [], [], [], [], []
        for hh in heads:
            kk.append(kk_ref[rows, cols[hh]])
            G.append(sum(jnp.dot(tri, t.astype(BF16), preferred_element_type=F32)
                         for t in _bf16_split3(lf_ref[rows, cols[hh]])))
            qs.append(q_ref[rows, cols[hh]].astype(F32))
            v.append(i_ref[rows, cols[hh]])
            hs_ref[hh] = G[hh] - jnp.log2(kk[hh])
            st.append(st_ref[hh])
            o_inter.append(lax.dot_general(
                (qs[hh] * jnp.exp2(G[hh])).astype(BF16), st[hh].astype(BF16), _NT,
                preferred_element_type=F32))
        a_mat = [jnp.zeros((c_rows, c_rows), F32) for _ in heads]
        g = sub
        for mask in level_masks:
            for hh in heads:
                ref = jnp.concatenate(
                    [jnp.broadcast_to(G[hh][p + g - 1:p + g], (2 * g, HG_DK))
                     for p in range(0, c_rows, 2 * g)], axis=0)
                e = jnp.exp2(-jnp.abs(G[hh] - ref))
                part = lax.dot_general((qs[hh] * e).astype(BF16), (kk[hh] * e).astype(BF16),
                                       _NT, preferred_element_type=F32)
                a_mat[hh] = jnp.where(mask, part, a_mat[hh])
            g *= 2
        a_rows = [[] for _ in heads]
        for b in range(c_rows // sub):
            lo = b * sub
            for hh in heads:
                q_b = qs[hh][lo:lo + sub]
                g_b = G[hh][lo:lo + sub]
                a_blk = a_mat[hh][lo:lo + sub]
                for s in range(lo, lo + sub):
                    y = q_b * jnp.exp2(g_b - hs_ref[hh, s:s + 1, :])
                    a_blk = jnp.where(lane_c == s, jnp.sum(y, axis=-1, keepdims=True), a_blk)
                a_rows[hh].append(jnp.where(lane_c <= t_sub + lo, a_blk, 0.0))
        for hh in heads:
            a_full = jnp.concatenate(a_rows[hh], axis=0).astype(BF16)
            o = o_inter[hh] + jnp.dot(a_full, v[hh], preferred_element_type=F32)
            g_last = G[hh][c_rows - 1:c_rows]
            kd = kk[hh] * jnp.exp2(g_last - G[hh])
            st_ref[hh] = st[hh] * jnp.exp2(g_last) + lax.dot_general(
                v[hh], kd.astype(BF16), _TN, preferred_element_type=F32)
            o = _rms(o, gain_ref[...]) * g_ref[rows, cols[hh]].astype(F32)
            o_ref[rows, cols[hh]] = o.astype(BF16)
        return carry

    lax.fori_loop(0, tl // c_rows, chunk, 0, unroll=4)


def _hgrn(hqig, hlf, hkk, gain, batch, seq_len):
    n = hqig.shape[0]
    tl = ROW_TILE
    nt = seq_len // tl
    w = HG_PER_STEP * HG_DK
    groups = HG_HEADS // HG_PER_STEP
    col = lambda off: pl.BlockSpec((tl, w), lambda b, h, j: (b * nt + j, off + h))
    return pl.pallas_call(
        functools.partial(_hgrn_kernel, tl=tl),
        grid=(batch, groups, nt),
        in_specs=[col(0), col(groups), col(2 * groups), col(0), col(0),
                  pl.BlockSpec((1, HG_DK), lambda b, h, j: (0, 0))],
        out_specs=col(0),
        out_shape=jax.ShapeDtypeStruct((n, HG_WIDTH), BF16),
        scratch_shapes=[pltpu.VMEM((HG_PER_STEP, HG_DK, HG_DK), F32),
                        pltpu.VMEM((HG_PER_STEP, HG_CHUNK, HG_DK), F32)],
        compiler_params=pltpu.CompilerParams(
            dimension_semantics=("arbitrary", "arbitrary", "arbitrary")),
        name="hgrn2",
    )(hqig, hqig, hqig, hlf, hkk, gain)


def _out_router_kernel(oa_ref, ob_ref, oc_ref, h_ref, wo_ref, g_ref, wrh_ref, wrl_ref, br_ref,
                       h1_ref, route_ref, cnt_ref, carry_ref):
    @pl.when(pl.program_id(0) == 0)
    def _():
        carry_ref[...] = jnp.zeros_like(carry_ref)

    h1 = (h_ref[...]
          + jnp.dot(oa_ref[...], wo_ref[0:512, :], preferred_element_type=F32)
          + jnp.dot(ob_ref[...], wo_ref[512:768, :], preferred_element_type=F32)
          + jnp.dot(oc_ref[...], wo_ref[768:1024, :], preferred_element_type=F32))
    h1_ref[...] = h1
    u = _rms(h1, g_ref[...])
    u_hi = u.astype(BF16)
    u_lo = (u - u_hi.astype(F32)).astype(BF16)
    logits = (jnp.dot(u_hi, wrh_ref[...], preferred_element_type=F32)
              + (jnp.dot(u_hi, wrl_ref[...], preferred_element_type=F32)
                 + jnp.dot(u_lo, wrh_ref[...], preferred_element_type=F32))
              + br_ref[...])
    tm = logits.shape[0]
    lane = lax.broadcasted_iota(jnp.int32, (tm, LANES), 1).astype(F32)
    big = float(LANES)
    is_g = (lane >= N_EXPERTS) & (lane < N_EXPERTS + N_GROUPS)
    gl = jnp.where(is_g, logits, -jnp.inf)
    gmax = jnp.max(gl, axis=-1, keepdims=True)
    gsel = jnp.min(jnp.where(gl == gmax, lane, big), axis=-1, keepdims=True) - N_EXPERTS
    p_g = 1.0 / jnp.sum(jnp.exp(gl - gmax), axis=-1, keepdims=True)
    lo = gsel * EXPERTS_PER_GROUP
    el = jnp.where((lane >= lo) & (lane < lo + EXPERTS_PER_GROUP), logits, -jnp.inf)
    m1 = jnp.max(el, axis=-1, keepdims=True)
    i1 = jnp.min(jnp.where(el == m1, lane, big), axis=-1, keepdims=True)
    el2 = jnp.where(lane == i1, -jnp.inf, el)
    m2 = jnp.max(el2, axis=-1, keepdims=True)
    i2 = jnp.min(jnp.where(el2 == m2, lane, big), axis=-1, keepdims=True)
    r = jnp.exp(m2 - m1)
    gate1 = p_g / (1.0 + r)
    gate2 = gate1 * r
    oh1 = lane == i1
    oh2 = lane == i2
    onehot = jnp.where(oh1 | oh2, 1.0, 0.0)
    tri = (lax.broadcasted_iota(jnp.int32, (tm, tm), 1)
           < lax.broadcasted_iota(jnp.int32, (tm, tm), 0)).astype(BF16)
    before = jnp.dot(tri, onehot.astype(BF16), preferred_element_type=F32) + carry_ref[...]
    rank1 = jnp.sum(jnp.where(oh1, before, 0.0), axis=-1, keepdims=True)
    rank2 = jnp.sum(jnp.where(oh2, before, 0.0), axis=-1, keepdims=True)
    total = carry_ref[...] + jnp.sum(onehot, axis=0, keepdims=True)
    carry_ref[...] = total
    cnt_ref[...] = total
    route = jnp.where(lane == 0, i1, 0.0)
    for idx, val in ((1, i2), (2, gate1), (3, gate2), (4, rank1), (5, rank2)):
        route = jnp.where(lane == idx, val, route)
    route_ref[...] = route


def _out_router(oa, ob, oc, h, w_out, g, w_rt, b_rt):
    n = h.shape[0]
    w_rt_hi = w_rt.astype(BF16)
    w_rt_lo = (w_rt - w_rt_hi.astype(F32)).astype(BF16)
    tm = ROUTER_TILE
    row = lambda w: pl.BlockSpec((tm, w), lambda i: (i, 0))
    const = lambda r, c: pl.BlockSpec((r, c), lambda i: (0, 0))
    return pl.pallas_call(
        _out_router_kernel,
        grid=(n // tm,),
        in_specs=[row(HG_WIDTH), row(DA_WIDTH), row(FX_WIDTH), row(D_MODEL),
                  const(D_MODEL, D_MODEL),
                  const(1, D_MODEL), const(D_MODEL, LANES), const(D_MODEL, LANES),
                  const(1, LANES)],
        out_specs=[row(D_MODEL), row(LANES), const(1, LANES)],
        out_shape=[jax.ShapeDtypeStruct((n, D_MODEL), F32),
                   jax.ShapeDtypeStruct((n, LANES), F32),
                   jax.ShapeDtypeStruct((1, LANES), F32)],
        scratch_shapes=[pltpu.VMEM((1, LANES), F32)],
        compiler_params=pltpu.CompilerParams(dimension_semantics=("arbitrary",)),
        name="out_router",
    )(oa, ob, oc, h, w_out, g, w_rt_hi, w_rt_lo, b_rt)


def _dispatch_kernel(dest_ref, unused_ref, h_ref, g_ref, xs_in_ref, xs_ref, back_ref,
                     u_ref, sem, *, tm, n, tb):
    del xs_in_ref
    i = pl.program_id(0)
    n_tiles = n // tm
    base = (i - 1) * tm
    cur = i % 2

    @pl.when(i == 0)
    def _():
        def fill(s, carry):
            parity = lax.shift_right_logical(s, tb.bit_length() - 1) & 1
            back_ref[s] = 2 * n + parity * tb + (s & (tb - 1))
            return carry

        n_ranges = unused_ref.shape[0] // 2

        def one_range(e, carry):
            return lax.fori_loop(unused_ref[e], unused_ref[n_ranges + e], fill, carry)

        lax.fori_loop(0, n_ranges, one_range, 0)

    def issue(r, carry):
        for k in range(2):
            src = k * n + base + r
            d = dest_ref[src]
            back_ref[d] = src
            pltpu.make_async_copy(u_ref.at[1 - cur, pl.ds(r, 1)], xs_ref.at[pl.ds(d, 1)],
                                  sem).start()
        return carry

    @pl.when(i >= 1)
    def _():
        lax.fori_loop(0, tm, issue, 0, unroll=8)

    @pl.when(i < n_tiles)
    def _():
        u_ref[cur] = _rms(h_ref[...], g_ref[...])

    @pl.when(i >= 1)
    def _():
        for _ in range(2):
            pltpu.make_async_copy(u_ref.at[1 - cur], xs_ref.at[pl.ds(0, tm)], sem).wait()


def _dispatch(dest, unused, h1, g, xs_init):
    n = h1.shape[0]
    tm = ROW_TILE
    p_rows = xs_init.shape[0]
    return pl.pallas_call(
        functools.partial(_dispatch_kernel, tm=tm, n=n, tb=MOE_BLOCK),
        grid_spec=pltpu.PrefetchScalarGridSpec(
            num_scalar_prefetch=2,
            grid=(n // tm + 1,),
            in_specs=[pl.BlockSpec((tm, D_MODEL),
                                   lambda i, d, un: (jnp.minimum(i, n // tm - 1), 0)),
                      pl.BlockSpec((1, D_MODEL), lambda i, d, un: (0, 0)),
                      pl.BlockSpec(memory_space=pl.ANY)],
            out_specs=[pl.BlockSpec(memory_space=pl.ANY),
                       pl.BlockSpec(memory_space=pltpu.SMEM)],
            scratch_shapes=[pltpu.VMEM((2, tm, D_MODEL), F32),
                            pltpu.SemaphoreType.DMA(())]),
        out_shape=[jax.ShapeDtypeStruct(xs_init.shape, xs_init.dtype),
                   jax.ShapeDtypeStruct((p_rows,), jnp.int32)],
        input_output_aliases={4: 0},
        compiler_params=pltpu.CompilerParams(dimension_semantics=("arbitrary",)),
        name="moe_dispatch",
    )(dest, unused, h1, g, xs_init)


def _expert_kernel(be_ref, na_ref, back_ref, x_ref, w1_ref, w3_ref, w2_ref, out_ref,
                   y_ref, w1b_ref, w3b_ref, w2b_ref, sem, *, tb):
    j = pl.program_id(0)
    na = na_ref[0]
    cur = j % 2

    def scatter(block, buf):
        for r in range(tb):
            pltpu.make_async_copy(y_ref.at[buf, pl.ds(r, 1)],
                                  out_ref.at[pl.ds(back_ref[block * tb + r], 1)],
                                  sem.at[buf]).start()

    def compute(buf):
        x = x_ref[...].astype(BF16)
        a = jnp.dot(x, w1b_ref[...], preferred_element_type=F32)
        b = jnp.dot(x, w3b_ref[...], preferred_element_type=F32)
        act = (a * _sigmoid(a) * b).astype(BF16)
        y_ref[buf] = jnp.dot(act, w2b_ref[...], preferred_element_type=F32)

    new_expert = (j == 0) | (be_ref[j] != be_ref[jnp.maximum(j - 1, 0)])

    @pl.when((j < na) & new_expert)
    def _():
        w1b_ref[...] = w1_ref[0].astype(BF16)
        w3b_ref[...] = w3_ref[0].astype(BF16)
        w2b_ref[...] = w2_ref[0].astype(BF16)

    @pl.when((j >= 2) & (j < na + 2))
    def _():
        pltpu.make_async_copy(y_ref.at[cur], out_ref.at[pl.ds(0, tb)], sem.at[cur]).wait()

    @pl.when(j == 0)
    def _():
        y_ref[...] = jnp.zeros_like(y_ref)
        first_spare = out_ref.shape[0] - 2 * tb
        spare = [pltpu.make_async_copy(
            y_ref.at[b], out_ref.at[pl.ds(first_spare + b * tb, tb)], sem.at[b])
            for b in range(2)]
        for copy in spare:
            copy.start()
        for copy in spare:
            copy.wait()
        compute(cur)

    @pl.when((j >= 1) & (j < na))
    def _():
        scatter(j - 1, 1 - cur)
        compute(cur)

    @pl.when(j == na)
    def _():
        scatter(j - 1, 1 - cur)


def _experts(blk_expert, n_active, slot_back, xs, w1, w3, w2, layer, n):
    p_rows = xs.shape[0]
    tb = MOE_BLOCK
    last = lambda j, na: jnp.maximum(jnp.minimum(j, na[0] - 1), 0)
    w_spec = lambda r, c: pl.BlockSpec((None, 1, r, c),
                                       lambda j, be, na, back: (layer, be[j], 0, 0))
    return pl.pallas_call(
        functools.partial(_expert_kernel, tb=tb),
        grid_spec=pltpu.PrefetchScalarGridSpec(
            num_scalar_prefetch=3,
            grid=(p_rows // tb + 1,),
            in_specs=[pl.BlockSpec((tb, D_MODEL), lambda j, be, na, back: (last(j, na), 0)),
                      w_spec(D_MODEL, D_EXPERT), w_spec(D_MODEL, D_EXPERT),
                      w_spec(D_EXPERT, D_MODEL)],
            out_specs=pl.BlockSpec(memory_space=pl.ANY),
            scratch_shapes=[pltpu.VMEM((2, tb, D_MODEL), F32),
                            pltpu.VMEM((D_MODEL, D_EXPERT), BF16),
                            pltpu.VMEM((D_MODEL, D_EXPERT), BF16),
                            pltpu.VMEM((D_EXPERT, D_MODEL), BF16),
                            pltpu.SemaphoreType.DMA((2,))]),
        out_shape=jax.ShapeDtypeStruct((2 * n + 2 * tb, D_MODEL), F32),
        compiler_params=pltpu.CompilerParams(dimension_semantics=("arbitrary",),
                                             vmem_limit_bytes=VMEM_LIMIT),
        name="moe_experts",
    )(blk_expert, n_active, slot_back, xs, w1, w3, w2)


def _combine_kernel(h_ref, route_ref, gfin_ref, y0_ref, y1_ref, o_ref, *, final):
    route = route_ref[...]
    out = h_ref[...] + route[:, 2:3] * y0_ref[...] + route[:, 3:4] * y1_ref[...]
    if final:
        out = _rms(out, gfin_ref[...])
    o_ref[...] = out


def _combine(h1, route, g_final, ys, batch, seq_len, final):
    n = h1.shape[0]
    tm = ROW_TILE
    tiles_per_seq = seq_len // tm
    skip_tiles = (LPAD + N_META) // tm if final else 0
    out_tiles = tiles_per_seq - skip_tiles
    tile = lambda b, i: b * tiles_per_seq + skip_tiles + i
    in_row = lambda w: pl.BlockSpec((tm, w), lambda b, i: (tile(b, i), 0))
    y_rows = lambda k: pl.BlockSpec((tm, D_MODEL),
                                    lambda b, i: (k * (n // tm) + tile(b, i), 0))
    return pl.pallas_call(
        functools.partial(_combine_kernel, final=final),
        grid=(batch, out_tiles),
        in_specs=[in_row(D_MODEL), in_row(LANES),
                  pl.BlockSpec((1, D_MODEL), lambda b, i: (0, 0)),
                  y_rows(0), y_rows(1)],
        out_specs=pl.BlockSpec((tm, D_MODEL), lambda b, i: (b * out_tiles + i, 0)),
        out_shape=jax.ShapeDtypeStruct((batch * out_tiles * tm, D_MODEL), F32),
        compiler_params=pltpu.CompilerParams(dimension_semantics=("arbitrary", "arbitrary")),
        name="moe_combine",
    )(h1, route, g_final, ys, ys)


def _rope_partner(m):
    lead = m.shape[:-1]
    x = m.reshape(lead + (DA_WIDTH // DA_DQK, 2, DA_DQK // 2))
    return x[..., ::-1, :].reshape(lead + (DA_WIDTH,))


def _rope_tables(seq_len):
    pos = (jnp.arange(seq_len) - LPAD).astype(F32)
    inv = ROPE_THETA ** (-jnp.arange(0, DA_DQK, 2, dtype=F32) / DA_DQK)
    ang = pos[:, None] * inv[None, :]
    comps = DA_DV // DA_DQK
    cos = jnp.tile(jnp.cos(ang), (1, 2 * comps))
    sin = jnp.tile(jnp.concatenate([-jnp.sin(ang), jnp.sin(ang)], axis=1), (1, comps))
    q_scale = DA_DQK ** -0.5 * LOG2E
    two = lambda t: jnp.tile(t, (1, LANES // DA_DV))
    lay = lambda t: jnp.concatenate([two(t * q_scale), two(t)], axis=1)
    return lay(cos), lay(sin)


def _in_weights(w):
    hq, hf, hi, hg = (w[:, i * 512:(i + 1) * 512] for i in range(4))
    dq, dk, dv = (w[:, 2048 + i * 256:2048 + (i + 1) * 256] for i in range(3))
    fq, fk, fv = (w[:, 2816 + i * 256:2816 + (i + 1) * 256] for i in range(3))
    ff = w[:, 3584:3588]
    cat = jnp.concatenate(
        [hq, hi, hg, hf,
         dq, dk, _rope_partner(dq), _rope_partner(dk),
         fq * (FX_DH ** -0.5), fk,
         ff, jnp.zeros((D_MODEL, LANES - FX_HEADS), w.dtype)],
        axis=1)
    w_vt = jnp.concatenate([dv, fv], axis=1).T
    return cat.astype(BF16), w_vt.astype(BF16)


def _pad_lanes(v, width=LANES):
    return jnp.zeros((1, width), F32).at[0, :v.shape[0]].set(v.astype(F32))


def kernel(x, meta_tokens, norm_mix, w_in, hgrn_lb, hgrn_norm, diff_lambda, diff_norm,
           fox_bias, fox_norm, w_out, norm_ffn, w_group, b_group, w_router, b_router,
           w1, w3, w2, norm_final):
    batch, seq, d = x.shape
    depth = w_in.shape[0]
    seq_len = LPAD + N_META + seq
    n = batch * seq_len
    pad = jnp.zeros((batch, LPAD, d), x.dtype)
    meta = jnp.broadcast_to(meta_tokens.astype(x.dtype)[None], (batch, N_META, d))
    h = jnp.concatenate([pad, meta, x], axis=1).reshape(n, d)

    s_lb = jax.nn.softmax(hgrn_lb.astype(F32), axis=0)
    lb_all = jnp.cumsum(s_lb, axis=0) - s_lb[0]
    cos_t, sin_t = _rope_tables(seq_len)

    tb = MOE_BLOCK
    n_blocks = (2 * n) // tb + N_EXPERTS
    p_rows = n_blocks * tb

    xs = jnp.zeros((p_rows, d), F32)
    for layer in range(depth):
        lam_init = 0.8 - 0.6 * math.exp(-0.3 * layer)
        w_cat, w_vt = _in_weights(w_in[layer])
        hqig, hlf, hkk, dqk, fqk, dvt, fvt = _in_proj(
            h, norm_mix[layer][None, :], w_cat, w_vt, cos_t, sin_t,
            _pad_lanes(fox_bias[layer]), lb_all[layer][None, :], batch, seq_len)

        o_a = _hgrn(hqig, hlf, hkk, hgrn_norm[layer][None, :], batch, seq_len)

        lam_vecs = jnp.zeros((8, LANES), F32).at[:4, :DA_DQK].set(diff_lambda[layer].astype(F32))
        o_b = _diff_attn(dqk, dvt, lam_vecs, diff_norm[layer][:, None], lam_init,
                         batch, seq_len)

        o_c = _fox_attn(fqk, fvt, fox_norm[layer][:, None], batch, seq_len)

        w_rt = jnp.concatenate(
            [w_router[layer], w_group[layer],
             jnp.zeros((d, LANES - N_EXPERTS - N_GROUPS), F32)], axis=1)
        b_rt = _pad_lanes(jnp.concatenate([b_router[layer], b_group[layer]]))
        h1, route, counts = _out_router(
            o_a, o_b, o_c, h, w_out[layer].astype(BF16), norm_ffn[layer][None, :], w_rt, b_rt)

        cnt = counts[0, :N_EXPERTS].astype(jnp.int32)
        padded = (cnt + tb - 1) // tb * tb
        p_end = jnp.cumsum(padded)
        p_start = p_end - padded
        ids = route[:, 0:2].astype(jnp.int32)
        ranks = route[:, 4:6].astype(jnp.int32)
        dest = (p_start[ids] + ranks).T.reshape(2 * n)
        blk_start = jnp.arange(n_blocks + 1, dtype=jnp.int32) * tb
        blk_expert = jnp.minimum(
            jnp.sum((p_end[None, :] <= blk_start[:, None]).astype(jnp.int32), axis=1),
            N_EXPERTS - 1)
        n_active = (p_end[-1:] // tb).astype(jnp.int32)

        unused = jnp.concatenate(
            [p_start + cnt, p_end[-1:], p_end, jnp.full((1,), p_rows)]).astype(jnp.int32)
        xs, slot_back = _dispatch(dest, unused, h1, norm_ffn[layer][None, :], xs)
        ys = _experts(blk_expert, n_active, slot_back, xs, w1, w3, w2, layer, n)
        h = _combine(h1, route, norm_final[None, :], ys, batch, seq_len,
                     final=(layer == depth - 1))

    return h.reshape(batch, seq, d)
```

```python
import functools
import math

import jax
import jax.numpy as jnp
from jax import lax
from jax.experimental import pallas as pl
from jax.experimental.pallas import tpu as pltpu

F32 = jnp.float32
BF16 = jnp.bfloat16

D_MODEL = 1024
N_META = 16
HG_HEADS = 4
HG_DK = 128
HG_WIDTH = 512
HG_CHUNK = 64
HG_SUB = 8
HG_PER_STEP = 4
DA_HEADS = 4
DA_DV = 64
DA_DQK = 32
DA_WIDTH = 256
FX_HEADS = 4
FX_DH = 64
FX_WIDTH = 256
N_GROUPS = 4
EXPERTS_PER_GROUP = 8
N_EXPERTS = 32
D_EXPERT = 512
ROPE_THETA = 10000.0
EPS = 1e-6
NEG = -1e30
TINY = 1e-30
LOG2E = 1.4426950408889634

LANES = 128
SLOT = LANES
HEAD_W = 4 * SLOT
ROW_TILE = 512
PROJ_TILE = 512
ROUTER_TILE = 512
CUMSUM_ROWS = 128
ATTN_BLOCK = 512
ATTN_HEADS_PER_STEP = 4
MOE_BLOCK = 256
LPAD = ATTN_BLOCK - N_META
FIRST_KEYS = LANES
assert ATTN_BLOCK - FIRST_KEYS <= LPAD
ONE_LANE = 64
VMEM_LIMIT = 56 * 1024 * 1024

_C_HQIG = (0, 1536)
_C_HF = (1536, 2048)
_C_DQK = (2048, 2560)
_C_DQKP = (2560, 3072)
_C_FQK = (3072, 3584)
_C_FF = (3584, 3712)
IN_COLS = 3712

_NT = (((1,), (1,)), ((), ()))
_TN = (((0,), (0,)), ((), ()))


def _sigmoid(x):
    return 1.0 / (1.0 + jnp.exp(-x))


def _rms(x, g):
    return x * lax.rsqrt(jnp.mean(x * x, axis=-1, keepdims=True) + EPS) * g


def _in_proj_kernel(h_ref, g_ref, w_ref, wvt_ref, cos_ref, sin_ref, fb_ref, lb_ref,
                    hqig_ref, hlf_ref, hkk_ref, dqk_ref, fqk_ref, dvt_ref, fvt_ref,
                    carry_ref, *, tiles_per_seq):
    u = _rms(h_ref[...], g_ref[...]).astype(BF16)
    tm = u.shape[0]

    def mm(c):
        return jnp.dot(u, w_ref[:, c[0]:c[1]], preferred_element_type=F32)

    hg = mm(_C_HQIG)
    q_raw, g_raw = hg[:, 0:HG_WIDTH], hg[:, 2 * HG_WIDTH:3 * HG_WIDTH]
    hqig_ref[:, 0:HG_WIDTH] = (q_raw * _sigmoid(q_raw) * (HG_DK ** -0.5)).astype(BF16)
    hqig_ref[:, HG_WIDTH:2 * HG_WIDTH] = hg[:, HG_WIDTH:2 * HG_WIDTH].astype(BF16)
    hqig_ref[:, 2 * HG_WIDTH:3 * HG_WIDTH] = (g_raw * _sigmoid(g_raw)).astype(BF16)
    lb = lb_ref[...]
    sig = _sigmoid(mm(_C_HF))
    row = ((pl.program_id(0) % tiles_per_seq) * tm
           + lax.broadcasted_iota(jnp.int32, (tm, 1), 0))
    valid = row >= LPAD
    hlf_ref[...] = jnp.where(
        valid, jnp.log2(jnp.maximum(lb + (1.0 - lb) * sig, TINY)), 0.0)
    hkk_ref[...] = jnp.where(valid, (1.0 - lb) * (1.0 - sig), 0.0)
    half_lane = lax.broadcasted_iota(jnp.int32, (tm, LANES), 1) < DA_DV

    def spread(x):
        tiles = []
        for t in range(x.shape[1] // LANES):
            tile = x[:, t * LANES:(t + 1) * LANES]
            tiles.append(jnp.where(half_lane, tile, 0.0))
            tiles.append(jnp.where(half_lane, pltpu.roll(tile, LANES // 2, axis=1), 0.0))
        return jnp.concatenate(tiles, axis=1)

    def per_tile(t_ref):
        return jnp.concatenate([t_ref[:, 0:LANES]] * 2 + [t_ref[:, LANES:2 * LANES]] * 2, axis=1)

    rot = mm(_C_DQK) * per_tile(cos_ref) + mm(_C_DQKP) * per_tile(sin_ref)
    dqk_ref[:, 0:HEAD_W] = spread(rot[:, 0:DA_WIDTH]).astype(BF16)
    dqk_ref[:, HEAD_W:2 * HEAD_W] = spread(rot[:, DA_WIDTH:2 * DA_WIDTH]).astype(BF16)

    @pl.when(pl.program_id(0) % tiles_per_seq == 0)
    def _():
        carry_ref[...] = jnp.zeros_like(carry_ref)

    x = mm(_C_FF) + fb_ref[...]
    lf = jnp.minimum(x, 0.0) - jnp.log(1.0 + jnp.exp(-jnp.abs(x)))
    tri = (lax.broadcasted_iota(jnp.int32, (CUMSUM_ROWS, CUMSUM_ROWS), 1)
           <= lax.broadcasted_iota(jnp.int32, (CUMSUM_ROWS, CUMSUM_ROWS), 0)).astype(BF16)
    lf_terms = [t.astype(BF16) for t in _bf16_split3(lf)]
    total = carry_ref[...]
    groups = []
    for r0 in range(0, tm, CUMSUM_ROWS):
        part = sum(jnp.dot(tri, t[r0:r0 + CUMSUM_ROWS], preferred_element_type=F32)
                   for t in lf_terms) + total
        total = part[CUMSUM_ROWS - 1:CUMSUM_ROWS, :]
        groups.append(part)
    cs = jnp.concatenate(groups, axis=0)
    carry_ref[...] = total
    parts = _bf16_split3(cs * LOG2E)
    fqk = mm(_C_FQK)
    fq = spread(fqk[:, 0:FX_WIDTH]) * LOG2E
    fk = spread(fqk[:, FX_WIDTH:2 * FX_WIDTH])
    lane = lax.broadcasted_iota(jnp.int32, (tm, SLOT), 1)
    d0 = FX_DH
    for h in range(FX_HEADS):
        q = fq[:, h * SLOT:(h + 1) * SLOT]
        k = fk[:, h * SLOT:(h + 1) * SLOT]
        for i, part in enumerate(parts):
            col = part[:, h:h + 1]
            q = jnp.where(lane == d0 + i, col, q)
            k = jnp.where(lane == d0 + 3 + i, -col, k)
        q = jnp.where((lane >= d0 + 3) & (lane < d0 + 6), 1.0, q)
        k = jnp.where((lane >= d0) & (lane < d0 + 3), 1.0, k)
        fqk_ref[:, h * SLOT:(h + 1) * SLOT] = q.astype(BF16)
        fqk_ref[:, HEAD_W + h * SLOT:HEAD_W + (h + 1) * SLOT] = k.astype(BF16)

    vt = lax.dot_general(wvt_ref[...], u, _NT, preferred_element_type=F32).astype(BF16)
    spare = SLOT - DA_DV
    ones_then_zeros = jnp.where(lax.broadcasted_iota(jnp.int32, (spare, tm), 0) == 0,
                                1.0, 0.0).astype(BF16)
    for h in range(4):
        for ref, first in ((dvt_ref, 0), (fvt_ref, DA_WIDTH)):
            ref[0, h, 0, 0:DA_DV, :] = vt[first + h * DA_DV:first + (h + 1) * DA_DV]
            ref[0, h, 0, DA_DV:SLOT, :] = ones_then_zeros


def _in_proj(h, g, w_cat, w_vt, cos_t, sin_t, fox_bias_row, lb_row, batch, seq_len):
    n = h.shape[0]
    tm = PROJ_TILE
    assert tm == ATTN_BLOCK
    nk = seq_len // tm
    row = lambda w: pl.BlockSpec((tm, w), lambda i: (i, 0))
    tab = pl.BlockSpec((tm, 2 * SLOT), lambda i: (i % nk, 0))
    once = lambda r, c: pl.BlockSpec((r, c), lambda i: (0, 0), pipeline_mode=pl.Buffered(1))
    widths = (3 * HG_WIDTH, HG_WIDTH, HG_WIDTH, 2 * HEAD_W, 2 * HEAD_W)
    dtypes = (BF16, F32, F32, BF16, BF16)
    vt_spec = pl.BlockSpec((1, 4, 1, SLOT, tm), lambda i: (i // nk, 0, i % nk, 0, 0))
    vt_shape = jax.ShapeDtypeStruct((batch, 4, nk, SLOT, tm), BF16)
    return pl.pallas_call(
        functools.partial(_in_proj_kernel, tiles_per_seq=nk),
        grid=(n // tm,),
        in_specs=[row(D_MODEL),
                  pl.BlockSpec((1, D_MODEL), lambda i: (0, 0)),
                  once(D_MODEL, IN_COLS), once(DA_WIDTH + FX_WIDTH, D_MODEL),
                  tab, tab,
                  pl.BlockSpec((1, LANES), lambda i: (0, 0)),
                  pl.BlockSpec((1, HG_WIDTH), lambda i: (0, 0))],
        out_specs=[row(w) for w in widths] + [vt_spec, vt_spec],
        out_shape=[jax.ShapeDtypeStruct((n, w), t) for w, t in zip(widths, dtypes)]
        + [vt_shape, vt_shape],
        scratch_shapes=[pltpu.VMEM((1, LANES), F32)],
        compiler_params=pltpu.CompilerParams(dimension_semantics=("arbitrary",),
                                             vmem_limit_bytes=VMEM_LIMIT),
        name="in_proj",
    )(h, g, w_cat, w_vt, cos_t, sin_t, fox_bias_row, lb_row)


def _bf16_split3(x):
    hi = x.astype(BF16).astype(F32)
    r = x - hi
    mid = r.astype(BF16).astype(F32)
    return hi, mid, r - mid


def _softmax_block(s, m_prev):
    m_new = jnp.maximum(m_prev, jnp.max(s, axis=0, keepdims=True))
    alpha = jnp.exp2(m_prev - m_new)
    p = jnp.exp2(s - m_new).astype(BF16)
    return p, alpha, m_new


def _block_start(j, blk):
    return j * blk if isinstance(j, int) else pl.multiple_of(j * blk, blk)


def _causal_valid(i, j, blk, first_row=0):
    k_idx = j * blk + first_row + lax.broadcasted_iota(jnp.int32, (blk - first_row, blk), 0)
    q_idx = i * blk + lax.broadcasted_iota(jnp.int32, (blk - first_row, blk), 1)
    return (k_idx <= q_idx) & (k_idx >= LPAD)


def _pipelined_key_blocks(i, first, scores, consume):
    first()

    @pl.when(i >= 1)
    def _():
        scores(1, 0)

    def pair(t, carry):
        scores(2 * t + 2, 1)
        consume(2 * t + 1, 0, False)
        scores(2 * t + 3, 0)
        consume(2 * t + 2, 1, False)
        return carry

    lax.fori_loop(0, lax.shift_right_logical(jnp.maximum(i - 1, 0), 1), pair, 0)
    odd = (i & 1) == 1

    @pl.when(odd)
    def _():
        consume(i, 0, True)

    @pl.when((i >= 2) & jnp.logical_not(odd))
    def _():
        scores(i, 1)
        consume(i - 1, 0, False)
        consume(i, 1, True)


def _head_out(acc):
    return acc[0:ONE_LANE] / acc[ONE_LANE:ONE_LANE + 1]


def _head_rms(o, g_col):
    return o * lax.rsqrt(jnp.mean(o * o, axis=0, keepdims=True) + EPS) * g_col


def _store_head_pairs(o_ref, outs):
    for p in range(len(outs) // 2):
        pair = jnp.concatenate([outs[2 * p], outs[2 * p + 1]], axis=0)
        o_ref[:, p * LANES:(p + 1) * LANES] = pair.T.astype(BF16)


def _fox_attn_kernel(q_ref, k_ref, vt_ref, g_ref, o_ref, m_ref, acc_ref, sa_ref, sb_ref,
                     *, blk):
    i = pl.program_id(2)
    heads = range(ATTN_HEADS_PER_STEP)
    slot = [slice(hh * SLOT, (hh + 1) * SLOT) for hh in heads]
    q = [q_ref[:, slot[hh]] for hh in heads]
    bufs = (sa_ref, sb_ref)
    m_ref[...] = jnp.full(m_ref.shape, NEG, F32)
    acc_ref[...] = jnp.zeros_like(acc_ref)

    def scores(j, buf):
        rows = pl.ds(_block_start(j, blk), blk)
        for hh in heads:
            bufs[buf][hh] = lax.dot_general(k_ref[rows, slot[hh]], q[hh], _NT,
                                            preferred_element_type=F32)

    def consume(j, buf, masked):
        mask = _causal_valid(i, j, blk) if masked else None
        for hh in heads:
            s = bufs[buf][hh]
            if masked:
                s = jnp.where(mask, s, NEG)
            p, alpha, m_new = _softmax_block(s, m_ref[hh])
            acc_ref[hh] = alpha * acc_ref[hh] + jnp.dot(
                vt_ref[0, hh, j], p, preferred_element_type=F32)
            m_ref[hh] = m_new

    def first():
        lo = blk - FIRST_KEYS
        mask = _causal_valid(i, 0, blk, lo)
        for hh in heads:
            s = lax.dot_general(k_ref[lo:blk, slot[hh]], q[hh], _NT,
                                preferred_element_type=F32)
            p, alpha, m_new = _softmax_block(jnp.where(mask, s, NEG), m_ref[hh])
            acc_ref[hh] = alpha * acc_ref[hh] + jnp.dot(
                vt_ref[0, hh, 0, :, lo:blk], p, preferred_element_type=F32)
            m_ref[hh] = m_new

    _pipelined_key_blocks(i, first, scores, consume)
    _store_head_pairs(
        o_ref, [_head_rms(_head_out(acc_ref[hh]), g_ref[...]) for hh in heads])


def _key_spec(seq_len, groups):
    return pl.BlockSpec((seq_len, ATTN_HEADS_PER_STEP * SLOT),
                        lambda b, h, i: (b, groups + h), pipeline_mode=pl.Buffered(1))


def _value_spec(seq_len):
    nk = seq_len // ATTN_BLOCK
    return pl.BlockSpec((1, ATTN_HEADS_PER_STEP, nk, SLOT, ATTN_BLOCK),
                        lambda b, h, i: (b, h, 0, 0, 0), pipeline_mode=pl.Buffered(1))


def _fox_attn(fqk, fvt, gain, batch, seq_len):
    n = fqk.shape[0]
    blk = ATTN_BLOCK
    nq = seq_len // blk
    hps = ATTN_HEADS_PER_STEP
    groups = FX_HEADS // hps
    return pl.pallas_call(
        functools.partial(_fox_attn_kernel, blk=blk),
        grid=(batch, groups, nq),
        in_specs=[pl.BlockSpec((blk, hps * SLOT), lambda b, h, i: (b * nq + i, h)),
                  _key_spec(seq_len, groups),
                  _value_spec(seq_len),
                  pl.BlockSpec((ONE_LANE, 1), lambda b, h, i: (0, 0))],
        out_specs=pl.BlockSpec((blk, hps * ONE_LANE), lambda b, h, i: (b * nq + i, h)),
        out_shape=jax.ShapeDtypeStruct((n, 4 * ONE_LANE), BF16),
        scratch_shapes=[pltpu.VMEM((hps, 1, blk), F32), pltpu.VMEM((hps, SLOT, blk), F32),
                        pltpu.VMEM((hps, blk, blk), F32), pltpu.VMEM((hps, blk, blk), F32)],
        compiler_params=pltpu.CompilerParams(
            dimension_semantics=("arbitrary", "arbitrary", "arbitrary"),
            vmem_limit_bytes=VMEM_LIMIT),
        name="fox_attn",
    )(fqk, fqk, fvt, gain)


def _diff_attn_kernel(q_ref, k_ref, vt_ref, lam_ref, g_ref, o_ref, m_ref, acc_ref,
                      sa_ref, sb_ref, *, blk, lam_init):
    i = pl.program_id(2)
    heads = range(ATTN_HEADS_PER_STEP)
    slot = lambda s: slice(s * SLOT, (s + 1) * SLOT)
    lane = lax.broadcasted_iota(jnp.int32, (blk, SLOT), 1)
    q = []
    for hh in heads:
        q_h = q_ref[:, slot(hh)]
        q.append([jnp.where((lane >= c * DA_DQK) & (lane < (c + 1) * DA_DQK), q_h,
                            jnp.zeros_like(q_h)) for c in range(2)])
    bufs = (sa_ref, sb_ref)
    m_ref[...] = jnp.full(m_ref.shape, NEG, F32)
    acc_ref[...] = jnp.zeros_like(acc_ref)

    def scores(j, buf):
        rows = pl.ds(_block_start(j, blk), blk)
        for hh in heads:
            k_h = k_ref[rows, slot(hh)]
            for c in range(2):
                bufs[buf][2 * hh + c] = lax.dot_general(k_h, q[hh][c], _NT,
                                                        preferred_element_type=F32)

    def consume(j, buf, masked):
        mask = _causal_valid(i, j, blk) if masked else None
        for hh in heads:
            ps, alphas = [], []
            for c in range(2):
                hc = 2 * hh + c
                s = bufs[buf][hc]
                if masked:
                    s = jnp.where(mask, s, NEG)
                p, alpha, m_new = _softmax_block(s, m_ref[hc])
                m_ref[hc] = m_new
                ps.append(p)
                alphas.append(alpha)
            pv = jnp.dot(vt_ref[0, hh, j], jnp.concatenate(ps, axis=1),
                         preferred_element_type=F32)
            for c in range(2):
                hc = 2 * hh + c
                acc_ref[hc] = alphas[c] * acc_ref[hc] + pv[:, c * blk:(c + 1) * blk]

    def first():
        lo = blk - FIRST_KEYS
        mask = _causal_valid(i, 0, blk, lo)
        for hh in heads:
            k_h = k_ref[lo:blk, slot(hh)]
            ps, alphas = [], []
            for c in range(2):
                hc = 2 * hh + c
                s = lax.dot_general(k_h, q[hh][c], _NT, preferred_element_type=F32)
                p, alpha, m_new = _softmax_block(jnp.where(mask, s, NEG), m_ref[hc])
                m_ref[hc] = m_new
                ps.append(p)
                alphas.append(alpha)
            pv = jnp.dot(vt_ref[0, hh, 0, :, lo:blk], jnp.concatenate(ps, axis=1),
                         preferred_element_type=F32)
            for c in range(2):
                hc = 2 * hh + c
                acc_ref[hc] = alphas[c] * acc_ref[hc] + pv[:, c * blk:(c + 1) * blk]

    _pipelined_key_blocks(i, first, scores, consume)
    lv = lam_ref[...]
    lam = (jnp.exp(jnp.sum(lv[0:1] * lv[1:2], axis=-1, keepdims=True))
           - jnp.exp(jnp.sum(lv[2:3] * lv[3:4], axis=-1, keepdims=True)) + lam_init)
    outs = []
    for hh in heads:
        o = _head_out(acc_ref[2 * hh]) - lam * _head_out(acc_ref[2 * hh + 1])
        outs.append(_head_rms(o, g_ref[...]) * (1.0 - lam_init))
    _store_head_pairs(o_ref, outs)


def _diff_attn(dqk, dvt, lam_vecs, gain, lam_init, batch, seq_len):
    n = dqk.shape[0]
    blk = ATTN_BLOCK
    nq = seq_len // blk
    hps = ATTN_HEADS_PER_STEP
    groups = DA_HEADS // hps
    return pl.pallas_call(
        functools.partial(_diff_attn_kernel, blk=blk, lam_init=lam_init),
        grid=(batch, groups, nq),
        in_specs=[pl.BlockSpec((blk, hps * SLOT), lambda b, h, i: (b * nq + i, h)),
                  _key_spec(seq_len, groups),
                  _value_spec(seq_len),
                  pl.BlockSpec((8, LANES), lambda b, h, i: (0, 0)),
                  pl.BlockSpec((ONE_LANE, 1), lambda b, h, i: (0, 0))],
        out_specs=pl.BlockSpec((blk, hps * ONE_LANE), lambda b, h, i: (b * nq + i, h)),
        out_shape=jax.ShapeDtypeStruct((n, 4 * ONE_LANE), BF16),
        scratch_shapes=[pltpu.VMEM((2 * hps, 1, blk), F32),
                        pltpu.VMEM((2 * hps, SLOT, blk), F32),
                        pltpu.VMEM((2 * hps, blk, blk), F32),
                        pltpu.VMEM((2 * hps, blk, blk), F32)],
        compiler_params=pltpu.CompilerParams(
            dimension_semantics=("arbitrary", "arbitrary", "arbitrary"),
            vmem_limit_bytes=VMEM_LIMIT),
        name="diff_attn",
    )(dqk, dqk, dvt, lam_vecs, gain)


def _hgrn_kernel(q_ref, i_ref, g_ref, lf_ref, kk_ref, gain_ref, o_ref,
                 st_ref, hs_ref, *, tl):
    @pl.when(pl.program_id(2) == 0)
    def _():
        st_ref[...] = jnp.zeros_like(st_ref)

    c_rows, sub = HG_CHUNK, HG_SUB
    tri = (lax.broadcasted_iota(jnp.int32, (c_rows, c_rows), 1)
           <= lax.broadcasted_iota(jnp.int32, (c_rows, c_rows), 0)).astype(BF16)
    t_sub = lax.broadcasted_iota(jnp.int32, (sub, 1), 0)
    lane_c = lax.broadcasted_iota(jnp.int32, (sub, c_rows), 1)

    t_all = lax.broadcasted_iota(jnp.int32, (c_rows, c_rows), 0)
    s_all = lax.broadcasted_iota(jnp.int32, (c_rows, c_rows), 1)
    level_masks = []
    g = sub
    while g < c_rows:
        level_masks.append(((t_all // g) == (s_all // g) + 1) & ((s_all // g) % 2 == 0))
        g *= 2

    heads = range(HG_PER_STEP)
    cols = [slice(hh * HG_DK, (hh + 1) * HG_DK) for hh in heads]

    def chunk(c, carry):
        r0 = pl.multiple_of(c * c_rows, c_rows)
        rows = pl.ds(r0, c_rows)
        G, kk, qs, v, st, o_inter = [], [], [], [], [], []
        for hh in heads:
            kk.append(kk_ref[rows, cols[hh]])
            G.append(sum(jnp.dot(tri, t.astype(BF16), preferred_element_type=F32)
                         for t in _bf16_split3(lf_ref[rows, cols[hh]])))
            qs.append(q_ref[rows, cols[hh]].astype(F32))
            v.append(i_ref[rows, cols[hh]])
            hs_ref[hh] = G[hh] - jnp.log2(kk[hh])
            st.append(st_ref[hh])
            o_inter.append(lax.dot_general(
                (qs[hh] * jnp.exp2(G[hh])).astype(BF16), st[hh].astype(BF16), _NT,
                preferred_element_type=F32))
        a_mat = [jnp.zeros((c_rows, c_rows), F32) for _ in heads]
        g = sub
        for mask in level_masks:
            for hh in heads:
                ref = jnp.concatenate(
                    [jnp.broadcast_to(G[hh][p + g - 1:p + g], (2 * g, HG_DK))
                     for p in range(0, c_rows, 2 * g)], axis=0)
                e = jnp.exp2(-jnp.abs(G[hh] - ref))
                part = lax.dot_general((qs[hh] * e).astype(BF16), (kk[hh] * e).astype(BF16),
                                       _NT, preferred_element_type=F32)
                a_mat[hh] = jnp.where(mask, part, a_mat[hh])
            g *= 2
        a_rows = [[] for _ in heads]
        for b in range(c_rows // sub):
            lo = b * sub
            for hh in heads:
                q_b = qs[hh][lo:lo + sub]
                g_b = G[hh][lo:lo + sub]
                a_blk = a_mat[hh][lo:lo + sub]
                for s in range(lo, lo + sub):
                    y = q_b * jnp.exp2(g_b - hs_ref[hh, s:s + 1, :])
                    a_blk = jnp.where(lane_c == s, jnp.sum(y, axis=-1, keepdims=True), a_blk)
                a_rows[hh].append(jnp.where(lane_c <= t_sub + lo, a_blk, 0.0))
        for hh in heads:
            a_full = jnp.concatenate(a_rows[hh], axis=0).astype(BF16)
            o = o_inter[hh] + jnp.dot(a_full, v[hh], preferred_element_type=F32)
            g_last = G[hh][c_rows - 1:c_rows]
            kd = kk[hh] * jnp.exp2(g_last - G[hh])
            st_ref[hh] = st[hh] * jnp.exp2(g_last) + lax.dot_general(
                v[hh], kd.astype(BF16), _TN, preferred_element_type=F32)
            o = _rms(o, gain_ref[...]) * g_ref[rows, cols[hh]].astype(F32)
            o_ref[rows, cols[hh]] = o.astype(BF16)
        return carry

    lax.fori_loop(0, tl // c_rows, chunk, 0, unroll=4)


def _hgrn(hqig, hlf, hkk, gain, batch, seq_len):
    n = hqig.shape[0]
    tl = ROW_TILE
    nt = seq_len // tl
    w = HG_PER_STEP * HG_DK
    groups = HG_HEADS // HG_PER_STEP
    col = lambda off: pl.BlockSpec((tl, w), lambda b, h, j: (b * nt + j, off + h))
    return pl.pallas_call(
        functools.partial(_hgrn_kernel, tl=tl),
        grid=(batch, groups, nt),
        in_specs=[col(0), col(groups), col(2 * groups), col(0), col(0),
                  pl.BlockSpec((1, HG_DK), lambda b, h, j: (0, 0))],
        out_specs=col(0),
        out_shape=jax.ShapeDtypeStruct((n, HG_WIDTH), BF16),
        scratch_shapes=[pltpu.VMEM((HG_PER_STEP, HG_DK, HG_DK), F32),
                        pltpu.VMEM((HG_PER_STEP, HG_CHUNK, HG_DK), F32)],
        compiler_params=pltpu.CompilerParams(
            dimension_semantics=("arbitrary", "arbitrary", "arbitrary")),
        name="hgrn2",
    )(hqig, hqig, hqig, hlf, hkk, gain)


def _out_router_kernel(oa_ref, ob_ref, oc_ref, h_ref, wo_ref, g_ref, wrh_ref, wrl_ref, br_ref,
                       h1_ref, route_ref, cnt_ref, carry_ref):
    @pl.when(pl.program_id(0) == 0)
    def _():
        carry_ref[...] = jnp.zeros_like(carry_ref)

    h1 = (h_ref[...]
          + jnp.dot(oa_ref[...], wo_ref[0:512, :], preferred_element_type=F32)
          + jnp.dot(ob_ref[...], wo_ref[512:768, :], preferred_element_type=F32)
          + jnp.dot(oc_ref[...], wo_ref[768:1024, :], preferred_element_type=F32))
    h1_ref[...] = h1
    u = _rms(h1, g_ref[...])
    u_hi = u.astype(BF16)
    u_lo = (u - u_hi.astype(F32)).astype(BF16)
    logits = (jnp.dot(u_hi, wrh_ref[...], preferred_element_type=F32)
              + (jnp.dot(u_hi, wrl_ref[...], preferred_element_type=F32)
                 + jnp.dot(u_lo, wrh_ref[...], preferred_element_type=F32))
              + br_ref[...])
    tm = logits.shape[0]
    lane = lax.broadcasted_iota(jnp.int32, (tm, LANES), 1).astype(F32)
    big = float(LANES)
    is_g = (lane >= N_EXPERTS) & (lane < N_EXPERTS + N_GROUPS)
    gl = jnp.where(is_g, logits, -jnp.inf)
    gmax = jnp.max(gl, axis=-1, keepdims=True)
    gsel = jnp.min(jnp.where(gl == gmax, lane, big), axis=-1, keepdims=True) - N_EXPERTS
    p_g = 1.0 / jnp.sum(jnp.exp(gl - gmax), axis=-1, keepdims=True)
    lo = gsel * EXPERTS_PER_GROUP
    el = jnp.where((lane >= lo) & (lane < lo + EXPERTS_PER_GROUP), logits, -jnp.inf)
    m1 = jnp.max(el, axis=-1, keepdims=True)
    i1 = jnp.min(jnp.where(el == m1, lane, big), axis=-1, keepdims=True)
    el2 = jnp.where(lane == i1, -jnp.inf, el)
    m2 = jnp.max(el2, axis=-1, keepdims=True)
    i2 = jnp.min(jnp.where(el2 == m2, lane, big), axis=-1, keepdims=True)
    r = jnp.exp(m2 - m1)
    gate1 = p_g / (1.0 + r)
    gate2 = gate1 * r
    oh1 = lane == i1
    oh2 = lane == i2
    onehot = jnp.where(oh1 | oh2, 1.0, 0.0)
    tri = (lax.broadcasted_iota(jnp.int32, (tm, tm), 1)
           < lax.broadcasted_iota(jnp.int32, (tm, tm), 0)).astype(BF16)
    before = jnp.dot(tri, onehot.astype(BF16), preferred_element_type=F32) + carry_ref[...]
    rank1 = jnp.sum(jnp.where(oh1, before, 0.0), axis=-1, keepdims=True)
    rank2 = jnp.sum(jnp.where(oh2, before, 0.0), axis=-1, keepdims=True)
    total = carry_ref[...] + jnp.sum(onehot, axis=0, keepdims=True)
    carry_ref[...] = total
    cnt_ref[...] = total
    route = jnp.where(lane == 0, i1, 0.0)
    for idx, val in ((1, i2), (2, gate1), (3, gate2), (4, rank1), (5, rank2)):
        route = jnp.where(lane == idx, val, route)
    route_ref[...] = route


def _out_router(oa, ob, oc, h, w_out, g, w_rt, b_rt):
    n = h.shape[0]
    w_rt_hi = w_rt.astype(BF16)
    w_rt_lo = (w_rt - w_rt_hi.astype(F32)).astype(BF16)
    tm = ROUTER_TILE
    row = lambda w: pl.BlockSpec((tm, w), lambda i: (i, 0))
    const = lambda r, c: pl.BlockSpec((r, c), lambda i: (0, 0))
    return pl.pallas_call(
        _out_router_kernel,
        grid=(n // tm,),
        in_specs=[row(HG_WIDTH), row(DA_WIDTH), row(FX_WIDTH), row(D_MODEL),
                  const(D_MODEL, D_MODEL),
                  const(1, D_MODEL), const(D_MODEL, LANES), const(D_MODEL, LANES),
                  const(1, LANES)],
        out_specs=[row(D_MODEL), row(LANES), const(1, LANES)],
        out_shape=[jax.ShapeDtypeStruct((n, D_MODEL), F32),
                   jax.ShapeDtypeStruct((n, LANES), F32),
                   jax.ShapeDtypeStruct((1, LANES), F32)],
        scratch_shapes=[pltpu.VMEM((1, LANES), F32)],
        compiler_params=pltpu.CompilerParams(dimension_semantics=("arbitrary",)),
        name="out_router",
    )(oa, ob, oc, h, w_out, g, w_rt_hi, w_rt_lo, b_rt)


def _dispatch_kernel(dest_ref, unused_ref, h_ref, g_ref, xs_in_ref, xs_ref, back_ref,
                     u_ref, sem, *, tm, n, tb):
    del xs_in_ref
    i = pl.program_id(0)
    n_tiles = n // tm
    base = (i - 1) * tm
    cur = i % 2

    @pl.when(i == 0)
    def _():
        def fill(s, carry):
            parity = lax.shift_right_logical(s, tb.bit_length() - 1) & 1
            back_ref[s] = 2 * n + parity * tb + (s & (tb - 1))
            return carry

        n_ranges = unused_ref.shape[0] // 2

        def one_range(e, carry):
            return lax.fori_loop(unused_ref[e], unused_ref[n_ranges + e], fill, carry)

        lax.fori_loop(0, n_ranges, one_range, 0)

    def issue(r, carry):
        for k in range(2):
            src = k * n + base + r
            d = dest_ref[src]
            back_ref[d] = src
            pltpu.make_async_copy(u_ref.at[1 - cur, pl.ds(r, 1)], xs_ref.at[pl.ds(d, 1)],
                                  sem).start(priority=k)
        return carry

    @pl.when(i >= 1)
    def _():
        lax.fori_loop(0, tm, issue, 0, unroll=8)

    @pl.when(i < n_tiles)
    def _():
        u_ref[cur] = _rms(h_ref[...], g_ref[...])

    @pl.when(i >= 1)
    def _():
        for _ in range(2):
            pltpu.make_async_copy(u_ref.at[1 - cur], xs_ref.at[pl.ds(0, tm)], sem).wait()


def _dispatch(dest, unused, h1, g, xs_init):
    n = h1.shape[0]
    tm = ROW_TILE
    p_rows = xs_init.shape[0]
    return pl.pallas_call(
        functools.partial(_dispatch_kernel, tm=tm, n=n, tb=MOE_BLOCK),
        grid_spec=pltpu.PrefetchScalarGridSpec(
            num_scalar_prefetch=2,
            grid=(n // tm + 1,),
            in_specs=[pl.BlockSpec((tm, D_MODEL),
                                   lambda i, d, un: (jnp.minimum(i, n // tm - 1), 0)),
                      pl.BlockSpec((1, D_MODEL), lambda i, d, un: (0, 0)),
                      pl.BlockSpec(memory_space=pl.ANY)],
            out_specs=[pl.BlockSpec(memory_space=pl.ANY),
                       pl.BlockSpec(memory_space=pltpu.SMEM)],
            scratch_shapes=[pltpu.VMEM((2, tm, D_MODEL), F32),
                            pltpu.SemaphoreType.DMA(())]),
        out_shape=[jax.ShapeDtypeStruct(xs_init.shape, xs_init.dtype),
                   jax.ShapeDtypeStruct((p_rows,), jnp.int32)],
        input_output_aliases={4: 0},
        compiler_params=pltpu.CompilerParams(dimension_semantics=("arbitrary",)),
        name="moe_dispatch",
    )(dest, unused, h1, g, xs_init)


def _expert_kernel(be_ref, na_ref, back_ref, x_ref, w1_ref, w3_ref, w2_ref, out_ref,
                   y_ref, w1b_ref, w3b_ref, w2b_ref, sem, *, tb):
    j = pl.program_id(0)
    na = na_ref[0]
    cur = j % 2

    def scatter(block, buf):
        for r in range(tb):
            pltpu.make_async_copy(y_ref.at[buf, pl.ds(r, 1)],
                                  out_ref.at[pl.ds(back_ref[block * tb + r], 1)],
                                  sem.at[buf]).start(priority=r % 2)

    def compute(buf):
        x = x_ref[...].astype(BF16)
        a = jnp.dot(x, w1b_ref[...], preferred_element_type=F32)
        b = jnp.dot(x, w3b_ref[...], preferred_element_type=F32)
        act = (a * _sigmoid(a) * b).astype(BF16)
        y_ref[buf] = jnp.dot(act, w2b_ref[...], preferred_element_type=F32)

    new_expert = (j == 0) | (be_ref[j] != be_ref[jnp.maximum(j - 1, 0)])

    @pl.when((j < na) & new_expert)
    def _():
        w1b_ref[...] = w1_ref[0].astype(BF16)
        w3b_ref[...] = w3_ref[0].astype(BF16)
        w2b_ref[...] = w2_ref[0].astype(BF16)

    @pl.when((j >= 2) & (j < na + 2))
    def _():
        pltpu.make_async_copy(y_ref.at[cur], out_ref.at[pl.ds(0, tb)], sem.at[cur]).wait()

    @pl.when(j == 0)
    def _():
        y_ref[...] = jnp.zeros_like(y_ref)
        first_spare = out_ref.shape[0] - 2 * tb
        spare = [pltpu.make_async_copy(
            y_ref.at[b], out_ref.at[pl.ds(first_spare + b * tb, tb)], sem.at[b])
            for b in range(2)]
        for copy in spare:
            copy.start()
        for copy in spare:
            copy.wait()
        compute(cur)

    @pl.when((j >= 1) & (j < na))
    def _():
        scatter(j - 1, 1 - cur)
        compute(cur)

    @pl.when(j == na)
    def _():
        scatter(j - 1, 1 - cur)


def _experts(blk_expert, n_active, slot_back, xs, w1, w3, w2, layer, n):
    p_rows = xs.shape[0]
    tb = MOE_BLOCK
    last = lambda j, na: jnp.maximum(jnp.minimum(j, na[0] - 1), 0)
    w_spec = lambda r, c: pl.BlockSpec((None, 1, r, c),
                                       lambda j, be, na, back: (layer, be[j], 0, 0))
    return pl.pallas_call(
        functools.partial(_expert_kernel, tb=tb),
        grid_spec=pltpu.PrefetchScalarGridSpec(
            num_scalar_prefetch=3,
            grid=(p_rows // tb + 1,),
            in_specs=[pl.BlockSpec((tb, D_MODEL), lambda j, be, na, back: (last(j, na), 0)),
                      w_spec(D_MODEL, D_EXPERT), w_spec(D_MODEL, D_EXPERT),
                      w_spec(D_EXPERT, D_MODEL)],
            out_specs=pl.BlockSpec(memory_space=pl.ANY),
            scratch_shapes=[pltpu.VMEM((2, tb, D_MODEL), F32),
                            pltpu.VMEM((D_MODEL, D_EXPERT), BF16),
                            pltpu.VMEM((D_MODEL, D_EXPERT), BF16),
                            pltpu.VMEM((D_EXPERT, D_MODEL), BF16),
                            pltpu.SemaphoreType.DMA((2,))]),
        out_shape=jax.ShapeDtypeStruct((2 * n + 2 * tb, D_MODEL), F32),
        compiler_params=pltpu.CompilerParams(dimension_semantics=("arbitrary",),
                                             vmem_limit_bytes=VMEM_LIMIT),
        name="moe_experts",
    )(blk_expert, n_active, slot_back, xs, w1, w3, w2)


def _combine_kernel(h_ref, route_ref, gfin_ref, y0_ref, y1_ref, o_ref, *, final):
    route = route_ref[...]
    out = h_ref[...] + route[:, 2:3] * y0_ref[...] + route[:, 3:4] * y1_ref[...]
    if final:
        out = _rms(out, gfin_ref[...])
    o_ref[...] = out


def _combine(h1, route, g_final, ys, batch, seq_len, final):
    n = h1.shape[0]
    tm = ROW_TILE
    tiles_per_seq = seq_len // tm
    skip_tiles = (LPAD + N_META) // tm if final else 0
    out_tiles = tiles_per_seq - skip_tiles
    tile = lambda b, i: b * tiles_per_seq + skip_tiles + i
    in_row = lambda w: pl.BlockSpec((tm, w), lambda b, i: (tile(b, i), 0))
    y_rows = lambda k: pl.BlockSpec((tm, D_MODEL),
                                    lambda b, i: (k * (n // tm) + tile(b, i), 0))
    return pl.pallas_call(
        functools.partial(_combine_kernel, final=final),
        grid=(batch, out_tiles),
        in_specs=[in_row(D_MODEL), in_row(LANES),
                  pl.BlockSpec((1, D_MODEL), lambda b, i: (0, 0)),
                  y_rows(0), y_rows(1)],
        out_specs=pl.BlockSpec((tm, D_MODEL), lambda b, i: (b * out_tiles + i, 0)),
        out_shape=jax.ShapeDtypeStruct((batch * out_tiles * tm, D_MODEL), F32),
        compiler_params=pltpu.CompilerParams(dimension_semantics=("arbitrary", "arbitrary")),
        name="moe_combine",
    )(h1, route, g_final, ys, ys)


def _rope_partner(m):
    lead = m.shape[:-1]
    x = m.reshape(lead + (DA_WIDTH // DA_DQK, 2, DA_DQK // 2))
    return x[..., ::-1, :].reshape(lead + (DA_WIDTH,))


def _rope_tables(seq_len):
    pos = (jnp.arange(seq_len) - LPAD).astype(F32)
    inv = ROPE_THETA ** (-jnp.arange(0, DA_DQK, 2, dtype=F32) / DA_DQK)
    ang = pos[:, None] * inv[None, :]
    comps = DA_DV // DA_DQK
    cos = jnp.tile(jnp.cos(ang), (1, 2 * comps))
    sin = jnp.tile(jnp.concatenate([-jnp.sin(ang), jnp.sin(ang)], axis=1), (1, comps))
    q_scale = DA_DQK ** -0.5 * LOG2E
    two = lambda t: jnp.tile(t, (1, LANES // DA_DV))
    lay = lambda t: jnp.concatenate([two(t * q_scale), two(t)], axis=1)
    return lay(cos), lay(sin)


def _in_weights(w):
    hq, hf, hi, hg = (w[:, i * 512:(i + 1) * 512] for i in range(4))
    dq, dk, dv = (w[:, 2048 + i * 256:2048 + (i + 1) * 256] for i in range(3))
    fq, fk, fv = (w[:, 2816 + i * 256:2816 + (i + 1) * 256] for i in range(3))
    ff = w[:, 3584:3588]
    cat = jnp.concatenate(
        [hq, hi, hg, hf,
         dq, dk, _rope_partner(dq), _rope_partner(dk),
         fq * (FX_DH ** -0.5), fk,
         ff, jnp.zeros((D_MODEL, LANES - FX_HEADS), w.dtype)],
        axis=1)
    w_vt = jnp.concatenate([dv, fv], axis=1).T
    return cat.astype(BF16), w_vt.astype(BF16)


def _pad_lanes(v, width=LANES):
    return jnp.zeros((1, width), F32).at[0, :v.shape[0]].set(v.astype(F32))


def kernel(x, meta_tokens, norm_mix, w_in, hgrn_lb, hgrn_norm, diff_lambda, diff_norm,
           fox_bias, fox_norm, w_out, norm_ffn, w_group, b_group, w_router, b_router,
           w1, w3, w2, norm_final):
    batch, seq, d = x.shape
    depth = w_in.shape[0]
    seq_len = LPAD + N_META + seq
    n = batch * seq_len
    pad = jnp.zeros((batch, LPAD, d), x.dtype)
    meta = jnp.broadcast_to(meta_tokens.astype(x.dtype)[None], (batch, N_META, d))
    h = jnp.concatenate([pad, meta, x], axis=1).reshape(n, d)

    s_lb = jax.nn.softmax(hgrn_lb.astype(F32), axis=0)
    lb_all = jnp.cumsum(s_lb, axis=0) - s_lb[0]
    cos_t, sin_t = _rope_tables(seq_len)

    tb = MOE_BLOCK
    n_blocks = (2 * n) // tb + N_EXPERTS
    p_rows = n_blocks * tb

    xs = jnp.zeros((p_rows, d), F32)
    for layer in range(depth):
        lam_init = 0.8 - 0.6 * math.exp(-0.3 * layer)
        w_cat, w_vt = _in_weights(w_in[layer])
        hqig, hlf, hkk, dqk, fqk, dvt, fvt = _in_proj(
            h, norm_mix[layer][None, :], w_cat, w_vt, cos_t, sin_t,
            _pad_lanes(fox_bias[layer]), lb_all[layer][None, :], batch, seq_len)

        o_a = _hgrn(hqig, hlf, hkk, hgrn_norm[layer][None, :], batch, seq_len)

        lam_vecs = jnp.zeros((8, LANES), F32).at[:4, :DA_DQK].set(diff_lambda[layer].astype(F32))
        o_b = _diff_attn(dqk, dvt, lam_vecs, diff_norm[layer][:, None], lam_init,
                         batch, seq_len)

        o_c = _fox_attn(fqk, fvt, fox_norm[layer][:, None], batch, seq_len)

        w_rt = jnp.concatenate(
            [w_router[layer], w_group[layer],
             jnp.zeros((d, LANES - N_EXPERTS - N_GROUPS), F32)], axis=1)
        b_rt = _pad_lanes(jnp.concatenate([b_router[layer], b_group[layer]]))
        h1, route, counts = _out_router(
            o_a, o_b, o_c, h, w_out[layer].astype(BF16), norm_ffn[layer][None, :], w_rt, b_rt)

        cnt = counts[0, :N_EXPERTS].astype(jnp.int32)
        padded = (cnt + tb - 1) // tb * tb
        p_end = jnp.cumsum(padded)
        p_start = p_end - padded
        ids = route[:, 0:2].astype(jnp.int32)
        ranks = route[:, 4:6].astype(jnp.int32)
        dest = (p_start[ids] + ranks).T.reshape(2 * n)
        blk_start = jnp.arange(n_blocks + 1, dtype=jnp.int32) * tb
        blk_expert = jnp.minimum(
            jnp.sum((p_end[None, :] <= blk_start[:, None]).astype(jnp.int32), axis=1),
            N_EXPERTS - 1)
        n_active = (p_end[-1:] // tb).astype(jnp.int32)

        unused = jnp.concatenate(
            [p_start + cnt, p_end[-1:], p_end, jnp.full((1,), p_rows)]).astype(jnp.int32)
        xs, slot_back = _dispatch(dest, unused, h1, norm_ffn[layer][None, :], xs)
        ys = _experts(blk_expert, n_active, slot_back, xs, w1, w3, w2, layer, n)
        h = _combine(h1, route, norm_final[None, :], ys, batch, seq_len,
                     final=(layer == depth - 1))

    return h.reshape(batch, seq, d)
```
